```python
import math
import jax, jax.numpy as jnp
from jax import lax
import numpy as np

D_MODEL = 1024
BATCH = 8
SEQ = 8192
DEPTH = 2

N_MIXERS = 2
N_A_LAYERS = (DEPTH + 1) // 2
N_B_LAYERS = DEPTH // 2

CHUNK = 128
GATE_WIDTH = D_MODEL
GATE_GROUPS = 8
GATE_GROUP_DIM = GATE_WIDTH // GATE_GROUPS

WINDOW_DILATIONS = ((128, 1), (512, 4), (2048, 16))
N_DIL_GROUPS = len(WINDOW_DILATIONS)
ATT_HEADS = 8
HEAD_DIM = 64
ATT_WIDTH = ATT_HEADS * HEAD_DIM

N_BUCKETS = 32
MAX_EXACT = N_BUCKETS // 2
REL_MAX_DISTANCE = max(w for w, _ in WINDOW_DILATIONS)

D_FF = 4 * D_MODEL

EPS = 1e-6
NEG_INF = -1e30

kernel_name = "interleaved_gmlp_dilated_attention_trunk"


def _rms_norm(x, g):
    xf = x.astype(jnp.float32)
    y = xf * lax.rsqrt(jnp.mean(xf * xf, axis=-1, keepdims=True) + EPS)
    return (y * g.astype(jnp.float32)).astype(x.dtype)


def _layer_norm(x, g, b):
    xf = x.astype(jnp.float32)
    mu = jnp.mean(xf, axis=-1, keepdims=True)
    xc = xf - mu
    y = xc * lax.rsqrt(jnp.mean(xc * xc, axis=-1, keepdims=True) + EPS)
    return (y * g.astype(jnp.float32) + b.astype(jnp.float32)).astype(x.dtype)


def _t5_bucket(distance):
    small = distance < MAX_EXACT
    nf = jnp.maximum(distance, 1).astype(jnp.float32)
    large = MAX_EXACT + (jnp.log(nf / MAX_EXACT) / math.log(REL_MAX_DISTANCE / MAX_EXACT)
                         * (N_BUCKETS - MAX_EXACT)).astype(jnp.int32)
    large = jnp.minimum(large, N_BUCKETS - 1)
    return jnp.where(small, distance, large)


def _chunk_gating_mixer(h, w_in, ln_g, ln_b, w_s, b_s, w_out):
    B_, S_, _ = h.shape
    uv = jax.nn.gelu(h @ w_in, approximate=False)
    u, v = jnp.split(uv, 2, axis=-1)
    v = _layer_norm(v, ln_g, ln_b)
    nc = S_ // CHUNK
    v = v.reshape(B_, nc, CHUNK, GATE_GROUPS, GATE_GROUP_DIM)
    causal = jnp.tril(jnp.ones((CHUNK, CHUNK), dtype=bool))
    w = jnp.where(causal[None], w_s, 0.0)
    mixed = jnp.einsum('gts,bnsgc->bntgc', w, v) + b_s.T[None, None, :, :, None]
    gate = mixed.reshape(B_, S_, GATE_WIDTH)
    return (u * gate) @ w_out


def _dilated_group(q, k, v, bias_table, window, dilation):
    B_, S_, H, hd = q.shape
    blk = window // dilation
    span = blk * dilation
    Sp = -(-S_ // span) * span
    nb = Sp // span

    def split(t):
        t = jnp.pad(t, ((0, 0), (0, Sp - S_), (0, 0), (0, 0)))
        return t.reshape(B_, nb, blk, dilation, H, hd)

    def with_prev(t):
        prev = jnp.pad(t, ((0, 0), (1, 0), (0, 0), (0, 0), (0, 0), (0, 0)))[:, :-1]
        return jnp.concatenate([prev, t], axis=2)

    qb = split(q)
    kc = with_prev(split(k))
    vc = with_prev(split(v))

    s = jnp.einsum('bnqrhc,bnkrhc->bnrhqk', qb, kc) * (HEAD_DIM ** -0.5)
    rel = blk + jnp.arange(blk)[:, None] - jnp.arange(2 * blk)[None, :]
    band = (rel >= 0) & (rel <= blk)
    bucket = _t5_bucket(jnp.clip(rel, 0, blk) * dilation)
    bias = jnp.transpose(bias_table[bucket], (2, 0, 1))
    first = (jnp.arange(nb)[:, None, None] == 0) & (jnp.arange(2 * blk)[None, None, :] < blk)
    valid = band[None] & ~first
    logits = jnp.where(valid[None, :, None, None], s + bias[None, None, None], NEG_INF)

    m = jnp.max(logits, axis=-1)
    p = jnp.exp(logits - m[..., None])
    den = jnp.sum(p, axis=-1)
    num = jnp.einsum('bnrhqk,bnkrhc->bnqrhc', p, vc)

    num = num.reshape(B_, Sp, H, hd)[:, :S_]
    den = jnp.transpose(den, (0, 1, 4, 2, 3)).reshape(B_, Sp, H)[:, :S_]
    m = jnp.transpose(m, (0, 1, 4, 2, 3)).reshape(B_, Sp, H)[:, :S_]
    return num, den, m


def _dilated_attention_mixer(h, w_qkv, w_out, rel_bias):
    B_, S_, _ = h.shape
    qkv = (h @ w_qkv).astype(jnp.float32).reshape(B_, S_, 3, N_DIL_GROUPS, ATT_HEADS, HEAD_DIM)
    rb = rel_bias.astype(jnp.float32)
    nums, dens, maxs = [], [], []
    for g, (window, dil) in enumerate(WINDOW_DILATIONS):
        n_, d_, m_ = _dilated_group(qkv[:, :, 0, g], qkv[:, :, 1, g], qkv[:, :, 2, g],
                                    rb[:, g * ATT_HEADS:(g + 1) * ATT_HEADS], window, dil)
        nums.append(n_)
        dens.append(d_)
        maxs.append(m_)
    m_all = jnp.max(jnp.stack(maxs), axis=0)
    scales = [jnp.exp(m_ - m_all) for m_ in maxs]
    num_sum = sum(n_ * c[..., None] for n_, c in zip(nums, scales))
    den_sum = sum(d_ * c for d_, c in zip(dens, scales))
    o = (num_sum / den_sum[..., None]).astype(h.dtype).reshape(B_, S_, ATT_WIDTH)
    return o @ w_out


def _fwd_setup_inputs(seed: int = 0) -> dict:
    key = jax.random.key(seed)
    ks = jax.random.split(key, 16)
    f32 = jnp.float32
    nrm = lambda k, shape, s: jax.random.normal(k, shape, f32) * s
    qkv_cols = 3 * N_DIL_GROUPS * ATT_HEADS * HEAD_DIM
    return {
        "x": jax.random.normal(ks[0], (BATCH, SEQ, D_MODEL), f32),
        "mix_norm_g": 1.0 + nrm(ks[1], (DEPTH, D_MODEL), 0.05),
        "mlp_norm_g": 1.0 + nrm(ks[2], (DEPTH, D_MODEL), 0.05),
        "final_norm_g": 1.0 + nrm(ks[3], (D_MODEL,), 0.05),
        "a_w_in": nrm(ks[4], (N_A_LAYERS, D_MODEL, 2 * GATE_WIDTH), D_MODEL ** -0.5),
        "a_ln_g": 1.0 + nrm(ks[5], (N_A_LAYERS, GATE_WIDTH), 0.05),
        "a_ln_b": nrm(ks[6], (N_A_LAYERS, GATE_WIDTH), 0.02),
        "a_w_s": nrm(ks[7], (N_A_LAYERS, GATE_GROUPS, CHUNK, CHUNK), CHUNK ** -0.5),
        "a_b_s": 1.0 + nrm(ks[8], (N_A_LAYERS, GATE_GROUPS, CHUNK), 0.1),
        "a_w_out": nrm(ks[9], (N_A_LAYERS, GATE_WIDTH, D_MODEL), GATE_WIDTH ** -0.5),
        "b_w_qkv": nrm(ks[10], (N_B_LAYERS, D_MODEL, qkv_cols), D_MODEL ** -0.5),
        "b_w_out": nrm(ks[11], (N_B_LAYERS, ATT_WIDTH, D_MODEL), ATT_WIDTH ** -0.5),
        "rel_bias": nrm(ks[12], (N_BUCKETS, N_DIL_GROUPS * ATT_HEADS), 0.5),
        "w_up": nrm(ks[13], (DEPTH, D_MODEL, D_FF), D_MODEL ** -0.5),
        "w_down": nrm(ks[14], (DEPTH, D_FF, D_MODEL), D_FF ** -0.5),
    }


def _fwd_reference(x, mix_norm_g, mlp_norm_g, final_norm_g, a_w_in, a_ln_g, a_ln_b, a_w_s, a_b_s,
              a_w_out, b_w_qkv, b_w_out, rel_bias, w_up, w_down):
    h = x
    for layer in range(DEPTH):
        y = _rms_norm(h, mix_norm_g[layer])
        j = layer // N_MIXERS
        if layer % N_MIXERS == 0:
            y = _chunk_gating_mixer(y, a_w_in[j], a_ln_g[j], a_ln_b[j], a_w_s[j], a_b_s[j], a_w_out[j])
        else:
            y = _dilated_attention_mixer(y, b_w_qkv[j], b_w_out[j], rel_bias)
        h = h + y
        y = _rms_norm(h, mlp_norm_g[layer])
        h = h + jnp.square(jax.nn.relu(y @ w_up[layer])) @ w_down[layer]
    return _rms_norm(h, final_norm_g)


import jax as _jax
import jax.numpy as _jnp

TWIN_FORMAT = 'train_step'
FWD_PARAMS = ['x', 'mix_norm_g', 'mlp_norm_g', 'final_norm_g', 'a_w_in', 'a_ln_g', 'a_ln_b', 'a_w_s', 'a_b_s', 'a_w_out', 'b_w_qkv', 'b_w_out', 'rel_bias', 'w_up', 'w_down']
TWIN_WEIGHTS = ['mix_norm_g', 'mlp_norm_g', 'final_norm_g', 'a_w_in', 'a_ln_g', 'a_ln_b', 'a_w_s', 'a_b_s', 'a_w_out', 'b_w_qkv', 'b_w_out', 'rel_bias', 'w_up', 'w_down']
TWIN_DIFF_INPUT = 'x'
TWIN_INPUTS = ['x', 'mix_norm_g', 'mlp_norm_g', 'final_norm_g', 'a_w_in', 'a_ln_g', 'a_ln_b', 'a_w_s', 'a_b_s', 'a_w_out', 'b_w_qkv', 'b_w_out', 'rel_bias', 'w_up', 'w_down', 'loss_target', 'm_mix_norm_g', 'm_mlp_norm_g', 'm_final_norm_g', 'm_a_w_in', 'm_a_ln_g', 'm_a_ln_b', 'm_a_w_s', 'm_a_b_s', 'm_a_w_out', 'm_b_w_qkv', 'm_b_w_out', 'm_rel_bias', 'm_w_up', 'm_w_down', 'v_mix_norm_g', 'v_mlp_norm_g', 'v_final_norm_g', 'v_a_w_in', 'v_a_ln_g', 'v_a_ln_b', 'v_a_w_s', 'v_a_b_s', 'v_a_w_out', 'v_b_w_qkv', 'v_b_w_out', 'v_rel_bias', 'v_w_up', 'v_w_down']
TWIN_OUTPUTS = ['loss', 'grad_x', 'grad_mix_norm_g', 'grad_mlp_norm_g', 'grad_final_norm_g', 'grad_a_w_in', 'grad_a_ln_g', 'grad_a_ln_b', 'grad_a_w_s', 'grad_a_b_s', 'grad_a_w_out', 'grad_b_w_qkv', 'grad_b_w_out', 'grad_rel_bias', 'grad_w_up', 'grad_w_down', 'delta_mix_norm_g', 'delta_mlp_norm_g', 'delta_final_norm_g', 'delta_a_w_in', 'delta_a_ln_g', 'delta_a_ln_b', 'delta_a_w_s', 'delta_a_b_s', 'delta_a_w_out', 'delta_b_w_qkv', 'delta_b_w_out', 'delta_rel_bias', 'delta_w_up', 'delta_w_down', 'new_m_mix_norm_g', 'new_m_mlp_norm_g', 'new_m_final_norm_g', 'new_m_a_w_in', 'new_m_a_ln_g', 'new_m_a_ln_b', 'new_m_a_w_s', 'new_m_a_b_s', 'new_m_a_w_out', 'new_m_b_w_qkv', 'new_m_b_w_out', 'new_m_rel_bias', 'new_m_w_up', 'new_m_w_down', 'new_v_mix_norm_g', 'new_v_mlp_norm_g', 'new_v_final_norm_g', 'new_v_a_w_in', 'new_v_a_ln_g', 'new_v_a_ln_b', 'new_v_a_w_s', 'new_v_a_b_s', 'new_v_a_w_out', 'new_v_b_w_qkv', 'new_v_b_w_out', 'new_v_rel_bias', 'new_v_w_up', 'new_v_w_down']
TWIN_LEAF_KINDS = {'loss': 'loss', 'grad_x': 'grad_x', 'grad_mix_norm_g': 'grad_w', 'grad_mlp_norm_g': 'grad_w', 'grad_final_norm_g': 'grad_w', 'grad_a_w_in': 'grad_w', 'grad_a_ln_g': 'grad_w', 'grad_a_ln_b': 'grad_w', 'grad_a_w_s': 'grad_w', 'grad_a_b_s': 'grad_w', 'grad_a_w_out': 'grad_w', 'grad_b_w_qkv': 'grad_w', 'grad_b_w_out': 'grad_w', 'grad_rel_bias': 'grad_w', 'grad_w_up': 'grad_w', 'grad_w_down': 'grad_w', 'delta_mix_norm_g': 'delta_w', 'delta_mlp_norm_g': 'delta_w', 'delta_final_norm_g': 'delta_w', 'delta_a_w_in': 'delta_w', 'delta_a_ln_g': 'delta_w', 'delta_a_ln_b': 'delta_w', 'delta_a_w_s': 'delta_w', 'delta_a_b_s': 'delta_w', 'delta_a_w_out': 'delta_w', 'delta_b_w_qkv': 'delta_w', 'delta_b_w_out': 'delta_w', 'delta_rel_bias': 'delta_w', 'delta_w_up': 'delta_w', 'delta_w_down': 'delta_w', 'new_m_mix_norm_g': 'new_m', 'new_m_mlp_norm_g': 'new_m', 'new_m_final_norm_g': 'new_m', 'new_m_a_w_in': 'new_m', 'new_m_a_ln_g': 'new_m', 'new_m_a_ln_b': 'new_m', 'new_m_a_w_s': 'new_m', 'new_m_a_b_s': 'new_m', 'new_m_a_w_out': 'new_m', 'new_m_b_w_qkv': 'new_m', 'new_m_b_w_out': 'new_m', 'new_m_rel_bias': 'new_m', 'new_m_w_up': 'new_m', 'new_m_w_down': 'new_m', 'new_v_mix_norm_g': 'new_v', 'new_v_mlp_norm_g': 'new_v', 'new_v_final_norm_g': 'new_v', 'new_v_a_w_in': 'new_v', 'new_v_a_ln_g': 'new_v', 'new_v_a_ln_b': 'new_v', 'new_v_a_w_s': 'new_v', 'new_v_a_b_s': 'new_v', 'new_v_a_w_out': 'new_v', 'new_v_b_w_qkv': 'new_v', 'new_v_b_w_out': 'new_v', 'new_v_rel_bias': 'new_v', 'new_v_w_up': 'new_v', 'new_v_w_down': 'new_v'}


def _forward(args):
    return _fwd_reference(*[args[k] for k in FWD_PARAMS])


def _output_shape():
    def fwd():
        inp = _fwd_setup_inputs(0)
        return _fwd_reference(*[inp[k] for k in FWD_PARAMS])
    out = _jax.eval_shape(fwd)
    return out.shape, out.dtype

N_MICROBATCH = 1
ADAM_LR = 0.001
ADAM_B1 = 0.9
ADAM_B2 = 0.999
ADAM_EPS = 1e-08
ADAM_WD = 0.01
ADAM_STEP = 10
PER_EXAMPLE_BATCH_AXIS = {'x': 0, 'loss_target': 0}
SHARED_INPUTS = []
_WEIGHT_DTYPES = {'mix_norm_g': _jnp.float32, 'mlp_norm_g': _jnp.float32, 'final_norm_g': _jnp.float32, 'a_w_in': _jnp.float32, 'a_ln_g': _jnp.float32, 'a_ln_b': _jnp.float32, 'a_w_s': _jnp.float32, 'a_b_s': _jnp.float32, 'a_w_out': _jnp.float32, 'b_w_qkv': _jnp.float32, 'b_w_out': _jnp.float32, 'rel_bias': _jnp.float32, 'w_up': _jnp.float32, 'w_down': _jnp.float32}
MOMENT_SCALE = {'mix_norm_g': 1.714717e-01, 'mlp_norm_g': 2.533815e-01, 'final_norm_g': 6.530859e+01, 'a_w_in': 1.560415e-01, 'a_ln_g': 1.025462e-01, 'a_ln_b': 1.096715e-01, 'a_w_s': 1.015979e-01, 'a_b_s': 1.538364e-01, 'a_w_out': 3.113728e-01, 'b_w_qkv': 7.566886e-02, 'b_w_out': 1.746041e-01, 'rel_bias': 4.060330e-02, 'w_up': 1.292449e-01, 'w_down': 4.898856e-01}


def _to_microbatches(a, axis):
    t = _jnp.moveaxis(a, axis, 0)
    t = t.reshape((N_MICROBATCH, t.shape[0] // N_MICROBATCH) + t.shape[1:])
    return _jnp.moveaxis(t, 1, axis + 1)


def setup_inputs(seed: int = 0) -> dict:
    inp = _fwd_setup_inputs(seed)
    key = _jax.random.fold_in(_jax.random.key(seed), 7919)
    shape, _ = _output_shape()
    out = dict(inp)
    out["loss_target"] = _jax.random.normal(_jax.random.fold_in(key, 0), shape, _jnp.float32)
    for i, name in enumerate(TWIN_WEIGHTS):
        w = inp[name].astype(_jnp.float32)
        if MOMENT_SCALE is None:
            s = _jnp.sqrt(_jnp.mean(_jnp.square(w)) + 1e-30)
        else:
            s = MOMENT_SCALE[name]
        km, kv = _jax.random.split(_jax.random.fold_in(key, i + 1))
        out[name] = w
        out["m_" + name] = s * _jax.random.normal(km, w.shape, _jnp.float32)
        out["v_" + name] = (s * s) * _jax.random.uniform(kv, w.shape, _jnp.float32, 0.5, 1.5)
    if N_MICROBATCH > 1:
        for name, axis in PER_EXAMPLE_BATCH_AXIS.items():
            out[name] = _to_microbatches(out[name], axis)
    return {'x': out['x'], 'mix_norm_g': out['mix_norm_g'], 'mlp_norm_g': out['mlp_norm_g'], 'final_norm_g': out['final_norm_g'], 'a_w_in': out['a_w_in'], 'a_ln_g': out['a_ln_g'], 'a_ln_b': out['a_ln_b'], 'a_w_s': out['a_w_s'], 'a_b_s': out['a_b_s'], 'a_w_out': out['a_w_out'], 'b_w_qkv': out['b_w_qkv'], 'b_w_out': out['b_w_out'], 'rel_bias': out['rel_bias'], 'w_up': out['w_up'], 'w_down': out['w_down'], 'loss_target': out['loss_target'], 'm_mix_norm_g': out['m_mix_norm_g'], 'm_mlp_norm_g': out['m_mlp_norm_g'], 'm_final_norm_g': out['m_final_norm_g'], 'm_a_w_in': out['m_a_w_in'], 'm_a_ln_g': out['m_a_ln_g'], 'm_a_ln_b': out['m_a_ln_b'], 'm_a_w_s': out['m_a_w_s'], 'm_a_b_s': out['m_a_b_s'], 'm_a_w_out': out['m_a_w_out'], 'm_b_w_qkv': out['m_b_w_qkv'], 'm_b_w_out': out['m_b_w_out'], 'm_rel_bias': out['m_rel_bias'], 'm_w_up': out['m_w_up'], 'm_w_down': out['m_w_down'], 'v_mix_norm_g': out['v_mix_norm_g'], 'v_mlp_norm_g': out['v_mlp_norm_g'], 'v_final_norm_g': out['v_final_norm_g'], 'v_a_w_in': out['v_a_w_in'], 'v_a_ln_g': out['v_a_ln_g'], 'v_a_ln_b': out['v_a_ln_b'], 'v_a_w_s': out['v_a_w_s'], 'v_a_b_s': out['v_a_b_s'], 'v_a_w_out': out['v_a_w_out'], 'v_b_w_qkv': out['v_b_w_qkv'], 'v_b_w_out': out['v_b_w_out'], 'v_rel_bias': out['v_rel_bias'], 'v_w_up': out['v_w_up'], 'v_w_down': out['v_w_down']}


def _loss(weights, diff, rest, loss_target):
    with _jax.named_scope("forward"):
        args = {**rest, TWIN_DIFF_INPUT: diff, **{k: w.astype(_WEIGHT_DTYPES[k]) for k, w in weights.items()}}
        y = _forward(args)
    with _jax.named_scope("loss_head"):
        err = _jnp.square(y.astype(_jnp.float32) - loss_target)
        return 0.5 * _jnp.sum(_jnp.mean(err, axis=-1)) if err.ndim else 0.5 * err


def _adamw(w, g, m, v):
    m = ADAM_B1 * m + (1.0 - ADAM_B1) * g
    v = ADAM_B2 * v + (1.0 - ADAM_B2) * _jnp.square(g)
    m_hat = m / (1.0 - ADAM_B1 ** ADAM_STEP)
    v_hat = v / (1.0 - ADAM_B2 ** ADAM_STEP)
    delta = -ADAM_LR * (m_hat / (_jnp.sqrt(v_hat) + ADAM_EPS) + ADAM_WD * w)
    return delta, m, v


def reference(x, mix_norm_g, mlp_norm_g, final_norm_g, a_w_in, a_ln_g, a_ln_b, a_w_s, a_b_s, a_w_out, b_w_qkv, b_w_out, rel_bias, w_up, w_down, loss_target, m_mix_norm_g, m_mlp_norm_g, m_final_norm_g, m_a_w_in, m_a_ln_g, m_a_ln_b, m_a_w_s, m_a_b_s, m_a_w_out, m_b_w_qkv, m_b_w_out, m_rel_bias, m_w_up, m_w_down, v_mix_norm_g, v_mlp_norm_g, v_final_norm_g, v_a_w_in, v_a_ln_g, v_a_ln_b, v_a_w_s, v_a_b_s, v_a_w_out, v_b_w_qkv, v_b_w_out, v_rel_bias, v_w_up, v_w_down):
    given = dict(x=x, mix_norm_g=mix_norm_g, mlp_norm_g=mlp_norm_g, final_norm_g=final_norm_g, a_w_in=a_w_in, a_ln_g=a_ln_g, a_ln_b=a_ln_b, a_w_s=a_w_s, a_b_s=a_b_s, a_w_out=a_w_out, b_w_qkv=b_w_qkv, b_w_out=b_w_out, rel_bias=rel_bias, w_up=w_up, w_down=w_down, loss_target=loss_target, m_mix_norm_g=m_mix_norm_g, m_mlp_norm_g=m_mlp_norm_g, m_final_norm_g=m_final_norm_g, m_a_w_in=m_a_w_in, m_a_ln_g=m_a_ln_g, m_a_ln_b=m_a_ln_b, m_a_w_s=m_a_w_s, m_a_b_s=m_a_b_s, m_a_w_out=m_a_w_out, m_b_w_qkv=m_b_w_qkv, m_b_w_out=m_b_w_out, m_rel_bias=m_rel_bias, m_w_up=m_w_up, m_w_down=m_w_down, v_mix_norm_g=v_mix_norm_g, v_mlp_norm_g=v_mlp_norm_g, v_final_norm_g=v_final_norm_g, v_a_w_in=v_a_w_in, v_a_ln_g=v_a_ln_g, v_a_ln_b=v_a_ln_b, v_a_w_s=v_a_w_s, v_a_b_s=v_a_b_s, v_a_w_out=v_a_w_out, v_b_w_qkv=v_b_w_qkv, v_b_w_out=v_b_w_out, v_rel_bias=v_rel_bias, v_w_up=v_w_up, v_w_down=v_w_down)
    weights = {n: given[n] for n in TWIN_WEIGHTS}
    shared = {n: given[n] for n in SHARED_INPUTS}
    per_example = {n: given[n] for n in ['x']}
    grad_fn = _jax.value_and_grad(_loss, argnums=(0, 1))

    def one_microbatch(ex, loss_target):
        ex = dict(ex)
        diff = ex.pop(TWIN_DIFF_INPUT)
        return grad_fn(weights, diff, {**shared, **ex}, loss_target)

    if N_MICROBATCH == 1:
        loss, (grad_w, grad_x) = one_microbatch(per_example, given["loss_target"])
    else:
        def body(carry, xs):
            loss_sum, grad_sum = carry
            l_k, (gw_k, gx_k) = one_microbatch(xs[0], xs[1])
            with _jax.named_scope("update"):
                return (loss_sum + l_k, _jax.tree.map(_jnp.add, grad_sum, gw_k)), gx_k

        init = (_jnp.zeros((), _jnp.float32), _jax.tree.map(_jnp.zeros_like, weights))
        (loss, grad_w), grad_x = _jax.lax.scan(body, init, (per_example, given["loss_target"]))
    with _jax.named_scope("update"):
        delta_w, new_m, new_v = {}, {}, {}
        for n in TWIN_WEIGHTS:
            delta_w[n], new_m[n], new_v[n] = _adamw(weights[n], grad_w[n], given["m_" + n], given["v_" + n])
    return (loss, grad_x, *[grad_w[n] for n in TWIN_WEIGHTS], *[delta_w[n] for n in TWIN_WEIGHTS],
            *[new_m[n] for n in TWIN_WEIGHTS], *[new_v[n] for n in TWIN_WEIGHTS])
```

```python
import functools
import math

import jax
import jax.numpy as jnp
from jax import lax
from jax.experimental import pallas as pl
from jax.experimental.pallas import tpu as pltpu

F32 = jnp.float32
BF16 = jnp.bfloat16
MESH = pl.DeviceIdType.MESH

N_DEV = 8
EPS = 1e-6
NEG_INF = -1e30
CHUNK = 128
GROUPS = 8
HEAD_DIM = 64
ATT_HEADS = 8
ATT_WIDTH = ATT_HEADS * HEAD_DIM
DILATIONS = (1, 4, 16)
N_DIL = len(DILATIONS)
N_BUCKETS = 32
MAX_EXACT = N_BUCKETS // 2
REL_MAX_DISTANCE = 2048
ATT_ROWS = 2048
LANES = 128
PACK_W = 1024

ADAM_LR = 0.001
ADAM_B1 = 0.9
ADAM_B2 = 0.999
ADAM_EPS = 1e-08
ADAM_WD = 0.01
ADAM_STEP = 10

VMEM_LIMIT_BYTES = 56 * 1024 * 1024


def _params(semantics=None):
    return pltpu.CompilerParams(dimension_semantics=semantics, vmem_limit_bytes=VMEM_LIMIT_BYTES)


def _bf(v):
    return v.astype(BF16)


def _dot(a, b, dims):
    return lax.dot_general(a, b, (dims, ((), ())), preferred_element_type=F32)


NN = ((1,), (0,))
NT = ((1,), (1,))
TN = ((0,), (0,))


def _rms_fwd(name, x, g, tm=512):
    S, D = x.shape

    def body(x_ref, g_ref, y_ref):
        xv = x_ref[...]
        r = lax.rsqrt(jnp.mean(xv * xv, axis=-1, keepdims=True) + EPS)
        y_ref[...] = _bf(xv * r * g_ref[...])

    return pl.pallas_call(
        body, name=name, grid=(S // tm,),
        in_specs=[pl.BlockSpec((tm, D), lambda i: (i, 0)), pl.BlockSpec((1, D), lambda i: (0, 0))],
        out_specs=pl.BlockSpec((tm, D), lambda i: (i, 0)),
        out_shape=jax.ShapeDtypeStruct((S, D), BF16),
        compiler_params=_params(("parallel",)),
    )(x, g)


def _rms_bwd(name, x, g, dy, dres, tm=512):
    S, D = x.shape

    def body(x_ref, g_ref, dy_ref, dres_ref, dx_ref, dg_ref):
        i = pl.program_id(0)
        xv = x_ref[...]
        r = lax.rsqrt(jnp.mean(xv * xv, axis=-1, keepdims=True) + EPS)
        xh = xv * r
        dy_v = dy_ref[...]
        dyg = dy_v * g_ref[...]
        c = jnp.mean(dyg * xh, axis=-1, keepdims=True)
        dx_ref[...] = dres_ref[...] + r * (dyg - xh * c)
        part = jnp.sum(dy_v * xh, axis=0, keepdims=True)

        @pl.when(i == 0)
        def _():
            dg_ref[...] = part

        @pl.when(i > 0)
        def _():
            dg_ref[...] += part

    row = pl.BlockSpec((tm, D), lambda i: (i, 0))
    vec = pl.BlockSpec((1, D), lambda i: (0, 0))
    return pl.pallas_call(
        body, name=name, grid=(S // tm,),
        in_specs=[row, vec, row, row],
        out_specs=[row, vec],
        out_shape=[jax.ShapeDtypeStruct((S, D), F32), jax.ShapeDtypeStruct((1, D), F32)],
        compiler_params=_params(("arbitrary",)),
    )(x, g, dy, dres)


def _final_loss(name, h, g, target, tm=512):
    S, D = h.shape

    def body(h_ref, g_ref, t_ref, dh_ref, dg_ref, l_ref):
        i = pl.program_id(0)
        xv = h_ref[...]
        r = lax.rsqrt(jnp.mean(xv * xv, axis=-1, keepdims=True) + EPS)
        xh = xv * r
        gv = g_ref[...]
        e = xh * gv - t_ref[...]
        dout = e / D
        dyg = dout * gv
        c = jnp.mean(dyg * xh, axis=-1, keepdims=True)
        dh_ref[...] = r * (dyg - xh * c)
        dg_part = jnp.sum(dout * xh, axis=0, keepdims=True)
        l_part = jnp.sum(e * e, axis=0, keepdims=True)

        @pl.when(i == 0)
        def _():
            dg_ref[...] = dg_part
            l_ref[...] = l_part

        @pl.when(i > 0)
        def _():
            dg_ref[...] += dg_part
            l_ref[...] += l_part

    row = pl.BlockSpec((tm, D), lambda i: (i, 0))
    vec = pl.BlockSpec((1, D), lambda i: (0, 0))
    return pl.pallas_call(
        body, name=name, grid=(S // tm,),
        in_specs=[row, vec, row],
        out_specs=[row, vec, vec],
        out_shape=[jax.ShapeDtypeStruct((S, D), F32), jax.ShapeDtypeStruct((1, D), F32),
                   jax.ShapeDtypeStruct((1, D), F32)],
        compiler_params=_params(("arbitrary",)),
    )(h, g, target)


def _mm_rows(name, pairs, n_out, *, nt, tm, nc, epi="plain", extra=None, out_dtype=F32):
    M = pairs[0][0].shape[0]
    np_ = len(pairs)

    def body(*refs):
        a_refs = refs[:np_]
        w_refs = refs[np_:2 * np_]
        pos = 2 * np_
        e_ref = None
        if extra is not None:
            e_ref = refs[pos]
            pos += 1
        outs = refs[pos:]
        a_vals = [_bf(a[...]) for a in a_refs]
        for j in range(n_out // nc):
            cols = slice(j * nc, (j + 1) * nc)
            acc = None
            for a_v, w_ref in zip(a_vals, w_refs):
                w_v = w_ref[cols, :] if nt else w_ref[:, cols]
                t = _dot(a_v, w_v, NT if nt else NN)
                acc = t if acc is None else acc + t
            if epi == "plain":
                outs[0][:, cols] = acc.astype(out_dtype)
            elif epi == "res":
                outs[0][:, cols] = e_ref[:, cols] + acc
            elif epi == "relu2":
                outs[0][:, cols] = acc
                rl = jnp.maximum(acc, 0.0)
                outs[1][:, cols] = _bf(rl * rl)
            elif epi == "mask2relu":
                outs[0][:, cols] = _bf(acc * (2.0 * jnp.maximum(e_ref[:, cols], 0.0)))

    in_specs = [pl.BlockSpec((tm, a.shape[1]), lambda i: (i, 0)) for a, _, _, _ in pairs]
    for _, _, wshape, widx in pairs:
        in_specs.append(pl.BlockSpec(wshape, functools.partial(lambda i, widx: widx, widx=widx)))
    args = [a for a, _, _, _ in pairs] + [w for _, w, _, _ in pairs]
    if extra is not None:
        in_specs.append(pl.BlockSpec((tm, n_out), lambda i: (i, 0)))
        args.append(extra)
    row_out = pl.BlockSpec((tm, n_out), lambda i: (i, 0))
    if epi == "relu2":
        out_specs = [row_out, row_out]
        out_shape = [jax.ShapeDtypeStruct((M, n_out), F32), jax.ShapeDtypeStruct((M, n_out), BF16)]
    else:
        dt = BF16 if epi == "mask2relu" else (F32 if epi == "res" else out_dtype)
        out_specs = row_out
        out_shape = jax.ShapeDtypeStruct((M, n_out), dt)
    return pl.pallas_call(
        body, name=name, grid=(M // tm,), in_specs=in_specs, out_specs=out_specs, out_shape=out_shape,
        compiler_params=_params(("parallel",)),
    )(*args)


def _full(w):
    return (w, w.shape, (0, 0))


def _mm_nn(name, a, w, **kw):
    return _mm_rows(name, [(a, w, w.shape, (0, 0))], w.shape[1], nt=False, **kw)


def _mm_nt(name, a, w, **kw):
    return _mm_rows(name, [(a, w, w.shape, (0, 0))], w.shape[0], nt=True, **kw)


def _mm_tn(name, a, b, *, t1, tn, tm=512):
    M, K1 = a.shape
    N = b.shape[1]
    nm = M // tm

    def body(a_ref, b_ref, o_ref, acc_ref):
        m = pl.program_id(2)
        t = _dot(_bf(a_ref[...]), _bf(b_ref[...]), TN)

        @pl.when(m == 0)
        def _():
            acc_ref[...] = t

        @pl.when(m > 0)
        def _():
            acc_ref[...] += t

        @pl.when(m == nm - 1)
        def _():
            o_ref[...] = acc_ref[...]

    return pl.pallas_call(
        body, name=name, grid=(K1 // t1, N // tn, nm),
        in_specs=[pl.BlockSpec((tm, t1), lambda i, j, m: (m, i)), pl.BlockSpec((tm, tn), lambda i, j, m: (m, j))],
        out_specs=pl.BlockSpec((t1, tn), lambda i, j, m: (i, j)),
        out_shape=jax.ShapeDtypeStruct((K1, N), F32),
        scratch_shapes=[pltpu.VMEM((t1, tn), F32)],
        compiler_params=_params(("parallel", "parallel", "arbitrary")),
    )(a, b)


_INV_SQRT2 = 1.0 / math.sqrt(2.0)
_INV_SQRT2PI = 1.0 / math.sqrt(2.0 * math.pi)


def _gelu(x):
    return 0.5 * x * (1.0 + lax.erf(x * _INV_SQRT2))


def _gelu_grad(x):
    return 0.5 * (1.0 + lax.erf(x * _INV_SQRT2)) + x * (_INV_SQRT2PI * jnp.exp(-0.5 * x * x))


def _layer_norm_parts(v):
    mu = jnp.mean(v, axis=-1, keepdims=True)
    xc = v - mu
    rs = lax.rsqrt(jnp.mean(xc * xc, axis=-1, keepdims=True) + EPS)
    return xc * rs, rs


def _gate_fwd(name, uvp, ln_g, ln_b, wm, bs_full, tr=512):
    S, W2 = uvp.shape
    W = W2 // 2
    gd = W // GROUPS

    def body(u_ref, v_ref, lg_ref, lb_ref, wm_ref, bs_ref, z_ref):
        u = _gelu(u_ref[...])
        vh, _ = _layer_norm_parts(_gelu(v_ref[...]))
        vn = _bf(vh * lg_ref[...] + lb_ref[...])
        for ci in range(tr // CHUNK):
            rows = slice(ci * CHUNK, (ci + 1) * CHUNK)
            for g in range(GROUPS):
                cols = slice(g * gd, (g + 1) * gd)
                mixed = _dot(wm_ref[g], vn[rows, cols], NN) + bs_ref[:, cols]
                z_ref[rows, cols] = _bf(u[rows, cols] * mixed)

    vec = pl.BlockSpec((1, W), lambda i: (0, 0))
    return pl.pallas_call(
        body, name=name, grid=(S // tr,),
        in_specs=[pl.BlockSpec((tr, W), lambda i: (i, 0)), pl.BlockSpec((tr, W), lambda i: (i, 1)), vec, vec,
                  pl.BlockSpec((GROUPS, CHUNK, CHUNK), lambda i: (0, 0, 0)),
                  pl.BlockSpec((CHUNK, W), lambda i: (0, 0))],
        out_specs=pl.BlockSpec((tr, W), lambda i: (i, 0)),
        out_shape=jax.ShapeDtypeStruct((S, W), BF16),
        compiler_params=_params(("parallel",)),
    )(uvp, uvp, ln_g, ln_b, wm, bs_full)


def _gate_bwd(name, uvp, dz, ln_g, ln_b, wm, bs_full, tr=256):
    S, W2 = uvp.shape
    W = W2 // 2
    gd = W // GROUPS
    n_steps = S // tr

    def body(u_ref, v_ref, dz_ref, lg_ref, lb_ref, wm_ref, bs_ref, duv_ref, dwm_ref, dmx_ref, dlg_ref, dlb_ref,
             dvn_ref):
        i = pl.program_id(0)
        up = u_ref[...]
        vp = v_ref[...]
        u = _gelu(up)
        vh, rs = _layer_norm_parts(_gelu(vp))
        lg = lg_ref[...]
        vn = _bf(vh * lg + lb_ref[...])
        dz_v = dz_ref[...]
        dmixed = dz_v * u
        dmixed_b = _bf(dmixed)

        @pl.when(i == 0)
        def _():
            dwm_ref[...] = jnp.zeros_like(dwm_ref)
            dmx_ref[...] = jnp.zeros_like(dmx_ref)
            dlg_ref[...] = jnp.zeros_like(dlg_ref)
            dlb_ref[...] = jnp.zeros_like(dlb_ref)

        for ci in range(tr // CHUNK):
            rows = slice(ci * CHUNK, (ci + 1) * CHUNK)
            dmx_ref[...] += dmixed[rows, :]
            for g in range(GROUPS):
                cols = slice(g * gd, (g + 1) * gd)
                mixed = _dot(wm_ref[g], vn[rows, cols], NN) + bs_ref[:, cols]
                duv_ref[rows, cols] = _bf(dz_v[rows, cols] * mixed * _gelu_grad(up[rows, cols]))
                dwm_ref[g] += _dot(dmixed_b[rows, cols], vn[rows, cols], NT)
                dvn_ref[rows, cols] = _dot(wm_ref[g], dmixed_b[rows, cols], TN)
        dvn = dvn_ref[...]
        dlg_ref[...] += jnp.sum(dvn * vh, axis=0, keepdims=True)
        dlb_ref[...] += jnp.sum(dvn, axis=0, keepdims=True)
        dvh = dvn * lg
        dv = rs * (dvh - jnp.mean(dvh, axis=-1, keepdims=True) - vh * jnp.mean(dvh * vh, axis=-1, keepdims=True))
        duv_ref[:, W:] = _bf(dv * _gelu_grad(vp))

        @pl.when(i == n_steps - 1)
        def _():
            t_idx = lax.broadcasted_iota(jnp.int32, (CHUNK, CHUNK), 0)
            s_idx = lax.broadcasted_iota(jnp.int32, (CHUNK, CHUNK), 1)
            keep = (s_idx <= t_idx).astype(F32)
            for g in range(GROUPS):
                dwm_ref[g] = dwm_ref[g] * keep

    vec = pl.BlockSpec((1, W), lambda i: (0, 0))
    row = pl.BlockSpec((tr, W), lambda i: (i, 0))
    return pl.pallas_call(
        body, name=name, grid=(n_steps,),
        in_specs=[row, pl.BlockSpec((tr, W), lambda i: (i, 1)), row, vec, vec,
                  pl.BlockSpec((GROUPS, CHUNK, CHUNK), lambda i: (0, 0, 0)),
                  pl.BlockSpec((CHUNK, W), lambda i: (0, 0))],
        out_specs=[pl.BlockSpec((tr, W2), lambda i: (i, 0)),
                   pl.BlockSpec((GROUPS, CHUNK, CHUNK), lambda i: (0, 0, 0)),
                   pl.BlockSpec((CHUNK, W), lambda i: (0, 0)), vec, vec],
        out_shape=[jax.ShapeDtypeStruct((S, W2), BF16), jax.ShapeDtypeStruct((GROUPS, CHUNK, CHUNK), F32),
                   jax.ShapeDtypeStruct((CHUNK, W), F32), jax.ShapeDtypeStruct((1, W), F32),
                   jax.ShapeDtypeStruct((1, W), F32)],
        scratch_shapes=[pltpu.VMEM((tr, W), F32)],
        compiler_params=_params(("arbitrary",)),
    )(uvp, uvp, dz, ln_g, ln_b, wm, bs_full)


def _t5_bucket(distance):
    small = distance < MAX_EXACT
    nf = jnp.maximum(distance, 1).astype(F32)
    large = MAX_EXACT + (jnp.log(nf / MAX_EXACT) / math.log(REL_MAX_DISTANCE / MAX_EXACT)
                         * (N_BUCKETS - MAX_EXACT)).astype(jnp.int32)
    large = jnp.minimum(large, N_BUCKETS - 1)
    return jnp.where(small, distance, large)


def _band_buckets():
    rel = CHUNK + jnp.arange(CHUNK)[:, None] - jnp.arange(2 * CHUNK)[None, :]
    band = (rel >= 0) & (rel <= CHUNK)
    buckets = [_t5_bucket(jnp.clip(rel, 0, CHUNK) * d) for d in DILATIONS]
    return jnp.stack(buckets), band


def _bias_tiles(rel_bias):
    buckets, band = _band_buckets()
    tiles = []
    for g in range(N_DIL):
        table = rel_bias[:, g * ATT_HEADS:(g + 1) * ATT_HEADS]
        bias = jnp.transpose(table[buckets[g]], (2, 0, 1))
        inner = jnp.where(band[None], bias, NEG_INF)
        first = jnp.where((jnp.arange(2 * CHUNK) >= CHUNK)[None, None, :], inner, NEG_INF)
        tiles.append(jnp.stack([first, inner]))
    return jnp.stack(tiles)


def _att_specs(order):
    def spec(part, prev):
        def index(*ids):
            hp, g, c = order(*ids)
            return (jnp.maximum(c - 1, 0) if prev else c, part * 3 * 4 + g * 4 + hp)
        return pl.BlockSpec((ATT_ROWS, LANES), index)
    return [spec(0, False), spec(1, False), spec(1, True), spec(2, False), spec(2, True)]


def _rows(start, d):
    if d == 1:
        return pl.ds(pl.multiple_of(start, CHUNK), CHUNK)
    return pl.ds(start, CHUNK, stride=d)


def _att_tile_offsets(t, d):
    n = t // d
    r = t % d
    return n * (CHUNK * d) + r, n


def _stage_prev_cur(dst, prev_ref, cur_ref):
    dst[0:ATT_ROWS, :] = prev_ref[...]
    dst[ATT_ROWS:2 * ATT_ROWS, :] = cur_ref[...]


def _att_fwd(name, qkv, bias_tiles):
    S = qkv.shape[0]
    n_chunks = S // ATT_ROWS
    tiles = ATT_ROWS // CHUNK

    def body(q_ref, kc_ref, kp_ref, vc_ref, vp_ref, b_ref, o_ref, l_ref, kk, vv, o_sc, m_sc, l_sc):
        c = pl.program_id(1)
        g = pl.program_id(2)

        @pl.when(g == 0)
        def _():
            o_sc[...] = jnp.zeros_like(o_sc)
            m_sc[...] = jnp.full_like(m_sc, NEG_INF)
            l_sc[...] = jnp.zeros_like(l_sc)

        _stage_prev_cur(kk, kp_ref, kc_ref)
        _stage_prev_cur(vv, vp_ref, vc_ref)

        for gi, d in enumerate(DILATIONS):
            @pl.when(g == gi)
            def _(gi=gi, d=d):
                span = CHUNK * d

                def tile(t, carry):
                    q0, n = _att_tile_offsets(t, d)
                    rows = _rows(q0, d)
                    cur = _rows(ATT_ROWS + q0, d)
                    prev = _rows(ATT_ROWS + q0 - span, d)
                    inner = jnp.where((c == 0) & (n == 0), 0, 1)
                    q2 = q_ref[rows, :] * (HEAD_DIM ** -0.5)
                    k2 = jnp.concatenate([kk[prev, :], kk[cur, :]], axis=0)
                    v2 = jnp.concatenate([vv[prev, :], vv[cur, :]], axis=0)
                    o_old = o_sc[rows, :]
                    o_new = []
                    for hh in range(2):
                        lanes = slice(hh * HEAD_DIM, (hh + 1) * HEAD_DIM)
                        s = _dot(_bf(q2[:, lanes]), _bf(k2[:, lanes]), NT) + b_ref[inner, hh]
                        m_old = m_sc[hh, rows, :]
                        m_new = jnp.maximum(m_old, jnp.max(s, axis=-1, keepdims=True))
                        p = jnp.exp(s - m_new)
                        alpha = jnp.exp(m_old - m_new)
                        l_sc[hh, rows, :] = alpha * l_sc[hh, rows, :] + jnp.sum(p, axis=-1, keepdims=True)
                        m_sc[hh, rows, :] = m_new
                        o_new.append(alpha * o_old[:, lanes] + _dot(_bf(p), _bf(v2[:, lanes]), NN))
                    o_sc[rows, :] = jnp.concatenate(o_new, axis=-1)
                    return carry

                lax.fori_loop(0, tiles, tile, 0)

        @pl.when(g == N_DIL - 1)
        def _():
            for hh in range(2):
                lanes = slice(hh * HEAD_DIM, (hh + 1) * HEAD_DIM)
                l_v = l_sc[hh]
                o_ref[:, lanes] = o_sc[:, lanes] / l_v
                l_ref[:, lanes] = jnp.broadcast_to(m_sc[hh] + jnp.log(l_v), (ATT_ROWS, HEAD_DIM))

    order = lambda hp, c, g: (hp, g, c)
    out_spec = pl.BlockSpec((ATT_ROWS, LANES), lambda hp, c, g: (c, hp))
    return pl.pallas_call(
        body, name=name, grid=(ATT_HEADS // 2, n_chunks, N_DIL),
        in_specs=_att_specs(order) + [
            pl.BlockSpec((None, 2, 2, CHUNK, 2 * CHUNK), lambda hp, c, g: (g, 0, hp, 0, 0))],
        out_specs=[out_spec, out_spec],
        out_shape=[jax.ShapeDtypeStruct((S, ATT_WIDTH), F32), jax.ShapeDtypeStruct((S, ATT_WIDTH), F32)],
        scratch_shapes=[pltpu.VMEM((2 * ATT_ROWS, LANES), F32), pltpu.VMEM((2 * ATT_ROWS, LANES), F32),
                        pltpu.VMEM((ATT_ROWS, LANES), F32), pltpu.VMEM((2, ATT_ROWS, 1), F32),
                        pltpu.VMEM((2, ATT_ROWS, 1), F32)],
        compiler_params=_params(("parallel", "parallel", "arbitrary")),
    )(qkv, qkv, qkv, qkv, qkv, bias_tiles)


def _att_bwd(name, qkv, o, lse, d_o, bias_tiles):
    S = qkv.shape[0]
    n_chunks = S // ATT_ROWS
    tiles = ATT_ROWS // CHUNK

    def body(q_ref, kc_ref, kp_ref, vc_ref, vp_ref, o_ref, l_ref, do_ref, b_ref, dq_ref, dk_ref, dv_ref, ds_ref,
             kk, vv):
        g = pl.program_id(1)
        c = pl.program_id(2)

        @pl.when(c == 0)
        def _():
            dk_ref[...] = jnp.zeros_like(dk_ref)
            dv_ref[...] = jnp.zeros_like(dv_ref)
            ds_ref[...] = jnp.zeros_like(ds_ref)

        _stage_prev_cur(kk, kp_ref, kc_ref)
        _stage_prev_cur(vv, vp_ref, vc_ref)
        base = c * ATT_ROWS

        for gi, d in enumerate(DILATIONS):
            @pl.when(g == gi)
            def _(gi=gi, d=d):
                span = CHUNK * d

                def tile(t, carry):
                    q0, n = _att_tile_offsets(t, d)
                    rows = _rows(q0, d)
                    cur = _rows(ATT_ROWS + q0, d)
                    prev = _rows(ATT_ROWS + q0 - span, d)
                    first = (c == 0) & (n == 0)
                    inner = jnp.where(first, 0, 1)
                    g_cur = _rows(base + q0, d)
                    g_prev = _rows(jnp.where(first, q0, base + q0 - span), d)
                    q2 = q_ref[rows, :] * (HEAD_DIM ** -0.5)
                    k2 = jnp.concatenate([kk[prev, :], kk[cur, :]], axis=0)
                    v2 = jnp.concatenate([vv[prev, :], vv[cur, :]], axis=0)
                    do2 = do_ref[rows, :]
                    o2 = o_ref[rows, :]
                    l2 = l_ref[rows, :]
                    dq_h, dk_h, dv_h = [], [], []
                    for hh in range(2):
                        lanes = slice(hh * HEAD_DIM, (hh + 1) * HEAD_DIM)
                        qh = _bf(q2[:, lanes])
                        kh = _bf(k2[:, lanes])
                        doh = do2[:, lanes]
                        doh_b = _bf(doh)
                        s = _dot(qh, kh, NT) + b_ref[inner, hh]
                        p = jnp.exp(s - l2[:, hh * HEAD_DIM:hh * HEAD_DIM + 1])
                        delta = jnp.sum(doh * o2[:, lanes], axis=-1, keepdims=True)
                        dp = _dot(doh_b, _bf(v2[:, lanes]), NT)
                        ds = p * (dp - delta)
                        ds_ref[hh] += ds
                        ds_b = _bf(ds)
                        dq_h.append(_dot(ds_b, kh, NN) * (HEAD_DIM ** -0.5))
                        dk_h.append(_dot(ds_b, qh, TN))
                        dv_h.append(_dot(_bf(p), doh_b, TN))
                    dq_ref[rows, :] = jnp.concatenate(dq_h, axis=-1)
                    dk2 = jnp.concatenate(dk_h, axis=-1)
                    dv2 = jnp.concatenate(dv_h, axis=-1)
                    dk_ref[g_prev, :] += dk2[0:CHUNK]
                    dk_ref[g_cur, :] += dk2[CHUNK:2 * CHUNK]
                    dv_ref[g_prev, :] += dv2[0:CHUNK]
                    dv_ref[g_cur, :] += dv2[CHUNK:2 * CHUNK]
                    return carry

                lax.fori_loop(0, tiles, tile, 0)

    order = lambda hp, g, c: (hp, g, c)
    chunk = pl.BlockSpec((ATT_ROWS, LANES), lambda hp, g, c: (c, hp))
    slab = pl.BlockSpec((S, LANES), lambda hp, g, c: (0, g * 4 + hp))
    width = N_DIL * ATT_WIDTH
    return pl.pallas_call(
        body, name=name, grid=(ATT_HEADS // 2, N_DIL, n_chunks),
        in_specs=_att_specs(order) + [chunk, chunk, chunk,
                                      pl.BlockSpec((None, 2, 2, CHUNK, 2 * CHUNK),
                                                   lambda hp, g, c: (g, 0, hp, 0, 0))],
        out_specs=[pl.BlockSpec((ATT_ROWS, LANES), lambda hp, g, c: (c, g * 4 + hp)), slab, slab,
                   pl.BlockSpec((None, 2, CHUNK, 2 * CHUNK), lambda hp, g, c: (g, hp, 0, 0))],
        out_shape=[jax.ShapeDtypeStruct((S, width), F32), jax.ShapeDtypeStruct((S, width), F32),
                   jax.ShapeDtypeStruct((S, width), F32),
                   jax.ShapeDtypeStruct((N_DIL, ATT_HEADS, CHUNK, 2 * CHUNK), F32)],
        scratch_shapes=[pltpu.VMEM((2 * ATT_ROWS, LANES), F32), pltpu.VMEM((2 * ATT_ROWS, LANES), F32)],
        compiler_params=_params(("parallel", "parallel", "arbitrary")),
    )(qkv, qkv, qkv, qkv, qkv, o, lse, d_o, bias_tiles)


def _bias_grad(name, ds_sums):
    buckets, _ = _band_buckets()
    onehot = (buckets.reshape(N_DIL, 1, CHUNK * 2 * CHUNK)
              == jnp.arange(N_BUCKETS)[None, :, None]).astype(F32)
    flat = ds_sums.reshape(N_DIL, ATT_HEADS, CHUNK * 2 * CHUNK)

    def body(oh_ref, ds_ref, out_ref):
        for g in range(N_DIL):
            out_ref[g] = lax.dot_general(oh_ref[g], ds_ref[g], ((NT), ((), ())), precision=lax.Precision.HIGHEST,
                                         preferred_element_type=F32)

    out = pl.pallas_call(
        body, name=name, out_shape=jax.ShapeDtypeStruct((N_DIL, N_BUCKETS, ATT_HEADS), F32),
        compiler_params=_params(),
    )(onehot, flat)
    return jnp.transpose(out, (1, 0, 2)).reshape(N_BUCKETS, N_DIL * ATT_HEADS)


def _peers():
    x, y, c = lax.axis_index("x"), lax.axis_index("y"), lax.axis_index("c")
    me = 4 * x + 2 * y + c
    others = [(x, y, 1 - c), (1 - x, y, c), (x, 1 - y, c), (1 - x, 1 - y, c),
              (1 - x, y, 1 - c), (x, 1 - y, 1 - c), (1 - x, 1 - y, 1 - c)]
    return me, others


def _slot(dev):
    return 4 * dev[0] + 2 * dev[1] + dev[2]


def _all_gather(name, shard):
    R, W = shard.shape

    def body(x_ref, out_ref, send_sems, recv_sems, local_sem):
        me, others = _peers()
        mine = pltpu.make_async_copy(x_ref, out_ref.at[me], local_sem)
        mine.start()
        sends = [pltpu.make_async_remote_copy(src_ref=x_ref, dst_ref=out_ref.at[me], send_sem=send_sems.at[k],
                                              recv_sem=recv_sems.at[k], device_id=dev, device_id_type=MESH)
                 for k, dev in enumerate(others)]
        for cp in sends:
            cp.start()
        for k, dev in enumerate(others):
            pltpu.make_async_remote_copy(src_ref=x_ref, dst_ref=out_ref.at[_slot(dev)], send_sem=send_sems.at[k],
                                         recv_sem=recv_sems.at[k], device_id=dev, device_id_type=MESH).wait_recv()
        for cp in sends:
            cp.wait_send()
        mine.wait()

    return pl.pallas_call(
        body, name=name,
        in_specs=[pl.BlockSpec(memory_space=pl.ANY)],
        out_specs=pl.BlockSpec(memory_space=pl.ANY),
        out_shape=jax.ShapeDtypeStruct((N_DEV, R, W), shard.dtype),
        scratch_shapes=[pltpu.SemaphoreType.DMA((N_DEV - 1,)), pltpu.SemaphoreType.DMA((N_DEV - 1,)),
                        pltpu.SemaphoreType.DMA],
    )(shard)


def _scatter_parts(name, parts):
    _, R, W = parts.shape

    def body(x_ref, out_ref, send_sems, recv_sems, local_sem):
        me, others = _peers()
        mine = pltpu.make_async_copy(x_ref.at[me], out_ref.at[me], local_sem)
        mine.start()
        sends = [pltpu.make_async_remote_copy(src_ref=x_ref.at[_slot(dev)], dst_ref=out_ref.at[me],
                                              send_sem=send_sems.at[k], recv_sem=recv_sems.at[k],
                                              device_id=dev, device_id_type=MESH)
                 for k, dev in enumerate(others)]
        for cp in sends:
            cp.start()
        for k, dev in enumerate(others):
            pltpu.make_async_remote_copy(src_ref=x_ref.at[me], dst_ref=out_ref.at[_slot(dev)],
                                         send_sem=send_sems.at[k], recv_sem=recv_sems.at[k],
                                         device_id=dev, device_id_type=MESH).wait_recv()
        for cp in sends:
            cp.wait_send()
        mine.wait()

    return pl.pallas_call(
        body, name=name,
        in_specs=[pl.BlockSpec(memory_space=pl.ANY)],
        out_specs=pl.BlockSpec(memory_space=pl.ANY),
        out_shape=jax.ShapeDtypeStruct(parts.shape, parts.dtype),
        scratch_shapes=[pltpu.SemaphoreType.DMA((N_DEV - 1,)), pltpu.SemaphoreType.DMA((N_DEV - 1,)),
                        pltpu.SemaphoreType.DMA],
    )(parts)


def _all_reduce_small(name, buf):
    rows = buf.shape[0]
    rb = rows // N_DEV

    def body(x_ref, out_ref, stage, send1, recv1, send2, recv2):
        me, others = _peers()

        def block(ref, k):
            return ref.at[pl.ds(k * rb, rb), :]

        first = [pltpu.make_async_remote_copy(src_ref=block(x_ref, _slot(dev)), dst_ref=stage.at[me],
                                              send_sem=send1.at[k], recv_sem=recv1.at[k], device_id=dev,
                                              device_id_type=MESH) for k, dev in enumerate(others)]
        for cp in first:
            cp.start()
        stage[me] = x_ref[pl.ds(pl.multiple_of(me * rb, 8), rb), :]
        for k, dev in enumerate(others):
            pltpu.make_async_remote_copy(src_ref=block(x_ref, me), dst_ref=stage.at[_slot(dev)],
                                         send_sem=send1.at[k], recv_sem=recv1.at[k], device_id=dev,
                                         device_id_type=MESH).wait_recv()
        total = stage[0]
        for j in range(1, N_DEV):
            total = total + stage[j]
        out_ref[pl.ds(pl.multiple_of(me * rb, 8), rb), :] = total
        second = [pltpu.make_async_remote_copy(src_ref=block(out_ref, me), dst_ref=block(out_ref, me),
                                               send_sem=send2.at[k], recv_sem=recv2.at[k], device_id=dev,
                                               device_id_type=MESH) for k, dev in enumerate(others)]
        for cp in second:
            cp.start()
        for k, dev in enumerate(others):
            pltpu.make_async_remote_copy(src_ref=block(out_ref, me), dst_ref=block(out_ref, _slot(dev)),
                                         send_sem=send2.at[k], recv_sem=recv2.at[k], device_id=dev,
                                         device_id_type=MESH).wait_recv()
        for cp in first + second:
            cp.wait_send()

    sems = pltpu.SemaphoreType.DMA((N_DEV - 1,))
    return pl.pallas_call(
        body, name=name,
        in_specs=[pl.BlockSpec(memory_space=pltpu.VMEM)],
        out_specs=pl.BlockSpec(memory_space=pltpu.VMEM),
        out_shape=jax.ShapeDtypeStruct(buf.shape, F32),
        scratch_shapes=[pltpu.VMEM((N_DEV, rb, LANES), F32), sems, sems, sems, sems],
        compiler_params=pltpu.CompilerParams(vmem_limit_bytes=VMEM_LIMIT_BYTES),
    )(buf)


def _adamw_math(w, g, m, v):
    m = ADAM_B1 * m + (1.0 - ADAM_B1) * g
    v = ADAM_B2 * v + (1.0 - ADAM_B2) * (g * g)
    m_hat = m / (1.0 - ADAM_B1 ** ADAM_STEP)
    v_hat = v / (1.0 - ADAM_B2 ** ADAM_STEP)
    delta = -ADAM_LR * (m_hat / (jnp.sqrt(v_hat) + ADAM_EPS) + ADAM_WD * w)
    return delta, m, v


def _adamw(name, parts, w, m, v, tr=256):
    P, R, W = parts.shape
    tr = min(tr, R)

    def body(p_ref, w_ref, m_ref, v_ref, g_out, d_out, m_out, v_out):
        g = p_ref[0].astype(F32)
        for j in range(1, P):
            g = g + p_ref[j].astype(F32)
        delta, m_new, v_new = _adamw_math(w_ref[...], g, m_ref[...], v_ref[...])
        g_out[...] = g
        d_out[...] = delta
        m_out[...] = m_new
        v_out[...] = v_new

    row = pl.BlockSpec((tr, W), lambda i: (i, 0))
    shape = jax.ShapeDtypeStruct((R, W), F32)
    return pl.pallas_call(
        body, name=name, grid=(R // tr,),
        in_specs=[pl.BlockSpec((P, tr, W), lambda i: (0, i, 0)), row, row, row],
        out_specs=[row, row, row, row],
        out_shape=[shape, shape, shape, shape],
        compiler_params=_params(("parallel",)),
    )(parts, w, m, v)


def _col_shards(full, n_local):
    K = full.shape[0]
    t = jnp.transpose(full.reshape(K, N_DEV, n_local), (1, 0, 2))
    return t.reshape(N_DEV, K * n_local // PACK_W, PACK_W)


def _from_col_shards(slots, K, n_local):
    t = slots.reshape(N_DEV, K, n_local)
    return jnp.transpose(t, (1, 0, 2)).reshape(K, N_DEV * n_local)


def _pack_rows(a):
    return a.reshape(-1, PACK_W)


_SMALL = ("mix_norm_g", "mlp_norm_g", "final_norm_g", "a_ln_g", "a_ln_b", "a_w_s", "a_b_s", "rel_bias")


def _pack_small(vals):
    pieces = []
    for n in _SMALL:
        flat = vals[n].reshape(-1)
        pad = (-flat.shape[0]) % (8 * LANES)
        pieces.append(jnp.pad(flat, (0, pad)).reshape(-1, LANES))
    rows = sum(p.shape[0] for p in pieces)
    tail = (-rows) % (8 * N_DEV)
    if tail:
        pieces.append(jnp.zeros((tail, LANES), F32))
    return jnp.concatenate(pieces, axis=0)


def _unpack_small(buf, like):
    out = {}
    r = 0
    for n in _SMALL:
        size = like[n].size
        nrows = -(-size // (8 * LANES)) * 8
        out[n] = buf[r:r + nrows].reshape(-1)[:size].reshape(like[n].shape)
        r += nrows
    return out


def kernel(x, mix_norm_g, mlp_norm_g, final_norm_g, a_w_in, a_ln_g, a_ln_b, a_w_s, a_b_s, a_w_out, b_w_qkv, b_w_out, rel_bias, w_up, w_down, loss_target, m_mix_norm_g, m_mlp_norm_g, m_final_norm_g, m_a_w_in, m_a_ln_g, m_a_ln_b, m_a_w_s, m_a_b_s, m_a_w_out, m_b_w_qkv, m_b_w_out, m_rel_bias, m_w_up, m_w_down, v_mix_norm_g, v_mlp_norm_g, v_final_norm_g, v_a_w_in, v_a_ln_g, v_a_ln_b, v_a_w_s, v_a_b_s, v_a_w_out, v_b_w_qkv, v_b_w_out, v_rel_bias, v_w_up, v_w_down):
    big = ("a_w_in", "a_w_out", "b_w_qkv", "b_w_out", "w_up", "w_down")
    w = dict(mix_norm_g=mix_norm_g, mlp_norm_g=mlp_norm_g, final_norm_g=final_norm_g, a_w_in=a_w_in, a_ln_g=a_ln_g,
             a_ln_b=a_ln_b, a_w_s=a_w_s, a_b_s=a_b_s, a_w_out=a_w_out, b_w_qkv=b_w_qkv, b_w_out=b_w_out,
             rel_bias=rel_bias, w_up=w_up, w_down=w_down)
    m = dict(mix_norm_g=m_mix_norm_g, mlp_norm_g=m_mlp_norm_g, final_norm_g=m_final_norm_g, a_w_in=m_a_w_in,
             a_ln_g=m_a_ln_g, a_ln_b=m_a_ln_b, a_w_s=m_a_w_s, a_b_s=m_a_b_s, a_w_out=m_a_w_out, b_w_qkv=m_b_w_qkv,
             b_w_out=m_b_w_out, rel_bias=m_rel_bias, w_up=m_w_up, w_down=m_w_down)
    v = dict(mix_norm_g=v_mix_norm_g, mlp_norm_g=v_mlp_norm_g, final_norm_g=v_final_norm_g, a_w_in=v_a_w_in,
             a_ln_g=v_a_ln_g, a_ln_b=v_a_ln_b, a_w_s=v_a_w_s, a_b_s=v_a_b_s, a_w_out=v_a_w_out, b_w_qkv=v_b_w_qkv,
             b_w_out=v_b_w_out, rel_bias=v_rel_bias, w_up=v_w_up, w_down=v_w_down)

    D = x.shape[-1]
    d_ff = w_down.shape[1] * N_DEV
    n_in = a_w_in.shape[2]
    n_qkv = b_w_qkv.shape[2]
    n_bo = b_w_out.shape[2]
    n_up = w_up.shape[2]

    def pack_big(t):
        return jnp.concatenate([_pack_rows(t[n]) for n in big], axis=0)

    gathered = _all_gather("gather_weights", _bf(pack_big(w)))
    sizes = [w[n].size // PACK_W for n in big]
    offs = [sum(sizes[:i]) for i in range(len(big))]
    seg = {n: gathered[:, o:o + s] for n, o, s in zip(big, offs, sizes)}
    win = _from_col_shards(seg["a_w_in"], D, n_in)
    wout = seg["a_w_out"].reshape(-1, D)
    wqkv = _from_col_shards(seg["b_w_qkv"], D, n_qkv)
    wo = _from_col_shards(seg["b_w_out"], ATT_WIDTH, n_bo)
    up_l = seg["w_up"].reshape(N_DEV, 2, D, n_up)
    wup = [jnp.transpose(up_l[:, l], (1, 0, 2)).reshape(D, d_ff) for l in range(2)]
    dn_l = seg["w_down"].reshape(N_DEV, 2, d_ff // N_DEV, D)
    wdn = [dn_l[:, l].reshape(d_ff, D) for l in range(2)]

    loss_local, grad_x, grads, small_g = _local_step(
        x[0], loss_target[0], mix_norm_g, mlp_norm_g, final_norm_g, a_ln_g, a_ln_b, a_w_s, a_b_s, rel_bias,
        win, wout, wqkv, wo, wup, wdn)

    parts = jnp.concatenate([
        _col_shards(grads["a_w_in"], n_in),
        grads["a_w_out"].reshape(N_DEV, -1, PACK_W),
        _col_shards(grads["b_w_qkv"], n_qkv),
        _col_shards(grads["b_w_out"], n_bo),
        jnp.transpose(grads["w_up"].reshape(2, D, N_DEV, n_up), (2, 0, 1, 3)).reshape(N_DEV, -1, PACK_W),
        jnp.transpose(grads["w_down"].reshape(2, N_DEV, d_ff // N_DEV, D), (1, 0, 2, 3)).reshape(N_DEV, -1, PACK_W),
    ], axis=1)
    received = _scatter_parts("scatter_grads", _bf(parts))
    g_big, d_big, m_big, v_big = _adamw("adamw_big", received, pack_big(w), pack_big(m), pack_big(v))

    reduced = _all_reduce_small("reduce_small", _pack_small(small_g))
    g_sm, d_sm, m_sm, v_sm = _adamw("adamw_small", reduced[None], _pack_small(w), _pack_small(m), _pack_small(v),
                                    tr=reduced.shape[0])

    def unpack_big(buf):
        return {n: buf[o:o + s].reshape(w[n].shape) for n, o, s in zip(big, offs, sizes)}

    outs = []
    for b_buf, s_buf in ((g_big, g_sm), (d_big, d_sm), (m_big, m_sm), (v_big, v_sm)):
        full = {**unpack_big(b_buf), **_unpack_small(s_buf, w)}
        outs.extend(full[n] for n in w)
    loss = lax.psum(loss_local, ("x", "y", "c"))
    return (loss, grad_x[None], *outs)


def _local_step(xs, tgt, mix_norm_g, mlp_norm_g, final_norm_g, a_ln_g, a_ln_b, a_w_s, a_b_s, rel_bias,
                win, wout, wqkv, wo, wup, wdn):
    D = xs.shape[-1]
    g_mix = [mix_norm_g[l][None, :] for l in range(2)]
    g_mlp = [mlp_norm_g[l][None, :] for l in range(2)]
    g_fin = final_norm_g[None, :]
    ln_g, ln_b = a_ln_g, a_ln_b
    causal = jnp.tril(jnp.ones((CHUNK, CHUNK), dtype=bool))
    wm = _bf(jnp.where(causal[None], a_w_s[0], 0.0))
    bs_full = jnp.repeat(a_b_s[0].T, D // GROUPS, axis=1)
    bias_tiles = _bias_tiles(rel_bias)

    y0 = _rms_fwd("rms_mix0", xs, g_mix[0])
    uvp = _mm_nn("gate_in", y0, win, tm=512, nc=512)
    z = _gate_fwd("gate_mid", uvp, ln_g, ln_b, wm, bs_full)
    h1 = _mm_nn("gate_out", z, wout, tm=512, nc=512, epi="res", extra=xs)
    y1 = _rms_fwd("rms_mlp0", h1, g_mlp[0])
    a0, f0 = _mm_nn("mlp0_up", y1, wup[0], tm=256, nc=512, epi="relu2")
    h2 = _mm_nn("mlp0_down", f0, wdn[0], tm=512, nc=512, epi="res", extra=h1)
    y2 = _rms_fwd("rms_mix1", h2, g_mix[1])
    qkv = _mm_nn("att_qkv", y2, wqkv, tm=256, nc=512)
    o_att, lse = _att_fwd("att_fwd", qkv, bias_tiles)
    h3 = _mm_nn("att_out", o_att, wo, tm=512, nc=512, epi="res", extra=h2)
    y3 = _rms_fwd("rms_mlp1", h3, g_mlp[1])
    a1, f1 = _mm_nn("mlp1_up", y3, wup[1], tm=256, nc=512, epi="relu2")
    h4 = _mm_nn("mlp1_down", f1, wdn[1], tm=512, nc=512, epi="res", extra=h3)
    dh, dg_fin, err2 = _final_loss("final_loss", h4, g_fin, tgt)
    loss_local = 0.5 * jnp.sum(err2) / D

    grads = {}

    def mlp_bwd(tag, dh, h_in, y, a, f, wup_l, wdn_l, g_row):
        da = _mm_nt(tag + "_dact", dh, wdn_l, tm=256, nc=512, epi="mask2relu", extra=a)
        g_dn = _mm_tn(tag + "_dwdown", f, dh, t1=1024, tn=1024)
        g_up = _mm_tn(tag + "_dwup", y, da, t1=1024, tn=1024)
        dy = _mm_nt(tag + "_dy", da, wup_l, tm=512, nc=512)
        dh_in, dg = _rms_bwd(tag + "_drms", h_in, g_row, dy, dh)
        return dh_in, g_up, g_dn, dg

    dh3, g_up1, g_dn1, dg_mlp1 = mlp_bwd("mlp1", dh, h3, y3, a1, f1, wup[1], wdn[1], g_mlp[1])

    d_o = _mm_nt("att_dout", dh3, wo, tm=512, nc=512)
    grads["b_w_out"] = _mm_tn("att_dwo", o_att, dh3, t1=512, tn=1024)
    dq, dk, dv, ds_sums = _att_bwd("att_bwd", qkv, o_att, lse, d_o, bias_tiles)
    grads["rel_bias"] = _bias_grad("att_dbias", ds_sums)
    part_w = N_DIL * ATT_WIDTH
    g_qkv = [_mm_tn("att_dwqkv%d" % p, y2, t, t1=1024, tn=part_w) for p, t in enumerate((dq, dk, dv))]
    grads["b_w_qkv"] = jnp.concatenate(g_qkv, axis=1)
    dy2 = _mm_rows("att_dy", [(t, wqkv, (D, part_w), (0, p)) for p, t in enumerate((dq, dk, dv))], D,
                   nt=True, tm=256, nc=512)
    dh2, dg_mix1 = _rms_bwd("att_drms", h2, g_mix[1], dy2, dh3)

    dh1, g_up0, g_dn0, dg_mlp0 = mlp_bwd("mlp0", dh2, h1, y1, a0, f0, wup[0], wdn[0], g_mlp[0])

    dz = _mm_nt("gate_dz", dh1, wout, tm=512, nc=512)
    grads["a_w_out"] = _mm_tn("gate_dwout", z, dh1, t1=1024, tn=1024)
    duvp, d_wm, d_mixed, d_lng, d_lnb = _gate_bwd("gate_dmid", uvp, dz, ln_g, ln_b, wm, bs_full)
    grads["a_w_in"] = _mm_tn("gate_dwin", y0, duvp, t1=1024, tn=1024)
    dy0 = _mm_nt("gate_dy", duvp, win, tm=512, nc=512)
    grad_x, dg_mix0 = _rms_bwd("gate_drms", xs, g_mix[0], dy0, dh1)

    grads["w_up"] = jnp.stack([g_up0, g_up1])
    grads["w_down"] = jnp.stack([g_dn0, g_dn1])
    small_g = dict(
        mix_norm_g=jnp.concatenate([dg_mix0, dg_mix1], axis=0),
        mlp_norm_g=jnp.concatenate([dg_mlp0, dg_mlp1], axis=0),
        final_norm_g=dg_fin[0], a_ln_g=d_lng, a_ln_b=d_lnb, a_w_s=d_wm[None],
        a_b_s=jnp.sum(d_mixed.reshape(CHUNK, GROUPS, D // GROUPS), axis=2).T[None],
        rel_bias=grads["rel_bias"])
    return loss_local, grad_x, grads, small_g
```

```python
import functools
import math

import jax
import jax.numpy as jnp
from jax import lax
from jax.experimental import pallas as pl
from jax.experimental.pallas import tpu as pltpu

F32 = jnp.float32
BF16 = jnp.bfloat16
MESH = pl.DeviceIdType.MESH

N_DEV = 8
EPS = 1e-6
NEG_INF = -1e30
CHUNK = 128
GROUPS = 8
HEAD_DIM = 64
ATT_HEADS = 8
ATT_WIDTH = ATT_HEADS * HEAD_DIM
DILATIONS = (1, 4, 16)
N_DIL = len(DILATIONS)
N_BUCKETS = 32
MAX_EXACT = N_BUCKETS // 2
REL_MAX_DISTANCE = 2048
ATT_ROWS = 2048
ATT_SCALE = HEAD_DIM ** -0.5
LANES = 128
PACK_W = 1024

ADAM_LR = 0.001
ADAM_B1 = 0.9
ADAM_B2 = 0.999
ADAM_EPS = 1e-08
ADAM_WD = 0.01
ADAM_STEP = 10

VMEM_LIMIT_BYTES = 56 * 1024 * 1024


def _params(semantics=None):
    return pltpu.CompilerParams(dimension_semantics=semantics, vmem_limit_bytes=VMEM_LIMIT_BYTES)


def _bf(v):
    return v.astype(BF16)


def _dot(a, b, dims):
    return lax.dot_general(a, b, (dims, ((), ())), preferred_element_type=F32)


NN = ((1,), (0,))
NT = ((1,), (1,))
TN = ((0,), (0,))


def _rms_fwd(name, x, g, tm=512):
    S, D = x.shape

    def body(x_ref, g_ref, y_ref):
        xv = x_ref[...]
        r = lax.rsqrt(jnp.mean(xv * xv, axis=-1, keepdims=True) + EPS)
        y_ref[...] = _bf(xv * r * g_ref[...])

    return pl.pallas_call(
        body, name=name, grid=(S // tm,),
        in_specs=[pl.BlockSpec((tm, D), lambda i: (i, 0)), pl.BlockSpec((1, D), lambda i: (0, 0))],
        out_specs=pl.BlockSpec((tm, D), lambda i: (i, 0)),
        out_shape=jax.ShapeDtypeStruct((S, D), BF16),
        compiler_params=_params(("parallel",)),
    )(x, g)


def _rms_bwd(name, x, g, dy, dres, tm=512):
    S, D = x.shape

    def body(x_ref, g_ref, dy_ref, dres_ref, dx_ref, dg_ref):
        i = pl.program_id(0)
        xv = x_ref[...]
        r = lax.rsqrt(jnp.mean(xv * xv, axis=-1, keepdims=True) + EPS)
        xh = xv * r
        dy_v = dy_ref[...]
        dyg = dy_v * g_ref[...]
        c = jnp.mean(dyg * xh, axis=-1, keepdims=True)
        dx_ref[...] = dres_ref[...] + r * (dyg - xh * c)
        part = jnp.sum(dy_v * xh, axis=0, keepdims=True)

        @pl.when(i == 0)
        def _():
            dg_ref[...] = part

        @pl.when(i > 0)
        def _():
            dg_ref[...] += part

    row = pl.BlockSpec((tm, D), lambda i: (i, 0))
    vec = pl.BlockSpec((1, D), lambda i: (0, 0))
    return pl.pallas_call(
        body, name=name, grid=(S // tm,),
        in_specs=[row, vec, row, row],
        out_specs=[row, vec],
        out_shape=[jax.ShapeDtypeStruct((S, D), F32), jax.ShapeDtypeStruct((1, D), F32)],
        compiler_params=_params(("arbitrary",)),
    )(x, g, dy, dres)


def _final_loss(name, h, g, target, tm=512):
    S, D = h.shape

    def body(h_ref, g_ref, t_ref, dh_ref, dg_ref, l_ref):
        i = pl.program_id(0)
        xv = h_ref[...]
        r = lax.rsqrt(jnp.mean(xv * xv, axis=-1, keepdims=True) + EPS)
        xh = xv * r
        gv = g_ref[...]
        e = xh * gv - t_ref[...]
        dout = e / D
        dyg = dout * gv
        c = jnp.mean(dyg * xh, axis=-1, keepdims=True)
        dh_ref[...] = r * (dyg - xh * c)
        dg_part = jnp.sum(dout * xh, axis=0, keepdims=True)
        l_part = jnp.sum(e * e, axis=0, keepdims=True)

        @pl.when(i == 0)
        def _():
            dg_ref[...] = dg_part
            l_ref[...] = l_part

        @pl.when(i > 0)
        def _():
            dg_ref[...] += dg_part
            l_ref[...] += l_part

    row = pl.BlockSpec((tm, D), lambda i: (i, 0))
    vec = pl.BlockSpec((1, D), lambda i: (0, 0))
    return pl.pallas_call(
        body, name=name, grid=(S // tm,),
        in_specs=[row, vec, row],
        out_specs=[row, vec, vec],
        out_shape=[jax.ShapeDtypeStruct((S, D), F32), jax.ShapeDtypeStruct((1, D), F32),
                   jax.ShapeDtypeStruct((1, D), F32)],
        compiler_params=_params(("arbitrary",)),
    )(h, g, target)


def _mm_rows(name, pairs, n_out, *, nt, tm, nc, epi="plain", extra=None, out_dtype=F32):
    M = pairs[0][0].shape[0]
    np_ = len(pairs)

    def body(*refs):
        a_refs = refs[:np_]
        w_refs = refs[np_:2 * np_]
        pos = 2 * np_
        e_ref = None
        if extra is not None:
            e_ref = refs[pos]
            pos += 1
        outs = refs[pos:]
        a_vals = [_bf(a[...]) for a in a_refs]
        for j in range(n_out // nc):
            cols = slice(j * nc, (j + 1) * nc)
            acc = None
            for a_v, w_ref in zip(a_vals, w_refs):
                w_v = w_ref[cols, :] if nt else w_ref[:, cols]
                t = _dot(a_v, w_v, NT if nt else NN)
                acc = t if acc is None else acc + t
            if epi == "plain":
                outs[0][:, cols] = acc.astype(out_dtype)
            elif epi == "res":
                outs[0][:, cols] = e_ref[:, cols] + acc
            elif epi == "relu2":
                outs[0][:, cols] = acc
                rl = jnp.maximum(acc, 0.0)
                outs[1][:, cols] = _bf(rl * rl)
            elif epi == "mask2relu":
                outs[0][:, cols] = _bf(acc * (2.0 * jnp.maximum(e_ref[:, cols], 0.0)))

    in_specs = [pl.BlockSpec((tm, a.shape[1]), lambda i: (i, 0)) for a, _, _, _ in pairs]
    for _, _, wshape, widx in pairs:
        in_specs.append(pl.BlockSpec(wshape, functools.partial(lambda i, widx: widx, widx=widx)))
    args = [a for a, _, _, _ in pairs] + [w for _, w, _, _ in pairs]
    if extra is not None:
        in_specs.append(pl.BlockSpec((tm, n_out), lambda i: (i, 0)))
        args.append(extra)
    row_out = pl.BlockSpec((tm, n_out), lambda i: (i, 0))
    if epi == "relu2":
        out_specs = [row_out, row_out]
        out_shape = [jax.ShapeDtypeStruct((M, n_out), F32), jax.ShapeDtypeStruct((M, n_out), BF16)]
    else:
        dt = BF16 if epi == "mask2relu" else (F32 if epi == "res" else out_dtype)
        out_specs = row_out
        out_shape = jax.ShapeDtypeStruct((M, n_out), dt)
    return pl.pallas_call(
        body, name=name, grid=(M // tm,), in_specs=in_specs, out_specs=out_specs, out_shape=out_shape,
        compiler_params=_params(("parallel",)),
    )(*args)


def _full(w):
    return (w, w.shape, (0, 0))


def _mm_nn(name, a, w, **kw):
    return _mm_rows(name, [(a, w, w.shape, (0, 0))], w.shape[1], nt=False, **kw)


def _mm_nt(name, a, w, **kw):
    return _mm_rows(name, [(a, w, w.shape, (0, 0))], w.shape[0], nt=True, **kw)


def _mm_tn(name, a, b, *, t1, tn, tm=512):
    M, K1 = a.shape
    N = b.shape[1]
    nm = M // tm

    def body(a_ref, b_ref, o_ref, acc_ref):
        m = pl.program_id(2)
        t = _dot(_bf(a_ref[...]), _bf(b_ref[...]), TN)

        @pl.when(m == 0)
        def _():
            acc_ref[...] = t

        @pl.when(m > 0)
        def _():
            acc_ref[...] += t

        @pl.when(m == nm - 1)
        def _():
            o_ref[...] = acc_ref[...]

    return pl.pallas_call(
        body, name=name, grid=(K1 // t1, N // tn, nm),
        in_specs=[pl.BlockSpec((tm, t1), lambda i, j, m: (m, i)), pl.BlockSpec((tm, tn), lambda i, j, m: (m, j))],
        out_specs=pl.BlockSpec((t1, tn), lambda i, j, m: (i, j)),
        out_shape=jax.ShapeDtypeStruct((K1, N), F32),
        scratch_shapes=[pltpu.VMEM((t1, tn), F32)],
        compiler_params=_params(("parallel", "parallel", "arbitrary")),
    )(a, b)


_INV_SQRT2 = 1.0 / math.sqrt(2.0)
_INV_SQRT2PI = 1.0 / math.sqrt(2.0 * math.pi)


def _gelu(x):
    return 0.5 * x * (1.0 + lax.erf(x * _INV_SQRT2))


def _gelu_grad(x):
    return 0.5 * (1.0 + lax.erf(x * _INV_SQRT2)) + x * (_INV_SQRT2PI * jnp.exp(-0.5 * x * x))


def _layer_norm_parts(v):
    mu = jnp.mean(v, axis=-1, keepdims=True)
    xc = v - mu
    rs = lax.rsqrt(jnp.mean(xc * xc, axis=-1, keepdims=True) + EPS)
    return xc * rs, rs


def _gate_fwd(name, uvp, ln_g, ln_b, wm, bs_full, tr=512):
    S, W2 = uvp.shape
    W = W2 // 2
    gd = W // GROUPS

    def body(u_ref, v_ref, lg_ref, lb_ref, wm_ref, bs_ref, z_ref):
        u = _gelu(u_ref[...])
        vh, _ = _layer_norm_parts(_gelu(v_ref[...]))
        vn = _bf(vh * lg_ref[...] + lb_ref[...])
        for ci in range(tr // CHUNK):
            rows = slice(ci * CHUNK, (ci + 1) * CHUNK)
            for g in range(GROUPS):
                cols = slice(g * gd, (g + 1) * gd)
                mixed = _dot(wm_ref[g], vn[rows, cols], NN) + bs_ref[:, cols]
                z_ref[rows, cols] = _bf(u[rows, cols] * mixed)

    vec = pl.BlockSpec((1, W), lambda i: (0, 0))
    return pl.pallas_call(
        body, name=name, grid=(S // tr,),
        in_specs=[pl.BlockSpec((tr, W), lambda i: (i, 0)), pl.BlockSpec((tr, W), lambda i: (i, 1)), vec, vec,
                  pl.BlockSpec((GROUPS, CHUNK, CHUNK), lambda i: (0, 0, 0)),
                  pl.BlockSpec((CHUNK, W), lambda i: (0, 0))],
        out_specs=pl.BlockSpec((tr, W), lambda i: (i, 0)),
        out_shape=jax.ShapeDtypeStruct((S, W), BF16),
        compiler_params=_params(("parallel",)),
    )(uvp, uvp, ln_g, ln_b, wm, bs_full)


def _gate_bwd(name, uvp, dz, ln_g, ln_b, wm, bs_full, tr=256):
    S, W2 = uvp.shape
    W = W2 // 2
    gd = W // GROUPS
    n_steps = S // tr

    def body(u_ref, v_ref, dz_ref, lg_ref, lb_ref, wm_ref, bs_ref, duv_ref, dwm_ref, dmx_ref, dlg_ref, dlb_ref,
             dvn_ref):
        i = pl.program_id(0)
        up = u_ref[...]
        vp = v_ref[...]
        u = _gelu(up)
        vh, rs = _layer_norm_parts(_gelu(vp))
        lg = lg_ref[...]
        vn = _bf(vh * lg + lb_ref[...])
        dz_v = dz_ref[...]
        dmixed = dz_v * u
        dmixed_b = _bf(dmixed)

        @pl.when(i == 0)
        def _():
            dwm_ref[...] = jnp.zeros_like(dwm_ref)
            dmx_ref[...] = jnp.zeros_like(dmx_ref)
            dlg_ref[...] = jnp.zeros_like(dlg_ref)
            dlb_ref[...] = jnp.zeros_like(dlb_ref)

        for ci in range(tr // CHUNK):
            rows = slice(ci * CHUNK, (ci + 1) * CHUNK)
            dmx_ref[...] += dmixed[rows, :]
            for g in range(GROUPS):
                cols = slice(g * gd, (g + 1) * gd)
                mixed = _dot(wm_ref[g], vn[rows, cols], NN) + bs_ref[:, cols]
                duv_ref[rows, cols] = _bf(dz_v[rows, cols] * mixed * _gelu_grad(up[rows, cols]))
                dwm_ref[g] += _dot(dmixed_b[rows, cols], vn[rows, cols], NT)
                dvn_ref[rows, cols] = _dot(wm_ref[g], dmixed_b[rows, cols], TN)
        dvn = dvn_ref[...]
        dlg_ref[...] += jnp.sum(dvn * vh, axis=0, keepdims=True)
        dlb_ref[...] += jnp.sum(dvn, axis=0, keepdims=True)
        dvh = dvn * lg
        dv = rs * (dvh - jnp.mean(dvh, axis=-1, keepdims=True) - vh * jnp.mean(dvh * vh, axis=-1, keepdims=True))
        duv_ref[:, W:] = _bf(dv * _gelu_grad(vp))

        @pl.when(i == n_steps - 1)
        def _():
            t_idx = lax.broadcasted_iota(jnp.int32, (CHUNK, CHUNK), 0)
            s_idx = lax.broadcasted_iota(jnp.int32, (CHUNK, CHUNK), 1)
            keep = (s_idx <= t_idx).astype(F32)
            for g in range(GROUPS):
                dwm_ref[g] = dwm_ref[g] * keep

    vec = pl.BlockSpec((1, W), lambda i: (0, 0))
    row = pl.BlockSpec((tr, W), lambda i: (i, 0))
    return pl.pallas_call(
        body, name=name, grid=(n_steps,),
        in_specs=[row, pl.BlockSpec((tr, W), lambda i: (i, 1)), row, vec, vec,
                  pl.BlockSpec((GROUPS, CHUNK, CHUNK), lambda i: (0, 0, 0)),
                  pl.BlockSpec((CHUNK, W), lambda i: (0, 0))],
        out_specs=[pl.BlockSpec((tr, W2), lambda i: (i, 0)),
                   pl.BlockSpec((GROUPS, CHUNK, CHUNK), lambda i: (0, 0, 0)),
                   pl.BlockSpec((CHUNK, W), lambda i: (0, 0)), vec, vec],
        out_shape=[jax.ShapeDtypeStruct((S, W2), BF16), jax.ShapeDtypeStruct((GROUPS, CHUNK, CHUNK), F32),
                   jax.ShapeDtypeStruct((CHUNK, W), F32), jax.ShapeDtypeStruct((1, W), F32),
                   jax.ShapeDtypeStruct((1, W), F32)],
        scratch_shapes=[pltpu.VMEM((tr, W), F32)],
        compiler_params=_params(("arbitrary",)),
    )(uvp, uvp, dz, ln_g, ln_b, wm, bs_full)


def _t5_bucket(distance):
    small = distance < MAX_EXACT
    nf = jnp.maximum(distance, 1).astype(F32)
    large = MAX_EXACT + (jnp.log(nf / MAX_EXACT) / math.log(REL_MAX_DISTANCE / MAX_EXACT)
                         * (N_BUCKETS - MAX_EXACT)).astype(jnp.int32)
    large = jnp.minimum(large, N_BUCKETS - 1)
    return jnp.where(small, distance, large)


TILE_ELEMS = 2 * CHUNK * CHUNK


def _band_buckets():
    rel = CHUNK + jnp.arange(CHUNK)[None, :] - jnp.arange(2 * CHUNK)[:, None]
    band = (rel >= 0) & (rel <= CHUNK)
    buckets = [_t5_bucket(jnp.clip(rel, 0, CHUNK) * d) for d in DILATIONS]
    return jnp.stack(buckets), band


def _bucket_onehot():
    buckets, _ = _band_buckets()
    return (buckets.reshape(N_DIL, 1, TILE_ELEMS) == jnp.arange(N_BUCKETS)[None, :, None]).astype(F32)


def _bias_tiles(name, rel_bias):
    _, band = _band_buckets()
    own = band & (jnp.arange(2 * CHUNK) >= CHUNK)[:, None]
    masks = jnp.stack([own, band]).reshape(2, TILE_ELEMS).astype(F32)
    tables = jnp.transpose(rel_bias.reshape(N_BUCKETS, N_DIL, ATT_HEADS), (1, 2, 0))

    def body(t_ref, oh_ref, m_ref, out_ref):
        for g in range(N_DIL):
            bias = lax.dot_general(t_ref[g], oh_ref[g], (NN, ((), ())), precision=lax.Precision.HIGHEST,
                                   preferred_element_type=F32)
            for f in range(2):
                out_ref[g, f] = jnp.where(m_ref[f:f + 1, :] > 0.5, bias, NEG_INF)

    out = pl.pallas_call(
        body, name=name, out_shape=jax.ShapeDtypeStruct((N_DIL, 2, ATT_HEADS, TILE_ELEMS), F32),
        compiler_params=_params(),
    )(tables, _bucket_onehot(), masks)
    return out.reshape(N_DIL, 2, ATT_HEADS, 2 * CHUNK, CHUNK)


def _att_specs(order):
    def spec(part, prev):
        def index(*ids):
            hp, g, c = order(*ids)
            return (jnp.maximum(c - 1, 0) if prev else c, part * 3 * 4 + g * 4 + hp)
        return pl.BlockSpec((ATT_ROWS, LANES), index)
    return [spec(0, False), spec(1, False), spec(1, True), spec(2, False), spec(2, True)]


def _rows(start, d):
    if d == 1:
        return pl.ds(pl.multiple_of(start, CHUNK), CHUNK)
    return pl.ds(start, CHUNK, stride=d)


def _att_tile_offsets(t, d):
    n = t // d
    r = t % d
    return n * (CHUNK * d) + r, n


def _stage_prev_cur(dst, prev_ref, cur_ref):
    dst[0:ATT_ROWS, :] = prev_ref[...]
    dst[ATT_ROWS:2 * ATT_ROWS, :] = cur_ref[...]


def _att_fwd(name, qkv, bias_tiles):
    S = qkv.shape[0]
    n_chunks = S // ATT_ROWS
    tiles = ATT_ROWS // CHUNK

    def body(q_ref, kc_ref, kp_ref, vc_ref, vp_ref, b_ref, o_ref, l_ref, kk, vv):
        c = pl.program_id(1)
        g = pl.program_id(2)
        _stage_prev_cur(kk, kp_ref, kc_ref)
        _stage_prev_cur(vv, vp_ref, vc_ref)
        zeros = jnp.zeros((HEAD_DIM, CHUNK), BF16)

        for gi, d in enumerate(DILATIONS):
            @pl.when(g == gi)
            def _(d=d):
                span = CHUNK * d

                def tile(t, carry):
                    q0, n = _att_tile_offsets(t, d)
                    rows = _rows(q0, d)
                    cur = _rows(ATT_ROWS + q0, d)
                    prev = _rows(ATT_ROWS + q0 - span, d)
                    inner = jnp.where((c == 0) & (n == 0), 0, 1)
                    q_t = _bf(q_ref[rows, :] * ATT_SCALE).T
                    k2 = _bf(jnp.concatenate([kk[prev, :], kk[cur, :]], axis=0))
                    v_t = _bf(jnp.concatenate([vv[prev, :], vv[cur, :]], axis=0)).T
                    o_parts, l_parts = [], []
                    for hh in range(2):
                        half = slice(hh * HEAD_DIM, (hh + 1) * HEAD_DIM)
                        q_h = jnp.concatenate([q_t[half], zeros] if hh == 0 else [zeros, q_t[half]], axis=0)
                        s = _dot(k2, q_h, NN) + b_ref[inner, hh]
                        m = jnp.max(s, axis=0, keepdims=True)
                        p = jnp.exp(s - m)
                        l = jnp.sum(p, axis=0, keepdims=True)
                        o_parts.append(_dot(v_t, _bf(p), NN)[half] / l)
                        l_parts.append(jnp.broadcast_to(m + jnp.log(l), (HEAD_DIM, CHUNK)))
                    o_ref[rows, :] = jnp.concatenate(o_parts, axis=0).T
                    l_ref[rows, :] = jnp.concatenate(l_parts, axis=0).T
                    return carry

                lax.fori_loop(0, tiles, tile, 0, unroll=2)

    order = lambda hp, c, g: (hp, g, c)
    out_spec = pl.BlockSpec((None, ATT_ROWS, LANES), lambda hp, c, g: (g, c, hp))
    shape = jax.ShapeDtypeStruct((N_DIL, S, ATT_WIDTH), F32)
    return pl.pallas_call(
        body, name=name, grid=(ATT_HEADS // 2, n_chunks, N_DIL),
        in_specs=_att_specs(order) + [
            pl.BlockSpec((None, 2, 2, 2 * CHUNK, CHUNK), lambda hp, c, g: (g, 0, hp, 0, 0))],
        out_specs=[out_spec, out_spec],
        out_shape=[shape, shape],
        scratch_shapes=[pltpu.VMEM((2 * ATT_ROWS, LANES), F32), pltpu.VMEM((2 * ATT_ROWS, LANES), F32)],
        compiler_params=_params(("parallel", "parallel", "parallel")),
    )(qkv, qkv, qkv, qkv, qkv, bias_tiles)


def _att_merge(name, o_g, l_g, tm=512):
    _, S, W = o_g.shape

    def body(o_ref, l_ref, out_ref, lse_ref):
        ls = [l_ref[g] for g in range(N_DIL)]
        mx = functools.reduce(jnp.maximum, ls)
        ws = [jnp.exp(l - mx) for l in ls]
        tot = functools.reduce(lambda a, b: a + b, ws)
        acc = ws[0] * o_ref[0]
        for g in range(1, N_DIL):
            acc = acc + ws[g] * o_ref[g]
        out_ref[...] = acc / tot
        lse_ref[...] = mx + jnp.log(tot)

    blk = pl.BlockSpec((N_DIL, tm, W), lambda i: (0, i, 0))
    row = pl.BlockSpec((tm, W), lambda i: (i, 0))
    shape = jax.ShapeDtypeStruct((S, W), F32)
    return pl.pallas_call(
        body, name=name, grid=(S // tm,), in_specs=[blk, blk], out_specs=[row, row], out_shape=[shape, shape],
        compiler_params=_params(("parallel",)),
    )(o_g, l_g)


def _att_bwd(name, qkv, o, lse, d_o, bias_tiles):
    S = qkv.shape[0]
    n_chunks = S // ATT_ROWS
    tiles = ATT_ROWS // CHUNK

    def body(q_ref, kc_ref, kp_ref, vc_ref, vp_ref, o_ref, l_ref, do_ref, b_ref, dq_ref, dk_ref, dv_ref, ds_ref,
             kk, vv):
        g = pl.program_id(1)
        c = pl.program_id(2)

        @pl.when(c == 0)
        def _():
            dk_ref[...] = jnp.zeros_like(dk_ref)
            dv_ref[...] = jnp.zeros_like(dv_ref)
            ds_ref[...] = jnp.zeros_like(ds_ref)

        _stage_prev_cur(kk, kp_ref, kc_ref)
        _stage_prev_cur(vv, vp_ref, vc_ref)
        base = c * ATT_ROWS
        zeros = jnp.zeros((HEAD_DIM, CHUNK), BF16)
        lane = lax.broadcasted_iota(jnp.int32, (CHUNK, LANES), 1)

        for gi, d in enumerate(DILATIONS):
            @pl.when(g == gi)
            def _(d=d):
                span = CHUNK * d

                def tile(t, carry):
                    q0, n = _att_tile_offsets(t, d)
                    rows = _rows(q0, d)
                    cur = _rows(ATT_ROWS + q0, d)
                    prev = _rows(ATT_ROWS + q0 - span, d)
                    first = (c == 0) & (n == 0)
                    inner = jnp.where(first, 0, 1)
                    g_cur = _rows(base + q0, d)
                    g_prev = _rows(jnp.where(first, q0, base + q0 - span), d)
                    q2 = _bf(q_ref[rows, :] * ATT_SCALE)
                    q_t = q2.T
                    k2 = _bf(jnp.concatenate([kk[prev, :], kk[cur, :]], axis=0))
                    k_t = k2.T
                    v2 = _bf(jnp.concatenate([vv[prev, :], vv[cur, :]], axis=0))
                    do2 = do_ref[rows, :]
                    do_b = _bf(do2)
                    do_t = do_b.T
                    lse_t = l_ref[rows, :].T
                    dd_t = (do2 * o_ref[rows, :]).T
                    dq_parts = []
                    dk2 = dv2 = None
                    for hh in range(2):
                        half = slice(hh * HEAD_DIM, (hh + 1) * HEAD_DIM)
                        mine = (lane < HEAD_DIM) if hh == 0 else (lane >= HEAD_DIM)
                        q_h = jnp.concatenate([q_t[half], zeros] if hh == 0 else [zeros, q_t[half]], axis=0)
                        do_h = jnp.concatenate([do_t[half], zeros] if hh == 0 else [zeros, do_t[half]], axis=0)
                        s = _dot(k2, q_h, NN) + b_ref[inner, hh]
                        p = jnp.exp(s - lse_t[hh * HEAD_DIM:hh * HEAD_DIM + 1])
                        delta = jnp.sum(dd_t[half], axis=0, keepdims=True)
                        ds = p * (_dot(v2, do_h, NN) - delta)
                        ds_ref[hh] += ds
                        ds_b = _bf(ds)
                        dq_parts.append(_dot(k_t, ds_b, NN)[half])
                        dk_h = _dot(ds_b, jnp.where(mine, q2, jnp.zeros_like(q2)), NN)
                        dv_h = _dot(_bf(p), jnp.where(mine, do_b, jnp.zeros_like(do_b)), NN)
                        dk2 = dk_h if dk2 is None else dk2 + dk_h
                        dv2 = dv_h if dv2 is None else dv2 + dv_h
                    dq_ref[rows, :] = (jnp.concatenate(dq_parts, axis=0) * ATT_SCALE).T
                    dk_ref[g_prev, :] += dk2[0:CHUNK]
                    dk_ref[g_cur, :] += dk2[CHUNK:2 * CHUNK]
                    dv_ref[g_prev, :] += dv2[0:CHUNK]
                    dv_ref[g_cur, :] += dv2[CHUNK:2 * CHUNK]
                    return carry

                lax.fori_loop(0, tiles, tile, 0, unroll=2)

    order = lambda hp, g, c: (hp, g, c)
    chunk = pl.BlockSpec((ATT_ROWS, LANES), lambda hp, g, c: (c, hp))
    slab = pl.BlockSpec((S, LANES), lambda hp, g, c: (0, g * 4 + hp))
    width = N_DIL * ATT_WIDTH
    return pl.pallas_call(
        body, name=name, grid=(ATT_HEADS // 2, N_DIL, n_chunks),
        in_specs=_att_specs(order) + [chunk, chunk, chunk,
                                      pl.BlockSpec((None, 2, 2, 2 * CHUNK, CHUNK),
                                                   lambda hp, g, c: (g, 0, hp, 0, 0))],
        out_specs=[pl.BlockSpec((ATT_ROWS, LANES), lambda hp, g, c: (c, g * 4 + hp)), slab, slab,
                   pl.BlockSpec((None, 2, 2 * CHUNK, CHUNK), lambda hp, g, c: (g, hp, 0, 0))],
        out_shape=[jax.ShapeDtypeStruct((S, width), F32), jax.ShapeDtypeStruct((S, width), F32),
                   jax.ShapeDtypeStruct((S, width), F32),
                   jax.ShapeDtypeStruct((N_DIL, ATT_HEADS, 2 * CHUNK, CHUNK), F32)],
        scratch_shapes=[pltpu.VMEM((2 * ATT_ROWS, LANES), F32), pltpu.VMEM((2 * ATT_ROWS, LANES), F32)],
        compiler_params=_params(("parallel", "parallel", "arbitrary")),
    )(qkv, qkv, qkv, qkv, qkv, o, lse, d_o, bias_tiles)


def _bias_grad(name, ds_sums):
    flat = ds_sums.reshape(N_DIL, ATT_HEADS, TILE_ELEMS)

    def body(oh_ref, ds_ref, out_ref):
        for g in range(N_DIL):
            out_ref[g] = lax.dot_general(oh_ref[g], ds_ref[g], (NT, ((), ())), precision=lax.Precision.HIGHEST,
                                         preferred_element_type=F32)

    out = pl.pallas_call(
        body, name=name, out_shape=jax.ShapeDtypeStruct((N_DIL, N_BUCKETS, ATT_HEADS), F32),
        compiler_params=_params(),
    )(_bucket_onehot(), flat)
    return jnp.transpose(out, (1, 0, 2)).reshape(N_BUCKETS, N_DIL * ATT_HEADS)


def _peers():
    x, y, c = lax.axis_index("x"), lax.axis_index("y"), lax.axis_index("c")
    me = 4 * x + 2 * y + c
    others = [(x, y, 1 - c), (1 - x, y, c), (x, 1 - y, c), (1 - x, 1 - y, c),
              (1 - x, y, 1 - c), (x, 1 - y, 1 - c), (1 - x, 1 - y, 1 - c)]
    return me, others


def _slot(dev):
    return 4 * dev[0] + 2 * dev[1] + dev[2]


def _all_gather(name, shard):
    R, W = shard.shape

    def body(x_ref, out_ref, send_sems, recv_sems, local_sem):
        me, others = _peers()
        mine = pltpu.make_async_copy(x_ref, out_ref.at[me], local_sem)
        mine.start()
        sends = [pltpu.make_async_remote_copy(src_ref=x_ref, dst_ref=out_ref.at[me], send_sem=send_sems.at[k],
                                              recv_sem=recv_sems.at[k], device_id=dev, device_id_type=MESH)
                 for k, dev in enumerate(others)]
        for cp in sends:
            cp.start()
        for k, dev in enumerate(others):
            pltpu.make_async_remote_copy(src_ref=x_ref, dst_ref=out_ref.at[_slot(dev)], send_sem=send_sems.at[k],
                                         recv_sem=recv_sems.at[k], device_id=dev, device_id_type=MESH).wait_recv()
        for cp in sends:
            cp.wait_send()
        mine.wait()

    return pl.pallas_call(
        body, name=name,
        in_specs=[pl.BlockSpec(memory_space=pl.ANY)],
        out_specs=pl.BlockSpec(memory_space=pl.ANY),
        out_shape=jax.ShapeDtypeStruct((N_DEV, R, W), shard.dtype),
        scratch_shapes=[pltpu.SemaphoreType.DMA((N_DEV - 1,)), pltpu.SemaphoreType.DMA((N_DEV - 1,)),
                        pltpu.SemaphoreType.DMA],
    )(shard)


def _scatter_parts(name, parts):
    _, R, W = parts.shape

    def body(x_ref, out_ref, send_sems, recv_sems, local_sem):
        me, others = _peers()
        mine = pltpu.make_async_copy(x_ref.at[me], out_ref.at[me], local_sem)
        mine.start()
        sends = [pltpu.make_async_remote_copy(src_ref=x_ref.at[_slot(dev)], dst_ref=out_ref.at[me],
                                              send_sem=send_sems.at[k], recv_sem=recv_sems.at[k],
                                              device_id=dev, device_id_type=MESH)
                 for k, dev in enumerate(others)]
        for cp in sends:
            cp.start()
        for k, dev in enumerate(others):
            pltpu.make_async_remote_copy(src_ref=x_ref.at[me], dst_ref=out_ref.at[_slot(dev)],
                                         send_sem=send_sems.at[k], recv_sem=recv_sems.at[k],
                                         device_id=dev, device_id_type=MESH).wait_recv()
        for cp in sends:
            cp.wait_send()
        mine.wait()

    return pl.pallas_call(
        body, name=name,
        in_specs=[pl.BlockSpec(memory_space=pl.ANY)],
        out_specs=pl.BlockSpec(memory_space=pl.ANY),
        out_shape=jax.ShapeDtypeStruct(parts.shape, parts.dtype),
        scratch_shapes=[pltpu.SemaphoreType.DMA((N_DEV - 1,)), pltpu.SemaphoreType.DMA((N_DEV - 1,)),
                        pltpu.SemaphoreType.DMA],
    )(parts)


def _all_reduce_small(name, buf):
    rows = buf.shape[0]
    rb = rows // N_DEV

    def body(x_ref, out_ref, stage, send1, recv1, send2, recv2):
        me, others = _peers()

        def block(ref, k):
            return ref.at[pl.ds(k * rb, rb), :]

        first = [pltpu.make_async_remote_copy(src_ref=block(x_ref, _slot(dev)), dst_ref=stage.at[me],
                                              send_sem=send1.at[k], recv_sem=recv1.at[k], device_id=dev,
                                              device_id_type=MESH) for k, dev in enumerate(others)]
        for cp in first:
            cp.start()
        stage[me] = x_ref[pl.ds(pl.multiple_of(me * rb, 8), rb), :]
        for k, dev in enumerate(others):
            pltpu.make_async_remote_copy(src_ref=block(x_ref, me), dst_ref=stage.at[_slot(dev)],
                                         send_sem=send1.at[k], recv_sem=recv1.at[k], device_id=dev,
                                         device_id_type=MESH).wait_recv()
        total = stage[0]
        for j in range(1, N_DEV):
            total = total + stage[j]
        out_ref[pl.ds(pl.multiple_of(me * rb, 8), rb), :] = total
        second = [pltpu.make_async_remote_copy(src_ref=block(out_ref, me), dst_ref=block(out_ref, me),
                                               send_sem=send2.at[k], recv_sem=recv2.at[k], device_id=dev,
                                               device_id_type=MESH) for k, dev in enumerate(others)]
        for cp in second:
            cp.start()
        for k, dev in enumerate(others):
            pltpu.make_async_remote_copy(src_ref=block(out_ref, me), dst_ref=block(out_ref, _slot(dev)),
                                         send_sem=send2.at[k], recv_sem=recv2.at[k], device_id=dev,
                                         device_id_type=MESH).wait_recv()
        for cp in first + second:
            cp.wait_send()

    sems = pltpu.SemaphoreType.DMA((N_DEV - 1,))
    return pl.pallas_call(
        body, name=name,
        in_specs=[pl.BlockSpec(memory_space=pltpu.VMEM)],
        out_specs=pl.BlockSpec(memory_space=pltpu.VMEM),
        out_shape=jax.ShapeDtypeStruct(buf.shape, F32),
        scratch_shapes=[pltpu.VMEM((N_DEV, rb, LANES), F32), sems, sems, sems, sems],
        compiler_params=pltpu.CompilerParams(vmem_limit_bytes=VMEM_LIMIT_BYTES),
    )(buf)


def _adamw_math(w, g, m, v):
    m = ADAM_B1 * m + (1.0 - ADAM_B1) * g
    v = ADAM_B2 * v + (1.0 - ADAM_B2) * (g * g)
    m_hat = m / (1.0 - ADAM_B1 ** ADAM_STEP)
    v_hat = v / (1.0 - ADAM_B2 ** ADAM_STEP)
    delta = -ADAM_LR * (m_hat / (jnp.sqrt(v_hat) + ADAM_EPS) + ADAM_WD * w)
    return delta, m, v


def _adamw(name, parts, w, m, v, tr=256):
    P, R, W = parts.shape
    tr = min(tr, R)

    def body(p_ref, w_ref, m_ref, v_ref, g_out, d_out, m_out, v_out):
        g = p_ref[0].astype(F32)
        for j in range(1, P):
            g = g + p_ref[j].astype(F32)
        delta, m_new, v_new = _adamw_math(w_ref[...], g, m_ref[...], v_ref[...])
        g_out[...] = g
        d_out[...] = delta
        m_out[...] = m_new
        v_out[...] = v_new

    row = pl.BlockSpec((tr, W), lambda i: (i, 0))
    shape = jax.ShapeDtypeStruct((R, W), F32)
    return pl.pallas_call(
        body, name=name, grid=(R // tr,),
        in_specs=[pl.BlockSpec((P, tr, W), lambda i: (0, i, 0)), row, row, row],
        out_specs=[row, row, row, row],
        out_shape=[shape, shape, shape, shape],
        compiler_params=_params(("parallel",)),
    )(parts, w, m, v)


def _col_shards(full, n_local):
    K = full.shape[0]
    t = jnp.transpose(full.reshape(K, N_DEV, n_local), (1, 0, 2))
    return t.reshape(N_DEV, K * n_local // PACK_W, PACK_W)


def _from_col_shards(slots, K, n_local):
    t = slots.reshape(N_DEV, K, n_local)
    return jnp.transpose(t, (1, 0, 2)).reshape(K, N_DEV * n_local)


def _pack_rows(a):
    return a.reshape(-1, PACK_W)


_SMALL = ("mix_norm_g", "mlp_norm_g", "final_norm_g", "a_ln_g", "a_ln_b", "a_w_s", "a_b_s", "rel_bias")


def _pack_small(vals):
    pieces = []
    for n in _SMALL:
        flat = vals[n].reshape(-1)
        pad = (-flat.shape[0]) % (8 * LANES)
        pieces.append(jnp.pad(flat, (0, pad)).reshape(-1, LANES))
    rows = sum(p.shape[0] for p in pieces)
    tail = (-rows) % (8 * N_DEV)
    if tail:
        pieces.append(jnp.zeros((tail, LANES), F32))
    return jnp.concatenate(pieces, axis=0)


def _unpack_small(buf, like):
    out = {}
    r = 0
    for n in _SMALL:
        size = like[n].size
        nrows = -(-size // (8 * LANES)) * 8
        out[n] = buf[r:r + nrows].reshape(-1)[:size].reshape(like[n].shape)
        r += nrows
    return out


def kernel(x, mix_norm_g, mlp_norm_g, final_norm_g, a_w_in, a_ln_g, a_ln_b, a_w_s, a_b_s, a_w_out, b_w_qkv, b_w_out, rel_bias, w_up, w_down, loss_target, m_mix_norm_g, m_mlp_norm_g, m_final_norm_g, m_a_w_in, m_a_ln_g, m_a_ln_b, m_a_w_s, m_a_b_s, m_a_w_out, m_b_w_qkv, m_b_w_out, m_rel_bias, m_w_up, m_w_down, v_mix_norm_g, v_mlp_norm_g, v_final_norm_g, v_a_w_in, v_a_ln_g, v_a_ln_b, v_a_w_s, v_a_b_s, v_a_w_out, v_b_w_qkv, v_b_w_out, v_rel_bias, v_w_up, v_w_down):
    big = ("a_w_in", "a_w_out", "b_w_qkv", "b_w_out", "w_up", "w_down")
    w = dict(mix_norm_g=mix_norm_g, mlp_norm_g=mlp_norm_g, final_norm_g=final_norm_g, a_w_in=a_w_in, a_ln_g=a_ln_g,
             a_ln_b=a_ln_b, a_w_s=a_w_s, a_b_s=a_b_s, a_w_out=a_w_out, b_w_qkv=b_w_qkv, b_w_out=b_w_out,
             rel_bias=rel_bias, w_up=w_up, w_down=w_down)
    m = dict(mix_norm_g=m_mix_norm_g, mlp_norm_g=m_mlp_norm_g, final_norm_g=m_final_norm_g, a_w_in=m_a_w_in,
             a_ln_g=m_a_ln_g, a_ln_b=m_a_ln_b, a_w_s=m_a_w_s, a_b_s=m_a_b_s, a_w_out=m_a_w_out, b_w_qkv=m_b_w_qkv,
             b_w_out=m_b_w_out, rel_bias=m_rel_bias, w_up=m_w_up, w_down=m_w_down)
    v = dict(mix_norm_g=v_mix_norm_g, mlp_norm_g=v_mlp_norm_g, final_norm_g=v_final_norm_g, a_w_in=v_a_w_in,
             a_ln_g=v_a_ln_g, a_ln_b=v_a_ln_b, a_w_s=v_a_w_s, a_b_s=v_a_b_s, a_w_out=v_a_w_out, b_w_qkv=v_b_w_qkv,
             b_w_out=v_b_w_out, rel_bias=v_rel_bias, w_up=v_w_up, w_down=v_w_down)

    D = x.shape[-1]
    d_ff = w_down.shape[1] * N_DEV
    n_in = a_w_in.shape[2]
    n_qkv = b_w_qkv.shape[2]
    n_bo = b_w_out.shape[2]
    n_up = w_up.shape[2]

    def pack_big(t):
        return jnp.concatenate([_pack_rows(t[n]) for n in big], axis=0)

    gathered = _all_gather("gather_weights", _bf(pack_big(w)))
    sizes = [w[n].size // PACK_W for n in big]
    offs = [sum(sizes[:i]) for i in range(len(big))]
    seg = {n: gathered[:, o:o + s] for n, o, s in zip(big, offs, sizes)}
    win = _from_col_shards(seg["a_w_in"], D, n_in)
    wout = seg["a_w_out"].reshape(-1, D)
    wqkv = _from_col_shards(seg["b_w_qkv"], D, n_qkv)
    wo = _from_col_shards(seg["b_w_out"], ATT_WIDTH, n_bo)
    up_l = seg["w_up"].reshape(N_DEV, 2, D, n_up)
    wup = [jnp.transpose(up_l[:, l], (1, 0, 2)).reshape(D, d_ff) for l in range(2)]
    dn_l = seg["w_down"].reshape(N_DEV, 2, d_ff // N_DEV, D)
    wdn = [dn_l[:, l].reshape(d_ff, D) for l in range(2)]

    loss_local, grad_x, grads, small_g = _local_step(
        x[0], loss_target[0], mix_norm_g, mlp_norm_g, final_norm_g, a_ln_g, a_ln_b, a_w_s, a_b_s, rel_bias,
        win, wout, wqkv, wo, wup, wdn)

    parts = jnp.concatenate([
        _col_shards(grads["a_w_in"], n_in),
        grads["a_w_out"].reshape(N_DEV, -1, PACK_W),
        _col_shards(grads["b_w_qkv"], n_qkv),
        _col_shards(grads["b_w_out"], n_bo),
        jnp.transpose(grads["w_up"].reshape(2, D, N_DEV, n_up), (2, 0, 1, 3)).reshape(N_DEV, -1, PACK_W),
        jnp.transpose(grads["w_down"].reshape(2, N_DEV, d_ff // N_DEV, D), (1, 0, 2, 3)).reshape(N_DEV, -1, PACK_W),
    ], axis=1)
    received = _scatter_parts("scatter_grads", _bf(parts))
    g_big, d_big, m_big, v_big = _adamw("adamw_big", received, pack_big(w), pack_big(m), pack_big(v))

    reduced = _all_reduce_small("reduce_small", _pack_small(small_g))
    g_sm, d_sm, m_sm, v_sm = _adamw("adamw_small", reduced[None], _pack_small(w), _pack_small(m), _pack_small(v),
                                    tr=reduced.shape[0])

    def unpack_big(buf):
        return {n: buf[o:o + s].reshape(w[n].shape) for n, o, s in zip(big, offs, sizes)}

    outs = []
    for b_buf, s_buf in ((g_big, g_sm), (d_big, d_sm), (m_big, m_sm), (v_big, v_sm)):
        full = {**unpack_big(b_buf), **_unpack_small(s_buf, w)}
        outs.extend(full[n] for n in w)
    loss = lax.psum(loss_local, ("x", "y", "c"))
    return (loss, grad_x[None], *outs)


def _local_step(xs, tgt, mix_norm_g, mlp_norm_g, final_norm_g, a_ln_g, a_ln_b, a_w_s, a_b_s, rel_bias,
                win, wout, wqkv, wo, wup, wdn):
    D = xs.shape[-1]
    g_mix = [mix_norm_g[l][None, :] for l in range(2)]
    g_mlp = [mlp_norm_g[l][None, :] for l in range(2)]
    g_fin = final_norm_g[None, :]
    ln_g, ln_b = a_ln_g, a_ln_b
    causal = jnp.tril(jnp.ones((CHUNK, CHUNK), dtype=bool))
    wm = _bf(jnp.where(causal[None], a_w_s[0], 0.0))
    bs_full = jnp.repeat(a_b_s[0].T, D // GROUPS, axis=1)
    bias_tiles = _bias_tiles("att_bias", rel_bias)

    y0 = _rms_fwd("rms_mix0", xs, g_mix[0])
    uvp = _mm_nn("gate_in", y0, win, tm=512, nc=512)
    z = _gate_fwd("gate_mid", uvp, ln_g, ln_b, wm, bs_full)
    h1 = _mm_nn("gate_out", z, wout, tm=512, nc=512, epi="res", extra=xs)
    y1 = _rms_fwd("rms_mlp0", h1, g_mlp[0])
    a0, f0 = _mm_nn("mlp0_up", y1, wup[0], tm=256, nc=512, epi="relu2")
    h2 = _mm_nn("mlp0_down", f0, wdn[0], tm=512, nc=512, epi="res", extra=h1)
    y2 = _rms_fwd("rms_mix1", h2, g_mix[1])
    qkv = _mm_nn("att_qkv", y2, wqkv, tm=256, nc=512)
    o_att, lse = _att_merge("att_merge", *_att_fwd("att_fwd", qkv, bias_tiles))
    h3 = _mm_nn("att_out", o_att, wo, tm=512, nc=512, epi="res", extra=h2)
    y3 = _rms_fwd("rms_mlp1", h3, g_mlp[1])
    a1, f1 = _mm_nn("mlp1_up", y3, wup[1], tm=256, nc=512, epi="relu2")
    h4 = _mm_nn("mlp1_down", f1, wdn[1], tm=512, nc=512, epi="res", extra=h3)
    dh, dg_fin, err2 = _final_loss("final_loss", h4, g_fin, tgt)
    loss_local = 0.5 * jnp.sum(err2) / D

    grads = {}

    def mlp_bwd(tag, dh, h_in, y, a, f, wup_l, wdn_l, g_row):
        da = _mm_nt(tag + "_dact", dh, wdn_l, tm=256, nc=512, epi="mask2relu", extra=a)
        g_dn = _mm_tn(tag + "_dwdown", f, dh, t1=1024, tn=1024)
        g_up = _mm_tn(tag + "_dwup", y, da, t1=1024, tn=1024)
        dy = _mm_nt(tag + "_dy", da, wup_l, tm=512, nc=512)
        dh_in, dg = _rms_bwd(tag + "_drms", h_in, g_row, dy, dh)
        return dh_in, g_up, g_dn, dg

    dh3, g_up1, g_dn1, dg_mlp1 = mlp_bwd("mlp1", dh, h3, y3, a1, f1, wup[1], wdn[1], g_mlp[1])

    d_o = _mm_nt("att_dout", dh3, wo, tm=512, nc=512)
    grads["b_w_out"] = _mm_tn("att_dwo", o_att, dh3, t1=512, tn=1024)
    dq, dk, dv, ds_sums = _att_bwd("att_bwd", qkv, o_att, lse, d_o, bias_tiles)
    grads["rel_bias"] = _bias_grad("att_dbias", ds_sums)
    part_w = N_DIL * ATT_WIDTH
    g_qkv = [_mm_tn("att_dwqkv%d" % p, y2, t, t1=1024, tn=part_w) for p, t in enumerate((dq, dk, dv))]
    grads["b_w_qkv"] = jnp.concatenate(g_qkv, axis=1)
    dy2 = _mm_rows("att_dy", [(t, wqkv, (D, part_w), (0, p)) for p, t in enumerate((dq, dk, dv))], D,
                   nt=True, tm=256, nc=512)
    dh2, dg_mix1 = _rms_bwd("att_drms", h2, g_mix[1], dy2, dh3)

    dh1, g_up0, g_dn0, dg_mlp0 = mlp_bwd("mlp0", dh2, h1, y1, a0, f0, wup[0], wdn[0], g_mlp[0])

    dz = _mm_nt("gate_dz", dh1, wout, tm=512, nc=512)
    grads["a_w_out"] = _mm_tn("gate_dwout", z, dh1, t1=1024, tn=1024)
    duvp, d_wm, d_mixed, d_lng, d_lnb = _gate_bwd("gate_dmid", uvp, dz, ln_g, ln_b, wm, bs_full)
    grads["a_w_in"] = _mm_tn("gate_dwin", y0, duvp, t1=1024, tn=1024)
    dy0 = _mm_nt("gate_dy", duvp, win, tm=512, nc=512)
    grad_x, dg_mix0 = _rms_bwd("gate_drms", xs, g_mix[0], dy0, dh1)

    grads["w_up"] = jnp.stack([g_up0, g_up1])
    grads["w_down"] = jnp.stack([g_dn0, g_dn1])
    small_g = dict(
        mix_norm_g=jnp.concatenate([dg_mix0, dg_mix1], axis=0),
        mlp_norm_g=jnp.concatenate([dg_mlp0, dg_mlp1], axis=0),
        final_norm_g=dg_fin[0], a_ln_g=d_lng, a_ln_b=d_lnb, a_w_s=d_wm[None],
        a_b_s=jnp.sum(d_mixed.reshape(CHUNK, GROUPS, D // GROUPS), axis=2).T[None],
        rel_bias=grads["rel_bias"])
    return loss_local, grad_x, grads, small_g
```

```python
import functools
import math

import jax
import jax.numpy as jnp
from jax import lax
from jax.experimental import pallas as pl
from jax.experimental.pallas import tpu as pltpu

F32 = jnp.float32
BF16 = jnp.bfloat16
MESH = pl.DeviceIdType.MESH

N_DEV = 8
EPS = 1e-6
NEG_INF = -1e30
CHUNK = 128
GROUPS = 8
HEAD_DIM = 64
ATT_HEADS = 8
ATT_WIDTH = ATT_HEADS * HEAD_DIM
DILATIONS = (1, 4, 16)
N_DIL = len(DILATIONS)
N_BUCKETS = 32
MAX_EXACT = N_BUCKETS // 2
REL_MAX_DISTANCE = 2048
ATT_ROWS = 2048
ATT_SCALE = HEAD_DIM ** -0.5
LANES = 128
PACK_W = 1024

ADAM_LR = 0.001
ADAM_B1 = 0.9
ADAM_B2 = 0.999
ADAM_EPS = 1e-08
ADAM_WD = 0.01
ADAM_STEP = 10

VMEM_LIMIT_BYTES = 56 * 1024 * 1024


def _params(semantics=None):
    return pltpu.CompilerParams(dimension_semantics=semantics, vmem_limit_bytes=VMEM_LIMIT_BYTES)


def _bf(v):
    return v.astype(BF16)


def _dot(a, b, dims):
    return lax.dot_general(a, b, (dims, ((), ())), preferred_element_type=F32)


NN = ((1,), (0,))
NT = ((1,), (1,))
TN = ((0,), (0,))


def _rms_fwd(name, x, g, tm=512):
    S, D = x.shape

    def body(x_ref, g_ref, y_ref):
        xv = x_ref[...]
        r = lax.rsqrt(jnp.mean(xv * xv, axis=-1, keepdims=True) + EPS)
        y_ref[...] = _bf(xv * r * g_ref[...])

    return pl.pallas_call(
        body, name=name, grid=(S // tm,),
        in_specs=[pl.BlockSpec((tm, D), lambda i: (i, 0)), pl.BlockSpec((1, D), lambda i: (0, 0))],
        out_specs=pl.BlockSpec((tm, D), lambda i: (i, 0)),
        out_shape=jax.ShapeDtypeStruct((S, D), BF16),
        compiler_params=_params(("parallel",)),
    )(x, g)


def _rms_bwd(name, x, g, dy, dres, tm=512):
    S, D = x.shape

    def body(x_ref, g_ref, dy_ref, dres_ref, dx_ref, dg_ref):
        i = pl.program_id(0)
        xv = x_ref[...]
        r = lax.rsqrt(jnp.mean(xv * xv, axis=-1, keepdims=True) + EPS)
        xh = xv * r
        dy_v = dy_ref[...]
        dyg = dy_v * g_ref[...]
        c = jnp.mean(dyg * xh, axis=-1, keepdims=True)
        dx_ref[...] = dres_ref[...] + r * (dyg - xh * c)
        part = jnp.sum(dy_v * xh, axis=0, keepdims=True)

        @pl.when(i == 0)
        def _():
            dg_ref[...] = part

        @pl.when(i > 0)
        def _():
            dg_ref[...] += part

    row = pl.BlockSpec((tm, D), lambda i: (i, 0))
    vec = pl.BlockSpec((1, D), lambda i: (0, 0))
    return pl.pallas_call(
        body, name=name, grid=(S // tm,),
        in_specs=[row, vec, row, row],
        out_specs=[row, vec],
        out_shape=[jax.ShapeDtypeStruct((S, D), F32), jax.ShapeDtypeStruct((1, D), F32)],
        compiler_params=_params(("arbitrary",)),
    )(x, g, dy, dres)


def _final_loss(name, h, g, target, tm=512):
    S, D = h.shape

    def body(h_ref, g_ref, t_ref, dh_ref, dg_ref, l_ref):
        i = pl.program_id(0)
        xv = h_ref[...]
        r = lax.rsqrt(jnp.mean(xv * xv, axis=-1, keepdims=True) + EPS)
        xh = xv * r
        gv = g_ref[...]
        e = xh * gv - t_ref[...]
        dout = e / D
        dyg = dout * gv
        c = jnp.mean(dyg * xh, axis=-1, keepdims=True)
        dh_ref[...] = r * (dyg - xh * c)
        dg_part = jnp.sum(dout * xh, axis=0, keepdims=True)
        l_part = jnp.sum(e * e, axis=0, keepdims=True)

        @pl.when(i == 0)
        def _():
            dg_ref[...] = dg_part
            l_ref[...] = l_part

        @pl.when(i > 0)
        def _():
            dg_ref[...] += dg_part
            l_ref[...] += l_part

    row = pl.BlockSpec((tm, D), lambda i: (i, 0))
    vec = pl.BlockSpec((1, D), lambda i: (0, 0))
    return pl.pallas_call(
        body, name=name, grid=(S // tm,),
        in_specs=[row, vec, row],
        out_specs=[row, vec, vec],
        out_shape=[jax.ShapeDtypeStruct((S, D), F32), jax.ShapeDtypeStruct((1, D), F32),
                   jax.ShapeDtypeStruct((1, D), F32)],
        compiler_params=_params(("arbitrary",)),
    )(h, g, target)


def _mm_rows(name, pairs, n_out, *, nt, tm, nc, epi="plain", extra=None, out_dtype=F32):
    M = pairs[0][0].shape[0]
    np_ = len(pairs)

    def body(*refs):
        a_refs = refs[:np_]
        w_refs = refs[np_:2 * np_]
        pos = 2 * np_
        e_ref = None
        if extra is not None:
            e_ref = refs[pos]
            pos += 1
        outs = refs[pos:]
        a_vals = [_bf(a[...]) for a in a_refs]
        for j in range(n_out // nc):
            cols = slice(j * nc, (j + 1) * nc)
            acc = None
            for a_v, w_ref in zip(a_vals, w_refs):
                w_v = w_ref[cols, :] if nt else w_ref[:, cols]
                t = _dot(a_v, w_v, NT if nt else NN)
                acc = t if acc is None else acc + t
            if epi == "plain":
                outs[0][:, cols] = acc.astype(out_dtype)
            elif epi == "res":
                outs[0][:, cols] = e_ref[:, cols] + acc
            elif epi == "relu2":
                outs[0][:, cols] = acc
                rl = jnp.maximum(acc, 0.0)
                outs[1][:, cols] = _bf(rl * rl)
            elif epi == "mask2relu":
                outs[0][:, cols] = _bf(acc * (2.0 * jnp.maximum(e_ref[:, cols], 0.0)))

    in_specs = [pl.BlockSpec((tm, a.shape[1]), lambda i: (i, 0)) for a, _, _, _ in pairs]
    for _, _, wshape, widx in pairs:
        in_specs.append(pl.BlockSpec(wshape, functools.partial(lambda i, widx: widx, widx=widx)))
    args = [a for a, _, _, _ in pairs] + [w for _, w, _, _ in pairs]
    if extra is not None:
        in_specs.append(pl.BlockSpec((tm, n_out), lambda i: (i, 0)))
        args.append(extra)
    row_out = pl.BlockSpec((tm, n_out), lambda i: (i, 0))
    if epi == "relu2":
        out_specs = [row_out, row_out]
        out_shape = [jax.ShapeDtypeStruct((M, n_out), F32), jax.ShapeDtypeStruct((M, n_out), BF16)]
    else:
        dt = BF16 if epi == "mask2relu" else (F32 if epi == "res" else out_dtype)
        out_specs = row_out
        out_shape = jax.ShapeDtypeStruct((M, n_out), dt)
    return pl.pallas_call(
        body, name=name, grid=(M // tm,), in_specs=in_specs, out_specs=out_specs, out_shape=out_shape,
        compiler_params=_params(("parallel",)),
    )(*args)


def _full(w):
    return (w, w.shape, (0, 0))


def _mm_nn(name, a, w, **kw):
    return _mm_rows(name, [(a, w, w.shape, (0, 0))], w.shape[1], nt=False, **kw)


def _mm_nt(name, a, w, **kw):
    return _mm_rows(name, [(a, w, w.shape, (0, 0))], w.shape[0], nt=True, **kw)


def _mm_tn(name, a, b, *, t1, tn, tm=512):
    M, K1 = a.shape
    N = b.shape[1]
    nm = M // tm

    def body(a_ref, b_ref, o_ref, acc_ref):
        m = pl.program_id(2)
        t = _dot(_bf(a_ref[...]), _bf(b_ref[...]), TN)

        @pl.when(m == 0)
        def _():
            acc_ref[...] = t

        @pl.when(m > 0)
        def _():
            acc_ref[...] += t

        @pl.when(m == nm - 1)
        def _():
            o_ref[...] = acc_ref[...]

    return pl.pallas_call(
        body, name=name, grid=(K1 // t1, N // tn, nm),
        in_specs=[pl.BlockSpec((tm, t1), lambda i, j, m: (m, i)), pl.BlockSpec((tm, tn), lambda i, j, m: (m, j))],
        out_specs=pl.BlockSpec((t1, tn), lambda i, j, m: (i, j)),
        out_shape=jax.ShapeDtypeStruct((K1, N), F32),
        scratch_shapes=[pltpu.VMEM((t1, tn), F32)],
        compiler_params=_params(("parallel", "parallel", "arbitrary")),
    )(a, b)


_INV_SQRT2 = 1.0 / math.sqrt(2.0)
_INV_SQRT2PI = 1.0 / math.sqrt(2.0 * math.pi)


def _gelu(x):
    return 0.5 * x * (1.0 + lax.erf(x * _INV_SQRT2))


def _gelu_grad(x):
    return 0.5 * (1.0 + lax.erf(x * _INV_SQRT2)) + x * (_INV_SQRT2PI * jnp.exp(-0.5 * x * x))


def _layer_norm_parts(v):
    mu = jnp.mean(v, axis=-1, keepdims=True)
    xc = v - mu
    rs = lax.rsqrt(jnp.mean(xc * xc, axis=-1, keepdims=True) + EPS)
    return xc * rs, rs


def _gate_fwd(name, uvp, ln_g, ln_b, wm, bs_full, tr=512):
    S, W2 = uvp.shape
    W = W2 // 2
    gd = W // GROUPS

    def body(u_ref, v_ref, lg_ref, lb_ref, wm_ref, bs_ref, z_ref):
        u = _gelu(u_ref[...])
        vh, _ = _layer_norm_parts(_gelu(v_ref[...]))
        vn = _bf(vh * lg_ref[...] + lb_ref[...])
        for ci in range(tr // CHUNK):
            rows = slice(ci * CHUNK, (ci + 1) * CHUNK)
            for g in range(GROUPS):
                cols = slice(g * gd, (g + 1) * gd)
                mixed = _dot(wm_ref[g], vn[rows, cols], NN) + bs_ref[:, cols]
                z_ref[rows, cols] = _bf(u[rows, cols] * mixed)

    vec = pl.BlockSpec((1, W), lambda i: (0, 0))
    return pl.pallas_call(
        body, name=name, grid=(S // tr,),
        in_specs=[pl.BlockSpec((tr, W), lambda i: (i, 0)), pl.BlockSpec((tr, W), lambda i: (i, 1)), vec, vec,
                  pl.BlockSpec((GROUPS, CHUNK, CHUNK), lambda i: (0, 0, 0)),
                  pl.BlockSpec((CHUNK, W), lambda i: (0, 0))],
        out_specs=pl.BlockSpec((tr, W), lambda i: (i, 0)),
        out_shape=jax.ShapeDtypeStruct((S, W), BF16),
        compiler_params=_params(("parallel",)),
    )(uvp, uvp, ln_g, ln_b, wm, bs_full)


def _gate_bwd(name, uvp, dz, ln_g, ln_b, wm, bs_full, tr=256):
    S, W2 = uvp.shape
    W = W2 // 2
    gd = W // GROUPS
    n_steps = S // tr

    def body(u_ref, v_ref, dz_ref, lg_ref, lb_ref, wm_ref, bs_ref, duv_ref, dwm_ref, dmx_ref, dlg_ref, dlb_ref,
             dvn_ref):
        i = pl.program_id(0)
        up = u_ref[...]
        vp = v_ref[...]
        u = _gelu(up)
        vh, rs = _layer_norm_parts(_gelu(vp))
        lg = lg_ref[...]
        vn = _bf(vh * lg + lb_ref[...])
        dz_v = dz_ref[...]
        dmixed = dz_v * u
        dmixed_b = _bf(dmixed)

        @pl.when(i == 0)
        def _():
            dwm_ref[...] = jnp.zeros_like(dwm_ref)
            dmx_ref[...] = jnp.zeros_like(dmx_ref)
            dlg_ref[...] = jnp.zeros_like(dlg_ref)
            dlb_ref[...] = jnp.zeros_like(dlb_ref)

        for ci in range(tr // CHUNK):
            rows = slice(ci * CHUNK, (ci + 1) * CHUNK)
            dmx_ref[...] += dmixed[rows, :]
            for g in range(GROUPS):
                cols = slice(g * gd, (g + 1) * gd)
                mixed = _dot(wm_ref[g], vn[rows, cols], NN) + bs_ref[:, cols]
                duv_ref[rows, cols] = _bf(dz_v[rows, cols] * mixed * _gelu_grad(up[rows, cols]))
                dwm_ref[g] += _dot(dmixed_b[rows, cols], vn[rows, cols], NT)
                dvn_ref[rows, cols] = _dot(wm_ref[g], dmixed_b[rows, cols], TN)
        dvn = dvn_ref[...]
        dlg_ref[...] += jnp.sum(dvn * vh, axis=0, keepdims=True)
        dlb_ref[...] += jnp.sum(dvn, axis=0, keepdims=True)
        dvh = dvn * lg
        dv = rs * (dvh - jnp.mean(dvh, axis=-1, keepdims=True) - vh * jnp.mean(dvh * vh, axis=-1, keepdims=True))
        duv_ref[:, W:] = _bf(dv * _gelu_grad(vp))

        @pl.when(i == n_steps - 1)
        def _():
            t_idx = lax.broadcasted_iota(jnp.int32, (CHUNK, CHUNK), 0)
            s_idx = lax.broadcasted_iota(jnp.int32, (CHUNK, CHUNK), 1)
            keep = (s_idx <= t_idx).astype(F32)
            for g in range(GROUPS):
                dwm_ref[g] = dwm_ref[g] * keep

    vec = pl.BlockSpec((1, W), lambda i: (0, 0))
    row = pl.BlockSpec((tr, W), lambda i: (i, 0))
    return pl.pallas_call(
        body, name=name, grid=(n_steps,),
        in_specs=[row, pl.BlockSpec((tr, W), lambda i: (i, 1)), row, vec, vec,
                  pl.BlockSpec((GROUPS, CHUNK, CHUNK), lambda i: (0, 0, 0)),
                  pl.BlockSpec((CHUNK, W), lambda i: (0, 0))],
        out_specs=[pl.BlockSpec((tr, W2), lambda i: (i, 0)),
                   pl.BlockSpec((GROUPS, CHUNK, CHUNK), lambda i: (0, 0, 0)),
                   pl.BlockSpec((CHUNK, W), lambda i: (0, 0)), vec, vec],
        out_shape=[jax.ShapeDtypeStruct((S, W2), BF16), jax.ShapeDtypeStruct((GROUPS, CHUNK, CHUNK), F32),
                   jax.ShapeDtypeStruct((CHUNK, W), F32), jax.ShapeDtypeStruct((1, W), F32),
                   jax.ShapeDtypeStruct((1, W), F32)],
        scratch_shapes=[pltpu.VMEM((tr, W), F32)],
        compiler_params=_params(("arbitrary",)),
    )(uvp, uvp, dz, ln_g, ln_b, wm, bs_full)


def _t5_bucket(distance):
    small = distance < MAX_EXACT
    nf = jnp.maximum(distance, 1).astype(F32)
    large = MAX_EXACT + (jnp.log(nf / MAX_EXACT) / math.log(REL_MAX_DISTANCE / MAX_EXACT)
                         * (N_BUCKETS - MAX_EXACT)).astype(jnp.int32)
    large = jnp.minimum(large, N_BUCKETS - 1)
    return jnp.where(small, distance, large)


TILE_ELEMS = 2 * CHUNK * CHUNK


def _band_buckets():
    rel = CHUNK + jnp.arange(CHUNK)[None, :] - jnp.arange(2 * CHUNK)[:, None]
    band = (rel >= 0) & (rel <= CHUNK)
    buckets = [_t5_bucket(jnp.clip(rel, 0, CHUNK) * d) for d in DILATIONS]
    return jnp.stack(buckets), band


def _bucket_onehot():
    buckets, _ = _band_buckets()
    return (buckets.reshape(N_DIL, 1, TILE_ELEMS) == jnp.arange(N_BUCKETS)[None, :, None]).astype(F32)


def _bias_tiles(name, rel_bias):
    _, band = _band_buckets()
    own = band & (jnp.arange(2 * CHUNK) >= CHUNK)[:, None]
    masks = jnp.stack([own, band]).reshape(2, TILE_ELEMS).astype(F32)
    tables = jnp.transpose(rel_bias.reshape(N_BUCKETS, N_DIL, ATT_HEADS), (1, 2, 0))

    def body(t_ref, oh_ref, m_ref, out_ref):
        for g in range(N_DIL):
            bias = lax.dot_general(t_ref[g], oh_ref[g], (NN, ((), ())), precision=lax.Precision.HIGHEST,
                                   preferred_element_type=F32)
            for f in range(2):
                out_ref[g, f] = jnp.where(m_ref[f:f + 1, :] > 0.5, bias, NEG_INF)

    out = pl.pallas_call(
        body, name=name, out_shape=jax.ShapeDtypeStruct((N_DIL, 2, ATT_HEADS, TILE_ELEMS), F32),
        compiler_params=_params(),
    )(tables, _bucket_onehot(), masks)
    return out.reshape(N_DIL, 2, ATT_HEADS, 2 * CHUNK, CHUNK)


def _att_specs(order):
    def spec(part, prev):
        def index(*ids):
            hp, g, c = order(*ids)
            return (jnp.maximum(c - 1, 0) if prev else c, part * 3 * 4 + g * 4 + hp)
        return pl.BlockSpec((ATT_ROWS, LANES), index)
    return [spec(0, False), spec(1, False), spec(1, True), spec(2, False), spec(2, True)]


def _rows(start, d):
    if d == 1:
        return pl.ds(pl.multiple_of(start, CHUNK), CHUNK)
    return pl.ds(start, CHUNK, stride=d)


def _att_tile_offsets(t, d):
    n = t // d
    r = t % d
    return n * (CHUNK * d) + r, n


def _stage_prev_cur(dst, prev_ref, cur_ref):
    dst[0:ATT_ROWS, :] = prev_ref[...]
    dst[ATT_ROWS:2 * ATT_ROWS, :] = cur_ref[...]


def _att_fwd(name, qkv, bias_tiles):
    S = qkv.shape[0]
    n_chunks = S // ATT_ROWS
    tiles = ATT_ROWS // CHUNK

    def body(q_ref, kc_ref, kp_ref, vc_ref, vp_ref, b_ref, o_ref, l_ref, kk, vv):
        c = pl.program_id(1)
        g = pl.program_id(2)
        _stage_prev_cur(kk, kp_ref, kc_ref)
        _stage_prev_cur(vv, vp_ref, vc_ref)
        zeros = jnp.zeros((HEAD_DIM, CHUNK), BF16)

        for gi, d in enumerate(DILATIONS):
            @pl.when(g == gi)
            def _(d=d):
                span = CHUNK * d

                def tile(t, carry):
                    q0, n = _att_tile_offsets(t, d)
                    rows = _rows(q0, d)
                    cur = _rows(ATT_ROWS + q0, d)
                    prev = _rows(ATT_ROWS + q0 - span, d)
                    inner = jnp.where((c == 0) & (n == 0), 0, 1)
                    q_t = _bf(q_ref[rows, :] * ATT_SCALE).T
                    k2 = _bf(jnp.concatenate([kk[prev, :], kk[cur, :]], axis=0))
                    v_t = _bf(jnp.concatenate([vv[prev, :], vv[cur, :]], axis=0)).T
                    o_parts, l_parts = [], []
                    for hh in range(2):
                        half = slice(hh * HEAD_DIM, (hh + 1) * HEAD_DIM)
                        q_h = jnp.concatenate([q_t[half], zeros] if hh == 0 else [zeros, q_t[half]], axis=0)
                        s = _dot(k2, q_h, NN) + b_ref[inner, hh]
                        m = jnp.max(s, axis=0, keepdims=True)
                        p = jnp.exp(s - m)
                        l = jnp.sum(p, axis=0, keepdims=True)
                        o_parts.append(_dot(v_t, _bf(p), NN)[half] / l)
                        l_parts.append(jnp.broadcast_to(m + jnp.log(l), (HEAD_DIM, CHUNK)))
                    o_ref[rows, :] = jnp.concatenate(o_parts, axis=0).T
                    l_ref[rows, :] = jnp.concatenate(l_parts, axis=0).T
                    return carry

                lax.fori_loop(0, tiles, tile, 0, unroll=2)

    order = lambda hp, c, g: (hp, g, c)
    out_spec = pl.BlockSpec((None, ATT_ROWS, LANES), lambda hp, c, g: (g, c, hp))
    shape = jax.ShapeDtypeStruct((N_DIL, S, ATT_WIDTH), F32)
    return pl.pallas_call(
        body, name=name, grid=(ATT_HEADS // 2, n_chunks, N_DIL),
        in_specs=_att_specs(order) + [
            pl.BlockSpec((None, 2, 2, 2 * CHUNK, CHUNK), lambda hp, c, g: (g, 0, hp, 0, 0))],
        out_specs=[out_spec, out_spec],
        out_shape=[shape, shape],
        scratch_shapes=[pltpu.VMEM((2 * ATT_ROWS, LANES), F32), pltpu.VMEM((2 * ATT_ROWS, LANES), F32)],
        compiler_params=_params(("parallel", "parallel", "parallel")),
    )(qkv, qkv, qkv, qkv, qkv, bias_tiles)


def _att_merge(name, o_g, l_g, tm=512):
    _, S, W = o_g.shape

    def body(o_ref, l_ref, out_ref, lse_ref):
        ls = [l_ref[g] for g in range(N_DIL)]
        mx = functools.reduce(jnp.maximum, ls)
        ws = [jnp.exp(l - mx) for l in ls]
        tot = functools.reduce(lambda a, b: a + b, ws)
        acc = ws[0] * o_ref[0]
        for g in range(1, N_DIL):
            acc = acc + ws[g] * o_ref[g]
        out_ref[...] = acc / tot
        lse_ref[...] = mx + jnp.log(tot)

    blk = pl.BlockSpec((N_DIL, tm, W), lambda i: (0, i, 0))
    row = pl.BlockSpec((tm, W), lambda i: (i, 0))
    shape = jax.ShapeDtypeStruct((S, W), F32)
    return pl.pallas_call(
        body, name=name, grid=(S // tm,), in_specs=[blk, blk], out_specs=[row, row], out_shape=[shape, shape],
        compiler_params=_params(("parallel",)),
    )(o_g, l_g)


def _att_bwd(name, qkv, o, lse, d_o, bias_tiles):
    S = qkv.shape[0]
    n_chunks = S // ATT_ROWS
    tiles = ATT_ROWS // CHUNK

    def body(q_ref, kc_ref, kp_ref, vc_ref, vp_ref, o_ref, l_ref, do_ref, b_ref, dq_ref, dk_ref, dv_ref, ds_ref,
             kk, vv):
        g = pl.program_id(1)
        c = pl.program_id(2)

        @pl.when(c == 0)
        def _():
            dk_ref[...] = jnp.zeros_like(dk_ref)
            dv_ref[...] = jnp.zeros_like(dv_ref)
            ds_ref[...] = jnp.zeros_like(ds_ref)

        _stage_prev_cur(kk, kp_ref, kc_ref)
        _stage_prev_cur(vv, vp_ref, vc_ref)
        base = c * ATT_ROWS
        zeros = jnp.zeros((HEAD_DIM, CHUNK), BF16)
        lane = lax.broadcasted_iota(jnp.int32, (CHUNK, LANES), 1)

        for gi, d in enumerate(DILATIONS):
            @pl.when(g == gi)
            def _(d=d):
                span = CHUNK * d

                def tile(t, carry):
                    q0, n = _att_tile_offsets(t, d)
                    rows = _rows(q0, d)
                    cur = _rows(ATT_ROWS + q0, d)
                    prev = _rows(ATT_ROWS + q0 - span, d)
                    first = (c == 0) & (n == 0)
                    inner = jnp.where(first, 0, 1)
                    g_cur = _rows(base + q0, d)
                    g_prev = _rows(jnp.where(first, q0, base + q0 - span), d)
                    q2 = _bf(q_ref[rows, :] * ATT_SCALE)
                    q_t = q2.T
                    k2 = _bf(jnp.concatenate([kk[prev, :], kk[cur, :]], axis=0))
                    k_t = k2.T
                    v2 = _bf(jnp.concatenate([vv[prev, :], vv[cur, :]], axis=0))
                    do2 = do_ref[rows, :]
                    do_b = _bf(do2)
                    do_t = do_b.T
                    lse_t = l_ref[rows, :].T
                    dd_t = (do2 * o_ref[rows, :]).T
                    dq_parts = []
                    dk2 = dv2 = None
                    for hh in range(2):
                        half = slice(hh * HEAD_DIM, (hh + 1) * HEAD_DIM)
                        mine = (lane < HEAD_DIM) if hh == 0 else (lane >= HEAD_DIM)
                        q_h = jnp.concatenate([q_t[half], zeros] if hh == 0 else [zeros, q_t[half]], axis=0)
                        do_h = jnp.concatenate([do_t[half], zeros] if hh == 0 else [zeros, do_t[half]], axis=0)
                        s = _dot(k2, q_h, NN) + b_ref[inner, hh]
                        p = jnp.exp(s - lse_t[hh * HEAD_DIM:hh * HEAD_DIM + 1])
                        delta = jnp.sum(dd_t[half], axis=0, keepdims=True)
                        ds = p * (_dot(v2, do_h, NN) - delta)
                        ds_ref[hh] += ds
                        ds_b = _bf(ds)
                        dq_parts.append(_dot(k_t, ds_b, NN)[half])
                        dk_h = _dot(ds_b, jnp.where(mine, q2, jnp.zeros_like(q2)), NN)
                        dv_h = _dot(_bf(p), jnp.where(mine, do_b, jnp.zeros_like(do_b)), NN)
                        dk2 = dk_h if dk2 is None else dk2 + dk_h
                        dv2 = dv_h if dv2 is None else dv2 + dv_h
                    dq_ref[rows, :] = (jnp.concatenate(dq_parts, axis=0) * ATT_SCALE).T
                    dk_ref[g_prev, :] += dk2[0:CHUNK]
                    dk_ref[g_cur, :] += dk2[CHUNK:2 * CHUNK]
                    dv_ref[g_prev, :] += dv2[0:CHUNK]
                    dv_ref[g_cur, :] += dv2[CHUNK:2 * CHUNK]
                    return carry

                lax.fori_loop(0, tiles, tile, 0, unroll=2)

    order = lambda hp, g, c: (hp, g, c)
    chunk = pl.BlockSpec((ATT_ROWS, LANES), lambda hp, g, c: (c, hp))
    slab = pl.BlockSpec((S, LANES), lambda hp, g, c: (0, g * 4 + hp))
    width = N_DIL * ATT_WIDTH
    return pl.pallas_call(
        body, name=name, grid=(ATT_HEADS // 2, N_DIL, n_chunks),
        in_specs=_att_specs(order) + [chunk, chunk, chunk,
                                      pl.BlockSpec((None, 2, 2, 2 * CHUNK, CHUNK),
                                                   lambda hp, g, c: (g, 0, hp, 0, 0))],
        out_specs=[pl.BlockSpec((ATT_ROWS, LANES), lambda hp, g, c: (c, g * 4 + hp)), slab, slab,
                   pl.BlockSpec((None, 2, 2 * CHUNK, CHUNK), lambda hp, g, c: (g, hp, 0, 0))],
        out_shape=[jax.ShapeDtypeStruct((S, width), F32), jax.ShapeDtypeStruct((S, width), F32),
                   jax.ShapeDtypeStruct((S, width), F32),
                   jax.ShapeDtypeStruct((N_DIL, ATT_HEADS, 2 * CHUNK, CHUNK), F32)],
        scratch_shapes=[pltpu.VMEM((2 * ATT_ROWS, LANES), F32), pltpu.VMEM((2 * ATT_ROWS, LANES), F32)],
        compiler_params=_params(("parallel", "parallel", "arbitrary")),
    )(qkv, qkv, qkv, qkv, qkv, o, lse, d_o, bias_tiles)


def _bias_grad(name, ds_sums):
    flat = ds_sums.reshape(N_DIL, ATT_HEADS, TILE_ELEMS)

    def body(oh_ref, ds_ref, out_ref):
        for g in range(N_DIL):
            out_ref[g] = lax.dot_general(oh_ref[g], ds_ref[g], (NT, ((), ())), precision=lax.Precision.HIGHEST,
                                         preferred_element_type=F32)

    out = pl.pallas_call(
        body, name=name, out_shape=jax.ShapeDtypeStruct((N_DIL, N_BUCKETS, ATT_HEADS), F32),
        compiler_params=_params(),
    )(_bucket_onehot(), flat)
    return jnp.transpose(out, (1, 0, 2)).reshape(N_BUCKETS, N_DIL * ATT_HEADS)


def _peers():
    x, y, c = lax.axis_index("x"), lax.axis_index("y"), lax.axis_index("c")
    me = 4 * x + 2 * y + c
    others = [(x, y, 1 - c), (1 - x, y, c), (x, 1 - y, c), (1 - x, 1 - y, c),
              (1 - x, y, 1 - c), (x, 1 - y, 1 - c), (1 - x, 1 - y, 1 - c)]
    return me, others


def _slot(dev):
    return 4 * dev[0] + 2 * dev[1] + dev[2]


def _all_gather(name, shard):
    R, W = shard.shape

    def body(x_ref, out_ref, send_sems, recv_sems, local_sem):
        me, others = _peers()
        mine = pltpu.make_async_copy(x_ref, out_ref.at[me], local_sem)
        mine.start()
        sends = [pltpu.make_async_remote_copy(src_ref=x_ref, dst_ref=out_ref.at[me], send_sem=send_sems.at[k],
                                              recv_sem=recv_sems.at[k], device_id=dev, device_id_type=MESH)
                 for k, dev in enumerate(others)]
        for cp in sends:
            cp.start()
        for k, dev in enumerate(others):
            pltpu.make_async_remote_copy(src_ref=x_ref, dst_ref=out_ref.at[_slot(dev)], send_sem=send_sems.at[k],
                                         recv_sem=recv_sems.at[k], device_id=dev, device_id_type=MESH).wait_recv()
        for cp in sends:
            cp.wait_send()
        mine.wait()

    return pl.pallas_call(
        body, name=name,
        in_specs=[pl.BlockSpec(memory_space=pl.ANY)],
        out_specs=pl.BlockSpec(memory_space=pl.ANY),
        out_shape=jax.ShapeDtypeStruct((N_DEV, R, W), shard.dtype),
        scratch_shapes=[pltpu.SemaphoreType.DMA((N_DEV - 1,)), pltpu.SemaphoreType.DMA((N_DEV - 1,)),
                        pltpu.SemaphoreType.DMA],
    )(shard)


_HBM = pl.BlockSpec(memory_space=pltpu.HBM)
_SEM = pl.BlockSpec(memory_space=pltpu.SEMAPHORE)
_EFFECT = pltpu.SideEffectType.DATAFLOW_SIDE_EFFECTING


def _my_slot():
    return 4 * lax.axis_index("x") + 2 * lax.axis_index("y") + lax.axis_index("c")


def _exchange_copy(src_ref, land_ref, send_sems, recv_sems, k, dev, me, scatter, arriving):
    src = src_ref.at[me if arriving else _slot(dev)] if scatter else src_ref
    dst = land_ref.at[_slot(dev) if arriving else me]
    return pltpu.make_async_remote_copy(src_ref=src, dst_ref=dst, send_sem=send_sems.at[k], recv_sem=recv_sems.at[k],
                                        device_id=dev, device_id_type=MESH)


def _exchange_start(name, src, scatter):
    R, W = src.shape[-2:]
    me = _my_slot()
    own = lax.dynamic_index_in_dim(src, me, 0, keepdims=True) if scatter else src[None]
    landing = lax.dynamic_update_slice(lax.empty((N_DEV, R, W), src.dtype), own, (me, 0, 0))

    def body(src_ref, land_ref, send_sems, recv_sems, src_thru, land_thru, token):
        me, others = _peers()
        for k, dev in enumerate(others):
            _exchange_copy(src_ref, land_ref, send_sems, recv_sems, k, dev, me, scatter, False).start()
        token[...] = jnp.zeros_like(token)

    sems = pltpu.SemaphoreType.DMA((N_DEV - 1,))
    send_sems, recv_sems, src_thru, land_thru, token = pl.pallas_call(
        body, name=name,
        out_shape=(sems, sems, pltpu.HBM(src.shape, src.dtype), pltpu.HBM(landing.shape, landing.dtype),
                   jax.ShapeDtypeStruct((8, LANES), F32)),
        in_specs=(_HBM, _HBM), out_specs=(_SEM, _SEM, _HBM, _HBM, pl.BlockSpec(memory_space=pltpu.VMEM)),
        input_output_aliases={0: 2, 1: 3},
        compiler_params=pltpu.CompilerParams(has_side_effects=_EFFECT),
    )(pltpu.with_memory_space_constraint(src, pltpu.HBM), pltpu.with_memory_space_constraint(landing, pltpu.HBM))
    return (send_sems, recv_sems, src_thru, land_thru, scatter), token


def _exchange_wait(name, handle, after):
    send_sems, recv_sems, src_thru, land_thru, scatter = handle

    def body(src_ref, land_ref, send_sems, recv_sems, after_ref, src_dead, got_ref):
        me, others = _peers()
        for k, dev in enumerate(others):
            cp = _exchange_copy(src_ref, land_ref, send_sems, recv_sems, k, dev, me, scatter, True)
            cp.wait_send()
            cp.wait_recv()

    return pl.pallas_call(
        body, name=name,
        out_shape=(pltpu.HBM(src_thru.shape, src_thru.dtype), pltpu.HBM(land_thru.shape, land_thru.dtype)),
        in_specs=(_HBM, _HBM, _SEM, _SEM, pl.BlockSpec(memory_space=pl.ANY)), out_specs=(_HBM, _HBM),
        input_output_aliases={0: 0, 1: 1},
        compiler_params=pltpu.CompilerParams(has_side_effects=_EFFECT),
    )(src_thru, land_thru, send_sems, recv_sems, after)[1]


def _all_reduce_small(name, buf):
    rows = buf.shape[0]
    rb = rows // N_DEV

    def body(x_ref, out_ref, stage, send1, recv1, send2, recv2):
        me, others = _peers()

        def block(ref, k):
            return ref.at[pl.ds(k * rb, rb), :]

        first = [pltpu.make_async_remote_copy(src_ref=block(x_ref, _slot(dev)), dst_ref=stage.at[me],
                                              send_sem=send1.at[k], recv_sem=recv1.at[k], device_id=dev,
                                              device_id_type=MESH) for k, dev in enumerate(others)]
        for cp in first:
            cp.start()
        stage[me] = x_ref[pl.ds(pl.multiple_of(me * rb, 8), rb), :]
        for k, dev in enumerate(others):
            pltpu.make_async_remote_copy(src_ref=block(x_ref, me), dst_ref=stage.at[_slot(dev)],
                                         send_sem=send1.at[k], recv_sem=recv1.at[k], device_id=dev,
                                         device_id_type=MESH).wait_recv()
        total = stage[0]
        for j in range(1, N_DEV):
            total = total + stage[j]
        out_ref[pl.ds(pl.multiple_of(me * rb, 8), rb), :] = total
        second = [pltpu.make_async_remote_copy(src_ref=block(out_ref, me), dst_ref=block(out_ref, me),
                                               send_sem=send2.at[k], recv_sem=recv2.at[k], device_id=dev,
                                               device_id_type=MESH) for k, dev in enumerate(others)]
        for cp in second:
            cp.start()
        for k, dev in enumerate(others):
            pltpu.make_async_remote_copy(src_ref=block(out_ref, me), dst_ref=block(out_ref, _slot(dev)),
                                         send_sem=send2.at[k], recv_sem=recv2.at[k], device_id=dev,
                                         device_id_type=MESH).wait_recv()
        for cp in first + second:
            cp.wait_send()

    sems = pltpu.SemaphoreType.DMA((N_DEV - 1,))
    return pl.pallas_call(
        body, name=name,
        in_specs=[pl.BlockSpec(memory_space=pltpu.VMEM)],
        out_specs=pl.BlockSpec(memory_space=pltpu.VMEM),
        out_shape=jax.ShapeDtypeStruct(buf.shape, F32),
        scratch_shapes=[pltpu.VMEM((N_DEV, rb, LANES), F32), sems, sems, sems, sems],
        compiler_params=pltpu.CompilerParams(vmem_limit_bytes=VMEM_LIMIT_BYTES),
    )(buf)


def _adamw_math(w, g, m, v):
    m = ADAM_B1 * m + (1.0 - ADAM_B1) * g
    v = ADAM_B2 * v + (1.0 - ADAM_B2) * (g * g)
    m_hat = m / (1.0 - ADAM_B1 ** ADAM_STEP)
    v_hat = v / (1.0 - ADAM_B2 ** ADAM_STEP)
    delta = -ADAM_LR * (m_hat / (jnp.sqrt(v_hat) + ADAM_EPS) + ADAM_WD * w)
    return delta, m, v


def _adamw(name, parts, w, m, v, tr=128):
    P, R, W = parts.shape
    tr = min(tr, R)

    def body(p_ref, w_ref, m_ref, v_ref, g_out, d_out, m_out, v_out):
        g = p_ref[0].astype(F32)
        for j in range(1, P):
            g = g + p_ref[j].astype(F32)
        delta, m_new, v_new = _adamw_math(w_ref[...], g, m_ref[...], v_ref[...])
        g_out[...] = g
        d_out[...] = delta
        m_out[...] = m_new
        v_out[...] = v_new

    row = pl.BlockSpec((tr, W), lambda i: (i, 0))
    shape = jax.ShapeDtypeStruct((R, W), F32)
    return pl.pallas_call(
        body, name=name, grid=(R // tr,),
        in_specs=[pl.BlockSpec((P, tr, W), lambda i: (0, i, 0)), row, row, row],
        out_specs=[row, row, row, row],
        out_shape=[shape, shape, shape, shape],
        compiler_params=_params(("parallel",)),
    )(parts, w, m, v)


def _col_shards(full, n_local):
    K = full.shape[0]
    t = jnp.transpose(full.reshape(K, N_DEV, n_local), (1, 0, 2))
    return t.reshape(N_DEV, K * n_local // PACK_W, PACK_W)


def _from_col_shards(slots, K, n_local):
    t = slots.reshape(N_DEV, K, n_local)
    return jnp.transpose(t, (1, 0, 2)).reshape(K, N_DEV * n_local)


def _pack_rows(a):
    return a.reshape(-1, PACK_W)


_SMALL = ("mix_norm_g", "mlp_norm_g", "final_norm_g", "a_ln_g", "a_ln_b", "a_w_s", "a_b_s", "rel_bias")


def _pack_small(vals):
    pieces = []
    for n in _SMALL:
        flat = vals[n].reshape(-1)
        pad = (-flat.shape[0]) % (8 * LANES)
        pieces.append(jnp.pad(flat, (0, pad)).reshape(-1, LANES))
    rows = sum(p.shape[0] for p in pieces)
    tail = (-rows) % (8 * N_DEV)
    if tail:
        pieces.append(jnp.zeros((tail, LANES), F32))
    return jnp.concatenate(pieces, axis=0)


def _unpack_small(buf, like):
    out = {}
    r = 0
    for n in _SMALL:
        size = like[n].size
        nrows = -(-size // (8 * LANES)) * 8
        out[n] = buf[r:r + nrows].reshape(-1)[:size].reshape(like[n].shape)
        r += nrows
    return out


_STAGES = (("gate", (("a_w_in", 0), ("a_w_out", 0))),
           ("mlp0", (("w_up", 0), ("w_down", 0))),
           ("att", (("b_w_qkv", 0), ("b_w_out", 0))),
           ("mlp1", (("w_up", 1), ("w_down", 1))))
_COL_SHARDED = ("a_w_in", "b_w_qkv", "b_w_out", "w_up")


def _pack_stage(t, pieces):
    return jnp.concatenate([_pack_rows(t[n][i]) for n, i in pieces], axis=0)


def kernel(x, mix_norm_g, mlp_norm_g, final_norm_g, a_w_in, a_ln_g, a_ln_b, a_w_s, a_b_s, a_w_out, b_w_qkv, b_w_out, rel_bias, w_up, w_down, loss_target, m_mix_norm_g, m_mlp_norm_g, m_final_norm_g, m_a_w_in, m_a_ln_g, m_a_ln_b, m_a_w_s, m_a_b_s, m_a_w_out, m_b_w_qkv, m_b_w_out, m_rel_bias, m_w_up, m_w_down, v_mix_norm_g, v_mlp_norm_g, v_final_norm_g, v_a_w_in, v_a_ln_g, v_a_ln_b, v_a_w_s, v_a_b_s, v_a_w_out, v_b_w_qkv, v_b_w_out, v_rel_bias, v_w_up, v_w_down):
    w = dict(mix_norm_g=mix_norm_g, mlp_norm_g=mlp_norm_g, final_norm_g=final_norm_g, a_w_in=a_w_in, a_ln_g=a_ln_g,
             a_ln_b=a_ln_b, a_w_s=a_w_s, a_b_s=a_b_s, a_w_out=a_w_out, b_w_qkv=b_w_qkv, b_w_out=b_w_out,
             rel_bias=rel_bias, w_up=w_up, w_down=w_down)
    m = dict(mix_norm_g=m_mix_norm_g, mlp_norm_g=m_mlp_norm_g, final_norm_g=m_final_norm_g, a_w_in=m_a_w_in,
             a_ln_g=m_a_ln_g, a_ln_b=m_a_ln_b, a_w_s=m_a_w_s, a_b_s=m_a_b_s, a_w_out=m_a_w_out, b_w_qkv=m_b_w_qkv,
             b_w_out=m_b_w_out, rel_bias=m_rel_bias, w_up=m_w_up, w_down=m_w_down)
    v = dict(mix_norm_g=v_mix_norm_g, mlp_norm_g=v_mlp_norm_g, final_norm_g=v_final_norm_g, a_w_in=v_a_w_in,
             a_ln_g=v_a_ln_g, a_ln_b=v_a_ln_b, a_w_s=v_a_w_s, a_b_s=v_a_b_s, a_w_out=v_a_w_out, b_w_qkv=v_b_w_qkv,
             b_w_out=v_b_w_out, rel_bias=v_rel_bias, w_up=v_w_up, w_down=v_w_down)

    stages = dict(_STAGES)
    order = [s for s, _ in _STAGES]

    def full_weights(pieces, gathered):
        out, r = [], 0
        for n, i in pieces:
            shard = w[n][i]
            rows = shard.size // PACK_W
            seg = gathered[:, r:r + rows]
            r += rows
            if n in _COL_SHARDED:
                out.append(_from_col_shards(seg, shard.shape[0], shard.shape[1]))
            else:
                out.append(seg.reshape(N_DEV * shard.shape[0], shard.shape[1]))
        return out

    pending = {}

    def get_weights(stage, dep):
        if stage == order[0]:
            gathered = _all_gather("gather_" + stage, _bf(_pack_stage(w, stages[stage])))
        else:
            gathered = _exchange_wait("gather_" + stage + "_wait", pending.pop(stage), dep)
        nxt = order.index(stage) + 1
        if nxt < len(order):
            shard = _bf(_pack_stage(w, stages[order[nxt]]))
            shard, gathered = lax.optimization_barrier((shard, gathered))
            pending[order[nxt]], token = _exchange_start("gather_" + order[nxt] + "_start", shard, False)
            dep, _ = lax.optimization_barrier((dep, token))
        return full_weights(stages[stage], gathered), dep

    sent = {}

    def put_grads(stage, grads, dep):
        parts = []
        for n, i in stages[stage]:
            shard = w[n][i]
            if n in _COL_SHARDED:
                parts.append(_col_shards(grads[n], shard.shape[1]))
            else:
                parts.append(grads[n].reshape(N_DEV, -1, PACK_W))
        sent[stage], token = _exchange_start("scatter_" + stage + "_start", _bf(jnp.concatenate(parts, axis=1)), True)
        dep, _ = lax.optimization_barrier((dep, token))
        return dep

    loss_local, grad_x, small_g = _local_step(
        x[0], loss_target[0], mix_norm_g, mlp_norm_g, final_norm_g, a_ln_g, a_ln_b, a_w_s, a_b_s, rel_bias,
        get_weights, put_grads)

    new = {}
    for stage in reversed(order):
        received = _exchange_wait("scatter_" + stage + "_wait", sent[stage], grad_x)
        bufs = _adamw("adamw_" + stage, received, *[_pack_stage(t, stages[stage]) for t in (w, m, v)])
        r = 0
        for n, i in stages[stage]:
            rows = w[n][i].size // PACK_W
            new[n, i] = [b[r:r + rows].reshape(w[n][i].shape) for b in bufs]
            r += rows

    reduced = _all_reduce_small("reduce_small", _pack_small(small_g))
    small = [_unpack_small(b, w) for b in _adamw("adamw_small", reduced[None], _pack_small(w), _pack_small(m),
                                                 _pack_small(v), tr=reduced.shape[0])]

    outs = []
    for j in range(4):
        for n in w:
            if n in _SMALL:
                outs.append(small[j][n])
            else:
                outs.append(jnp.stack([new[n, i][j] for i in range(w[n].shape[0])]))
    loss = lax.psum(loss_local, ("x", "y", "c"))
    return (loss, grad_x[None], *outs)


def _local_step(xs, tgt, mix_norm_g, mlp_norm_g, final_norm_g, a_ln_g, a_ln_b, a_w_s, a_b_s, rel_bias,
                get_weights, put_grads):
    D = xs.shape[-1]
    g_mix = [mix_norm_g[l][None, :] for l in range(2)]
    g_mlp = [mlp_norm_g[l][None, :] for l in range(2)]
    g_fin = final_norm_g[None, :]
    ln_g, ln_b = a_ln_g, a_ln_b
    causal = jnp.tril(jnp.ones((CHUNK, CHUNK), dtype=bool))
    wm = _bf(jnp.where(causal[None], a_w_s[0], 0.0))
    bs_full = jnp.repeat(a_b_s[0].T, D // GROUPS, axis=1)
    bias_tiles = _bias_tiles("att_bias", rel_bias)

    (win, wout), xs = get_weights("gate", xs)
    y0 = _rms_fwd("rms_mix0", xs, g_mix[0])
    uvp = _mm_nn("gate_in", y0, win, tm=512, nc=512)
    z = _gate_fwd("gate_mid", uvp, ln_g, ln_b, wm, bs_full)
    h1 = _mm_nn("gate_out", z, wout, tm=512, nc=512, epi="res", extra=xs)
    (wup0, wdn0), h1 = get_weights("mlp0", h1)
    y1 = _rms_fwd("rms_mlp0", h1, g_mlp[0])
    a0, f0 = _mm_nn("mlp0_up", y1, wup0, tm=256, nc=512, epi="relu2")
    h2 = _mm_nn("mlp0_down", f0, wdn0, tm=512, nc=512, epi="res", extra=h1)
    (wqkv, wo), h2 = get_weights("att", h2)
    y2 = _rms_fwd("rms_mix1", h2, g_mix[1])
    qkv = _mm_nn("att_qkv", y2, wqkv, tm=256, nc=512)
    o_att, lse = _att_merge("att_merge", *_att_fwd("att_fwd", qkv, bias_tiles))
    h3 = _mm_nn("att_out", o_att, wo, tm=512, nc=512, epi="res", extra=h2)
    (wup1, wdn1), h3 = get_weights("mlp1", h3)
    y3 = _rms_fwd("rms_mlp1", h3, g_mlp[1])
    a1, f1 = _mm_nn("mlp1_up", y3, wup1, tm=256, nc=512, epi="relu2")
    h4 = _mm_nn("mlp1_down", f1, wdn1, tm=512, nc=512, epi="res", extra=h3)
    dh, dg_fin, err2 = _final_loss("final_loss", h4, g_fin, tgt)
    loss_local = 0.5 * jnp.sum(err2) / D

    def mlp_bwd(tag, dh, h_in, y, a, f, wup_l, wdn_l, g_row):
        da = _mm_nt(tag + "_dact", dh, wdn_l, tm=256, nc=512, epi="mask2relu", extra=a)
        g_dn = _mm_tn(tag + "_dwdown", f, dh, t1=1024, tn=1024)
        g_up = _mm_tn(tag + "_dwup", y, da, t1=1024, tn=1024)
        dy = _mm_nt(tag + "_dy", da, wup_l, tm=512, nc=512)
        dh_in, dg = _rms_bwd(tag + "_drms", h_in, g_row, dy, dh)
        return put_grads(tag, dict(w_up=g_up, w_down=g_dn), dh_in), dg

    dh3, dg_mlp1 = mlp_bwd("mlp1", dh, h3, y3, a1, f1, wup1, wdn1, g_mlp[1])

    d_o = _mm_nt("att_dout", dh3, wo, tm=512, nc=512)
    g_wo = _mm_tn("att_dwo", o_att, dh3, t1=512, tn=1024)
    dq, dk, dv, ds_sums = _att_bwd("att_bwd", qkv, o_att, lse, d_o, bias_tiles)
    part_w = N_DIL * ATT_WIDTH
    g_qkv = [_mm_tn("att_dwqkv%d" % p, y2, t, t1=1024, tn=part_w) for p, t in enumerate((dq, dk, dv))]
    dy2 = _mm_rows("att_dy", [(t, wqkv, (D, part_w), (0, p)) for p, t in enumerate((dq, dk, dv))], D,
                   nt=True, tm=256, nc=512)
    dh2, dg_mix1 = _rms_bwd("att_drms", h2, g_mix[1], dy2, dh3)
    dh2 = put_grads("att", dict(b_w_qkv=jnp.concatenate(g_qkv, axis=1), b_w_out=g_wo), dh2)

    dh1, dg_mlp0 = mlp_bwd("mlp0", dh2, h1, y1, a0, f0, wup0, wdn0, g_mlp[0])

    dz = _mm_nt("gate_dz", dh1, wout, tm=512, nc=512)
    g_wout = _mm_tn("gate_dwout", z, dh1, t1=1024, tn=1024)
    duvp, d_wm, d_mixed, d_lng, d_lnb = _gate_bwd("gate_dmid", uvp, dz, ln_g, ln_b, wm, bs_full)
    g_win = _mm_tn("gate_dwin", y0, duvp, t1=1024, tn=1024)
    dy0 = _mm_nt("gate_dy", duvp, win, tm=512, nc=512)
    grad_x, dg_mix0 = _rms_bwd("gate_drms", xs, g_mix[0], dy0, dh1)
    grad_x = put_grads("gate", dict(a_w_in=g_win, a_w_out=g_wout), grad_x)

    small_g = dict(
        mix_norm_g=jnp.concatenate([dg_mix0, dg_mix1], axis=0),
        mlp_norm_g=jnp.concatenate([dg_mlp0, dg_mlp1], axis=0),
        final_norm_g=dg_fin[0], a_ln_g=d_lng, a_ln_b=d_lnb, a_w_s=d_wm[None],
        a_b_s=jnp.sum(d_mixed.reshape(CHUNK, GROUPS, D // GROUPS), axis=2).T[None],
        rel_bias=_bias_grad("att_dbias", ds_sums))
    return loss_local, grad_x, small_g
```

```python
import functools
import math

import jax
import jax.numpy as jnp
from jax import lax
from jax.experimental import pallas as pl
from jax.experimental.pallas import tpu as pltpu

F32 = jnp.float32
BF16 = jnp.bfloat16
MESH = pl.DeviceIdType.MESH

N_DEV = 8
EPS = 1e-6
NEG_INF = -1e30
CHUNK = 128
GROUPS = 8
HEAD_DIM = 64
ATT_HEADS = 8
ATT_WIDTH = ATT_HEADS * HEAD_DIM
DILATIONS = (1, 4, 16)
N_DIL = len(DILATIONS)
N_BUCKETS = 32
MAX_EXACT = N_BUCKETS // 2
REL_MAX_DISTANCE = 2048
ATT_ROWS = 2048
ATT_SCALE = HEAD_DIM ** -0.5
LANES = 128
PACK_W = 1024

ADAM_LR = 0.001
ADAM_B1 = 0.9
ADAM_B2 = 0.999
ADAM_EPS = 1e-08
ADAM_WD = 0.01
ADAM_STEP = 10

VMEM_LIMIT_BYTES = 56 * 1024 * 1024


def _params(semantics=None):
    return pltpu.CompilerParams(dimension_semantics=semantics, vmem_limit_bytes=VMEM_LIMIT_BYTES)


def _bf(v):
    return v.astype(BF16)


def _dot(a, b, dims):
    return lax.dot_general(a, b, (dims, ((), ())), preferred_element_type=F32)


NN = ((1,), (0,))
NT = ((1,), (1,))
TN = ((0,), (0,))


def _after_operand(after):
    if after is None:
        return [], []
    return [after], [pl.BlockSpec(memory_space=pl.ANY)]


def _rms_fwd(name, x, g, tm=512, after=None):
    S, D = x.shape
    after_args, after_specs = _after_operand(after)

    def body(x_ref, g_ref, *rest):
        y_ref = rest[-1]
        xv = x_ref[...]
        r = lax.rsqrt(jnp.mean(xv * xv, axis=-1, keepdims=True) + EPS)
        y_ref[...] = _bf(xv * r * g_ref[...])

    return pl.pallas_call(
        body, name=name, grid=(S // tm,),
        in_specs=[pl.BlockSpec((tm, D), lambda i: (i, 0)), pl.BlockSpec((1, D), lambda i: (0, 0))] + after_specs,
        out_specs=pl.BlockSpec((tm, D), lambda i: (i, 0)),
        out_shape=jax.ShapeDtypeStruct((S, D), BF16),
        compiler_params=_params(("parallel",)),
    )(x, g, *after_args)


def _rms_bwd(name, x, g, dy, dres, tm=512):
    S, D = x.shape

    def body(x_ref, g_ref, dy_ref, dres_ref, dx_ref, dg_ref):
        i = pl.program_id(0)
        xv = x_ref[...]
        r = lax.rsqrt(jnp.mean(xv * xv, axis=-1, keepdims=True) + EPS)
        xh = xv * r
        dy_v = dy_ref[...]
        dyg = dy_v * g_ref[...]
        c = jnp.mean(dyg * xh, axis=-1, keepdims=True)
        dx_ref[...] = dres_ref[...] + r * (dyg - xh * c)
        part = jnp.sum(dy_v * xh, axis=0, keepdims=True)

        @pl.when(i == 0)
        def _():
            dg_ref[...] = part

        @pl.when(i > 0)
        def _():
            dg_ref[...] += part

    row = pl.BlockSpec((tm, D), lambda i: (i, 0))
    vec = pl.BlockSpec((1, D), lambda i: (0, 0))
    return pl.pallas_call(
        body, name=name, grid=(S // tm,),
        in_specs=[row, vec, row, row],
        out_specs=[row, vec],
        out_shape=[jax.ShapeDtypeStruct((S, D), F32), jax.ShapeDtypeStruct((1, D), F32)],
        compiler_params=_params(("arbitrary",)),
    )(x, g, dy, dres)


def _final_loss(name, h, g, target, tm=512):
    S, D = h.shape

    def body(h_ref, g_ref, t_ref, dh_ref, dg_ref, l_ref):
        i = pl.program_id(0)
        xv = h_ref[...]
        r = lax.rsqrt(jnp.mean(xv * xv, axis=-1, keepdims=True) + EPS)
        xh = xv * r
        gv = g_ref[...]
        e = xh * gv - t_ref[...]
        dout = e / D
        dyg = dout * gv
        c = jnp.mean(dyg * xh, axis=-1, keepdims=True)
        dh_ref[...] = r * (dyg - xh * c)
        dg_part = jnp.sum(dout * xh, axis=0, keepdims=True)
        l_part = jnp.sum(e * e, axis=0, keepdims=True)

        @pl.when(i == 0)
        def _():
            dg_ref[...] = dg_part
            l_ref[...] = l_part

        @pl.when(i > 0)
        def _():
            dg_ref[...] += dg_part
            l_ref[...] += l_part

    row = pl.BlockSpec((tm, D), lambda i: (i, 0))
    vec = pl.BlockSpec((1, D), lambda i: (0, 0))
    return pl.pallas_call(
        body, name=name, grid=(S // tm,),
        in_specs=[row, vec, row],
        out_specs=[row, vec, vec],
        out_shape=[jax.ShapeDtypeStruct((S, D), F32), jax.ShapeDtypeStruct((1, D), F32),
                   jax.ShapeDtypeStruct((1, D), F32)],
        compiler_params=_params(("arbitrary",)),
    )(h, g, target)


def _mm_rows(name, pairs, n_out, *, nt, tm, nc, epi="plain", extra=None, out_dtype=F32, after=None):
    M = pairs[0][0].shape[0]
    np_ = len(pairs)
    after_args, after_specs = _after_operand(after)

    def body(*refs):
        a_refs = refs[:np_]
        w_refs = refs[np_:2 * np_]
        pos = 2 * np_
        e_ref = None
        if extra is not None:
            e_ref = refs[pos]
            pos += 1
        pos += len(after_args)
        outs = refs[pos:]
        a_vals = [_bf(a[...]) for a in a_refs]
        for j in range(n_out // nc):
            cols = slice(j * nc, (j + 1) * nc)
            acc = None
            for a_v, w_ref in zip(a_vals, w_refs):
                w_v = w_ref[cols, :] if nt else w_ref[:, cols]
                t = _dot(a_v, w_v, NT if nt else NN)
                acc = t if acc is None else acc + t
            if epi == "plain":
                outs[0][:, cols] = acc.astype(out_dtype)
            elif epi == "res":
                outs[0][:, cols] = e_ref[:, cols] + acc
            elif epi == "relu2":
                outs[0][:, cols] = acc
                rl = jnp.maximum(acc, 0.0)
                outs[1][:, cols] = _bf(rl * rl)
            elif epi == "mask2relu":
                outs[0][:, cols] = _bf(acc * (2.0 * jnp.maximum(e_ref[:, cols], 0.0)))

    in_specs = [pl.BlockSpec((tm, a.shape[1]), lambda i: (i, 0)) for a, _, _, _ in pairs]
    for _, _, wshape, widx in pairs:
        in_specs.append(pl.BlockSpec(wshape, functools.partial(lambda i, widx: widx, widx=widx)))
    args = [a for a, _, _, _ in pairs] + [w for _, w, _, _ in pairs]
    if extra is not None:
        in_specs.append(pl.BlockSpec((tm, n_out), lambda i: (i, 0)))
        args.append(extra)
    in_specs += after_specs
    args += after_args
    row_out = pl.BlockSpec((tm, n_out), lambda i: (i, 0))
    if epi == "relu2":
        out_specs = [row_out, row_out]
        out_shape = [jax.ShapeDtypeStruct((M, n_out), F32), jax.ShapeDtypeStruct((M, n_out), BF16)]
    else:
        dt = BF16 if epi == "mask2relu" else (F32 if epi == "res" else out_dtype)
        out_specs = row_out
        out_shape = jax.ShapeDtypeStruct((M, n_out), dt)
    return pl.pallas_call(
        body, name=name, grid=(M // tm,), in_specs=in_specs, out_specs=out_specs, out_shape=out_shape,
        compiler_params=_params(("parallel",)),
    )(*args)


def _full(w):
    return (w, w.shape, (0, 0))


def _mm_nn(name, a, w, **kw):
    return _mm_rows(name, [(a, w, w.shape, (0, 0))], w.shape[1], nt=False, **kw)


def _mm_nt(name, a, w, **kw):
    return _mm_rows(name, [(a, w, w.shape, (0, 0))], w.shape[0], nt=True, **kw)


def _mm_tn(name, a, b, *, t1, tn, tm=512):
    M, K1 = a.shape
    N = b.shape[1]
    nm = M // tm

    def body(a_ref, b_ref, o_ref, acc_ref):
        m = pl.program_id(2)
        t = _dot(_bf(a_ref[...]), _bf(b_ref[...]), TN)

        @pl.when(m == 0)
        def _():
            acc_ref[...] = t

        @pl.when(m > 0)
        def _():
            acc_ref[...] += t

        @pl.when(m == nm - 1)
        def _():
            o_ref[...] = acc_ref[...]

    return pl.pallas_call(
        body, name=name, grid=(K1 // t1, N // tn, nm),
        in_specs=[pl.BlockSpec((tm, t1), lambda i, j, m: (m, i)), pl.BlockSpec((tm, tn), lambda i, j, m: (m, j))],
        out_specs=pl.BlockSpec((t1, tn), lambda i, j, m: (i, j)),
        out_shape=jax.ShapeDtypeStruct((K1, N), F32),
        scratch_shapes=[pltpu.VMEM((t1, tn), F32)],
        compiler_params=_params(("parallel", "parallel", "arbitrary")),
    )(a, b)


_INV_SQRT2 = 1.0 / math.sqrt(2.0)
_INV_SQRT2PI = 1.0 / math.sqrt(2.0 * math.pi)


def _gelu(x):
    return 0.5 * x * (1.0 + lax.erf(x * _INV_SQRT2))


def _gelu_grad(x):
    return 0.5 * (1.0 + lax.erf(x * _INV_SQRT2)) + x * (_INV_SQRT2PI * jnp.exp(-0.5 * x * x))


def _layer_norm_parts(v):
    mu = jnp.mean(v, axis=-1, keepdims=True)
    xc = v - mu
    rs = lax.rsqrt(jnp.mean(xc * xc, axis=-1, keepdims=True) + EPS)
    return xc * rs, rs


def _gate_fwd(name, uvp, ln_g, ln_b, wm, bs_full, tr=512):
    S, W2 = uvp.shape
    W = W2 // 2
    gd = W // GROUPS

    def body(u_ref, v_ref, lg_ref, lb_ref, wm_ref, bs_ref, z_ref):
        u = _gelu(u_ref[...])
        vh, _ = _layer_norm_parts(_gelu(v_ref[...]))
        vn = _bf(vh * lg_ref[...] + lb_ref[...])
        for ci in range(tr // CHUNK):
            rows = slice(ci * CHUNK, (ci + 1) * CHUNK)
            for g in range(GROUPS):
                cols = slice(g * gd, (g + 1) * gd)
                mixed = _dot(wm_ref[g], vn[rows, cols], NN) + bs_ref[:, cols]
                z_ref[rows, cols] = _bf(u[rows, cols] * mixed)

    vec = pl.BlockSpec((1, W), lambda i: (0, 0))
    return pl.pallas_call(
        body, name=name, grid=(S // tr,),
        in_specs=[pl.BlockSpec((tr, W), lambda i: (i, 0)), pl.BlockSpec((tr, W), lambda i: (i, 1)), vec, vec,
                  pl.BlockSpec((GROUPS, CHUNK, CHUNK), lambda i: (0, 0, 0)),
                  pl.BlockSpec((CHUNK, W), lambda i: (0, 0))],
        out_specs=pl.BlockSpec((tr, W), lambda i: (i, 0)),
        out_shape=jax.ShapeDtypeStruct((S, W), BF16),
        compiler_params=_params(("parallel",)),
    )(uvp, uvp, ln_g, ln_b, wm, bs_full)


def _gate_bwd(name, uvp, dz, ln_g, ln_b, wm, bs_full, tr=256):
    S, W2 = uvp.shape
    W = W2 // 2
    gd = W // GROUPS
    n_steps = S // tr

    def body(u_ref, v_ref, dz_ref, lg_ref, lb_ref, wm_ref, bs_ref, duv_ref, dwm_ref, dmx_ref, dlg_ref, dlb_ref,
             dvn_ref):
        i = pl.program_id(0)
        up = u_ref[...]
        vp = v_ref[...]
        u = _gelu(up)
        vh, rs = _layer_norm_parts(_gelu(vp))
        lg = lg_ref[...]
        vn = _bf(vh * lg + lb_ref[...])
        dz_v = dz_ref[...]
        dmixed = dz_v * u
        dmixed_b = _bf(dmixed)

        @pl.when(i == 0)
        def _():
            dwm_ref[...] = jnp.zeros_like(dwm_ref)
            dmx_ref[...] = jnp.zeros_like(dmx_ref)
            dlg_ref[...] = jnp.zeros_like(dlg_ref)
            dlb_ref[...] = jnp.zeros_like(dlb_ref)

        for ci in range(tr // CHUNK):
            rows = slice(ci * CHUNK, (ci + 1) * CHUNK)
            dmx_ref[...] += dmixed[rows, :]
            for g in range(GROUPS):
                cols = slice(g * gd, (g + 1) * gd)
                mixed = _dot(wm_ref[g], vn[rows, cols], NN) + bs_ref[:, cols]
                duv_ref[rows, cols] = _bf(dz_v[rows, cols] * mixed * _gelu_grad(up[rows, cols]))
                dwm_ref[g] += _dot(dmixed_b[rows, cols], vn[rows, cols], NT)
                dvn_ref[rows, cols] = _dot(wm_ref[g], dmixed_b[rows, cols], TN)
        dvn = dvn_ref[...]
        dlg_ref[...] += jnp.sum(dvn * vh, axis=0, keepdims=True)
        dlb_ref[...] += jnp.sum(dvn, axis=0, keepdims=True)
        dvh = dvn * lg
        dv = rs * (dvh - jnp.mean(dvh, axis=-1, keepdims=True) - vh * jnp.mean(dvh * vh, axis=-1, keepdims=True))
        duv_ref[:, W:] = _bf(dv * _gelu_grad(vp))

        @pl.when(i == n_steps - 1)
        def _():
            t_idx = lax.broadcasted_iota(jnp.int32, (CHUNK, CHUNK), 0)
            s_idx = lax.broadcasted_iota(jnp.int32, (CHUNK, CHUNK), 1)
            keep = (s_idx <= t_idx).astype(F32)
            for g in range(GROUPS):
                dwm_ref[g] = dwm_ref[g] * keep

    vec = pl.BlockSpec((1, W), lambda i: (0, 0))
    row = pl.BlockSpec((tr, W), lambda i: (i, 0))
    return pl.pallas_call(
        body, name=name, grid=(n_steps,),
        in_specs=[row, pl.BlockSpec((tr, W), lambda i: (i, 1)), row, vec, vec,
                  pl.BlockSpec((GROUPS, CHUNK, CHUNK), lambda i: (0, 0, 0)),
                  pl.BlockSpec((CHUNK, W), lambda i: (0, 0))],
        out_specs=[pl.BlockSpec((tr, W2), lambda i: (i, 0)),
                   pl.BlockSpec((GROUPS, CHUNK, CHUNK), lambda i: (0, 0, 0)),
                   pl.BlockSpec((CHUNK, W), lambda i: (0, 0)), vec, vec],
        out_shape=[jax.ShapeDtypeStruct((S, W2), BF16), jax.ShapeDtypeStruct((GROUPS, CHUNK, CHUNK), F32),
                   jax.ShapeDtypeStruct((CHUNK, W), F32), jax.ShapeDtypeStruct((1, W), F32),
                   jax.ShapeDtypeStruct((1, W), F32)],
        scratch_shapes=[pltpu.VMEM((tr, W), F32)],
        compiler_params=_params(("arbitrary",)),
    )(uvp, uvp, dz, ln_g, ln_b, wm, bs_full)


def _t5_bucket(distance):
    small = distance < MAX_EXACT
    nf = jnp.maximum(distance, 1).astype(F32)
    large = MAX_EXACT + (jnp.log(nf / MAX_EXACT) / math.log(REL_MAX_DISTANCE / MAX_EXACT)
                         * (N_BUCKETS - MAX_EXACT)).astype(jnp.int32)
    large = jnp.minimum(large, N_BUCKETS - 1)
    return jnp.where(small, distance, large)


TILE_ELEMS = 2 * CHUNK * CHUNK


def _band_buckets():
    rel = CHUNK + jnp.arange(CHUNK)[None, :] - jnp.arange(2 * CHUNK)[:, None]
    band = (rel >= 0) & (rel <= CHUNK)
    buckets = [_t5_bucket(jnp.clip(rel, 0, CHUNK) * d) for d in DILATIONS]
    return jnp.stack(buckets), band


def _bucket_onehot():
    buckets, _ = _band_buckets()
    return (buckets.reshape(N_DIL, 1, TILE_ELEMS) == jnp.arange(N_BUCKETS)[None, :, None]).astype(F32)


def _bias_tiles(name, rel_bias):
    _, band = _band_buckets()
    own = band & (jnp.arange(2 * CHUNK) >= CHUNK)[:, None]
    masks = jnp.stack([own, band]).reshape(2, TILE_ELEMS).astype(F32)
    tables = jnp.transpose(rel_bias.reshape(N_BUCKETS, N_DIL, ATT_HEADS), (1, 2, 0))

    def body(t_ref, oh_ref, m_ref, out_ref):
        for g in range(N_DIL):
            bias = lax.dot_general(t_ref[g], oh_ref[g], (NN, ((), ())), precision=lax.Precision.HIGHEST,
                                   preferred_element_type=F32)
            for f in range(2):
                out_ref[g, f] = jnp.where(m_ref[f:f + 1, :] > 0.5, bias, NEG_INF)

    out = pl.pallas_call(
        body, name=name, out_shape=jax.ShapeDtypeStruct((N_DIL, 2, ATT_HEADS, TILE_ELEMS), F32),
        compiler_params=_params(),
    )(tables, _bucket_onehot(), masks)
    return out.reshape(N_DIL, 2, ATT_HEADS, 2 * CHUNK, CHUNK)


def _att_specs(order):
    def spec(part, prev):
        def index(*ids):
            hp, g, c = order(*ids)
            return (jnp.maximum(c - 1, 0) if prev else c, part * 3 * 4 + g * 4 + hp)
        return pl.BlockSpec((ATT_ROWS, LANES), index)
    return [spec(0, False), spec(1, False), spec(1, True), spec(2, False), spec(2, True)]


def _rows(start, d):
    if d == 1:
        return pl.ds(pl.multiple_of(start, CHUNK), CHUNK)
    return pl.ds(start, CHUNK, stride=d)


def _att_tile_offsets(t, d):
    n = t // d
    r = t % d
    return n * (CHUNK * d) + r, n


def _stage_prev_cur(dst, prev_ref, cur_ref):
    dst[0:ATT_ROWS, :] = prev_ref[...]
    dst[ATT_ROWS:2 * ATT_ROWS, :] = cur_ref[...]


def _att_fwd(name, qkv, bias_tiles):
    S = qkv.shape[0]
    n_chunks = S // ATT_ROWS
    tiles = ATT_ROWS // CHUNK

    def body(q_ref, kc_ref, kp_ref, vc_ref, vp_ref, b_ref, o_ref, l_ref, kk, vv):
        c = pl.program_id(1)
        g = pl.program_id(2)
        _stage_prev_cur(kk, kp_ref, kc_ref)
        _stage_prev_cur(vv, vp_ref, vc_ref)
        zeros = jnp.zeros((HEAD_DIM, CHUNK), BF16)

        for gi, d in enumerate(DILATIONS):
            @pl.when(g == gi)
            def _(d=d):
                span = CHUNK * d

                def tile(t, carry):
                    q0, n = _att_tile_offsets(t, d)
                    rows = _rows(q0, d)
                    cur = _rows(ATT_ROWS + q0, d)
                    prev = _rows(ATT_ROWS + q0 - span, d)
                    inner = jnp.where((c == 0) & (n == 0), 0, 1)
                    q_t = _bf(q_ref[rows, :] * ATT_SCALE).T
                    k2 = _bf(jnp.concatenate([kk[prev, :], kk[cur, :]], axis=0))
                    v_t = _bf(jnp.concatenate([vv[prev, :], vv[cur, :]], axis=0)).T
                    o_parts, l_parts = [], []
                    for hh in range(2):
                        half = slice(hh * HEAD_DIM, (hh + 1) * HEAD_DIM)
                        q_h = jnp.concatenate([q_t[half], zeros] if hh == 0 else [zeros, q_t[half]], axis=0)
                        s = _dot(k2, q_h, NN) + b_ref[inner, hh]
                        m = jnp.max(s, axis=0, keepdims=True)
                        p = jnp.exp(s - m)
                        l = jnp.sum(p, axis=0, keepdims=True)
                        o_parts.append(_dot(v_t, _bf(p), NN)[half] / l)
                        l_parts.append(jnp.broadcast_to(m + jnp.log(l), (HEAD_DIM, CHUNK)))
                    o_ref[rows, :] = jnp.concatenate(o_parts, axis=0).T
                    l_ref[rows, :] = jnp.concatenate(l_parts, axis=0).T
                    return carry

                lax.fori_loop(0, tiles, tile, 0, unroll=2)

    order = lambda hp, c, g: (hp, g, c)
    out_spec = pl.BlockSpec((None, ATT_ROWS, LANES), lambda hp, c, g: (g, c, hp))
    shape = jax.ShapeDtypeStruct((N_DIL, S, ATT_WIDTH), F32)
    return pl.pallas_call(
        body, name=name, grid=(ATT_HEADS // 2, n_chunks, N_DIL),
        in_specs=_att_specs(order) + [
            pl.BlockSpec((None, 2, 2, 2 * CHUNK, CHUNK), lambda hp, c, g: (g, 0, hp, 0, 0))],
        out_specs=[out_spec, out_spec],
        out_shape=[shape, shape],
        scratch_shapes=[pltpu.VMEM((2 * ATT_ROWS, LANES), F32), pltpu.VMEM((2 * ATT_ROWS, LANES), F32)],
        compiler_params=_params(("parallel", "parallel", "parallel")),
    )(qkv, qkv, qkv, qkv, qkv, bias_tiles)


def _att_merge(name, o_g, l_g, tm=512):
    _, S, W = o_g.shape

    def body(o_ref, l_ref, out_ref, lse_ref):
        ls = [l_ref[g] for g in range(N_DIL)]
        mx = functools.reduce(jnp.maximum, ls)
        ws = [jnp.exp(l - mx) for l in ls]
        tot = functools.reduce(lambda a, b: a + b, ws)
        acc = ws[0] * o_ref[0]
        for g in range(1, N_DIL):
            acc = acc + ws[g] * o_ref[g]
        out_ref[...] = acc / tot
        lse_ref[...] = mx + jnp.log(tot)

    blk = pl.BlockSpec((N_DIL, tm, W), lambda i: (0, i, 0))
    row = pl.BlockSpec((tm, W), lambda i: (i, 0))
    shape = jax.ShapeDtypeStruct((S, W), F32)
    return pl.pallas_call(
        body, name=name, grid=(S // tm,), in_specs=[blk, blk], out_specs=[row, row], out_shape=[shape, shape],
        compiler_params=_params(("parallel",)),
    )(o_g, l_g)


def _att_bwd(name, qkv, o, lse, d_o, bias_tiles):
    S = qkv.shape[0]
    n_chunks = S // ATT_ROWS
    tiles = ATT_ROWS // CHUNK

    def body(q_ref, kc_ref, kp_ref, vc_ref, vp_ref, o_ref, l_ref, do_ref, b_ref, dq_ref, dk_ref, dv_ref, ds_ref,
             kk, vv):
        g = pl.program_id(1)
        c = pl.program_id(2)

        @pl.when(c == 0)
        def _():
            dk_ref[...] = jnp.zeros_like(dk_ref)
            dv_ref[...] = jnp.zeros_like(dv_ref)
            ds_ref[...] = jnp.zeros_like(ds_ref)

        _stage_prev_cur(kk, kp_ref, kc_ref)
        _stage_prev_cur(vv, vp_ref, vc_ref)
        base = c * ATT_ROWS
        zeros = jnp.zeros((HEAD_DIM, CHUNK), BF16)
        lane = lax.broadcasted_iota(jnp.int32, (CHUNK, LANES), 1)

        for gi, d in enumerate(DILATIONS):
            @pl.when(g == gi)
            def _(d=d):
                span = CHUNK * d

                def tile(t, carry):
                    q0, n = _att_tile_offsets(t, d)
                    rows = _rows(q0, d)
                    cur = _rows(ATT_ROWS + q0, d)
                    prev = _rows(ATT_ROWS + q0 - span, d)
                    first = (c == 0) & (n == 0)
                    inner = jnp.where(first, 0, 1)
                    g_cur = _rows(base + q0, d)
                    g_prev = _rows(jnp.where(first, q0, base + q0 - span), d)
                    q2 = _bf(q_ref[rows, :] * ATT_SCALE)
                    q_t = q2.T
                    k2 = _bf(jnp.concatenate([kk[prev, :], kk[cur, :]], axis=0))
                    k_t = k2.T
                    v2 = _bf(jnp.concatenate([vv[prev, :], vv[cur, :]], axis=0))
                    do2 = do_ref[rows, :]
                    do_b = _bf(do2)
                    do_t = do_b.T
                    lse_t = l_ref[rows, :].T
                    dd_t = (do2 * o_ref[rows, :]).T
                    dq_parts = []
                    dk2 = dv2 = None
                    for hh in range(2):
                        half = slice(hh * HEAD_DIM, (hh + 1) * HEAD_DIM)
                        mine = (lane < HEAD_DIM) if hh == 0 else (lane >= HEAD_DIM)
                        q_h = jnp.concatenate([q_t[half], zeros] if hh == 0 else [zeros, q_t[half]], axis=0)
                        do_h = jnp.concatenate([do_t[half], zeros] if hh == 0 else [zeros, do_t[half]], axis=0)
                        s = _dot(k2, q_h, NN) + b_ref[inner, hh]
                        p = jnp.exp(s - lse_t[hh * HEAD_DIM:hh * HEAD_DIM + 1])
                        delta = jnp.sum(dd_t[half], axis=0, keepdims=True)
                        ds = p * (_dot(v2, do_h, NN) - delta)
                        ds_ref[hh] += ds
                        ds_b = _bf(ds)
                        dq_parts.append(_dot(k_t, ds_b, NN)[half])
                        dk_h = _dot(ds_b, jnp.where(mine, q2, jnp.zeros_like(q2)), NN)
                        dv_h = _dot(_bf(p), jnp.where(mine, do_b, jnp.zeros_like(do_b)), NN)
                        dk2 = dk_h if dk2 is None else dk2 + dk_h
                        dv2 = dv_h if dv2 is None else dv2 + dv_h
                    dq_ref[rows, :] = (jnp.concatenate(dq_parts, axis=0) * ATT_SCALE).T
                    dk_ref[g_prev, :] += dk2[0:CHUNK]
                    dk_ref[g_cur, :] += dk2[CHUNK:2 * CHUNK]
                    dv_ref[g_prev, :] += dv2[0:CHUNK]
                    dv_ref[g_cur, :] += dv2[CHUNK:2 * CHUNK]
                    return carry

                lax.fori_loop(0, tiles, tile, 0, unroll=2)

    order = lambda hp, g, c: (hp, g, c)
    chunk = pl.BlockSpec((ATT_ROWS, LANES), lambda hp, g, c: (c, hp))
    slab = pl.BlockSpec((S, LANES), lambda hp, g, c: (0, g * 4 + hp))
    width = N_DIL * ATT_WIDTH
    return pl.pallas_call(
        body, name=name, grid=(ATT_HEADS // 2, N_DIL, n_chunks),
        in_specs=_att_specs(order) + [chunk, chunk, chunk,
                                      pl.BlockSpec((None, 2, 2, 2 * CHUNK, CHUNK),
                                                   lambda hp, g, c: (g, 0, hp, 0, 0))],
        out_specs=[pl.BlockSpec((ATT_ROWS, LANES), lambda hp, g, c: (c, g * 4 + hp)), slab, slab,
                   pl.BlockSpec((None, 2, 2 * CHUNK, CHUNK), lambda hp, g, c: (g, hp, 0, 0))],
        out_shape=[jax.ShapeDtypeStruct((S, width), F32), jax.ShapeDtypeStruct((S, width), F32),
                   jax.ShapeDtypeStruct((S, width), F32),
                   jax.ShapeDtypeStruct((N_DIL, ATT_HEADS, 2 * CHUNK, CHUNK), F32)],
        scratch_shapes=[pltpu.VMEM((2 * ATT_ROWS, LANES), F32), pltpu.VMEM((2 * ATT_ROWS, LANES), F32)],
        compiler_params=_params(("parallel", "parallel", "arbitrary")),
    )(qkv, qkv, qkv, qkv, qkv, o, lse, d_o, bias_tiles)


def _bias_grad(name, ds_sums):
    flat = ds_sums.reshape(N_DIL, ATT_HEADS, TILE_ELEMS)

    def body(oh_ref, ds_ref, out_ref):
        for g in range(N_DIL):
            out_ref[g] = lax.dot_general(oh_ref[g], ds_ref[g], (NT, ((), ())), precision=lax.Precision.HIGHEST,
                                         preferred_element_type=F32)

    out = pl.pallas_call(
        body, name=name, out_shape=jax.ShapeDtypeStruct((N_DIL, N_BUCKETS, ATT_HEADS), F32),
        compiler_params=_params(),
    )(_bucket_onehot(), flat)
    return jnp.transpose(out, (1, 0, 2)).reshape(N_BUCKETS, N_DIL * ATT_HEADS)


def _peers():
    x, y, c = lax.axis_index("x"), lax.axis_index("y"), lax.axis_index("c")
    me = 4 * x + 2 * y + c
    others = [(x, y, 1 - c), (1 - x, y, c), (x, 1 - y, c), (1 - x, 1 - y, c),
              (1 - x, y, 1 - c), (x, 1 - y, 1 - c), (1 - x, 1 - y, 1 - c)]
    return me, others


def _slot(dev):
    return 4 * dev[0] + 2 * dev[1] + dev[2]


def _all_gather(name, shard):
    R, W = shard.shape

    def body(x_ref, out_ref, send_sems, recv_sems, local_sem):
        me, others = _peers()
        mine = pltpu.make_async_copy(x_ref, out_ref.at[me], local_sem)
        mine.start()
        sends = [pltpu.make_async_remote_copy(src_ref=x_ref, dst_ref=out_ref.at[me], send_sem=send_sems.at[k],
                                              recv_sem=recv_sems.at[k], device_id=dev, device_id_type=MESH)
                 for k, dev in enumerate(others)]
        for cp in sends:
            cp.start()
        for k, dev in enumerate(others):
            pltpu.make_async_remote_copy(src_ref=x_ref, dst_ref=out_ref.at[_slot(dev)], send_sem=send_sems.at[k],
                                         recv_sem=recv_sems.at[k], device_id=dev, device_id_type=MESH).wait_recv()
        for cp in sends:
            cp.wait_send()
        mine.wait()

    return pl.pallas_call(
        body, name=name,
        in_specs=[pl.BlockSpec(memory_space=pl.ANY)],
        out_specs=pl.BlockSpec(memory_space=pl.ANY),
        out_shape=jax.ShapeDtypeStruct((N_DEV, R, W), shard.dtype),
        scratch_shapes=[pltpu.SemaphoreType.DMA((N_DEV - 1,)), pltpu.SemaphoreType.DMA((N_DEV - 1,)),
                        pltpu.SemaphoreType.DMA],
    )(shard)


_HBM = pl.BlockSpec(memory_space=pltpu.HBM)
_SEM = pl.BlockSpec(memory_space=pltpu.SEMAPHORE)
_EFFECT = pltpu.SideEffectType.DATAFLOW_SIDE_EFFECTING


def _my_slot():
    return 4 * lax.axis_index("x") + 2 * lax.axis_index("y") + lax.axis_index("c")


def _exchange_copy(src_ref, land_ref, send_sems, recv_sems, k, dev, me, scatter, arriving):
    src = src_ref.at[me if arriving else _slot(dev)] if scatter else src_ref
    dst = land_ref.at[_slot(dev) if arriving else me]
    return pltpu.make_async_remote_copy(src_ref=src, dst_ref=dst, send_sem=send_sems.at[k], recv_sem=recv_sems.at[k],
                                        device_id=dev, device_id_type=MESH)


def _exchange_start(name, src, scatter):
    R, W = src.shape[-2:]
    me = _my_slot()
    own = lax.dynamic_index_in_dim(src, me, 0, keepdims=True) if scatter else src[None]
    landing = lax.dynamic_update_slice(lax.empty((N_DEV, R, W), src.dtype), own, (me, 0, 0))

    def body(src_ref, land_ref, send_sems, recv_sems, src_thru, land_thru, token):
        me, others = _peers()
        for k, dev in enumerate(others):
            _exchange_copy(src_ref, land_ref, send_sems, recv_sems, k, dev, me, scatter, False).start()
        token[...] = jnp.zeros_like(token)

    sems = pltpu.SemaphoreType.DMA((N_DEV - 1,))
    send_sems, recv_sems, src_thru, land_thru, token = pl.pallas_call(
        body, name=name,
        out_shape=(sems, sems, pltpu.HBM(src.shape, src.dtype), pltpu.HBM(landing.shape, landing.dtype),
                   jax.ShapeDtypeStruct((8, LANES), F32)),
        in_specs=(_HBM, _HBM), out_specs=(_SEM, _SEM, _HBM, _HBM, pl.BlockSpec(memory_space=pltpu.VMEM)),
        input_output_aliases={0: 2, 1: 3},
        compiler_params=pltpu.CompilerParams(has_side_effects=_EFFECT),
    )(pltpu.with_memory_space_constraint(src, pltpu.HBM), pltpu.with_memory_space_constraint(landing, pltpu.HBM))
    return (send_sems, recv_sems, src_thru, land_thru, scatter), token


def _exchange_wait(name, handle, after):
    send_sems, recv_sems, src_thru, land_thru, scatter = handle

    def body(src_ref, land_ref, send_sems, recv_sems, after_ref, src_dead, got_ref):
        me, others = _peers()
        for k, dev in enumerate(others):
            cp = _exchange_copy(src_ref, land_ref, send_sems, recv_sems, k, dev, me, scatter, True)
            cp.wait_send()
            cp.wait_recv()

    return pl.pallas_call(
        body, name=name,
        out_shape=(pltpu.HBM(src_thru.shape, src_thru.dtype), pltpu.HBM(land_thru.shape, land_thru.dtype)),
        in_specs=(_HBM, _HBM, _SEM, _SEM, pl.BlockSpec(memory_space=pl.ANY)), out_specs=(_HBM, _HBM),
        input_output_aliases={0: 0, 1: 1},
        compiler_params=pltpu.CompilerParams(has_side_effects=_EFFECT),
    )(src_thru, land_thru, send_sems, recv_sems, after)[1]


def _all_reduce_small(name, buf):
    rows = buf.shape[0]
    rb = rows // N_DEV

    def body(x_ref, out_ref, stage, send1, recv1, send2, recv2):
        me, others = _peers()

        def block(ref, k):
            return ref.at[pl.ds(k * rb, rb), :]

        first = [pltpu.make_async_remote_copy(src_ref=block(x_ref, _slot(dev)), dst_ref=stage.at[me],
                                              send_sem=send1.at[k], recv_sem=recv1.at[k], device_id=dev,
                                              device_id_type=MESH) for k, dev in enumerate(others)]
        for cp in first:
            cp.start()
        stage[me] = x_ref[pl.ds(pl.multiple_of(me * rb, 8), rb), :]
        for k, dev in enumerate(others):
            pltpu.make_async_remote_copy(src_ref=block(x_ref, me), dst_ref=stage.at[_slot(dev)],
                                         send_sem=send1.at[k], recv_sem=recv1.at[k], device_id=dev,
                                         device_id_type=MESH).wait_recv()
        total = stage[0]
        for j in range(1, N_DEV):
            total = total + stage[j]
        out_ref[pl.ds(pl.multiple_of(me * rb, 8), rb), :] = total
        second = [pltpu.make_async_remote_copy(src_ref=block(out_ref, me), dst_ref=block(out_ref, me),
                                               send_sem=send2.at[k], recv_sem=recv2.at[k], device_id=dev,
                                               device_id_type=MESH) for k, dev in enumerate(others)]
        for cp in second:
            cp.start()
        for k, dev in enumerate(others):
            pltpu.make_async_remote_copy(src_ref=block(out_ref, me), dst_ref=block(out_ref, _slot(dev)),
                                         send_sem=send2.at[k], recv_sem=recv2.at[k], device_id=dev,
                                         device_id_type=MESH).wait_recv()
        for cp in first + second:
            cp.wait_send()

    sems = pltpu.SemaphoreType.DMA((N_DEV - 1,))
    return pl.pallas_call(
        body, name=name,
        in_specs=[pl.BlockSpec(memory_space=pltpu.VMEM)],
        out_specs=pl.BlockSpec(memory_space=pltpu.VMEM),
        out_shape=jax.ShapeDtypeStruct(buf.shape, F32),
        scratch_shapes=[pltpu.VMEM((N_DEV, rb, LANES), F32), sems, sems, sems, sems],
        compiler_params=pltpu.CompilerParams(vmem_limit_bytes=VMEM_LIMIT_BYTES),
    )(buf)


def _adamw_math(w, g, m, v):
    m = ADAM_B1 * m + (1.0 - ADAM_B1) * g
    v = ADAM_B2 * v + (1.0 - ADAM_B2) * (g * g)
    m_hat = m / (1.0 - ADAM_B1 ** ADAM_STEP)
    v_hat = v / (1.0 - ADAM_B2 ** ADAM_STEP)
    delta = -ADAM_LR * (m_hat / (jnp.sqrt(v_hat) + ADAM_EPS) + ADAM_WD * w)
    return delta, m, v


def _adamw(name, parts, w, m, v, tr=128):
    P, R, W = parts.shape
    tr = min(tr, R)

    def body(p_ref, w_ref, m_ref, v_ref, g_out, d_out, m_out, v_out):
        g = p_ref[0].astype(F32)
        for j in range(1, P):
            g = g + p_ref[j].astype(F32)
        delta, m_new, v_new = _adamw_math(w_ref[...], g, m_ref[...], v_ref[...])
        g_out[...] = g
        d_out[...] = delta
        m_out[...] = m_new
        v_out[...] = v_new

    row = pl.BlockSpec((tr, W), lambda i: (i, 0))
    shape = jax.ShapeDtypeStruct((R, W), F32)
    return pl.pallas_call(
        body, name=name, grid=(R // tr,),
        in_specs=[pl.BlockSpec((P, tr, W), lambda i: (0, i, 0)), row, row, row],
        out_specs=[row, row, row, row],
        out_shape=[shape, shape, shape, shape],
        compiler_params=_params(("parallel",)),
    )(parts, w, m, v)


def _col_shards(full, n_local):
    K = full.shape[0]
    t = jnp.transpose(full.reshape(K, N_DEV, n_local), (1, 0, 2))
    return t.reshape(N_DEV, K * n_local // PACK_W, PACK_W)


def _from_col_shards(slots, K, n_local):
    t = slots.reshape(N_DEV, K, n_local)
    return jnp.transpose(t, (1, 0, 2)).reshape(K, N_DEV * n_local)


def _pack_rows(a):
    return a.reshape(-1, PACK_W)


_SMALL = ("mix_norm_g", "mlp_norm_g", "final_norm_g", "a_ln_g", "a_ln_b", "a_w_s", "a_b_s", "rel_bias")


def _pack_small(vals):
    pieces = []
    for n in _SMALL:
        flat = vals[n].reshape(-1)
        pad = (-flat.shape[0]) % (8 * LANES)
        pieces.append(jnp.pad(flat, (0, pad)).reshape(-1, LANES))
    rows = sum(p.shape[0] for p in pieces)
    tail = (-rows) % (8 * N_DEV)
    if tail:
        pieces.append(jnp.zeros((tail, LANES), F32))
    return jnp.concatenate(pieces, axis=0)


def _unpack_small(buf, like):
    out = {}
    r = 0
    for n in _SMALL:
        size = like[n].size
        nrows = -(-size // (8 * LANES)) * 8
        out[n] = buf[r:r + nrows].reshape(-1)[:size].reshape(like[n].shape)
        r += nrows
    return out


_STAGES = (("gate", (("a_w_in", 0), ("a_w_out", 0))),
           ("mlp0", (("w_up", 0), ("w_down", 0))),
           ("att", (("b_w_qkv", 0), ("b_w_out", 0))),
           ("mlp1", (("w_up", 1), ("w_down", 1))))
_COL_SHARDED = ("a_w_in", "b_w_qkv", "b_w_out", "w_up")


def _pack_stage(t, pieces):
    return jnp.concatenate([_pack_rows(t[n][i]) for n, i in pieces], axis=0)


def kernel(x, mix_norm_g, mlp_norm_g, final_norm_g, a_w_in, a_ln_g, a_ln_b, a_w_s, a_b_s, a_w_out, b_w_qkv, b_w_out, rel_bias, w_up, w_down, loss_target, m_mix_norm_g, m_mlp_norm_g, m_final_norm_g, m_a_w_in, m_a_ln_g, m_a_ln_b, m_a_w_s, m_a_b_s, m_a_w_out, m_b_w_qkv, m_b_w_out, m_rel_bias, m_w_up, m_w_down, v_mix_norm_g, v_mlp_norm_g, v_final_norm_g, v_a_w_in, v_a_ln_g, v_a_ln_b, v_a_w_s, v_a_b_s, v_a_w_out, v_b_w_qkv, v_b_w_out, v_rel_bias, v_w_up, v_w_down):
    w = dict(mix_norm_g=mix_norm_g, mlp_norm_g=mlp_norm_g, final_norm_g=final_norm_g, a_w_in=a_w_in, a_ln_g=a_ln_g,
             a_ln_b=a_ln_b, a_w_s=a_w_s, a_b_s=a_b_s, a_w_out=a_w_out, b_w_qkv=b_w_qkv, b_w_out=b_w_out,
             rel_bias=rel_bias, w_up=w_up, w_down=w_down)
    m = dict(mix_norm_g=m_mix_norm_g, mlp_norm_g=m_mlp_norm_g, final_norm_g=m_final_norm_g, a_w_in=m_a_w_in,
             a_ln_g=m_a_ln_g, a_ln_b=m_a_ln_b, a_w_s=m_a_w_s, a_b_s=m_a_b_s, a_w_out=m_a_w_out, b_w_qkv=m_b_w_qkv,
             b_w_out=m_b_w_out, rel_bias=m_rel_bias, w_up=m_w_up, w_down=m_w_down)
    v = dict(mix_norm_g=v_mix_norm_g, mlp_norm_g=v_mlp_norm_g, final_norm_g=v_final_norm_g, a_w_in=v_a_w_in,
             a_ln_g=v_a_ln_g, a_ln_b=v_a_ln_b, a_w_s=v_a_w_s, a_b_s=v_a_b_s, a_w_out=v_a_w_out, b_w_qkv=v_b_w_qkv,
             b_w_out=v_b_w_out, rel_bias=v_rel_bias, w_up=v_w_up, w_down=v_w_down)

    stages = dict(_STAGES)
    order = [s for s, _ in _STAGES]

    def full_weights(pieces, gathered):
        out, r = [], 0
        for n, i in pieces:
            shard = w[n][i]
            rows = shard.size // PACK_W
            seg = gathered[:, r:r + rows]
            r += rows
            if n in _COL_SHARDED:
                out.append(_from_col_shards(seg, shard.shape[0], shard.shape[1]))
            else:
                out.append(seg.reshape(N_DEV * shard.shape[0], shard.shape[1]))
        return out

    pending = {}

    def get_weights(stage, dep):
        if stage == order[0]:
            gathered = _all_gather("gather_" + stage, _bf(_pack_stage(w, stages[stage])))
        else:
            gathered = _exchange_wait("gather_" + stage + "_wait", pending.pop(stage), dep)
        nxt = order.index(stage) + 1
        token = None
        if nxt < len(order):
            shard = _bf(_pack_stage(w, stages[order[nxt]]))
            shard, gathered = lax.optimization_barrier((shard, gathered))
            pending[order[nxt]], token = _exchange_start("gather_" + order[nxt] + "_start", shard, False)
        return full_weights(stages[stage], gathered), token

    sent = {}

    def put_grads(stage, grads):
        parts = []
        for n, i in stages[stage]:
            shard = w[n][i]
            if n in _COL_SHARDED:
                parts.append(_col_shards(grads[n], shard.shape[1]))
            else:
                parts.append(grads[n].reshape(N_DEV, -1, PACK_W))
        sent[stage], token = _exchange_start("scatter_" + stage + "_start", _bf(jnp.concatenate(parts, axis=1)), True)
        return token

    loss_local, grad_x, small_g = _local_step(
        x[0], loss_target[0], mix_norm_g, mlp_norm_g, final_norm_g, a_ln_g, a_ln_b, a_w_s, a_b_s, rel_bias,
        get_weights, put_grads)

    new = {}
    for stage in reversed(order):
        received = _exchange_wait("scatter_" + stage + "_wait", sent[stage], grad_x)
        bufs = _adamw("adamw_" + stage, received, *[_pack_stage(t, stages[stage]) for t in (w, m, v)])
        r = 0
        for n, i in stages[stage]:
            rows = w[n][i].size // PACK_W
            new[n, i] = [b[r:r + rows].reshape(w[n][i].shape) for b in bufs]
            r += rows

    reduced = _all_reduce_small("reduce_small", _pack_small(small_g))
    small = [_unpack_small(b, w) for b in _adamw("adamw_small", reduced[None], _pack_small(w), _pack_small(m),
                                                 _pack_small(v), tr=reduced.shape[0])]

    outs = []
    for j in range(4):
        for n in w:
            if n in _SMALL:
                outs.append(small[j][n])
            else:
                outs.append(jnp.stack([new[n, i][j] for i in range(w[n].shape[0])]))
    loss = lax.psum(loss_local, ("x", "y", "c"))
    return (loss, grad_x[None], *outs)


def _local_step(xs, tgt, mix_norm_g, mlp_norm_g, final_norm_g, a_ln_g, a_ln_b, a_w_s, a_b_s, rel_bias,
                get_weights, put_grads):
    D = xs.shape[-1]
    g_mix = [mix_norm_g[l][None, :] for l in range(2)]
    g_mlp = [mlp_norm_g[l][None, :] for l in range(2)]
    g_fin = final_norm_g[None, :]
    ln_g, ln_b = a_ln_g, a_ln_b
    causal = jnp.tril(jnp.ones((CHUNK, CHUNK), dtype=bool))
    wm = _bf(jnp.where(causal[None], a_w_s[0], 0.0))
    bs_full = jnp.repeat(a_b_s[0].T, D // GROUPS, axis=1)
    bias_tiles = _bias_tiles("att_bias", rel_bias)

    (win, wout), token = get_weights("gate", xs)
    y0 = _rms_fwd("rms_mix0", xs, g_mix[0], after=token)
    uvp = _mm_nn("gate_in", y0, win, tm=512, nc=512)
    z = _gate_fwd("gate_mid", uvp, ln_g, ln_b, wm, bs_full)
    h1 = _mm_nn("gate_out", z, wout, tm=512, nc=512, epi="res", extra=xs)
    (wup0, wdn0), token = get_weights("mlp0", h1)
    y1 = _rms_fwd("rms_mlp0", h1, g_mlp[0], after=token)
    a0, f0 = _mm_nn("mlp0_up", y1, wup0, tm=256, nc=512, epi="relu2")
    h2 = _mm_nn("mlp0_down", f0, wdn0, tm=512, nc=512, epi="res", extra=h1)
    (wqkv, wo), token = get_weights("att", h2)
    y2 = _rms_fwd("rms_mix1", h2, g_mix[1], after=token)
    qkv = _mm_nn("att_qkv", y2, wqkv, tm=256, nc=512)
    o_att, lse = _att_merge("att_merge", *_att_fwd("att_fwd", qkv, bias_tiles))
    h3 = _mm_nn("att_out", o_att, wo, tm=512, nc=512, epi="res", extra=h2)
    (wup1, wdn1), token = get_weights("mlp1", h3)
    y3 = _rms_fwd("rms_mlp1", h3, g_mlp[1], after=token)
    a1, f1 = _mm_nn("mlp1_up", y3, wup1, tm=256, nc=512, epi="relu2")
    h4 = _mm_nn("mlp1_down", f1, wdn1, tm=512, nc=512, epi="res", extra=h3)
    dh, dg_fin, err2 = _final_loss("final_loss", h4, g_fin, tgt)
    loss_local = 0.5 * jnp.sum(err2) / D

    def mlp_bwd(tag, dh, h_in, y, a, f, wup_l, wdn_l, g_row, after):
        da = _mm_nt(tag + "_dact", dh, wdn_l, tm=256, nc=512, epi="mask2relu", extra=a, after=after)
        g_dn = _mm_tn(tag + "_dwdown", f, dh, t1=1024, tn=1024)
        g_up = _mm_tn(tag + "_dwup", y, da, t1=1024, tn=1024)
        dy = _mm_nt(tag + "_dy", da, wup_l, tm=512, nc=512)
        dh_in, dg = _rms_bwd(tag + "_drms", h_in, g_row, dy, dh)
        return dh_in, dg, put_grads(tag, dict(w_up=g_up, w_down=g_dn))

    dh3, dg_mlp1, token = mlp_bwd("mlp1", dh, h3, y3, a1, f1, wup1, wdn1, g_mlp[1], None)

    d_o = _mm_nt("att_dout", dh3, wo, tm=512, nc=512, after=token)
    g_wo = _mm_tn("att_dwo", o_att, dh3, t1=512, tn=1024)
    dq, dk, dv, ds_sums = _att_bwd("att_bwd", qkv, o_att, lse, d_o, bias_tiles)
    part_w = N_DIL * ATT_WIDTH
    g_qkv = [_mm_tn("att_dwqkv%d" % p, y2, t, t1=1024, tn=part_w) for p, t in enumerate((dq, dk, dv))]
    dy2 = _mm_rows("att_dy", [(t, wqkv, (D, part_w), (0, p)) for p, t in enumerate((dq, dk, dv))], D,
                   nt=True, tm=256, nc=512)
    dh2, dg_mix1 = _rms_bwd("att_drms", h2, g_mix[1], dy2, dh3)
    token = put_grads("att", dict(b_w_qkv=jnp.concatenate(g_qkv, axis=1), b_w_out=g_wo))

    dh1, dg_mlp0, token = mlp_bwd("mlp0", dh2, h1, y1, a0, f0, wup0, wdn0, g_mlp[0], token)

    dz = _mm_nt("gate_dz", dh1, wout, tm=512, nc=512, after=token)
    g_wout = _mm_tn("gate_dwout", z, dh1, t1=1024, tn=1024)
    duvp, d_wm, d_mixed, d_lng, d_lnb = _gate_bwd("gate_dmid", uvp, dz, ln_g, ln_b, wm, bs_full)
    g_win = _mm_tn("gate_dwin", y0, duvp, t1=1024, tn=1024)
    dy0 = _mm_nt("gate_dy", duvp, win, tm=512, nc=512)
    grad_x, dg_mix0 = _rms_bwd("gate_drms", xs, g_mix[0], dy0, dh1)
    put_grads("gate", dict(a_w_in=g_win, a_w_out=g_wout))

    small_g = dict(
        mix_norm_g=jnp.concatenate([dg_mix0, dg_mix1], axis=0),
        mlp_norm_g=jnp.concatenate([dg_mlp0, dg_mlp1], axis=0),
        final_norm_g=dg_fin[0], a_ln_g=d_lng, a_ln_b=d_lnb, a_w_s=d_wm[None],
        a_b_s=jnp.sum(d_mixed.reshape(CHUNK, GROUPS, D // GROUPS), axis=2).T[None],
        rel_bias=_bias_grad("att_dbias", ds_sums))
    return loss_local, grad_x, small_g
```

```python
import functools
import math

import jax
import jax.numpy as jnp
from jax import lax
from jax.experimental import pallas as pl
from jax.experimental.pallas import tpu as pltpu

F32 = jnp.float32
BF16 = jnp.bfloat16
MESH = pl.DeviceIdType.MESH

N_DEV = 8
EPS = 1e-6
NEG_INF = -1e30
CHUNK = 128
GROUPS = 8
HEAD_DIM = 64
ATT_HEADS = 8
ATT_WIDTH = ATT_HEADS * HEAD_DIM
DILATIONS = (1, 4, 16)
N_DIL = len(DILATIONS)
N_BUCKETS = 32
MAX_EXACT = N_BUCKETS // 2
REL_MAX_DISTANCE = 2048
ATT_ROWS = 2048
ATT_SCALE = HEAD_DIM ** -0.5
LANES = 128
PACK_W = 1024

ADAM_LR = 0.001
ADAM_B1 = 0.9
ADAM_B2 = 0.999
ADAM_EPS = 1e-08
ADAM_WD = 0.01
ADAM_STEP = 10

VMEM_LIMIT_BYTES = 56 * 1024 * 1024


def _params(semantics=None):
    return pltpu.CompilerParams(dimension_semantics=semantics, vmem_limit_bytes=VMEM_LIMIT_BYTES)


def _bf(v):
    return v.astype(BF16)


def _dot(a, b, dims):
    return lax.dot_general(a, b, (dims, ((), ())), preferred_element_type=F32)


NN = ((1,), (0,))
NT = ((1,), (1,))
TN = ((0,), (0,))


def _after_operand(after):
    if after is None:
        return [], []
    return [after], [pl.BlockSpec(memory_space=pl.ANY)]


def _rms_fwd(name, x, g, tm=512, after=None):
    S, D = x.shape
    after_args, after_specs = _after_operand(after)

    def body(x_ref, g_ref, *rest):
        y_ref = rest[-1]
        xv = x_ref[...]
        r = lax.rsqrt(jnp.mean(xv * xv, axis=-1, keepdims=True) + EPS)
        y_ref[...] = _bf(xv * r * g_ref[...])

    return pl.pallas_call(
        body, name=name, grid=(S // tm,),
        in_specs=[pl.BlockSpec((tm, D), lambda i: (i, 0)), pl.BlockSpec((1, D), lambda i: (0, 0))] + after_specs,
        out_specs=pl.BlockSpec((tm, D), lambda i: (i, 0)),
        out_shape=jax.ShapeDtypeStruct((S, D), BF16),
        compiler_params=_params(("parallel",)),
    )(x, g, *after_args)


def _final_loss(name, h, g, target, tm=512):
    S, D = h.shape

    def body(h_ref, g_ref, t_ref, dh_ref, dg_ref, l_ref):
        i = pl.program_id(0)
        xv = h_ref[...]
        r = lax.rsqrt(jnp.mean(xv * xv, axis=-1, keepdims=True) + EPS)
        xh = xv * r
        gv = g_ref[...]
        e = xh * gv - t_ref[...]
        dout = e / D
        dyg = dout * gv
        c = jnp.mean(dyg * xh, axis=-1, keepdims=True)
        dh_ref[...] = r * (dyg - xh * c)
        dg_part = jnp.sum(dout * xh, axis=0, keepdims=True)
        l_part = jnp.sum(e * e, axis=0, keepdims=True)

        @pl.when(i == 0)
        def _():
            dg_ref[...] = dg_part
            l_ref[...] = l_part

        @pl.when(i > 0)
        def _():
            dg_ref[...] += dg_part
            l_ref[...] += l_part

    row = pl.BlockSpec((tm, D), lambda i: (i, 0))
    vec = pl.BlockSpec((1, D), lambda i: (0, 0))
    return pl.pallas_call(
        body, name=name, grid=(S // tm,),
        in_specs=[row, vec, row],
        out_specs=[row, vec, vec],
        out_shape=[jax.ShapeDtypeStruct((S, D), F32), jax.ShapeDtypeStruct((1, D), F32),
                   jax.ShapeDtypeStruct((1, D), F32)],
        compiler_params=_params(("arbitrary",)),
    )(h, g, target)


def _mm_rows(name, pairs, n_out, *, nt, tm, nc, epi="plain", extra=None, out_dtype=F32, after=None):
    M = pairs[0][0].shape[0]
    np_ = len(pairs)
    after_args, after_specs = _after_operand(after)

    def body(*refs):
        a_refs = refs[:np_]
        w_refs = refs[np_:2 * np_]
        pos = 2 * np_
        e_ref = None
        if extra is not None:
            e_ref = refs[pos]
            pos += 1
        pos += len(after_args)
        outs = refs[pos:]
        a_vals = [_bf(a[...]) for a in a_refs]
        for j in range(n_out // nc):
            cols = slice(j * nc, (j + 1) * nc)
            acc = None
            for a_v, w_ref in zip(a_vals, w_refs):
                w_v = w_ref[cols, :] if nt else w_ref[:, cols]
                t = _dot(a_v, w_v, NT if nt else NN)
                acc = t if acc is None else acc + t
            if epi == "plain":
                outs[0][:, cols] = acc.astype(out_dtype)
            elif epi == "res":
                outs[0][:, cols] = e_ref[:, cols] + acc
            elif epi == "relu2":
                outs[0][:, cols] = _bf(acc)
                rl = jnp.maximum(acc, 0.0)
                outs[1][:, cols] = _bf(rl * rl)
            elif epi == "mask2relu":
                outs[0][:, cols] = _bf(acc * (2.0 * jnp.maximum(e_ref[:, cols].astype(F32), 0.0)))

    in_specs = [pl.BlockSpec((tm, a.shape[1]), lambda i: (i, 0)) for a, _, _, _ in pairs]
    for _, _, wshape, widx in pairs:
        in_specs.append(pl.BlockSpec(wshape, functools.partial(lambda i, widx: widx, widx=widx)))
    args = [a for a, _, _, _ in pairs] + [w for _, w, _, _ in pairs]
    if extra is not None:
        in_specs.append(pl.BlockSpec((tm, n_out), lambda i: (i, 0)))
        args.append(extra)
    in_specs += after_specs
    args += after_args
    row_out = pl.BlockSpec((tm, n_out), lambda i: (i, 0))
    if epi == "relu2":
        out_specs = [row_out, row_out]
        out_shape = [jax.ShapeDtypeStruct((M, n_out), BF16), jax.ShapeDtypeStruct((M, n_out), BF16)]
    else:
        dt = BF16 if epi == "mask2relu" else (F32 if epi == "res" else out_dtype)
        out_specs = row_out
        out_shape = jax.ShapeDtypeStruct((M, n_out), dt)
    return pl.pallas_call(
        body, name=name, grid=(M // tm,), in_specs=in_specs, out_specs=out_specs, out_shape=out_shape,
        compiler_params=_params(("parallel",)),
    )(*args)


def _full(w):
    return (w, w.shape, (0, 0))


def _mm_nn(name, a, w, **kw):
    return _mm_rows(name, [(a, w, w.shape, (0, 0))], w.shape[1], nt=False, **kw)


def _mm_nt(name, a, w, **kw):
    return _mm_rows(name, [(a, w, w.shape, (0, 0))], w.shape[0], nt=True, **kw)


def _mm_nt_rms_bwd(name, pairs, x, g, dres, *, tm, nc, after=None):
    M, D = x.shape
    np_ = len(pairs)
    after_args, after_specs = _after_operand(after)

    def body(*refs):
        a_refs = refs[:np_]
        w_refs = refs[np_:2 * np_]
        x_ref, g_ref, r_ref = refs[2 * np_:2 * np_ + 3]
        dx_ref, dg_ref, dy_sc = refs[-3:]
        i = pl.program_id(0)
        a_vals = [_bf(a[...]) for a in a_refs]
        for j in range(D // nc):
            cols = slice(j * nc, (j + 1) * nc)
            acc = None
            for a_v, w_ref in zip(a_vals, w_refs):
                t = _dot(a_v, w_ref[cols, :], NT)
                acc = t if acc is None else acc + t
            dy_sc[:, cols] = acc
        xv = x_ref[...]
        r = lax.rsqrt(jnp.mean(xv * xv, axis=-1, keepdims=True) + EPS)
        xh = xv * r
        dy_v = dy_sc[...]
        dyg = dy_v * g_ref[...]
        c = jnp.mean(dyg * xh, axis=-1, keepdims=True)
        dx_ref[...] = r_ref[...] + r * (dyg - xh * c)
        part = jnp.sum(dy_v * xh, axis=0, keepdims=True)

        @pl.when(i == 0)
        def _():
            dg_ref[...] = part

        @pl.when(i > 0)
        def _():
            dg_ref[...] += part

    row = pl.BlockSpec((tm, D), lambda i: (i, 0))
    vec = pl.BlockSpec((1, D), lambda i: (0, 0))
    in_specs = [pl.BlockSpec((tm, a.shape[1]), lambda i: (i, 0)) for a, _, _, _ in pairs]
    for _, _, wshape, widx in pairs:
        in_specs.append(pl.BlockSpec(wshape, functools.partial(lambda i, widx: widx, widx=widx)))
    args = [a for a, _, _, _ in pairs] + [w for _, w, _, _ in pairs]
    return pl.pallas_call(
        body, name=name, grid=(M // tm,),
        in_specs=in_specs + [row, vec, row] + after_specs,
        out_specs=[row, vec],
        out_shape=[jax.ShapeDtypeStruct((M, D), F32), jax.ShapeDtypeStruct((1, D), F32)],
        scratch_shapes=[pltpu.VMEM((tm, D), F32)],
        compiler_params=_params(("arbitrary",)),
    )(*args, x, g, dres, *after_args)


def _mm_tn(name, a, b, *, t1, tn, tm=2048):
    M, K1 = a.shape
    N = b.shape[1]

    def body(a_ref, b_ref, o_ref):
        m = pl.program_id(2)
        t = _dot(_bf(a_ref[...]), _bf(b_ref[...]), TN)

        @pl.when(m == 0)
        def _():
            o_ref[...] = t

        @pl.when(m > 0)
        def _():
            o_ref[...] += t

    return pl.pallas_call(
        body, name=name, grid=(K1 // t1, N // tn, M // tm),
        in_specs=[pl.BlockSpec((tm, t1), lambda i, j, m: (m, i)), pl.BlockSpec((tm, tn), lambda i, j, m: (m, j))],
        out_specs=pl.BlockSpec((t1, tn), lambda i, j, m: (i, j)),
        out_shape=jax.ShapeDtypeStruct((K1, N), F32),
        compiler_params=_params(("parallel", "parallel", "arbitrary")),
    )(a, b)


_INV_SQRT2 = 1.0 / math.sqrt(2.0)
_INV_SQRT2PI = 1.0 / math.sqrt(2.0 * math.pi)


def _gelu(x):
    return 0.5 * x * (1.0 + lax.erf(x * _INV_SQRT2))


def _gelu_grad(x):
    return 0.5 * (1.0 + lax.erf(x * _INV_SQRT2)) + x * (_INV_SQRT2PI * jnp.exp(-0.5 * x * x))


def _layer_norm_parts(v):
    mu = jnp.mean(v, axis=-1, keepdims=True)
    xc = v - mu
    rs = lax.rsqrt(jnp.mean(xc * xc, axis=-1, keepdims=True) + EPS)
    return xc * rs, rs


def _gate_fwd(name, uvp, ln_g, ln_b, wm, bs_full, tr=512):
    S, W2 = uvp.shape
    W = W2 // 2
    gd = W // GROUPS

    def body(u_ref, v_ref, lg_ref, lb_ref, wm_ref, bs_ref, z_ref):
        u = _gelu(u_ref[...])
        vh, _ = _layer_norm_parts(_gelu(v_ref[...]))
        vn = _bf(vh * lg_ref[...] + lb_ref[...])
        for ci in range(tr // CHUNK):
            rows = slice(ci * CHUNK, (ci + 1) * CHUNK)
            for g in range(GROUPS):
                cols = slice(g * gd, (g + 1) * gd)
                mixed = _dot(wm_ref[g], vn[rows, cols], NN) + bs_ref[:, cols]
                z_ref[rows, cols] = _bf(u[rows, cols] * mixed)

    vec = pl.BlockSpec((1, W), lambda i: (0, 0))
    return pl.pallas_call(
        body, name=name, grid=(S // tr,),
        in_specs=[pl.BlockSpec((tr, W), lambda i: (i, 0)), pl.BlockSpec((tr, W), lambda i: (i, 1)), vec, vec,
                  pl.BlockSpec((GROUPS, CHUNK, CHUNK), lambda i: (0, 0, 0)),
                  pl.BlockSpec((CHUNK, W), lambda i: (0, 0))],
        out_specs=pl.BlockSpec((tr, W), lambda i: (i, 0)),
        out_shape=jax.ShapeDtypeStruct((S, W), BF16),
        compiler_params=_params(("parallel",)),
    )(uvp, uvp, ln_g, ln_b, wm, bs_full)


def _gate_bwd(name, uvp, dz, ln_g, ln_b, wm, bs_full, tr=256):
    S, W2 = uvp.shape
    W = W2 // 2
    gd = W // GROUPS
    n_steps = S // tr

    def body(u_ref, v_ref, dz_ref, lg_ref, lb_ref, wm_ref, bs_ref, duv_ref, dwm_ref, dmx_ref, dlg_ref, dlb_ref,
             dvn_ref):
        i = pl.program_id(0)
        up = u_ref[...]
        vp = v_ref[...]
        u = _gelu(up)
        vh, rs = _layer_norm_parts(_gelu(vp))
        lg = lg_ref[...]
        vn = _bf(vh * lg + lb_ref[...])
        dz_v = dz_ref[...]
        dmixed = dz_v * u
        dmixed_b = _bf(dmixed)

        @pl.when(i == 0)
        def _():
            dwm_ref[...] = jnp.zeros_like(dwm_ref)
            dmx_ref[...] = jnp.zeros_like(dmx_ref)
            dlg_ref[...] = jnp.zeros_like(dlg_ref)
            dlb_ref[...] = jnp.zeros_like(dlb_ref)

        for ci in range(tr // CHUNK):
            rows = slice(ci * CHUNK, (ci + 1) * CHUNK)
            dmx_ref[...] += dmixed[rows, :]
            for g in range(GROUPS):
                cols = slice(g * gd, (g + 1) * gd)
                mixed = _dot(wm_ref[g], vn[rows, cols], NN) + bs_ref[:, cols]
                duv_ref[rows, cols] = _bf(dz_v[rows, cols] * mixed * _gelu_grad(up[rows, cols]))
                dwm_ref[g] += _dot(dmixed_b[rows, cols], vn[rows, cols], NT)
                dvn_ref[rows, cols] = _dot(wm_ref[g], dmixed_b[rows, cols], TN)
        dvn = dvn_ref[...]
        dlg_ref[...] += jnp.sum(dvn * vh, axis=0, keepdims=True)
        dlb_ref[...] += jnp.sum(dvn, axis=0, keepdims=True)
        dvh = dvn * lg
        dv = rs * (dvh - jnp.mean(dvh, axis=-1, keepdims=True) - vh * jnp.mean(dvh * vh, axis=-1, keepdims=True))
        duv_ref[:, W:] = _bf(dv * _gelu_grad(vp))

        @pl.when(i == n_steps - 1)
        def _():
            t_idx = lax.broadcasted_iota(jnp.int32, (CHUNK, CHUNK), 0)
            s_idx = lax.broadcasted_iota(jnp.int32, (CHUNK, CHUNK), 1)
            keep = (s_idx <= t_idx).astype(F32)
            for g in range(GROUPS):
                dwm_ref[g] = dwm_ref[g] * keep

    vec = pl.BlockSpec((1, W), lambda i: (0, 0))
    row = pl.BlockSpec((tr, W), lambda i: (i, 0))
    return pl.pallas_call(
        body, name=name, grid=(n_steps,),
        in_specs=[row, pl.BlockSpec((tr, W), lambda i: (i, 1)), row, vec, vec,
                  pl.BlockSpec((GROUPS, CHUNK, CHUNK), lambda i: (0, 0, 0)),
                  pl.BlockSpec((CHUNK, W), lambda i: (0, 0))],
        out_specs=[pl.BlockSpec((tr, W2), lambda i: (i, 0)),
                   pl.BlockSpec((GROUPS, CHUNK, CHUNK), lambda i: (0, 0, 0)),
                   pl.BlockSpec((CHUNK, W), lambda i: (0, 0)), vec, vec],
        out_shape=[jax.ShapeDtypeStruct((S, W2), BF16), jax.ShapeDtypeStruct((GROUPS, CHUNK, CHUNK), F32),
                   jax.ShapeDtypeStruct((CHUNK, W), F32), jax.ShapeDtypeStruct((1, W), F32),
                   jax.ShapeDtypeStruct((1, W), F32)],
        scratch_shapes=[pltpu.VMEM((tr, W), F32)],
        compiler_params=_params(("arbitrary",)),
    )(uvp, uvp, dz, ln_g, ln_b, wm, bs_full)


def _t5_bucket(distance):
    small = distance < MAX_EXACT
    nf = jnp.maximum(distance, 1).astype(F32)
    large = MAX_EXACT + (jnp.log(nf / MAX_EXACT) / math.log(REL_MAX_DISTANCE / MAX_EXACT)
                         * (N_BUCKETS - MAX_EXACT)).astype(jnp.int32)
    large = jnp.minimum(large, N_BUCKETS - 1)
    return jnp.where(small, distance, large)


TILE_ELEMS = 2 * CHUNK * CHUNK


def _band_buckets():
    rel = CHUNK + jnp.arange(CHUNK)[None, :] - jnp.arange(2 * CHUNK)[:, None]
    band = (rel >= 0) & (rel <= CHUNK)
    buckets = [_t5_bucket(jnp.clip(rel, 0, CHUNK) * d) for d in DILATIONS]
    return jnp.stack(buckets), band


def _bucket_onehot():
    buckets, _ = _band_buckets()
    return (buckets.reshape(N_DIL, 1, TILE_ELEMS) == jnp.arange(N_BUCKETS)[None, :, None]).astype(F32)


def _bias_tiles(name, rel_bias, after=None):
    _, band = _band_buckets()
    own = band & (jnp.arange(2 * CHUNK) >= CHUNK)[:, None]
    masks = jnp.stack([own, band]).reshape(2, TILE_ELEMS).astype(F32)
    tables = jnp.transpose(rel_bias.reshape(N_BUCKETS, N_DIL, ATT_HEADS), (1, 2, 0))
    after_args, after_specs = _after_operand(after)

    def body(t_ref, oh_ref, m_ref, *rest):
        out_ref = rest[-1]
        for g in range(N_DIL):
            bias = lax.dot_general(t_ref[g], oh_ref[g], (NN, ((), ())), precision=lax.Precision.HIGHEST,
                                   preferred_element_type=F32)
            for f in range(2):
                out_ref[g, f] = jnp.where(m_ref[f:f + 1, :] > 0.5, bias, NEG_INF)

    whole = pl.BlockSpec(memory_space=pltpu.VMEM)
    out = pl.pallas_call(
        body, name=name, out_shape=jax.ShapeDtypeStruct((N_DIL, 2, ATT_HEADS, TILE_ELEMS), F32),
        in_specs=[whole, whole, whole] + after_specs, out_specs=whole,
        compiler_params=_params(),
    )(tables, _bucket_onehot(), masks, *after_args)
    return out.reshape(N_DIL, 2, ATT_HEADS, 2 * CHUNK, CHUNK)


def _att_specs(order):
    def spec(part, prev):
        def index(*ids):
            hp, g, c = order(*ids)
            return (jnp.maximum(c - 1, 0) if prev else c, part * 3 * 4 + g * 4 + hp)
        return pl.BlockSpec((ATT_ROWS, LANES), index)
    return [spec(0, False), spec(1, False), spec(1, True), spec(2, False), spec(2, True)]


def _rows(start, d):
    if d == 1:
        return pl.ds(pl.multiple_of(start, CHUNK), CHUNK)
    return pl.ds(start, CHUNK, stride=d)


def _att_tile_offsets(t, d):
    n = t // d
    r = t % d
    return n * (CHUNK * d) + r, n


def _stage_prev_cur(dst, prev_ref, cur_ref):
    dst[0:ATT_ROWS, :] = prev_ref[...]
    dst[ATT_ROWS:2 * ATT_ROWS, :] = cur_ref[...]


def _att_fwd(name, qkv, bias_tiles):
    S = qkv.shape[0]
    n_chunks = S // ATT_ROWS
    tiles = ATT_ROWS // CHUNK

    def body(q_ref, kc_ref, kp_ref, vc_ref, vp_ref, b_ref, o_ref, l_ref, kk, vv):
        c = pl.program_id(1)
        g = pl.program_id(2)
        _stage_prev_cur(kk, kp_ref, kc_ref)
        _stage_prev_cur(vv, vp_ref, vc_ref)
        zeros = jnp.zeros((HEAD_DIM, CHUNK), BF16)

        for gi, d in enumerate(DILATIONS):
            @pl.when(g == gi)
            def _(d=d):
                span = CHUNK * d

                def tile(t, carry):
                    q0, n = _att_tile_offsets(t, d)
                    rows = _rows(q0, d)
                    cur = _rows(ATT_ROWS + q0, d)
                    prev = _rows(ATT_ROWS + q0 - span, d)
                    inner = jnp.where((c == 0) & (n == 0), 0, 1)
                    q_t = _bf(q_ref[rows, :] * ATT_SCALE).T
                    k2 = _bf(jnp.concatenate([kk[prev, :], kk[cur, :]], axis=0))
                    v_t = _bf(jnp.concatenate([vv[prev, :], vv[cur, :]], axis=0)).T
                    o_parts, l_parts = [], []
                    for hh in range(2):
                        half = slice(hh * HEAD_DIM, (hh + 1) * HEAD_DIM)
                        q_h = jnp.concatenate([q_t[half], zeros] if hh == 0 else [zeros, q_t[half]], axis=0)
                        s = _dot(k2, q_h, NN) + b_ref[inner, hh]
                        m = jnp.max(s, axis=0, keepdims=True)
                        p = jnp.exp(s - m)
                        l = jnp.sum(p, axis=0, keepdims=True)
                        o_parts.append(_dot(v_t, _bf(p), NN)[half] / l)
                        l_parts.append(jnp.broadcast_to(m + jnp.log(l), (HEAD_DIM, CHUNK)))
                    o_ref[rows, :] = jnp.concatenate(o_parts, axis=0).T
                    l_ref[rows, :] = jnp.concatenate(l_parts, axis=0).T
                    return carry

                lax.fori_loop(0, tiles, tile, 0, unroll=2)

    order = lambda hp, c, g: (hp, g, c)
    out_spec = pl.BlockSpec((None, ATT_ROWS, LANES), lambda hp, c, g: (g, c, hp))
    shape = jax.ShapeDtypeStruct((N_DIL, S, ATT_WIDTH), F32)
    return pl.pallas_call(
        body, name=name, grid=(ATT_HEADS // 2, n_chunks, N_DIL),
        in_specs=_att_specs(order) + [
            pl.BlockSpec((None, 2, 2, 2 * CHUNK, CHUNK), lambda hp, c, g: (g, 0, hp, 0, 0))],
        out_specs=[out_spec, out_spec],
        out_shape=[shape, shape],
        scratch_shapes=[pltpu.VMEM((2 * ATT_ROWS, LANES), F32), pltpu.VMEM((2 * ATT_ROWS, LANES), F32)],
        compiler_params=_params(("parallel", "parallel", "parallel")),
    )(qkv, qkv, qkv, qkv, qkv, bias_tiles)


def _att_merge(name, o_g, l_g, tm=512):
    _, S, W = o_g.shape

    def body(o_ref, l_ref, out_ref, lse_ref):
        ls = [l_ref[g] for g in range(N_DIL)]
        mx = functools.reduce(jnp.maximum, ls)
        ws = [jnp.exp(l - mx) for l in ls]
        tot = functools.reduce(lambda a, b: a + b, ws)
        acc = ws[0] * o_ref[0]
        for g in range(1, N_DIL):
            acc = acc + ws[g] * o_ref[g]
        out_ref[...] = acc / tot
        lse_ref[...] = mx + jnp.log(tot)

    blk = pl.BlockSpec((N_DIL, tm, W), lambda i: (0, i, 0))
    row = pl.BlockSpec((tm, W), lambda i: (i, 0))
    shape = jax.ShapeDtypeStruct((S, W), F32)
    return pl.pallas_call(
        body, name=name, grid=(S // tm,), in_specs=[blk, blk], out_specs=[row, row], out_shape=[shape, shape],
        compiler_params=_params(("parallel",)),
    )(o_g, l_g)


def _att_bwd(name, qkv, o, lse, d_o, bias_tiles):
    S = qkv.shape[0]
    n_chunks = S // ATT_ROWS
    tiles = ATT_ROWS // CHUNK

    def body(q_ref, kc_ref, kp_ref, vc_ref, vp_ref, o_ref, l_ref, do_ref, b_ref, dq_ref, dk_ref, dv_ref, ds_ref,
             kk, vv):
        g = pl.program_id(1)
        c = pl.program_id(2)

        @pl.when(c == 0)
        def _():
            dk_ref[...] = jnp.zeros_like(dk_ref)
            dv_ref[...] = jnp.zeros_like(dv_ref)
            ds_ref[...] = jnp.zeros_like(ds_ref)

        _stage_prev_cur(kk, kp_ref, kc_ref)
        _stage_prev_cur(vv, vp_ref, vc_ref)
        base = c * ATT_ROWS
        zeros = jnp.zeros((HEAD_DIM, CHUNK), BF16)
        lane = lax.broadcasted_iota(jnp.int32, (CHUNK, LANES), 1)

        for gi, d in enumerate(DILATIONS):
            @pl.when(g == gi)
            def _(d=d):
                span = CHUNK * d

                def tile(t, carry):
                    q0, n = _att_tile_offsets(t, d)
                    rows = _rows(q0, d)
                    cur = _rows(ATT_ROWS + q0, d)
                    prev = _rows(ATT_ROWS + q0 - span, d)
                    first = (c == 0) & (n == 0)
                    inner = jnp.where(first, 0, 1)
                    g_cur = _rows(base + q0, d)
                    g_prev = _rows(jnp.where(first, q0, base + q0 - span), d)
                    q2 = _bf(q_ref[rows, :] * ATT_SCALE)
                    q_t = q2.T
                    k2 = _bf(jnp.concatenate([kk[prev, :], kk[cur, :]], axis=0))
                    k_t = k2.T
                    v2 = _bf(jnp.concatenate([vv[prev, :], vv[cur, :]], axis=0))
                    do2 = do_ref[rows, :]
                    do_b = _bf(do2)
                    do_t = do_b.T
                    lse_t = l_ref[rows, :].T
                    dd_t = (do2 * o_ref[rows, :]).T
                    dq_parts = []
                    dk2 = dv2 = None
                    for hh in range(2):
                        half = slice(hh * HEAD_DIM, (hh + 1) * HEAD_DIM)
                        mine = (lane < HEAD_DIM) if hh == 0 else (lane >= HEAD_DIM)
                        q_h = jnp.concatenate([q_t[half], zeros] if hh == 0 else [zeros, q_t[half]], axis=0)
                        do_h = jnp.concatenate([do_t[half], zeros] if hh == 0 else [zeros, do_t[half]], axis=0)
                        s = _dot(k2, q_h, NN) + b_ref[inner, hh]
                        p = jnp.exp(s - lse_t[hh * HEAD_DIM:hh * HEAD_DIM + 1])
                        delta = jnp.sum(dd_t[half], axis=0, keepdims=True)
                        ds = p * (_dot(v2, do_h, NN) - delta)
                        ds_ref[hh] += ds
                        ds_b = _bf(ds)
                        dq_parts.append(_dot(k_t, ds_b, NN)[half])
                        dk_h = _dot(ds_b, jnp.where(mine, q2, jnp.zeros_like(q2)), NN)
                        dv_h = _dot(_bf(p), jnp.where(mine, do_b, jnp.zeros_like(do_b)), NN)
                        dk2 = dk_h if dk2 is None else dk2 + dk_h
                        dv2 = dv_h if dv2 is None else dv2 + dv_h
                    dq_ref[rows, :] = (jnp.concatenate(dq_parts, axis=0) * ATT_SCALE).T
                    dk_ref[g_prev, :] += dk2[0:CHUNK]
                    dk_ref[g_cur, :] += dk2[CHUNK:2 * CHUNK]
                    dv_ref[g_prev, :] += dv2[0:CHUNK]
                    dv_ref[g_cur, :] += dv2[CHUNK:2 * CHUNK]
                    return carry

                lax.fori_loop(0, tiles, tile, 0, unroll=2)

    order = lambda hp, g, c: (hp, g, c)
    chunk = pl.BlockSpec((ATT_ROWS, LANES), lambda hp, g, c: (c, hp))
    slab = pl.BlockSpec((S, LANES), lambda hp, g, c: (0, g * 4 + hp))
    width = N_DIL * ATT_WIDTH
    return pl.pallas_call(
        body, name=name, grid=(ATT_HEADS // 2, N_DIL, n_chunks),
        in_specs=_att_specs(order) + [chunk, chunk, chunk,
                                      pl.BlockSpec((None, 2, 2, 2 * CHUNK, CHUNK),
                                                   lambda hp, g, c: (g, 0, hp, 0, 0))],
        out_specs=[pl.BlockSpec((ATT_ROWS, LANES), lambda hp, g, c: (c, g * 4 + hp)), slab, slab,
                   pl.BlockSpec((None, 2, 2 * CHUNK, CHUNK), lambda hp, g, c: (g, hp, 0, 0))],
        out_shape=[jax.ShapeDtypeStruct((S, width), F32), jax.ShapeDtypeStruct((S, width), F32),
                   jax.ShapeDtypeStruct((S, width), F32),
                   jax.ShapeDtypeStruct((N_DIL, ATT_HEADS, 2 * CHUNK, CHUNK), F32)],
        scratch_shapes=[pltpu.VMEM((2 * ATT_ROWS, LANES), F32), pltpu.VMEM((2 * ATT_ROWS, LANES), F32)],
        compiler_params=_params(("parallel", "parallel", "arbitrary")),
    )(qkv, qkv, qkv, qkv, qkv, o, lse, d_o, bias_tiles)


def _bias_grad(name, ds_sums):
    flat = ds_sums.reshape(N_DIL, ATT_HEADS, TILE_ELEMS)

    def body(oh_ref, ds_ref, out_ref):
        for g in range(N_DIL):
            out_ref[g] = lax.dot_general(oh_ref[g], ds_ref[g], (NT, ((), ())), precision=lax.Precision.HIGHEST,
                                         preferred_element_type=F32)

    out = pl.pallas_call(
        body, name=name, out_shape=jax.ShapeDtypeStruct((N_DIL, N_BUCKETS, ATT_HEADS), F32),
        compiler_params=_params(),
    )(_bucket_onehot(), flat)
    return jnp.transpose(out, (1, 0, 2)).reshape(N_BUCKETS, N_DIL * ATT_HEADS)


def _peers():
    x, y, c = lax.axis_index("x"), lax.axis_index("y"), lax.axis_index("c")
    me = 4 * x + 2 * y + c
    others = [(x, y, 1 - c), (1 - x, y, c), (x, 1 - y, c), (1 - x, 1 - y, c),
              (1 - x, y, 1 - c), (x, 1 - y, 1 - c), (1 - x, 1 - y, 1 - c)]
    return me, others


def _slot(dev):
    return 4 * dev[0] + 2 * dev[1] + dev[2]


_HBM =pl.BlockSpec(memory_space=pltpu.HBM)
_SEM = pl.BlockSpec(memory_space=pltpu.SEMAPHORE)
_EFFECT = pltpu.SideEffectType.DATAFLOW_SIDE_EFFECTING


def _my_slot():
    return 4 * lax.axis_index("x") + 2 * lax.axis_index("y") + lax.axis_index("c")


def _exchange_copy(src_ref, land_ref, send_sems, recv_sems, k, dev, me, scatter, arriving):
    src = src_ref.at[me if arriving else _slot(dev)] if scatter else src_ref
    dst = land_ref.at[_slot(dev) if arriving else me]
    return pltpu.make_async_remote_copy(src_ref=src, dst_ref=dst, send_sem=send_sems.at[k], recv_sem=recv_sems.at[k],
                                        device_id=dev, device_id_type=MESH)


def _exchange_start(name, src, scatter):
    R, W = src.shape[-2:]
    me = _my_slot()
    own = lax.dynamic_index_in_dim(src, me, 0, keepdims=True) if scatter else src[None]
    landing = lax.dynamic_update_slice(lax.empty((N_DEV, R, W), src.dtype), own, (me, 0, 0))

    def body(src_ref, land_ref, send_sems, recv_sems, src_thru, land_thru, token):
        me, others = _peers()
        for k, dev in enumerate(others):
            _exchange_copy(src_ref, land_ref, send_sems, recv_sems, k, dev, me, scatter, False).start()
        token[...] = jnp.zeros_like(token)

    sems = pltpu.SemaphoreType.DMA((N_DEV - 1,))
    send_sems, recv_sems, src_thru, land_thru, token = pl.pallas_call(
        body, name=name,
        out_shape=(sems, sems, pltpu.HBM(src.shape, src.dtype), pltpu.HBM(landing.shape, landing.dtype),
                   jax.ShapeDtypeStruct((8, LANES), F32)),
        in_specs=(_HBM, _HBM), out_specs=(_SEM, _SEM, _HBM, _HBM, pl.BlockSpec(memory_space=pltpu.VMEM)),
        input_output_aliases={0: 2, 1: 3},
        compiler_params=pltpu.CompilerParams(has_side_effects=_EFFECT),
    )(pltpu.with_memory_space_constraint(src, pltpu.HBM), pltpu.with_memory_space_constraint(landing, pltpu.HBM))
    return (send_sems, recv_sems, src_thru, land_thru, scatter), token


def _exchange_wait(name, handle, after):
    send_sems, recv_sems, src_thru, land_thru, scatter = handle

    def body(src_ref, land_ref, send_sems, recv_sems, after_ref, src_dead, got_ref):
        me, others = _peers()
        for k, dev in enumerate(others):
            cp = _exchange_copy(src_ref, land_ref, send_sems, recv_sems, k, dev, me, scatter, True)
            cp.wait_send()
            cp.wait_recv()

    return pl.pallas_call(
        body, name=name,
        out_shape=(pltpu.HBM(src_thru.shape, src_thru.dtype), pltpu.HBM(land_thru.shape, land_thru.dtype)),
        in_specs=(_HBM, _HBM, _SEM, _SEM, pl.BlockSpec(memory_space=pl.ANY)), out_specs=(_HBM, _HBM),
        input_output_aliases={0: 0, 1: 1},
        compiler_params=pltpu.CompilerParams(has_side_effects=_EFFECT),
    )(src_thru, land_thru, send_sems, recv_sems, after)[1]


def _all_reduce_small(name, buf):
    rows = buf.shape[0]
    rb = rows // N_DEV

    def body(x_ref, out_ref, stage, send1, recv1, send2, recv2):
        me, others = _peers()

        def block(ref, k):
            return ref.at[pl.ds(k * rb, rb), :]

        first = [pltpu.make_async_remote_copy(src_ref=block(x_ref, _slot(dev)), dst_ref=stage.at[me],
                                              send_sem=send1.at[k], recv_sem=recv1.at[k], device_id=dev,
                                              device_id_type=MESH) for k, dev in enumerate(others)]
        for cp in first:
            cp.start()
        stage[me] = x_ref[pl.ds(pl.multiple_of(me * rb, 8), rb), :]
        for k, dev in enumerate(others):
            pltpu.make_async_remote_copy(src_ref=block(x_ref, me), dst_ref=stage.at[_slot(dev)],
                                         send_sem=send1.at[k], recv_sem=recv1.at[k], device_id=dev,
                                         device_id_type=MESH).wait_recv()
        total = stage[0]
        for j in range(1, N_DEV):
            total = total + stage[j]
        out_ref[pl.ds(pl.multiple_of(me * rb, 8), rb), :] = total
        second = [pltpu.make_async_remote_copy(src_ref=block(out_ref, me), dst_ref=block(out_ref, me),
                                               send_sem=send2.at[k], recv_sem=recv2.at[k], device_id=dev,
                                               device_id_type=MESH) for k, dev in enumerate(others)]
        for cp in second:
            cp.start()
        for k, dev in enumerate(others):
            pltpu.make_async_remote_copy(src_ref=block(out_ref, me), dst_ref=block(out_ref, _slot(dev)),
                                         send_sem=send2.at[k], recv_sem=recv2.at[k], device_id=dev,
                                         device_id_type=MESH).wait_recv()
        for cp in first + second:
            cp.wait_send()

    sems = pltpu.SemaphoreType.DMA((N_DEV - 1,))
    return pl.pallas_call(
        body, name=name,
        in_specs=[pl.BlockSpec(memory_space=pltpu.VMEM)],
        out_specs=pl.BlockSpec(memory_space=pltpu.VMEM),
        out_shape=jax.ShapeDtypeStruct(buf.shape, F32),
        scratch_shapes=[pltpu.VMEM((N_DEV, rb, LANES), F32), sems, sems, sems, sems],
        compiler_params=pltpu.CompilerParams(vmem_limit_bytes=VMEM_LIMIT_BYTES),
    )(buf)


def _adamw_math(w, g, m, v):
    m = ADAM_B1 * m + (1.0 - ADAM_B1) * g
    v = ADAM_B2 * v + (1.0 - ADAM_B2) * (g * g)
    m_hat = m / (1.0 - ADAM_B1 ** ADAM_STEP)
    v_hat = v / (1.0 - ADAM_B2 ** ADAM_STEP)
    delta = -ADAM_LR * (m_hat / (jnp.sqrt(v_hat) + ADAM_EPS) + ADAM_WD * w)
    return delta, m, v


def _adamw(name, parts, w, m, v, tr=128):
    P, R, W = parts.shape
    tr = min(tr, R)

    def body(p_ref, w_ref, m_ref, v_ref, g_out, d_out, m_out, v_out):
        g = p_ref[0].astype(F32)
        for j in range(1, P):
            g = g + p_ref[j].astype(F32)
        delta, m_new, v_new = _adamw_math(w_ref[...], g, m_ref[...], v_ref[...])
        g_out[...] = g
        d_out[...] = delta
        m_out[...] = m_new
        v_out[...] = v_new

    row = pl.BlockSpec((tr, W), lambda i: (i, 0))
    shape = jax.ShapeDtypeStruct((R, W), F32)
    return pl.pallas_call(
        body, name=name, grid=(R // tr,),
        in_specs=[pl.BlockSpec((P, tr, W), lambda i: (0, i, 0)), row, row, row],
        out_specs=[row, row, row, row],
        out_shape=[shape, shape, shape, shape],
        compiler_params=_params(("parallel",)),
    )(parts, w, m, v)


def _col_shards(full, n_local):
    K = full.shape[0]
    t = jnp.transpose(full.reshape(K, N_DEV, n_local), (1, 0, 2))
    return t.reshape(N_DEV, K * n_local // PACK_W, PACK_W)


def _from_col_shards(slots, K, n_local):
    t = slots.reshape(N_DEV, K, n_local)
    return jnp.transpose(t, (1, 0, 2)).reshape(K, N_DEV * n_local)


def _pack_rows(a):
    return a.reshape(-1, PACK_W)


_SMALL = ("mix_norm_g", "mlp_norm_g", "final_norm_g", "a_ln_g", "a_ln_b", "a_w_s", "a_b_s", "rel_bias")


def _pack_small(vals):
    pieces = []
    for n in _SMALL:
        flat = vals[n].reshape(-1)
        pad = (-flat.shape[0]) % (8 * LANES)
        pieces.append(jnp.pad(flat, (0, pad)).reshape(-1, LANES))
    rows = sum(p.shape[0] for p in pieces)
    tail = (-rows) % (8 * N_DEV)
    if tail:
        pieces.append(jnp.zeros((tail, LANES), F32))
    return jnp.concatenate(pieces, axis=0)


def _unpack_small(buf, like):
    out = {}
    r = 0
    for n in _SMALL:
        size = like[n].size
        nrows = -(-size // (8 * LANES)) * 8
        out[n] = buf[r:r + nrows].reshape(-1)[:size].reshape(like[n].shape)
        r += nrows
    return out


_STAGES = (("gate", (("a_w_in", 0), ("a_w_out", 0))),
           ("mlp0", (("w_up", 0), ("w_down", 0))),
           ("att", (("b_w_qkv", 0), ("b_w_out", 0))),
           ("mlp1", (("w_up", 1), ("w_down", 1))))
_COL_SHARDED = ("a_w_in", "b_w_qkv", "b_w_out", "w_up")


def _pack_stage(t, pieces):
    return jnp.concatenate([_pack_rows(t[n][i]) for n, i in pieces], axis=0)


def kernel(x, mix_norm_g, mlp_norm_g, final_norm_g, a_w_in, a_ln_g, a_ln_b, a_w_s, a_b_s, a_w_out, b_w_qkv, b_w_out, rel_bias, w_up, w_down, loss_target, m_mix_norm_g, m_mlp_norm_g, m_final_norm_g, m_a_w_in, m_a_ln_g, m_a_ln_b, m_a_w_s, m_a_b_s, m_a_w_out, m_b_w_qkv, m_b_w_out, m_rel_bias, m_w_up, m_w_down, v_mix_norm_g, v_mlp_norm_g, v_final_norm_g, v_a_w_in, v_a_ln_g, v_a_ln_b, v_a_w_s, v_a_b_s, v_a_w_out, v_b_w_qkv, v_b_w_out, v_rel_bias, v_w_up, v_w_down):
    w = dict(mix_norm_g=mix_norm_g, mlp_norm_g=mlp_norm_g, final_norm_g=final_norm_g, a_w_in=a_w_in, a_ln_g=a_ln_g,
             a_ln_b=a_ln_b, a_w_s=a_w_s, a_b_s=a_b_s, a_w_out=a_w_out, b_w_qkv=b_w_qkv, b_w_out=b_w_out,
             rel_bias=rel_bias, w_up=w_up, w_down=w_down)
    m = dict(mix_norm_g=m_mix_norm_g, mlp_norm_g=m_mlp_norm_g, final_norm_g=m_final_norm_g, a_w_in=m_a_w_in,
             a_ln_g=m_a_ln_g, a_ln_b=m_a_ln_b, a_w_s=m_a_w_s, a_b_s=m_a_b_s, a_w_out=m_a_w_out, b_w_qkv=m_b_w_qkv,
             b_w_out=m_b_w_out, rel_bias=m_rel_bias, w_up=m_w_up, w_down=m_w_down)
    v = dict(mix_norm_g=v_mix_norm_g, mlp_norm_g=v_mlp_norm_g, final_norm_g=v_final_norm_g, a_w_in=v_a_w_in,
             a_ln_g=v_a_ln_g, a_ln_b=v_a_ln_b, a_w_s=v_a_w_s, a_b_s=v_a_b_s, a_w_out=v_a_w_out, b_w_qkv=v_b_w_qkv,
             b_w_out=v_b_w_out, rel_bias=v_rel_bias, w_up=v_w_up, w_down=v_w_down)

    stages = dict(_STAGES)
    order = [s for s, _ in _STAGES]

    def full_weights(pieces, gathered):
        out, r = [], 0
        for n, i in pieces:
            shard = w[n][i]
            rows = shard.size // PACK_W
            seg = gathered[:, r:r + rows]
            r += rows
            if n in _COL_SHARDED:
                out.append(_from_col_shards(seg, shard.shape[0], shard.shape[1]))
            else:
                out.append(seg.reshape(N_DEV * shard.shape[0], shard.shape[1]))
        return out

    pending = {}
    pending[order[0]], first_token = _exchange_start("gather_" + order[0] + "_start",
                                                     _bf(_pack_stage(w, stages[order[0]])), False)

    def get_weights(stage, dep):
        gathered = _exchange_wait("gather_" + stage + "_wait", pending.pop(stage), dep)
        nxt = order.index(stage) + 1
        token = None
        if nxt < len(order):
            shard = _bf(_pack_stage(w, stages[order[nxt]]))
            shard, gathered = lax.optimization_barrier((shard, gathered))
            pending[order[nxt]], token = _exchange_start("gather_" + order[nxt] + "_start", shard, False)
        return full_weights(stages[stage], gathered), token

    sent = {}

    def put_grads(stage, grads):
        parts = []
        for n, i in stages[stage]:
            shard = w[n][i]
            if n in _COL_SHARDED:
                parts.append(_col_shards(grads[n], shard.shape[1]))
            else:
                parts.append(grads[n].reshape(N_DEV, -1, PACK_W))
        sent[stage], token = _exchange_start("scatter_" + stage + "_start", _bf(jnp.concatenate(parts, axis=1)), True)
        return token

    loss_local, grad_x, small_g = _local_step(
        x[0], loss_target[0], mix_norm_g, mlp_norm_g, final_norm_g, a_ln_g, a_ln_b, a_w_s, a_b_s, rel_bias,
        get_weights, put_grads, first_token)

    new = {}
    for stage in reversed(order):
        received = _exchange_wait("scatter_" + stage + "_wait", sent[stage], grad_x)
        bufs = _adamw("adamw_" + stage, received, *[_pack_stage(t, stages[stage]) for t in (w, m, v)])
        r = 0
        for n, i in stages[stage]:
            rows = w[n][i].size // PACK_W
            new[n, i] = [b[r:r + rows].reshape(w[n][i].shape) for b in bufs]
            r += rows

    reduced = _all_reduce_small("reduce_small", _pack_small(small_g))
    small = [_unpack_small(b, w) for b in _adamw("adamw_small", reduced[None], _pack_small(w), _pack_small(m),
                                                 _pack_small(v), tr=reduced.shape[0])]

    outs = []
    for j in range(4):
        for n in w:
            if n in _SMALL:
                outs.append(small[j][n])
            else:
                outs.append(jnp.stack([new[n, i][j] for i in range(w[n].shape[0])]))
    loss = lax.psum(loss_local, ("x", "y", "c"))
    return (loss, grad_x[None], *outs)


def _local_step(xs, tgt, mix_norm_g, mlp_norm_g, final_norm_g, a_ln_g, a_ln_b, a_w_s, a_b_s, rel_bias,
                get_weights, put_grads, first_token=None):
    D = xs.shape[-1]
    g_mix = [mix_norm_g[l][None, :] for l in range(2)]
    g_mlp = [mlp_norm_g[l][None, :] for l in range(2)]
    g_fin = final_norm_g[None, :]
    ln_g, ln_b = a_ln_g, a_ln_b
    causal = jnp.tril(jnp.ones((CHUNK, CHUNK), dtype=bool))
    wm = _bf(jnp.where(causal[None], a_w_s[0], 0.0))
    bs_full = jnp.repeat(a_b_s[0].T, D // GROUPS, axis=1)
    bias_tiles = _bias_tiles("att_bias", rel_bias, after=first_token)

    (win, wout), token = get_weights("gate", bias_tiles)
    y0 = _rms_fwd("rms_mix0", xs, g_mix[0], after=token)
    uvp = _mm_nn("gate_in", y0, win, tm=512, nc=512)
    z = _gate_fwd("gate_mid", uvp, ln_g, ln_b, wm, bs_full)
    h1 = _mm_nn("gate_out", z, wout, tm=512, nc=512, epi="res", extra=xs)
    (wup0, wdn0), token = get_weights("mlp0", h1)
    y1 = _rms_fwd("rms_mlp0", h1, g_mlp[0], after=token)
    a0, f0 = _mm_nn("mlp0_up", y1, wup0, tm=512, nc=512, epi="relu2")
    h2 = _mm_nn("mlp0_down", f0, wdn0, tm=512, nc=512, epi="res", extra=h1)
    (wqkv, wo), token = get_weights("att", h2)
    y2 = _rms_fwd("rms_mix1", h2, g_mix[1], after=token)
    qkv = _mm_nn("att_qkv", y2, wqkv, tm=256, nc=512)
    o_att, lse = _att_merge("att_merge", *_att_fwd("att_fwd", qkv, bias_tiles))
    h3 = _mm_nn("att_out", o_att, wo, tm=512, nc=512, epi="res", extra=h2)
    (wup1, wdn1), token = get_weights("mlp1", h3)
    y3 = _rms_fwd("rms_mlp1", h3, g_mlp[1], after=token)
    a1, f1 = _mm_nn("mlp1_up", y3, wup1, tm=512, nc=512, epi="relu2")
    h4 = _mm_nn("mlp1_down", f1, wdn1, tm=512, nc=512, epi="res", extra=h3)
    dh, dg_fin, err2 = _final_loss("final_loss", h4, g_fin, tgt)
    loss_local = 0.5 * jnp.sum(err2) / D

    def mlp_bwd(tag, dh, h_in, y, a, f, wup_l, wdn_l, g_row, after):
        da = _mm_nt(tag + "_dact", dh, wdn_l, tm=512, nc=512, epi="mask2relu", extra=a, after=after)
        g_dn = _mm_tn(tag + "_dwdown", f, dh, t1=1024, tn=1024)
        g_up = _mm_tn(tag + "_dwup", y, da, t1=1024, tn=1024)
        dh_in, dg = _mm_nt_rms_bwd(tag + "_dy", [(da, wup_l, wup_l.shape, (0, 0))], h_in, g_row, dh, tm=512, nc=512)
        return dh_in, dg, put_grads(tag, dict(w_up=g_up, w_down=g_dn))

    dh3, dg_mlp1, token = mlp_bwd("mlp1", dh, h3, y3, a1, f1, wup1, wdn1, g_mlp[1], None)

    d_o = _mm_nt("att_dout", dh3, wo, tm=512, nc=512, after=token)
    g_wo = _mm_tn("att_dwo", o_att, dh3, t1=512, tn=1024)
    dq, dk, dv, ds_sums = _att_bwd("att_bwd", qkv, o_att, lse, d_o, bias_tiles)
    part_w = N_DIL * ATT_WIDTH
    g_qkv = [_mm_tn("att_dwqkv%d" % p, y2, t, t1=1024, tn=part_w, tm=1024) for p, t in enumerate((dq, dk, dv))]
    dh2, dg_mix1 = _mm_nt_rms_bwd("att_dy", [(t, wqkv, (D, part_w), (0, p)) for p, t in enumerate((dq, dk, dv))],
                                  h2, g_mix[1], dh3, tm=256, nc=512)
    token = put_grads("att", dict(b_w_qkv=jnp.concatenate(g_qkv, axis=1), b_w_out=g_wo))

    dh1, dg_mlp0, token = mlp_bwd("mlp0", dh2, h1, y1, a0, f0, wup0, wdn0, g_mlp[0], token)

    dz = _mm_nt("gate_dz", dh1, wout, tm=512, nc=512, after=token)
    g_wout = _mm_tn("gate_dwout", z, dh1, t1=1024, tn=1024)
    duvp, d_wm, d_mixed, d_lng, d_lnb = _gate_bwd("gate_dmid", uvp, dz, ln_g, ln_b, wm, bs_full)
    g_win = _mm_tn("gate_dwin", y0, duvp, t1=1024, tn=1024)
    token = put_grads("gate", dict(a_w_in=g_win, a_w_out=g_wout))
    grad_x, dg_mix0 = _mm_nt_rms_bwd("gate_dy", [(duvp, win, win.shape, (0, 0))], xs, g_mix[0], dh1, tm=512, nc=512,
                                     after=token)

    small_g = dict(
        mix_norm_g=jnp.concatenate([dg_mix0, dg_mix1], axis=0),
        mlp_norm_g=jnp.concatenate([dg_mlp0, dg_mlp1], axis=0),
        final_norm_g=dg_fin[0], a_ln_g=d_lng, a_ln_b=d_lnb, a_w_s=d_wm[None],
        a_b_s=jnp.sum(d_mixed.reshape(CHUNK, GROUPS, D // GROUPS), axis=2).T[None],
        rel_bias=_bias_grad("att_dbias", ds_sums))
    return loss_local, grad_x, small_g
```

```python
import functools
import math

import jax
import jax.numpy as jnp
from jax import lax
from jax.experimental import pallas as pl
from jax.experimental.pallas import tpu as pltpu

F32 = jnp.float32
BF16 = jnp.bfloat16
MESH = pl.DeviceIdType.MESH

N_DEV = 8
EPS = 1e-6
NEG_INF = -1e30
CHUNK = 128
GROUPS = 8
HEAD_DIM = 64
ATT_HEADS = 8
ATT_WIDTH = ATT_HEADS * HEAD_DIM
DILATIONS = (1, 4, 16)
N_DIL = len(DILATIONS)
N_BUCKETS = 32
MAX_EXACT = N_BUCKETS // 2
REL_MAX_DISTANCE = 2048
ATT_ROWS = 2048
ATT_SCALE = HEAD_DIM ** -0.5
LANES = 128
PACK_W = 1024

ADAM_LR = 0.001
ADAM_B1 = 0.9
ADAM_B2 = 0.999
ADAM_EPS = 1e-08
ADAM_WD = 0.01
ADAM_STEP = 10

VMEM_LIMIT_BYTES = 56 * 1024 * 1024


def _params(semantics=None):
    return pltpu.CompilerParams(dimension_semantics=semantics, vmem_limit_bytes=VMEM_LIMIT_BYTES)


def _bf(v):
    return v.astype(BF16)


def _dot(a, b, dims):
    return lax.dot_general(a, b, (dims, ((), ())), preferred_element_type=F32)


NN = ((1,), (0,))
NT = ((1,), (1,))
TN = ((0,), (0,))


def _after_operand(after):
    if after is None:
        return [], []
    return [after], [pl.BlockSpec(memory_space=pl.ANY)]


def _rms_fwd(name, x, g, tm=512, after=None):
    S, D = x.shape
    after_args, after_specs = _after_operand(after)

    def body(x_ref, g_ref, *rest):
        y_ref = rest[-1]
        xv = x_ref[...]
        r = lax.rsqrt(jnp.mean(xv * xv, axis=-1, keepdims=True) + EPS)
        y_ref[...] = _bf(xv * r * g_ref[...])

    return pl.pallas_call(
        body, name=name, grid=(S // tm,),
        in_specs=[pl.BlockSpec((tm, D), lambda i: (i, 0)), pl.BlockSpec((1, D), lambda i: (0, 0))] + after_specs,
        out_specs=pl.BlockSpec((tm, D), lambda i: (i, 0)),
        out_shape=jax.ShapeDtypeStruct((S, D), BF16),
        compiler_params=_params(("parallel",)),
    )(x, g, *after_args)


def _final_loss(name, h, g, target, tm=512):
    S, D = h.shape

    def body(h_ref, g_ref, t_ref, dh_ref, dg_ref, l_ref):
        i = pl.program_id(0)
        xv = h_ref[...]
        r = lax.rsqrt(jnp.mean(xv * xv, axis=-1, keepdims=True) + EPS)
        xh = xv * r
        gv = g_ref[...]
        e = xh * gv - t_ref[...]
        dout = e / D
        dyg = dout * gv
        c = jnp.mean(dyg * xh, axis=-1, keepdims=True)
        dh_ref[...] = r * (dyg - xh * c)
        dg_part = jnp.sum(dout * xh, axis=0, keepdims=True)
        l_part = jnp.sum(e * e, axis=0, keepdims=True)

        @pl.when(i == 0)
        def _():
            dg_ref[...] = dg_part
            l_ref[...] = l_part

        @pl.when(i > 0)
        def _():
            dg_ref[...] += dg_part
            l_ref[...] += l_part

    row = pl.BlockSpec((tm, D), lambda i: (i, 0))
    vec = pl.BlockSpec((1, D), lambda i: (0, 0))
    return pl.pallas_call(
        body, name=name, grid=(S // tm,),
        in_specs=[row, vec, row],
        out_specs=[row, vec, vec],
        out_shape=[jax.ShapeDtypeStruct((S, D), F32), jax.ShapeDtypeStruct((1, D), F32),
                   jax.ShapeDtypeStruct((1, D), F32)],
        compiler_params=_params(("arbitrary",)),
    )(h, g, target)


def _mm_rows(name, pairs, n_out, *, nt, tm, nc, epi="plain", extra=None, out_dtype=F32, after=None):
    M = pairs[0][0].shape[0]
    np_ = len(pairs)
    after_args, after_specs = _after_operand(after)

    def body(*refs):
        a_refs = refs[:np_]
        w_refs = refs[np_:2 * np_]
        pos = 2 * np_
        e_ref = None
        if extra is not None:
            e_ref = refs[pos]
            pos += 1
        pos += len(after_args)
        outs = refs[pos:]
        a_vals = [_bf(a[...]) for a in a_refs]
        for j in range(n_out // nc):
            cols = slice(j * nc, (j + 1) * nc)
            acc = None
            for a_v, w_ref in zip(a_vals, w_refs):
                w_v = w_ref[cols, :] if nt else w_ref[:, cols]
                t = _dot(a_v, w_v, NT if nt else NN)
                acc = t if acc is None else acc + t
            if epi == "plain":
                outs[0][:, cols] = acc.astype(out_dtype)
            elif epi == "res":
                outs[0][:, cols] = e_ref[:, cols] + acc
            elif epi == "relu2":
                outs[0][:, cols] = _bf(acc)
                rl = jnp.maximum(acc, 0.0)
                outs[1][:, cols] = _bf(rl * rl)
            elif epi == "mask2relu":
                outs[0][:, cols] = _bf(acc * (2.0 * jnp.maximum(e_ref[:, cols].astype(F32), 0.0)))

    in_specs = [pl.BlockSpec((tm, a.shape[1]), lambda i: (i, 0)) for a, _, _, _ in pairs]
    for _, _, wshape, widx in pairs:
        in_specs.append(pl.BlockSpec(wshape, functools.partial(lambda i, widx: widx, widx=widx)))
    args = [a for a, _, _, _ in pairs] + [w for _, w, _, _ in pairs]
    if extra is not None:
        in_specs.append(pl.BlockSpec((tm, n_out), lambda i: (i, 0)))
        args.append(extra)
    in_specs += after_specs
    args += after_args
    row_out = pl.BlockSpec((tm, n_out), lambda i: (i, 0))
    if epi == "relu2":
        out_specs = [row_out, row_out]
        out_shape = [jax.ShapeDtypeStruct((M, n_out), BF16), jax.ShapeDtypeStruct((M, n_out), BF16)]
    else:
        dt = BF16 if epi == "mask2relu" else (F32 if epi == "res" else out_dtype)
        out_specs = row_out
        out_shape = jax.ShapeDtypeStruct((M, n_out), dt)
    return pl.pallas_call(
        body, name=name, grid=(M // tm,), in_specs=in_specs, out_specs=out_specs, out_shape=out_shape,
        compiler_params=_params(("parallel",)),
    )(*args)


def _full(w):
    return (w, w.shape, (0, 0))


def _mm_nn(name, a, w, **kw):
    return _mm_rows(name, [(a, w, w.shape, (0, 0))], w.shape[1], nt=False, **kw)


def _mm_nt(name, a, w, **kw):
    return _mm_rows(name, [(a, w, w.shape, (0, 0))], w.shape[0], nt=True, **kw)


def _mm_nt_rms_bwd(name, pairs, x, g, dres, *, tm, nc, after=None):
    M, D = x.shape
    np_ = len(pairs)
    after_args, after_specs = _after_operand(after)

    def body(*refs):
        a_refs = refs[:np_]
        w_refs = refs[np_:2 * np_]
        x_ref, g_ref, r_ref = refs[2 * np_:2 * np_ + 3]
        dx_ref, dg_ref, dy_sc = refs[-3:]
        i = pl.program_id(0)
        a_vals = [_bf(a[...]) for a in a_refs]
        for j in range(D // nc):
            cols = slice(j * nc, (j + 1) * nc)
            acc = None
            for a_v, w_ref in zip(a_vals, w_refs):
                t = _dot(a_v, w_ref[cols, :], NT)
                acc = t if acc is None else acc + t
            dy_sc[:, cols] = acc
        xv = x_ref[...]
        r = lax.rsqrt(jnp.mean(xv * xv, axis=-1, keepdims=True) + EPS)
        xh = xv * r
        dy_v = dy_sc[...]
        dyg = dy_v * g_ref[...]
        c = jnp.mean(dyg * xh, axis=-1, keepdims=True)
        dx_ref[...] = r_ref[...] + r * (dyg - xh * c)
        part = jnp.sum(dy_v * xh, axis=0, keepdims=True)

        @pl.when(i == 0)
        def _():
            dg_ref[...] = part

        @pl.when(i > 0)
        def _():
            dg_ref[...] += part

    row = pl.BlockSpec((tm, D), lambda i: (i, 0))
    vec = pl.BlockSpec((1, D), lambda i: (0, 0))
    in_specs = [pl.BlockSpec((tm, a.shape[1]), lambda i: (i, 0)) for a, _, _, _ in pairs]
    for _, _, wshape, widx in pairs:
        in_specs.append(pl.BlockSpec(wshape, functools.partial(lambda i, widx: widx, widx=widx)))
    args = [a for a, _, _, _ in pairs] + [w for _, w, _, _ in pairs]
    return pl.pallas_call(
        body, name=name, grid=(M // tm,),
        in_specs=in_specs + [row, vec, row] + after_specs,
        out_specs=[row, vec],
        out_shape=[jax.ShapeDtypeStruct((M, D), F32), jax.ShapeDtypeStruct((1, D), F32)],
        scratch_shapes=[pltpu.VMEM((tm, D), F32)],
        compiler_params=_params(("arbitrary",)),
    )(*args, x, g, dres, *after_args)


def _mm_tn(name, a, b, *, t1, tn, tm=2048):
    M, K1 = a.shape
    N = b.shape[1]

    def body(a_ref, b_ref, o_ref):
        m = pl.program_id(2)
        t = _dot(_bf(a_ref[...]), _bf(b_ref[...]), TN)

        @pl.when(m == 0)
        def _():
            o_ref[...] = t

        @pl.when(m > 0)
        def _():
            o_ref[...] += t

    return pl.pallas_call(
        body, name=name, grid=(K1 // t1, N // tn, M // tm),
        in_specs=[pl.BlockSpec((tm, t1), lambda i, j, m: (m, i)), pl.BlockSpec((tm, tn), lambda i, j, m: (m, j))],
        out_specs=pl.BlockSpec((t1, tn), lambda i, j, m: (i, j)),
        out_shape=jax.ShapeDtypeStruct((K1, N), F32),
        compiler_params=_params(("parallel", "parallel", "arbitrary")),
    )(a, b)


_INV_SQRT2 = 1.0 / math.sqrt(2.0)
_INV_SQRT2PI = 1.0 / math.sqrt(2.0 * math.pi)


def _gelu(x):
    return 0.5 * x * (1.0 + lax.erf(x * _INV_SQRT2))


def _gelu_grad(x):
    return 0.5 * (1.0 + lax.erf(x * _INV_SQRT2)) + x * (_INV_SQRT2PI * jnp.exp(-0.5 * x * x))


def _layer_norm_parts(v):
    mu = jnp.mean(v, axis=-1, keepdims=True)
    xc = v - mu
    rs = lax.rsqrt(jnp.mean(xc * xc, axis=-1, keepdims=True) + EPS)
    return xc * rs, rs


def _gate_fwd(name, uvp, ln_g, ln_b, wm, bs_full, tr=512):
    S, W2 = uvp.shape
    W = W2 // 2
    gd = W // GROUPS

    def body(u_ref, v_ref, lg_ref, lb_ref, wm_ref, bs_ref, z_ref):
        u = _gelu(u_ref[...])
        vh, _ = _layer_norm_parts(_gelu(v_ref[...]))
        vn = _bf(vh * lg_ref[...] + lb_ref[...])
        for ci in range(tr // CHUNK):
            rows = slice(ci * CHUNK, (ci + 1) * CHUNK)
            for g in range(GROUPS):
                cols = slice(g * gd, (g + 1) * gd)
                mixed = _dot(wm_ref[g], vn[rows, cols], NN) + bs_ref[:, cols]
                z_ref[rows, cols] = _bf(u[rows, cols] * mixed)

    vec = pl.BlockSpec((1, W), lambda i: (0, 0))
    return pl.pallas_call(
        body, name=name, grid=(S // tr,),
        in_specs=[pl.BlockSpec((tr, W), lambda i: (i, 0)), pl.BlockSpec((tr, W), lambda i: (i, 1)), vec, vec,
                  pl.BlockSpec((GROUPS, CHUNK, CHUNK), lambda i: (0, 0, 0)),
                  pl.BlockSpec((CHUNK, W), lambda i: (0, 0))],
        out_specs=pl.BlockSpec((tr, W), lambda i: (i, 0)),
        out_shape=jax.ShapeDtypeStruct((S, W), BF16),
        compiler_params=_params(("parallel",)),
    )(uvp, uvp, ln_g, ln_b, wm, bs_full)


def _gate_bwd(name, uvp, dz, ln_g, ln_b, wm, bs_full, tr=256):
    S, W2 = uvp.shape
    W = W2 // 2
    gd = W // GROUPS
    n_steps = S // tr

    def body(u_ref, v_ref, dz_ref, lg_ref, lb_ref, wm_ref, bs_ref, duv_ref, dwm_ref, dmx_ref, dlg_ref, dlb_ref,
             dvn_ref):
        i = pl.program_id(0)
        up = u_ref[...]
        vp = v_ref[...]
        u = _gelu(up)
        vh, rs = _layer_norm_parts(_gelu(vp))
        lg = lg_ref[...]
        vn = _bf(vh * lg + lb_ref[...])
        dz_v = dz_ref[...]
        dmixed = dz_v * u
        dmixed_b = _bf(dmixed)

        @pl.when(i == 0)
        def _():
            dwm_ref[...] = jnp.zeros_like(dwm_ref)
            dmx_ref[...] = jnp.zeros_like(dmx_ref)
            dlg_ref[...] = jnp.zeros_like(dlg_ref)
            dlb_ref[...] = jnp.zeros_like(dlb_ref)

        for ci in range(tr // CHUNK):
            rows = slice(ci * CHUNK, (ci + 1) * CHUNK)
            dmx_ref[...] += dmixed[rows, :]
            for g in range(GROUPS):
                cols = slice(g * gd, (g + 1) * gd)
                mixed = _dot(wm_ref[g], vn[rows, cols], NN) + bs_ref[:, cols]
                duv_ref[rows, cols] = _bf(dz_v[rows, cols] * mixed * _gelu_grad(up[rows, cols]))
                dwm_ref[g] += _dot(dmixed_b[rows, cols], vn[rows, cols], NT)
                dvn_ref[rows, cols] = _dot(wm_ref[g], dmixed_b[rows, cols], TN)
        dvn = dvn_ref[...]
        dlg_ref[...] += jnp.sum(dvn * vh, axis=0, keepdims=True)
        dlb_ref[...] += jnp.sum(dvn, axis=0, keepdims=True)
        dvh = dvn * lg
        dv = rs * (dvh - jnp.mean(dvh, axis=-1, keepdims=True) - vh * jnp.mean(dvh * vh, axis=-1, keepdims=True))
        duv_ref[:, W:] = _bf(dv * _gelu_grad(vp))

        @pl.when(i == n_steps - 1)
        def _():
            t_idx = lax.broadcasted_iota(jnp.int32, (CHUNK, CHUNK), 0)
            s_idx = lax.broadcasted_iota(jnp.int32, (CHUNK, CHUNK), 1)
            keep = (s_idx <= t_idx).astype(F32)
            for g in range(GROUPS):
                dwm_ref[g] = dwm_ref[g] * keep

    vec = pl.BlockSpec((1, W), lambda i: (0, 0))
    row = pl.BlockSpec((tr, W), lambda i: (i, 0))
    return pl.pallas_call(
        body, name=name, grid=(n_steps,),
        in_specs=[row, pl.BlockSpec((tr, W), lambda i: (i, 1)), row, vec, vec,
                  pl.BlockSpec((GROUPS, CHUNK, CHUNK), lambda i: (0, 0, 0)),
                  pl.BlockSpec((CHUNK, W), lambda i: (0, 0))],
        out_specs=[pl.BlockSpec((tr, W2), lambda i: (i, 0)),
                   pl.BlockSpec((GROUPS, CHUNK, CHUNK), lambda i: (0, 0, 0)),
                   pl.BlockSpec((CHUNK, W), lambda i: (0, 0)), vec, vec],
        out_shape=[jax.ShapeDtypeStruct((S, W2), BF16), jax.ShapeDtypeStruct((GROUPS, CHUNK, CHUNK), F32),
                   jax.ShapeDtypeStruct((CHUNK, W), F32), jax.ShapeDtypeStruct((1, W), F32),
                   jax.ShapeDtypeStruct((1, W), F32)],
        scratch_shapes=[pltpu.VMEM((tr, W), F32)],
        compiler_params=_params(("arbitrary",)),
    )(uvp, uvp, dz, ln_g, ln_b, wm, bs_full)


def _t5_bucket(distance):
    small = distance < MAX_EXACT
    nf = jnp.maximum(distance, 1).astype(F32)
    large = MAX_EXACT + (jnp.log(nf / MAX_EXACT) / math.log(REL_MAX_DISTANCE / MAX_EXACT)
                         * (N_BUCKETS - MAX_EXACT)).astype(jnp.int32)
    large = jnp.minimum(large, N_BUCKETS - 1)
    return jnp.where(small, distance, large)


TILE_ELEMS = 2 * CHUNK * CHUNK


def _band_buckets():
    rel = CHUNK + jnp.arange(CHUNK)[None, :] - jnp.arange(2 * CHUNK)[:, None]
    band = (rel >= 0) & (rel <= CHUNK)
    buckets = [_t5_bucket(jnp.clip(rel, 0, CHUNK) * d) for d in DILATIONS]
    return jnp.stack(buckets), band


def _bucket_onehot():
    buckets, _ = _band_buckets()
    return (buckets.reshape(N_DIL, 1, TILE_ELEMS) == jnp.arange(N_BUCKETS)[None, :, None]).astype(F32)


def _bias_tiles(name, rel_bias, after=None):
    _, band = _band_buckets()
    own = band & (jnp.arange(2 * CHUNK) >= CHUNK)[:, None]
    masks = jnp.stack([own, band]).reshape(2, TILE_ELEMS).astype(F32)
    tables = jnp.transpose(rel_bias.reshape(N_BUCKETS, N_DIL, ATT_HEADS), (1, 2, 0))
    after_args, after_specs = _after_operand(after)

    def body(t_ref, oh_ref, m_ref, *rest):
        out_ref = rest[-1]
        for g in range(N_DIL):
            bias = lax.dot_general(t_ref[g], oh_ref[g], (NN, ((), ())), precision=lax.Precision.HIGHEST,
                                   preferred_element_type=F32)
            for f in range(2):
                out_ref[g, f] = jnp.where(m_ref[f:f + 1, :] > 0.5, bias, NEG_INF)

    whole = pl.BlockSpec(memory_space=pltpu.VMEM)
    out = pl.pallas_call(
        body, name=name, out_shape=jax.ShapeDtypeStruct((N_DIL, 2, ATT_HEADS, TILE_ELEMS), F32),
        in_specs=[whole, whole, whole] + after_specs, out_specs=whole,
        compiler_params=_params(),
    )(tables, _bucket_onehot(), masks, *after_args)
    out = out.reshape(N_DIL, 2, ATT_HEADS // 2, 2, 2 * CHUNK, CHUNK)
    return jnp.transpose(out, (0, 1, 2, 4, 3, 5)).reshape(N_DIL, 2, ATT_HEADS // 2, 2 * CHUNK, 2 * CHUNK)


def _att_specs(order):
    def spec(part, prev):
        def index(*ids):
            hp, g, c = order(*ids)
            return (jnp.maximum(c - 1, 0) if prev else c, part * 3 * 4 + g * 4 + hp)
        return pl.BlockSpec((ATT_ROWS, LANES), index)
    return [spec(0, False), spec(1, False), spec(1, True), spec(2, False), spec(2, True)]


def _rows(start, d):
    if d == 1:
        return pl.ds(pl.multiple_of(start, CHUNK), CHUNK)
    return pl.ds(start, CHUNK, stride=d)


def _att_tile_offsets(t, d):
    n = t // d
    r = t % d
    return n * (CHUNK * d) + r, n


def _head_pair_columns(x_t):
    zeros = jnp.zeros((HEAD_DIM, CHUNK), x_t.dtype)
    return jnp.concatenate([jnp.concatenate([x_t[:HEAD_DIM], zeros], axis=0),
                            jnp.concatenate([zeros, x_t[HEAD_DIM:]], axis=0)], axis=1)


def _head_pair_rows(y):
    return jnp.concatenate([y[:HEAD_DIM, :CHUNK], y[HEAD_DIM:, CHUNK:]], axis=0)


def _stage_prev_cur(dst, prev_ref, cur_ref):
    dst[0:ATT_ROWS, :] = prev_ref[...]
    dst[ATT_ROWS:2 * ATT_ROWS, :] = cur_ref[...]


def _att_fwd(name, qkv, bias_tiles):
    S = qkv.shape[0]
    n_chunks = S // ATT_ROWS
    tiles = ATT_ROWS // CHUNK

    def body(q_ref, kc_ref, kp_ref, vc_ref, vp_ref, b_ref, o_ref, l_ref, kk, vv):
        c = pl.program_id(1)
        g = pl.program_id(2)
        _stage_prev_cur(kk, kp_ref, kc_ref)
        _stage_prev_cur(vv, vp_ref, vc_ref)
        zeros = jnp.zeros((HEAD_DIM, CHUNK), BF16)

        for gi, d in enumerate(DILATIONS):
            @pl.when(g == gi)
            def _(d=d):
                span = CHUNK * d

                def tile(t, carry):
                    q0, n = _att_tile_offsets(t, d)
                    rows = _rows(q0, d)
                    cur = _rows(ATT_ROWS + q0, d)
                    prev = _rows(ATT_ROWS + q0 - span, d)
                    inner = jnp.where((c == 0) & (n == 0), 0, 1)
                    q_t = _bf(q_ref[rows, :] * ATT_SCALE).T
                    k2 = _bf(jnp.concatenate([kk[prev, :], kk[cur, :]], axis=0))
                    v_t = _bf(jnp.concatenate([vv[prev, :], vv[cur, :]], axis=0)).T
                    o_parts, l_parts = [], []
                    for hh in range(2):
                        half = slice(hh * HEAD_DIM, (hh + 1) * HEAD_DIM)
                        q_h = jnp.concatenate([q_t[half], zeros] if hh == 0 else [zeros, q_t[half]], axis=0)
                        s = _dot(k2, q_h, NN) + b_ref[inner, :, hh * CHUNK:(hh + 1) * CHUNK]
                        m = jnp.max(s, axis=0, keepdims=True)
                        p = jnp.exp(s - m)
                        l = jnp.sum(p, axis=0, keepdims=True)
                        o_parts.append(_dot(v_t, _bf(p), NN)[half] / l)
                        l_parts.append(jnp.broadcast_to(m + jnp.log(l), (HEAD_DIM, CHUNK)))
                    o_ref[rows, :] = jnp.concatenate(o_parts, axis=0).T
                    l_ref[rows, :] = jnp.concatenate(l_parts, axis=0).T
                    return carry

                lax.fori_loop(0, tiles, tile, 0, unroll=4)

    order = lambda hp, c, g: (hp, g, c)
    out_spec = pl.BlockSpec((None, ATT_ROWS, LANES), lambda hp, c, g: (g, c, hp))
    shape = jax.ShapeDtypeStruct((N_DIL, S, ATT_WIDTH), F32)
    return pl.pallas_call(
        body, name=name, grid=(ATT_HEADS // 2, n_chunks, N_DIL),
        in_specs=_att_specs(order) + [
            pl.BlockSpec((None, 2, None, 2 * CHUNK, 2 * CHUNK), lambda hp, c, g: (g, 0, hp, 0, 0))],
        out_specs=[out_spec, out_spec],
        out_shape=[shape, shape],
        scratch_shapes=[pltpu.VMEM((2 * ATT_ROWS, LANES), F32), pltpu.VMEM((2 * ATT_ROWS, LANES), F32)],
        compiler_params=_params(("parallel", "parallel", "parallel")),
    )(qkv, qkv, qkv, qkv, qkv, bias_tiles)


def _att_merge(name, o_g, l_g, tm=512):
    _, S, W = o_g.shape

    def body(o_ref, l_ref, out_ref, lse_ref):
        ls = [l_ref[g] for g in range(N_DIL)]
        mx = functools.reduce(jnp.maximum, ls)
        ws = [jnp.exp(l - mx) for l in ls]
        tot = functools.reduce(lambda a, b: a + b, ws)
        acc = ws[0] * o_ref[0]
        for g in range(1, N_DIL):
            acc = acc + ws[g] * o_ref[g]
        out_ref[...] = acc / tot
        lse_ref[...] = mx + jnp.log(tot)

    blk = pl.BlockSpec((N_DIL, tm, W), lambda i: (0, i, 0))
    row = pl.BlockSpec((tm, W), lambda i: (i, 0))
    shape = jax.ShapeDtypeStruct((S, W), F32)
    return pl.pallas_call(
        body, name=name, grid=(S // tm,), in_specs=[blk, blk], out_specs=[row, row], out_shape=[shape, shape],
        compiler_params=_params(("parallel",)),
    )(o_g, l_g)


def _att_bwd(name, qkv, o, lse, d_o, bias_tiles):
    S = qkv.shape[0]
    n_chunks = S // ATT_ROWS
    tiles = ATT_ROWS // CHUNK

    def body(q_ref, kc_ref, kp_ref, vc_ref, vp_ref, o_ref, l_ref, do_ref, b_ref, dq_ref, dk_ref, dv_ref, ds_ref,
             kk, vv):
        g = pl.program_id(1)
        c = pl.program_id(2)

        @pl.when(c == 0)
        def _():
            dk_ref[...] = jnp.zeros_like(dk_ref)
            dv_ref[...] = jnp.zeros_like(dv_ref)
            ds_ref[...] = jnp.zeros_like(ds_ref)

        _stage_prev_cur(kk, kp_ref, kc_ref)
        _stage_prev_cur(vv, vp_ref, vc_ref)
        base = c * ATT_ROWS
        head0 = lax.broadcasted_iota(jnp.int32, (CHUNK, LANES), 1) < HEAD_DIM

        def head_pair_stack(x):
            zero = jnp.zeros_like(x)
            return jnp.concatenate([jnp.where(head0, x, zero), jnp.where(head0, zero, x)], axis=0)

        for gi, d in enumerate(DILATIONS):
            @pl.when(g == gi)
            def _(d=d):
                span = CHUNK * d

                def tile(t, carry):
                    q0, n = _att_tile_offsets(t, d)
                    rows = _rows(q0, d)
                    cur = _rows(ATT_ROWS + q0, d)
                    prev = _rows(ATT_ROWS + q0 - span, d)
                    first = (c == 0) & (n == 0)
                    inner = jnp.where(first, 0, 1)
                    g_cur = _rows(base + q0, d)
                    g_prev = _rows(jnp.where(first, q0, base + q0 - span), d)
                    q2 = _bf(q_ref[rows, :] * ATT_SCALE)
                    q_t = q2.T
                    k2 = _bf(jnp.concatenate([kk[prev, :], kk[cur, :]], axis=0))
                    k_t = k2.T
                    v2 = _bf(jnp.concatenate([vv[prev, :], vv[cur, :]], axis=0))
                    do2 = do_ref[rows, :]
                    do_b = _bf(do2)
                    do_t = do_b.T
                    lse_t = l_ref[rows, :].T
                    dd_t = (do2 * o_ref[rows, :]).T
                    lse = jnp.concatenate([lse_t[0:1], lse_t[HEAD_DIM:HEAD_DIM + 1]], axis=1)
                    delta = jnp.concatenate([jnp.sum(dd_t[:HEAD_DIM], axis=0, keepdims=True),
                                             jnp.sum(dd_t[HEAD_DIM:], axis=0, keepdims=True)], axis=1)
                    s = _dot(k2, _head_pair_columns(q_t), NN) + b_ref[inner]
                    p = jnp.exp(s - lse)
                    ds = p * (_dot(v2, _head_pair_columns(do_t), NN) - delta)
                    ds_ref[...] += ds
                    ds_b = _bf(ds)
                    dq_t = _head_pair_rows(_dot(k_t, ds_b, NN))
                    dk2 = _dot(ds_b, head_pair_stack(q2), NN)
                    dv2 = _dot(_bf(p), head_pair_stack(do_b), NN)
                    dq_ref[rows, :] = (dq_t * ATT_SCALE).T
                    dk_ref[g_prev, :] += dk2[0:CHUNK]
                    dk_ref[g_cur, :] += dk2[CHUNK:2 * CHUNK]
                    dv_ref[g_prev, :] += dv2[0:CHUNK]
                    dv_ref[g_cur, :] += dv2[CHUNK:2 * CHUNK]
                    return carry

                lax.fori_loop(0, tiles, tile, 0, unroll=4)

    order = lambda hp, g, c: (hp, g, c)
    chunk = pl.BlockSpec((ATT_ROWS, LANES), lambda hp, g, c: (c, hp))
    slab = pl.BlockSpec((S, LANES), lambda hp, g, c: (0, g * 4 + hp))
    width = N_DIL * ATT_WIDTH
    dq, dk, dv, ds_sums = pl.pallas_call(
        body, name=name, grid=(ATT_HEADS // 2, N_DIL, n_chunks),
        in_specs=_att_specs(order) + [chunk, chunk, chunk,
                                      pl.BlockSpec((None, 2, None, 2 * CHUNK, 2 * CHUNK),
                                                   lambda hp, g, c: (g, 0, hp, 0, 0))],
        out_specs=[pl.BlockSpec((ATT_ROWS, LANES), lambda hp, g, c: (c, g * 4 + hp)), slab, slab,
                   pl.BlockSpec((None, None, 2 * CHUNK, 2 * CHUNK), lambda hp, g, c: (g, hp, 0, 0))],
        out_shape=[jax.ShapeDtypeStruct((S, width), F32), jax.ShapeDtypeStruct((S, width), F32),
                   jax.ShapeDtypeStruct((S, width), F32),
                   jax.ShapeDtypeStruct((N_DIL, ATT_HEADS // 2, 2 * CHUNK, 2 * CHUNK), F32)],
        scratch_shapes=[pltpu.VMEM((2 * ATT_ROWS, LANES), F32), pltpu.VMEM((2 * ATT_ROWS, LANES), F32)],
        compiler_params=_params(("parallel", "parallel", "arbitrary")),
    )(qkv, qkv, qkv, qkv, qkv, o, lse, d_o, bias_tiles)
    ds_sums = ds_sums.reshape(N_DIL, ATT_HEADS // 2, 2 * CHUNK, 2, CHUNK)
    ds_sums = jnp.transpose(ds_sums, (0, 1, 3, 2, 4)).reshape(N_DIL, ATT_HEADS, 2 * CHUNK, CHUNK)
    return dq, dk, dv, ds_sums


def _bias_grad(name, ds_sums):
    flat = ds_sums.reshape(N_DIL, ATT_HEADS, TILE_ELEMS)

    def body(oh_ref, ds_ref, out_ref):
        for g in range(N_DIL):
            out_ref[g] = lax.dot_general(oh_ref[g], ds_ref[g], (NT, ((), ())), precision=lax.Precision.HIGHEST,
                                         preferred_element_type=F32)

    out = pl.pallas_call(
        body, name=name, out_shape=jax.ShapeDtypeStruct((N_DIL, N_BUCKETS, ATT_HEADS), F32),
        compiler_params=_params(),
    )(_bucket_onehot(), flat)
    return jnp.transpose(out, (1, 0, 2)).reshape(N_BUCKETS, N_DIL * ATT_HEADS)


def _peers():
    x, y, c = lax.axis_index("x"), lax.axis_index("y"), lax.axis_index("c")
    me = 4 * x + 2 * y + c
    others = [(x, y, 1 - c), (1 - x, y, c), (x, 1 - y, c), (1 - x, 1 - y, c),
              (1 - x, y, 1 - c), (x, 1 - y, 1 - c), (1 - x, 1 - y, 1 - c)]
    return me, others


def _slot(dev):
    return 4 * dev[0] + 2 * dev[1] + dev[2]


_HBM =pl.BlockSpec(memory_space=pltpu.HBM)
_SEM = pl.BlockSpec(memory_space=pltpu.SEMAPHORE)
_EFFECT = pltpu.SideEffectType.DATAFLOW_SIDE_EFFECTING


def _my_slot():
    return 4 * lax.axis_index("x") + 2 * lax.axis_index("y") + lax.axis_index("c")


def _exchange_copy(src_ref, land_ref, send_sems, recv_sems, k, dev, me, scatter, arriving):
    src = src_ref.at[me if arriving else _slot(dev)] if scatter else src_ref
    dst = land_ref.at[_slot(dev) if arriving else me]
    return pltpu.make_async_remote_copy(src_ref=src, dst_ref=dst, send_sem=send_sems.at[k], recv_sem=recv_sems.at[k],
                                        device_id=dev, device_id_type=MESH)


def _exchange_start(name, src, scatter):
    R, W = src.shape[-2:]
    me = _my_slot()
    own = lax.dynamic_index_in_dim(src, me, 0, keepdims=True) if scatter else src[None]
    landing = lax.dynamic_update_slice(lax.empty((N_DEV, R, W), src.dtype), own, (me, 0, 0))

    def body(src_ref, land_ref, send_sems, recv_sems, src_thru, land_thru, token):
        me, others = _peers()
        for k, dev in enumerate(others):
            _exchange_copy(src_ref, land_ref, send_sems, recv_sems, k, dev, me, scatter, False).start()
        token[...] = jnp.zeros_like(token)

    sems = pltpu.SemaphoreType.DMA((N_DEV - 1,))
    send_sems, recv_sems, src_thru, land_thru, token = pl.pallas_call(
        body, name=name,
        out_shape=(sems, sems, pltpu.HBM(src.shape, src.dtype), pltpu.HBM(landing.shape, landing.dtype),
                   jax.ShapeDtypeStruct((8, LANES), F32)),
        in_specs=(_HBM, _HBM), out_specs=(_SEM, _SEM, _HBM, _HBM, pl.BlockSpec(memory_space=pltpu.VMEM)),
        input_output_aliases={0: 2, 1: 3},
        compiler_params=pltpu.CompilerParams(has_side_effects=_EFFECT),
    )(pltpu.with_memory_space_constraint(src, pltpu.HBM), pltpu.with_memory_space_constraint(landing, pltpu.HBM))
    return (send_sems, recv_sems, src_thru, land_thru, scatter), token


def _exchange_wait(name, handle, after):
    send_sems, recv_sems, src_thru, land_thru, scatter = handle

    def body(src_ref, land_ref, send_sems, recv_sems, after_ref, src_dead, got_ref):
        me, others = _peers()
        for k, dev in enumerate(others):
            cp = _exchange_copy(src_ref, land_ref, send_sems, recv_sems, k, dev, me, scatter, True)
            cp.wait_send()
            cp.wait_recv()

    return pl.pallas_call(
        body, name=name,
        out_shape=(pltpu.HBM(src_thru.shape, src_thru.dtype), pltpu.HBM(land_thru.shape, land_thru.dtype)),
        in_specs=(_HBM, _HBM, _SEM, _SEM, pl.BlockSpec(memory_space=pl.ANY)), out_specs=(_HBM, _HBM),
        input_output_aliases={0: 0, 1: 1},
        compiler_params=pltpu.CompilerParams(has_side_effects=_EFFECT),
    )(src_thru, land_thru, send_sems, recv_sems, after)[1]


def _all_reduce_small(name, buf):
    rows = buf.shape[0]
    rb = rows // N_DEV

    def body(x_ref, out_ref, stage, send1, recv1, send2, recv2):
        me, others = _peers()

        def block(ref, k):
            return ref.at[pl.ds(k * rb, rb), :]

        first = [pltpu.make_async_remote_copy(src_ref=block(x_ref, _slot(dev)), dst_ref=stage.at[me],
                                              send_sem=send1.at[k], recv_sem=recv1.at[k], device_id=dev,
                                              device_id_type=MESH) for k, dev in enumerate(others)]
        for cp in first:
            cp.start()
        stage[me] = x_ref[pl.ds(pl.multiple_of(me * rb, 8), rb), :]
        for k, dev in enumerate(others):
            pltpu.make_async_remote_copy(src_ref=block(x_ref, me), dst_ref=stage.at[_slot(dev)],
                                         send_sem=send1.at[k], recv_sem=recv1.at[k], device_id=dev,
                                         device_id_type=MESH).wait_recv()
        total = stage[0]
        for j in range(1, N_DEV):
            total = total + stage[j]
        out_ref[pl.ds(pl.multiple_of(me * rb, 8), rb), :] = total
        second = [pltpu.make_async_remote_copy(src_ref=block(out_ref, me), dst_ref=block(out_ref, me),
                                               send_sem=send2.at[k], recv_sem=recv2.at[k], device_id=dev,
                                               device_id_type=MESH) for k, dev in enumerate(others)]
        for cp in second:
            cp.start()
        for k, dev in enumerate(others):
            pltpu.make_async_remote_copy(src_ref=block(out_ref, me), dst_ref=block(out_ref, _slot(dev)),
                                         send_sem=send2.at[k], recv_sem=recv2.at[k], device_id=dev,
                                         device_id_type=MESH).wait_recv()
        for cp in first + second:
            cp.wait_send()

    sems = pltpu.SemaphoreType.DMA((N_DEV - 1,))
    return pl.pallas_call(
        body, name=name,
        in_specs=[pl.BlockSpec(memory_space=pltpu.VMEM)],
        out_specs=pl.BlockSpec(memory_space=pltpu.VMEM),
        out_shape=jax.ShapeDtypeStruct(buf.shape, F32),
        scratch_shapes=[pltpu.VMEM((N_DEV, rb, LANES), F32), sems, sems, sems, sems],
        compiler_params=pltpu.CompilerParams(vmem_limit_bytes=VMEM_LIMIT_BYTES),
    )(buf)


def _adamw_math(w, g, m, v):
    m = ADAM_B1 * m + (1.0 - ADAM_B1) * g
    v = ADAM_B2 * v + (1.0 - ADAM_B2) * (g * g)
    m_hat = m / (1.0 - ADAM_B1 ** ADAM_STEP)
    v_hat = v / (1.0 - ADAM_B2 ** ADAM_STEP)
    delta = -ADAM_LR * (m_hat / (jnp.sqrt(v_hat) + ADAM_EPS) + ADAM_WD * w)
    return delta, m, v


def _adamw(name, parts, w, m, v, tr=128):
    P, R, W = parts.shape
    tr = min(tr, R)

    def body(p_ref, w_ref, m_ref, v_ref, g_out, d_out, m_out, v_out):
        g = p_ref[0].astype(F32)
        for j in range(1, P):
            g = g + p_ref[j].astype(F32)
        delta, m_new, v_new = _adamw_math(w_ref[...], g, m_ref[...], v_ref[...])
        g_out[...] = g
        d_out[...] = delta
        m_out[...] = m_new
        v_out[...] = v_new

    row = pl.BlockSpec((tr, W), lambda i: (i, 0))
    shape = jax.ShapeDtypeStruct((R, W), F32)
    return pl.pallas_call(
        body, name=name, grid=(R // tr,),
        in_specs=[pl.BlockSpec((P, tr, W), lambda i: (0, i, 0)), row, row, row],
        out_specs=[row, row, row, row],
        out_shape=[shape, shape, shape, shape],
        compiler_params=_params(("parallel",)),
    )(parts, w, m, v)


def _col_shards(full, n_local):
    K = full.shape[0]
    t = jnp.transpose(full.reshape(K, N_DEV, n_local), (1, 0, 2))
    return t.reshape(N_DEV, K * n_local // PACK_W, PACK_W)


def _from_col_shards(slots, K, n_local):
    t = slots.reshape(N_DEV, K, n_local)
    return jnp.transpose(t, (1, 0, 2)).reshape(K, N_DEV * n_local)


def _pack_rows(a):
    return a.reshape(-1, PACK_W)


_SMALL = ("mix_norm_g", "mlp_norm_g", "final_norm_g", "a_ln_g", "a_ln_b", "a_w_s", "a_b_s", "rel_bias")


def _pack_small(vals):
    pieces = []
    for n in _SMALL:
        flat = vals[n].reshape(-1)
        pad = (-flat.shape[0]) % (8 * LANES)
        pieces.append(jnp.pad(flat, (0, pad)).reshape(-1, LANES))
    rows = sum(p.shape[0] for p in pieces)
    tail = (-rows) % (8 * N_DEV)
    if tail:
        pieces.append(jnp.zeros((tail, LANES), F32))
    return jnp.concatenate(pieces, axis=0)


def _unpack_small(buf, like):
    out = {}
    r = 0
    for n in _SMALL:
        size = like[n].size
        nrows = -(-size // (8 * LANES)) * 8
        out[n] = buf[r:r + nrows].reshape(-1)[:size].reshape(like[n].shape)
        r += nrows
    return out


_STAGES = (("gate", (("a_w_in", 0), ("a_w_out", 0))),
           ("mlp0", (("w_up", 0), ("w_down", 0))),
           ("att", (("b_w_qkv", 0), ("b_w_out", 0))),
           ("mlp1", (("w_up", 1), ("w_down", 1))))
_COL_SHARDED = ("a_w_in", "b_w_qkv", "b_w_out", "w_up")


def _pack_stage(t, pieces):
    return jnp.concatenate([_pack_rows(t[n][i]) for n, i in pieces], axis=0)


def kernel(x, mix_norm_g, mlp_norm_g, final_norm_g, a_w_in, a_ln_g, a_ln_b, a_w_s, a_b_s, a_w_out, b_w_qkv, b_w_out, rel_bias, w_up, w_down, loss_target, m_mix_norm_g, m_mlp_norm_g, m_final_norm_g, m_a_w_in, m_a_ln_g, m_a_ln_b, m_a_w_s, m_a_b_s, m_a_w_out, m_b_w_qkv, m_b_w_out, m_rel_bias, m_w_up, m_w_down, v_mix_norm_g, v_mlp_norm_g, v_final_norm_g, v_a_w_in, v_a_ln_g, v_a_ln_b, v_a_w_s, v_a_b_s, v_a_w_out, v_b_w_qkv, v_b_w_out, v_rel_bias, v_w_up, v_w_down):
    w = dict(mix_norm_g=mix_norm_g, mlp_norm_g=mlp_norm_g, final_norm_g=final_norm_g, a_w_in=a_w_in, a_ln_g=a_ln_g,
             a_ln_b=a_ln_b, a_w_s=a_w_s, a_b_s=a_b_s, a_w_out=a_w_out, b_w_qkv=b_w_qkv, b_w_out=b_w_out,
             rel_bias=rel_bias, w_up=w_up, w_down=w_down)
    m = dict(mix_norm_g=m_mix_norm_g, mlp_norm_g=m_mlp_norm_g, final_norm_g=m_final_norm_g, a_w_in=m_a_w_in,
             a_ln_g=m_a_ln_g, a_ln_b=m_a_ln_b, a_w_s=m_a_w_s, a_b_s=m_a_b_s, a_w_out=m_a_w_out, b_w_qkv=m_b_w_qkv,
             b_w_out=m_b_w_out, rel_bias=m_rel_bias, w_up=m_w_up, w_down=m_w_down)
    v = dict(mix_norm_g=v_mix_norm_g, mlp_norm_g=v_mlp_norm_g, final_norm_g=v_final_norm_g, a_w_in=v_a_w_in,
             a_ln_g=v_a_ln_g, a_ln_b=v_a_ln_b, a_w_s=v_a_w_s, a_b_s=v_a_b_s, a_w_out=v_a_w_out, b_w_qkv=v_b_w_qkv,
             b_w_out=v_b_w_out, rel_bias=v_rel_bias, w_up=v_w_up, w_down=v_w_down)

    stages = dict(_STAGES)
    order = [s for s, _ in _STAGES]

    def full_weights(pieces, gathered):
        out, r = [], 0
        for n, i in pieces:
            shard = w[n][i]
            rows = shard.size // PACK_W
            seg = gathered[:, r:r + rows]
            r += rows
            if n in _COL_SHARDED:
                out.append(_from_col_shards(seg, shard.shape[0], shard.shape[1]))
            else:
                out.append(seg.reshape(N_DEV * shard.shape[0], shard.shape[1]))
        return out

    pending = {}
    pending[order[0]], first_token = _exchange_start("gather_" + order[0] + "_start",
                                                     _bf(_pack_stage(w, stages[order[0]])), False)

    def get_weights(stage, dep):
        gathered = _exchange_wait("gather_" + stage + "_wait", pending.pop(stage), dep)
        nxt = order.index(stage) + 1
        token = None
        if nxt < len(order):
            shard = _bf(_pack_stage(w, stages[order[nxt]]))
            shard, gathered = lax.optimization_barrier((shard, gathered))
            pending[order[nxt]], token = _exchange_start("gather_" + order[nxt] + "_start", shard, False)
        return full_weights(stages[stage], gathered), token

    sent = {}

    def put_grads(stage, grads):
        parts = []
        for n, i in stages[stage]:
            shard = w[n][i]
            if n in _COL_SHARDED:
                parts.append(_col_shards(grads[n], shard.shape[1]))
            else:
                parts.append(grads[n].reshape(N_DEV, -1, PACK_W))
        sent[stage], token = _exchange_start("scatter_" + stage + "_start", _bf(jnp.concatenate(parts, axis=1)), True)
        return token

    loss_local, grad_x, small_g = _local_step(
        x[0], loss_target[0], mix_norm_g, mlp_norm_g, final_norm_g, a_ln_g, a_ln_b, a_w_s, a_b_s, rel_bias,
        get_weights, put_grads, first_token)

    new = {}
    for stage in reversed(order):
        received = _exchange_wait("scatter_" + stage + "_wait", sent[stage], grad_x)
        bufs = _adamw("adamw_" + stage, received, *[_pack_stage(t, stages[stage]) for t in (w, m, v)])
        r = 0
        for n, i in stages[stage]:
            rows = w[n][i].size // PACK_W
            new[n, i] = [b[r:r + rows].reshape(w[n][i].shape) for b in bufs]
            r += rows

    reduced = _all_reduce_small("reduce_small", _pack_small(small_g))
    small = [_unpack_small(b, w) for b in _adamw("adamw_small", reduced[None], _pack_small(w), _pack_small(m),
                                                 _pack_small(v), tr=reduced.shape[0])]

    outs = []
    for j in range(4):
        for n in w:
            if n in _SMALL:
                outs.append(small[j][n])
            else:
                outs.append(jnp.stack([new[n, i][j] for i in range(w[n].shape[0])]))
    loss = lax.psum(loss_local, ("x", "y", "c"))
    return (loss, grad_x[None], *outs)


def _local_step(xs, tgt, mix_norm_g, mlp_norm_g, final_norm_g, a_ln_g, a_ln_b, a_w_s, a_b_s, rel_bias,
                get_weights, put_grads, first_token=None):
    D = xs.shape[-1]
    g_mix = [mix_norm_g[l][None, :] for l in range(2)]
    g_mlp = [mlp_norm_g[l][None, :] for l in range(2)]
    g_fin = final_norm_g[None, :]
    ln_g, ln_b = a_ln_g, a_ln_b
    causal = jnp.tril(jnp.ones((CHUNK, CHUNK), dtype=bool))
    wm = _bf(jnp.where(causal[None], a_w_s[0], 0.0))
    bs_full = jnp.repeat(a_b_s[0].T, D // GROUPS, axis=1)
    bias_tiles = _bias_tiles("att_bias", rel_bias, after=first_token)

    (win, wout), token = get_weights("gate", bias_tiles)
    y0 = _rms_fwd("rms_mix0", xs, g_mix[0], after=token)
    uvp = _mm_nn("gate_in", y0, win, tm=512, nc=512)
    z = _gate_fwd("gate_mid", uvp, ln_g, ln_b, wm, bs_full)
    h1 = _mm_nn("gate_out", z, wout, tm=512, nc=512, epi="res", extra=xs)
    (wup0, wdn0), token = get_weights("mlp0", h1)
    y1 = _rms_fwd("rms_mlp0", h1, g_mlp[0], after=token)
    a0, f0 = _mm_nn("mlp0_up", y1, wup0, tm=512, nc=512, epi="relu2")
    h2 = _mm_nn("mlp0_down", f0, wdn0, tm=512, nc=512, epi="res", extra=h1)
    (wqkv, wo), token = get_weights("att", h2)
    y2 = _rms_fwd("rms_mix1", h2, g_mix[1], after=token)
    qkv = _mm_nn("att_qkv", y2, wqkv, tm=256, nc=512)
    o_att, lse = _att_merge("att_merge", *_att_fwd("att_fwd", qkv, bias_tiles))
    h3 = _mm_nn("att_out", o_att, wo, tm=512, nc=512, epi="res", extra=h2)
    (wup1, wdn1), token = get_weights("mlp1", h3)
    y3 = _rms_fwd("rms_mlp1", h3, g_mlp[1], after=token)
    a1, f1 = _mm_nn("mlp1_up", y3, wup1, tm=512, nc=512, epi="relu2")
    h4 = _mm_nn("mlp1_down", f1, wdn1, tm=512, nc=512, epi="res", extra=h3)
    dh, dg_fin, err2 = _final_loss("final_loss", h4, g_fin, tgt)
    loss_local = 0.5 * jnp.sum(err2) / D

    def mlp_bwd(tag, dh, h_in, y, a, f, wup_l, wdn_l, g_row, after):
        da = _mm_nt(tag + "_dact", dh, wdn_l, tm=512, nc=512, epi="mask2relu", extra=a, after=after)
        g_dn = _mm_tn(tag + "_dwdown", f, dh, t1=1024, tn=1024)
        g_up = _mm_tn(tag + "_dwup", y, da, t1=1024, tn=1024)
        dh_in, dg = _mm_nt_rms_bwd(tag + "_dy", [(da, wup_l, wup_l.shape, (0, 0))], h_in, g_row, dh, tm=512, nc=512)
        return dh_in, dg, put_grads(tag, dict(w_up=g_up, w_down=g_dn))

    dh3, dg_mlp1, token = mlp_bwd("mlp1", dh, h3, y3, a1, f1, wup1, wdn1, g_mlp[1], None)

    d_o = _mm_nt("att_dout", dh3, wo, tm=512, nc=512, after=token)
    g_wo = _mm_tn("att_dwo", o_att, dh3, t1=512, tn=1024)
    dq, dk, dv, ds_sums = _att_bwd("att_bwd", qkv, o_att, lse, d_o, bias_tiles)
    part_w = N_DIL * ATT_WIDTH
    g_qkv = [_mm_tn("att_dwqkv%d" % p, y2, t, t1=1024, tn=part_w, tm=1024) for p, t in enumerate((dq, dk, dv))]
    dh2, dg_mix1 = _mm_nt_rms_bwd("att_dy", [(t, wqkv, (D, part_w), (0, p)) for p, t in enumerate((dq, dk, dv))],
                                  h2, g_mix[1], dh3, tm=256, nc=512)
    token = put_grads("att", dict(b_w_qkv=jnp.concatenate(g_qkv, axis=1), b_w_out=g_wo))

    dh1, dg_mlp0, token = mlp_bwd("mlp0", dh2, h1, y1, a0, f0, wup0, wdn0, g_mlp[0], token)

    dz = _mm_nt("gate_dz", dh1, wout, tm=512, nc=512, after=token)
    g_wout = _mm_tn("gate_dwout", z, dh1, t1=1024, tn=1024)
    duvp, d_wm, d_mixed, d_lng, d_lnb = _gate_bwd("gate_dmid", uvp, dz, ln_g, ln_b, wm, bs_full)
    g_win = _mm_tn("gate_dwin", y0, duvp, t1=1024, tn=1024)
    token = put_grads("gate", dict(a_w_in=g_win, a_w_out=g_wout))
    grad_x, dg_mix0 = _mm_nt_rms_bwd("gate_dy", [(duvp, win, win.shape, (0, 0))], xs, g_mix[0], dh1, tm=512, nc=512,
                                     after=token)

    small_g = dict(
        mix_norm_g=jnp.concatenate([dg_mix0, dg_mix1], axis=0),
        mlp_norm_g=jnp.concatenate([dg_mlp0, dg_mlp1], axis=0),
        final_norm_g=dg_fin[0], a_ln_g=d_lng, a_ln_b=d_lnb, a_w_s=d_wm[None],
        a_b_s=jnp.sum(d_mixed.reshape(CHUNK, GROUPS, D // GROUPS), axis=2).T[None],
        rel_bias=_bias_grad("att_dbias", ds_sums))
    return loss_local, grad_x, small_g
```

```python
import functools
import math

import jax
import jax.numpy as jnp
from jax import lax
from jax.experimental import pallas as pl
from jax.experimental.pallas import tpu as pltpu

F32 = jnp.float32
BF16 = jnp.bfloat16
MESH = pl.DeviceIdType.MESH

N_DEV = 8
EPS = 1e-6
NEG_INF = -1e30
CHUNK = 128
GROUPS = 8
HEAD_DIM = 64
ATT_HEADS = 8
ATT_WIDTH = ATT_HEADS * HEAD_DIM
DILATIONS = (1, 4, 16)
N_DIL = len(DILATIONS)
N_BUCKETS = 32
MAX_EXACT = N_BUCKETS // 2
REL_MAX_DISTANCE = 2048
ATT_ROWS = 2048
ATT_SCALE = HEAD_DIM ** -0.5
LANES = 128

ADAM_LR = 0.001
ADAM_B1 = 0.9
ADAM_B2 = 0.999
ADAM_EPS = 1e-08
ADAM_WD = 0.01
ADAM_STEP = 10

VMEM_LIMIT_BYTES = 56 * 1024 * 1024


def _params(semantics=None):
    return pltpu.CompilerParams(dimension_semantics=semantics, vmem_limit_bytes=VMEM_LIMIT_BYTES)


def _bf(v):
    return v.astype(BF16)


def _dot(a, b, dims):
    return lax.dot_general(a, b, (dims, ((), ())), preferred_element_type=F32)


NN = ((1,), (0,))
NT = ((1,), (1,))
TN = ((0,), (0,))


def _after_operand(after):
    if after is None:
        return [], []
    return [after], [pl.BlockSpec(memory_space=pl.ANY)]


def _rms_fwd(name, x, g, tm=512, after=None):
    S, D = x.shape
    after_args, after_specs = _after_operand(after)

    def body(x_ref, g_ref, *rest):
        y_ref = rest[-1]
        xv = x_ref[...]
        r = lax.rsqrt(jnp.mean(xv * xv, axis=-1, keepdims=True) + EPS)
        y_ref[...] = _bf(xv * r * g_ref[...])

    return pl.pallas_call(
        body, name=name, grid=(S // tm,),
        in_specs=[pl.BlockSpec((tm, D), lambda i: (i, 0)), pl.BlockSpec((1, D), lambda i: (0, 0))] + after_specs,
        out_specs=pl.BlockSpec((tm, D), lambda i: (i, 0)),
        out_shape=jax.ShapeDtypeStruct((S, D), BF16),
        compiler_params=_params(("parallel",)),
    )(x, g, *after_args)


def _final_loss(name, h, g, target, tm=512):
    S, D = h.shape

    def body(h_ref, g_ref, t_ref, dh_ref, dg_ref, l_ref):
        i = pl.program_id(0)
        xv = h_ref[...]
        r = lax.rsqrt(jnp.mean(xv * xv, axis=-1, keepdims=True) + EPS)
        xh = xv * r
        gv = g_ref[...]
        e = xh * gv - t_ref[...]
        dout = e / D
        dyg = dout * gv
        c = jnp.mean(dyg * xh, axis=-1, keepdims=True)
        dh_ref[...] = r * (dyg - xh * c)
        dg_part = jnp.sum(dout * xh, axis=0, keepdims=True)
        l_part = jnp.sum(e * e, axis=0, keepdims=True)

        @pl.when(i == 0)
        def _():
            dg_ref[...] = dg_part
            l_ref[...] = l_part

        @pl.when(i > 0)
        def _():
            dg_ref[...] += dg_part
            l_ref[...] += l_part

    row = pl.BlockSpec((tm, D), lambda i: (i, 0))
    vec = pl.BlockSpec((1, D), lambda i: (0, 0))
    return pl.pallas_call(
        body, name=name, grid=(S // tm,),
        in_specs=[row, vec, row],
        out_specs=[row, vec, vec],
        out_shape=[jax.ShapeDtypeStruct((S, D), F32), jax.ShapeDtypeStruct((1, D), F32),
                   jax.ShapeDtypeStruct((1, D), F32)],
        compiler_params=_params(("arbitrary",)),
    )(h, g, target)


def _chunk_product(a_vals, w_refs, j, nc, nt, shards):
    cols = slice(j * nc, (j + 1) * nc)
    acc = None
    for a_v, w_ref in zip(a_vals, w_refs):
        if not shards:
            terms = [_dot(a_v, w_ref[cols, :], NT) if nt else _dot(a_v, w_ref[:, cols], NN)]
        elif nt:
            nl = w_ref.shape[2]
            terms = [_dot(a_v[:, k * nl:(k + 1) * nl], w_ref[k, cols, :], NT) for k in range(N_DEV)]
        else:
            terms = [_dot(a_v, w_ref[j], NN)]
        for t in terms:
            acc = t if acc is None else acc + t
    return cols, acc


def _mm_rows(name, pairs, n_out, *, nt, tm, nc, epi="plain", extra=None, out_dtype=F32, after=None, shards=False):
    M = pairs[0][0].shape[0]
    np_ = len(pairs)
    after_args, after_specs = _after_operand(after)

    def body(*refs):
        a_refs = refs[:np_]
        w_refs = refs[np_:2 * np_]
        pos = 2 * np_
        e_ref = None
        if extra is not None:
            e_ref = refs[pos]
            pos += 1
        pos += len(after_args)
        outs = refs[pos:]
        a_vals = [_bf(a[...]) for a in a_refs]
        for j in range(n_out // nc):
            cols, acc = _chunk_product(a_vals, w_refs, j, nc, nt, shards)
            if epi == "plain":
                outs[0][:, cols] = acc.astype(out_dtype)
            elif epi == "res":
                outs[0][:, cols] = e_ref[:, cols] + acc
            elif epi == "relu2":
                outs[0][:, cols] = _bf(acc)
                rl = jnp.maximum(acc, 0.0)
                outs[1][:, cols] = _bf(rl * rl)
            elif epi == "mask2relu":
                outs[0][:, cols] = _bf(acc * (2.0 * jnp.maximum(e_ref[:, cols].astype(F32), 0.0)))

    in_specs = [pl.BlockSpec((tm, a.shape[1]), lambda i: (i, 0)) for a, _, _, _ in pairs]
    for _, _, wshape, widx in pairs:
        in_specs.append(pl.BlockSpec(wshape, functools.partial(lambda i, widx: widx, widx=widx)))
    args = [a for a, _, _, _ in pairs] + [w for _, w, _, _ in pairs]
    if extra is not None:
        in_specs.append(pl.BlockSpec((tm, n_out), lambda i: (i, 0)))
        args.append(extra)
    in_specs += after_specs
    args += after_args
    row_out = pl.BlockSpec((tm, n_out), lambda i: (i, 0))
    if epi == "relu2":
        out_specs = [row_out, row_out]
        out_shape = [jax.ShapeDtypeStruct((M, n_out), BF16), jax.ShapeDtypeStruct((M, n_out), BF16)]
    else:
        dt = BF16 if epi == "mask2relu" else (F32 if epi == "res" else out_dtype)
        out_specs = row_out
        out_shape = jax.ShapeDtypeStruct((M, n_out), dt)
    return pl.pallas_call(
        body, name=name, grid=(M // tm,), in_specs=in_specs, out_specs=out_specs, out_shape=out_shape,
        compiler_params=_params(("parallel",)),
    )(*args)


def _whole(w):
    return w.shape, (0,) * w.ndim


def _mm_nn(name, a, w, **kw):
    n_out = w.shape[0] * w.shape[2] if w.ndim == 3 else w.shape[1]
    return _mm_rows(name, [(a, w, *_whole(w))], n_out, nt=False, **kw)


def _mm_nt(name, a, w, **kw):
    return _mm_rows(name, [(a, w, *_whole(w))], w.shape[0], nt=True, **kw)


def _mm_nt_rms_bwd(name, pairs, x, g, dres, *, tm, nc, after=None, shards=False):
    M, D = x.shape
    np_ = len(pairs)
    after_args, after_specs = _after_operand(after)

    def body(*refs):
        a_refs = refs[:np_]
        w_refs = refs[np_:2 * np_]
        x_ref, g_ref, r_ref = refs[2 * np_:2 * np_ + 3]
        dx_ref, dg_ref, dy_sc = refs[-3:]
        i = pl.program_id(0)
        a_vals = [_bf(a[...]) for a in a_refs]
        for j in range(D // nc):
            cols, acc = _chunk_product(a_vals, w_refs, j, nc, True, shards)
            dy_sc[:, cols] = acc
        xv = x_ref[...]
        r = lax.rsqrt(jnp.mean(xv * xv, axis=-1, keepdims=True) + EPS)
        xh = xv * r
        dy_v = dy_sc[...]
        dyg = dy_v * g_ref[...]
        c = jnp.mean(dyg * xh, axis=-1, keepdims=True)
        dx_ref[...] = r_ref[...] + r * (dyg - xh * c)
        part = jnp.sum(dy_v * xh, axis=0, keepdims=True)

        @pl.when(i == 0)
        def _():
            dg_ref[...] = part

        @pl.when(i > 0)
        def _():
            dg_ref[...] += part

    row = pl.BlockSpec((tm, D), lambda i: (i, 0))
    vec = pl.BlockSpec((1, D), lambda i: (0, 0))
    in_specs = [pl.BlockSpec((tm, a.shape[1]), lambda i: (i, 0)) for a, _, _, _ in pairs]
    for _, _, wshape, widx in pairs:
        in_specs.append(pl.BlockSpec(wshape, functools.partial(lambda i, widx: widx, widx=widx)))
    args = [a for a, _, _, _ in pairs] + [w for _, w, _, _ in pairs]
    return pl.pallas_call(
        body, name=name, grid=(M // tm,),
        in_specs=in_specs + [row, vec, row] + after_specs,
        out_specs=[row, vec],
        out_shape=[jax.ShapeDtypeStruct((M, D), F32), jax.ShapeDtypeStruct((1, D), F32)],
        scratch_shapes=[pltpu.VMEM((tm, D), F32)],
        compiler_params=_params(("arbitrary",)),
    )(*args, x, g, dres, *after_args)


def _mm_tn(name, a, b, *, t1, tn, tm=2048, column_slots=False):
    M, K1 = a.shape
    N = b.shape[1]
    nm = M // tm

    def body(a_ref, b_ref, o_ref, acc_ref):
        m = pl.program_id(2)
        t = _dot(_bf(a_ref[...]), _bf(b_ref[...]), TN)

        @pl.when(m == 0)
        def _():
            acc_ref[...] = t

        @pl.when(m > 0)
        def _():
            acc_ref[...] += t

        @pl.when(m == nm - 1)
        def _():
            o_ref[...] = _bf(acc_ref[...])

    if column_slots:
        out_spec = pl.BlockSpec((None, t1, tn), lambda i, j, m: (j, i, 0))
        out_shape = jax.ShapeDtypeStruct((N // tn, K1, tn), BF16)
    else:
        out_spec = pl.BlockSpec((t1, tn), lambda i, j, m: (i, j))
        out_shape = jax.ShapeDtypeStruct((K1, N), BF16)
    return pl.pallas_call(
        body, name=name, grid=(K1 // t1, N // tn, nm),
        in_specs=[pl.BlockSpec((tm, t1), lambda i, j, m: (m, i)), pl.BlockSpec((tm, tn), lambda i, j, m: (m, j))],
        out_specs=out_spec, out_shape=out_shape,
        scratch_shapes=[pltpu.VMEM((t1, tn), F32)],
        compiler_params=_params(("parallel", "parallel", "arbitrary")),
    )(a, b)


_INV_SQRT2 = 1.0 / math.sqrt(2.0)
_INV_SQRT2PI = 1.0 / math.sqrt(2.0 * math.pi)


def _gelu(x):
    return 0.5 * x * (1.0 + lax.erf(x * _INV_SQRT2))


def _gelu_grad(x):
    return 0.5 * (1.0 + lax.erf(x * _INV_SQRT2)) + x * (_INV_SQRT2PI * jnp.exp(-0.5 * x * x))


def _layer_norm_parts(v):
    mu = jnp.mean(v, axis=-1, keepdims=True)
    xc = v - mu
    rs = lax.rsqrt(jnp.mean(xc * xc, axis=-1, keepdims=True) + EPS)
    return xc * rs, rs


def _gate_fwd(name, uvp, ln_g, ln_b, wm, bs_full, tr=512):
    S, W2 = uvp.shape
    W = W2 // 2
    gd = W // GROUPS

    def body(u_ref, v_ref, lg_ref, lb_ref, wm_ref, bs_ref, z_ref):
        u = _gelu(u_ref[...])
        vh, _ = _layer_norm_parts(_gelu(v_ref[...]))
        vn = _bf(vh * lg_ref[...] + lb_ref[...])
        for ci in range(tr // CHUNK):
            rows = slice(ci * CHUNK, (ci + 1) * CHUNK)
            for g in range(GROUPS):
                cols = slice(g * gd, (g + 1) * gd)
                mixed = _dot(wm_ref[g], vn[rows, cols], NN) + bs_ref[:, cols]
                z_ref[rows, cols] = _bf(u[rows, cols] * mixed)

    vec = pl.BlockSpec((1, W), lambda i: (0, 0))
    return pl.pallas_call(
        body, name=name, grid=(S // tr,),
        in_specs=[pl.BlockSpec((tr, W), lambda i: (i, 0)), pl.BlockSpec((tr, W), lambda i: (i, 1)), vec, vec,
                  pl.BlockSpec((GROUPS, CHUNK, CHUNK), lambda i: (0, 0, 0)),
                  pl.BlockSpec((CHUNK, W), lambda i: (0, 0))],
        out_specs=pl.BlockSpec((tr, W), lambda i: (i, 0)),
        out_shape=jax.ShapeDtypeStruct((S, W), BF16),
        compiler_params=_params(("parallel",)),
    )(uvp, uvp, ln_g, ln_b, wm, bs_full)


def _gate_bwd(name, uvp, dz, ln_g, ln_b, wm, bs_full, tr=256):
    S, W2 = uvp.shape
    W = W2 // 2
    gd = W // GROUPS
    n_steps = S // tr

    def body(u_ref, v_ref, dz_ref, lg_ref, lb_ref, wm_ref, bs_ref, duv_ref, dwm_ref, dmx_ref, dlg_ref, dlb_ref,
             dvn_ref):
        i = pl.program_id(0)
        up = u_ref[...]
        vp = v_ref[...]
        u = _gelu(up)
        vh, rs = _layer_norm_parts(_gelu(vp))
        lg = lg_ref[...]
        vn = _bf(vh * lg + lb_ref[...])
        dz_v = dz_ref[...]
        dmixed = dz_v * u
        dmixed_b = _bf(dmixed)

        @pl.when(i == 0)
        def _():
            dwm_ref[...] = jnp.zeros_like(dwm_ref)
            dmx_ref[...] = jnp.zeros_like(dmx_ref)
            dlg_ref[...] = jnp.zeros_like(dlg_ref)
            dlb_ref[...] = jnp.zeros_like(dlb_ref)

        for ci in range(tr // CHUNK):
            rows = slice(ci * CHUNK, (ci + 1) * CHUNK)
            dmx_ref[...] += dmixed[rows, :]
            for g in range(GROUPS):
                cols = slice(g * gd, (g + 1) * gd)
                mixed = _dot(wm_ref[g], vn[rows, cols], NN) + bs_ref[:, cols]
                duv_ref[rows, cols] = _bf(dz_v[rows, cols] * mixed * _gelu_grad(up[rows, cols]))
                dwm_ref[g] += _dot(dmixed_b[rows, cols], vn[rows, cols], NT)
                dvn_ref[rows, cols] = _dot(wm_ref[g], dmixed_b[rows, cols], TN)
        dvn = dvn_ref[...]
        dlg_ref[...] += jnp.sum(dvn * vh, axis=0, keepdims=True)
        dlb_ref[...] += jnp.sum(dvn, axis=0, keepdims=True)
        dvh = dvn * lg
        dv = rs * (dvh - jnp.mean(dvh, axis=-1, keepdims=True) - vh * jnp.mean(dvh * vh, axis=-1, keepdims=True))
        duv_ref[:, W:] = _bf(dv * _gelu_grad(vp))

        @pl.when(i == n_steps - 1)
        def _():
            t_idx = lax.broadcasted_iota(jnp.int32, (CHUNK, CHUNK), 0)
            s_idx = lax.broadcasted_iota(jnp.int32, (CHUNK, CHUNK), 1)
            keep = (s_idx <= t_idx).astype(F32)
            for g in range(GROUPS):
                dwm_ref[g] = dwm_ref[g] * keep

    vec = pl.BlockSpec((1, W), lambda i: (0, 0))
    row = pl.BlockSpec((tr, W), lambda i: (i, 0))
    return pl.pallas_call(
        body, name=name, grid=(n_steps,),
        in_specs=[row, pl.BlockSpec((tr, W), lambda i: (i, 1)), row, vec, vec,
                  pl.BlockSpec((GROUPS, CHUNK, CHUNK), lambda i: (0, 0, 0)),
                  pl.BlockSpec((CHUNK, W), lambda i: (0, 0))],
        out_specs=[pl.BlockSpec((tr, W2), lambda i: (i, 0)),
                   pl.BlockSpec((GROUPS, CHUNK, CHUNK), lambda i: (0, 0, 0)),
                   pl.BlockSpec((CHUNK, W), lambda i: (0, 0)), vec, vec],
        out_shape=[jax.ShapeDtypeStruct((S, W2), BF16), jax.ShapeDtypeStruct((GROUPS, CHUNK, CHUNK), F32),
                   jax.ShapeDtypeStruct((CHUNK, W), F32), jax.ShapeDtypeStruct((1, W), F32),
                   jax.ShapeDtypeStruct((1, W), F32)],
        scratch_shapes=[pltpu.VMEM((tr, W), F32)],
        compiler_params=_params(("arbitrary",)),
    )(uvp, uvp, dz, ln_g, ln_b, wm, bs_full)


def _t5_bucket(distance):
    small = distance < MAX_EXACT
    nf = jnp.maximum(distance, 1).astype(F32)
    large = MAX_EXACT + (jnp.log(nf / MAX_EXACT) / math.log(REL_MAX_DISTANCE / MAX_EXACT)
                         * (N_BUCKETS - MAX_EXACT)).astype(jnp.int32)
    large = jnp.minimum(large, N_BUCKETS - 1)
    return jnp.where(small, distance, large)


TILE_ELEMS = 2 * CHUNK * CHUNK


def _band_buckets():
    rel = CHUNK + jnp.arange(CHUNK)[None, :] - jnp.arange(2 * CHUNK)[:, None]
    band = (rel >= 0) & (rel <= CHUNK)
    buckets = [_t5_bucket(jnp.clip(rel, 0, CHUNK) * d) for d in DILATIONS]
    return jnp.stack(buckets), band


def _bucket_onehot():
    buckets, _ = _band_buckets()
    return (buckets.reshape(N_DIL, 1, TILE_ELEMS) == jnp.arange(N_BUCKETS)[None, :, None]).astype(F32)


def _bias_tiles(name, rel_bias, after=None):
    _, band = _band_buckets()
    own = band & (jnp.arange(2 * CHUNK) >= CHUNK)[:, None]
    masks = jnp.stack([own, band]).reshape(2, TILE_ELEMS).astype(F32)
    tables = jnp.transpose(rel_bias.reshape(N_BUCKETS, N_DIL, ATT_HEADS), (1, 2, 0))
    after_args, after_specs = _after_operand(after)

    def body(t_ref, oh_ref, m_ref, *rest):
        out_ref = rest[-1]
        for g in range(N_DIL):
            bias = lax.dot_general(t_ref[g], oh_ref[g], (NN, ((), ())), precision=lax.Precision.HIGHEST,
                                   preferred_element_type=F32)
            for f in range(2):
                out_ref[g, f] = jnp.where(m_ref[f:f + 1, :] > 0.5, bias, NEG_INF)

    whole = pl.BlockSpec(memory_space=pltpu.VMEM)
    out = pl.pallas_call(
        body, name=name, out_shape=jax.ShapeDtypeStruct((N_DIL, 2, ATT_HEADS, TILE_ELEMS), F32),
        in_specs=[whole, whole, whole] + after_specs, out_specs=whole,
        compiler_params=_params(),
    )(tables, _bucket_onehot(), masks, *after_args)
    out = out.reshape(N_DIL, 2, ATT_HEADS // 2, 2, 2 * CHUNK, CHUNK)
    return jnp.transpose(out, (0, 1, 2, 4, 3, 5)).reshape(N_DIL, 2, ATT_HEADS // 2, 2 * CHUNK, 2 * CHUNK)


def _att_specs(order):
    def spec(part, prev):
        def index(*ids):
            hp, g, c = order(*ids)
            return (jnp.maximum(c - 1, 0) if prev else c, part * 3 * 4 + g * 4 + hp)
        return pl.BlockSpec((ATT_ROWS, LANES), index)
    return [spec(0, False), spec(1, False), spec(1, True), spec(2, False), spec(2, True)]


def _rows(start, d):
    if d == 1:
        return pl.ds(pl.multiple_of(start, CHUNK), CHUNK)
    return pl.ds(start, CHUNK, stride=d)


def _att_tile_offsets(t, d):
    n = t // d
    r = t % d
    return n * (CHUNK * d) + r, n


def _head_pair_columns(x_t):
    zeros = jnp.zeros((HEAD_DIM, CHUNK), x_t.dtype)
    return jnp.concatenate([jnp.concatenate([x_t[:HEAD_DIM], zeros], axis=0),
                            jnp.concatenate([zeros, x_t[HEAD_DIM:]], axis=0)], axis=1)


def _head_pair_rows(y):
    return jnp.concatenate([y[:HEAD_DIM, :CHUNK], y[HEAD_DIM:, CHUNK:]], axis=0)


def _stage_prev_cur(dst, prev_ref, cur_ref):
    dst[0:ATT_ROWS, :] = prev_ref[...]
    dst[ATT_ROWS:2 * ATT_ROWS, :] = cur_ref[...]


def _att_fwd(name, qkv, bias_tiles):
    S = qkv.shape[0]
    n_chunks = S // ATT_ROWS
    tiles = ATT_ROWS // CHUNK

    def body(q_ref, kc_ref, kp_ref, vc_ref, vp_ref, b_ref, o_ref, l_ref, kk, vv):
        c = pl.program_id(1)
        g = pl.program_id(2)
        _stage_prev_cur(kk, kp_ref, kc_ref)
        _stage_prev_cur(vv, vp_ref, vc_ref)
        zeros = jnp.zeros((HEAD_DIM, CHUNK), BF16)

        for gi, d in enumerate(DILATIONS):
            @pl.when(g == gi)
            def _(d=d):
                span = CHUNK * d

                def tile(t, carry):
                    q0, n = _att_tile_offsets(t, d)
                    rows = _rows(q0, d)
                    cur = _rows(ATT_ROWS + q0, d)
                    prev = _rows(ATT_ROWS + q0 - span, d)
                    inner = jnp.where((c == 0) & (n == 0), 0, 1)
                    q_t = _bf(q_ref[rows, :] * ATT_SCALE).T
                    k2 = _bf(jnp.concatenate([kk[prev, :], kk[cur, :]], axis=0))
                    v_t = _bf(jnp.concatenate([vv[prev, :], vv[cur, :]], axis=0)).T
                    o_parts, l_parts = [], []
                    for hh in range(2):
                        half = slice(hh * HEAD_DIM, (hh + 1) * HEAD_DIM)
                        q_h = jnp.concatenate([q_t[half], zeros] if hh == 0 else [zeros, q_t[half]], axis=0)
                        s = _dot(k2, q_h, NN) + b_ref[inner, :, hh * CHUNK:(hh + 1) * CHUNK]
                        m = jnp.max(s, axis=0, keepdims=True)
                        p = jnp.exp(s - m)
                        l = jnp.sum(p, axis=0, keepdims=True)
                        o_parts.append(_dot(v_t, _bf(p), NN)[half] / l)
                        l_parts.append(jnp.broadcast_to(m + jnp.log(l), (HEAD_DIM, CHUNK)))
                    o_ref[rows, :] = jnp.concatenate(o_parts, axis=0).T
                    l_ref[rows, :] = jnp.concatenate(l_parts, axis=0).T
                    return carry

                lax.fori_loop(0, tiles, tile, 0, unroll=4)

    order = lambda hp, c, g: (hp, g, c)
    out_spec = pl.BlockSpec((None, ATT_ROWS, LANES), lambda hp, c, g: (g, c, hp))
    shape = jax.ShapeDtypeStruct((N_DIL, S, ATT_WIDTH), F32)
    return pl.pallas_call(
        body, name=name, grid=(ATT_HEADS // 2, n_chunks, N_DIL),
        in_specs=_att_specs(order) + [
            pl.BlockSpec((None, 2, None, 2 * CHUNK, 2 * CHUNK), lambda hp, c, g: (g, 0, hp, 0, 0))],
        out_specs=[out_spec, out_spec],
        out_shape=[shape, shape],
        scratch_shapes=[pltpu.VMEM((2 * ATT_ROWS, LANES), F32), pltpu.VMEM((2 * ATT_ROWS, LANES), F32)],
        compiler_params=_params(("parallel", "parallel", "parallel")),
    )(qkv, qkv, qkv, qkv, qkv, bias_tiles)


def _att_merge(name, o_g, l_g, tm=512):
    _, S, W = o_g.shape

    def body(o_ref, l_ref, out_ref, lse_ref):
        ls = [l_ref[g] for g in range(N_DIL)]
        mx = functools.reduce(jnp.maximum, ls)
        ws = [jnp.exp(l - mx) for l in ls]
        tot = functools.reduce(lambda a, b: a + b, ws)
        acc = ws[0] * o_ref[0]
        for g in range(1, N_DIL):
            acc = acc + ws[g] * o_ref[g]
        out_ref[...] = acc / tot
        lse_ref[...] = mx + jnp.log(tot)

    blk = pl.BlockSpec((N_DIL, tm, W), lambda i: (0, i, 0))
    row = pl.BlockSpec((tm, W), lambda i: (i, 0))
    shape = jax.ShapeDtypeStruct((S, W), F32)
    return pl.pallas_call(
        body, name=name, grid=(S // tm,), in_specs=[blk, blk], out_specs=[row, row], out_shape=[shape, shape],
        compiler_params=_params(("parallel",)),
    )(o_g, l_g)


def _att_bwd(name, qkv, o, lse, d_o, bias_tiles):
    S = qkv.shape[0]
    n_chunks = S // ATT_ROWS
    tiles = ATT_ROWS // CHUNK

    def body(q_ref, kc_ref, kp_ref, vc_ref, vp_ref, o_ref, l_ref, do_ref, b_ref, dq_ref, dk_ref, dv_ref, ds_ref,
             kk, vv):
        g = pl.program_id(1)
        c = pl.program_id(2)

        @pl.when(c == 0)
        def _():
            dk_ref[...] = jnp.zeros_like(dk_ref)
            dv_ref[...] = jnp.zeros_like(dv_ref)
            ds_ref[...] = jnp.zeros_like(ds_ref)

        _stage_prev_cur(kk, kp_ref, kc_ref)
        _stage_prev_cur(vv, vp_ref, vc_ref)
        base = c * ATT_ROWS
        head0 = lax.broadcasted_iota(jnp.int32, (CHUNK, LANES), 1) < HEAD_DIM

        def head_pair_stack(x):
            zero = jnp.zeros_like(x)
            return jnp.concatenate([jnp.where(head0, x, zero), jnp.where(head0, zero, x)], axis=0)

        for gi, d in enumerate(DILATIONS):
            @pl.when(g == gi)
            def _(d=d):
                span = CHUNK * d

                def tile(t, carry):
                    q0, n = _att_tile_offsets(t, d)
                    rows = _rows(q0, d)
                    cur = _rows(ATT_ROWS + q0, d)
                    prev = _rows(ATT_ROWS + q0 - span, d)
                    first = (c == 0) & (n == 0)
                    inner = jnp.where(first, 0, 1)
                    g_cur = _rows(base + q0, d)
                    g_prev = _rows(jnp.where(first, q0, base + q0 - span), d)
                    q2 = _bf(q_ref[rows, :] * ATT_SCALE)
                    q_t = q2.T
                    k2 = _bf(jnp.concatenate([kk[prev, :], kk[cur, :]], axis=0))
                    k_t = k2.T
                    v2 = _bf(jnp.concatenate([vv[prev, :], vv[cur, :]], axis=0))
                    do2 = do_ref[rows, :]
                    do_b = _bf(do2)
                    do_t = do_b.T
                    lse_t = l_ref[rows, :].T
                    dd_t = (do2 * o_ref[rows, :]).T
                    lse = jnp.concatenate([lse_t[0:1], lse_t[HEAD_DIM:HEAD_DIM + 1]], axis=1)
                    delta = jnp.concatenate([jnp.sum(dd_t[:HEAD_DIM], axis=0, keepdims=True),
                                             jnp.sum(dd_t[HEAD_DIM:], axis=0, keepdims=True)], axis=1)
                    s = _dot(k2, _head_pair_columns(q_t), NN) + b_ref[inner]
                    p = jnp.exp(s - lse)
                    ds = p * (_dot(v2, _head_pair_columns(do_t), NN) - delta)
                    ds_ref[...] += ds
                    ds_b = _bf(ds)
                    dq_t = _head_pair_rows(_dot(k_t, ds_b, NN))
                    dk2 = _dot(ds_b, head_pair_stack(q2), NN)
                    dv2 = _dot(_bf(p), head_pair_stack(do_b), NN)
                    dq_ref[rows, :] = (dq_t * ATT_SCALE).T
                    dk_ref[g_prev, :] += dk2[0:CHUNK]
                    dk_ref[g_cur, :] += dk2[CHUNK:2 * CHUNK]
                    dv_ref[g_prev, :] += dv2[0:CHUNK]
                    dv_ref[g_cur, :] += dv2[CHUNK:2 * CHUNK]
                    return carry

                lax.fori_loop(0, tiles, tile, 0, unroll=4)

    order = lambda hp, g, c: (hp, g, c)
    chunk = pl.BlockSpec((ATT_ROWS, LANES), lambda hp, g, c: (c, hp))
    slab = pl.BlockSpec((S, LANES), lambda hp, g, c: (0, g * 4 + hp))
    width = N_DIL * ATT_WIDTH
    dq, dk, dv, ds_sums = pl.pallas_call(
        body, name=name, grid=(ATT_HEADS // 2, N_DIL, n_chunks),
        in_specs=_att_specs(order) + [chunk, chunk, chunk,
                                      pl.BlockSpec((None, 2, None, 2 * CHUNK, 2 * CHUNK),
                                                   lambda hp, g, c: (g, 0, hp, 0, 0))],
        out_specs=[pl.BlockSpec((ATT_ROWS, LANES), lambda hp, g, c: (c, g * 4 + hp)), slab, slab,
                   pl.BlockSpec((None, None, 2 * CHUNK, 2 * CHUNK), lambda hp, g, c: (g, hp, 0, 0))],
        out_shape=[jax.ShapeDtypeStruct((S, width), F32), jax.ShapeDtypeStruct((S, width), F32),
                   jax.ShapeDtypeStruct((S, width), F32),
                   jax.ShapeDtypeStruct((N_DIL, ATT_HEADS // 2, 2 * CHUNK, 2 * CHUNK), F32)],
        scratch_shapes=[pltpu.VMEM((2 * ATT_ROWS, LANES), F32), pltpu.VMEM((2 * ATT_ROWS, LANES), F32)],
        compiler_params=_params(("parallel", "parallel", "arbitrary")),
    )(qkv, qkv, qkv, qkv, qkv, o, lse, d_o, bias_tiles)
    ds_sums = ds_sums.reshape(N_DIL, ATT_HEADS // 2, 2 * CHUNK, 2, CHUNK)
    ds_sums = jnp.transpose(ds_sums, (0, 1, 3, 2, 4)).reshape(N_DIL, ATT_HEADS, 2 * CHUNK, CHUNK)
    return dq, dk, dv, ds_sums


def _bias_grad(name, ds_sums):
    flat = ds_sums.reshape(N_DIL, ATT_HEADS, TILE_ELEMS)

    def body(oh_ref, ds_ref, out_ref):
        for g in range(N_DIL):
            out_ref[g] = lax.dot_general(oh_ref[g], ds_ref[g], (NT, ((), ())), precision=lax.Precision.HIGHEST,
                                         preferred_element_type=F32)

    out = pl.pallas_call(
        body, name=name, out_shape=jax.ShapeDtypeStruct((N_DIL, N_BUCKETS, ATT_HEADS), F32),
        compiler_params=_params(),
    )(_bucket_onehot(), flat)
    return jnp.transpose(out, (1, 0, 2)).reshape(N_BUCKETS, N_DIL * ATT_HEADS)


def _peers():
    x, y, c = lax.axis_index("x"), lax.axis_index("y"), lax.axis_index("c")
    me = 4 * x + 2 * y + c
    others = [(x, y, 1 - c), (1 - x, y, c), (x, 1 - y, c), (1 - x, 1 - y, c),
              (1 - x, y, 1 - c), (x, 1 - y, 1 - c), (1 - x, 1 - y, 1 - c)]
    return me, others


def _slot(dev):
    return 4 * dev[0] + 2 * dev[1] + dev[2]


_HBM =pl.BlockSpec(memory_space=pltpu.HBM)
_SEM = pl.BlockSpec(memory_space=pltpu.SEMAPHORE)
_EFFECT = pltpu.SideEffectType.DATAFLOW_SIDE_EFFECTING


def _my_slot():
    return 4 * lax.axis_index("x") + 2 * lax.axis_index("y") + lax.axis_index("c")


def _exchange_copy(src_ref, land_ref, send_sems, recv_sems, k, dev, me, scatter, arriving):
    src = src_ref.at[me if arriving else _slot(dev)] if scatter else src_ref
    dst = land_ref.at[_slot(dev) if arriving else me]
    return pltpu.make_async_remote_copy(src_ref=src, dst_ref=dst, send_sem=send_sems.at[k], recv_sem=recv_sems.at[k],
                                        device_id=dev, device_id_type=MESH)


def _exchange_start(name, srcs, scatter):
    n = len(srcs)
    me = _my_slot()
    landings = []
    for src in srcs:
        own = lax.dynamic_index_in_dim(src, me, 0, keepdims=True) if scatter else src[None]
        landings.append(lax.dynamic_update_slice(lax.empty((N_DEV,) + src.shape[-2:], src.dtype), own, (me, 0, 0)))

    def body(*refs):
        src_refs, land_refs = refs[:n], refs[n:2 * n]
        send_sems, recv_sems = refs[2 * n:2 * n + 2]
        token = refs[-1]
        me, others = _peers()
        for p in range(n):
            for k, dev in enumerate(others):
                _exchange_copy(src_refs[p], land_refs[p], send_sems, recv_sems, p * (N_DEV - 1) + k, dev, me,
                               scatter, False).start()
        token[...] = jnp.zeros_like(token)

    sems = pltpu.SemaphoreType.DMA((n * (N_DEV - 1),))
    hbm = lambda a: pltpu.with_memory_space_constraint(a, pltpu.HBM)
    outs = pl.pallas_call(
        body, name=name,
        out_shape=(sems, sems, *[pltpu.HBM(a.shape, a.dtype) for a in srcs + landings],
                   jax.ShapeDtypeStruct((8, LANES), F32)),
        in_specs=(_HBM,) * (2 * n), out_specs=(_SEM, _SEM) + (_HBM,) * (2 * n) + (pl.BlockSpec(memory_space=pltpu.VMEM),),
        input_output_aliases={i: 2 + i for i in range(2 * n)},
        compiler_params=pltpu.CompilerParams(has_side_effects=_EFFECT),
    )(*[hbm(a) for a in srcs + landings])
    return (outs[0], outs[1], list(outs[2:2 + n]), list(outs[2 + n:2 + 2 * n]), scatter), outs[-1]


def _exchange_wait(name, handle, after):
    send_sems, recv_sems, src_thru, land_thru, scatter = handle
    n = len(src_thru)

    def body(*refs):
        src_refs, land_refs = refs[:n], refs[n:2 * n]
        send_sems, recv_sems = refs[2 * n:2 * n + 2]
        me, others = _peers()
        for p in range(n):
            for k, dev in enumerate(others):
                cp = _exchange_copy(src_refs[p], land_refs[p], send_sems, recv_sems, p * (N_DEV - 1) + k, dev, me,
                                    scatter, True)
                cp.wait_send()
                cp.wait_recv()

    outs = pl.pallas_call(
        body, name=name,
        out_shape=tuple(pltpu.HBM(a.shape, a.dtype) for a in src_thru + land_thru),
        in_specs=(_HBM,) * (2 * n) + (_SEM, _SEM, pl.BlockSpec(memory_space=pl.ANY)), out_specs=(_HBM,) * (2 * n),
        input_output_aliases={i: i for i in range(2 * n)},
        compiler_params=pltpu.CompilerParams(has_side_effects=_EFFECT),
    )(*src_thru, *land_thru, send_sems, recv_sems, after)
    return list(outs[n:])


def _all_reduce_small(name, buf):
    rows = buf.shape[0]
    rb = rows // N_DEV

    def body(x_ref, out_ref, stage, send1, recv1, send2, recv2):
        me, others = _peers()

        def block(ref, k):
            return ref.at[pl.ds(k * rb, rb), :]

        first = [pltpu.make_async_remote_copy(src_ref=block(x_ref, _slot(dev)), dst_ref=stage.at[me],
                                              send_sem=send1.at[k], recv_sem=recv1.at[k], device_id=dev,
                                              device_id_type=MESH) for k, dev in enumerate(others)]
        for cp in first:
            cp.start()
        stage[me] = x_ref[pl.ds(pl.multiple_of(me * rb, 8), rb), :]
        for k, dev in enumerate(others):
            pltpu.make_async_remote_copy(src_ref=block(x_ref, me), dst_ref=stage.at[_slot(dev)],
                                         send_sem=send1.at[k], recv_sem=recv1.at[k], device_id=dev,
                                         device_id_type=MESH).wait_recv()
        total = stage[0]
        for j in range(1, N_DEV):
            total = total + stage[j]
        out_ref[pl.ds(pl.multiple_of(me * rb, 8), rb), :] = total
        second = [pltpu.make_async_remote_copy(src_ref=block(out_ref, me), dst_ref=block(out_ref, me),
                                               send_sem=send2.at[k], recv_sem=recv2.at[k], device_id=dev,
                                               device_id_type=MESH) for k, dev in enumerate(others)]
        for cp in second:
            cp.start()
        for k, dev in enumerate(others):
            pltpu.make_async_remote_copy(src_ref=block(out_ref, me), dst_ref=block(out_ref, _slot(dev)),
                                         send_sem=send2.at[k], recv_sem=recv2.at[k], device_id=dev,
                                         device_id_type=MESH).wait_recv()
        for cp in first + second:
            cp.wait_send()

    sems = pltpu.SemaphoreType.DMA((N_DEV - 1,))
    return pl.pallas_call(
        body, name=name,
        in_specs=[pl.BlockSpec(memory_space=pltpu.VMEM)],
        out_specs=pl.BlockSpec(memory_space=pltpu.VMEM),
        out_shape=jax.ShapeDtypeStruct(buf.shape, F32),
        scratch_shapes=[pltpu.VMEM((N_DEV, rb, LANES), F32), sems, sems, sems, sems],
        compiler_params=pltpu.CompilerParams(vmem_limit_bytes=VMEM_LIMIT_BYTES),
    )(buf)


def _adamw_math(w, g, m, v):
    m = ADAM_B1 * m + (1.0 - ADAM_B1) * g
    v = ADAM_B2 * v + (1.0 - ADAM_B2) * (g * g)
    m_hat = m / (1.0 - ADAM_B1 ** ADAM_STEP)
    v_hat = v / (1.0 - ADAM_B2 ** ADAM_STEP)
    delta = -ADAM_LR * (m_hat / (jnp.sqrt(v_hat) + ADAM_EPS) + ADAM_WD * w)
    return delta, m, v


def _adamw(name, parts, w, m, v, tr=128):
    P, R, W = parts.shape
    tr = min(tr, R)

    def body(p_ref, w_ref, m_ref, v_ref, g_out, d_out, m_out, v_out):
        g = p_ref[0].astype(F32)
        for j in range(1, P):
            g = g + p_ref[j].astype(F32)
        delta, m_new, v_new = _adamw_math(w_ref[...], g, m_ref[...], v_ref[...])
        g_out[...] = g
        d_out[...] = delta
        m_out[...] = m_new
        v_out[...] = v_new

    row = pl.BlockSpec((tr, W), lambda i: (i, 0))
    shape = jax.ShapeDtypeStruct((R, W), F32)
    return pl.pallas_call(
        body, name=name, grid=(R // tr,),
        in_specs=[pl.BlockSpec((P, tr, W), lambda i: (0, i, 0)), row, row, row],
        out_specs=[row, row, row, row],
        out_shape=[shape, shape, shape, shape],
        compiler_params=_params(("parallel",)),
    )(parts, w, m, v)


def _adamw_shard(name, parts, w, m, v, layer, earlier=None, tr=256):
    L, K, N = w.shape
    tr = min(tr, K)
    n_prev = 0 if earlier is None else 4

    def body(p_ref, w_ref, m_ref, v_ref, *rest):
        g_out, d_out, m_out, v_out = rest[n_prev:]
        g = p_ref[0].astype(F32)
        for j in range(1, N_DEV):
            g = g + p_ref[j].astype(F32)
        delta, m_new, v_new = _adamw_math(w_ref[...], g, m_ref[...], v_ref[...])
        g_out[...] = g
        d_out[...] = delta
        m_out[...] = m_new
        v_out[...] = v_new

    row = pl.BlockSpec((None, tr, N), lambda i: (layer, i, 0))
    shape = jax.ShapeDtypeStruct((L, K, N), F32)
    return pl.pallas_call(
        body, name=name, grid=(K // tr,),
        in_specs=[pl.BlockSpec((N_DEV, tr, N), lambda i: (0, i, 0)), row, row, row]
        + [pl.BlockSpec(memory_space=pl.ANY)] * n_prev,
        out_specs=[row, row, row, row],
        out_shape=[shape, shape, shape, shape],
        input_output_aliases={4 + j: j for j in range(n_prev)},
        compiler_params=_params(("parallel",)),
    )(parts, w, m, v, *(earlier or ()))


def _column_slots(full):
    K, N = full.shape
    return jnp.transpose(full.reshape(K, N_DEV, N // N_DEV), (1, 0, 2))


def _from_column_slots(slots):
    _, K, n = slots.shape
    return jnp.transpose(slots, (1, 0, 2)).reshape(K, N_DEV * n)


_SMALL =("mix_norm_g", "mlp_norm_g", "final_norm_g", "a_ln_g", "a_ln_b", "a_w_s", "a_b_s", "rel_bias")


def _pack_small(vals):
    pieces = []
    for n in _SMALL:
        flat = vals[n].reshape(-1)
        pad = (-flat.shape[0]) % (8 * LANES)
        pieces.append(jnp.pad(flat, (0, pad)).reshape(-1, LANES))
    rows = sum(p.shape[0] for p in pieces)
    tail = (-rows) % (8 * N_DEV)
    if tail:
        pieces.append(jnp.zeros((tail, LANES), F32))
    return jnp.concatenate(pieces, axis=0)


def _unpack_small(buf, like):
    out = {}
    r = 0
    for n in _SMALL:
        size = like[n].size
        nrows = -(-size // (8 * LANES)) * 8
        out[n] = buf[r:r + nrows].reshape(-1)[:size].reshape(like[n].shape)
        r += nrows
    return out


_STAGES = (("gate", ("a_w_in", "a_w_out"), 0),
           ("mlp0", ("w_up", "w_down"), 0),
           ("att", ("b_w_qkv", "b_w_out"), 0),
           ("mlp1", ("w_up", "w_down"), 1))


def kernel(x, mix_norm_g, mlp_norm_g, final_norm_g, a_w_in, a_ln_g, a_ln_b, a_w_s, a_b_s, a_w_out, b_w_qkv, b_w_out, rel_bias, w_up, w_down, loss_target, m_mix_norm_g, m_mlp_norm_g, m_final_norm_g, m_a_w_in, m_a_ln_g, m_a_ln_b, m_a_w_s, m_a_b_s, m_a_w_out, m_b_w_qkv, m_b_w_out, m_rel_bias, m_w_up, m_w_down, v_mix_norm_g, v_mlp_norm_g, v_final_norm_g, v_a_w_in, v_a_ln_g, v_a_ln_b, v_a_w_s, v_a_b_s, v_a_w_out, v_b_w_qkv, v_b_w_out, v_rel_bias, v_w_up, v_w_down):
    w = dict(mix_norm_g=mix_norm_g, mlp_norm_g=mlp_norm_g, final_norm_g=final_norm_g, a_w_in=a_w_in, a_ln_g=a_ln_g,
             a_ln_b=a_ln_b, a_w_s=a_w_s, a_b_s=a_b_s, a_w_out=a_w_out, b_w_qkv=b_w_qkv, b_w_out=b_w_out,
             rel_bias=rel_bias, w_up=w_up, w_down=w_down)
    m = dict(mix_norm_g=m_mix_norm_g, mlp_norm_g=m_mlp_norm_g, final_norm_g=m_final_norm_g, a_w_in=m_a_w_in,
             a_ln_g=m_a_ln_g, a_ln_b=m_a_ln_b, a_w_s=m_a_w_s, a_b_s=m_a_b_s, a_w_out=m_a_w_out, b_w_qkv=m_b_w_qkv,
             b_w_out=m_b_w_out, rel_bias=m_rel_bias, w_up=m_w_up, w_down=m_w_down)
    v = dict(mix_norm_g=v_mix_norm_g, mlp_norm_g=v_mlp_norm_g, final_norm_g=v_final_norm_g, a_w_in=v_a_w_in,
             a_ln_g=v_a_ln_g, a_ln_b=v_a_ln_b, a_w_s=v_a_w_s, a_b_s=v_a_b_s, a_w_out=v_a_w_out, b_w_qkv=v_b_w_qkv,
             b_w_out=v_b_w_out, rel_bias=v_rel_bias, w_up=v_w_up, w_down=v_w_down)

    stages = {s: (names, layer) for s, names, layer in _STAGES}
    order = [s for s, _, _ in _STAGES]

    def shards_of(stage):
        names, layer = stages[stage]
        return [_bf(w[n][layer]) for n in names]

    pending = {}
    pending[order[0]], first_token = _exchange_start("gather_" + order[0] + "_start", shards_of(order[0]), False)

    def get_weights(stage, dep):
        gathered = _exchange_wait("gather_" + stage + "_wait", pending.pop(stage), dep)
        nxt = order.index(stage) + 1
        token = None
        if nxt < len(order):
            shards, gathered = lax.optimization_barrier((shards_of(order[nxt]), gathered))
            pending[order[nxt]], token = _exchange_start("gather_" + order[nxt] + "_start", shards, False)
        return gathered, token

    sent = {}

    def put_grads(stage, slot_grads):
        sent[stage], token = _exchange_start("scatter_" + stage + "_start", slot_grads, True)
        return token

    loss_local, grad_x, small_g = _local_step(
        x[0], loss_target[0], mix_norm_g, mlp_norm_g, final_norm_g, a_ln_g, a_ln_b, a_w_s, a_b_s, rel_bias,
        get_weights, put_grads, first_token)

    results = {}
    for stage in reversed(order):
        names, layer = stages[stage]
        received = _exchange_wait("scatter_" + stage + "_wait", sent[stage], grad_x)
        for n, parts in zip(names, received):
            results[n] = _adamw_shard("adamw_%s_%s" % (stage, n), parts, w[n], m[n], v[n], layer, results.get(n))

    reduced = _all_reduce_small("reduce_small", _pack_small(small_g))
    small = [_unpack_small(b, w) for b in _adamw("adamw_small", reduced[None], _pack_small(w), _pack_small(m),
                                                 _pack_small(v), tr=reduced.shape[0])]

    outs = []
    for j in range(4):
        outs.extend(small[j][n] if n in _SMALL else results[n][j] for n in w)
    loss = lax.psum(loss_local, ("x", "y", "c"))
    return (loss, grad_x[None], *outs)


def _local_step(xs, tgt, mix_norm_g, mlp_norm_g, final_norm_g, a_ln_g, a_ln_b, a_w_s, a_b_s, rel_bias,
                get_weights, put_grads, first_token=None):
    D = xs.shape[-1]
    g_mix = [mix_norm_g[l][None, :] for l in range(2)]
    g_mlp = [mlp_norm_g[l][None, :] for l in range(2)]
    g_fin = final_norm_g[None, :]
    ln_g, ln_b = a_ln_g, a_ln_b
    causal = jnp.tril(jnp.ones((CHUNK, CHUNK), dtype=bool))
    wm = _bf(jnp.where(causal[None], a_w_s[0], 0.0))
    bs_full = jnp.repeat(a_b_s[0].T, D // GROUPS, axis=1)
    bias_tiles = _bias_tiles("att_bias", rel_bias, after=first_token)

    (win, wout), token = get_weights("gate", bias_tiles)
    wout = wout.reshape(-1, D)
    y0 = _rms_fwd("rms_mix0", xs, g_mix[0], after=token)
    uvp = _mm_nn("gate_in", y0, win, tm=512, nc=win.shape[2], shards=True)
    z = _gate_fwd("gate_mid", uvp, ln_g, ln_b, wm, bs_full)
    h1 = _mm_nn("gate_out", z, wout, tm=512, nc=512, epi="res", extra=xs)
    (wup0, wdn0), token = get_weights("mlp0", h1)
    wdn0 = wdn0.reshape(-1, D)
    y1 = _rms_fwd("rms_mlp0", h1, g_mlp[0], after=token)
    a0, f0 = _mm_nn("mlp0_up", y1, wup0, tm=512, nc=wup0.shape[2], epi="relu2", shards=True)
    h2 = _mm_nn("mlp0_down", f0, wdn0, tm=512, nc=512, epi="res", extra=h1)
    (wqkv, wo), token = get_weights("att", h2)
    wqkv, wo = _from_column_slots(wqkv), _from_column_slots(wo)
    y2 = _rms_fwd("rms_mix1", h2, g_mix[1], after=token)
    qkv = _mm_nn("att_qkv", y2, wqkv, tm=256, nc=512)
    o_att, lse = _att_merge("att_merge", *_att_fwd("att_fwd", qkv, bias_tiles))
    h3 = _mm_nn("att_out", o_att, wo, tm=512, nc=512, epi="res", extra=h2)
    (wup1, wdn1), token = get_weights("mlp1", h3)
    wdn1 = wdn1.reshape(-1, D)
    y3 = _rms_fwd("rms_mlp1", h3, g_mlp[1], after=token)
    a1, f1 = _mm_nn("mlp1_up", y3, wup1, tm=512, nc=wup1.shape[2], epi="relu2", shards=True)
    h4 = _mm_nn("mlp1_down", f1, wdn1, tm=512, nc=512, epi="res", extra=h3)
    dh, dg_fin, err2 = _final_loss("final_loss", h4, g_fin, tgt)
    loss_local = 0.5 * jnp.sum(err2) / D

    def mlp_bwd(tag, dh, h_in, y, a, f, wup_l, wdn_l, g_row, after):
        da = _mm_nt(tag + "_dact", dh, wdn_l, tm=512, nc=512, epi="mask2relu", extra=a, after=after)
        g_dn = _mm_tn(tag + "_dwdown", f, dh, t1=1024, tn=1024)
        g_up = _mm_tn(tag + "_dwup", y, da, t1=1024, tn=wup_l.shape[2], column_slots=True)
        dh_in, dg = _mm_nt_rms_bwd(tag + "_dy", [(da, wup_l, *_whole(wup_l))], h_in, g_row, dh, tm=512, nc=512,
                                   shards=True)
        return dh_in, dg, put_grads(tag, [g_up, g_dn.reshape(N_DEV, -1, D)])

    dh3, dg_mlp1, token = mlp_bwd("mlp1", dh, h3, y3, a1, f1, wup1, wdn1, g_mlp[1], None)

    d_o = _mm_nt("att_dout", dh3, wo, tm=512, nc=512, after=token)
    g_wo = _mm_tn("att_dwo", o_att, dh3, t1=512, tn=1024)
    dq, dk, dv, ds_sums = _att_bwd("att_bwd", qkv, o_att, lse, d_o, bias_tiles)
    part_w = N_DIL * ATT_WIDTH
    g_qkv = [_mm_tn("att_dwqkv%d" % p, y2, t, t1=1024, tn=part_w, tm=1024) for p, t in enumerate((dq, dk, dv))]
    dh2, dg_mix1 = _mm_nt_rms_bwd("att_dy", [(t, wqkv, (D, part_w), (0, p)) for p, t in enumerate((dq, dk, dv))],
                                  h2, g_mix[1], dh3, tm=256, nc=512)
    token = put_grads("att", [_column_slots(jnp.concatenate(g_qkv, axis=1)), _column_slots(g_wo)])

    dh1, dg_mlp0, token = mlp_bwd("mlp0", dh2, h1, y1, a0, f0, wup0, wdn0, g_mlp[0], token)

    dz = _mm_nt("gate_dz", dh1, wout, tm=512, nc=512, after=token)
    g_wout = _mm_tn("gate_dwout", z, dh1, t1=1024, tn=1024)
    duvp, d_wm, d_mixed, d_lng, d_lnb = _gate_bwd("gate_dmid", uvp, dz, ln_g, ln_b, wm, bs_full)
    g_win = _mm_tn("gate_dwin", y0, duvp, t1=1024, tn=win.shape[2], column_slots=True)
    token = put_grads("gate", [g_win, g_wout.reshape(N_DEV, -1, D)])
    grad_x, dg_mix0 = _mm_nt_rms_bwd("gate_dy", [(duvp, win, *_whole(win))], xs, g_mix[0], dh1, tm=512, nc=512,
                                     after=token, shards=True)

    small_g = dict(
        mix_norm_g=jnp.concatenate([dg_mix0, dg_mix1], axis=0),
        mlp_norm_g=jnp.concatenate([dg_mlp0, dg_mlp1], axis=0),
        final_norm_g=dg_fin[0], a_ln_g=d_lng, a_ln_b=d_lnb, a_w_s=d_wm[None],
        a_b_s=jnp.sum(d_mixed.reshape(CHUNK, GROUPS, D // GROUPS), axis=2).T[None],
        rel_bias=_bias_grad("att_dbias", ds_sums))
    return loss_local, grad_x, small_g
```

```python
import functools
import math

import jax
import jax.numpy as jnp
from jax import lax
from jax.experimental import pallas as pl
from jax.experimental.pallas import tpu as pltpu

F32 = jnp.float32
BF16 = jnp.bfloat16
MESH = pl.DeviceIdType.MESH

N_DEV = 8
EPS = 1e-6
NEG_INF = -1e30
CHUNK = 128
GROUPS = 8
HEAD_DIM = 64
ATT_HEADS = 8
ATT_WIDTH = ATT_HEADS * HEAD_DIM
DILATIONS = (1, 4, 16)
N_DIL = len(DILATIONS)
N_BUCKETS = 32
MAX_EXACT = N_BUCKETS // 2
REL_MAX_DISTANCE = 2048
ATT_ROWS = 2048
ATT_SCALE = HEAD_DIM ** -0.5
LANES = 128

ADAM_LR = 0.001
ADAM_B1 = 0.9
ADAM_B2 = 0.999
ADAM_EPS = 1e-08
ADAM_WD = 0.01
ADAM_STEP = 10

VMEM_LIMIT_BYTES = 56 * 1024 * 1024


def _params(semantics=None):
    return pltpu.CompilerParams(dimension_semantics=semantics, vmem_limit_bytes=VMEM_LIMIT_BYTES)


def _bf(v):
    return v.astype(BF16)


def _dot(a, b, dims):
    return lax.dot_general(a, b, (dims, ((), ())), preferred_element_type=F32)


NN = ((1,), (0,))
NT = ((1,), (1,))
TN = ((0,), (0,))


def _after_operand(after):
    if after is None:
        return [], []
    return [after], [pl.BlockSpec(memory_space=pl.ANY)]


def _rms_fwd(name, x, g, tm=512, after=None):
    S, D = x.shape
    after_args, after_specs = _after_operand(after)

    def body(x_ref, g_ref, *rest):
        y_ref = rest[-1]
        xv = x_ref[...]
        r = lax.rsqrt(jnp.mean(xv * xv, axis=-1, keepdims=True) + EPS)
        y_ref[...] = _bf(xv * r * g_ref[...])

    return pl.pallas_call(
        body, name=name, grid=(S // tm,),
        in_specs=[pl.BlockSpec((tm, D), lambda i: (i, 0)), pl.BlockSpec((1, D), lambda i: (0, 0))] + after_specs,
        out_specs=pl.BlockSpec((tm, D), lambda i: (i, 0)),
        out_shape=jax.ShapeDtypeStruct((S, D), BF16),
        compiler_params=_params(("parallel",)),
    )(x, g, *after_args)


def _final_loss(name, h, g, target, tm=512):
    S, D = h.shape

    def body(h_ref, g_ref, t_ref, dh_ref, dg_ref, l_ref):
        i = pl.program_id(0)
        xv = h_ref[...]
        r = lax.rsqrt(jnp.mean(xv * xv, axis=-1, keepdims=True) + EPS)
        xh = xv * r
        gv = g_ref[...]
        e = xh * gv - t_ref[...]
        dout = e / D
        dyg = dout * gv
        c = jnp.mean(dyg * xh, axis=-1, keepdims=True)
        dh_ref[...] = r * (dyg - xh * c)
        dg_part = jnp.sum(dout * xh, axis=0, keepdims=True)
        l_part = jnp.sum(e * e, axis=0, keepdims=True)

        @pl.when(i == 0)
        def _():
            dg_ref[...] = dg_part
            l_ref[...] = l_part

        @pl.when(i > 0)
        def _():
            dg_ref[...] += dg_part
            l_ref[...] += l_part

    row = pl.BlockSpec((tm, D), lambda i: (i, 0))
    vec = pl.BlockSpec((1, D), lambda i: (0, 0))
    return pl.pallas_call(
        body, name=name, grid=(S // tm,),
        in_specs=[row, vec, row],
        out_specs=[row, vec, vec],
        out_shape=[jax.ShapeDtypeStruct((S, D), F32), jax.ShapeDtypeStruct((1, D), F32),
                   jax.ShapeDtypeStruct((1, D), F32)],
        compiler_params=_params(("arbitrary",)),
    )(h, g, target)


def _chunk_product(a_vals, w_refs, j, nc, nt, shards):
    cols = slice(j * nc, (j + 1) * nc)
    acc = None
    for a_v, w_ref in zip(a_vals, w_refs):
        if not shards:
            terms = [_dot(a_v, w_ref[cols, :], NT) if nt else _dot(a_v, w_ref[:, cols], NN)]
        elif nt:
            nl = w_ref.shape[2]
            terms = [_dot(a_v[:, k * nl:(k + 1) * nl], w_ref[k, cols, :], NT) for k in range(N_DEV)]
        else:
            terms = [_dot(a_v, w_ref[j], NN)]
        for t in terms:
            acc = t if acc is None else acc + t
    return cols, acc


def _mm_rows(name, pairs, n_out, *, nt, tm, nc, epi="plain", extra=None, out_dtype=F32, after=None, shards=False,
             norm_g=None):
    M = pairs[0][0].shape[0]
    np_ = len(pairs)
    after_args, after_specs = _after_operand(after)

    def body(*refs):
        a_refs = refs[:np_]
        w_refs = refs[np_:2 * np_]
        pos = 2 * np_
        e_ref = None
        if extra is not None:
            e_ref = refs[pos]
            pos += 1
        if norm_g is not None:
            g_ref = refs[pos]
            pos += 1
        pos += len(after_args)
        outs = refs[pos:]
        a_vals = [_bf(a[...]) for a in a_refs]
        for j in range(n_out // nc):
            cols, acc = _chunk_product(a_vals, w_refs, j, nc, nt, shards)
            if epi == "plain":
                outs[0][:, cols] = acc.astype(out_dtype)
            elif epi == "res":
                outs[0][:, cols] = e_ref[:, cols] + acc
            elif epi == "relu2":
                outs[0][:, cols] = _bf(acc)
                rl = jnp.maximum(acc, 0.0)
                outs[1][:, cols] = _bf(rl * rl)
            elif epi == "mask2relu":
                outs[0][:, cols] = _bf(acc * (2.0 * jnp.maximum(e_ref[:, cols].astype(F32), 0.0)))
        if norm_g is not None:
            hv = outs[0][...]
            r = lax.rsqrt(jnp.mean(hv * hv, axis=-1, keepdims=True) + EPS)
            outs[1][...] = _bf(hv * r * g_ref[...])

    in_specs = [pl.BlockSpec((tm, a.shape[1]), lambda i: (i, 0)) for a, _, _, _ in pairs]
    for _, _, wshape, widx in pairs:
        in_specs.append(pl.BlockSpec(wshape, functools.partial(lambda i, widx: widx, widx=widx)))
    args = [a for a, _, _, _ in pairs] + [w for _, w, _, _ in pairs]
    if extra is not None:
        in_specs.append(pl.BlockSpec((tm, n_out), lambda i: (i, 0)))
        args.append(extra)
    if norm_g is not None:
        in_specs.append(pl.BlockSpec((1, n_out), lambda i: (0, 0)))
        args.append(norm_g)
    in_specs += after_specs
    args += after_args
    row_out = pl.BlockSpec((tm, n_out), lambda i: (i, 0))
    if epi == "relu2":
        out_specs = [row_out, row_out]
        out_shape = [jax.ShapeDtypeStruct((M, n_out), BF16), jax.ShapeDtypeStruct((M, n_out), BF16)]
    elif norm_g is not None:
        out_specs = [row_out, row_out]
        out_shape = [jax.ShapeDtypeStruct((M, n_out), F32), jax.ShapeDtypeStruct((M, n_out), BF16)]
    else:
        dt = BF16 if epi == "mask2relu" else (F32 if epi == "res" else out_dtype)
        out_specs = row_out
        out_shape = jax.ShapeDtypeStruct((M, n_out), dt)
    return pl.pallas_call(
        body, name=name, grid=(M // tm,), in_specs=in_specs, out_specs=out_specs, out_shape=out_shape,
        compiler_params=_params(("parallel",)),
    )(*args)


def _whole(w):
    return w.shape, (0,) * w.ndim


def _mm_nn(name, a, w, **kw):
    n_out = w.shape[0] * w.shape[2] if w.ndim == 3 else w.shape[1]
    return _mm_rows(name, [(a, w, *_whole(w))], n_out, nt=False, **kw)


def _mm_nt(name, a, w, **kw):
    return _mm_rows(name, [(a, w, *_whole(w))], w.shape[0], nt=True, **kw)


def _mm_nt_rms_bwd(name, pairs, x, g, dres, *, tm, nc, after=None, shards=False):
    M, D = x.shape
    np_ = len(pairs)
    after_args, after_specs = _after_operand(after)

    def body(*refs):
        a_refs = refs[:np_]
        w_refs = refs[np_:2 * np_]
        x_ref, g_ref, r_ref = refs[2 * np_:2 * np_ + 3]
        dx_ref, dg_ref, dy_sc = refs[-3:]
        i = pl.program_id(0)
        a_vals = [_bf(a[...]) for a in a_refs]
        for j in range(D // nc):
            cols, acc = _chunk_product(a_vals, w_refs, j, nc, True, shards)
            dy_sc[:, cols] = acc
        xv = x_ref[...]
        r = lax.rsqrt(jnp.mean(xv * xv, axis=-1, keepdims=True) + EPS)
        xh = xv * r
        dy_v = dy_sc[...]
        dyg = dy_v * g_ref[...]
        c = jnp.mean(dyg * xh, axis=-1, keepdims=True)
        dx_ref[...] = r_ref[...] + r * (dyg - xh * c)
        part = jnp.sum(dy_v * xh, axis=0, keepdims=True)

        @pl.when(i == 0)
        def _():
            dg_ref[...] = part

        @pl.when(i > 0)
        def _():
            dg_ref[...] += part

    row = pl.BlockSpec((tm, D), lambda i: (i, 0))
    vec = pl.BlockSpec((1, D), lambda i: (0, 0))
    in_specs = [pl.BlockSpec((tm, a.shape[1]), lambda i: (i, 0)) for a, _, _, _ in pairs]
    for _, _, wshape, widx in pairs:
        in_specs.append(pl.BlockSpec(wshape, functools.partial(lambda i, widx: widx, widx=widx)))
    args = [a for a, _, _, _ in pairs] + [w for _, w, _, _ in pairs]
    return pl.pallas_call(
        body, name=name, grid=(M // tm,),
        in_specs=in_specs + [row, vec, row] + after_specs,
        out_specs=[row, vec],
        out_shape=[jax.ShapeDtypeStruct((M, D), F32), jax.ShapeDtypeStruct((1, D), F32)],
        scratch_shapes=[pltpu.VMEM((tm, D), F32)],
        compiler_params=_params(("arbitrary",)),
    )(*args, x, g, dres, *after_args)


def _mm_tn(name, a, b, *, t1, tn, tm=2048, slot_cols=None):
    M, K1 = a.shape
    N = b.shape[1]
    nm = M // tm

    def body(a_ref, b_ref, o_ref, acc_ref):
        m = pl.program_id(2)
        t = _dot(_bf(a_ref[...]), _bf(b_ref[...]), TN)

        @pl.when(m == 0)
        def _():
            acc_ref[...] = t

        @pl.when(m > 0)
        def _():
            acc_ref[...] += t

        @pl.when(m == nm - 1)
        def _():
            if slot_cols is None:
                o_ref[...] = _bf(acc_ref[...])
            else:
                for k in range(tn // slot_cols):
                    o_ref[k] = _bf(acc_ref[:, k * slot_cols:(k + 1) * slot_cols])

    if slot_cols is not None:
        out_spec = pl.BlockSpec((tn // slot_cols, t1, slot_cols), lambda i, j, m: (j, i, 0))
        out_shape = jax.ShapeDtypeStruct((N // slot_cols, K1, slot_cols), BF16)
    else:
        out_spec = pl.BlockSpec((t1, tn), lambda i, j, m: (i, j))
        out_shape = jax.ShapeDtypeStruct((K1, N), BF16)
    return pl.pallas_call(
        body, name=name, grid=(K1 // t1, N // tn, nm),
        in_specs=[pl.BlockSpec((tm, t1), lambda i, j, m: (m, i)), pl.BlockSpec((tm, tn), lambda i, j, m: (m, j))],
        out_specs=out_spec, out_shape=out_shape,
        scratch_shapes=[pltpu.VMEM((t1, tn), F32)],
        compiler_params=_params(("parallel", "parallel", "arbitrary")),
    )(a, b)


_INV_SQRT2 = 1.0 / math.sqrt(2.0)
_INV_SQRT2PI = 1.0 / math.sqrt(2.0 * math.pi)


def _gelu(x):
    return 0.5 * x * (1.0 + lax.erf(x * _INV_SQRT2))


def _gelu_and_grad(x):
    cdf = 0.5 * (1.0 + lax.erf(x * _INV_SQRT2))
    return x * cdf, cdf + x * (_INV_SQRT2PI * jnp.exp(-0.5 * x * x))


def _layer_norm_parts(v):
    mu = jnp.mean(v, axis=-1, keepdims=True)
    xc = v - mu
    rs = lax.rsqrt(jnp.mean(xc * xc, axis=-1, keepdims=True) + EPS)
    return xc * rs, rs


def _gate_fwd(name, uvp, ln_g, ln_b, wm, bs_full, tr=512):
    S, W2 = uvp.shape
    W = W2 // 2
    gd = W // GROUPS

    def body(u_ref, v_ref, lg_ref, lb_ref, wm_ref, bs_ref, z_ref):
        u = _gelu(u_ref[...])
        vh, _ = _layer_norm_parts(_gelu(v_ref[...]))
        vn = _bf(vh * lg_ref[...] + lb_ref[...])
        for ci in range(tr // CHUNK):
            rows = slice(ci * CHUNK, (ci + 1) * CHUNK)
            for g in range(GROUPS):
                cols = slice(g * gd, (g + 1) * gd)
                mixed = _dot(wm_ref[g], vn[rows, cols], NN) + bs_ref[:, cols]
                z_ref[rows, cols] = _bf(u[rows, cols] * mixed)

    vec = pl.BlockSpec((1, W), lambda i: (0, 0))
    return pl.pallas_call(
        body, name=name, grid=(S // tr,),
        in_specs=[pl.BlockSpec((tr, W), lambda i: (i, 0)), pl.BlockSpec((tr, W), lambda i: (i, 1)), vec, vec,
                  pl.BlockSpec((GROUPS, CHUNK, CHUNK), lambda i: (0, 0, 0)),
                  pl.BlockSpec((CHUNK, W), lambda i: (0, 0))],
        out_specs=pl.BlockSpec((tr, W), lambda i: (i, 0)),
        out_shape=jax.ShapeDtypeStruct((S, W), BF16),
        compiler_params=_params(("parallel",)),
    )(uvp, uvp, ln_g, ln_b, wm, bs_full)


def _gate_bwd(name, uvp, dz, ln_g, ln_b, wm, bs_full, tr=256):
    S, W2 = uvp.shape
    W = W2 // 2
    gd = W // GROUPS
    n_steps = S // tr

    def body(u_ref, v_ref, dz_ref, lg_ref, lb_ref, wm_ref, bs_ref, duv_ref, dwm_ref, dmx_ref, dlg_ref, dlb_ref,
             dvn_ref):
        i = pl.program_id(0)
        u, du_dup = _gelu_and_grad(u_ref[...])
        v, dv_dvp = _gelu_and_grad(v_ref[...])
        vh, rs = _layer_norm_parts(v)
        lg = lg_ref[...]
        vn = _bf(vh * lg + lb_ref[...])
        dz_v = dz_ref[...]
        dmixed = dz_v * u
        dmixed_b = _bf(dmixed)

        @pl.when(i == 0)
        def _():
            dwm_ref[...] = jnp.zeros_like(dwm_ref)
            dmx_ref[...] = jnp.zeros_like(dmx_ref)
            dlg_ref[...] = jnp.zeros_like(dlg_ref)
            dlb_ref[...] = jnp.zeros_like(dlb_ref)

        for ci in range(tr // CHUNK):
            rows = slice(ci * CHUNK, (ci + 1) * CHUNK)
            dmx_ref[...] += dmixed[rows, :]
            for g in range(GROUPS):
                cols = slice(g * gd, (g + 1) * gd)
                mixed = _dot(wm_ref[g], vn[rows, cols], NN) + bs_ref[:, cols]
                duv_ref[rows, cols] = _bf(dz_v[rows, cols] * mixed * du_dup[rows, cols])
                dwm_ref[g] += _dot(dmixed_b[rows, cols], vn[rows, cols], NT)
                dvn_ref[rows, cols] = _dot(wm_ref[g], dmixed_b[rows, cols], TN)
        dvn = dvn_ref[...]
        dlg_ref[...] += jnp.sum(dvn * vh, axis=0, keepdims=True)
        dlb_ref[...] += jnp.sum(dvn, axis=0, keepdims=True)
        dvh = dvn * lg
        dv = rs * (dvh - jnp.mean(dvh, axis=-1, keepdims=True) - vh * jnp.mean(dvh * vh, axis=-1, keepdims=True))
        duv_ref[:, W:] = _bf(dv * dv_dvp)

        @pl.when(i == n_steps - 1)
        def _():
            t_idx = lax.broadcasted_iota(jnp.int32, (CHUNK, CHUNK), 0)
            s_idx = lax.broadcasted_iota(jnp.int32, (CHUNK, CHUNK), 1)
            keep = (s_idx <= t_idx).astype(F32)
            for g in range(GROUPS):
                dwm_ref[g] = dwm_ref[g] * keep

    vec = pl.BlockSpec((1, W), lambda i: (0, 0))
    row = pl.BlockSpec((tr, W), lambda i: (i, 0))
    return pl.pallas_call(
        body, name=name, grid=(n_steps,),
        in_specs=[row, pl.BlockSpec((tr, W), lambda i: (i, 1)), row, vec, vec,
                  pl.BlockSpec((GROUPS, CHUNK, CHUNK), lambda i: (0, 0, 0)),
                  pl.BlockSpec((CHUNK, W), lambda i: (0, 0))],
        out_specs=[pl.BlockSpec((tr, W2), lambda i: (i, 0)),
                   pl.BlockSpec((GROUPS, CHUNK, CHUNK), lambda i: (0, 0, 0)),
                   pl.BlockSpec((CHUNK, W), lambda i: (0, 0)), vec, vec],
        out_shape=[jax.ShapeDtypeStruct((S, W2), BF16), jax.ShapeDtypeStruct((GROUPS, CHUNK, CHUNK), F32),
                   jax.ShapeDtypeStruct((CHUNK, W), F32), jax.ShapeDtypeStruct((1, W), F32),
                   jax.ShapeDtypeStruct((1, W), F32)],
        scratch_shapes=[pltpu.VMEM((tr, W), F32)],
        compiler_params=_params(("arbitrary",)),
    )(uvp, uvp, dz, ln_g, ln_b, wm, bs_full)


def _t5_bucket(distance):
    small = distance < MAX_EXACT
    nf = jnp.maximum(distance, 1).astype(F32)
    large = MAX_EXACT + (jnp.log(nf / MAX_EXACT) / math.log(REL_MAX_DISTANCE / MAX_EXACT)
                         * (N_BUCKETS - MAX_EXACT)).astype(jnp.int32)
    large = jnp.minimum(large, N_BUCKETS - 1)
    return jnp.where(small, distance, large)


TILE_ELEMS = 2 * CHUNK * CHUNK


def _band_buckets():
    rel = CHUNK + jnp.arange(CHUNK)[None, :] - jnp.arange(2 * CHUNK)[:, None]
    band = (rel >= 0) & (rel <= CHUNK)
    buckets = [_t5_bucket(jnp.clip(rel, 0, CHUNK) * d) for d in DILATIONS]
    return jnp.stack(buckets), band


def _bucket_onehot():
    buckets, _ = _band_buckets()
    return (buckets.reshape(N_DIL, 1, TILE_ELEMS) == jnp.arange(N_BUCKETS)[None, :, None]).astype(F32)


def _bias_tiles(name, rel_bias, after=None):
    _, band = _band_buckets()
    own = band & (jnp.arange(2 * CHUNK) >= CHUNK)[:, None]
    masks = jnp.stack([own, band]).reshape(2, TILE_ELEMS).astype(F32)
    tables = jnp.transpose(rel_bias.reshape(N_BUCKETS, N_DIL, ATT_HEADS), (1, 2, 0))
    after_args, after_specs = _after_operand(after)

    def body(t_ref, oh_ref, m_ref, *rest):
        out_ref = rest[-1]
        for g in range(N_DIL):
            bias = lax.dot_general(t_ref[g], oh_ref[g], (NN, ((), ())), precision=lax.Precision.HIGHEST,
                                   preferred_element_type=F32)
            for f in range(2):
                out_ref[g, f] = jnp.where(m_ref[f:f + 1, :] > 0.5, bias, NEG_INF)

    whole = pl.BlockSpec(memory_space=pltpu.VMEM)
    out = pl.pallas_call(
        body, name=name, out_shape=jax.ShapeDtypeStruct((N_DIL, 2, ATT_HEADS, TILE_ELEMS), F32),
        in_specs=[whole, whole, whole] + after_specs, out_specs=whole,
        compiler_params=_params(),
    )(tables, _bucket_onehot(), masks, *after_args)
    out = out.reshape(N_DIL, 2, ATT_HEADS // 2, 2, 2 * CHUNK, CHUNK)
    return jnp.transpose(out, (0, 1, 2, 4, 3, 5)).reshape(N_DIL, 2, ATT_HEADS // 2, 2 * CHUNK, 2 * CHUNK)


def _att_specs(order):
    def spec(part, prev):
        def index(*ids):
            hp, g, c = order(*ids)
            return (jnp.maximum(c - 1, 0) if prev else c, part * 3 * 4 + g * 4 + hp)
        return pl.BlockSpec((ATT_ROWS, LANES), index)
    return [spec(0, False), spec(1, False), spec(1, True), spec(2, False), spec(2, True)]


def _rows(start, d):
    if d == 1:
        return pl.ds(pl.multiple_of(start, CHUNK), CHUNK)
    return pl.ds(start, CHUNK, stride=d)


def _att_tile_offsets(t, d):
    n = t // d
    r = t % d
    return n * (CHUNK * d) + r, n


def _head_pair_columns(x_t):
    zeros = jnp.zeros((HEAD_DIM, CHUNK), x_t.dtype)
    return jnp.concatenate([jnp.concatenate([x_t[:HEAD_DIM], zeros], axis=0),
                            jnp.concatenate([zeros, x_t[HEAD_DIM:]], axis=0)], axis=1)


def _head_pair_rows(y):
    return jnp.concatenate([y[:HEAD_DIM, :CHUNK], y[HEAD_DIM:, CHUNK:]], axis=0)


def _stage_prev_cur(dst, prev_ref, cur_ref):
    dst[0:ATT_ROWS, :] = prev_ref[...]
    dst[ATT_ROWS:2 * ATT_ROWS, :] = cur_ref[...]


def _att_fwd(name, qkv, bias_tiles):
    S = qkv.shape[0]
    n_chunks = S // ATT_ROWS
    tiles = ATT_ROWS // CHUNK

    def body(q_ref, kc_ref, kp_ref, vc_ref, vp_ref, b_ref, o_ref, l_ref, kk, vv):
        c = pl.program_id(1)
        g = pl.program_id(2)
        _stage_prev_cur(kk, kp_ref, kc_ref)
        _stage_prev_cur(vv, vp_ref, vc_ref)
        zeros = jnp.zeros((HEAD_DIM, CHUNK), BF16)

        for gi, d in enumerate(DILATIONS):
            @pl.when(g == gi)
            def _(d=d):
                span = CHUNK * d

                def tile(t, carry):
                    q0, n = _att_tile_offsets(t, d)
                    rows = _rows(q0, d)
                    cur = _rows(ATT_ROWS + q0, d)
                    prev = _rows(ATT_ROWS + q0 - span, d)
                    inner = jnp.where((c == 0) & (n == 0), 0, 1)
                    q_t = _bf(q_ref[rows, :] * ATT_SCALE).T
                    k2 = _bf(jnp.concatenate([kk[prev, :], kk[cur, :]], axis=0))
                    v_t = _bf(jnp.concatenate([vv[prev, :], vv[cur, :]], axis=0)).T
                    o_parts, l_parts = [], []
                    for hh in range(2):
                        half = slice(hh * HEAD_DIM, (hh + 1) * HEAD_DIM)
                        q_h = jnp.concatenate([q_t[half], zeros] if hh == 0 else [zeros, q_t[half]], axis=0)
                        s = _dot(k2, q_h, NN) + b_ref[inner, :, hh * CHUNK:(hh + 1) * CHUNK]
                        m = jnp.max(s, axis=0, keepdims=True)
                        p = jnp.exp(s - m)
                        l = jnp.sum(p, axis=0, keepdims=True)
                        o_parts.append(_dot(v_t, _bf(p), NN)[half] / l)
                        l_parts.append(jnp.broadcast_to(m + jnp.log(l), (HEAD_DIM, CHUNK)))
                    o_ref[rows, :] = jnp.concatenate(o_parts, axis=0).T
                    l_ref[rows, :] = jnp.concatenate(l_parts, axis=0).T
                    return carry

                lax.fori_loop(0, tiles, tile, 0, unroll=4)

    order = lambda hp, c, g: (hp, g, c)
    out_spec = pl.BlockSpec((None, ATT_ROWS, LANES), lambda hp, c, g: (g, c, hp))
    shape = jax.ShapeDtypeStruct((N_DIL, S, ATT_WIDTH), F32)
    return pl.pallas_call(
        body, name=name, grid=(ATT_HEADS // 2, n_chunks, N_DIL),
        in_specs=_att_specs(order) + [
            pl.BlockSpec((None, 2, None, 2 * CHUNK, 2 * CHUNK), lambda hp, c, g: (g, 0, hp, 0, 0))],
        out_specs=[out_spec, out_spec],
        out_shape=[shape, shape],
        scratch_shapes=[pltpu.VMEM((2 * ATT_ROWS, LANES), F32), pltpu.VMEM((2 * ATT_ROWS, LANES), F32)],
        compiler_params=_params(("parallel", "parallel", "parallel")),
    )(qkv, qkv, qkv, qkv, qkv, bias_tiles)


def _att_merge(name, o_g, l_g, tm=512):
    _, S, W = o_g.shape

    def body(o_ref, l_ref, out_ref, lse_ref):
        ls = [l_ref[g] for g in range(N_DIL)]
        mx = functools.reduce(jnp.maximum, ls)
        ws = [jnp.exp(l - mx) for l in ls]
        tot = functools.reduce(lambda a, b: a + b, ws)
        acc = ws[0] * o_ref[0]
        for g in range(1, N_DIL):
            acc = acc + ws[g] * o_ref[g]
        out_ref[...] = acc / tot
        lse_ref[...] = mx + jnp.log(tot)

    blk = pl.BlockSpec((N_DIL, tm, W), lambda i: (0, i, 0))
    row = pl.BlockSpec((tm, W), lambda i: (i, 0))
    shape = jax.ShapeDtypeStruct((S, W), F32)
    return pl.pallas_call(
        body, name=name, grid=(S // tm,), in_specs=[blk, blk], out_specs=[row, row], out_shape=[shape, shape],
        compiler_params=_params(("parallel",)),
    )(o_g, l_g)


def _att_bwd(name, qkv, o, lse, d_o, bias_tiles):
    S = qkv.shape[0]
    n_chunks = S // ATT_ROWS
    tiles = ATT_ROWS // CHUNK

    def body(q_ref, kc_ref, kp_ref, vc_ref, vp_ref, o_ref, l_ref, do_ref, b_ref, dq_ref, dk_ref, dv_ref, ds_ref,
             kk, vv):
        g = pl.program_id(1)
        c = pl.program_id(2)

        @pl.when(c == 0)
        def _():
            dk_ref[...] = jnp.zeros_like(dk_ref)
            dv_ref[...] = jnp.zeros_like(dv_ref)
            ds_ref[...] = jnp.zeros_like(ds_ref)

        _stage_prev_cur(kk, kp_ref, kc_ref)
        _stage_prev_cur(vv, vp_ref, vc_ref)
        base = c * ATT_ROWS
        head0 = lax.broadcasted_iota(jnp.int32, (CHUNK, LANES), 1) < HEAD_DIM

        def head_pair_stack(x):
            zero = jnp.zeros_like(x)
            return jnp.concatenate([jnp.where(head0, x, zero), jnp.where(head0, zero, x)], axis=0)

        for gi, d in enumerate(DILATIONS):
            @pl.when(g == gi)
            def _(d=d):
                span = CHUNK * d

                def tile(t, carry):
                    q0, n = _att_tile_offsets(t, d)
                    rows = _rows(q0, d)
                    cur = _rows(ATT_ROWS + q0, d)
                    prev = _rows(ATT_ROWS + q0 - span, d)
                    first = (c == 0) & (n == 0)
                    inner = jnp.where(first, 0, 1)
                    g_cur = _rows(base + q0, d)
                    g_prev = _rows(jnp.where(first, q0, base + q0 - span), d)
                    q2 = _bf(q_ref[rows, :] * ATT_SCALE)
                    q_t = q2.T
                    k2 = _bf(jnp.concatenate([kk[prev, :], kk[cur, :]], axis=0))
                    k_t = k2.T
                    v2 = _bf(jnp.concatenate([vv[prev, :], vv[cur, :]], axis=0))
                    do2 = do_ref[rows, :]
                    do_b = _bf(do2)
                    do_t = do_b.T
                    lse_t = l_ref[rows, :].T
                    dd_t = (do2 * o_ref[rows, :]).T
                    lse = jnp.concatenate([lse_t[0:1], lse_t[HEAD_DIM:HEAD_DIM + 1]], axis=1)
                    delta = jnp.concatenate([jnp.sum(dd_t[:HEAD_DIM], axis=0, keepdims=True),
                                             jnp.sum(dd_t[HEAD_DIM:], axis=0, keepdims=True)], axis=1)
                    s = _dot(k2, _head_pair_columns(q_t), NN) + b_ref[inner]
                    p = jnp.exp(s - lse)
                    ds = p * (_dot(v2, _head_pair_columns(do_t), NN) - delta)
                    ds_ref[...] += ds
                    ds_b = _bf(ds)
                    dq_t = _head_pair_rows(_dot(k_t, ds_b, NN))
                    dk2 = _dot(ds_b, head_pair_stack(q2), NN)
                    dv2 = _dot(_bf(p), head_pair_stack(do_b), NN)
                    dq_ref[rows, :] = (dq_t * ATT_SCALE).T
                    dk_ref[g_prev, :] += dk2[0:CHUNK]
                    dk_ref[g_cur, :] += dk2[CHUNK:2 * CHUNK]
                    dv_ref[g_prev, :] += dv2[0:CHUNK]
                    dv_ref[g_cur, :] += dv2[CHUNK:2 * CHUNK]
                    return carry

                lax.fori_loop(0, tiles, tile, 0, unroll=4)

    order = lambda hp, g, c: (hp, g, c)
    chunk = pl.BlockSpec((ATT_ROWS, LANES), lambda hp, g, c: (c, hp))
    slab = pl.BlockSpec((S, LANES), lambda hp, g, c: (0, g * 4 + hp))
    width = N_DIL * ATT_WIDTH
    dq, dk, dv, ds_sums = pl.pallas_call(
        body, name=name, grid=(ATT_HEADS // 2, N_DIL, n_chunks),
        in_specs=_att_specs(order) + [chunk, chunk, chunk,
                                      pl.BlockSpec((None, 2, None, 2 * CHUNK, 2 * CHUNK),
                                                   lambda hp, g, c: (g, 0, hp, 0, 0))],
        out_specs=[pl.BlockSpec((ATT_ROWS, LANES), lambda hp, g, c: (c, g * 4 + hp)), slab, slab,
                   pl.BlockSpec((None, None, 2 * CHUNK, 2 * CHUNK), lambda hp, g, c: (g, hp, 0, 0))],
        out_shape=[jax.ShapeDtypeStruct((S, width), F32), jax.ShapeDtypeStruct((S, width), F32),
                   jax.ShapeDtypeStruct((S, width), F32),
                   jax.ShapeDtypeStruct((N_DIL, ATT_HEADS // 2, 2 * CHUNK, 2 * CHUNK), F32)],
        scratch_shapes=[pltpu.VMEM((2 * ATT_ROWS, LANES), F32), pltpu.VMEM((2 * ATT_ROWS, LANES), F32)],
        compiler_params=_params(("parallel", "parallel", "arbitrary")),
    )(qkv, qkv, qkv, qkv, qkv, o, lse, d_o, bias_tiles)
    ds_sums = ds_sums.reshape(N_DIL, ATT_HEADS // 2, 2 * CHUNK, 2, CHUNK)
    ds_sums = jnp.transpose(ds_sums, (0, 1, 3, 2, 4)).reshape(N_DIL, ATT_HEADS, 2 * CHUNK, CHUNK)
    return dq, dk, dv, ds_sums


def _bias_grad(name, ds_sums):
    flat = ds_sums.reshape(N_DIL, ATT_HEADS, TILE_ELEMS)

    def body(oh_ref, ds_ref, out_ref):
        for g in range(N_DIL):
            out_ref[g] = lax.dot_general(oh_ref[g], ds_ref[g], (NT, ((), ())), precision=lax.Precision.HIGHEST,
                                         preferred_element_type=F32)

    out = pl.pallas_call(
        body, name=name, out_shape=jax.ShapeDtypeStruct((N_DIL, N_BUCKETS, ATT_HEADS), F32),
        compiler_params=_params(),
    )(_bucket_onehot(), flat)
    return jnp.transpose(out, (1, 0, 2)).reshape(N_BUCKETS, N_DIL * ATT_HEADS)


def _peers():
    x, y, c = lax.axis_index("x"), lax.axis_index("y"), lax.axis_index("c")
    me = 4 * x + 2 * y + c
    others = [(x, y, 1 - c), (1 - x, y, c), (x, 1 - y, c), (1 - x, 1 - y, c),
              (1 - x, y, 1 - c), (x, 1 - y, 1 - c), (1 - x, 1 - y, 1 - c)]
    return me, others


def _slot(dev):
    return 4 * dev[0] + 2 * dev[1] + dev[2]


_HBM =pl.BlockSpec(memory_space=pltpu.HBM)
_SEM = pl.BlockSpec(memory_space=pltpu.SEMAPHORE)
_EFFECT = pltpu.SideEffectType.DATAFLOW_SIDE_EFFECTING


def _my_slot():
    return 4 * lax.axis_index("x") + 2 * lax.axis_index("y") + lax.axis_index("c")


def _exchange_copy(src_ref, land_ref, send_sems, recv_sems, k, dev, me, scatter, arriving):
    src = src_ref.at[me if arriving else _slot(dev)] if scatter else src_ref
    dst = land_ref.at[_slot(dev) if arriving else me]
    return pltpu.make_async_remote_copy(src_ref=src, dst_ref=dst, send_sem=send_sems.at[k], recv_sem=recv_sems.at[k],
                                        device_id=dev, device_id_type=MESH)


def _exchange_start(name, srcs, scatter):
    n = len(srcs)
    me = _my_slot()
    landings = []
    for src in srcs:
        own = lax.dynamic_index_in_dim(src, me, 0, keepdims=True) if scatter else src[None]
        landings.append(lax.dynamic_update_slice(lax.empty((N_DEV,) + src.shape[-2:], src.dtype), own, (me, 0, 0)))

    def body(*refs):
        src_refs, land_refs = refs[:n], refs[n:2 * n]
        send_sems, recv_sems = refs[2 * n:2 * n + 2]
        token = refs[-1]
        me, others = _peers()
        for p in range(n):
            for k, dev in enumerate(others):
                _exchange_copy(src_refs[p], land_refs[p], send_sems, recv_sems, p * (N_DEV - 1) + k, dev, me,
                               scatter, False).start()
        token[...] = jnp.zeros_like(token)

    sems = pltpu.SemaphoreType.DMA((n * (N_DEV - 1),))
    hbm = lambda a: pltpu.with_memory_space_constraint(a, pltpu.HBM)
    outs = pl.pallas_call(
        body, name=name,
        out_shape=(sems, sems, *[pltpu.HBM(a.shape, a.dtype) for a in srcs + landings],
                   jax.ShapeDtypeStruct((8, LANES), F32)),
        in_specs=(_HBM,) * (2 * n), out_specs=(_SEM, _SEM) + (_HBM,) * (2 * n) + (pl.BlockSpec(memory_space=pltpu.VMEM),),
        input_output_aliases={i: 2 + i for i in range(2 * n)},
        compiler_params=pltpu.CompilerParams(has_side_effects=_EFFECT),
    )(*[hbm(a) for a in srcs + landings])
    return (outs[0], outs[1], list(outs[2:2 + n]), list(outs[2 + n:2 + 2 * n]), scatter), outs[-1]


def _exchange_wait(name, handle, after):
    send_sems, recv_sems, src_thru, land_thru, scatter = handle
    n = len(src_thru)

    def body(*refs):
        src_refs, land_refs = refs[:n], refs[n:2 * n]
        send_sems, recv_sems = refs[2 * n:2 * n + 2]
        me, others = _peers()
        for p in range(n):
            for k, dev in enumerate(others):
                cp = _exchange_copy(src_refs[p], land_refs[p], send_sems, recv_sems, p * (N_DEV - 1) + k, dev, me,
                                    scatter, True)
                cp.wait_send()
                cp.wait_recv()

    outs = pl.pallas_call(
        body, name=name,
        out_shape=tuple(pltpu.HBM(a.shape, a.dtype) for a in src_thru + land_thru),
        in_specs=(_HBM,) * (2 * n) + (_SEM, _SEM, pl.BlockSpec(memory_space=pl.ANY)), out_specs=(_HBM,) * (2 * n),
        input_output_aliases={i: i for i in range(2 * n)},
        compiler_params=pltpu.CompilerParams(has_side_effects=_EFFECT),
    )(*src_thru, *land_thru, send_sems, recv_sems, after)
    return list(outs[n:])


def _all_reduce_small(name, buf):
    rows = buf.shape[0]
    rb = rows // N_DEV

    def body(x_ref, out_ref, stage, send1, recv1, send2, recv2):
        me, others = _peers()

        def block(ref, k):
            return ref.at[pl.ds(k * rb, rb), :]

        first = [pltpu.make_async_remote_copy(src_ref=block(x_ref, _slot(dev)), dst_ref=stage.at[me],
                                              send_sem=send1.at[k], recv_sem=recv1.at[k], device_id=dev,
                                              device_id_type=MESH) for k, dev in enumerate(others)]
        for cp in first:
            cp.start()
        stage[me] = x_ref[pl.ds(pl.multiple_of(me * rb, 8), rb), :]
        for k, dev in enumerate(others):
            pltpu.make_async_remote_copy(src_ref=block(x_ref, me), dst_ref=stage.at[_slot(dev)],
                                         send_sem=send1.at[k], recv_sem=recv1.at[k], device_id=dev,
                                         device_id_type=MESH).wait_recv()
        total = stage[0]
        for j in range(1, N_DEV):
            total = total + stage[j]
        out_ref[pl.ds(pl.multiple_of(me * rb, 8), rb), :] = total
        second = [pltpu.make_async_remote_copy(src_ref=block(out_ref, me), dst_ref=block(out_ref, me),
                                               send_sem=send2.at[k], recv_sem=recv2.at[k], device_id=dev,
                                               device_id_type=MESH) for k, dev in enumerate(others)]
        for cp in second:
            cp.start()
        for k, dev in enumerate(others):
            pltpu.make_async_remote_copy(src_ref=block(out_ref, me), dst_ref=block(out_ref, _slot(dev)),
                                         send_sem=send2.at[k], recv_sem=recv2.at[k], device_id=dev,
                                         device_id_type=MESH).wait_recv()
        for cp in first + second:
            cp.wait_send()

    sems = pltpu.SemaphoreType.DMA((N_DEV - 1,))
    return pl.pallas_call(
        body, name=name,
        in_specs=[pl.BlockSpec(memory_space=pltpu.VMEM)],
        out_specs=pl.BlockSpec(memory_space=pltpu.VMEM),
        out_shape=jax.ShapeDtypeStruct(buf.shape, F32),
        scratch_shapes=[pltpu.VMEM((N_DEV, rb, LANES), F32), sems, sems, sems, sems],
        compiler_params=pltpu.CompilerParams(vmem_limit_bytes=VMEM_LIMIT_BYTES),
    )(buf)


def _adamw_math(w, g, m, v):
    m = ADAM_B1 * m + (1.0 - ADAM_B1) * g
    v = ADAM_B2 * v + (1.0 - ADAM_B2) * (g * g)
    m_hat = m / (1.0 - ADAM_B1 ** ADAM_STEP)
    v_hat = v / (1.0 - ADAM_B2 ** ADAM_STEP)
    delta = -ADAM_LR * (m_hat / (jnp.sqrt(v_hat) + ADAM_EPS) + ADAM_WD * w)
    return delta, m, v


def _adamw(name, parts, w, m, v, tr=128):
    P, R, W = parts.shape
    tr = min(tr, R)

    def body(p_ref, w_ref, m_ref, v_ref, g_out, d_out, m_out, v_out):
        g = p_ref[0].astype(F32)
        for j in range(1, P):
            g = g + p_ref[j].astype(F32)
        delta, m_new, v_new = _adamw_math(w_ref[...], g, m_ref[...], v_ref[...])
        g_out[...] = g
        d_out[...] = delta
        m_out[...] = m_new
        v_out[...] = v_new

    row = pl.BlockSpec((tr, W), lambda i: (i, 0))
    shape = jax.ShapeDtypeStruct((R, W), F32)
    return pl.pallas_call(
        body, name=name, grid=(R // tr,),
        in_specs=[pl.BlockSpec((P, tr, W), lambda i: (0, i, 0)), row, row, row],
        out_specs=[row, row, row, row],
        out_shape=[shape, shape, shape, shape],
        compiler_params=_params(("parallel",)),
    )(parts, w, m, v)


def _adamw_shard(name, parts, w, m, v, layer, earlier=None, tr=256):
    L, K, N = w.shape
    tr = min(tr, K)
    n_prev = 0 if earlier is None else 4

    def body(p_ref, w_ref, m_ref, v_ref, *rest):
        g_out, d_out, m_out, v_out = rest[n_prev:]
        g = p_ref[0].astype(F32)
        for j in range(1, N_DEV):
            g = g + p_ref[j].astype(F32)
        delta, m_new, v_new = _adamw_math(w_ref[...], g, m_ref[...], v_ref[...])
        g_out[...] = g
        d_out[...] = delta
        m_out[...] = m_new
        v_out[...] = v_new

    row = pl.BlockSpec((None, tr, N), lambda i: (layer, i, 0))
    shape = jax.ShapeDtypeStruct((L, K, N), F32)
    return pl.pallas_call(
        body, name=name, grid=(K // tr,),
        in_specs=[pl.BlockSpec((N_DEV, tr, N), lambda i: (0, i, 0)), row, row, row]
        + [pl.BlockSpec(memory_space=pl.ANY)] * n_prev,
        out_specs=[row, row, row, row],
        out_shape=[shape, shape, shape, shape],
        input_output_aliases={4 + j: j for j in range(n_prev)},
        compiler_params=_params(("parallel",)),
    )(parts, w, m, v, *(earlier or ()))


def _column_slots(full):
    K, N = full.shape
    return jnp.transpose(full.reshape(K, N_DEV, N // N_DEV), (1, 0, 2))


def _from_column_slots(slots):
    _, K, n = slots.shape
    return jnp.transpose(slots, (1, 0, 2)).reshape(K, N_DEV * n)


_SMALL =("mix_norm_g", "mlp_norm_g", "final_norm_g", "a_ln_g", "a_ln_b", "a_w_s", "a_b_s", "rel_bias")


def _pack_small(vals):
    pieces = []
    for n in _SMALL:
        flat = vals[n].reshape(-1)
        pad = (-flat.shape[0]) % (8 * LANES)
        pieces.append(jnp.pad(flat, (0, pad)).reshape(-1, LANES))
    rows = sum(p.shape[0] for p in pieces)
    tail = (-rows) % (8 * N_DEV)
    if tail:
        pieces.append(jnp.zeros((tail, LANES), F32))
    return jnp.concatenate(pieces, axis=0)


def _unpack_small(buf, like):
    out = {}
    r = 0
    for n in _SMALL:
        size = like[n].size
        nrows = -(-size // (8 * LANES)) * 8
        out[n] = buf[r:r + nrows].reshape(-1)[:size].reshape(like[n].shape)
        r += nrows
    return out


_STAGES = (("gate", ("a_w_in", "a_w_out"), 0),
           ("mlp0", ("w_up", "w_down"), 0),
           ("att", ("b_w_qkv", "b_w_out"), 0),
           ("mlp1", ("w_up", "w_down"), 1))


def kernel(x, mix_norm_g, mlp_norm_g, final_norm_g, a_w_in, a_ln_g, a_ln_b, a_w_s, a_b_s, a_w_out, b_w_qkv, b_w_out, rel_bias, w_up, w_down, loss_target, m_mix_norm_g, m_mlp_norm_g, m_final_norm_g, m_a_w_in, m_a_ln_g, m_a_ln_b, m_a_w_s, m_a_b_s, m_a_w_out, m_b_w_qkv, m_b_w_out, m_rel_bias, m_w_up, m_w_down, v_mix_norm_g, v_mlp_norm_g, v_final_norm_g, v_a_w_in, v_a_ln_g, v_a_ln_b, v_a_w_s, v_a_b_s, v_a_w_out, v_b_w_qkv, v_b_w_out, v_rel_bias, v_w_up, v_w_down):
    w = dict(mix_norm_g=mix_norm_g, mlp_norm_g=mlp_norm_g, final_norm_g=final_norm_g, a_w_in=a_w_in, a_ln_g=a_ln_g,
             a_ln_b=a_ln_b, a_w_s=a_w_s, a_b_s=a_b_s, a_w_out=a_w_out, b_w_qkv=b_w_qkv, b_w_out=b_w_out,
             rel_bias=rel_bias, w_up=w_up, w_down=w_down)
    m = dict(mix_norm_g=m_mix_norm_g, mlp_norm_g=m_mlp_norm_g, final_norm_g=m_final_norm_g, a_w_in=m_a_w_in,
             a_ln_g=m_a_ln_g, a_ln_b=m_a_ln_b, a_w_s=m_a_w_s, a_b_s=m_a_b_s, a_w_out=m_a_w_out, b_w_qkv=m_b_w_qkv,
             b_w_out=m_b_w_out, rel_bias=m_rel_bias, w_up=m_w_up, w_down=m_w_down)
    v = dict(mix_norm_g=v_mix_norm_g, mlp_norm_g=v_mlp_norm_g, final_norm_g=v_final_norm_g, a_w_in=v_a_w_in,
             a_ln_g=v_a_ln_g, a_ln_b=v_a_ln_b, a_w_s=v_a_w_s, a_b_s=v_a_b_s, a_w_out=v_a_w_out, b_w_qkv=v_b_w_qkv,
             b_w_out=v_b_w_out, rel_bias=v_rel_bias, w_up=v_w_up, w_down=v_w_down)

    stages = {s: (names, layer) for s, names, layer in _STAGES}
    order = [s for s, _, _ in _STAGES]

    def shards_of(stage):
        names, layer = stages[stage]
        return [_bf(w[n][layer]) for n in names]

    pending = {}
    pending[order[0]], first_token = _exchange_start("gather_" + order[0] + "_start", shards_of(order[0]), False)

    def get_weights(stage, dep):
        gathered = _exchange_wait("gather_" + stage + "_wait", pending.pop(stage), dep)
        nxt = order.index(stage) + 1
        token = None
        if nxt < len(order):
            shards, gathered = lax.optimization_barrier((shards_of(order[nxt]), gathered))
            pending[order[nxt]], token = _exchange_start("gather_" + order[nxt] + "_start", shards, False)
        return gathered, token

    sent = {}

    def put_grads(stage, slot_grads):
        sent[stage], token = _exchange_start("scatter_" + stage + "_start", slot_grads, True)
        return token

    loss_local, grad_x, small_g = _local_step(
        x[0], loss_target[0], mix_norm_g, mlp_norm_g, final_norm_g, a_ln_g, a_ln_b, a_w_s, a_b_s, rel_bias,
        get_weights, put_grads, first_token)

    results = {}
    for stage in reversed(order):
        names, layer = stages[stage]
        received = _exchange_wait("scatter_" + stage + "_wait", sent[stage], grad_x)
        for n, parts in zip(names, received):
            results[n] = _adamw_shard("adamw_%s_%s" % (stage, n), parts, w[n], m[n], v[n], layer, results.get(n))

    reduced = _all_reduce_small("reduce_small", _pack_small(small_g))
    small = [_unpack_small(b, w) for b in _adamw("adamw_small", reduced[None], _pack_small(w), _pack_small(m),
                                                 _pack_small(v), tr=reduced.shape[0])]

    outs = []
    for j in range(4):
        outs.extend(small[j][n] if n in _SMALL else results[n][j] for n in w)
    loss = lax.psum(loss_local, ("x", "y", "c"))
    return (loss, grad_x[None], *outs)


def _local_step(xs, tgt, mix_norm_g, mlp_norm_g, final_norm_g, a_ln_g, a_ln_b, a_w_s, a_b_s, rel_bias,
                get_weights, put_grads, first_token=None):
    D = xs.shape[-1]
    g_mix = [mix_norm_g[l][None, :] for l in range(2)]
    g_mlp = [mlp_norm_g[l][None, :] for l in range(2)]
    g_fin = final_norm_g[None, :]
    ln_g, ln_b = a_ln_g, a_ln_b
    causal = jnp.tril(jnp.ones((CHUNK, CHUNK), dtype=bool))
    wm = _bf(jnp.where(causal[None], a_w_s[0], 0.0))
    bs_full = jnp.repeat(a_b_s[0].T, D // GROUPS, axis=1)
    bias_tiles = _bias_tiles("att_bias", rel_bias, after=first_token)

    (win, wout), token = get_weights("gate", bias_tiles)
    wout = wout.reshape(-1, D)
    y0 = _rms_fwd("rms_mix0", xs, g_mix[0], after=token)
    uvp = _mm_nn("gate_in", y0, win, tm=512, nc=win.shape[2], shards=True)
    z = _gate_fwd("gate_mid", uvp, ln_g, ln_b, wm, bs_full)
    h1, y1 = _mm_nn("gate_out", z, wout, tm=512, nc=512, epi="res", extra=xs, norm_g=g_mlp[0])
    (wup0, wdn0), token = get_weights("mlp0", h1)
    wdn0 = wdn0.reshape(-1, D)
    a0, f0 = _mm_nn("mlp0_up", y1, wup0, tm=512, nc=wup0.shape[2], epi="relu2", shards=True, after=token)
    h2, y2 = _mm_nn("mlp0_down", f0, wdn0, tm=512, nc=512, epi="res", extra=h1, norm_g=g_mix[1])
    (wqkv, wo), token = get_weights("att", h2)
    wqkv, wo = _from_column_slots(wqkv), _from_column_slots(wo)
    qkv = _mm_nn("att_qkv", y2, wqkv, tm=256, nc=512, after=token)
    o_att, lse = _att_merge("att_merge", *_att_fwd("att_fwd", qkv, bias_tiles))
    h3, y3 = _mm_nn("att_out", o_att, wo, tm=512, nc=512, epi="res", extra=h2, norm_g=g_mlp[1])
    (wup1, wdn1), _ = get_weights("mlp1", h3)
    wdn1 = wdn1.reshape(-1, D)
    a1, f1 = _mm_nn("mlp1_up", y3, wup1, tm=512, nc=wup1.shape[2], epi="relu2", shards=True)
    h4 = _mm_nn("mlp1_down", f1, wdn1, tm=512, nc=512, epi="res", extra=h3)
    dh, dg_fin, err2 = _final_loss("final_loss", h4, g_fin, tgt)
    loss_local = 0.5 * jnp.sum(err2) / D

    def mlp_bwd(tag, dh, h_in, y, a, f, wup_l, wdn_l, g_row, after):
        da = _mm_nt(tag + "_dact", dh, wdn_l, tm=512, nc=512, epi="mask2relu", extra=a, after=after)
        g_dn = _mm_tn(tag + "_dwdown", f, dh, t1=1024, tn=1024)
        g_up = _mm_tn(tag + "_dwup", y, da, t1=1024, tn=1024, slot_cols=wup_l.shape[2])
        dh_in, dg = _mm_nt_rms_bwd(tag + "_dy", [(da, wup_l, *_whole(wup_l))], h_in, g_row, dh, tm=512, nc=512,
                                   shards=True)
        return dh_in, dg, put_grads(tag, [g_up, g_dn.reshape(N_DEV, -1, D)])

    dh3, dg_mlp1, token = mlp_bwd("mlp1", dh, h3, y3, a1, f1, wup1, wdn1, g_mlp[1], None)

    d_o = _mm_nt("att_dout", dh3, wo, tm=512, nc=512, after=token)
    g_wo = _mm_tn("att_dwo", o_att, dh3, t1=512, tn=1024)
    dq, dk, dv, ds_sums = _att_bwd("att_bwd", qkv, o_att, lse, d_o, bias_tiles)
    part_w = N_DIL * ATT_WIDTH
    g_qkv = [_mm_tn("att_dwqkv%d" % p, y2, t, t1=1024, tn=part_w, tm=1024) for p, t in enumerate((dq, dk, dv))]
    dh2, dg_mix1 = _mm_nt_rms_bwd("att_dy", [(t, wqkv, (D, part_w), (0, p)) for p, t in enumerate((dq, dk, dv))],
                                  h2, g_mix[1], dh3, tm=256, nc=512)
    token = put_grads("att", [_column_slots(jnp.concatenate(g_qkv, axis=1)), _column_slots(g_wo)])

    dh1, dg_mlp0, token = mlp_bwd("mlp0", dh2, h1, y1, a0, f0, wup0, wdn0, g_mlp[0], token)

    dz = _mm_nt("gate_dz", dh1, wout, tm=512, nc=512, after=token)
    g_wout = _mm_tn("gate_dwout", z, dh1, t1=1024, tn=1024)
    duvp, d_wm, d_mixed, d_lng, d_lnb = _gate_bwd("gate_dmid", uvp, dz, ln_g, ln_b, wm, bs_full)
    g_win = _mm_tn("gate_dwin", y0, duvp, t1=1024, tn=1024, slot_cols=win.shape[2])
    token = put_grads("gate", [g_win, g_wout.reshape(N_DEV, -1, D)])
    grad_x, dg_mix0 = _mm_nt_rms_bwd("gate_dy", [(duvp, win, *_whole(win))], xs, g_mix[0], dh1, tm=512, nc=512,
                                     after=token, shards=True)

    small_g = dict(
        mix_norm_g=jnp.concatenate([dg_mix0, dg_mix1], axis=0),
        mlp_norm_g=jnp.concatenate([dg_mlp0, dg_mlp1], axis=0),
        final_norm_g=dg_fin[0], a_ln_g=d_lng, a_ln_b=d_lnb, a_w_s=d_wm[None],
        a_b_s=jnp.sum(d_mixed.reshape(CHUNK, GROUPS, D // GROUPS), axis=2).T[None],
        rel_bias=_bias_grad("att_dbias", ds_sums))
    return loss_local, grad_x, small_g
```

```python
import functools
import math

import jax
import jax.numpy as jnp
from jax import lax
from jax.experimental import pallas as pl
from jax.experimental.pallas import tpu as pltpu

F32 = jnp.float32
BF16 = jnp.bfloat16
MESH = pl.DeviceIdType.MESH

N_DEV = 8
EPS = 1e-6
NEG_INF = -1e30
CHUNK = 128
GROUPS = 8
HEAD_DIM = 64
ATT_HEADS = 8
ATT_WIDTH = ATT_HEADS * HEAD_DIM
DILATIONS = (1, 4, 16)
N_DIL = len(DILATIONS)
N_BUCKETS = 32
MAX_EXACT = N_BUCKETS // 2
REL_MAX_DISTANCE = 2048
ATT_ROWS = 2048
ATT_SCALE = HEAD_DIM ** -0.5
LANES = 128

ADAM_LR = 0.001
ADAM_B1 = 0.9
ADAM_B2 = 0.999
ADAM_EPS = 1e-08
ADAM_WD = 0.01
ADAM_STEP = 10

VMEM_LIMIT_BYTES = 56 * 1024 * 1024


def _params(semantics=None):
    return pltpu.CompilerParams(dimension_semantics=semantics, vmem_limit_bytes=VMEM_LIMIT_BYTES)


def _bf(v):
    return v.astype(BF16)


def _dot(a, b, dims):
    return lax.dot_general(a, b, (dims, ((), ())), preferred_element_type=F32)


NN = ((1,), (0,))
NT = ((1,), (1,))
TN = ((0,), (0,))


def _after_operand(after):
    if after is None:
        return [], []
    return [after], [pl.BlockSpec(memory_space=pl.ANY)]


def _rms_fwd(name, x, g, tm=512, after=None):
    S, D = x.shape
    after_args, after_specs = _after_operand(after)

    def body(x_ref, g_ref, *rest):
        y_ref = rest[-1]
        xv = x_ref[...]
        r = lax.rsqrt(jnp.mean(xv * xv, axis=-1, keepdims=True) + EPS)
        y_ref[...] = _bf(xv * r * g_ref[...])

    return pl.pallas_call(
        body, name=name, grid=(S // tm,),
        in_specs=[pl.BlockSpec((tm, D), lambda i: (i, 0)), pl.BlockSpec((1, D), lambda i: (0, 0))] + after_specs,
        out_specs=pl.BlockSpec((tm, D), lambda i: (i, 0)),
        out_shape=jax.ShapeDtypeStruct((S, D), BF16),
        compiler_params=_params(("parallel",)),
    )(x, g, *after_args)


def _final_loss(name, h, g, target, tm=512):
    S, D = h.shape

    def body(h_ref, g_ref, t_ref, dh_ref, dg_ref, l_ref):
        i = pl.program_id(0)
        xv = h_ref[...]
        r = lax.rsqrt(jnp.mean(xv * xv, axis=-1, keepdims=True) + EPS)
        xh = xv * r
        gv = g_ref[...]
        e = xh * gv - t_ref[...]
        dout = e / D
        dyg = dout * gv
        c = jnp.mean(dyg * xh, axis=-1, keepdims=True)
        dh_ref[...] = r * (dyg - xh * c)
        dg_part = jnp.sum(dout * xh, axis=0, keepdims=True)
        l_part = jnp.sum(e * e, axis=0, keepdims=True)

        @pl.when(i == 0)
        def _():
            dg_ref[...] = dg_part
            l_ref[...] = l_part

        @pl.when(i > 0)
        def _():
            dg_ref[...] += dg_part
            l_ref[...] += l_part

    row = pl.BlockSpec((tm, D), lambda i: (i, 0))
    vec = pl.BlockSpec((1, D), lambda i: (0, 0))
    return pl.pallas_call(
        body, name=name, grid=(S // tm,),
        in_specs=[row, vec, row],
        out_specs=[row, vec, vec],
        out_shape=[jax.ShapeDtypeStruct((S, D), F32), jax.ShapeDtypeStruct((1, D), F32),
                   jax.ShapeDtypeStruct((1, D), F32)],
        compiler_params=_params(("arbitrary",)),
    )(h, g, target)


def _chunk_product(a_vals, w_refs, j, nc, nt, shards):
    cols = slice(j * nc, (j + 1) * nc)
    acc = None
    for a_v, w_ref in zip(a_vals, w_refs):
        if not shards:
            terms = [_dot(a_v, w_ref[cols, :], NT) if nt else _dot(a_v, w_ref[:, cols], NN)]
        elif nt:
            nl = w_ref.shape[2]
            terms = [_dot(a_v[:, k * nl:(k + 1) * nl], w_ref[k, cols, :], NT) for k in range(N_DEV)]
        else:
            terms = [_dot(a_v, w_ref[j], NN)]
        for t in terms:
            acc = t if acc is None else acc + t
    return cols, acc


def _mm_rows(name, pairs, n_out, *, nt, tm, nc, epi="plain", extra=None, out_dtype=F32, after=None, shards=False,
             norm_g=None):
    M = pairs[0][0].shape[0]
    np_ = len(pairs)
    after_args, after_specs = _after_operand(after)

    def body(*refs):
        a_refs = refs[:np_]
        w_refs = refs[np_:2 * np_]
        pos = 2 * np_
        e_ref = None
        if extra is not None:
            e_ref = refs[pos]
            pos += 1
        if norm_g is not None:
            g_ref = refs[pos]
            pos += 1
        pos += len(after_args)
        outs = refs[pos:]
        a_vals = [_bf(a[...]) for a in a_refs]
        for j in range(n_out // nc):
            cols, acc = _chunk_product(a_vals, w_refs, j, nc, nt, shards)
            if epi == "plain":
                outs[0][:, cols] = acc.astype(out_dtype)
            elif epi == "res":
                outs[0][:, cols] = e_ref[:, cols] + acc
            elif epi == "relu2":
                outs[0][:, cols] = _bf(acc)
                rl = jnp.maximum(acc, 0.0)
                outs[1][:, cols] = _bf(rl * rl)
            elif epi == "mask2relu":
                outs[0][:, cols] = _bf(acc * (2.0 * jnp.maximum(e_ref[:, cols].astype(F32), 0.0)))
        if norm_g is not None:
            hv = outs[0][...]
            r = lax.rsqrt(jnp.mean(hv * hv, axis=-1, keepdims=True) + EPS)
            outs[1][...] = _bf(hv * r * g_ref[...])

    in_specs = [pl.BlockSpec((tm, a.shape[1]), lambda i: (i, 0)) for a, _, _, _ in pairs]
    for _, _, wshape, widx in pairs:
        in_specs.append(pl.BlockSpec(wshape, functools.partial(lambda i, widx: widx, widx=widx)))
    args = [a for a, _, _, _ in pairs] + [w for _, w, _, _ in pairs]
    if extra is not None:
        in_specs.append(pl.BlockSpec((tm, n_out), lambda i: (i, 0)))
        args.append(extra)
    if norm_g is not None:
        in_specs.append(pl.BlockSpec((1, n_out), lambda i: (0, 0)))
        args.append(norm_g)
    in_specs += after_specs
    args += after_args
    row_out = pl.BlockSpec((tm, n_out), lambda i: (i, 0))
    if epi == "relu2":
        out_specs = [row_out, row_out]
        out_shape = [jax.ShapeDtypeStruct((M, n_out), BF16), jax.ShapeDtypeStruct((M, n_out), BF16)]
    elif norm_g is not None:
        out_specs = [row_out, row_out]
        out_shape = [jax.ShapeDtypeStruct((M, n_out), F32), jax.ShapeDtypeStruct((M, n_out), BF16)]
    else:
        dt = BF16 if epi == "mask2relu" else (F32 if epi == "res" else out_dtype)
        out_specs = row_out
        out_shape = jax.ShapeDtypeStruct((M, n_out), dt)
    return pl.pallas_call(
        body, name=name, grid=(M // tm,), in_specs=in_specs, out_specs=out_specs, out_shape=out_shape,
        compiler_params=_params(("parallel",)),
    )(*args)


def _whole(w):
    return w.shape, (0,) * w.ndim


def _mm_nn(name, a, w, **kw):
    n_out = w.shape[0] * w.shape[2] if w.ndim == 3 else w.shape[1]
    return _mm_rows(name, [(a, w, *_whole(w))], n_out, nt=False, **kw)


def _mm_nt(name, a, w, **kw):
    return _mm_rows(name, [(a, w, *_whole(w))], w.shape[0], nt=True, **kw)


def _mm_nt_rms_bwd(name, pairs, x, g, dres, *, tm, nc, after=None, shards=False):
    M, D = x.shape
    np_ = len(pairs)
    after_args, after_specs = _after_operand(after)

    def body(*refs):
        a_refs = refs[:np_]
        w_refs = refs[np_:2 * np_]
        x_ref, g_ref, r_ref = refs[2 * np_:2 * np_ + 3]
        dx_ref, dg_ref, dy_sc = refs[-3:]
        i = pl.program_id(0)
        a_vals = [_bf(a[...]) for a in a_refs]
        for j in range(D // nc):
            cols, acc = _chunk_product(a_vals, w_refs, j, nc, True, shards)
            dy_sc[:, cols] = acc
        xv = x_ref[...]
        r = lax.rsqrt(jnp.mean(xv * xv, axis=-1, keepdims=True) + EPS)
        xh = xv * r
        dy_v = dy_sc[...]
        dyg = dy_v * g_ref[...]
        c = jnp.mean(dyg * xh, axis=-1, keepdims=True)
        dx_ref[...] = r_ref[...] + r * (dyg - xh * c)
        part = jnp.sum(dy_v * xh, axis=0, keepdims=True)

        @pl.when(i == 0)
        def _():
            dg_ref[...] = part

        @pl.when(i > 0)
        def _():
            dg_ref[...] += part

    row = pl.BlockSpec((tm, D), lambda i: (i, 0))
    vec = pl.BlockSpec((1, D), lambda i: (0, 0))
    in_specs = [pl.BlockSpec((tm, a.shape[1]), lambda i: (i, 0)) for a, _, _, _ in pairs]
    for _, _, wshape, widx in pairs:
        in_specs.append(pl.BlockSpec(wshape, functools.partial(lambda i, widx: widx, widx=widx)))
    args = [a for a, _, _, _ in pairs] + [w for _, w, _, _ in pairs]
    return pl.pallas_call(
        body, name=name, grid=(M // tm,),
        in_specs=in_specs + [row, vec, row] + after_specs,
        out_specs=[row, vec],
        out_shape=[jax.ShapeDtypeStruct((M, D), F32), jax.ShapeDtypeStruct((1, D), F32)],
        scratch_shapes=[pltpu.VMEM((tm, D), F32)],
        compiler_params=_params(("arbitrary",)),
    )(*args, x, g, dres, *after_args)


def _mm_tn(name, a, b, *, t1, tn, tm=2048, slot_cols=None):
    M, K1 = a.shape
    N = b.shape[1]
    nm = M // tm

    def body(a_ref, b_ref, o_ref, acc_ref):
        m = pl.program_id(2)
        t = _dot(_bf(a_ref[...]), _bf(b_ref[...]), TN)

        @pl.when(m == 0)
        def _():
            acc_ref[...] = t

        @pl.when(m > 0)
        def _():
            acc_ref[...] += t

        @pl.when(m == nm - 1)
        def _():
            if slot_cols is None:
                o_ref[...] = _bf(acc_ref[...])
            else:
                for k in range(tn // slot_cols):
                    o_ref[k] = _bf(acc_ref[:, k * slot_cols:(k + 1) * slot_cols])

    if slot_cols is not None:
        out_spec = pl.BlockSpec((tn // slot_cols, t1, slot_cols), lambda i, j, m: (j, i, 0))
        out_shape = jax.ShapeDtypeStruct((N // slot_cols, K1, slot_cols), BF16)
    else:
        out_spec = pl.BlockSpec((t1, tn), lambda i, j, m: (i, j))
        out_shape = jax.ShapeDtypeStruct((K1, N), BF16)
    return pl.pallas_call(
        body, name=name, grid=(K1 // t1, N // tn, nm),
        in_specs=[pl.BlockSpec((tm, t1), lambda i, j, m: (m, i)), pl.BlockSpec((tm, tn), lambda i, j, m: (m, j))],
        out_specs=out_spec, out_shape=out_shape,
        scratch_shapes=[pltpu.VMEM((t1, tn), F32)],
        compiler_params=_params(("parallel", "parallel", "arbitrary")),
    )(a, b)


_INV_SQRT2 = 1.0 / math.sqrt(2.0)
_INV_SQRT2PI = 1.0 / math.sqrt(2.0 * math.pi)


def _gelu(x):
    return 0.5 * x * (1.0 + lax.erf(x * _INV_SQRT2))


def _gelu_and_grad(x):
    cdf = 0.5 * (1.0 + lax.erf(x * _INV_SQRT2))
    return x * cdf, cdf + x * (_INV_SQRT2PI * jnp.exp(-0.5 * x * x))


def _layer_norm_parts(v):
    mu = jnp.mean(v, axis=-1, keepdims=True)
    xc = v - mu
    rs = lax.rsqrt(jnp.mean(xc * xc, axis=-1, keepdims=True) + EPS)
    return xc * rs, rs


def _gate_fwd(name, uvp, ln_g, ln_b, wm, bs_full, tr=512):
    S, W2 = uvp.shape
    W = W2 // 2
    gd = W // GROUPS

    def body(u_ref, v_ref, lg_ref, lb_ref, wm_ref, bs_ref, z_ref):
        vh, _ = _layer_norm_parts(_gelu(v_ref[...]))
        vn = _bf(vh * lg_ref[...] + lb_ref[...])
        for ci in range(tr // CHUNK):
            rows = slice(ci * CHUNK, (ci + 1) * CHUNK)
            for g in range(GROUPS):
                cols = slice(g * gd, (g + 1) * gd)
                mixed = _dot(wm_ref[g], vn[rows, cols], NN) + bs_ref[:, cols]
                z_ref[rows, cols] = _bf(_gelu(u_ref[rows, cols]) * mixed)

    vec = pl.BlockSpec((1, W), lambda i: (0, 0))
    return pl.pallas_call(
        body, name=name, grid=(S // tr,),
        in_specs=[pl.BlockSpec((tr, W), lambda i: (i, 0)), pl.BlockSpec((tr, W), lambda i: (i, 1)), vec, vec,
                  pl.BlockSpec((GROUPS, CHUNK, CHUNK), lambda i: (0, 0, 0)),
                  pl.BlockSpec((CHUNK, W), lambda i: (0, 0))],
        out_specs=pl.BlockSpec((tr, W), lambda i: (i, 0)),
        out_shape=jax.ShapeDtypeStruct((S, W), BF16),
        compiler_params=_params(("parallel",)),
    )(uvp, uvp, ln_g, ln_b, wm, bs_full)


def _gate_bwd(name, uvp, dz, ln_g, ln_b, wm, bs_full, tr=256):
    S, W2 = uvp.shape
    W = W2 // 2
    gd = W // GROUPS
    n_steps = S // tr

    def body(u_ref, v_ref, dz_ref, lg_ref, lb_ref, wm_ref, bs_ref, duv_ref, dwm_ref, dmx_ref, dlg_ref, dlb_ref,
             dvn_ref):
        i = pl.program_id(0)
        v, dv_dvp = _gelu_and_grad(v_ref[...])
        vh, rs = _layer_norm_parts(v)
        lg = lg_ref[...]
        vn = _bf(vh * lg + lb_ref[...])

        @pl.when(i == 0)
        def _():
            dwm_ref[...] = jnp.zeros_like(dwm_ref)
            dmx_ref[...] = jnp.zeros_like(dmx_ref)
            dlg_ref[...] = jnp.zeros_like(dlg_ref)
            dlb_ref[...] = jnp.zeros_like(dlb_ref)

        for ci in range(tr // CHUNK):
            rows = slice(ci * CHUNK, (ci + 1) * CHUNK)
            for g in range(GROUPS):
                cols = slice(g * gd, (g + 1) * gd)
                u, du_dup = _gelu_and_grad(u_ref[rows, cols])
                dz_v = dz_ref[rows, cols]
                dmixed = dz_v * u
                dmx_ref[:, cols] += dmixed
                dmixed_b = _bf(dmixed)
                mixed = _dot(wm_ref[g], vn[rows, cols], NN) + bs_ref[:, cols]
                duv_ref[rows, cols] = _bf(dz_v * mixed * du_dup)
                dwm_ref[g] += _dot(dmixed_b, vn[rows, cols], NT)
                dvn_ref[rows, cols] = _dot(wm_ref[g], dmixed_b, TN)
        dvn = dvn_ref[...]
        dlg_ref[...] += jnp.sum(dvn * vh, axis=0, keepdims=True)
        dlb_ref[...] += jnp.sum(dvn, axis=0, keepdims=True)
        dvh = dvn * lg
        dv = rs * (dvh - jnp.mean(dvh, axis=-1, keepdims=True) - vh * jnp.mean(dvh * vh, axis=-1, keepdims=True))
        duv_ref[:, W:] = _bf(dv * dv_dvp)

        @pl.when(i == n_steps - 1)
        def _():
            t_idx = lax.broadcasted_iota(jnp.int32, (CHUNK, CHUNK), 0)
            s_idx = lax.broadcasted_iota(jnp.int32, (CHUNK, CHUNK), 1)
            keep = (s_idx <= t_idx).astype(F32)
            for g in range(GROUPS):
                dwm_ref[g] = dwm_ref[g] * keep

    vec = pl.BlockSpec((1, W), lambda i: (0, 0))
    row = pl.BlockSpec((tr, W), lambda i: (i, 0))
    return pl.pallas_call(
        body, name=name, grid=(n_steps,),
        in_specs=[row, pl.BlockSpec((tr, W), lambda i: (i, 1)), row, vec, vec,
                  pl.BlockSpec((GROUPS, CHUNK, CHUNK), lambda i: (0, 0, 0)),
                  pl.BlockSpec((CHUNK, W), lambda i: (0, 0))],
        out_specs=[pl.BlockSpec((tr, W2), lambda i: (i, 0)),
                   pl.BlockSpec((GROUPS, CHUNK, CHUNK), lambda i: (0, 0, 0)),
                   pl.BlockSpec((CHUNK, W), lambda i: (0, 0)), vec, vec],
        out_shape=[jax.ShapeDtypeStruct((S, W2), BF16), jax.ShapeDtypeStruct((GROUPS, CHUNK, CHUNK), F32),
                   jax.ShapeDtypeStruct((CHUNK, W), F32), jax.ShapeDtypeStruct((1, W), F32),
                   jax.ShapeDtypeStruct((1, W), F32)],
        scratch_shapes=[pltpu.VMEM((tr, W), F32)],
        compiler_params=_params(("arbitrary",)),
    )(uvp, uvp, dz, ln_g, ln_b, wm, bs_full)


def _t5_bucket(distance):
    small = distance < MAX_EXACT
    nf = jnp.maximum(distance, 1).astype(F32)
    large = MAX_EXACT + (jnp.log(nf / MAX_EXACT) / math.log(REL_MAX_DISTANCE / MAX_EXACT)
                         * (N_BUCKETS - MAX_EXACT)).astype(jnp.int32)
    large = jnp.minimum(large, N_BUCKETS - 1)
    return jnp.where(small, distance, large)


TILE_ELEMS = 2 * CHUNK * CHUNK


def _band_buckets():
    rel = CHUNK + jnp.arange(CHUNK)[None, :] - jnp.arange(2 * CHUNK)[:, None]
    band = (rel >= 0) & (rel <= CHUNK)
    buckets = [_t5_bucket(jnp.clip(rel, 0, CHUNK) * d) for d in DILATIONS]
    return jnp.stack(buckets), band


def _bucket_onehot():
    buckets, _ = _band_buckets()
    return (buckets.reshape(N_DIL, 1, TILE_ELEMS) == jnp.arange(N_BUCKETS)[None, :, None]).astype(F32)


def _bias_tiles(name, rel_bias, after=None):
    _, band = _band_buckets()
    own = band & (jnp.arange(2 * CHUNK) >= CHUNK)[:, None]
    masks = jnp.stack([own, band]).reshape(2, TILE_ELEMS).astype(F32)
    tables = jnp.transpose(rel_bias.reshape(N_BUCKETS, N_DIL, ATT_HEADS), (1, 2, 0))
    after_args, after_specs = _after_operand(after)

    def body(t_ref, oh_ref, m_ref, *rest):
        out_ref = rest[-1]
        for g in range(N_DIL):
            bias = lax.dot_general(t_ref[g], oh_ref[g], (NN, ((), ())), precision=lax.Precision.HIGHEST,
                                   preferred_element_type=F32)
            for f in range(2):
                out_ref[g, f] = jnp.where(m_ref[f:f + 1, :] > 0.5, bias, NEG_INF)

    whole = pl.BlockSpec(memory_space=pltpu.VMEM)
    out = pl.pallas_call(
        body, name=name, out_shape=jax.ShapeDtypeStruct((N_DIL, 2, ATT_HEADS, TILE_ELEMS), F32),
        in_specs=[whole, whole, whole] + after_specs, out_specs=whole,
        compiler_params=_params(),
    )(tables, _bucket_onehot(), masks, *after_args)
    out = out.reshape(N_DIL, 2, ATT_HEADS // 2, 2, 2 * CHUNK, CHUNK)
    return jnp.transpose(out, (0, 1, 2, 4, 3, 5)).reshape(N_DIL, 2, ATT_HEADS // 2, 2 * CHUNK, 2 * CHUNK)


def _att_specs(order):
    def spec(part, prev):
        def index(*ids):
            hp, g, c = order(*ids)
            return (jnp.maximum(c - 1, 0) if prev else c, part * 3 * 4 + g * 4 + hp)
        return pl.BlockSpec((ATT_ROWS, LANES), index)
    return [spec(0, False), spec(1, False), spec(1, True), spec(2, False), spec(2, True)]


def _rows(start, d):
    if d == 1:
        return pl.ds(pl.multiple_of(start, CHUNK), CHUNK)
    return pl.ds(start, CHUNK, stride=d)


def _att_tile_offsets(t, d):
    n = t // d
    r = t % d
    return n * (CHUNK * d) + r, n


def _head_pair_columns(x_t):
    zeros = jnp.zeros((HEAD_DIM, CHUNK), x_t.dtype)
    return jnp.concatenate([jnp.concatenate([x_t[:HEAD_DIM], zeros], axis=0),
                            jnp.concatenate([zeros, x_t[HEAD_DIM:]], axis=0)], axis=1)


def _head_pair_rows(y):
    return jnp.concatenate([y[:HEAD_DIM, :CHUNK], y[HEAD_DIM:, CHUNK:]], axis=0)


def _stage_prev_cur(dst, prev_ref, cur_ref):
    dst[0:ATT_ROWS, :] = prev_ref[...]
    dst[ATT_ROWS:2 * ATT_ROWS, :] = cur_ref[...]


def _att_fwd(name, qkv, bias_tiles):
    S = qkv.shape[0]
    n_chunks = S // ATT_ROWS
    tiles = ATT_ROWS // CHUNK

    def body(q_ref, kc_ref, kp_ref, vc_ref, vp_ref, b_ref, o_ref, l_ref, kk, vv):
        c = pl.program_id(1)
        g = pl.program_id(2)
        _stage_prev_cur(kk, kp_ref, kc_ref)
        _stage_prev_cur(vv, vp_ref, vc_ref)
        zeros = jnp.zeros((HEAD_DIM, CHUNK), BF16)

        for gi, d in enumerate(DILATIONS):
            @pl.when(g == gi)
            def _(d=d):
                span = CHUNK * d

                def tile(t, carry):
                    q0, n = _att_tile_offsets(t, d)
                    rows = _rows(q0, d)
                    cur = _rows(ATT_ROWS + q0, d)
                    prev = _rows(ATT_ROWS + q0 - span, d)
                    inner = jnp.where((c == 0) & (n == 0), 0, 1)
                    q_t = _bf(q_ref[rows, :] * ATT_SCALE).T
                    k2 = _bf(jnp.concatenate([kk[prev, :], kk[cur, :]], axis=0))
                    v_t = _bf(jnp.concatenate([vv[prev, :], vv[cur, :]], axis=0)).T
                    o_parts, l_parts = [], []
                    for hh in range(2):
                        half = slice(hh * HEAD_DIM, (hh + 1) * HEAD_DIM)
                        q_h = jnp.concatenate([q_t[half], zeros] if hh == 0 else [zeros, q_t[half]], axis=0)
                        s = _dot(k2, q_h, NN) + b_ref[inner, :, hh * CHUNK:(hh + 1) * CHUNK]
                        m = jnp.max(s, axis=0, keepdims=True)
                        p = jnp.exp(s - m)
                        l = jnp.sum(p, axis=0, keepdims=True)
                        o_parts.append(_dot(v_t, _bf(p), NN)[half] / l)
                        l_parts.append(jnp.broadcast_to(m + jnp.log(l), (HEAD_DIM, CHUNK)))
                    o_ref[rows, :] = jnp.concatenate(o_parts, axis=0).T
                    l_ref[rows, :] = jnp.concatenate(l_parts, axis=0).T
                    return carry

                lax.fori_loop(0, tiles, tile, 0, unroll=4)

    order = lambda hp, c, g: (hp, g, c)
    out_spec = pl.BlockSpec((None, ATT_ROWS, LANES), lambda hp, c, g: (g, c, hp))
    shape = jax.ShapeDtypeStruct((N_DIL, S, ATT_WIDTH), F32)
    return pl.pallas_call(
        body, name=name, grid=(ATT_HEADS // 2, n_chunks, N_DIL),
        in_specs=_att_specs(order) + [
            pl.BlockSpec((None, 2, None, 2 * CHUNK, 2 * CHUNK), lambda hp, c, g: (g, 0, hp, 0, 0))],
        out_specs=[out_spec, out_spec],
        out_shape=[shape, shape],
        scratch_shapes=[pltpu.VMEM((2 * ATT_ROWS, LANES), F32), pltpu.VMEM((2 * ATT_ROWS, LANES), F32)],
        compiler_params=_params(("parallel", "parallel", "parallel")),
    )(qkv, qkv, qkv, qkv, qkv, bias_tiles)


def _att_merge(name, o_g, l_g, tm=512):
    _, S, W = o_g.shape

    def body(o_ref, l_ref, out_ref, lse_ref):
        ls = [l_ref[g] for g in range(N_DIL)]
        mx = functools.reduce(jnp.maximum, ls)
        ws = [jnp.exp(l - mx) for l in ls]
        tot = functools.reduce(lambda a, b: a + b, ws)
        acc = ws[0] * o_ref[0]
        for g in range(1, N_DIL):
            acc = acc + ws[g] * o_ref[g]
        out_ref[...] = acc / tot
        lse_ref[...] = mx + jnp.log(tot)

    blk = pl.BlockSpec((N_DIL, tm, W), lambda i: (0, i, 0))
    row = pl.BlockSpec((tm, W), lambda i: (i, 0))
    shape = jax.ShapeDtypeStruct((S, W), F32)
    return pl.pallas_call(
        body, name=name, grid=(S // tm,), in_specs=[blk, blk], out_specs=[row, row], out_shape=[shape, shape],
        compiler_params=_params(("parallel",)),
    )(o_g, l_g)


def _att_bwd(name, qkv, o, lse, d_o, bias_tiles):
    S = qkv.shape[0]
    n_chunks = S // ATT_ROWS
    tiles = ATT_ROWS // CHUNK

    def body(q_ref, kc_ref, kp_ref, vc_ref, vp_ref, o_ref, l_ref, do_ref, b_ref, dq_ref, dk_ref, dv_ref, ds_ref,
             kk, vv):
        g = pl.program_id(1)
        c = pl.program_id(2)

        @pl.when(c == 0)
        def _():
            dk_ref[...] = jnp.zeros_like(dk_ref)
            dv_ref[...] = jnp.zeros_like(dv_ref)
            ds_ref[...] = jnp.zeros_like(ds_ref)

        _stage_prev_cur(kk, kp_ref, kc_ref)
        _stage_prev_cur(vv, vp_ref, vc_ref)
        base = c * ATT_ROWS
        head0 = lax.broadcasted_iota(jnp.int32, (CHUNK, LANES), 1) < HEAD_DIM

        def head_pair_stack(x):
            zero = jnp.zeros_like(x)
            return jnp.concatenate([jnp.where(head0, x, zero), jnp.where(head0, zero, x)], axis=0)

        for gi, d in enumerate(DILATIONS):
            @pl.when(g == gi)
            def _(d=d):
                span = CHUNK * d

                def tile(t, carry):
                    q0, n = _att_tile_offsets(t, d)
                    rows = _rows(q0, d)
                    cur = _rows(ATT_ROWS + q0, d)
                    prev = _rows(ATT_ROWS + q0 - span, d)
                    first = (c == 0) & (n == 0)
                    inner = jnp.where(first, 0, 1)
                    g_cur = _rows(base + q0, d)
                    g_prev = _rows(jnp.where(first, q0, base + q0 - span), d)
                    q2 = _bf(q_ref[rows, :] * ATT_SCALE)
                    q_t = q2.T
                    k2 = _bf(jnp.concatenate([kk[prev, :], kk[cur, :]], axis=0))
                    k_t = k2.T
                    v2 = _bf(jnp.concatenate([vv[prev, :], vv[cur, :]], axis=0))
                    do2 = do_ref[rows, :]
                    do_b = _bf(do2)
                    do_t = do_b.T
                    lse_t = l_ref[rows, :].T
                    dd_t = (do2 * o_ref[rows, :]).T
                    lse = jnp.concatenate([lse_t[0:1], lse_t[HEAD_DIM:HEAD_DIM + 1]], axis=1)
                    delta = jnp.concatenate([jnp.sum(dd_t[:HEAD_DIM], axis=0, keepdims=True),
                                             jnp.sum(dd_t[HEAD_DIM:], axis=0, keepdims=True)], axis=1)
                    s = _dot(k2, _head_pair_columns(q_t), NN) + b_ref[inner]
                    p = jnp.exp(s - lse)
                    ds = p * (_dot(v2, _head_pair_columns(do_t), NN) - delta)
                    ds_ref[...] += ds
                    ds_b = _bf(ds)
                    dq_t = _head_pair_rows(_dot(k_t, ds_b, NN))
                    dk2 = _dot(ds_b, head_pair_stack(q2), NN)
                    dv2 = _dot(_bf(p), head_pair_stack(do_b), NN)
                    dq_ref[rows, :] = (dq_t * ATT_SCALE).T
                    dk_ref[g_prev, :] += dk2[0:CHUNK]
                    dk_ref[g_cur, :] += dk2[CHUNK:2 * CHUNK]
                    dv_ref[g_prev, :] += dv2[0:CHUNK]
                    dv_ref[g_cur, :] += dv2[CHUNK:2 * CHUNK]
                    return carry

                lax.fori_loop(0, tiles, tile, 0, unroll=4)

    order = lambda hp, g, c: (hp, g, c)
    chunk = pl.BlockSpec((ATT_ROWS, LANES), lambda hp, g, c: (c, hp))
    slab = pl.BlockSpec((S, LANES), lambda hp, g, c: (0, g * 4 + hp))
    width = N_DIL * ATT_WIDTH
    dq, dk, dv, ds_sums = pl.pallas_call(
        body, name=name, grid=(ATT_HEADS // 2, N_DIL, n_chunks),
        in_specs=_att_specs(order) + [chunk, chunk, chunk,
                                      pl.BlockSpec((None, 2, None, 2 * CHUNK, 2 * CHUNK),
                                                   lambda hp, g, c: (g, 0, hp, 0, 0))],
        out_specs=[pl.BlockSpec((ATT_ROWS, LANES), lambda hp, g, c: (c, g * 4 + hp)), slab, slab,
                   pl.BlockSpec((None, None, 2 * CHUNK, 2 * CHUNK), lambda hp, g, c: (g, hp, 0, 0))],
        out_shape=[jax.ShapeDtypeStruct((S, width), F32), jax.ShapeDtypeStruct((S, width), F32),
                   jax.ShapeDtypeStruct((S, width), F32),
                   jax.ShapeDtypeStruct((N_DIL, ATT_HEADS // 2, 2 * CHUNK, 2 * CHUNK), F32)],
        scratch_shapes=[pltpu.VMEM((2 * ATT_ROWS, LANES), F32), pltpu.VMEM((2 * ATT_ROWS, LANES), F32)],
        compiler_params=_params(("parallel", "parallel", "arbitrary")),
    )(qkv, qkv, qkv, qkv, qkv, o, lse, d_o, bias_tiles)
    ds_sums = ds_sums.reshape(N_DIL, ATT_HEADS // 2, 2 * CHUNK, 2, CHUNK)
    ds_sums = jnp.transpose(ds_sums, (0, 1, 3, 2, 4)).reshape(N_DIL, ATT_HEADS, 2 * CHUNK, CHUNK)
    return dq, dk, dv, ds_sums


def _bias_grad(name, ds_sums):
    flat = ds_sums.reshape(N_DIL, ATT_HEADS, TILE_ELEMS)

    def body(oh_ref, ds_ref, out_ref):
        for g in range(N_DIL):
            out_ref[g] = lax.dot_general(oh_ref[g], ds_ref[g], (NT, ((), ())), precision=lax.Precision.HIGHEST,
                                         preferred_element_type=F32)

    out = pl.pallas_call(
        body, name=name, out_shape=jax.ShapeDtypeStruct((N_DIL, N_BUCKETS, ATT_HEADS), F32),
        compiler_params=_params(),
    )(_bucket_onehot(), flat)
    return jnp.transpose(out, (1, 0, 2)).reshape(N_BUCKETS, N_DIL * ATT_HEADS)


def _peers():
    x, y, c = lax.axis_index("x"), lax.axis_index("y"), lax.axis_index("c")
    me = 4 * x + 2 * y + c
    others = [(x, y, 1 - c), (1 - x, y, c), (x, 1 - y, c), (1 - x, 1 - y, c),
              (1 - x, y, 1 - c), (x, 1 - y, 1 - c), (1 - x, 1 - y, 1 - c)]
    return me, others


def _slot(dev):
    return 4 * dev[0] + 2 * dev[1] + dev[2]


_HBM =pl.BlockSpec(memory_space=pltpu.HBM)
_SEM = pl.BlockSpec(memory_space=pltpu.SEMAPHORE)
_EFFECT = pltpu.SideEffectType.DATAFLOW_SIDE_EFFECTING


def _my_slot():
    return 4 * lax.axis_index("x") + 2 * lax.axis_index("y") + lax.axis_index("c")


def _exchange_copy(src_ref, land_ref, send_sems, recv_sems, k, dev, me, scatter, arriving):
    src = src_ref.at[me if arriving else _slot(dev)] if scatter else src_ref
    dst = land_ref.at[_slot(dev) if arriving else me]
    return pltpu.make_async_remote_copy(src_ref=src, dst_ref=dst, send_sem=send_sems.at[k], recv_sem=recv_sems.at[k],
                                        device_id=dev, device_id_type=MESH)


def _exchange_start(name, srcs, scatter):
    n = len(srcs)
    me = _my_slot()
    landings = []
    for src in srcs:
        own = lax.dynamic_index_in_dim(src, me, 0, keepdims=True) if scatter else src[None]
        landings.append(lax.dynamic_update_slice(lax.empty((N_DEV,) + src.shape[-2:], src.dtype), own, (me, 0, 0)))

    def body(*refs):
        src_refs, land_refs = refs[:n], refs[n:2 * n]
        send_sems, recv_sems = refs[2 * n:2 * n + 2]
        token = refs[-1]
        me, others = _peers()
        for p in range(n):
            for k, dev in enumerate(others):
                _exchange_copy(src_refs[p], land_refs[p], send_sems, recv_sems, p * (N_DEV - 1) + k, dev, me,
                               scatter, False).start()
        token[...] = jnp.zeros_like(token)

    sems = pltpu.SemaphoreType.DMA((n * (N_DEV - 1),))
    hbm = lambda a: pltpu.with_memory_space_constraint(a, pltpu.HBM)
    outs = pl.pallas_call(
        body, name=name,
        out_shape=(sems, sems, *[pltpu.HBM(a.shape, a.dtype) for a in srcs + landings],
                   jax.ShapeDtypeStruct((8, LANES), F32)),
        in_specs=(_HBM,) * (2 * n), out_specs=(_SEM, _SEM) + (_HBM,) * (2 * n) + (pl.BlockSpec(memory_space=pltpu.VMEM),),
        input_output_aliases={i: 2 + i for i in range(2 * n)},
        compiler_params=pltpu.CompilerParams(has_side_effects=_EFFECT),
    )(*[hbm(a) for a in srcs + landings])
    return (outs[0], outs[1], list(outs[2:2 + n]), list(outs[2 + n:2 + 2 * n]), scatter), outs[-1]


def _exchange_wait(name, handle, after):
    send_sems, recv_sems, src_thru, land_thru, scatter = handle
    n = len(src_thru)

    def body(*refs):
        src_refs, land_refs = refs[:n], refs[n:2 * n]
        send_sems, recv_sems = refs[2 * n:2 * n + 2]
        me, others = _peers()
        for p in range(n):
            for k, dev in enumerate(others):
                cp = _exchange_copy(src_refs[p], land_refs[p], send_sems, recv_sems, p * (N_DEV - 1) + k, dev, me,
                                    scatter, True)
                cp.wait_send()
                cp.wait_recv()

    outs = pl.pallas_call(
        body, name=name,
        out_shape=tuple(pltpu.HBM(a.shape, a.dtype) for a in src_thru + land_thru),
        in_specs=(_HBM,) * (2 * n) + (_SEM, _SEM, pl.BlockSpec(memory_space=pl.ANY)), out_specs=(_HBM,) * (2 * n),
        input_output_aliases={i: i for i in range(2 * n)},
        compiler_params=pltpu.CompilerParams(has_side_effects=_EFFECT),
    )(*src_thru, *land_thru, send_sems, recv_sems, after)
    return list(outs[n:])


def _all_reduce_small(name, buf):
    rows = buf.shape[0]
    rb = rows // N_DEV

    def body(x_ref, out_ref, stage, send1, recv1, send2, recv2):
        me, others = _peers()

        def block(ref, k):
            return ref.at[pl.ds(k * rb, rb), :]

        first = [pltpu.make_async_remote_copy(src_ref=block(x_ref, _slot(dev)), dst_ref=stage.at[me],
                                              send_sem=send1.at[k], recv_sem=recv1.at[k], device_id=dev,
                                              device_id_type=MESH) for k, dev in enumerate(others)]
        for cp in first:
            cp.start()
        stage[me] = x_ref[pl.ds(pl.multiple_of(me * rb, 8), rb), :]
        for k, dev in enumerate(others):
            pltpu.make_async_remote_copy(src_ref=block(x_ref, me), dst_ref=stage.at[_slot(dev)],
                                         send_sem=send1.at[k], recv_sem=recv1.at[k], device_id=dev,
                                         device_id_type=MESH).wait_recv()
        total = stage[0]
        for j in range(1, N_DEV):
            total = total + stage[j]
        out_ref[pl.ds(pl.multiple_of(me * rb, 8), rb), :] = total
        second = [pltpu.make_async_remote_copy(src_ref=block(out_ref, me), dst_ref=block(out_ref, me),
                                               send_sem=send2.at[k], recv_sem=recv2.at[k], device_id=dev,
                                               device_id_type=MESH) for k, dev in enumerate(others)]
        for cp in second:
            cp.start()
        for k, dev in enumerate(others):
            pltpu.make_async_remote_copy(src_ref=block(out_ref, me), dst_ref=block(out_ref, _slot(dev)),
                                         send_sem=send2.at[k], recv_sem=recv2.at[k], device_id=dev,
                                         device_id_type=MESH).wait_recv()
        for cp in first + second:
            cp.wait_send()

    sems = pltpu.SemaphoreType.DMA((N_DEV - 1,))
    return pl.pallas_call(
        body, name=name,
        in_specs=[pl.BlockSpec(memory_space=pltpu.VMEM)],
        out_specs=pl.BlockSpec(memory_space=pltpu.VMEM),
        out_shape=jax.ShapeDtypeStruct(buf.shape, F32),
        scratch_shapes=[pltpu.VMEM((N_DEV, rb, LANES), F32), sems, sems, sems, sems],
        compiler_params=pltpu.CompilerParams(vmem_limit_bytes=VMEM_LIMIT_BYTES),
    )(buf)


def _adamw_math(w, g, m, v):
    m = ADAM_B1 * m + (1.0 - ADAM_B1) * g
    v = ADAM_B2 * v + (1.0 - ADAM_B2) * (g * g)
    m_hat = m / (1.0 - ADAM_B1 ** ADAM_STEP)
    v_hat = v / (1.0 - ADAM_B2 ** ADAM_STEP)
    delta = -ADAM_LR * (m_hat / (jnp.sqrt(v_hat) + ADAM_EPS) + ADAM_WD * w)
    return delta, m, v


def _adamw(name, parts, w, m, v, tr=128):
    P, R, W = parts.shape
    tr = min(tr, R)

    def body(p_ref, w_ref, m_ref, v_ref, g_out, d_out, m_out, v_out):
        g = p_ref[0].astype(F32)
        for j in range(1, P):
            g = g + p_ref[j].astype(F32)
        delta, m_new, v_new = _adamw_math(w_ref[...], g, m_ref[...], v_ref[...])
        g_out[...] = g
        d_out[...] = delta
        m_out[...] = m_new
        v_out[...] = v_new

    row = pl.BlockSpec((tr, W), lambda i: (i, 0))
    shape = jax.ShapeDtypeStruct((R, W), F32)
    return pl.pallas_call(
        body, name=name, grid=(R // tr,),
        in_specs=[pl.BlockSpec((P, tr, W), lambda i: (0, i, 0)), row, row, row],
        out_specs=[row, row, row, row],
        out_shape=[shape, shape, shape, shape],
        compiler_params=_params(("parallel",)),
    )(parts, w, m, v)


def _adamw_shard(name, parts, w, m, v, layer, earlier=None, tr=256):
    L, K, N = w.shape
    tr = min(tr, K)
    n_prev = 0 if earlier is None else 4

    def body(p_ref, w_ref, m_ref, v_ref, *rest):
        g_out, d_out, m_out, v_out = rest[n_prev:]
        g = p_ref[0].astype(F32)
        for j in range(1, N_DEV):
            g = g + p_ref[j].astype(F32)
        delta, m_new, v_new = _adamw_math(w_ref[...], g, m_ref[...], v_ref[...])
        g_out[...] = g
        d_out[...] = delta
        m_out[...] = m_new
        v_out[...] = v_new

    row = pl.BlockSpec((None, tr, N), lambda i: (layer, i, 0))
    shape = jax.ShapeDtypeStruct((L, K, N), F32)
    return pl.pallas_call(
        body, name=name, grid=(K // tr,),
        in_specs=[pl.BlockSpec((N_DEV, tr, N), lambda i: (0, i, 0)), row, row, row]
        + [pl.BlockSpec(memory_space=pl.ANY)] * n_prev,
        out_specs=[row, row, row, row],
        out_shape=[shape, shape, shape, shape],
        input_output_aliases={4 + j: j for j in range(n_prev)},
        compiler_params=_params(("parallel",)),
    )(parts, w, m, v, *(earlier or ()))


def _column_slots(full):
    K, N = full.shape
    return jnp.transpose(full.reshape(K, N_DEV, N // N_DEV), (1, 0, 2))


def _from_column_slots(slots):
    _, K, n = slots.shape
    return jnp.transpose(slots, (1, 0, 2)).reshape(K, N_DEV * n)


_SMALL =("mix_norm_g", "mlp_norm_g", "final_norm_g", "a_ln_g", "a_ln_b", "a_w_s", "a_b_s", "rel_bias")


def _pack_small(vals):
    pieces = []
    for n in _SMALL:
        flat = vals[n].reshape(-1)
        pad = (-flat.shape[0]) % (8 * LANES)
        pieces.append(jnp.pad(flat, (0, pad)).reshape(-1, LANES))
    rows = sum(p.shape[0] for p in pieces)
    tail = (-rows) % (8 * N_DEV)
    if tail:
        pieces.append(jnp.zeros((tail, LANES), F32))
    return jnp.concatenate(pieces, axis=0)


def _unpack_small(buf, like):
    out = {}
    r = 0
    for n in _SMALL:
        size = like[n].size
        nrows = -(-size // (8 * LANES)) * 8
        out[n] = buf[r:r + nrows].reshape(-1)[:size].reshape(like[n].shape)
        r += nrows
    return out


_STAGES = (("gate", ("a_w_in", "a_w_out"), 0),
           ("mlp0", ("w_up", "w_down"), 0),
           ("att", ("b_w_qkv", "b_w_out"), 0),
           ("mlp1", ("w_up", "w_down"), 1))


def kernel(x, mix_norm_g, mlp_norm_g, final_norm_g, a_w_in, a_ln_g, a_ln_b, a_w_s, a_b_s, a_w_out, b_w_qkv, b_w_out, rel_bias, w_up, w_down, loss_target, m_mix_norm_g, m_mlp_norm_g, m_final_norm_g, m_a_w_in, m_a_ln_g, m_a_ln_b, m_a_w_s, m_a_b_s, m_a_w_out, m_b_w_qkv, m_b_w_out, m_rel_bias, m_w_up, m_w_down, v_mix_norm_g, v_mlp_norm_g, v_final_norm_g, v_a_w_in, v_a_ln_g, v_a_ln_b, v_a_w_s, v_a_b_s, v_a_w_out, v_b_w_qkv, v_b_w_out, v_rel_bias, v_w_up, v_w_down):
    w = dict(mix_norm_g=mix_norm_g, mlp_norm_g=mlp_norm_g, final_norm_g=final_norm_g, a_w_in=a_w_in, a_ln_g=a_ln_g,
             a_ln_b=a_ln_b, a_w_s=a_w_s, a_b_s=a_b_s, a_w_out=a_w_out, b_w_qkv=b_w_qkv, b_w_out=b_w_out,
             rel_bias=rel_bias, w_up=w_up, w_down=w_down)
    m = dict(mix_norm_g=m_mix_norm_g, mlp_norm_g=m_mlp_norm_g, final_norm_g=m_final_norm_g, a_w_in=m_a_w_in,
             a_ln_g=m_a_ln_g, a_ln_b=m_a_ln_b, a_w_s=m_a_w_s, a_b_s=m_a_b_s, a_w_out=m_a_w_out, b_w_qkv=m_b_w_qkv,
             b_w_out=m_b_w_out, rel_bias=m_rel_bias, w_up=m_w_up, w_down=m_w_down)
    v = dict(mix_norm_g=v_mix_norm_g, mlp_norm_g=v_mlp_norm_g, final_norm_g=v_final_norm_g, a_w_in=v_a_w_in,
             a_ln_g=v_a_ln_g, a_ln_b=v_a_ln_b, a_w_s=v_a_w_s, a_b_s=v_a_b_s, a_w_out=v_a_w_out, b_w_qkv=v_b_w_qkv,
             b_w_out=v_b_w_out, rel_bias=v_rel_bias, w_up=v_w_up, w_down=v_w_down)

    stages = {s: (names, layer) for s, names, layer in _STAGES}
    order = [s for s, _, _ in _STAGES]

    def shards_of(stage):
        names, layer = stages[stage]
        return [_bf(w[n][layer]) for n in names]

    pending = {}
    pending[order[0]], first_token = _exchange_start("gather_" + order[0] + "_start", shards_of(order[0]), False)

    def get_weights(stage, dep):
        gathered = _exchange_wait("gather_" + stage + "_wait", pending.pop(stage), dep)
        nxt = order.index(stage) + 1
        token = None
        if nxt < len(order):
            shards, gathered = lax.optimization_barrier((shards_of(order[nxt]), gathered))
            pending[order[nxt]], token = _exchange_start("gather_" + order[nxt] + "_start", shards, False)
        return gathered, token

    sent = {}

    def put_grads(stage, slot_grads):
        sent[stage], token = _exchange_start("scatter_" + stage + "_start", slot_grads, True)
        return token

    loss_local, grad_x, small_g = _local_step(
        x[0], loss_target[0], mix_norm_g, mlp_norm_g, final_norm_g, a_ln_g, a_ln_b, a_w_s, a_b_s, rel_bias,
        get_weights, put_grads, first_token)

    results = {}
    for stage in reversed(order):
        names, layer = stages[stage]
        received = _exchange_wait("scatter_" + stage + "_wait", sent[stage], grad_x)
        for n, parts in zip(names, received):
            results[n] = _adamw_shard("adamw_%s_%s" % (stage, n), parts, w[n], m[n], v[n], layer, results.get(n))

    reduced = _all_reduce_small("reduce_small", _pack_small(small_g))
    small = [_unpack_small(b, w) for b in _adamw("adamw_small", reduced[None], _pack_small(w), _pack_small(m),
                                                 _pack_small(v), tr=reduced.shape[0])]

    outs = []
    for j in range(4):
        outs.extend(small[j][n] if n in _SMALL else results[n][j] for n in w)
    loss = lax.psum(loss_local, ("x", "y", "c"))
    return (loss, grad_x[None], *outs)


def _local_step(xs, tgt, mix_norm_g, mlp_norm_g, final_norm_g, a_ln_g, a_ln_b, a_w_s, a_b_s, rel_bias,
                get_weights, put_grads, first_token=None):
    D = xs.shape[-1]
    g_mix = [mix_norm_g[l][None, :] for l in range(2)]
    g_mlp = [mlp_norm_g[l][None, :] for l in range(2)]
    g_fin = final_norm_g[None, :]
    ln_g, ln_b = a_ln_g, a_ln_b
    causal = jnp.tril(jnp.ones((CHUNK, CHUNK), dtype=bool))
    wm = _bf(jnp.where(causal[None], a_w_s[0], 0.0))
    bs_full = jnp.repeat(a_b_s[0].T, D // GROUPS, axis=1)
    bias_tiles = _bias_tiles("att_bias", rel_bias, after=first_token)

    y0 = _rms_fwd("rms_mix0", xs, g_mix[0], after=bias_tiles)
    (win, wout), token = get_weights("gate", y0)
    wout = wout.reshape(-1, D)
    uvp = _mm_nn("gate_in", y0, win, tm=512, nc=win.shape[2], shards=True, after=token)
    z = _gate_fwd("gate_mid", uvp, ln_g, ln_b, wm, bs_full)
    h1, y1 = _mm_nn("gate_out", z, wout, tm=512, nc=512, epi="res", extra=xs, norm_g=g_mlp[0])
    (wup0, wdn0), token = get_weights("mlp0", h1)
    wdn0 = wdn0.reshape(-1, D)
    a0, f0 = _mm_nn("mlp0_up", y1, wup0, tm=512, nc=wup0.shape[2], epi="relu2", shards=True, after=token)
    h2, y2 = _mm_nn("mlp0_down", f0, wdn0, tm=512, nc=512, epi="res", extra=h1, norm_g=g_mix[1])
    (wqkv, wo), token = get_weights("att", h2)
    wqkv, wo = _from_column_slots(wqkv), _from_column_slots(wo)
    qkv = _mm_nn("att_qkv", y2, wqkv, tm=256, nc=512, after=token)
    o_att, lse = _att_merge("att_merge", *_att_fwd("att_fwd", qkv, bias_tiles))
    h3, y3 = _mm_nn("att_out", o_att, wo, tm=512, nc=512, epi="res", extra=h2, norm_g=g_mlp[1])
    (wup1, wdn1), _ = get_weights("mlp1", h3)
    wdn1 = wdn1.reshape(-1, D)
    a1, f1 = _mm_nn("mlp1_up", y3, wup1, tm=512, nc=wup1.shape[2], epi="relu2", shards=True)
    h4 = _mm_nn("mlp1_down", f1, wdn1, tm=512, nc=512, epi="res", extra=h3)
    dh, dg_fin, err2 = _final_loss("final_loss", h4, g_fin, tgt)
    loss_local = 0.5 * jnp.sum(err2) / D

    def mlp_bwd(tag, dh, h_in, y, a, f, wup_l, wdn_l, g_row, after):
        da = _mm_nt(tag + "_dact", dh, wdn_l, tm=512, nc=512, epi="mask2relu", extra=a, after=after)
        g_dn = _mm_tn(tag + "_dwdown", f, dh, t1=1024, tn=1024)
        g_up = _mm_tn(tag + "_dwup", y, da, t1=1024, tn=1024, slot_cols=wup_l.shape[2])
        dh_in, dg = _mm_nt_rms_bwd(tag + "_dy", [(da, wup_l, *_whole(wup_l))], h_in, g_row, dh, tm=512, nc=512,
                                   shards=True)
        return dh_in, dg, put_grads(tag, [g_up, g_dn.reshape(N_DEV, -1, D)])

    dh3, dg_mlp1, token = mlp_bwd("mlp1", dh, h3, y3, a1, f1, wup1, wdn1, g_mlp[1], None)

    d_o = _mm_nt("att_dout", dh3, wo, tm=512, nc=512, after=token)
    g_wo = _mm_tn("att_dwo", o_att, dh3, t1=512, tn=1024)
    dq, dk, dv, ds_sums = _att_bwd("att_bwd", qkv, o_att, lse, d_o, bias_tiles)
    part_w = N_DIL * ATT_WIDTH
    g_qkv = [_mm_tn("att_dwqkv%d" % p, y2, t, t1=1024, tn=part_w, tm=1024) for p, t in enumerate((dq, dk, dv))]
    dh2, dg_mix1 = _mm_nt_rms_bwd("att_dy", [(t, wqkv, (D, part_w), (0, p)) for p, t in enumerate((dq, dk, dv))],
                                  h2, g_mix[1], dh3, tm=256, nc=512)
    token = put_grads("att", [_column_slots(jnp.concatenate(g_qkv, axis=1)), _column_slots(g_wo)])

    dh1, dg_mlp0, token = mlp_bwd("mlp0", dh2, h1, y1, a0, f0, wup0, wdn0, g_mlp[0], token)

    dz = _mm_nt("gate_dz", dh1, wout, tm=512, nc=512, after=token)
    g_wout = _mm_tn("gate_dwout", z, dh1, t1=1024, tn=1024)
    duvp, d_wm, d_mixed, d_lng, d_lnb = _gate_bwd("gate_dmid", uvp, dz, ln_g, ln_b, wm, bs_full)
    g_win = _mm_tn("gate_dwin", y0, duvp, t1=1024, tn=1024, slot_cols=win.shape[2])
    token = put_grads("gate", [g_win, g_wout.reshape(N_DEV, -1, D)])
    grad_x, dg_mix0 = _mm_nt_rms_bwd("gate_dy", [(duvp, win, *_whole(win))], xs, g_mix[0], dh1, tm=512, nc=512,
                                     after=token, shards=True)

    small_g = dict(
        mix_norm_g=jnp.concatenate([dg_mix0, dg_mix1], axis=0),
        mlp_norm_g=jnp.concatenate([dg_mlp0, dg_mlp1], axis=0),
        final_norm_g=dg_fin[0], a_ln_g=d_lng, a_ln_b=d_lnb, a_w_s=d_wm[None],
        a_b_s=jnp.sum(d_mixed.reshape(CHUNK, GROUPS, D // GROUPS), axis=2).T[None],
        rel_bias=_bias_grad("att_dbias", ds_sums))
    return loss_local, grad_x, small_g
```

```python
import functools
import math

import jax
import jax.numpy as jnp
from jax import lax
from jax.experimental import pallas as pl
from jax.experimental.pallas import tpu as pltpu

F32 = jnp.float32
BF16 = jnp.bfloat16
MESH = pl.DeviceIdType.MESH

N_DEV = 8
EPS = 1e-6
NEG_INF = -1e30
CHUNK = 128
GROUPS = 8
HEAD_DIM = 64
ATT_HEADS = 8
ATT_WIDTH = ATT_HEADS * HEAD_DIM
DILATIONS = (1, 4, 16)
N_DIL = len(DILATIONS)
N_BUCKETS = 32
MAX_EXACT = N_BUCKETS // 2
REL_MAX_DISTANCE = 2048
ATT_ROWS = 2048
ATT_SCALE = HEAD_DIM ** -0.5
LANES = 128

ADAM_LR = 0.001
ADAM_B1 = 0.9
ADAM_B2 = 0.999
ADAM_EPS = 1e-08
ADAM_WD = 0.01
ADAM_STEP = 10

VMEM_LIMIT_BYTES = 56 * 1024 * 1024


def _params(semantics=None):
    return pltpu.CompilerParams(dimension_semantics=semantics, vmem_limit_bytes=VMEM_LIMIT_BYTES)


def _bf(v):
    return v.astype(BF16)


def _dot(a, b, dims):
    return lax.dot_general(a, b, (dims, ((), ())), preferred_element_type=F32)


NN = ((1,), (0,))
NT = ((1,), (1,))
TN = ((0,), (0,))


def _after_operand(after):
    if after is None:
        return [], []
    return [after], [pl.BlockSpec(memory_space=pl.ANY)]


def _rms_fwd(name, x, g, tm=512, after=None):
    S, D = x.shape
    after_args, after_specs = _after_operand(after)

    def body(x_ref, g_ref, *rest):
        y_ref = rest[-1]
        xv = x_ref[...]
        r = lax.rsqrt(jnp.mean(xv * xv, axis=-1, keepdims=True) + EPS)
        y_ref[...] = _bf(xv * r * g_ref[...])

    return pl.pallas_call(
        body, name=name, grid=(S // tm,),
        in_specs=[pl.BlockSpec((tm, D), lambda i: (i, 0)), pl.BlockSpec((1, D), lambda i: (0, 0))] + after_specs,
        out_specs=pl.BlockSpec((tm, D), lambda i: (i, 0)),
        out_shape=jax.ShapeDtypeStruct((S, D), BF16),
        compiler_params=_params(("parallel",)),
    )(x, g, *after_args)


def _final_loss(name, h, g, target, tm=512):
    S, D = h.shape

    def body(h_ref, g_ref, t_ref, dh_ref, dg_ref, l_ref):
        i = pl.program_id(0)
        xv = h_ref[...]
        r = lax.rsqrt(jnp.mean(xv * xv, axis=-1, keepdims=True) + EPS)
        xh = xv * r
        gv = g_ref[...]
        e = xh * gv - t_ref[...]
        dout = e / D
        dyg = dout * gv
        c = jnp.mean(dyg * xh, axis=-1, keepdims=True)
        dh_ref[...] = r * (dyg - xh * c)
        dg_part = jnp.sum(dout * xh, axis=0, keepdims=True)
        l_part = jnp.sum(e * e, axis=0, keepdims=True)

        @pl.when(i == 0)
        def _():
            dg_ref[...] = dg_part
            l_ref[...] = l_part

        @pl.when(i > 0)
        def _():
            dg_ref[...] += dg_part
            l_ref[...] += l_part

    row = pl.BlockSpec((tm, D), lambda i: (i, 0))
    vec = pl.BlockSpec((1, D), lambda i: (0, 0))
    return pl.pallas_call(
        body, name=name, grid=(S // tm,),
        in_specs=[row, vec, row],
        out_specs=[row, vec, vec],
        out_shape=[jax.ShapeDtypeStruct((S, D), F32), jax.ShapeDtypeStruct((1, D), F32),
                   jax.ShapeDtypeStruct((1, D), F32)],
        compiler_params=_params(("arbitrary",)),
    )(h, g, target)


def _chunk_product(a_vals, w_refs, j, nc, nt, shards):
    cols = slice(j * nc, (j + 1) * nc)
    acc = None
    for a_v, w_ref in zip(a_vals, w_refs):
        if not shards:
            terms = [_dot(a_v, w_ref[cols, :], NT) if nt else _dot(a_v, w_ref[:, cols], NN)]
        elif nt:
            nl = w_ref.shape[2]
            terms = [_dot(a_v[:, k * nl:(k + 1) * nl], w_ref[k, cols, :], NT) for k in range(N_DEV)]
        else:
            terms = [_dot(a_v, w_ref[j], NN)]
        for t in terms:
            acc = t if acc is None else acc + t
    return cols, acc


def _mm_rows(name, pairs, n_out, *, nt, tm, nc, epi="plain", extra=None, out_dtype=F32, after=None, shards=False,
             norm_g=None):
    M = pairs[0][0].shape[0]
    np_ = len(pairs)
    after_args, after_specs = _after_operand(after)

    def body(*refs):
        a_refs = refs[:np_]
        w_refs = refs[np_:2 * np_]
        pos = 2 * np_
        e_ref = None
        if extra is not None:
            e_ref = refs[pos]
            pos += 1
        if norm_g is not None:
            g_ref = refs[pos]
            pos += 1
        pos += len(after_args)
        outs = refs[pos:]
        a_vals = [_bf(a[...]) for a in a_refs]
        for j in range(n_out // nc):
            cols, acc = _chunk_product(a_vals, w_refs, j, nc, nt, shards)
            if epi == "plain":
                outs[0][:, cols] = acc.astype(out_dtype)
            elif epi == "res":
                outs[0][:, cols] = e_ref[:, cols] + acc
            elif epi == "relu2":
                outs[0][:, cols] = _bf(acc)
                rl = jnp.maximum(acc, 0.0)
                outs[1][:, cols] = _bf(rl * rl)
            elif epi == "mask2relu":
                outs[0][:, cols] = _bf(acc * (2.0 * jnp.maximum(e_ref[:, cols].astype(F32), 0.0)))
        if norm_g is not None:
            hv = outs[0][...]
            r = lax.rsqrt(jnp.mean(hv * hv, axis=-1, keepdims=True) + EPS)
            outs[1][...] = _bf(hv * r * g_ref[...])

    in_specs = [pl.BlockSpec((tm, a.shape[1]), lambda i: (i, 0)) for a, _, _, _ in pairs]
    for _, _, wshape, widx in pairs:
        in_specs.append(pl.BlockSpec(wshape, functools.partial(lambda i, widx: widx, widx=widx)))
    args = [a for a, _, _, _ in pairs] + [w for _, w, _, _ in pairs]
    if extra is not None:
        in_specs.append(pl.BlockSpec((tm, n_out), lambda i: (i, 0)))
        args.append(extra)
    if norm_g is not None:
        in_specs.append(pl.BlockSpec((1, n_out), lambda i: (0, 0)))
        args.append(norm_g)
    in_specs += after_specs
    args += after_args
    row_out = pl.BlockSpec((tm, n_out), lambda i: (i, 0))
    if epi == "relu2":
        out_specs = [row_out, row_out]
        out_shape = [jax.ShapeDtypeStruct((M, n_out), BF16), jax.ShapeDtypeStruct((M, n_out), BF16)]
    elif norm_g is not None:
        out_specs = [row_out, row_out]
        out_shape = [jax.ShapeDtypeStruct((M, n_out), F32), jax.ShapeDtypeStruct((M, n_out), BF16)]
    else:
        dt = BF16 if epi == "mask2relu" else (F32 if epi == "res" else out_dtype)
        out_specs = row_out
        out_shape = jax.ShapeDtypeStruct((M, n_out), dt)
    return pl.pallas_call(
        body, name=name, grid=(M // tm,), in_specs=in_specs, out_specs=out_specs, out_shape=out_shape,
        compiler_params=_params(("parallel",)),
    )(*args)


def _whole(w):
    return w.shape, (0,) * w.ndim


def _mm_nn(name, a, w, **kw):
    n_out = w.shape[0] * w.shape[2] if w.ndim == 3 else w.shape[1]
    return _mm_rows(name, [(a, w, *_whole(w))], n_out, nt=False, **kw)


def _mm_nt(name, a, w, **kw):
    return _mm_rows(name, [(a, w, *_whole(w))], w.shape[0], nt=True, **kw)


def _mm_nt_rms_bwd(name, pairs, x, g, dres, *, tm, nc, after=None, shards=False):
    M, D = x.shape
    np_ = len(pairs)
    after_args, after_specs = _after_operand(after)

    def body(*refs):
        a_refs = refs[:np_]
        w_refs = refs[np_:2 * np_]
        x_ref, g_ref, r_ref = refs[2 * np_:2 * np_ + 3]
        dx_ref, dg_ref, dy_sc = refs[-3:]
        i = pl.program_id(0)
        a_vals = [_bf(a[...]) for a in a_refs]
        for j in range(D // nc):
            cols, acc = _chunk_product(a_vals, w_refs, j, nc, True, shards)
            dy_sc[:, cols] = acc
        xv = x_ref[...]
        r = lax.rsqrt(jnp.mean(xv * xv, axis=-1, keepdims=True) + EPS)
        xh = xv * r
        dy_v = dy_sc[...]
        dyg = dy_v * g_ref[...]
        c = jnp.mean(dyg * xh, axis=-1, keepdims=True)
        dx_ref[...] = r_ref[...] + r * (dyg - xh * c)
        part = jnp.sum(dy_v * xh, axis=0, keepdims=True)

        @pl.when(i == 0)
        def _():
            dg_ref[...] = part

        @pl.when(i > 0)
        def _():
            dg_ref[...] += part

    row = pl.BlockSpec((tm, D), lambda i: (i, 0))
    vec = pl.BlockSpec((1, D), lambda i: (0, 0))
    in_specs = [pl.BlockSpec((tm, a.shape[1]), lambda i: (i, 0)) for a, _, _, _ in pairs]
    for _, _, wshape, widx in pairs:
        in_specs.append(pl.BlockSpec(wshape, functools.partial(lambda i, widx: widx, widx=widx)))
    args = [a for a, _, _, _ in pairs] + [w for _, w, _, _ in pairs]
    return pl.pallas_call(
        body, name=name, grid=(M // tm,),
        in_specs=in_specs + [row, vec, row] + after_specs,
        out_specs=[row, vec],
        out_shape=[jax.ShapeDtypeStruct((M, D), F32), jax.ShapeDtypeStruct((1, D), F32)],
        scratch_shapes=[pltpu.VMEM((tm, D), F32)],
        compiler_params=_params(("arbitrary",)),
    )(*args, x, g, dres, *after_args)


def _mm_tn(name, a, b, *, t1, tn, tm=2048, slot_cols=None):
    M, K1 = a.shape
    N = b.shape[1]
    nm = M // tm

    def body(a_ref, b_ref, o_ref, acc_ref):
        m = pl.program_id(2)
        t = _dot(_bf(a_ref[...]), _bf(b_ref[...]), TN)

        @pl.when(m == 0)
        def _():
            acc_ref[...] = t

        @pl.when(m > 0)
        def _():
            acc_ref[...] += t

        @pl.when(m == nm - 1)
        def _():
            if slot_cols is None:
                o_ref[...] = _bf(acc_ref[...])
            else:
                for k in range(tn // slot_cols):
                    o_ref[k] = _bf(acc_ref[:, k * slot_cols:(k + 1) * slot_cols])

    if slot_cols is not None:
        out_spec = pl.BlockSpec((tn // slot_cols, t1, slot_cols), lambda i, j, m: (j, i, 0))
        out_shape = jax.ShapeDtypeStruct((N // slot_cols, K1, slot_cols), BF16)
    else:
        out_spec = pl.BlockSpec((t1, tn), lambda i, j, m: (i, j))
        out_shape = jax.ShapeDtypeStruct((K1, N), BF16)
    return pl.pallas_call(
        body, name=name, grid=(K1 // t1, N // tn, nm),
        in_specs=[pl.BlockSpec((tm, t1), lambda i, j, m: (m, i)), pl.BlockSpec((tm, tn), lambda i, j, m: (m, j))],
        out_specs=out_spec, out_shape=out_shape,
        scratch_shapes=[pltpu.VMEM((t1, tn), F32)],
        compiler_params=_params(("parallel", "parallel", "arbitrary")),
    )(a, b)


_INV_SQRT2 = 1.0 / math.sqrt(2.0)
_INV_SQRT2PI = 1.0 / math.sqrt(2.0 * math.pi)


def _gelu(x):
    return 0.5 * x * (1.0 + lax.erf(x * _INV_SQRT2))


def _gelu_and_grad(x):
    cdf = 0.5 * (1.0 + lax.erf(x * _INV_SQRT2))
    return x * cdf, cdf + x * (_INV_SQRT2PI * jnp.exp(-0.5 * x * x))


def _layer_norm_parts(v):
    mu = jnp.mean(v, axis=-1, keepdims=True)
    xc = v - mu
    rs = lax.rsqrt(jnp.mean(xc * xc, axis=-1, keepdims=True) + EPS)
    return xc * rs, rs


def _gate_fwd(name, uvp, ln_g, ln_b, wm, bs_full, tr=512):
    S, W2 = uvp.shape
    W = W2 // 2
    gd = W // GROUPS

    def body(u_ref, v_ref, lg_ref, lb_ref, wm_ref, bs_ref, z_ref):
        vh, _ = _layer_norm_parts(_gelu(v_ref[...]))
        vn = _bf(vh * lg_ref[...] + lb_ref[...])
        for ci in range(tr // CHUNK):
            rows = slice(ci * CHUNK, (ci + 1) * CHUNK)
            for g in range(GROUPS):
                cols = slice(g * gd, (g + 1) * gd)
                mixed = _dot(wm_ref[g], vn[rows, cols], NN) + bs_ref[:, cols]
                z_ref[rows, cols] = _bf(_gelu(u_ref[rows, cols]) * mixed)

    vec = pl.BlockSpec((1, W), lambda i: (0, 0))
    return pl.pallas_call(
        body, name=name, grid=(S // tr,),
        in_specs=[pl.BlockSpec((tr, W), lambda i: (i, 0)), pl.BlockSpec((tr, W), lambda i: (i, 1)), vec, vec,
                  pl.BlockSpec((GROUPS, CHUNK, CHUNK), lambda i: (0, 0, 0)),
                  pl.BlockSpec((CHUNK, W), lambda i: (0, 0))],
        out_specs=pl.BlockSpec((tr, W), lambda i: (i, 0)),
        out_shape=jax.ShapeDtypeStruct((S, W), BF16),
        compiler_params=_params(("parallel",)),
    )(uvp, uvp, ln_g, ln_b, wm, bs_full)


def _gate_bwd(name, uvp, dz, ln_g, ln_b, wm, bs_full, tr=256):
    S, W2 = uvp.shape
    W = W2 // 2
    gd = W // GROUPS
    n_steps = S // tr

    def body(u_ref, v_ref, dz_ref, lg_ref, lb_ref, wm_ref, bs_ref, duv_ref, dwm_ref, dmx_ref, dlg_ref, dlb_ref,
             dvn_ref):
        i = pl.program_id(0)
        v, dv_dvp = _gelu_and_grad(v_ref[...])
        vh, rs = _layer_norm_parts(v)
        lg = lg_ref[...]
        vn = _bf(vh * lg + lb_ref[...])

        @pl.when(i == 0)
        def _():
            dwm_ref[...] = jnp.zeros_like(dwm_ref)
            dmx_ref[...] = jnp.zeros_like(dmx_ref)
            dlg_ref[...] = jnp.zeros_like(dlg_ref)
            dlb_ref[...] = jnp.zeros_like(dlb_ref)

        for ci in range(tr // CHUNK):
            rows = slice(ci * CHUNK, (ci + 1) * CHUNK)
            for g in range(GROUPS):
                cols = slice(g * gd, (g + 1) * gd)
                u, du_dup = _gelu_and_grad(u_ref[rows, cols])
                dz_v = dz_ref[rows, cols]
                dmixed = dz_v * u
                dmx_ref[:, cols] += dmixed
                dmixed_b = _bf(dmixed)
                mixed = _dot(wm_ref[g], vn[rows, cols], NN) + bs_ref[:, cols]
                duv_ref[rows, cols] = _bf(dz_v * mixed * du_dup)
                dwm_ref[g] += _dot(dmixed_b, vn[rows, cols], NT)
                dvn_ref[rows, cols] = _dot(wm_ref[g], dmixed_b, TN)
        dvn = dvn_ref[...]
        dlg_ref[...] += jnp.sum(dvn * vh, axis=0, keepdims=True)
        dlb_ref[...] += jnp.sum(dvn, axis=0, keepdims=True)
        dvh = dvn * lg
        dv = rs * (dvh - jnp.mean(dvh, axis=-1, keepdims=True) - vh * jnp.mean(dvh * vh, axis=-1, keepdims=True))
        duv_ref[:, W:] = _bf(dv * dv_dvp)

        @pl.when(i == n_steps - 1)
        def _():
            t_idx = lax.broadcasted_iota(jnp.int32, (CHUNK, CHUNK), 0)
            s_idx = lax.broadcasted_iota(jnp.int32, (CHUNK, CHUNK), 1)
            keep = (s_idx <= t_idx).astype(F32)
            for g in range(GROUPS):
                dwm_ref[g] = dwm_ref[g] * keep

    vec = pl.BlockSpec((1, W), lambda i: (0, 0))
    row = pl.BlockSpec((tr, W), lambda i: (i, 0))
    return pl.pallas_call(
        body, name=name, grid=(n_steps,),
        in_specs=[row, pl.BlockSpec((tr, W), lambda i: (i, 1)), row, vec, vec,
                  pl.BlockSpec((GROUPS, CHUNK, CHUNK), lambda i: (0, 0, 0)),
                  pl.BlockSpec((CHUNK, W), lambda i: (0, 0))],
        out_specs=[pl.BlockSpec((tr, W2), lambda i: (i, 0)),
                   pl.BlockSpec((GROUPS, CHUNK, CHUNK), lambda i: (0, 0, 0)),
                   pl.BlockSpec((CHUNK, W), lambda i: (0, 0)), vec, vec],
        out_shape=[jax.ShapeDtypeStruct((S, W2), BF16), jax.ShapeDtypeStruct((GROUPS, CHUNK, CHUNK), F32),
                   jax.ShapeDtypeStruct((CHUNK, W), F32), jax.ShapeDtypeStruct((1, W), F32),
                   jax.ShapeDtypeStruct((1, W), F32)],
        scratch_shapes=[pltpu.VMEM((tr, W), F32)],
        compiler_params=_params(("arbitrary",)),
    )(uvp, uvp, dz, ln_g, ln_b, wm, bs_full)


def _t5_bucket(distance):
    small = distance < MAX_EXACT
    nf = jnp.maximum(distance, 1).astype(F32)
    large = MAX_EXACT + (jnp.log(nf / MAX_EXACT) / math.log(REL_MAX_DISTANCE / MAX_EXACT)
                         * (N_BUCKETS - MAX_EXACT)).astype(jnp.int32)
    large = jnp.minimum(large, N_BUCKETS - 1)
    return jnp.where(small, distance, large)


TILE_ELEMS = 2 * CHUNK * CHUNK


def _band_buckets():
    rel = CHUNK + jnp.arange(CHUNK)[None, :] - jnp.arange(2 * CHUNK)[:, None]
    band = (rel >= 0) & (rel <= CHUNK)
    buckets = [_t5_bucket(jnp.clip(rel, 0, CHUNK) * d) for d in DILATIONS]
    return jnp.stack(buckets), band


def _bucket_onehot():
    buckets, _ = _band_buckets()
    return (buckets.reshape(N_DIL, 1, TILE_ELEMS) == jnp.arange(N_BUCKETS)[None, :, None]).astype(F32)


def _bias_tiles(name, rel_bias, after=None):
    _, band = _band_buckets()
    own = band & (jnp.arange(2 * CHUNK) >= CHUNK)[:, None]
    masks = jnp.stack([own, band]).reshape(2, TILE_ELEMS).astype(F32)
    tables = jnp.transpose(rel_bias.reshape(N_BUCKETS, N_DIL, ATT_HEADS), (1, 2, 0))
    after_args, after_specs = _after_operand(after)

    def body(t_ref, oh_ref, m_ref, *rest):
        out_ref = rest[-1]
        for g in range(N_DIL):
            bias = lax.dot_general(t_ref[g], oh_ref[g], (NN, ((), ())), precision=lax.Precision.HIGHEST,
                                   preferred_element_type=F32)
            for f in range(2):
                out_ref[g, f] = jnp.where(m_ref[f:f + 1, :] > 0.5, bias, NEG_INF)

    whole = pl.BlockSpec(memory_space=pltpu.VMEM)
    out = pl.pallas_call(
        body, name=name, out_shape=jax.ShapeDtypeStruct((N_DIL, 2, ATT_HEADS, TILE_ELEMS), F32),
        in_specs=[whole, whole, whole] + after_specs, out_specs=whole,
        compiler_params=_params(),
    )(tables, _bucket_onehot(), masks, *after_args)
    out = out.reshape(N_DIL, 2, ATT_HEADS // 2, 2, 2 * CHUNK, CHUNK)
    return jnp.transpose(out, (0, 1, 2, 4, 3, 5)).reshape(N_DIL, 2, ATT_HEADS // 2, 2 * CHUNK, 2 * CHUNK)


def _att_specs(order):
    def column(part, ids):
        hp, g, _ = order(*ids)
        return part * 3 * 4 + g * 4 + hp

    def window(part):
        def index(*ids):
            c = order(*ids)[2]
            return pl.multiple_of(jnp.maximum(c - 1, 0) * ATT_ROWS, ATT_ROWS), column(part, ids) * LANES
        return pl.BlockSpec((pl.Element(2 * ATT_ROWS), pl.Element(LANES)), index)

    return [pl.BlockSpec((ATT_ROWS, LANES), lambda *ids: (order(*ids)[2], column(0, ids))), window(1), window(2)]


def _window_base(c):
    return jnp.where(c == 0, 0, ATT_ROWS)


def _rows(start, d):
    if d == 1:
        return pl.ds(pl.multiple_of(start, CHUNK), CHUNK)
    return pl.ds(start, CHUNK, stride=d)


def _att_tile_offsets(t, d):
    n = t // d
    r = t % d
    return n * (CHUNK * d) + r, n


def _head_pair_columns(x_t):
    zeros = jnp.zeros((HEAD_DIM, CHUNK), x_t.dtype)
    return jnp.concatenate([jnp.concatenate([x_t[:HEAD_DIM], zeros], axis=0),
                            jnp.concatenate([zeros, x_t[HEAD_DIM:]], axis=0)], axis=1)


def _head_pair_rows(y):
    return jnp.concatenate([y[:HEAD_DIM, :CHUNK], y[HEAD_DIM:, CHUNK:]], axis=0)


def _att_fwd(name, qkv, bias_tiles):
    S = qkv.shape[0]
    n_chunks = S // ATT_ROWS
    tiles = ATT_ROWS // CHUNK

    def body(q_ref, kk, vv, b_ref, o_ref, l_ref):
        c = pl.program_id(1)
        g = pl.program_id(2)
        base = _window_base(c)

        for gi, d in enumerate(DILATIONS):
            @pl.when(g == gi)
            def _(d=d):
                span = CHUNK * d

                def tile(t, carry):
                    q0, n = _att_tile_offsets(t, d)
                    first = (c == 0) & (n == 0)
                    rows = _rows(q0, d)
                    cur = _rows(base + q0, d)
                    prev = _rows(jnp.where(first, q0, base + q0 - span), d)
                    inner = jnp.where(first, 0, 1)
                    qq = _head_pair_columns(_bf(q_ref[rows, :] * ATT_SCALE).T)
                    s_p = _dot(_bf(kk[prev, :]), qq, NN) + b_ref[inner, 0:CHUNK, :]
                    s_c = _dot(_bf(kk[cur, :]), qq, NN) + b_ref[inner, CHUNK:2 * CHUNK, :]
                    m = jnp.maximum(jnp.max(s_p, axis=0, keepdims=True), jnp.max(s_c, axis=0, keepdims=True))
                    p_p = jnp.exp(s_p - m)
                    p_c = jnp.exp(s_c - m)
                    l = jnp.sum(p_p, axis=0, keepdims=True) + jnp.sum(p_c, axis=0, keepdims=True)
                    o2 = (_dot(_bf(vv[prev, :]).T, _bf(p_p), NN)
                          + _dot(_bf(vv[cur, :]).T, _bf(p_c), NN)) * (1.0 / l)
                    lse = m + jnp.log(l)
                    l_t = jnp.concatenate([jnp.broadcast_to(lse[:, :CHUNK], (HEAD_DIM, CHUNK)),
                                           jnp.broadcast_to(lse[:, CHUNK:], (HEAD_DIM, CHUNK))], axis=0)
                    o_ref[rows, :] = _head_pair_rows(o2).T
                    l_ref[rows, :] = l_t.T
                    return carry

                lax.fori_loop(0, tiles, tile, 0, unroll=8)

    order = lambda hp, c, g: (hp, g, c)
    out_spec = pl.BlockSpec((None, ATT_ROWS, LANES), lambda hp, c, g: (g, c, hp))
    shape = jax.ShapeDtypeStruct((N_DIL, S, ATT_WIDTH), F32)
    return pl.pallas_call(
        body, name=name, grid=(ATT_HEADS // 2, n_chunks, N_DIL),
        in_specs=_att_specs(order) + [
            pl.BlockSpec((None, 2, None, 2 * CHUNK, 2 * CHUNK), lambda hp, c, g: (g, 0, hp, 0, 0))],
        out_specs=[out_spec, out_spec],
        out_shape=[shape, shape],
        compiler_params=_params(("parallel", "parallel", "parallel")),
    )(qkv, qkv, qkv, bias_tiles)


def _att_merge(name, o_g, l_g, tm=512):
    _, S, W = o_g.shape

    def body(o_ref, l_ref, out_ref, lse_ref):
        ls = [l_ref[g] for g in range(N_DIL)]
        mx = functools.reduce(jnp.maximum, ls)
        ws = [jnp.exp(l - mx) for l in ls]
        tot = functools.reduce(lambda a, b: a + b, ws)
        acc = ws[0] * o_ref[0]
        for g in range(1, N_DIL):
            acc = acc + ws[g] * o_ref[g]
        out_ref[...] = acc / tot
        lse_ref[...] = mx + jnp.log(tot)

    blk = pl.BlockSpec((N_DIL, tm, W), lambda i: (0, i, 0))
    row = pl.BlockSpec((tm, W), lambda i: (i, 0))
    shape = jax.ShapeDtypeStruct((S, W), F32)
    return pl.pallas_call(
        body, name=name, grid=(S // tm,), in_specs=[blk, blk], out_specs=[row, row], out_shape=[shape, shape],
        compiler_params=_params(("parallel",)),
    )(o_g, l_g)


def _att_bwd(name, qkv, o, lse, d_o, bias_tiles):
    S = qkv.shape[0]
    n_chunks = S // ATT_ROWS
    tiles = ATT_ROWS // CHUNK

    def body(q_ref, kk, vv, o_ref, l_ref, do_ref, b_ref, dq_ref, dk_ref, dv_ref, ds_ref):
        g = pl.program_id(1)
        c = pl.program_id(2)

        @pl.when(c == 0)
        def _():
            dk_ref[...] = jnp.zeros_like(dk_ref)
            dv_ref[...] = jnp.zeros_like(dv_ref)
            ds_ref[...] = jnp.zeros_like(ds_ref)

        base = _window_base(c)
        first_row = c * ATT_ROWS
        head0 = lax.broadcasted_iota(jnp.int32, (CHUNK, LANES), 1) < HEAD_DIM

        def head_pair_stack(x):
            zero = jnp.zeros_like(x)
            return jnp.concatenate([jnp.where(head0, x, zero), jnp.where(head0, zero, x)], axis=0)

        for gi, d in enumerate(DILATIONS):
            @pl.when(g == gi)
            def _(d=d):
                span = CHUNK * d

                def tile(t, carry):
                    q0, n = _att_tile_offsets(t, d)
                    first = (c == 0) & (n == 0)
                    rows = _rows(q0, d)
                    cur = _rows(base + q0, d)
                    prev = _rows(jnp.where(first, q0, base + q0 - span), d)
                    inner = jnp.where(first, 0, 1)
                    g_cur = _rows(first_row + q0, d)
                    g_prev = _rows(jnp.where(first, q0, first_row + q0 - span), d)
                    q2 = _bf(q_ref[rows, :] * ATT_SCALE)
                    q_t = q2.T
                    k2 = _bf(jnp.concatenate([kk[prev, :], kk[cur, :]], axis=0))
                    k_t = k2.T
                    v2 = _bf(jnp.concatenate([vv[prev, :], vv[cur, :]], axis=0))
                    do2 = do_ref[rows, :]
                    do_b = _bf(do2)
                    do_t = do_b.T
                    lse_t = l_ref[rows, :].T
                    dd_t = (do2 * o_ref[rows, :]).T
                    lse = jnp.concatenate([lse_t[0:1], lse_t[HEAD_DIM:HEAD_DIM + 1]], axis=1)
                    delta = jnp.concatenate([jnp.sum(dd_t[:HEAD_DIM], axis=0, keepdims=True),
                                             jnp.sum(dd_t[HEAD_DIM:], axis=0, keepdims=True)], axis=1)
                    s = _dot(k2, _head_pair_columns(q_t), NN) + b_ref[inner]
                    p = jnp.exp(s - lse)
                    ds = p * (_dot(v2, _head_pair_columns(do_t), NN) - delta)
                    ds_ref[...] += ds
                    ds_b = _bf(ds)
                    dq_t = _head_pair_rows(_dot(k_t, ds_b, NN))
                    dk2 = _dot(ds_b, head_pair_stack(q2), NN)
                    dv2 = _dot(_bf(p), head_pair_stack(do_b), NN)
                    dq_ref[rows, :] = (dq_t * ATT_SCALE).T
                    dk_ref[g_prev, :] += dk2[0:CHUNK]
                    dk_ref[g_cur, :] += dk2[CHUNK:2 * CHUNK]
                    dv_ref[g_prev, :] += dv2[0:CHUNK]
                    dv_ref[g_cur, :] += dv2[CHUNK:2 * CHUNK]
                    return carry

                lax.fori_loop(0, tiles, tile, 0, unroll=4)

    order = lambda hp, g, c: (hp, g, c)
    chunk = pl.BlockSpec((ATT_ROWS, LANES), lambda hp, g, c: (c, hp))
    slab = pl.BlockSpec((S, LANES), lambda hp, g, c: (0, g * 4 + hp))
    width = N_DIL * ATT_WIDTH
    dq, dk, dv, ds_sums = pl.pallas_call(
        body, name=name, grid=(ATT_HEADS // 2, N_DIL, n_chunks),
        in_specs=_att_specs(order) + [chunk, chunk, chunk,
                                      pl.BlockSpec((None, 2, None, 2 * CHUNK, 2 * CHUNK),
                                                   lambda hp, g, c: (g, 0, hp, 0, 0))],
        out_specs=[pl.BlockSpec((ATT_ROWS, LANES), lambda hp, g, c: (c, g * 4 + hp)), slab, slab,
                   pl.BlockSpec((None, None, 2 * CHUNK, 2 * CHUNK), lambda hp, g, c: (g, hp, 0, 0))],
        out_shape=[jax.ShapeDtypeStruct((S, width), F32), jax.ShapeDtypeStruct((S, width), F32),
                   jax.ShapeDtypeStruct((S, width), F32),
                   jax.ShapeDtypeStruct((N_DIL, ATT_HEADS // 2, 2 * CHUNK, 2 * CHUNK), F32)],
        compiler_params=_params(("parallel", "parallel", "arbitrary")),
    )(qkv, qkv, qkv, o, lse, d_o, bias_tiles)
    ds_sums = ds_sums.reshape(N_DIL, ATT_HEADS // 2, 2 * CHUNK, 2, CHUNK)
    ds_sums = jnp.transpose(ds_sums, (0, 1, 3, 2, 4)).reshape(N_DIL, ATT_HEADS, 2 * CHUNK, CHUNK)
    return dq, dk, dv, ds_sums


def _bias_grad(name, ds_sums):
    flat = ds_sums.reshape(N_DIL, ATT_HEADS, TILE_ELEMS)

    def body(oh_ref, ds_ref, out_ref):
        for g in range(N_DIL):
            out_ref[g] = lax.dot_general(oh_ref[g], ds_ref[g], (NT, ((), ())), precision=lax.Precision.HIGHEST,
                                         preferred_element_type=F32)

    out = pl.pallas_call(
        body, name=name, out_shape=jax.ShapeDtypeStruct((N_DIL, N_BUCKETS, ATT_HEADS), F32),
        compiler_params=_params(),
    )(_bucket_onehot(), flat)
    return jnp.transpose(out, (1, 0, 2)).reshape(N_BUCKETS, N_DIL * ATT_HEADS)


def _peers():
    x, y, c = lax.axis_index("x"), lax.axis_index("y"), lax.axis_index("c")
    me = 4 * x + 2 * y + c
    others = [(x, y, 1 - c), (1 - x, y, c), (x, 1 - y, c), (1 - x, 1 - y, c),
              (1 - x, y, 1 - c), (x, 1 - y, 1 - c), (1 - x, 1 - y, 1 - c)]
    return me, others


def _slot(dev):
    return 4 * dev[0] + 2 * dev[1] + dev[2]


_HBM =pl.BlockSpec(memory_space=pltpu.HBM)
_SEM = pl.BlockSpec(memory_space=pltpu.SEMAPHORE)
_EFFECT = pltpu.SideEffectType.DATAFLOW_SIDE_EFFECTING


def _my_slot():
    return 4 * lax.axis_index("x") + 2 * lax.axis_index("y") + lax.axis_index("c")


def _exchange_copy(src_ref, land_ref, send_sems, recv_sems, k, dev, me, scatter, arriving):
    src = src_ref.at[me if arriving else _slot(dev)] if scatter else src_ref
    dst = land_ref.at[_slot(dev) if arriving else me]
    return pltpu.make_async_remote_copy(src_ref=src, dst_ref=dst, send_sem=send_sems.at[k], recv_sem=recv_sems.at[k],
                                        device_id=dev, device_id_type=MESH)


def _exchange_start(name, srcs, scatter):
    n = len(srcs)
    me = _my_slot()
    landings = []
    for src in srcs:
        own = lax.dynamic_index_in_dim(src, me, 0, keepdims=True) if scatter else src[None]
        landings.append(lax.dynamic_update_slice(lax.empty((N_DEV,) + src.shape[-2:], src.dtype), own, (me, 0, 0)))

    def body(*refs):
        src_refs, land_refs = refs[:n], refs[n:2 * n]
        send_sems, recv_sems = refs[2 * n:2 * n + 2]
        token = refs[-1]
        me, others = _peers()
        for p in range(n):
            for k, dev in enumerate(others):
                _exchange_copy(src_refs[p], land_refs[p], send_sems, recv_sems, p * (N_DEV - 1) + k, dev, me,
                               scatter, False).start()
        token[...] = jnp.zeros_like(token)

    sems = pltpu.SemaphoreType.DMA((n * (N_DEV - 1),))
    hbm = lambda a: pltpu.with_memory_space_constraint(a, pltpu.HBM)
    outs = pl.pallas_call(
        body, name=name,
        out_shape=(sems, sems, *[pltpu.HBM(a.shape, a.dtype) for a in srcs + landings],
                   jax.ShapeDtypeStruct((8, LANES), F32)),
        in_specs=(_HBM,) * (2 * n), out_specs=(_SEM, _SEM) + (_HBM,) * (2 * n) + (pl.BlockSpec(memory_space=pltpu.VMEM),),
        input_output_aliases={i: 2 + i for i in range(2 * n)},
        compiler_params=pltpu.CompilerParams(has_side_effects=_EFFECT),
    )(*[hbm(a) for a in srcs + landings])
    return (outs[0], outs[1], list(outs[2:2 + n]), list(outs[2 + n:2 + 2 * n]), scatter), outs[-1]


def _exchange_wait(name, handle, after):
    send_sems, recv_sems, src_thru, land_thru, scatter = handle
    n = len(src_thru)

    def body(*refs):
        src_refs, land_refs = refs[:n], refs[n:2 * n]
        send_sems, recv_sems = refs[2 * n:2 * n + 2]
        me, others = _peers()
        for p in range(n):
            for k, dev in enumerate(others):
                cp = _exchange_copy(src_refs[p], land_refs[p], send_sems, recv_sems, p * (N_DEV - 1) + k, dev, me,
                                    scatter, True)
                cp.wait_send()
                cp.wait_recv()

    outs = pl.pallas_call(
        body, name=name,
        out_shape=tuple(pltpu.HBM(a.shape, a.dtype) for a in src_thru + land_thru),
        in_specs=(_HBM,) * (2 * n) + (_SEM, _SEM, pl.BlockSpec(memory_space=pl.ANY)), out_specs=(_HBM,) * (2 * n),
        input_output_aliases={i: i for i in range(2 * n)},
        compiler_params=pltpu.CompilerParams(has_side_effects=_EFFECT),
    )(*src_thru, *land_thru, send_sems, recv_sems, after)
    return list(outs[n:])


def _all_reduce_small(name, buf):
    rows = buf.shape[0]
    rb = rows // N_DEV

    def body(x_ref, out_ref, stage, send1, recv1, send2, recv2):
        me, others = _peers()

        def block(ref, k):
            return ref.at[pl.ds(k * rb, rb), :]

        first = [pltpu.make_async_remote_copy(src_ref=block(x_ref, _slot(dev)), dst_ref=stage.at[me],
                                              send_sem=send1.at[k], recv_sem=recv1.at[k], device_id=dev,
                                              device_id_type=MESH) for k, dev in enumerate(others)]
        for cp in first:
            cp.start()
        stage[me] = x_ref[pl.ds(pl.multiple_of(me * rb, 8), rb), :]
        for k, dev in enumerate(others):
            pltpu.make_async_remote_copy(src_ref=block(x_ref, me), dst_ref=stage.at[_slot(dev)],
                                         send_sem=send1.at[k], recv_sem=recv1.at[k], device_id=dev,
                                         device_id_type=MESH).wait_recv()
        total = stage[0]
        for j in range(1, N_DEV):
            total = total + stage[j]
        out_ref[pl.ds(pl.multiple_of(me * rb, 8), rb), :] = total
        second = [pltpu.make_async_remote_copy(src_ref=block(out_ref, me), dst_ref=block(out_ref, me),
                                               send_sem=send2.at[k], recv_sem=recv2.at[k], device_id=dev,
                                               device_id_type=MESH) for k, dev in enumerate(others)]
        for cp in second:
            cp.start()
        for k, dev in enumerate(others):
            pltpu.make_async_remote_copy(src_ref=block(out_ref, me), dst_ref=block(out_ref, _slot(dev)),
                                         send_sem=send2.at[k], recv_sem=recv2.at[k], device_id=dev,
                                         device_id_type=MESH).wait_recv()
        for cp in first + second:
            cp.wait_send()

    sems = pltpu.SemaphoreType.DMA((N_DEV - 1,))
    return pl.pallas_call(
        body, name=name,
        in_specs=[pl.BlockSpec(memory_space=pltpu.VMEM)],
        out_specs=pl.BlockSpec(memory_space=pltpu.VMEM),
        out_shape=jax.ShapeDtypeStruct(buf.shape, F32),
        scratch_shapes=[pltpu.VMEM((N_DEV, rb, LANES), F32), sems, sems, sems, sems],
        compiler_params=pltpu.CompilerParams(vmem_limit_bytes=VMEM_LIMIT_BYTES),
    )(buf)


def _adamw_math(w, g, m, v):
    m = ADAM_B1 * m + (1.0 - ADAM_B1) * g
    v = ADAM_B2 * v + (1.0 - ADAM_B2) * (g * g)
    m_hat = m / (1.0 - ADAM_B1 ** ADAM_STEP)
    v_hat = v / (1.0 - ADAM_B2 ** ADAM_STEP)
    delta = -ADAM_LR * (m_hat / (jnp.sqrt(v_hat) + ADAM_EPS) + ADAM_WD * w)
    return delta, m, v


def _adamw(name, parts, w, m, v, tr=128):
    P, R, W = parts.shape
    tr = min(tr, R)

    def body(p_ref, w_ref, m_ref, v_ref, g_out, d_out, m_out, v_out):
        g = p_ref[0].astype(F32)
        for j in range(1, P):
            g = g + p_ref[j].astype(F32)
        delta, m_new, v_new = _adamw_math(w_ref[...], g, m_ref[...], v_ref[...])
        g_out[...] = g
        d_out[...] = delta
        m_out[...] = m_new
        v_out[...] = v_new

    row = pl.BlockSpec((tr, W), lambda i: (i, 0))
    shape = jax.ShapeDtypeStruct((R, W), F32)
    return pl.pallas_call(
        body, name=name, grid=(R // tr,),
        in_specs=[pl.BlockSpec((P, tr, W), lambda i: (0, i, 0)), row, row, row],
        out_specs=[row, row, row, row],
        out_shape=[shape, shape, shape, shape],
        compiler_params=_params(("parallel",)),
    )(parts, w, m, v)


def _adamw_shard(name, parts, w, m, v, layer, earlier=None, tr=256):
    L, K, N = w.shape
    tr = min(tr, K)
    n_prev = 0 if earlier is None else 4

    def body(p_ref, w_ref, m_ref, v_ref, *rest):
        g_out, d_out, m_out, v_out = rest[n_prev:]
        g = p_ref[0].astype(F32)
        for j in range(1, N_DEV):
            g = g + p_ref[j].astype(F32)
        delta, m_new, v_new = _adamw_math(w_ref[...], g, m_ref[...], v_ref[...])
        g_out[...] = g
        d_out[...] = delta
        m_out[...] = m_new
        v_out[...] = v_new

    row = pl.BlockSpec((None, tr, N), lambda i: (layer, i, 0))
    shape = jax.ShapeDtypeStruct((L, K, N), F32)
    return pl.pallas_call(
        body, name=name, grid=(K // tr,),
        in_specs=[pl.BlockSpec((N_DEV, tr, N), lambda i: (0, i, 0)), row, row, row]
        + [pl.BlockSpec(memory_space=pl.ANY)] * n_prev,
        out_specs=[row, row, row, row],
        out_shape=[shape, shape, shape, shape],
        input_output_aliases={4 + j: j for j in range(n_prev)},
        compiler_params=_params(("parallel",)),
    )(parts, w, m, v, *(earlier or ()))


def _column_slots(full):
    K, N = full.shape
    return jnp.transpose(full.reshape(K, N_DEV, N // N_DEV), (1, 0, 2))


def _from_column_slots(slots):
    _, K, n = slots.shape
    return jnp.transpose(slots, (1, 0, 2)).reshape(K, N_DEV * n)


_SMALL =("mix_norm_g", "mlp_norm_g", "final_norm_g", "a_ln_g", "a_ln_b", "a_w_s", "a_b_s", "rel_bias")


def _pack_small(vals):
    pieces = []
    for n in _SMALL:
        flat = vals[n].reshape(-1)
        pad = (-flat.shape[0]) % (8 * LANES)
        pieces.append(jnp.pad(flat, (0, pad)).reshape(-1, LANES))
    rows = sum(p.shape[0] for p in pieces)
    tail = (-rows) % (8 * N_DEV)
    if tail:
        pieces.append(jnp.zeros((tail, LANES), F32))
    return jnp.concatenate(pieces, axis=0)


def _unpack_small(buf, like):
    out = {}
    r = 0
    for n in _SMALL:
        size = like[n].size
        nrows = -(-size // (8 * LANES)) * 8
        out[n] = buf[r:r + nrows].reshape(-1)[:size].reshape(like[n].shape)
        r += nrows
    return out


_STAGES = (("gate", ("a_w_in", "a_w_out"), 0),
           ("mlp0", ("w_up", "w_down"), 0),
           ("att", ("b_w_qkv", "b_w_out"), 0),
           ("mlp1", ("w_up", "w_down"), 1))


def kernel(x, mix_norm_g, mlp_norm_g, final_norm_g, a_w_in, a_ln_g, a_ln_b, a_w_s, a_b_s, a_w_out, b_w_qkv, b_w_out, rel_bias, w_up, w_down, loss_target, m_mix_norm_g, m_mlp_norm_g, m_final_norm_g, m_a_w_in, m_a_ln_g, m_a_ln_b, m_a_w_s, m_a_b_s, m_a_w_out, m_b_w_qkv, m_b_w_out, m_rel_bias, m_w_up, m_w_down, v_mix_norm_g, v_mlp_norm_g, v_final_norm_g, v_a_w_in, v_a_ln_g, v_a_ln_b, v_a_w_s, v_a_b_s, v_a_w_out, v_b_w_qkv, v_b_w_out, v_rel_bias, v_w_up, v_w_down):
    w = dict(mix_norm_g=mix_norm_g, mlp_norm_g=mlp_norm_g, final_norm_g=final_norm_g, a_w_in=a_w_in, a_ln_g=a_ln_g,
             a_ln_b=a_ln_b, a_w_s=a_w_s, a_b_s=a_b_s, a_w_out=a_w_out, b_w_qkv=b_w_qkv, b_w_out=b_w_out,
             rel_bias=rel_bias, w_up=w_up, w_down=w_down)
    m = dict(mix_norm_g=m_mix_norm_g, mlp_norm_g=m_mlp_norm_g, final_norm_g=m_final_norm_g, a_w_in=m_a_w_in,
             a_ln_g=m_a_ln_g, a_ln_b=m_a_ln_b, a_w_s=m_a_w_s, a_b_s=m_a_b_s, a_w_out=m_a_w_out, b_w_qkv=m_b_w_qkv,
             b_w_out=m_b_w_out, rel_bias=m_rel_bias, w_up=m_w_up, w_down=m_w_down)
    v = dict(mix_norm_g=v_mix_norm_g, mlp_norm_g=v_mlp_norm_g, final_norm_g=v_final_norm_g, a_w_in=v_a_w_in,
             a_ln_g=v_a_ln_g, a_ln_b=v_a_ln_b, a_w_s=v_a_w_s, a_b_s=v_a_b_s, a_w_out=v_a_w_out, b_w_qkv=v_b_w_qkv,
             b_w_out=v_b_w_out, rel_bias=v_rel_bias, w_up=v_w_up, w_down=v_w_down)

    stages = {s: (names, layer) for s, names, layer in _STAGES}
    order = [s for s, _, _ in _STAGES]

    def shards_of(stage):
        names, layer = stages[stage]
        return [_bf(w[n][layer]) for n in names]

    pending = {}
    pending[order[0]], first_token = _exchange_start("gather_" + order[0] + "_start", shards_of(order[0]), False)

    def get_weights(stage, dep):
        gathered = _exchange_wait("gather_" + stage + "_wait", pending.pop(stage), dep)
        nxt = order.index(stage) + 1
        token = None
        if nxt < len(order):
            shards, gathered = lax.optimization_barrier((shards_of(order[nxt]), gathered))
            pending[order[nxt]], token = _exchange_start("gather_" + order[nxt] + "_start", shards, False)
        return gathered, token

    sent = {}

    def put_grads(stage, slot_grads):
        sent[stage], token = _exchange_start("scatter_" + stage + "_start", slot_grads, True)
        return token

    loss_local, grad_x, small_g = _local_step(
        x[0], loss_target[0], mix_norm_g, mlp_norm_g, final_norm_g, a_ln_g, a_ln_b, a_w_s, a_b_s, rel_bias,
        get_weights, put_grads, first_token)

    results = {}
    for stage in reversed(order):
        names, layer = stages[stage]
        received = _exchange_wait("scatter_" + stage + "_wait", sent[stage], grad_x)
        for n, parts in zip(names, received):
            results[n] = _adamw_shard("adamw_%s_%s" % (stage, n), parts, w[n], m[n], v[n], layer, results.get(n))

    reduced = _all_reduce_small("reduce_small", _pack_small(small_g))
    small = [_unpack_small(b, w) for b in _adamw("adamw_small", reduced[None], _pack_small(w), _pack_small(m),
                                                 _pack_small(v), tr=reduced.shape[0])]

    outs = []
    for j in range(4):
        outs.extend(small[j][n] if n in _SMALL else results[n][j] for n in w)
    loss = lax.psum(loss_local, ("x", "y", "c"))
    return (loss, grad_x[None], *outs)


def _local_step(xs, tgt, mix_norm_g, mlp_norm_g, final_norm_g, a_ln_g, a_ln_b, a_w_s, a_b_s, rel_bias,
                get_weights, put_grads, first_token=None):
    D = xs.shape[-1]
    g_mix = [mix_norm_g[l][None, :] for l in range(2)]
    g_mlp = [mlp_norm_g[l][None, :] for l in range(2)]
    g_fin = final_norm_g[None, :]
    ln_g, ln_b = a_ln_g, a_ln_b
    causal = jnp.tril(jnp.ones((CHUNK, CHUNK), dtype=bool))
    wm = _bf(jnp.where(causal[None], a_w_s[0], 0.0))
    bs_full = jnp.repeat(a_b_s[0].T, D // GROUPS, axis=1)
    bias_tiles = _bias_tiles("att_bias", rel_bias, after=first_token)

    y0 = _rms_fwd("rms_mix0", xs, g_mix[0], after=bias_tiles)
    (win, wout), token = get_weights("gate", y0)
    wout = wout.reshape(-1, D)
    uvp = _mm_nn("gate_in", y0, win, tm=512, nc=win.shape[2], shards=True, after=token)
    z = _gate_fwd("gate_mid", uvp, ln_g, ln_b, wm, bs_full)
    h1, y1 = _mm_nn("gate_out", z, wout, tm=512, nc=512, epi="res", extra=xs, norm_g=g_mlp[0])
    (wup0, wdn0), token = get_weights("mlp0", h1)
    wdn0 = wdn0.reshape(-1, D)
    a0, f0 = _mm_nn("mlp0_up", y1, wup0, tm=512, nc=wup0.shape[2], epi="relu2", shards=True, after=token)
    h2, y2 = _mm_nn("mlp0_down", f0, wdn0, tm=512, nc=512, epi="res", extra=h1, norm_g=g_mix[1])
    (wqkv, wo), token = get_weights("att", h2)
    wqkv, wo = _from_column_slots(wqkv), _from_column_slots(wo)
    qkv = _mm_nn("att_qkv", y2, wqkv, tm=256, nc=512, after=token)
    o_att, lse = _att_merge("att_merge", *_att_fwd("att_fwd", qkv, bias_tiles))
    h3, y3 = _mm_nn("att_out", o_att, wo, tm=512, nc=512, epi="res", extra=h2, norm_g=g_mlp[1])
    (wup1, wdn1), _ = get_weights("mlp1", h3)
    wdn1 = wdn1.reshape(-1, D)
    a1, f1 = _mm_nn("mlp1_up", y3, wup1, tm=512, nc=wup1.shape[2], epi="relu2", shards=True)
    h4 = _mm_nn("mlp1_down", f1, wdn1, tm=512, nc=512, epi="res", extra=h3)
    dh, dg_fin, err2 = _final_loss("final_loss", h4, g_fin, tgt)
    loss_local = 0.5 * jnp.sum(err2) / D

    def mlp_bwd(tag, dh, h_in, y, a, f, wup_l, wdn_l, g_row, after):
        da = _mm_nt(tag + "_dact", dh, wdn_l, tm=512, nc=512, epi="mask2relu", extra=a, after=after)
        g_dn = _mm_tn(tag + "_dwdown", f, dh, t1=1024, tn=1024)
        g_up = _mm_tn(tag + "_dwup", y, da, t1=1024, tn=1024, slot_cols=wup_l.shape[2])
        dh_in, dg = _mm_nt_rms_bwd(tag + "_dy", [(da, wup_l, *_whole(wup_l))], h_in, g_row, dh, tm=512, nc=512,
                                   shards=True)
        return dh_in, dg, put_grads(tag, [g_up, g_dn.reshape(N_DEV, -1, D)])

    dh3, dg_mlp1, token = mlp_bwd("mlp1", dh, h3, y3, a1, f1, wup1, wdn1, g_mlp[1], None)

    d_o = _mm_nt("att_dout", dh3, wo, tm=512, nc=512, after=token)
    g_wo = _mm_tn("att_dwo", o_att, dh3, t1=512, tn=1024)
    dq, dk, dv, ds_sums = _att_bwd("att_bwd", qkv, o_att, lse, d_o, bias_tiles)
    part_w = N_DIL * ATT_WIDTH
    g_qkv = [_mm_tn("att_dwqkv%d" % p, y2, t, t1=1024, tn=part_w, tm=1024) for p, t in enumerate((dq, dk, dv))]
    dh2, dg_mix1 = _mm_nt_rms_bwd("att_dy", [(t, wqkv, (D, part_w), (0, p)) for p, t in enumerate((dq, dk, dv))],
                                  h2, g_mix[1], dh3, tm=256, nc=512)
    token = put_grads("att", [_column_slots(jnp.concatenate(g_qkv, axis=1)), _column_slots(g_wo)])

    dh1, dg_mlp0, token = mlp_bwd("mlp0", dh2, h1, y1, a0, f0, wup0, wdn0, g_mlp[0], token)

    dz = _mm_nt("gate_dz", dh1, wout, tm=512, nc=512, after=token)
    g_wout = _mm_tn("gate_dwout", z, dh1, t1=1024, tn=1024)
    duvp, d_wm, d_mixed, d_lng, d_lnb = _gate_bwd("gate_dmid", uvp, dz, ln_g, ln_b, wm, bs_full)
    g_win = _mm_tn("gate_dwin", y0, duvp, t1=1024, tn=1024, slot_cols=win.shape[2])
    token = put_grads("gate", [g_win, g_wout.reshape(N_DEV, -1, D)])
    grad_x, dg_mix0 = _mm_nt_rms_bwd("gate_dy", [(duvp, win, *_whole(win))], xs, g_mix[0], dh1, tm=512, nc=512,
                                     after=token, shards=True)

    small_g = dict(
        mix_norm_g=jnp.concatenate([dg_mix0, dg_mix1], axis=0),
        mlp_norm_g=jnp.concatenate([dg_mlp0, dg_mlp1], axis=0),
        final_norm_g=dg_fin[0], a_ln_g=d_lng, a_ln_b=d_lnb, a_w_s=d_wm[None],
        a_b_s=jnp.sum(d_mixed.reshape(CHUNK, GROUPS, D // GROUPS), axis=2).T[None],
        rel_bias=_bias_grad("att_dbias", ds_sums))
    return loss_local, grad_x, small_g
```

```python
import functools
import math

import jax
import jax.numpy as jnp
from jax import lax
from jax.experimental import pallas as pl
from jax.experimental.pallas import tpu as pltpu

F32 = jnp.float32
BF16 = jnp.bfloat16
MESH = pl.DeviceIdType.MESH

N_DEV = 8
EPS = 1e-6
NEG_INF = -1e30
CHUNK = 128
GROUPS = 8
HEAD_DIM = 64
ATT_HEADS = 8
ATT_WIDTH = ATT_HEADS * HEAD_DIM
DILATIONS = (1, 4, 16)
N_DIL = len(DILATIONS)
N_BUCKETS = 32
MAX_EXACT = N_BUCKETS // 2
REL_MAX_DISTANCE = 2048
ATT_ROWS = 2048
ATT_SCALE = HEAD_DIM ** -0.5
LANES = 128

ADAM_LR = 0.001
ADAM_B1 = 0.9
ADAM_B2 = 0.999
ADAM_EPS = 1e-08
ADAM_WD = 0.01
ADAM_STEP = 10

VMEM_LIMIT_BYTES = 56 * 1024 * 1024


def _params(semantics=None):
    return pltpu.CompilerParams(dimension_semantics=semantics, vmem_limit_bytes=VMEM_LIMIT_BYTES)


def _bf(v):
    return v.astype(BF16)


def _dot(a, b, dims):
    return lax.dot_general(a, b, (dims, ((), ())), preferred_element_type=F32)


NN = ((1,), (0,))
NT = ((1,), (1,))
TN = ((0,), (0,))


def _after_operand(after):
    if after is None:
        return [], []
    return [after], [pl.BlockSpec(memory_space=pl.ANY)]


def _rms_fwd(name, x, g, tm=512, after=None):
    S, D = x.shape
    after_args, after_specs = _after_operand(after)

    def body(x_ref, g_ref, *rest):
        y_ref = rest[-1]
        xv = x_ref[...]
        r = lax.rsqrt(jnp.mean(xv * xv, axis=-1, keepdims=True) + EPS)
        y_ref[...] = _bf(xv * r * g_ref[...])

    return pl.pallas_call(
        body, name=name, grid=(S // tm,),
        in_specs=[pl.BlockSpec((tm, D), lambda i: (i, 0)), pl.BlockSpec((1, D), lambda i: (0, 0))] + after_specs,
        out_specs=pl.BlockSpec((tm, D), lambda i: (i, 0)),
        out_shape=jax.ShapeDtypeStruct((S, D), BF16),
        compiler_params=_params(("parallel",)),
    )(x, g, *after_args)


def _mm_res_loss(name, a, w, res, g, target, *, tm, nc):
    M, D = res.shape

    def body(a_ref, w_ref, r_ref, g_ref, t_ref, dh_ref, dg_ref, l_ref, h_sc):
        i = pl.program_id(0)
        a_v = _bf(a_ref[...])
        for j in range(D // nc):
            cols, acc = _chunk_product([a_v], [w_ref], j, nc, False, False)
            h_sc[:, cols] = r_ref[:, cols] + acc
        xv = h_sc[...]
        r = lax.rsqrt(jnp.mean(xv * xv, axis=-1, keepdims=True) + EPS)
        xh = xv * r
        gv = g_ref[...]
        e = xh * gv - t_ref[...]
        dout = e / D
        dyg = dout * gv
        c = jnp.mean(dyg * xh, axis=-1, keepdims=True)
        dh_ref[...] = r * (dyg - xh * c)
        dg_part = jnp.sum(dout * xh, axis=0, keepdims=True)
        l_part = jnp.sum(e * e, axis=0, keepdims=True)

        @pl.when(i == 0)
        def _():
            dg_ref[...] = dg_part
            l_ref[...] = l_part

        @pl.when(i > 0)
        def _():
            dg_ref[...] += dg_part
            l_ref[...] += l_part

    row = pl.BlockSpec((tm, D), lambda i: (i, 0))
    vec = pl.BlockSpec((1, D), lambda i: (0, 0))
    return pl.pallas_call(
        body, name=name, grid=(M // tm,),
        in_specs=[pl.BlockSpec((tm, a.shape[1]), lambda i: (i, 0)), pl.BlockSpec(w.shape, lambda i: (0, 0)),
                  row, vec, row],
        out_specs=[row, vec, vec],
        out_shape=[jax.ShapeDtypeStruct((M, D), F32), jax.ShapeDtypeStruct((1, D), F32),
                   jax.ShapeDtypeStruct((1, D), F32)],
        scratch_shapes=[pltpu.VMEM((tm, D), F32)],
        compiler_params=_params(("arbitrary",)),
    )(a, w, res, g, target)


def _chunk_product(a_vals, w_refs, j, nc, nt, shards):
    cols = slice(j * nc, (j + 1) * nc)
    acc = None
    for a_v, w_ref in zip(a_vals, w_refs):
        if not shards:
            terms = [_dot(a_v, w_ref[cols, :], NT) if nt else _dot(a_v, w_ref[:, cols], NN)]
        elif nt:
            nl = w_ref.shape[2]
            terms = [_dot(a_v[:, k * nl:(k + 1) * nl], w_ref[k, cols, :], NT) for k in range(N_DEV)]
        else:
            terms = [_dot(a_v, w_ref[j], NN)]
        for t in terms:
            acc = t if acc is None else acc + t
    return cols, acc


def _mm_rows(name, pairs, n_out, *, nt, tm, nc, epi="plain", extra=None, out_dtype=F32, after=None, shards=False,
             norm_g=None):
    M = pairs[0][0].shape[0]
    np_ = len(pairs)
    after_args, after_specs = _after_operand(after)

    def body(*refs):
        a_refs = refs[:np_]
        w_refs = refs[np_:2 * np_]
        pos = 2 * np_
        e_ref = None
        if extra is not None:
            e_ref = refs[pos]
            pos += 1
        if norm_g is not None:
            g_ref = refs[pos]
            pos += 1
        pos += len(after_args)
        outs = refs[pos:]
        a_vals = [_bf(a[...]) for a in a_refs]
        for j in range(n_out // nc):
            cols, acc = _chunk_product(a_vals, w_refs, j, nc, nt, shards)
            if epi == "plain":
                outs[0][:, cols] = acc.astype(out_dtype)
            elif epi == "res":
                outs[0][:, cols] = e_ref[:, cols] + acc
            elif epi == "relu2":
                outs[0][:, cols] = _bf(acc)
                rl = jnp.maximum(acc, 0.0)
                outs[1][:, cols] = _bf(rl * rl)
            elif epi == "mask2relu":
                outs[0][:, cols] = _bf(acc * (2.0 * jnp.maximum(e_ref[:, cols].astype(F32), 0.0)))
        if norm_g is not None:
            hv = outs[0][...]
            r = lax.rsqrt(jnp.mean(hv * hv, axis=-1, keepdims=True) + EPS)
            outs[1][...] = _bf(hv * r * g_ref[...])

    in_specs = [pl.BlockSpec((tm, a.shape[1]), lambda i: (i, 0)) for a, _, _, _ in pairs]
    for _, _, wshape, widx in pairs:
        in_specs.append(pl.BlockSpec(wshape, functools.partial(lambda i, widx: widx, widx=widx)))
    args = [a for a, _, _, _ in pairs] + [w for _, w, _, _ in pairs]
    if extra is not None:
        in_specs.append(pl.BlockSpec((tm, n_out), lambda i: (i, 0)))
        args.append(extra)
    if norm_g is not None:
        in_specs.append(pl.BlockSpec((1, n_out), lambda i: (0, 0)))
        args.append(norm_g)
    in_specs += after_specs
    args += after_args
    row_out = pl.BlockSpec((tm, n_out), lambda i: (i, 0))
    if epi == "relu2":
        out_specs = [row_out, row_out]
        out_shape = [jax.ShapeDtypeStruct((M, n_out), BF16), jax.ShapeDtypeStruct((M, n_out), BF16)]
    elif norm_g is not None:
        out_specs = [row_out, row_out]
        out_shape = [jax.ShapeDtypeStruct((M, n_out), F32), jax.ShapeDtypeStruct((M, n_out), BF16)]
    else:
        dt = BF16 if epi == "mask2relu" else (F32 if epi == "res" else out_dtype)
        out_specs = row_out
        out_shape = jax.ShapeDtypeStruct((M, n_out), dt)
    return pl.pallas_call(
        body, name=name, grid=(M // tm,), in_specs=in_specs, out_specs=out_specs, out_shape=out_shape,
        compiler_params=_params(("parallel",)),
    )(*args)


def _whole(w):
    return w.shape, (0,) * w.ndim


def _mm_nn(name, a, w, **kw):
    n_out = w.shape[0] * w.shape[2] if w.ndim == 3 else w.shape[1]
    return _mm_rows(name, [(a, w, *_whole(w))], n_out, nt=False, **kw)


def _mm_nt(name, a, w, **kw):
    return _mm_rows(name, [(a, w, *_whole(w))], w.shape[0], nt=True, **kw)


def _mm_nt_rms_bwd(name, pairs, x, g, dres, *, tm, nc, after=None, shards=False):
    M, D = x.shape
    np_ = len(pairs)
    after_args, after_specs = _after_operand(after)

    def body(*refs):
        a_refs = refs[:np_]
        w_refs = refs[np_:2 * np_]
        x_ref, g_ref, r_ref = refs[2 * np_:2 * np_ + 3]
        dx_ref, dg_ref, dy_sc = refs[-3:]
        i = pl.program_id(0)
        a_vals = [_bf(a[...]) for a in a_refs]
        for j in range(D // nc):
            cols, acc = _chunk_product(a_vals, w_refs, j, nc, True, shards)
            dy_sc[:, cols] = acc
        xv = x_ref[...]
        r = lax.rsqrt(jnp.mean(xv * xv, axis=-1, keepdims=True) + EPS)
        xh = xv * r
        dy_v = dy_sc[...]
        dyg = dy_v * g_ref[...]
        c = jnp.mean(dyg * xh, axis=-1, keepdims=True)
        dx_ref[...] = r_ref[...] + r * (dyg - xh * c)
        part = jnp.sum(dy_v * xh, axis=0, keepdims=True)

        @pl.when(i == 0)
        def _():
            dg_ref[...] = part

        @pl.when(i > 0)
        def _():
            dg_ref[...] += part

    row = pl.BlockSpec((tm, D), lambda i: (i, 0))
    vec = pl.BlockSpec((1, D), lambda i: (0, 0))
    in_specs = [pl.BlockSpec((tm, a.shape[1]), lambda i: (i, 0)) for a, _, _, _ in pairs]
    for _, _, wshape, widx in pairs:
        in_specs.append(pl.BlockSpec(wshape, functools.partial(lambda i, widx: widx, widx=widx)))
    args = [a for a, _, _, _ in pairs] + [w for _, w, _, _ in pairs]
    return pl.pallas_call(
        body, name=name, grid=(M // tm,),
        in_specs=in_specs + [row, vec, row] + after_specs,
        out_specs=[row, vec],
        out_shape=[jax.ShapeDtypeStruct((M, D), F32), jax.ShapeDtypeStruct((1, D), F32)],
        scratch_shapes=[pltpu.VMEM((tm, D), F32)],
        compiler_params=_params(("arbitrary",)),
    )(*args, x, g, dres, *after_args)


def _mm_tn(name, a, b, *, t1, tn, tm=2048, slot_cols=None):
    M, K1 = a.shape
    N = b.shape[1]
    nm = M // tm

    def body(a_ref, b_ref, o_ref, acc_ref):
        m = pl.program_id(2)
        t = _dot(_bf(a_ref[...]), _bf(b_ref[...]), TN)

        @pl.when(m == 0)
        def _():
            acc_ref[...] = t

        @pl.when(m > 0)
        def _():
            acc_ref[...] += t

        @pl.when(m == nm - 1)
        def _():
            if slot_cols is None:
                o_ref[...] = _bf(acc_ref[...])
            else:
                for k in range(tn // slot_cols):
                    o_ref[k] = _bf(acc_ref[:, k * slot_cols:(k + 1) * slot_cols])

    if slot_cols is not None:
        out_spec = pl.BlockSpec((tn // slot_cols, t1, slot_cols), lambda i, j, m: (j, i, 0))
        out_shape = jax.ShapeDtypeStruct((N // slot_cols, K1, slot_cols), BF16)
    else:
        out_spec = pl.BlockSpec((t1, tn), lambda i, j, m: (i, j))
        out_shape = jax.ShapeDtypeStruct((K1, N), BF16)
    return pl.pallas_call(
        body, name=name, grid=(K1 // t1, N // tn, nm),
        in_specs=[pl.BlockSpec((tm, t1), lambda i, j, m: (m, i)), pl.BlockSpec((tm, tn), lambda i, j, m: (m, j))],
        out_specs=out_spec, out_shape=out_shape,
        scratch_shapes=[pltpu.VMEM((t1, tn), F32)],
        compiler_params=_params(("parallel", "parallel", "arbitrary")),
    )(a, b)


_INV_SQRT2 = 1.0 / math.sqrt(2.0)
_INV_SQRT2PI = 1.0 / math.sqrt(2.0 * math.pi)


def _gelu(x):
    return 0.5 * x * (1.0 + lax.erf(x * _INV_SQRT2))


def _gelu_and_grad(x):
    cdf = 0.5 * (1.0 + lax.erf(x * _INV_SQRT2))
    return x * cdf, cdf + x * (_INV_SQRT2PI * jnp.exp(-0.5 * x * x))


def _layer_norm_parts(v):
    mu = jnp.mean(v, axis=-1, keepdims=True)
    xc = v - mu
    rs = lax.rsqrt(jnp.mean(xc * xc, axis=-1, keepdims=True) + EPS)
    return xc * rs, rs


def _gate_fwd(name, uvp, ln_g, ln_b, wm, bs_full, tr=512):
    S, W2 = uvp.shape
    W = W2 // 2
    gd = W // GROUPS

    def body(u_ref, v_ref, lg_ref, lb_ref, wm_ref, bs_ref, z_ref):
        vh, _ = _layer_norm_parts(_gelu(v_ref[...]))
        vn = _bf(vh * lg_ref[...] + lb_ref[...])
        for ci in range(tr // CHUNK):
            rows = slice(ci * CHUNK, (ci + 1) * CHUNK)
            for g in range(GROUPS):
                cols = slice(g * gd, (g + 1) * gd)
                mixed = _dot(wm_ref[g], vn[rows, cols], NN) + bs_ref[:, cols]
                z_ref[rows, cols] = _bf(_gelu(u_ref[rows, cols]) * mixed)

    vec = pl.BlockSpec((1, W), lambda i: (0, 0))
    return pl.pallas_call(
        body, name=name, grid=(S // tr,),
        in_specs=[pl.BlockSpec((tr, W), lambda i: (i, 0)), pl.BlockSpec((tr, W), lambda i: (i, 1)), vec, vec,
                  pl.BlockSpec((GROUPS, CHUNK, CHUNK), lambda i: (0, 0, 0)),
                  pl.BlockSpec((CHUNK, W), lambda i: (0, 0))],
        out_specs=pl.BlockSpec((tr, W), lambda i: (i, 0)),
        out_shape=jax.ShapeDtypeStruct((S, W), BF16),
        compiler_params=_params(("parallel",)),
    )(uvp, uvp, ln_g, ln_b, wm, bs_full)


def _gate_bwd(name, uvp, dz, ln_g, ln_b, wm, bs_full, tr=256):
    S, W2 = uvp.shape
    W = W2 // 2
    gd = W // GROUPS
    n_steps = S // tr

    def body(u_ref, v_ref, dz_ref, lg_ref, lb_ref, wm_ref, bs_ref, duv_ref, dwm_ref, dmx_ref, dlg_ref, dlb_ref,
             dvn_ref):
        i = pl.program_id(0)
        v, dv_dvp = _gelu_and_grad(v_ref[...])
        vh, rs = _layer_norm_parts(v)
        lg = lg_ref[...]
        vn = _bf(vh * lg + lb_ref[...])

        @pl.when(i == 0)
        def _():
            dwm_ref[...] = jnp.zeros_like(dwm_ref)
            dmx_ref[...] = jnp.zeros_like(dmx_ref)
            dlg_ref[...] = jnp.zeros_like(dlg_ref)
            dlb_ref[...] = jnp.zeros_like(dlb_ref)

        for ci in range(tr // CHUNK):
            rows = slice(ci * CHUNK, (ci + 1) * CHUNK)
            for g in range(GROUPS):
                cols = slice(g * gd, (g + 1) * gd)
                u, du_dup = _gelu_and_grad(u_ref[rows, cols])
                dz_v = dz_ref[rows, cols]
                dmixed = dz_v * u
                dmx_ref[:, cols] += dmixed
                dmixed_b = _bf(dmixed)
                mixed = _dot(wm_ref[g], vn[rows, cols], NN) + bs_ref[:, cols]
                duv_ref[rows, cols] = _bf(dz_v * mixed * du_dup)
                dwm_ref[g] += _dot(dmixed_b, vn[rows, cols], NT)
                dvn_ref[rows, cols] = _dot(wm_ref[g], dmixed_b, TN)
        dvn = dvn_ref[...]
        dlg_ref[...] += jnp.sum(dvn * vh, axis=0, keepdims=True)
        dlb_ref[...] += jnp.sum(dvn, axis=0, keepdims=True)
        dvh = dvn * lg
        dv = rs * (dvh - jnp.mean(dvh, axis=-1, keepdims=True) - vh * jnp.mean(dvh * vh, axis=-1, keepdims=True))
        duv_ref[:, W:] = _bf(dv * dv_dvp)

        @pl.when(i == n_steps - 1)
        def _():
            t_idx = lax.broadcasted_iota(jnp.int32, (CHUNK, CHUNK), 0)
            s_idx = lax.broadcasted_iota(jnp.int32, (CHUNK, CHUNK), 1)
            keep = (s_idx <= t_idx).astype(F32)
            for g in range(GROUPS):
                dwm_ref[g] = dwm_ref[g] * keep

    vec = pl.BlockSpec((1, W), lambda i: (0, 0))
    row = pl.BlockSpec((tr, W), lambda i: (i, 0))
    return pl.pallas_call(
        body, name=name, grid=(n_steps,),
        in_specs=[row, pl.BlockSpec((tr, W), lambda i: (i, 1)), row, vec, vec,
                  pl.BlockSpec((GROUPS, CHUNK, CHUNK), lambda i: (0, 0, 0)),
                  pl.BlockSpec((CHUNK, W), lambda i: (0, 0))],
        out_specs=[pl.BlockSpec((tr, W2), lambda i: (i, 0)),
                   pl.BlockSpec((GROUPS, CHUNK, CHUNK), lambda i: (0, 0, 0)),
                   pl.BlockSpec((CHUNK, W), lambda i: (0, 0)), vec, vec],
        out_shape=[jax.ShapeDtypeStruct((S, W2), BF16), jax.ShapeDtypeStruct((GROUPS, CHUNK, CHUNK), F32),
                   jax.ShapeDtypeStruct((CHUNK, W), F32), jax.ShapeDtypeStruct((1, W), F32),
                   jax.ShapeDtypeStruct((1, W), F32)],
        scratch_shapes=[pltpu.VMEM((tr, W), F32)],
        compiler_params=_params(("arbitrary",)),
    )(uvp, uvp, dz, ln_g, ln_b, wm, bs_full)


def _t5_bucket(distance):
    small = distance < MAX_EXACT
    nf = jnp.maximum(distance, 1).astype(F32)
    large = MAX_EXACT + (jnp.log(nf / MAX_EXACT) / math.log(REL_MAX_DISTANCE / MAX_EXACT)
                         * (N_BUCKETS - MAX_EXACT)).astype(jnp.int32)
    large = jnp.minimum(large, N_BUCKETS - 1)
    return jnp.where(small, distance, large)


TILE_ELEMS = 2 * CHUNK * CHUNK


def _band_buckets():
    rel = CHUNK + jnp.arange(CHUNK)[None, :] - jnp.arange(2 * CHUNK)[:, None]
    band = (rel >= 0) & (rel <= CHUNK)
    buckets = [_t5_bucket(jnp.clip(rel, 0, CHUNK) * d) for d in DILATIONS]
    return jnp.stack(buckets), band


def _bucket_onehot():
    buckets, _ = _band_buckets()
    return (buckets.reshape(N_DIL, 1, TILE_ELEMS) == jnp.arange(N_BUCKETS)[None, :, None]).astype(F32)


def _bias_tiles(name, rel_bias, after=None):
    _, band = _band_buckets()
    own = band & (jnp.arange(2 * CHUNK) >= CHUNK)[:, None]
    masks = jnp.stack([own, band]).reshape(2, TILE_ELEMS).astype(F32)
    tables = jnp.transpose(rel_bias.reshape(N_BUCKETS, N_DIL, ATT_HEADS), (1, 2, 0))
    after_args, after_specs = _after_operand(after)

    def body(t_ref, oh_ref, m_ref, *rest):
        out_ref = rest[-1]
        for g in range(N_DIL):
            bias = lax.dot_general(t_ref[g], oh_ref[g], (NN, ((), ())), precision=lax.Precision.HIGHEST,
                                   preferred_element_type=F32)
            for f in range(2):
                out_ref[g, f] = jnp.where(m_ref[f:f + 1, :] > 0.5, bias, NEG_INF)

    whole = pl.BlockSpec(memory_space=pltpu.VMEM)
    out = pl.pallas_call(
        body, name=name, out_shape=jax.ShapeDtypeStruct((N_DIL, 2, ATT_HEADS, TILE_ELEMS), F32),
        in_specs=[whole, whole, whole] + after_specs, out_specs=whole,
        compiler_params=_params(),
    )(tables, _bucket_onehot(), masks, *after_args)
    out = out.reshape(N_DIL, 2, ATT_HEADS // 2, 2, 2 * CHUNK, CHUNK)
    return jnp.transpose(out, (0, 1, 2, 4, 3, 5)).reshape(N_DIL, 2, ATT_HEADS // 2, 2 * CHUNK, 2 * CHUNK)


def _att_specs(order):
    def column(part, ids):
        hp, g, _ = order(*ids)
        return part * 3 * 4 + g * 4 + hp

    def window(part):
        def index(*ids):
            c = order(*ids)[2]
            return pl.multiple_of(jnp.maximum(c - 1, 0) * ATT_ROWS, ATT_ROWS), column(part, ids) * LANES
        return pl.BlockSpec((pl.Element(2 * ATT_ROWS), pl.Element(LANES)), index)

    return [pl.BlockSpec((ATT_ROWS, LANES), lambda *ids: (order(*ids)[2], column(0, ids))), window(1), window(2)]


def _window_base(c):
    return jnp.where(c == 0, 0, ATT_ROWS)


def _rows(start, d):
    if d == 1:
        return pl.ds(pl.multiple_of(start, CHUNK), CHUNK)
    return pl.ds(start, CHUNK, stride=d)


def _att_tile_offsets(t, d):
    n = t // d
    r = t % d
    return n * (CHUNK * d) + r, n


def _head_pair_columns(x_t):
    zeros = jnp.zeros((HEAD_DIM, CHUNK), x_t.dtype)
    return jnp.concatenate([jnp.concatenate([x_t[:HEAD_DIM], zeros], axis=0),
                            jnp.concatenate([zeros, x_t[HEAD_DIM:]], axis=0)], axis=1)


def _head_pair_rows(y):
    return jnp.concatenate([y[:HEAD_DIM, :CHUNK], y[HEAD_DIM:, CHUNK:]], axis=0)


def _att_fwd(name, qkv, bias_tiles):
    S = qkv.shape[0]
    n_chunks = S // ATT_ROWS
    tiles = ATT_ROWS // CHUNK

    def body(q_ref, kk, vv, b_ref, o_ref, l_ref):
        c = pl.program_id(1)
        g = pl.program_id(2)
        base = _window_base(c)

        for gi, d in enumerate(DILATIONS):
            @pl.when(g == gi)
            def _(d=d):
                span = CHUNK * d

                def tile(t, carry):
                    q0, n = _att_tile_offsets(t, d)
                    first = (c == 0) & (n == 0)
                    rows = _rows(q0, d)
                    cur = _rows(base + q0, d)
                    prev = _rows(jnp.where(first, q0, base + q0 - span), d)
                    inner = jnp.where(first, 0, 1)
                    qq = _head_pair_columns(_bf(q_ref[rows, :] * ATT_SCALE).T)
                    s_p = _dot(_bf(kk[prev, :]), qq, NN) + b_ref[inner, 0:CHUNK, :]
                    s_c = _dot(_bf(kk[cur, :]), qq, NN) + b_ref[inner, CHUNK:2 * CHUNK, :]
                    m = jnp.maximum(jnp.max(s_p, axis=0, keepdims=True), jnp.max(s_c, axis=0, keepdims=True))
                    p_p = jnp.exp(s_p - m)
                    p_c = jnp.exp(s_c - m)
                    l = jnp.sum(p_p, axis=0, keepdims=True) + jnp.sum(p_c, axis=0, keepdims=True)
                    o2 = (_dot(_bf(vv[prev, :]).T, _bf(p_p), NN)
                          + _dot(_bf(vv[cur, :]).T, _bf(p_c), NN)) * (1.0 / l)
                    lse = m + jnp.log(l)
                    l_t = jnp.concatenate([jnp.broadcast_to(lse[:, :CHUNK], (HEAD_DIM, CHUNK)),
                                           jnp.broadcast_to(lse[:, CHUNK:], (HEAD_DIM, CHUNK))], axis=0)
                    o_ref[rows, :] = _head_pair_rows(o2).T
                    l_ref[rows, :] = l_t.T
                    return carry

                lax.fori_loop(0, tiles, tile, 0, unroll=8)

    order = lambda hp, c, g: (hp, g, c)
    out_spec = pl.BlockSpec((None, ATT_ROWS, LANES), lambda hp, c, g: (g, c, hp))
    shape = jax.ShapeDtypeStruct((N_DIL, S, ATT_WIDTH), F32)
    return pl.pallas_call(
        body, name=name, grid=(ATT_HEADS // 2, n_chunks, N_DIL),
        in_specs=_att_specs(order) + [
            pl.BlockSpec((None, 2, None, 2 * CHUNK, 2 * CHUNK), lambda hp, c, g: (g, 0, hp, 0, 0))],
        out_specs=[out_spec, out_spec],
        out_shape=[shape, shape],
        compiler_params=_params(("parallel", "parallel", "parallel")),
    )(qkv, qkv, qkv, bias_tiles)


def _att_merge(name, o_g, l_g, tm=512):
    _, S, W = o_g.shape

    def body(o_ref, l_ref, out_ref, lse_ref):
        ls = [l_ref[g] for g in range(N_DIL)]
        mx = functools.reduce(jnp.maximum, ls)
        ws = [jnp.exp(l - mx) for l in ls]
        tot = functools.reduce(lambda a, b: a + b, ws)
        acc = ws[0] * o_ref[0]
        for g in range(1, N_DIL):
            acc = acc + ws[g] * o_ref[g]
        out_ref[...] = acc / tot
        lse_ref[...] = mx + jnp.log(tot)

    blk = pl.BlockSpec((N_DIL, tm, W), lambda i: (0, i, 0))
    row = pl.BlockSpec((tm, W), lambda i: (i, 0))
    shape = jax.ShapeDtypeStruct((S, W), F32)
    return pl.pallas_call(
        body, name=name, grid=(S // tm,), in_specs=[blk, blk], out_specs=[row, row], out_shape=[shape, shape],
        compiler_params=_params(("parallel",)),
    )(o_g, l_g)


def _att_bwd(name, qkv, o, lse, d_o, bias_tiles):
    S = qkv.shape[0]
    n_chunks = S // ATT_ROWS
    tiles = ATT_ROWS // CHUNK

    def body(q_ref, kk, vv, o_ref, l_ref, do_ref, b_ref, dq_out, dk_out, dv_out, ds_ref, dq_ref, dk_ref, dv_ref):
        g = pl.program_id(1)
        c = pl.program_id(2)

        @pl.when(c == 0)
        def _():
            dk_ref[...] = jnp.zeros_like(dk_ref)
            dv_ref[...] = jnp.zeros_like(dv_ref)
            ds_ref[...] = jnp.zeros_like(ds_ref)

        base = _window_base(c)
        first_row = c * ATT_ROWS
        head0 = lax.broadcasted_iota(jnp.int32, (CHUNK, LANES), 1) < HEAD_DIM

        def head_pair_stack(x):
            zero = jnp.zeros_like(x)
            return jnp.concatenate([jnp.where(head0, x, zero), jnp.where(head0, zero, x)], axis=0)

        for gi, d in enumerate(DILATIONS):
            @pl.when(g == gi)
            def _(d=d):
                span = CHUNK * d

                def tile(t, carry):
                    q0, n = _att_tile_offsets(t, d)
                    first = (c == 0) & (n == 0)
                    rows = _rows(q0, d)
                    cur = _rows(base + q0, d)
                    prev = _rows(jnp.where(first, q0, base + q0 - span), d)
                    inner = jnp.where(first, 0, 1)
                    g_cur = _rows(first_row + q0, d)
                    g_prev = _rows(jnp.where(first, q0, first_row + q0 - span), d)
                    q2 = _bf(q_ref[rows, :] * ATT_SCALE)
                    q_t = q2.T
                    k2 = _bf(jnp.concatenate([kk[prev, :], kk[cur, :]], axis=0))
                    k_t = k2.T
                    v2 = _bf(jnp.concatenate([vv[prev, :], vv[cur, :]], axis=0))
                    do2 = do_ref[rows, :]
                    do_b = _bf(do2)
                    do_t = do_b.T
                    lse_t = l_ref[rows, :].T
                    dd_t = (do2 * o_ref[rows, :]).T
                    lse = jnp.concatenate([lse_t[0:1], lse_t[HEAD_DIM:HEAD_DIM + 1]], axis=1)
                    delta = jnp.concatenate([jnp.sum(dd_t[:HEAD_DIM], axis=0, keepdims=True),
                                             jnp.sum(dd_t[HEAD_DIM:], axis=0, keepdims=True)], axis=1)
                    s = _dot(k2, _head_pair_columns(q_t), NN) + b_ref[inner]
                    p = jnp.exp(s - lse)
                    ds = p * (_dot(v2, _head_pair_columns(do_t), NN) - delta)
                    ds_ref[...] += ds
                    ds_b = _bf(ds)
                    dq_t = _head_pair_rows(_dot(k_t, ds_b, NN))
                    dk2 = _dot(ds_b, head_pair_stack(q2), NN)
                    dv2 = _dot(_bf(p), head_pair_stack(do_b), NN)
                    dq_ref[rows, :] = (dq_t * ATT_SCALE).T
                    dk_ref[g_prev, :] += dk2[0:CHUNK]
                    dk_ref[g_cur, :] += dk2[CHUNK:2 * CHUNK]
                    dv_ref[g_prev, :] += dv2[0:CHUNK]
                    dv_ref[g_cur, :] += dv2[CHUNK:2 * CHUNK]
                    return carry

                lax.fori_loop(0, tiles, tile, 0, unroll=4)

        dq_out[...] = _bf(dq_ref[...])

        @pl.when(c == n_chunks - 1)
        def _():
            dk_out[...] = _bf(dk_ref[...])
            dv_out[...] = _bf(dv_ref[...])

    order = lambda hp, g, c: (hp, g, c)
    chunk = pl.BlockSpec((ATT_ROWS, LANES), lambda hp, g, c: (c, hp))
    slab = pl.BlockSpec((S, LANES), lambda hp, g, c: (0, g * 4 + hp))
    width = N_DIL * ATT_WIDTH
    dq, dk, dv, ds_sums = pl.pallas_call(
        body, name=name, grid=(ATT_HEADS // 2, N_DIL, n_chunks),
        in_specs=_att_specs(order) + [chunk, chunk, chunk,
                                      pl.BlockSpec((None, 2, None, 2 * CHUNK, 2 * CHUNK),
                                                   lambda hp, g, c: (g, 0, hp, 0, 0))],
        out_specs=[pl.BlockSpec((ATT_ROWS, LANES), lambda hp, g, c: (c, g * 4 + hp)), slab, slab,
                   pl.BlockSpec((None, None, 2 * CHUNK, 2 * CHUNK), lambda hp, g, c: (g, hp, 0, 0))],
        out_shape=[jax.ShapeDtypeStruct((S, width), BF16), jax.ShapeDtypeStruct((S, width), BF16),
                   jax.ShapeDtypeStruct((S, width), BF16),
                   jax.ShapeDtypeStruct((N_DIL, ATT_HEADS // 2, 2 * CHUNK, 2 * CHUNK), F32)],
        scratch_shapes=[pltpu.VMEM((ATT_ROWS, LANES), F32), pltpu.VMEM((S, LANES), F32),
                        pltpu.VMEM((S, LANES), F32)],
        compiler_params=_params(("parallel", "parallel", "arbitrary")),
    )(qkv, qkv, qkv, o, lse, d_o, bias_tiles)
    ds_sums = ds_sums.reshape(N_DIL, ATT_HEADS // 2, 2 * CHUNK, 2, CHUNK)
    ds_sums = jnp.transpose(ds_sums, (0, 1, 3, 2, 4)).reshape(N_DIL, ATT_HEADS, 2 * CHUNK, CHUNK)
    return dq, dk, dv, ds_sums


def _bias_grad(name, ds_sums):
    flat = ds_sums.reshape(N_DIL, ATT_HEADS, TILE_ELEMS)

    def body(oh_ref, ds_ref, out_ref):
        for g in range(N_DIL):
            out_ref[g] = lax.dot_general(oh_ref[g], ds_ref[g], (NT, ((), ())), precision=lax.Precision.HIGHEST,
                                         preferred_element_type=F32)

    out = pl.pallas_call(
        body, name=name, out_shape=jax.ShapeDtypeStruct((N_DIL, N_BUCKETS, ATT_HEADS), F32),
        compiler_params=_params(),
    )(_bucket_onehot(), flat)
    return jnp.transpose(out, (1, 0, 2)).reshape(N_BUCKETS, N_DIL * ATT_HEADS)


def _peers():
    x, y, c = lax.axis_index("x"), lax.axis_index("y"), lax.axis_index("c")
    me = 4 * x + 2 * y + c
    others = [(x, y, 1 - c), (1 - x, y, c), (x, 1 - y, c), (1 - x, 1 - y, c),
              (1 - x, y, 1 - c), (x, 1 - y, 1 - c), (1 - x, 1 - y, 1 - c)]
    return me, others


def _slot(dev):
    return 4 * dev[0] + 2 * dev[1] + dev[2]


_HBM =pl.BlockSpec(memory_space=pltpu.HBM)
_SEM = pl.BlockSpec(memory_space=pltpu.SEMAPHORE)
_EFFECT = pltpu.SideEffectType.DATAFLOW_SIDE_EFFECTING


def _my_slot():
    return 4 * lax.axis_index("x") + 2 * lax.axis_index("y") + lax.axis_index("c")


def _exchange_copy(src_ref, land_ref, send_sems, recv_sems, k, dev, me, scatter, arriving):
    src = src_ref.at[me if arriving else _slot(dev)] if scatter else src_ref
    dst = land_ref.at[_slot(dev) if arriving else me]
    return pltpu.make_async_remote_copy(src_ref=src, dst_ref=dst, send_sem=send_sems.at[k], recv_sem=recv_sems.at[k],
                                        device_id=dev, device_id_type=MESH)


def _exchange_start(name, srcs, scatter):
    n = len(srcs)
    me = _my_slot()
    landings = []
    for src in srcs:
        own = lax.dynamic_index_in_dim(src, me, 0, keepdims=True) if scatter else src[None]
        landings.append(lax.dynamic_update_slice(lax.empty((N_DEV,) + src.shape[-2:], src.dtype), own, (me, 0, 0)))

    def body(*refs):
        src_refs, land_refs = refs[:n], refs[n:2 * n]
        send_sems, recv_sems = refs[2 * n:2 * n + 2]
        token = refs[-1]
        me, others = _peers()
        for p in range(n):
            for k, dev in enumerate(others):
                _exchange_copy(src_refs[p], land_refs[p], send_sems, recv_sems, p * (N_DEV - 1) + k, dev, me,
                               scatter, False).start()
        token[...] = jnp.zeros_like(token)

    sems = pltpu.SemaphoreType.DMA((n * (N_DEV - 1),))
    hbm = lambda a: pltpu.with_memory_space_constraint(a, pltpu.HBM)
    outs = pl.pallas_call(
        body, name=name,
        out_shape=(sems, sems, *[pltpu.HBM(a.shape, a.dtype) for a in srcs + landings],
                   jax.ShapeDtypeStruct((8, LANES), F32)),
        in_specs=(_HBM,) * (2 * n), out_specs=(_SEM, _SEM) + (_HBM,) * (2 * n) + (pl.BlockSpec(memory_space=pltpu.VMEM),),
        input_output_aliases={i: 2 + i for i in range(2 * n)},
        compiler_params=pltpu.CompilerParams(has_side_effects=_EFFECT),
    )(*[hbm(a) for a in srcs + landings])
    return (outs[0], outs[1], list(outs[2:2 + n]), list(outs[2 + n:2 + 2 * n]), scatter), outs[-1]


def _exchange_wait(name, handle, after):
    send_sems, recv_sems, src_thru, land_thru, scatter = handle
    n = len(src_thru)

    def body(*refs):
        src_refs, land_refs = refs[:n], refs[n:2 * n]
        send_sems, recv_sems = refs[2 * n:2 * n + 2]
        me, others = _peers()
        for p in range(n):
            for k, dev in enumerate(others):
                cp = _exchange_copy(src_refs[p], land_refs[p], send_sems, recv_sems, p * (N_DEV - 1) + k, dev, me,
                                    scatter, True)
                cp.wait_send()
                cp.wait_recv()

    outs = pl.pallas_call(
        body, name=name,
        out_shape=tuple(pltpu.HBM(a.shape, a.dtype) for a in src_thru + land_thru),
        in_specs=(_HBM,) * (2 * n) + (_SEM, _SEM, pl.BlockSpec(memory_space=pl.ANY)), out_specs=(_HBM,) * (2 * n),
        input_output_aliases={i: i for i in range(2 * n)},
        compiler_params=pltpu.CompilerParams(has_side_effects=_EFFECT),
    )(*src_thru, *land_thru, send_sems, recv_sems, after)
    return list(outs[n:])


def _all_reduce_small(name, buf):
    rows = buf.shape[0]
    rb = rows // N_DEV

    def body(x_ref, out_ref, stage, send1, recv1, send2, recv2):
        me, others = _peers()

        def block(ref, k):
            return ref.at[pl.ds(k * rb, rb), :]

        first = [pltpu.make_async_remote_copy(src_ref=block(x_ref, _slot(dev)), dst_ref=stage.at[me],
                                              send_sem=send1.at[k], recv_sem=recv1.at[k], device_id=dev,
                                              device_id_type=MESH) for k, dev in enumerate(others)]
        for cp in first:
            cp.start()
        stage[me] = x_ref[pl.ds(pl.multiple_of(me * rb, 8), rb), :]
        for k, dev in enumerate(others):
            pltpu.make_async_remote_copy(src_ref=block(x_ref, me), dst_ref=stage.at[_slot(dev)],
                                         send_sem=send1.at[k], recv_sem=recv1.at[k], device_id=dev,
                                         device_id_type=MESH).wait_recv()
        total = stage[0]
        for j in range(1, N_DEV):
            total = total + stage[j]
        out_ref[pl.ds(pl.multiple_of(me * rb, 8), rb), :] = total
        second = [pltpu.make_async_remote_copy(src_ref=block(out_ref, me), dst_ref=block(out_ref, me),
                                               send_sem=send2.at[k], recv_sem=recv2.at[k], device_id=dev,
                                               device_id_type=MESH) for k, dev in enumerate(others)]
        for cp in second:
            cp.start()
        for k, dev in enumerate(others):
            pltpu.make_async_remote_copy(src_ref=block(out_ref, me), dst_ref=block(out_ref, _slot(dev)),
                                         send_sem=send2.at[k], recv_sem=recv2.at[k], device_id=dev,
                                         device_id_type=MESH).wait_recv()
        for cp in first + second:
            cp.wait_send()

    sems = pltpu.SemaphoreType.DMA((N_DEV - 1,))
    return pl.pallas_call(
        body, name=name,
        in_specs=[pl.BlockSpec(memory_space=pltpu.VMEM)],
        out_specs=pl.BlockSpec(memory_space=pltpu.VMEM),
        out_shape=jax.ShapeDtypeStruct(buf.shape, F32),
        scratch_shapes=[pltpu.VMEM((N_DEV, rb, LANES), F32), sems, sems, sems, sems],
        compiler_params=pltpu.CompilerParams(vmem_limit_bytes=VMEM_LIMIT_BYTES),
    )(buf)


def _adamw_math(w, g, m, v):
    m = ADAM_B1 * m + (1.0 - ADAM_B1) * g
    v = ADAM_B2 * v + (1.0 - ADAM_B2) * (g * g)
    m_hat = m / (1.0 - ADAM_B1 ** ADAM_STEP)
    v_hat = v / (1.0 - ADAM_B2 ** ADAM_STEP)
    delta = -ADAM_LR * (m_hat / (jnp.sqrt(v_hat) + ADAM_EPS) + ADAM_WD * w)
    return delta, m, v


def _adamw(name, parts, w, m, v, tr=128):
    P, R, W = parts.shape
    tr = min(tr, R)

    def body(p_ref, w_ref, m_ref, v_ref, g_out, d_out, m_out, v_out):
        g = p_ref[0].astype(F32)
        for j in range(1, P):
            g = g + p_ref[j].astype(F32)
        delta, m_new, v_new = _adamw_math(w_ref[...], g, m_ref[...], v_ref[...])
        g_out[...] = g
        d_out[...] = delta
        m_out[...] = m_new
        v_out[...] = v_new

    row = pl.BlockSpec((tr, W), lambda i: (i, 0))
    shape = jax.ShapeDtypeStruct((R, W), F32)
    return pl.pallas_call(
        body, name=name, grid=(R // tr,),
        in_specs=[pl.BlockSpec((P, tr, W), lambda i: (0, i, 0)), row, row, row],
        out_specs=[row, row, row, row],
        out_shape=[shape, shape, shape, shape],
        compiler_params=_params(("parallel",)),
    )(parts, w, m, v)


def _adamw_shard(name, parts, w, m, v, layer, earlier=None, tr=256):
    L, K, N = w.shape
    tr = min(tr, K)
    n_prev = 0 if earlier is None else 4

    def body(p_ref, w_ref, m_ref, v_ref, *rest):
        g_out, d_out, m_out, v_out = rest[n_prev:]
        g = p_ref[0].astype(F32)
        for j in range(1, N_DEV):
            g = g + p_ref[j].astype(F32)
        delta, m_new, v_new = _adamw_math(w_ref[...], g, m_ref[...], v_ref[...])
        g_out[...] = g
        d_out[...] = delta
        m_out[...] = m_new
        v_out[...] = v_new

    row = pl.BlockSpec((None, tr, N), lambda i: (layer, i, 0))
    shape = jax.ShapeDtypeStruct((L, K, N), F32)
    return pl.pallas_call(
        body, name=name, grid=(K // tr,),
        in_specs=[pl.BlockSpec((N_DEV, tr, N), lambda i: (0, i, 0)), row, row, row]
        + [pl.BlockSpec(memory_space=pl.ANY)] * n_prev,
        out_specs=[row, row, row, row],
        out_shape=[shape, shape, shape, shape],
        input_output_aliases={4 + j: j for j in range(n_prev)},
        compiler_params=_params(("parallel",)),
    )(parts, w, m, v, *(earlier or ()))


def _column_slots(full):
    K, N = full.shape
    return jnp.transpose(full.reshape(K, N_DEV, N // N_DEV), (1, 0, 2))


def _from_column_slots(slots):
    _, K, n = slots.shape
    return jnp.transpose(slots, (1, 0, 2)).reshape(K, N_DEV * n)


_SMALL =("mix_norm_g", "mlp_norm_g", "final_norm_g", "a_ln_g", "a_ln_b", "a_w_s", "a_b_s", "rel_bias")


def _pack_small(vals):
    pieces = []
    for n in _SMALL:
        flat = vals[n].reshape(-1)
        pad = (-flat.shape[0]) % (8 * LANES)
        pieces.append(jnp.pad(flat, (0, pad)).reshape(-1, LANES))
    rows = sum(p.shape[0] for p in pieces)
    tail = (-rows) % (8 * N_DEV)
    if tail:
        pieces.append(jnp.zeros((tail, LANES), F32))
    return jnp.concatenate(pieces, axis=0)


def _unpack_small(buf, like):
    out = {}
    r = 0
    for n in _SMALL:
        size = like[n].size
        nrows = -(-size // (8 * LANES)) * 8
        out[n] = buf[r:r + nrows].reshape(-1)[:size].reshape(like[n].shape)
        r += nrows
    return out


_STAGES = (("gate", ("a_w_in", "a_w_out"), 0),
           ("mlp0", ("w_up", "w_down"), 0),
           ("att", ("b_w_qkv", "b_w_out"), 0),
           ("mlp1", ("w_up", "w_down"), 1))


def kernel(x, mix_norm_g, mlp_norm_g, final_norm_g, a_w_in, a_ln_g, a_ln_b, a_w_s, a_b_s, a_w_out, b_w_qkv, b_w_out, rel_bias, w_up, w_down, loss_target, m_mix_norm_g, m_mlp_norm_g, m_final_norm_g, m_a_w_in, m_a_ln_g, m_a_ln_b, m_a_w_s, m_a_b_s, m_a_w_out, m_b_w_qkv, m_b_w_out, m_rel_bias, m_w_up, m_w_down, v_mix_norm_g, v_mlp_norm_g, v_final_norm_g, v_a_w_in, v_a_ln_g, v_a_ln_b, v_a_w_s, v_a_b_s, v_a_w_out, v_b_w_qkv, v_b_w_out, v_rel_bias, v_w_up, v_w_down):
    w = dict(mix_norm_g=mix_norm_g, mlp_norm_g=mlp_norm_g, final_norm_g=final_norm_g, a_w_in=a_w_in, a_ln_g=a_ln_g,
             a_ln_b=a_ln_b, a_w_s=a_w_s, a_b_s=a_b_s, a_w_out=a_w_out, b_w_qkv=b_w_qkv, b_w_out=b_w_out,
             rel_bias=rel_bias, w_up=w_up, w_down=w_down)
    m = dict(mix_norm_g=m_mix_norm_g, mlp_norm_g=m_mlp_norm_g, final_norm_g=m_final_norm_g, a_w_in=m_a_w_in,
             a_ln_g=m_a_ln_g, a_ln_b=m_a_ln_b, a_w_s=m_a_w_s, a_b_s=m_a_b_s, a_w_out=m_a_w_out, b_w_qkv=m_b_w_qkv,
             b_w_out=m_b_w_out, rel_bias=m_rel_bias, w_up=m_w_up, w_down=m_w_down)
    v = dict(mix_norm_g=v_mix_norm_g, mlp_norm_g=v_mlp_norm_g, final_norm_g=v_final_norm_g, a_w_in=v_a_w_in,
             a_ln_g=v_a_ln_g, a_ln_b=v_a_ln_b, a_w_s=v_a_w_s, a_b_s=v_a_b_s, a_w_out=v_a_w_out, b_w_qkv=v_b_w_qkv,
             b_w_out=v_b_w_out, rel_bias=v_rel_bias, w_up=v_w_up, w_down=v_w_down)

    stages = {s: (names, layer) for s, names, layer in _STAGES}
    order = [s for s, _, _ in _STAGES]

    def shards_of(stage):
        names, layer = stages[stage]
        return [_bf(w[n][layer]) for n in names]

    pending = {}
    pending[order[0]], first_token = _exchange_start("gather_" + order[0] + "_start", shards_of(order[0]), False)

    def get_weights(stage, dep):
        gathered = _exchange_wait("gather_" + stage + "_wait", pending.pop(stage), dep)
        nxt = order.index(stage) + 1
        token = None
        if nxt < len(order):
            shards, gathered = lax.optimization_barrier((shards_of(order[nxt]), gathered))
            pending[order[nxt]], token = _exchange_start("gather_" + order[nxt] + "_start", shards, False)
        return gathered, token

    sent = {}

    def put_grads(stage, slot_grads):
        sent[stage], token = _exchange_start("scatter_" + stage + "_start", slot_grads, True)
        return token

    loss_local, grad_x, small_g = _local_step(
        x[0], loss_target[0], mix_norm_g, mlp_norm_g, final_norm_g, a_ln_g, a_ln_b, a_w_s, a_b_s, rel_bias,
        get_weights, put_grads, first_token)

    results = {}
    for stage in reversed(order):
        names, layer = stages[stage]
        received = _exchange_wait("scatter_" + stage + "_wait", sent[stage], grad_x)
        for n, parts in zip(names, received):
            results[n] = _adamw_shard("adamw_%s_%s" % (stage, n), parts, w[n], m[n], v[n], layer, results.get(n))

    reduced = _all_reduce_small("reduce_small", _pack_small(small_g))
    small = [_unpack_small(b, w) for b in _adamw("adamw_small", reduced[None], _pack_small(w), _pack_small(m),
                                                 _pack_small(v), tr=reduced.shape[0])]

    outs = []
    for j in range(4):
        outs.extend(small[j][n] if n in _SMALL else results[n][j] for n in w)
    loss = lax.psum(loss_local, ("x", "y", "c"))
    return (loss, grad_x[None], *outs)


def _local_step(xs, tgt, mix_norm_g, mlp_norm_g, final_norm_g, a_ln_g, a_ln_b, a_w_s, a_b_s, rel_bias,
                get_weights, put_grads, first_token=None):
    D = xs.shape[-1]
    g_mix = [mix_norm_g[l][None, :] for l in range(2)]
    g_mlp = [mlp_norm_g[l][None, :] for l in range(2)]
    g_fin = final_norm_g[None, :]
    ln_g, ln_b = a_ln_g, a_ln_b
    causal = jnp.tril(jnp.ones((CHUNK, CHUNK), dtype=bool))
    wm = _bf(jnp.where(causal[None], a_w_s[0], 0.0))
    bs_full = jnp.repeat(a_b_s[0].T, D // GROUPS, axis=1)
    bias_tiles = _bias_tiles("att_bias", rel_bias, after=first_token)

    y0 = _rms_fwd("rms_mix0", xs, g_mix[0], after=bias_tiles)
    (win, wout), token = get_weights("gate", y0)
    wout = wout.reshape(-1, D)
    uvp = _mm_nn("gate_in", y0, win, tm=512, nc=win.shape[2], shards=True, after=token)
    z = _gate_fwd("gate_mid", uvp, ln_g, ln_b, wm, bs_full)
    h1, y1 = _mm_nn("gate_out", z, wout, tm=512, nc=512, epi="res", extra=xs, norm_g=g_mlp[0])
    (wup0, wdn0), token = get_weights("mlp0", h1)
    wdn0 = wdn0.reshape(-1, D)
    a0, f0 = _mm_nn("mlp0_up", y1, wup0, tm=512, nc=wup0.shape[2], epi="relu2", shards=True, after=token)
    h2, y2 = _mm_nn("mlp0_down", f0, wdn0, tm=512, nc=512, epi="res", extra=h1, norm_g=g_mix[1])
    (wqkv, wo), token = get_weights("att", h2)
    wqkv, wo = _from_column_slots(wqkv), _from_column_slots(wo)
    qkv = _mm_nn("att_qkv", y2, wqkv, tm=256, nc=512, after=token)
    o_att, lse = _att_merge("att_merge", *_att_fwd("att_fwd", qkv, bias_tiles))
    h3, y3 = _mm_nn("att_out", o_att, wo, tm=512, nc=512, epi="res", extra=h2, norm_g=g_mlp[1])
    (wup1, wdn1), _ = get_weights("mlp1", h3)
    wdn1 = wdn1.reshape(-1, D)
    a1, f1 = _mm_nn("mlp1_up", y3, wup1, tm=512, nc=wup1.shape[2], epi="relu2", shards=True)
    dh, dg_fin, err2 = _mm_res_loss("mlp1_down_loss", f1, wdn1, h3, g_fin, tgt, tm=512, nc=512)
    loss_local = 0.5 * jnp.sum(err2) / D

    def mlp_bwd(tag, dh, h_in, y, a, f, wup_l, wdn_l, g_row, after):
        da = _mm_nt(tag + "_dact", dh, wdn_l, tm=512, nc=512, epi="mask2relu", extra=a, after=after)
        g_dn = _mm_tn(tag + "_dwdown", f, dh, t1=1024, tn=1024)
        g_up = _mm_tn(tag + "_dwup", y, da, t1=1024, tn=1024, slot_cols=wup_l.shape[2])
        dh_in, dg = _mm_nt_rms_bwd(tag + "_dy", [(da, wup_l, *_whole(wup_l))], h_in, g_row, dh, tm=512, nc=512,
                                   shards=True)
        return dh_in, dg, put_grads(tag, [g_up, g_dn.reshape(N_DEV, -1, D)])

    dh3, dg_mlp1, token = mlp_bwd("mlp1", dh, h3, y3, a1, f1, wup1, wdn1, g_mlp[1], None)

    d_o = _mm_nt("att_dout", dh3, wo, tm=512, nc=512, after=token)
    g_wo = _mm_tn("att_dwo", o_att, dh3, t1=512, tn=1024)
    dq, dk, dv, ds_sums = _att_bwd("att_bwd", qkv, o_att, lse, d_o, bias_tiles)
    part_w = N_DIL * ATT_WIDTH
    g_qkv = [_mm_tn("att_dwqkv%d" % p, y2, t, t1=1024, tn=part_w) for p, t in enumerate((dq, dk, dv))]
    dh2, dg_mix1 = _mm_nt_rms_bwd("att_dy", [(t, wqkv, (D, part_w), (0, p)) for p, t in enumerate((dq, dk, dv))],
                                  h2, g_mix[1], dh3, tm=512, nc=512)
    token = put_grads("att", [_column_slots(jnp.concatenate(g_qkv, axis=1)), _column_slots(g_wo)])

    dh1, dg_mlp0, token = mlp_bwd("mlp0", dh2, h1, y1, a0, f0, wup0, wdn0, g_mlp[0], token)

    dz = _mm_nt("gate_dz", dh1, wout, tm=512, nc=512, after=token)
    g_wout = _mm_tn("gate_dwout", z, dh1, t1=1024, tn=1024)
    duvp, d_wm, d_mixed, d_lng, d_lnb = _gate_bwd("gate_dmid", uvp, dz, ln_g, ln_b, wm, bs_full)
    g_win = _mm_tn("gate_dwin", y0, duvp, t1=1024, tn=1024, slot_cols=win.shape[2])
    token = put_grads("gate", [g_win, g_wout.reshape(N_DEV, -1, D)])
    grad_x, dg_mix0 = _mm_nt_rms_bwd("gate_dy", [(duvp, win, *_whole(win))], xs, g_mix[0], dh1, tm=512, nc=512,
                                     after=token, shards=True)

    small_g = dict(
        mix_norm_g=jnp.concatenate([dg_mix0, dg_mix1], axis=0),
        mlp_norm_g=jnp.concatenate([dg_mlp0, dg_mlp1], axis=0),
        final_norm_g=dg_fin[0], a_ln_g=d_lng, a_ln_b=d_lnb, a_w_s=d_wm[None],
        a_b_s=jnp.sum(d_mixed.reshape(CHUNK, GROUPS, D // GROUPS), axis=2).T[None],
        rel_bias=_bias_grad("att_dbias", ds_sums))
    return loss_local, grad_x, small_g
```

```python
import functools
import math

import jax
import jax.numpy as jnp
from jax import lax
from jax.experimental import pallas as pl
from jax.experimental.pallas import tpu as pltpu

F32 = jnp.float32
BF16 = jnp.bfloat16
MESH = pl.DeviceIdType.MESH

N_DEV = 8
EPS = 1e-6
NEG_INF = -1e30
CHUNK = 128
GROUPS = 8
HEAD_DIM = 64
ATT_HEADS = 8
ATT_WIDTH = ATT_HEADS * HEAD_DIM
DILATIONS = (1, 4, 16)
N_DIL = len(DILATIONS)
N_BUCKETS = 32
MAX_EXACT = N_BUCKETS // 2
REL_MAX_DISTANCE = 2048
ATT_ROWS = 2048
ATT_SCALE = HEAD_DIM ** -0.5
LANES = 128

ADAM_LR = 0.001
ADAM_B1 = 0.9
ADAM_B2 = 0.999
ADAM_EPS = 1e-08
ADAM_WD = 0.01
ADAM_STEP = 10

VMEM_LIMIT_BYTES = 56 * 1024 * 1024


def _params(semantics=None):
    return pltpu.CompilerParams(dimension_semantics=semantics, vmem_limit_bytes=VMEM_LIMIT_BYTES)


def _bf(v):
    return v.astype(BF16)


def _dot(a, b, dims):
    return lax.dot_general(a, b, (dims, ((), ())), preferred_element_type=F32)


NN = ((1,), (0,))
NT = ((1,), (1,))
TN = ((0,), (0,))


def _after_operand(after):
    if after is None:
        return [], []
    return [after], [pl.BlockSpec(memory_space=pl.ANY)]


def _rms_fwd(name, x, g, tm=512, after=None):
    S, D = x.shape
    after_args, after_specs = _after_operand(after)

    def body(x_ref, g_ref, *rest):
        y_ref = rest[-1]
        xv = x_ref[...]
        r = lax.rsqrt(jnp.mean(xv * xv, axis=-1, keepdims=True) + EPS)
        y_ref[...] = _bf(xv * r * g_ref[...])

    return pl.pallas_call(
        body, name=name, grid=(S // tm,),
        in_specs=[pl.BlockSpec((tm, D), lambda i: (i, 0)), pl.BlockSpec((1, D), lambda i: (0, 0))] + after_specs,
        out_specs=pl.BlockSpec((tm, D), lambda i: (i, 0)),
        out_shape=jax.ShapeDtypeStruct((S, D), BF16),
        compiler_params=_params(("parallel",)),
    )(x, g, *after_args)


def _mm_res_loss(name, a, w, res, g, target, *, tm, nc):
    M, D = res.shape

    def body(a_ref, w_ref, r_ref, g_ref, t_ref, dh_ref, dg_ref, l_ref, h_sc):
        i = pl.program_id(0)
        gv = g_ref[...]
        dg_part = l_part = None
        for rows in _row_halves(tm):
            a_v = _bf(a_ref[rows, :])
            for j in range(D // nc):
                cols, acc = _chunk_product([a_v], [w_ref], j, nc, False, False)
                h_sc[rows, cols] = r_ref[rows, cols] + acc
            xv = h_sc[rows, :]
            r = lax.rsqrt(jnp.mean(xv * xv, axis=-1, keepdims=True) + EPS)
            xh = xv * r
            e = xh * gv - t_ref[rows, :]
            dout = e / D
            dyg = dout * gv
            c = jnp.mean(dyg * xh, axis=-1, keepdims=True)
            dh_ref[rows, :] = r * (dyg - xh * c)
            dg_p = jnp.sum(dout * xh, axis=0, keepdims=True)
            l_p = jnp.sum(e * e, axis=0, keepdims=True)
            dg_part = dg_p if dg_part is None else dg_part + dg_p
            l_part = l_p if l_part is None else l_part + l_p

        @pl.when(i == 0)
        def _():
            dg_ref[...] = dg_part
            l_ref[...] = l_part

        @pl.when(i > 0)
        def _():
            dg_ref[...] += dg_part
            l_ref[...] += l_part

    row = pl.BlockSpec((tm, D), lambda i: (i, 0))
    vec = pl.BlockSpec((1, D), lambda i: (0, 0))
    return pl.pallas_call(
        body, name=name, grid=(M // tm,),
        in_specs=[pl.BlockSpec((tm, a.shape[1]), lambda i: (i, 0)), pl.BlockSpec(w.shape, lambda i: (0, 0)),
                  row, vec, row],
        out_specs=[row, vec, vec],
        out_shape=[jax.ShapeDtypeStruct((M, D), F32), jax.ShapeDtypeStruct((1, D), F32),
                   jax.ShapeDtypeStruct((1, D), F32)],
        scratch_shapes=[pltpu.VMEM((tm, D), F32)],
        compiler_params=_params(("arbitrary",)),
    )(a, w, res, g, target)


def _row_halves(tm):
    return [slice(0, tm // 2), slice(tm // 2, tm)]


def _chunk_product(a_vals, w_refs, j, nc, nt, shards):
    cols = slice(j * nc, (j + 1) * nc)
    acc = None
    for a_v, w_ref in zip(a_vals, w_refs):
        if not shards:
            terms = [_dot(a_v, w_ref[cols, :], NT) if nt else _dot(a_v, w_ref[:, cols], NN)]
        elif nt:
            nl = w_ref.shape[2]
            terms = [_dot(a_v[:, k * nl:(k + 1) * nl], w_ref[k, cols, :], NT) for k in range(N_DEV)]
        else:
            terms = [_dot(a_v, w_ref[j], NN)]
        for t in terms:
            acc = t if acc is None else acc + t
    return cols, acc


def _mm_rows(name, pairs, n_out, *, nt, tm, nc, epi="plain", extra=None, out_dtype=F32, after=None, shards=False,
             norm_g=None):
    M = pairs[0][0].shape[0]
    np_ = len(pairs)
    after_args, after_specs = _after_operand(after)

    def body(*refs):
        a_refs = refs[:np_]
        w_refs = refs[np_:2 * np_]
        pos = 2 * np_
        e_ref = None
        if extra is not None:
            e_ref = refs[pos]
            pos += 1
        if norm_g is not None:
            g_ref = refs[pos]
            pos += 1
        pos += len(after_args)
        outs = refs[pos:]
        for rows in (_row_halves(tm) if norm_g is not None else [slice(0, tm)]):
            a_vals = [_bf(a[rows, :]) for a in a_refs]
            for j in range(n_out // nc):
                cols, acc = _chunk_product(a_vals, w_refs, j, nc, nt, shards)
                if epi == "plain":
                    outs[0][rows, cols] = acc.astype(out_dtype)
                elif epi == "res":
                    outs[0][rows, cols] = e_ref[rows, cols] + acc
                elif epi == "relu2":
                    outs[0][rows, cols] = _bf(acc)
                    rl = jnp.maximum(acc, 0.0)
                    outs[1][rows, cols] = _bf(rl * rl)
                elif epi == "mask2relu":
                    outs[0][rows, cols] = _bf(acc * (2.0 * jnp.maximum(e_ref[rows, cols].astype(F32), 0.0)))
            if norm_g is not None:
                hv = outs[0][rows, :]
                r = lax.rsqrt(jnp.mean(hv * hv, axis=-1, keepdims=True) + EPS)
                outs[1][rows, :] = _bf(hv * r * g_ref[...])

    in_specs = [pl.BlockSpec((tm, a.shape[1]), lambda i: (i, 0)) for a, _, _, _ in pairs]
    for _, _, wshape, widx in pairs:
        in_specs.append(pl.BlockSpec(wshape, functools.partial(lambda i, widx: widx, widx=widx)))
    args = [a for a, _, _, _ in pairs] + [w for _, w, _, _ in pairs]
    if extra is not None:
        in_specs.append(pl.BlockSpec((tm, n_out), lambda i: (i, 0)))
        args.append(extra)
    if norm_g is not None:
        in_specs.append(pl.BlockSpec((1, n_out), lambda i: (0, 0)))
        args.append(norm_g)
    in_specs += after_specs
    args += after_args
    row_out = pl.BlockSpec((tm, n_out), lambda i: (i, 0))
    if epi == "relu2":
        out_specs = [row_out, row_out]
        out_shape = [jax.ShapeDtypeStruct((M, n_out), BF16), jax.ShapeDtypeStruct((M, n_out), BF16)]
    elif norm_g is not None:
        out_specs = [row_out, row_out]
        out_shape = [jax.ShapeDtypeStruct((M, n_out), F32), jax.ShapeDtypeStruct((M, n_out), BF16)]
    else:
        dt = BF16 if epi == "mask2relu" else (F32 if epi == "res" else out_dtype)
        out_specs = row_out
        out_shape = jax.ShapeDtypeStruct((M, n_out), dt)
    return pl.pallas_call(
        body, name=name, grid=(M // tm,), in_specs=in_specs, out_specs=out_specs, out_shape=out_shape,
        compiler_params=_params(("parallel",)),
    )(*args)


def _whole(w):
    return w.shape, (0,) * w.ndim


def _mm_nn(name, a, w, **kw):
    n_out = w.shape[0] * w.shape[2] if w.ndim == 3 else w.shape[1]
    return _mm_rows(name, [(a, w, *_whole(w))], n_out, nt=False, **kw)


def _mm_nt(name, a, w, **kw):
    return _mm_rows(name, [(a, w, *_whole(w))], w.shape[0], nt=True, **kw)


def _mm_nt_rms_bwd(name, pairs, x, g, dres, *, tm, nc, after=None, shards=False):
    M, D = x.shape
    np_ = len(pairs)
    after_args, after_specs = _after_operand(after)

    def body(*refs):
        a_refs = refs[:np_]
        w_refs = refs[np_:2 * np_]
        x_ref, g_ref, r_ref = refs[2 * np_:2 * np_ + 3]
        dx_ref, dg_ref, dy_sc = refs[-3:]
        i = pl.program_id(0)
        part = None
        for rows in _row_halves(tm):
            a_vals = [_bf(a[rows, :]) for a in a_refs]
            for j in range(D // nc):
                cols, acc = _chunk_product(a_vals, w_refs, j, nc, True, shards)
                dy_sc[rows, cols] = acc
            xv = x_ref[rows, :]
            r = lax.rsqrt(jnp.mean(xv * xv, axis=-1, keepdims=True) + EPS)
            xh = xv * r
            dy_v = dy_sc[rows, :]
            dyg = dy_v * g_ref[...]
            c = jnp.mean(dyg * xh, axis=-1, keepdims=True)
            dx_ref[rows, :] = r_ref[rows, :] + r * (dyg - xh * c)
            p = jnp.sum(dy_v * xh, axis=0, keepdims=True)
            part = p if part is None else part + p

        @pl.when(i == 0)
        def _():
            dg_ref[...] = part

        @pl.when(i > 0)
        def _():
            dg_ref[...] += part

    row = pl.BlockSpec((tm, D), lambda i: (i, 0))
    vec = pl.BlockSpec((1, D), lambda i: (0, 0))
    in_specs = [pl.BlockSpec((tm, a.shape[1]), lambda i: (i, 0)) for a, _, _, _ in pairs]
    for _, _, wshape, widx in pairs:
        in_specs.append(pl.BlockSpec(wshape, functools.partial(lambda i, widx: widx, widx=widx)))
    args = [a for a, _, _, _ in pairs] + [w for _, w, _, _ in pairs]
    return pl.pallas_call(
        body, name=name, grid=(M // tm,),
        in_specs=in_specs + [row, vec, row] + after_specs,
        out_specs=[row, vec],
        out_shape=[jax.ShapeDtypeStruct((M, D), F32), jax.ShapeDtypeStruct((1, D), F32)],
        scratch_shapes=[pltpu.VMEM((tm, D), F32)],
        compiler_params=_params(("arbitrary",)),
    )(*args, x, g, dres, *after_args)


def _mm_tn(name, a, b, *, t1, tn, tm=2048, slot_cols=None):
    M, K1 = a.shape
    N = b.shape[1]
    nm = M // tm

    def body(a_ref, b_ref, o_ref, acc_ref):
        m = pl.program_id(2)
        t = _dot(_bf(a_ref[...]), _bf(b_ref[...]), TN)

        @pl.when(m == 0)
        def _():
            acc_ref[...] = t

        @pl.when(m > 0)
        def _():
            acc_ref[...] += t

        @pl.when(m == nm - 1)
        def _():
            if slot_cols is None:
                o_ref[...] = _bf(acc_ref[...])
            else:
                for k in range(tn // slot_cols):
                    o_ref[k] = _bf(acc_ref[:, k * slot_cols:(k + 1) * slot_cols])

    if slot_cols is not None:
        out_spec = pl.BlockSpec((tn // slot_cols, t1, slot_cols), lambda i, j, m: (j, i, 0))
        out_shape = jax.ShapeDtypeStruct((N // slot_cols, K1, slot_cols), BF16)
    else:
        out_spec = pl.BlockSpec((t1, tn), lambda i, j, m: (i, j))
        out_shape = jax.ShapeDtypeStruct((K1, N), BF16)
    return pl.pallas_call(
        body, name=name, grid=(K1 // t1, N // tn, nm),
        in_specs=[pl.BlockSpec((tm, t1), lambda i, j, m: (m, i)), pl.BlockSpec((tm, tn), lambda i, j, m: (m, j))],
        out_specs=out_spec, out_shape=out_shape,
        scratch_shapes=[pltpu.VMEM((t1, tn), F32)],
        compiler_params=_params(("parallel", "parallel", "arbitrary")),
    )(a, b)


_INV_SQRT2 = 1.0 / math.sqrt(2.0)
_INV_SQRT2PI = 1.0 / math.sqrt(2.0 * math.pi)


def _gelu(x):
    return 0.5 * x * (1.0 + lax.erf(x * _INV_SQRT2))


def _gelu_and_grad(x):
    cdf = 0.5 * (1.0 + lax.erf(x * _INV_SQRT2))
    return x * cdf, cdf + x * (_INV_SQRT2PI * jnp.exp(-0.5 * x * x))


def _layer_norm_parts(v):
    mu = jnp.mean(v, axis=-1, keepdims=True)
    xc = v - mu
    rs = lax.rsqrt(jnp.mean(xc * xc, axis=-1, keepdims=True) + EPS)
    return xc * rs, rs


def _gate_fwd(name, uvp, ln_g, ln_b, wm, bs_full, tr=512):
    S, W2 = uvp.shape
    W = W2 // 2
    gd = W // GROUPS

    def body(u_ref, v_ref, lg_ref, lb_ref, wm_ref, bs_ref, z_ref):
        vh, _ = _layer_norm_parts(_gelu(v_ref[...]))
        vn = _bf(vh * lg_ref[...] + lb_ref[...])
        for ci in range(tr // CHUNK):
            rows = slice(ci * CHUNK, (ci + 1) * CHUNK)
            for g in range(GROUPS):
                cols = slice(g * gd, (g + 1) * gd)
                mixed = _dot(wm_ref[g], vn[rows, cols], NN) + bs_ref[:, cols]
                z_ref[rows, cols] = _bf(_gelu(u_ref[rows, cols]) * mixed)

    vec = pl.BlockSpec((1, W), lambda i: (0, 0))
    return pl.pallas_call(
        body, name=name, grid=(S // tr,),
        in_specs=[pl.BlockSpec((tr, W), lambda i: (i, 0)), pl.BlockSpec((tr, W), lambda i: (i, 1)), vec, vec,
                  pl.BlockSpec((GROUPS, CHUNK, CHUNK), lambda i: (0, 0, 0)),
                  pl.BlockSpec((CHUNK, W), lambda i: (0, 0))],
        out_specs=pl.BlockSpec((tr, W), lambda i: (i, 0)),
        out_shape=jax.ShapeDtypeStruct((S, W), BF16),
        compiler_params=_params(("parallel",)),
    )(uvp, uvp, ln_g, ln_b, wm, bs_full)


def _gate_bwd(name, uvp, dz, ln_g, ln_b, wm, bs_full, tr=256):
    S, W2 = uvp.shape
    W = W2 // 2
    gd = W // GROUPS
    n_steps = S // tr

    def body(u_ref, v_ref, dz_ref, lg_ref, lb_ref, wm_ref, bs_ref, duv_ref, dwm_ref, dmx_ref, dlg_ref, dlb_ref,
             dvn_ref):
        i = pl.program_id(0)
        v, dv_dvp = _gelu_and_grad(v_ref[...])
        vh, rs = _layer_norm_parts(v)
        lg = lg_ref[...]
        vn = _bf(vh * lg + lb_ref[...])

        @pl.when(i == 0)
        def _():
            dwm_ref[...] = jnp.zeros_like(dwm_ref)
            dmx_ref[...] = jnp.zeros_like(dmx_ref)
            dlg_ref[...] = jnp.zeros_like(dlg_ref)
            dlb_ref[...] = jnp.zeros_like(dlb_ref)

        for ci in range(tr // CHUNK):
            rows = slice(ci * CHUNK, (ci + 1) * CHUNK)
            for g in range(GROUPS):
                cols = slice(g * gd, (g + 1) * gd)
                u, du_dup = _gelu_and_grad(u_ref[rows, cols])
                dz_v = dz_ref[rows, cols]
                dmixed = dz_v * u
                dmx_ref[:, cols] += dmixed
                dmixed_b = _bf(dmixed)
                mixed = _dot(wm_ref[g], vn[rows, cols], NN) + bs_ref[:, cols]
                duv_ref[rows, cols] = _bf(dz_v * mixed * du_dup)
                dwm_ref[g] += _dot(dmixed_b, vn[rows, cols], NT)
                dvn_ref[rows, cols] = _dot(wm_ref[g], dmixed_b, TN)
        dvn = dvn_ref[...]
        dlg_ref[...] += jnp.sum(dvn * vh, axis=0, keepdims=True)
        dlb_ref[...] += jnp.sum(dvn, axis=0, keepdims=True)
        dvh = dvn * lg
        dv = rs * (dvh - jnp.mean(dvh, axis=-1, keepdims=True) - vh * jnp.mean(dvh * vh, axis=-1, keepdims=True))
        duv_ref[:, W:] = _bf(dv * dv_dvp)

        @pl.when(i == n_steps - 1)
        def _():
            t_idx = lax.broadcasted_iota(jnp.int32, (CHUNK, CHUNK), 0)
            s_idx = lax.broadcasted_iota(jnp.int32, (CHUNK, CHUNK), 1)
            keep = (s_idx <= t_idx).astype(F32)
            for g in range(GROUPS):
                dwm_ref[g] = dwm_ref[g] * keep

    vec = pl.BlockSpec((1, W), lambda i: (0, 0))
    row = pl.BlockSpec((tr, W), lambda i: (i, 0))
    return pl.pallas_call(
        body, name=name, grid=(n_steps,),
        in_specs=[row, pl.BlockSpec((tr, W), lambda i: (i, 1)), row, vec, vec,
                  pl.BlockSpec((GROUPS, CHUNK, CHUNK), lambda i: (0, 0, 0)),
                  pl.BlockSpec((CHUNK, W), lambda i: (0, 0))],
        out_specs=[pl.BlockSpec((tr, W2), lambda i: (i, 0)),
                   pl.BlockSpec((GROUPS, CHUNK, CHUNK), lambda i: (0, 0, 0)),
                   pl.BlockSpec((CHUNK, W), lambda i: (0, 0)), vec, vec],
        out_shape=[jax.ShapeDtypeStruct((S, W2), BF16), jax.ShapeDtypeStruct((GROUPS, CHUNK, CHUNK), F32),
                   jax.ShapeDtypeStruct((CHUNK, W), F32), jax.ShapeDtypeStruct((1, W), F32),
                   jax.ShapeDtypeStruct((1, W), F32)],
        scratch_shapes=[pltpu.VMEM((tr, W), F32)],
        compiler_params=_params(("arbitrary",)),
    )(uvp, uvp, dz, ln_g, ln_b, wm, bs_full)


def _t5_bucket(distance):
    small = distance < MAX_EXACT
    nf = jnp.maximum(distance, 1).astype(F32)
    large = MAX_EXACT + (jnp.log(nf / MAX_EXACT) / math.log(REL_MAX_DISTANCE / MAX_EXACT)
                         * (N_BUCKETS - MAX_EXACT)).astype(jnp.int32)
    large = jnp.minimum(large, N_BUCKETS - 1)
    return jnp.where(small, distance, large)


TILE_ELEMS = 2 * CHUNK * CHUNK


def _band_buckets():
    rel = CHUNK + jnp.arange(CHUNK)[None, :] - jnp.arange(2 * CHUNK)[:, None]
    band = (rel >= 0) & (rel <= CHUNK)
    buckets = [_t5_bucket(jnp.clip(rel, 0, CHUNK) * d) for d in DILATIONS]
    return jnp.stack(buckets), band


def _bucket_onehot():
    buckets, _ = _band_buckets()
    return (buckets.reshape(N_DIL, 1, TILE_ELEMS) == jnp.arange(N_BUCKETS)[None, :, None]).astype(F32)


def _bias_tiles(name, rel_bias, after=None):
    _, band = _band_buckets()
    own = band & (jnp.arange(2 * CHUNK) >= CHUNK)[:, None]
    masks = jnp.stack([own, band]).reshape(2, TILE_ELEMS).astype(F32)
    tables = jnp.transpose(rel_bias.reshape(N_BUCKETS, N_DIL, ATT_HEADS), (1, 2, 0))
    after_args, after_specs = _after_operand(after)

    def body(t_ref, oh_ref, m_ref, *rest):
        out_ref = rest[-1]
        for g in range(N_DIL):
            bias = lax.dot_general(t_ref[g], oh_ref[g], (NN, ((), ())), precision=lax.Precision.HIGHEST,
                                   preferred_element_type=F32)
            for f in range(2):
                out_ref[g, f] = jnp.where(m_ref[f:f + 1, :] > 0.5, bias, NEG_INF)

    whole = pl.BlockSpec(memory_space=pltpu.VMEM)
    out = pl.pallas_call(
        body, name=name, out_shape=jax.ShapeDtypeStruct((N_DIL, 2, ATT_HEADS, TILE_ELEMS), F32),
        in_specs=[whole, whole, whole] + after_specs, out_specs=whole,
        compiler_params=_params(),
    )(tables, _bucket_onehot(), masks, *after_args)
    out = out.reshape(N_DIL, 2, ATT_HEADS // 2, 2, 2 * CHUNK, CHUNK)
    return jnp.transpose(out, (0, 1, 2, 4, 3, 5)).reshape(N_DIL, 2, ATT_HEADS // 2, 2 * CHUNK, 2 * CHUNK)


def _att_specs(order):
    def column(part, ids):
        hp, g, _ = order(*ids)
        return part * 3 * 4 + g * 4 + hp

    def window(part):
        def index(*ids):
            c = order(*ids)[2]
            return pl.multiple_of(jnp.maximum(c - 1, 0) * ATT_ROWS, ATT_ROWS), column(part, ids) * LANES
        return pl.BlockSpec((pl.Element(2 * ATT_ROWS), pl.Element(LANES)), index)

    return [pl.BlockSpec((ATT_ROWS, LANES), lambda *ids: (order(*ids)[2], column(0, ids))), window(1), window(2)]


def _window_base(c):
    return jnp.where(c == 0, 0, ATT_ROWS)


def _rows(start, d):
    if d == 1:
        return pl.ds(pl.multiple_of(start, CHUNK), CHUNK)
    return pl.ds(start, CHUNK, stride=d)


def _att_tile_offsets(t, d):
    n = t // d
    r = t % d
    return n * (CHUNK * d) + r, n


def _head_pair_columns(x_t):
    zeros = jnp.zeros((HEAD_DIM, CHUNK), x_t.dtype)
    return jnp.concatenate([jnp.concatenate([x_t[:HEAD_DIM], zeros], axis=0),
                            jnp.concatenate([zeros, x_t[HEAD_DIM:]], axis=0)], axis=1)


def _head_pair_rows(y):
    return jnp.concatenate([y[:HEAD_DIM, :CHUNK], y[HEAD_DIM:, CHUNK:]], axis=0)


def _att_fwd(name, qkv, bias_tiles):
    S = qkv.shape[0]
    n_chunks = S // ATT_ROWS
    tiles = ATT_ROWS // CHUNK

    def body(q_ref, kk, vv, b_ref, o_ref, l_ref):
        c = pl.program_id(1)
        g = pl.program_id(2)
        base = _window_base(c)

        for gi, d in enumerate(DILATIONS):
            @pl.when(g == gi)
            def _(d=d):
                span = CHUNK * d

                def tile(t, carry):
                    q0, n = _att_tile_offsets(t, d)
                    first = (c == 0) & (n == 0)
                    rows = _rows(q0, d)
                    cur = _rows(base + q0, d)
                    prev = _rows(jnp.where(first, q0, base + q0 - span), d)
                    inner = jnp.where(first, 0, 1)
                    qq = _head_pair_columns(_bf(q_ref[rows, :] * ATT_SCALE).T)
                    s_p = _dot(_bf(kk[prev, :]), qq, NN) + b_ref[inner, 0:CHUNK, :]
                    s_c = _dot(_bf(kk[cur, :]), qq, NN) + b_ref[inner, CHUNK:2 * CHUNK, :]
                    m = jnp.maximum(jnp.max(s_p, axis=0, keepdims=True), jnp.max(s_c, axis=0, keepdims=True))
                    p_p = jnp.exp(s_p - m)
                    p_c = jnp.exp(s_c - m)
                    l = jnp.sum(p_p, axis=0, keepdims=True) + jnp.sum(p_c, axis=0, keepdims=True)
                    o2 = (_dot(_bf(vv[prev, :]).T, _bf(p_p), NN)
                          + _dot(_bf(vv[cur, :]).T, _bf(p_c), NN)) * (1.0 / l)
                    lse = m + jnp.log(l)
                    l_t = jnp.concatenate([jnp.broadcast_to(lse[:, :CHUNK], (HEAD_DIM, CHUNK)),
                                           jnp.broadcast_to(lse[:, CHUNK:], (HEAD_DIM, CHUNK))], axis=0)
                    o_ref[rows, :] = _head_pair_rows(o2).T
                    l_ref[rows, :] = l_t.T
                    return carry

                lax.fori_loop(0, tiles, tile, 0, unroll=8)

    order = lambda hp, c, g: (hp, g, c)
    out_spec = pl.BlockSpec((None, ATT_ROWS, LANES), lambda hp, c, g: (g, c, hp))
    shape = jax.ShapeDtypeStruct((N_DIL, S, ATT_WIDTH), F32)
    return pl.pallas_call(
        body, name=name, grid=(ATT_HEADS // 2, n_chunks, N_DIL),
        in_specs=_att_specs(order) + [
            pl.BlockSpec((None, 2, None, 2 * CHUNK, 2 * CHUNK), lambda hp, c, g: (g, 0, hp, 0, 0))],
        out_specs=[out_spec, out_spec],
        out_shape=[shape, shape],
        compiler_params=_params(("parallel", "parallel", "parallel")),
    )(qkv, qkv, qkv, bias_tiles)


def _att_merge(name, o_g, l_g, tm=512):
    _, S, W = o_g.shape

    def body(o_ref, l_ref, out_ref, lse_ref):
        ls = [l_ref[g] for g in range(N_DIL)]
        mx = functools.reduce(jnp.maximum, ls)
        ws = [jnp.exp(l - mx) for l in ls]
        tot = functools.reduce(lambda a, b: a + b, ws)
        acc = ws[0] * o_ref[0]
        for g in range(1, N_DIL):
            acc = acc + ws[g] * o_ref[g]
        out_ref[...] = acc / tot
        lse_ref[...] = mx + jnp.log(tot)

    blk = pl.BlockSpec((N_DIL, tm, W), lambda i: (0, i, 0))
    row = pl.BlockSpec((tm, W), lambda i: (i, 0))
    shape = jax.ShapeDtypeStruct((S, W), F32)
    return pl.pallas_call(
        body, name=name, grid=(S // tm,), in_specs=[blk, blk], out_specs=[row, row], out_shape=[shape, shape],
        compiler_params=_params(("parallel",)),
    )(o_g, l_g)


def _att_bwd(name, qkv, o, lse, d_o, bias_tiles):
    S = qkv.shape[0]
    n_chunks = S // ATT_ROWS
    tiles = ATT_ROWS // CHUNK

    def body(q_ref, kk, vv, o_ref, l_ref, do_ref, b_ref, dq_out, dk_out, dv_out, ds_ref, dq_ref, dk_ref, dv_ref):
        g = pl.program_id(1)
        c = pl.program_id(2)

        @pl.when(c == 0)
        def _():
            dk_ref[...] = jnp.zeros_like(dk_ref)
            dv_ref[...] = jnp.zeros_like(dv_ref)
            ds_ref[...] = jnp.zeros_like(ds_ref)

        base = _window_base(c)
        first_row = c * ATT_ROWS
        head0 = lax.broadcasted_iota(jnp.int32, (CHUNK, LANES), 1) < HEAD_DIM

        def head_pair_stack(x):
            zero = jnp.zeros_like(x)
            return jnp.concatenate([jnp.where(head0, x, zero), jnp.where(head0, zero, x)], axis=0)

        for gi, d in enumerate(DILATIONS):
            @pl.when(g == gi)
            def _(d=d):
                span = CHUNK * d

                def tile(t, carry):
                    q0, n = _att_tile_offsets(t, d)
                    first = (c == 0) & (n == 0)
                    rows = _rows(q0, d)
                    cur = _rows(base + q0, d)
                    prev = _rows(jnp.where(first, q0, base + q0 - span), d)
                    inner = jnp.where(first, 0, 1)
                    g_cur = _rows(first_row + q0, d)
                    g_prev = _rows(jnp.where(first, q0, first_row + q0 - span), d)
                    q2 = _bf(q_ref[rows, :] * ATT_SCALE)
                    q_t = q2.T
                    k2 = _bf(jnp.concatenate([kk[prev, :], kk[cur, :]], axis=0))
                    k_t = k2.T
                    v2 = _bf(jnp.concatenate([vv[prev, :], vv[cur, :]], axis=0))
                    do2 = do_ref[rows, :]
                    do_b = _bf(do2)
                    do_t = do_b.T
                    lse_t = l_ref[rows, :].T
                    dd_t = (do2 * o_ref[rows, :]).T
                    lse = jnp.concatenate([lse_t[0:1], lse_t[HEAD_DIM:HEAD_DIM + 1]], axis=1)
                    delta = jnp.concatenate([jnp.sum(dd_t[:HEAD_DIM], axis=0, keepdims=True),
                                             jnp.sum(dd_t[HEAD_DIM:], axis=0, keepdims=True)], axis=1)
                    s = _dot(k2, _head_pair_columns(q_t), NN) + b_ref[inner]
                    p = jnp.exp(s - lse)
                    ds = p * (_dot(v2, _head_pair_columns(do_t), NN) - delta)
                    ds_ref[...] += ds
                    ds_b = _bf(ds)
                    dq_t = _head_pair_rows(_dot(k_t, ds_b, NN))
                    dk2 = _dot(ds_b, head_pair_stack(q2), NN)
                    dv2 = _dot(_bf(p), head_pair_stack(do_b), NN)
                    dq_ref[rows, :] = (dq_t * ATT_SCALE).T
                    dk_ref[g_prev, :] += dk2[0:CHUNK]
                    dk_ref[g_cur, :] += dk2[CHUNK:2 * CHUNK]
                    dv_ref[g_prev, :] += dv2[0:CHUNK]
                    dv_ref[g_cur, :] += dv2[CHUNK:2 * CHUNK]
                    return carry

                lax.fori_loop(0, tiles, tile, 0, unroll=4)

        dq_out[...] = _bf(dq_ref[...])

        @pl.when(c == n_chunks - 1)
        def _():
            dk_out[...] = _bf(dk_ref[...])
            dv_out[...] = _bf(dv_ref[...])

    order = lambda hp, g, c: (hp, g, c)
    chunk = pl.BlockSpec((ATT_ROWS, LANES), lambda hp, g, c: (c, hp))
    slab = pl.BlockSpec((S, LANES), lambda hp, g, c: (0, g * 4 + hp))
    width = N_DIL * ATT_WIDTH
    dq, dk, dv, ds_sums = pl.pallas_call(
        body, name=name, grid=(ATT_HEADS // 2, N_DIL, n_chunks),
        in_specs=_att_specs(order) + [chunk, chunk, chunk,
                                      pl.BlockSpec((None, 2, None, 2 * CHUNK, 2 * CHUNK),
                                                   lambda hp, g, c: (g, 0, hp, 0, 0))],
        out_specs=[pl.BlockSpec((ATT_ROWS, LANES), lambda hp, g, c: (c, g * 4 + hp)), slab, slab,
                   pl.BlockSpec((None, None, 2 * CHUNK, 2 * CHUNK), lambda hp, g, c: (g, hp, 0, 0))],
        out_shape=[jax.ShapeDtypeStruct((S, width), BF16), jax.ShapeDtypeStruct((S, width), BF16),
                   jax.ShapeDtypeStruct((S, width), BF16),
                   jax.ShapeDtypeStruct((N_DIL, ATT_HEADS // 2, 2 * CHUNK, 2 * CHUNK), F32)],
        scratch_shapes=[pltpu.VMEM((ATT_ROWS, LANES), F32), pltpu.VMEM((S, LANES), F32),
                        pltpu.VMEM((S, LANES), F32)],
        compiler_params=_params(("parallel", "parallel", "arbitrary")),
    )(qkv, qkv, qkv, o, lse, d_o, bias_tiles)
    ds_sums = ds_sums.reshape(N_DIL, ATT_HEADS // 2, 2 * CHUNK, 2, CHUNK)
    ds_sums = jnp.transpose(ds_sums, (0, 1, 3, 2, 4)).reshape(N_DIL, ATT_HEADS, 2 * CHUNK, CHUNK)
    return dq, dk, dv, ds_sums


def _bias_grad(name, ds_sums):
    flat = ds_sums.reshape(N_DIL, ATT_HEADS, TILE_ELEMS)

    def body(oh_ref, ds_ref, out_ref):
        for g in range(N_DIL):
            out_ref[g] = lax.dot_general(oh_ref[g], ds_ref[g], (NT, ((), ())), precision=lax.Precision.HIGHEST,
                                         preferred_element_type=F32)

    out = pl.pallas_call(
        body, name=name, out_shape=jax.ShapeDtypeStruct((N_DIL, N_BUCKETS, ATT_HEADS), F32),
        compiler_params=_params(),
    )(_bucket_onehot(), flat)
    return jnp.transpose(out, (1, 0, 2)).reshape(N_BUCKETS, N_DIL * ATT_HEADS)


def _peers():
    x, y, c = lax.axis_index("x"), lax.axis_index("y"), lax.axis_index("c")
    me = 4 * x + 2 * y + c
    others = [(x, y, 1 - c), (1 - x, y, c), (x, 1 - y, c), (1 - x, 1 - y, c),
              (1 - x, y, 1 - c), (x, 1 - y, 1 - c), (1 - x, 1 - y, 1 - c)]
    return me, others


def _slot(dev):
    return 4 * dev[0] + 2 * dev[1] + dev[2]


_HBM =pl.BlockSpec(memory_space=pltpu.HBM)
_SEM = pl.BlockSpec(memory_space=pltpu.SEMAPHORE)
_EFFECT = pltpu.SideEffectType.DATAFLOW_SIDE_EFFECTING


def _my_slot():
    return 4 * lax.axis_index("x") + 2 * lax.axis_index("y") + lax.axis_index("c")


def _exchange_copy(src_ref, land_ref, send_sems, recv_sems, k, dev, me, scatter, arriving):
    src = src_ref.at[me if arriving else _slot(dev)] if scatter else src_ref
    dst = land_ref.at[_slot(dev) if arriving else me]
    return pltpu.make_async_remote_copy(src_ref=src, dst_ref=dst, send_sem=send_sems.at[k], recv_sem=recv_sems.at[k],
                                        device_id=dev, device_id_type=MESH)


def _exchange_start(name, srcs, scatter):
    n = len(srcs)
    me = _my_slot()
    landings = []
    for src in srcs:
        own = lax.dynamic_index_in_dim(src, me, 0, keepdims=True) if scatter else src[None]
        landings.append(lax.dynamic_update_slice(lax.empty((N_DEV,) + src.shape[-2:], src.dtype), own, (me, 0, 0)))

    def body(*refs):
        src_refs, land_refs = refs[:n], refs[n:2 * n]
        send_sems, recv_sems = refs[2 * n:2 * n + 2]
        token = refs[-1]
        me, others = _peers()
        for p in range(n):
            for k, dev in enumerate(others):
                _exchange_copy(src_refs[p], land_refs[p], send_sems, recv_sems, p * (N_DEV - 1) + k, dev, me,
                               scatter, False).start()
        token[...] = jnp.zeros_like(token)

    sems = pltpu.SemaphoreType.DMA((n * (N_DEV - 1),))
    hbm = lambda a: pltpu.with_memory_space_constraint(a, pltpu.HBM)
    outs = pl.pallas_call(
        body, name=name,
        out_shape=(sems, sems, *[pltpu.HBM(a.shape, a.dtype) for a in srcs + landings],
                   jax.ShapeDtypeStruct((8, LANES), F32)),
        in_specs=(_HBM,) * (2 * n), out_specs=(_SEM, _SEM) + (_HBM,) * (2 * n) + (pl.BlockSpec(memory_space=pltpu.VMEM),),
        input_output_aliases={i: 2 + i for i in range(2 * n)},
        compiler_params=pltpu.CompilerParams(has_side_effects=_EFFECT),
    )(*[hbm(a) for a in srcs + landings])
    return (outs[0], outs[1], list(outs[2:2 + n]), list(outs[2 + n:2 + 2 * n]), scatter), outs[-1]


def _exchange_wait(name, handle, after):
    send_sems, recv_sems, src_thru, land_thru, scatter = handle
    n = len(src_thru)

    def body(*refs):
        src_refs, land_refs = refs[:n], refs[n:2 * n]
        send_sems, recv_sems = refs[2 * n:2 * n + 2]
        me, others = _peers()
        for p in range(n):
            for k, dev in enumerate(others):
                cp = _exchange_copy(src_refs[p], land_refs[p], send_sems, recv_sems, p * (N_DEV - 1) + k, dev, me,
                                    scatter, True)
                cp.wait_send()
                cp.wait_recv()

    outs = pl.pallas_call(
        body, name=name,
        out_shape=tuple(pltpu.HBM(a.shape, a.dtype) for a in src_thru + land_thru),
        in_specs=(_HBM,) * (2 * n) + (_SEM, _SEM, pl.BlockSpec(memory_space=pl.ANY)), out_specs=(_HBM,) * (2 * n),
        input_output_aliases={i: i for i in range(2 * n)},
        compiler_params=pltpu.CompilerParams(has_side_effects=_EFFECT),
    )(*src_thru, *land_thru, send_sems, recv_sems, after)
    return list(outs[n:])


def _all_reduce_small(name, buf):
    rows = buf.shape[0]
    rb = rows // N_DEV

    def body(x_ref, out_ref, stage, send1, recv1, send2, recv2):
        me, others = _peers()

        def block(ref, k):
            return ref.at[pl.ds(k * rb, rb), :]

        first = [pltpu.make_async_remote_copy(src_ref=block(x_ref, _slot(dev)), dst_ref=stage.at[me],
                                              send_sem=send1.at[k], recv_sem=recv1.at[k], device_id=dev,
                                              device_id_type=MESH) for k, dev in enumerate(others)]
        for cp in first:
            cp.start()
        stage[me] = x_ref[pl.ds(pl.multiple_of(me * rb, 8), rb), :]
        for k, dev in enumerate(others):
            pltpu.make_async_remote_copy(src_ref=block(x_ref, me), dst_ref=stage.at[_slot(dev)],
                                         send_sem=send1.at[k], recv_sem=recv1.at[k], device_id=dev,
                                         device_id_type=MESH).wait_recv()
        total = stage[0]
        for j in range(1, N_DEV):
            total = total + stage[j]
        out_ref[pl.ds(pl.multiple_of(me * rb, 8), rb), :] = total
        second = [pltpu.make_async_remote_copy(src_ref=block(out_ref, me), dst_ref=block(out_ref, me),
                                               send_sem=send2.at[k], recv_sem=recv2.at[k], device_id=dev,
                                               device_id_type=MESH) for k, dev in enumerate(others)]
        for cp in second:
            cp.start()
        for k, dev in enumerate(others):
            pltpu.make_async_remote_copy(src_ref=block(out_ref, me), dst_ref=block(out_ref, _slot(dev)),
                                         send_sem=send2.at[k], recv_sem=recv2.at[k], device_id=dev,
                                         device_id_type=MESH).wait_recv()
        for cp in first + second:
            cp.wait_send()

    sems = pltpu.SemaphoreType.DMA((N_DEV - 1,))
    return pl.pallas_call(
        body, name=name,
        in_specs=[pl.BlockSpec(memory_space=pltpu.VMEM)],
        out_specs=pl.BlockSpec(memory_space=pltpu.VMEM),
        out_shape=jax.ShapeDtypeStruct(buf.shape, F32),
        scratch_shapes=[pltpu.VMEM((N_DEV, rb, LANES), F32), sems, sems, sems, sems],
        compiler_params=pltpu.CompilerParams(vmem_limit_bytes=VMEM_LIMIT_BYTES),
    )(buf)


def _adamw_math(w, g, m, v):
    m = ADAM_B1 * m + (1.0 - ADAM_B1) * g
    v = ADAM_B2 * v + (1.0 - ADAM_B2) * (g * g)
    m_hat = m / (1.0 - ADAM_B1 ** ADAM_STEP)
    v_hat = v / (1.0 - ADAM_B2 ** ADAM_STEP)
    delta = -ADAM_LR * (m_hat / (jnp.sqrt(v_hat) + ADAM_EPS) + ADAM_WD * w)
    return delta, m, v


def _adamw(name, parts, w, m, v, tr=128):
    P, R, W = parts.shape
    tr = min(tr, R)

    def body(p_ref, w_ref, m_ref, v_ref, g_out, d_out, m_out, v_out):
        g = p_ref[0].astype(F32)
        for j in range(1, P):
            g = g + p_ref[j].astype(F32)
        delta, m_new, v_new = _adamw_math(w_ref[...], g, m_ref[...], v_ref[...])
        g_out[...] = g
        d_out[...] = delta
        m_out[...] = m_new
        v_out[...] = v_new

    row = pl.BlockSpec((tr, W), lambda i: (i, 0))
    shape = jax.ShapeDtypeStruct((R, W), F32)
    return pl.pallas_call(
        body, name=name, grid=(R // tr,),
        in_specs=[pl.BlockSpec((P, tr, W), lambda i: (0, i, 0)), row, row, row],
        out_specs=[row, row, row, row],
        out_shape=[shape, shape, shape, shape],
        compiler_params=_params(("parallel",)),
    )(parts, w, m, v)


def _adamw_shard(name, parts, w, m, v, layer, earlier=None, tr=256):
    L, K, N = w.shape
    tr = min(tr, K)
    n_prev = 0 if earlier is None else 4

    def body(p_ref, w_ref, m_ref, v_ref, *rest):
        g_out, d_out, m_out, v_out = rest[n_prev:]
        g = p_ref[0].astype(F32)
        for j in range(1, N_DEV):
            g = g + p_ref[j].astype(F32)
        delta, m_new, v_new = _adamw_math(w_ref[...], g, m_ref[...], v_ref[...])
        g_out[...] = g
        d_out[...] = delta
        m_out[...] = m_new
        v_out[...] = v_new

    row = pl.BlockSpec((None, tr, N), lambda i: (layer, i, 0))
    shape = jax.ShapeDtypeStruct((L, K, N), F32)
    return pl.pallas_call(
        body, name=name, grid=(K // tr,),
        in_specs=[pl.BlockSpec((N_DEV, tr, N), lambda i: (0, i, 0)), row, row, row]
        + [pl.BlockSpec(memory_space=pl.ANY)] * n_prev,
        out_specs=[row, row, row, row],
        out_shape=[shape, shape, shape, shape],
        input_output_aliases={4 + j: j for j in range(n_prev)},
        compiler_params=_params(("parallel",)),
    )(parts, w, m, v, *(earlier or ()))


def _column_slots(full):
    K, N = full.shape
    return jnp.transpose(full.reshape(K, N_DEV, N // N_DEV), (1, 0, 2))


def _from_column_slots(slots):
    _, K, n = slots.shape
    return jnp.transpose(slots, (1, 0, 2)).reshape(K, N_DEV * n)


_SMALL =("mix_norm_g", "mlp_norm_g", "final_norm_g", "a_ln_g", "a_ln_b", "a_w_s", "a_b_s", "rel_bias")


def _pack_small(vals):
    pieces = []
    for n in _SMALL:
        flat = vals[n].reshape(-1)
        pad = (-flat.shape[0]) % (8 * LANES)
        pieces.append(jnp.pad(flat, (0, pad)).reshape(-1, LANES))
    rows = sum(p.shape[0] for p in pieces)
    tail = (-rows) % (8 * N_DEV)
    if tail:
        pieces.append(jnp.zeros((tail, LANES), F32))
    return jnp.concatenate(pieces, axis=0)


def _unpack_small(buf, like):
    out = {}
    r = 0
    for n in _SMALL:
        size = like[n].size
        nrows = -(-size // (8 * LANES)) * 8
        out[n] = buf[r:r + nrows].reshape(-1)[:size].reshape(like[n].shape)
        r += nrows
    return out


_STAGES = (("gate", ("a_w_in", "a_w_out"), 0),
           ("mlp0", ("w_up", "w_down"), 0),
           ("att", ("b_w_qkv", "b_w_out"), 0),
           ("mlp1", ("w_up", "w_down"), 1))


def kernel(x, mix_norm_g, mlp_norm_g, final_norm_g, a_w_in, a_ln_g, a_ln_b, a_w_s, a_b_s, a_w_out, b_w_qkv, b_w_out, rel_bias, w_up, w_down, loss_target, m_mix_norm_g, m_mlp_norm_g, m_final_norm_g, m_a_w_in, m_a_ln_g, m_a_ln_b, m_a_w_s, m_a_b_s, m_a_w_out, m_b_w_qkv, m_b_w_out, m_rel_bias, m_w_up, m_w_down, v_mix_norm_g, v_mlp_norm_g, v_final_norm_g, v_a_w_in, v_a_ln_g, v_a_ln_b, v_a_w_s, v_a_b_s, v_a_w_out, v_b_w_qkv, v_b_w_out, v_rel_bias, v_w_up, v_w_down):
    w = dict(mix_norm_g=mix_norm_g, mlp_norm_g=mlp_norm_g, final_norm_g=final_norm_g, a_w_in=a_w_in, a_ln_g=a_ln_g,
             a_ln_b=a_ln_b, a_w_s=a_w_s, a_b_s=a_b_s, a_w_out=a_w_out, b_w_qkv=b_w_qkv, b_w_out=b_w_out,
             rel_bias=rel_bias, w_up=w_up, w_down=w_down)
    m = dict(mix_norm_g=m_mix_norm_g, mlp_norm_g=m_mlp_norm_g, final_norm_g=m_final_norm_g, a_w_in=m_a_w_in,
             a_ln_g=m_a_ln_g, a_ln_b=m_a_ln_b, a_w_s=m_a_w_s, a_b_s=m_a_b_s, a_w_out=m_a_w_out, b_w_qkv=m_b_w_qkv,
             b_w_out=m_b_w_out, rel_bias=m_rel_bias, w_up=m_w_up, w_down=m_w_down)
    v = dict(mix_norm_g=v_mix_norm_g, mlp_norm_g=v_mlp_norm_g, final_norm_g=v_final_norm_g, a_w_in=v_a_w_in,
             a_ln_g=v_a_ln_g, a_ln_b=v_a_ln_b, a_w_s=v_a_w_s, a_b_s=v_a_b_s, a_w_out=v_a_w_out, b_w_qkv=v_b_w_qkv,
             b_w_out=v_b_w_out, rel_bias=v_rel_bias, w_up=v_w_up, w_down=v_w_down)

    stages = {s: (names, layer) for s, names, layer in _STAGES}
    order = [s for s, _, _ in _STAGES]

    def shards_of(stage):
        names, layer = stages[stage]
        return [_bf(w[n][layer]) for n in names]

    pending = {}
    pending[order[0]], first_token = _exchange_start("gather_" + order[0] + "_start", shards_of(order[0]), False)

    def get_weights(stage, dep):
        gathered = _exchange_wait("gather_" + stage + "_wait", pending.pop(stage), dep)
        nxt = order.index(stage) + 1
        token = None
        if nxt < len(order):
            shards, gathered = lax.optimization_barrier((shards_of(order[nxt]), gathered))
            pending[order[nxt]], token = _exchange_start("gather_" + order[nxt] + "_start", shards, False)
        return gathered, token

    sent = {}

    def put_grads(stage, slot_grads):
        sent[stage], token = _exchange_start("scatter_" + stage + "_start", slot_grads, True)
        return token

    loss_local, grad_x, small_g = _local_step(
        x[0], loss_target[0], mix_norm_g, mlp_norm_g, final_norm_g, a_ln_g, a_ln_b, a_w_s, a_b_s, rel_bias,
        get_weights, put_grads, first_token)

    results = {}
    for stage in reversed(order):
        names, layer = stages[stage]
        received = _exchange_wait("scatter_" + stage + "_wait", sent[stage], grad_x)
        for n, parts in zip(names, received):
            results[n] = _adamw_shard("adamw_%s_%s" % (stage, n), parts, w[n], m[n], v[n], layer, results.get(n))

    reduced = _all_reduce_small("reduce_small", _pack_small(small_g))
    small = [_unpack_small(b, w) for b in _adamw("adamw_small", reduced[None], _pack_small(w), _pack_small(m),
                                                 _pack_small(v), tr=reduced.shape[0])]

    outs = []
    for j in range(4):
        outs.extend(small[j][n] if n in _SMALL else results[n][j] for n in w)
    loss = lax.psum(loss_local, ("x", "y", "c"))
    return (loss, grad_x[None], *outs)


def _local_step(xs, tgt, mix_norm_g, mlp_norm_g, final_norm_g, a_ln_g, a_ln_b, a_w_s, a_b_s, rel_bias,
                get_weights, put_grads, first_token=None):
    D = xs.shape[-1]
    g_mix = [mix_norm_g[l][None, :] for l in range(2)]
    g_mlp = [mlp_norm_g[l][None, :] for l in range(2)]
    g_fin = final_norm_g[None, :]
    ln_g, ln_b = a_ln_g, a_ln_b
    causal = jnp.tril(jnp.ones((CHUNK, CHUNK), dtype=bool))
    wm = _bf(jnp.where(causal[None], a_w_s[0], 0.0))
    bs_full = jnp.repeat(a_b_s[0].T, D // GROUPS, axis=1)
    bias_tiles = _bias_tiles("att_bias", rel_bias, after=first_token)

    y0 = _rms_fwd("rms_mix0", xs, g_mix[0], after=bias_tiles)
    (win, wout), token = get_weights("gate", y0)
    wout = wout.reshape(-1, D)
    uvp = _mm_nn("gate_in", y0, win, tm=512, nc=win.shape[2], shards=True, after=token)
    z = _gate_fwd("gate_mid", uvp, ln_g, ln_b, wm, bs_full)
    h1, y1 = _mm_nn("gate_out", z, wout, tm=512, nc=512, epi="res", extra=xs, norm_g=g_mlp[0])
    (wup0, wdn0), token = get_weights("mlp0", h1)
    wdn0 = wdn0.reshape(-1, D)
    a0, f0 = _mm_nn("mlp0_up", y1, wup0, tm=512, nc=wup0.shape[2], epi="relu2", shards=True, after=token)
    h2, y2 = _mm_nn("mlp0_down", f0, wdn0, tm=512, nc=512, epi="res", extra=h1, norm_g=g_mix[1])
    (wqkv, wo), token = get_weights("att", h2)
    wqkv, wo = _from_column_slots(wqkv), _from_column_slots(wo)
    qkv = _mm_nn("att_qkv", y2, wqkv, tm=256, nc=512, after=token)
    o_att, lse = _att_merge("att_merge", *_att_fwd("att_fwd", qkv, bias_tiles))
    h3, y3 = _mm_nn("att_out", o_att, wo, tm=512, nc=512, epi="res", extra=h2, norm_g=g_mlp[1])
    (wup1, wdn1), _ = get_weights("mlp1", h3)
    wdn1 = wdn1.reshape(-1, D)
    a1, f1 = _mm_nn("mlp1_up", y3, wup1, tm=512, nc=wup1.shape[2], epi="relu2", shards=True)
    dh, dg_fin, err2 = _mm_res_loss("mlp1_down_loss", f1, wdn1, h3, g_fin, tgt, tm=512, nc=512)
    loss_local = 0.5 * jnp.sum(err2) / D

    def mlp_bwd(tag, dh, h_in, y, a, f, wup_l, wdn_l, g_row, after):
        da = _mm_nt(tag + "_dact", dh, wdn_l, tm=512, nc=512, epi="mask2relu", extra=a, after=after)
        g_dn = _mm_tn(tag + "_dwdown", f, dh, t1=1024, tn=1024)
        g_up = _mm_tn(tag + "_dwup", y, da, t1=1024, tn=1024, slot_cols=wup_l.shape[2])
        dh_in, dg = _mm_nt_rms_bwd(tag + "_dy", [(da, wup_l, *_whole(wup_l))], h_in, g_row, dh, tm=512, nc=512,
                                   shards=True)
        return dh_in, dg, put_grads(tag, [g_up, g_dn.reshape(N_DEV, -1, D)])

    dh3, dg_mlp1, token = mlp_bwd("mlp1", dh, h3, y3, a1, f1, wup1, wdn1, g_mlp[1], None)

    d_o = _mm_nt("att_dout", dh3, wo, tm=512, nc=512, after=token)
    g_wo = _mm_tn("att_dwo", o_att, dh3, t1=512, tn=1024)
    dq, dk, dv, ds_sums = _att_bwd("att_bwd", qkv, o_att, lse, d_o, bias_tiles)
    part_w = N_DIL * ATT_WIDTH
    g_qkv = [_mm_tn("att_dwqkv%d" % p, y2, t, t1=1024, tn=part_w) for p, t in enumerate((dq, dk, dv))]
    dh2, dg_mix1 = _mm_nt_rms_bwd("att_dy", [(t, wqkv, (D, part_w), (0, p)) for p, t in enumerate((dq, dk, dv))],
                                  h2, g_mix[1], dh3, tm=512, nc=512)
    token = put_grads("att", [_column_slots(jnp.concatenate(g_qkv, axis=1)), _column_slots(g_wo)])

    dh1, dg_mlp0, token = mlp_bwd("mlp0", dh2, h1, y1, a0, f0, wup0, wdn0, g_mlp[0], token)

    dz = _mm_nt("gate_dz", dh1, wout, tm=512, nc=512, after=token)
    g_wout = _mm_tn("gate_dwout", z, dh1, t1=1024, tn=1024)
    duvp, d_wm, d_mixed, d_lng, d_lnb = _gate_bwd("gate_dmid", uvp, dz, ln_g, ln_b, wm, bs_full)
    g_win = _mm_tn("gate_dwin", y0, duvp, t1=1024, tn=1024, slot_cols=win.shape[2])
    token = put_grads("gate", [g_win, g_wout.reshape(N_DEV, -1, D)])
    grad_x, dg_mix0 = _mm_nt_rms_bwd("gate_dy", [(duvp, win, *_whole(win))], xs, g_mix[0], dh1, tm=512, nc=512,
                                     after=token, shards=True)

    small_g = dict(
        mix_norm_g=jnp.concatenate([dg_mix0, dg_mix1], axis=0),
        mlp_norm_g=jnp.concatenate([dg_mlp0, dg_mlp1], axis=0),
        final_norm_g=dg_fin[0], a_ln_g=d_lng, a_ln_b=d_lnb, a_w_s=d_wm[None],
        a_b_s=jnp.sum(d_mixed.reshape(CHUNK, GROUPS, D // GROUPS), axis=2).T[None],
        rel_bias=_bias_grad("att_dbias", ds_sums))
    return loss_local, grad_x, small_g
```

```python
import functools
import math

import jax
import jax.numpy as jnp
from jax import lax
from jax.experimental import pallas as pl
from jax.experimental.pallas import tpu as pltpu

F32 = jnp.float32
BF16 = jnp.bfloat16
MESH = pl.DeviceIdType.MESH

N_DEV = 8
EPS = 1e-6
NEG_INF = -1e30
CHUNK = 128
GROUPS = 8
HEAD_DIM = 64
ATT_HEADS = 8
ATT_WIDTH = ATT_HEADS * HEAD_DIM
DILATIONS = (1, 4, 16)
N_DIL = len(DILATIONS)
N_BUCKETS = 32
MAX_EXACT = N_BUCKETS // 2
REL_MAX_DISTANCE = 2048
ATT_ROWS = 2048
ATT_SCALE = HEAD_DIM ** -0.5
LANES = 128

ADAM_LR = 0.001
ADAM_B1 = 0.9
ADAM_B2 = 0.999
ADAM_EPS = 1e-08
ADAM_WD = 0.01
ADAM_STEP = 10

VMEM_LIMIT_BYTES = 56 * 1024 * 1024


def _params(semantics=None):
    return pltpu.CompilerParams(dimension_semantics=semantics, vmem_limit_bytes=VMEM_LIMIT_BYTES)


def _bf(v):
    return v.astype(BF16)


def _dot(a, b, dims):
    return lax.dot_general(a, b, (dims, ((), ())), preferred_element_type=F32)


NN = ((1,), (0,))
NT = ((1,), (1,))
TN = ((0,), (0,))


def _after_operand(after):
    if after is None:
        return [], []
    return [after], [pl.BlockSpec(memory_space=pl.ANY)]


def _rms_fwd(name, x, g, tm=512, after=None):
    S, D = x.shape
    after_args, after_specs = _after_operand(after)

    def body(x_ref, g_ref, *rest):
        y_ref = rest[-1]
        xv = x_ref[...]
        r = lax.rsqrt(jnp.mean(xv * xv, axis=-1, keepdims=True) + EPS)
        y_ref[...] = _bf(xv * r * g_ref[...])

    return pl.pallas_call(
        body, name=name, grid=(S // tm,),
        in_specs=[pl.BlockSpec((tm, D), lambda i: (i, 0)), pl.BlockSpec((1, D), lambda i: (0, 0))] + after_specs,
        out_specs=pl.BlockSpec((tm, D), lambda i: (i, 0)),
        out_shape=jax.ShapeDtypeStruct((S, D), BF16),
        compiler_params=_params(("parallel",)),
    )(x, g, *after_args)


def _mm_res_loss(name, a, w, res, g, target, *, tm, nc):
    M, D = res.shape

    def body(a_ref, w_ref, r_ref, g_ref, t_ref, dh_ref, dg_ref, l_ref, h_sc):
        i = pl.program_id(0)
        a_v = _bf(a_ref[...])
        for j in range(D // nc):
            cols, acc = _chunk_product([a_v], [w_ref], j, nc, False, False)
            h_sc[:, cols] = r_ref[:, cols] + acc
        xv = h_sc[...]
        r = lax.rsqrt(jnp.mean(xv * xv, axis=-1, keepdims=True) + EPS)
        xh = xv * r
        gv = g_ref[...]
        e = xh * gv - t_ref[...]
        dout = e / D
        dyg = dout * gv
        c = jnp.mean(dyg * xh, axis=-1, keepdims=True)
        dh_ref[...] = r * (dyg - xh * c)
        dg_part = jnp.sum(dout * xh, axis=0, keepdims=True)
        l_part = jnp.sum(e * e, axis=0, keepdims=True)

        @pl.when(i == 0)
        def _():
            dg_ref[...] = dg_part
            l_ref[...] = l_part

        @pl.when(i > 0)
        def _():
            dg_ref[...] += dg_part
            l_ref[...] += l_part

    row = pl.BlockSpec((tm, D), lambda i: (i, 0))
    vec = pl.BlockSpec((1, D), lambda i: (0, 0))
    return pl.pallas_call(
        body, name=name, grid=(M // tm,),
        in_specs=[pl.BlockSpec((tm, a.shape[1]), lambda i: (i, 0)), pl.BlockSpec(w.shape, lambda i: (0, 0)),
                  row, vec, row],
        out_specs=[row, vec, vec],
        out_shape=[jax.ShapeDtypeStruct((M, D), F32), jax.ShapeDtypeStruct((1, D), F32),
                   jax.ShapeDtypeStruct((1, D), F32)],
        scratch_shapes=[pltpu.VMEM((tm, D), F32)],
        compiler_params=_params(("arbitrary",)),
    )(a, w, res, g, target)


def _chunk_product(a_vals, w_refs, j, nc, nt, shards):
    cols = slice(j * nc, (j + 1) * nc)
    acc = None
    for a_v, w_ref in zip(a_vals, w_refs):
        if not shards:
            terms = [_dot(a_v, w_ref[cols, :], NT) if nt else _dot(a_v, w_ref[:, cols], NN)]
        elif nt:
            nl = w_ref.shape[2]
            terms = [_dot(a_v[:, k * nl:(k + 1) * nl], w_ref[k, cols, :], NT) for k in range(N_DEV)]
        else:
            terms = [_dot(a_v, w_ref[j], NN)]
        for t in terms:
            acc = t if acc is None else acc + t
    return cols, acc


def _mm_rows(name, pairs, n_out, *, nt, tm, nc, epi="plain", extra=None, out_dtype=F32, after=None, shards=False,
             norm_g=None):
    M = pairs[0][0].shape[0]
    np_ = len(pairs)
    after_args, after_specs = _after_operand(after)

    def body(*refs):
        a_refs = refs[:np_]
        w_refs = refs[np_:2 * np_]
        pos = 2 * np_
        e_ref = None
        if extra is not None:
            e_ref = refs[pos]
            pos += 1
        if norm_g is not None:
            g_ref = refs[pos]
            pos += 1
        pos += len(after_args)
        outs = refs[pos:]
        a_vals = [_bf(a[...]) for a in a_refs]
        for j in range(n_out // nc):
            cols, acc = _chunk_product(a_vals, w_refs, j, nc, nt, shards)
            if epi == "plain":
                outs[0][:, cols] = acc.astype(out_dtype)
            elif epi == "res":
                outs[0][:, cols] = e_ref[:, cols] + acc
            elif epi == "relu2":
                outs[0][:, cols] = _bf(acc)
                rl = jnp.maximum(acc, 0.0)
                outs[1][:, cols] = _bf(rl * rl)
            elif epi == "mask2relu":
                outs[0][:, cols] = _bf(acc * (2.0 * jnp.maximum(e_ref[:, cols].astype(F32), 0.0)))
        if norm_g is not None:
            hv = outs[0][...]
            r = lax.rsqrt(jnp.mean(hv * hv, axis=-1, keepdims=True) + EPS)
            outs[1][...] = _bf(hv * r * g_ref[...])

    in_specs = [pl.BlockSpec((tm, a.shape[1]), lambda i: (i, 0)) for a, _, _, _ in pairs]
    for _, _, wshape, widx in pairs:
        in_specs.append(pl.BlockSpec(wshape, functools.partial(lambda i, widx: widx, widx=widx)))
    args = [a for a, _, _, _ in pairs] + [w for _, w, _, _ in pairs]
    if extra is not None:
        in_specs.append(pl.BlockSpec((tm, n_out), lambda i: (i, 0)))
        args.append(extra)
    if norm_g is not None:
        in_specs.append(pl.BlockSpec((1, n_out), lambda i: (0, 0)))
        args.append(norm_g)
    in_specs += after_specs
    args += after_args
    row_out = pl.BlockSpec((tm, n_out), lambda i: (i, 0))
    if epi == "relu2":
        out_specs = [row_out, row_out]
        out_shape = [jax.ShapeDtypeStruct((M, n_out), BF16), jax.ShapeDtypeStruct((M, n_out), BF16)]
    elif norm_g is not None:
        out_specs = [row_out, row_out]
        out_shape = [jax.ShapeDtypeStruct((M, n_out), F32), jax.ShapeDtypeStruct((M, n_out), BF16)]
    else:
        dt = BF16 if epi == "mask2relu" else (F32 if epi == "res" else out_dtype)
        out_specs = row_out
        out_shape = jax.ShapeDtypeStruct((M, n_out), dt)
    return pl.pallas_call(
        body, name=name, grid=(M // tm,), in_specs=in_specs, out_specs=out_specs, out_shape=out_shape,
        compiler_params=_params(("parallel",)),
    )(*args)


def _whole(w):
    return w.shape, (0,) * w.ndim


def _mm_nn(name, a, w, **kw):
    n_out = w.shape[0] * w.shape[2] if w.ndim == 3 else w.shape[1]
    return _mm_rows(name, [(a, w, *_whole(w))], n_out, nt=False, **kw)


def _mm_nt(name, a, w, **kw):
    return _mm_rows(name, [(a, w, *_whole(w))], w.shape[0], nt=True, **kw)


def _mm_nt_rms_bwd(name, pairs, x, g, dres, *, tm, nc, after=None, shards=False):
    M, D = x.shape
    np_ = len(pairs)
    after_args, after_specs = _after_operand(after)

    def body(*refs):
        a_refs = refs[:np_]
        w_refs = refs[np_:2 * np_]
        x_ref, g_ref, r_ref = refs[2 * np_:2 * np_ + 3]
        dx_ref, dg_ref, dy_sc = refs[-3:]
        i = pl.program_id(0)
        a_vals = [_bf(a[...]) for a in a_refs]
        for j in range(D // nc):
            cols, acc = _chunk_product(a_vals, w_refs, j, nc, True, shards)
            dy_sc[:, cols] = acc
        xv = x_ref[...]
        r = lax.rsqrt(jnp.mean(xv * xv, axis=-1, keepdims=True) + EPS)
        xh = xv * r
        dy_v = dy_sc[...]
        dyg = dy_v * g_ref[...]
        c = jnp.mean(dyg * xh, axis=-1, keepdims=True)
        dx_ref[...] = r_ref[...] + r * (dyg - xh * c)
        part = jnp.sum(dy_v * xh, axis=0, keepdims=True)

        @pl.when(i == 0)
        def _():
            dg_ref[...] = part

        @pl.when(i > 0)
        def _():
            dg_ref[...] += part

    row = pl.BlockSpec((tm, D), lambda i: (i, 0))
    vec = pl.BlockSpec((1, D), lambda i: (0, 0))
    in_specs = [pl.BlockSpec((tm, a.shape[1]), lambda i: (i, 0)) for a, _, _, _ in pairs]
    for _, _, wshape, widx in pairs:
        in_specs.append(pl.BlockSpec(wshape, functools.partial(lambda i, widx: widx, widx=widx)))
    args = [a for a, _, _, _ in pairs] + [w for _, w, _, _ in pairs]
    return pl.pallas_call(
        body, name=name, grid=(M // tm,),
        in_specs=in_specs + [row, vec, row] + after_specs,
        out_specs=[row, vec],
        out_shape=[jax.ShapeDtypeStruct((M, D), F32), jax.ShapeDtypeStruct((1, D), F32)],
        scratch_shapes=[pltpu.VMEM((tm, D), F32)],
        compiler_params=_params(("arbitrary",)),
    )(*args, x, g, dres, *after_args)


def _mm_tn(name, a, b, *, t1, tn, tm=2048, slot_cols=None):
    M, K1 = a.shape
    N = b.shape[1]
    nm = M // tm

    def body(a_ref, b_ref, o_ref, acc_ref):
        m = pl.program_id(2)
        t = _dot(_bf(a_ref[...]), _bf(b_ref[...]), TN)

        @pl.when(m == 0)
        def _():
            acc_ref[...] = t

        @pl.when(m > 0)
        def _():
            acc_ref[...] += t

        @pl.when(m == nm - 1)
        def _():
            if slot_cols is None:
                o_ref[...] = _bf(acc_ref[...])
            else:
                for k in range(tn // slot_cols):
                    o_ref[k] = _bf(acc_ref[:, k * slot_cols:(k + 1) * slot_cols])

    if slot_cols is not None:
        out_spec = pl.BlockSpec((tn // slot_cols, t1, slot_cols), lambda i, j, m: (j, i, 0))
        out_shape = jax.ShapeDtypeStruct((N // slot_cols, K1, slot_cols), BF16)
    else:
        out_spec = pl.BlockSpec((t1, tn), lambda i, j, m: (i, j))
        out_shape = jax.ShapeDtypeStruct((K1, N), BF16)
    return pl.pallas_call(
        body, name=name, grid=(K1 // t1, N // tn, nm),
        in_specs=[pl.BlockSpec((tm, t1), lambda i, j, m: (m, i)), pl.BlockSpec((tm, tn), lambda i, j, m: (m, j))],
        out_specs=out_spec, out_shape=out_shape,
        scratch_shapes=[pltpu.VMEM((t1, tn), F32)],
        compiler_params=_params(("parallel", "parallel", "arbitrary")),
    )(a, b)


_INV_SQRT2 = 1.0 / math.sqrt(2.0)
_INV_SQRT2PI = 1.0 / math.sqrt(2.0 * math.pi)


def _gelu(x):
    return 0.5 * x * (1.0 + lax.erf(x * _INV_SQRT2))


def _gelu_and_grad(x):
    cdf = 0.5 * (1.0 + lax.erf(x * _INV_SQRT2))
    return x * cdf, cdf + x * (_INV_SQRT2PI * jnp.exp(-0.5 * x * x))


def _layer_norm_parts(v):
    mu = jnp.mean(v, axis=-1, keepdims=True)
    xc = v - mu
    rs = lax.rsqrt(jnp.mean(xc * xc, axis=-1, keepdims=True) + EPS)
    return xc * rs, rs


def _gate_fwd(name, uvp, ln_g, ln_b, wm, bs_full, tr=512):
    S, W2 = uvp.shape
    W = W2 // 2
    gd = W // GROUPS

    def body(u_ref, v_ref, lg_ref, lb_ref, wm_ref, bs_ref, z_ref):
        vh, _ = _layer_norm_parts(_gelu(v_ref[...]))
        vn = _bf(vh * lg_ref[...] + lb_ref[...])
        for ci in range(tr // CHUNK):
            rows = slice(ci * CHUNK, (ci + 1) * CHUNK)
            for g in range(GROUPS):
                cols = slice(g * gd, (g + 1) * gd)
                mixed = _dot(wm_ref[g], vn[rows, cols], NN) + bs_ref[:, cols]
                z_ref[rows, cols] = _bf(_gelu(u_ref[rows, cols]) * mixed)

    vec = pl.BlockSpec((1, W), lambda i: (0, 0))
    return pl.pallas_call(
        body, name=name, grid=(S // tr,),
        in_specs=[pl.BlockSpec((tr, W), lambda i: (i, 0)), pl.BlockSpec((tr, W), lambda i: (i, 1)), vec, vec,
                  pl.BlockSpec((GROUPS, CHUNK, CHUNK), lambda i: (0, 0, 0)),
                  pl.BlockSpec((CHUNK, W), lambda i: (0, 0))],
        out_specs=pl.BlockSpec((tr, W), lambda i: (i, 0)),
        out_shape=jax.ShapeDtypeStruct((S, W), BF16),
        compiler_params=_params(("parallel",)),
    )(uvp, uvp, ln_g, ln_b, wm, bs_full)


def _gate_bwd(name, uvp, dz, ln_g, ln_b, wm, bs_full, tr=256):
    S, W2 = uvp.shape
    W = W2 // 2
    gd = W // GROUPS
    n_steps = S // tr

    def body(u_ref, v_ref, dz_ref, lg_ref, lb_ref, wm_ref, bs_ref, duv_ref, dwm_ref, dmx_ref, dlg_ref, dlb_ref,
             dvn_ref):
        i = pl.program_id(0)
        v, dv_dvp = _gelu_and_grad(v_ref[...])
        vh, rs = _layer_norm_parts(v)
        lg = lg_ref[...]
        vn = _bf(vh * lg + lb_ref[...])

        @pl.when(i == 0)
        def _():
            dwm_ref[...] = jnp.zeros_like(dwm_ref)
            dmx_ref[...] = jnp.zeros_like(dmx_ref)
            dlg_ref[...] = jnp.zeros_like(dlg_ref)
            dlb_ref[...] = jnp.zeros_like(dlb_ref)

        for ci in range(tr // CHUNK):
            rows = slice(ci * CHUNK, (ci + 1) * CHUNK)
            for g in range(GROUPS):
                cols = slice(g * gd, (g + 1) * gd)
                u, du_dup = _gelu_and_grad(u_ref[rows, cols])
                dz_v = dz_ref[rows, cols]
                dmixed = dz_v * u
                dmx_ref[:, cols] += dmixed
                dmixed_b = _bf(dmixed)
                mixed = _dot(wm_ref[g], vn[rows, cols], NN) + bs_ref[:, cols]
                duv_ref[rows, cols] = _bf(dz_v * mixed * du_dup)
                dwm_ref[g] += _dot(dmixed_b, vn[rows, cols], NT)
                dvn_ref[rows, cols] = _dot(wm_ref[g], dmixed_b, TN)
        dvn = dvn_ref[...]
        dlg_ref[...] += jnp.sum(dvn * vh, axis=0, keepdims=True)
        dlb_ref[...] += jnp.sum(dvn, axis=0, keepdims=True)
        dvh = dvn * lg
        dv = rs * (dvh - jnp.mean(dvh, axis=-1, keepdims=True) - vh * jnp.mean(dvh * vh, axis=-1, keepdims=True))
        duv_ref[:, W:] = _bf(dv * dv_dvp)

        @pl.when(i == n_steps - 1)
        def _():
            t_idx = lax.broadcasted_iota(jnp.int32, (CHUNK, CHUNK), 0)
            s_idx = lax.broadcasted_iota(jnp.int32, (CHUNK, CHUNK), 1)
            keep = (s_idx <= t_idx).astype(F32)
            for g in range(GROUPS):
                dwm_ref[g] = dwm_ref[g] * keep

    vec = pl.BlockSpec((1, W), lambda i: (0, 0))
    row = pl.BlockSpec((tr, W), lambda i: (i, 0))
    return pl.pallas_call(
        body, name=name, grid=(n_steps,),
        in_specs=[row, pl.BlockSpec((tr, W), lambda i: (i, 1)), row, vec, vec,
                  pl.BlockSpec((GROUPS, CHUNK, CHUNK), lambda i: (0, 0, 0)),
                  pl.BlockSpec((CHUNK, W), lambda i: (0, 0))],
        out_specs=[pl.BlockSpec((tr, W2), lambda i: (i, 0)),
                   pl.BlockSpec((GROUPS, CHUNK, CHUNK), lambda i: (0, 0, 0)),
                   pl.BlockSpec((CHUNK, W), lambda i: (0, 0)), vec, vec],
        out_shape=[jax.ShapeDtypeStruct((S, W2), BF16), jax.ShapeDtypeStruct((GROUPS, CHUNK, CHUNK), F32),
                   jax.ShapeDtypeStruct((CHUNK, W), F32), jax.ShapeDtypeStruct((1, W), F32),
                   jax.ShapeDtypeStruct((1, W), F32)],
        scratch_shapes=[pltpu.VMEM((tr, W), F32)],
        compiler_params=_params(("arbitrary",)),
    )(uvp, uvp, dz, ln_g, ln_b, wm, bs_full)


def _t5_bucket(distance):
    small = distance < MAX_EXACT
    nf = jnp.maximum(distance, 1).astype(F32)
    large = MAX_EXACT + (jnp.log(nf / MAX_EXACT) / math.log(REL_MAX_DISTANCE / MAX_EXACT)
                         * (N_BUCKETS - MAX_EXACT)).astype(jnp.int32)
    large = jnp.minimum(large, N_BUCKETS - 1)
    return jnp.where(small, distance, large)


TILE_ELEMS = 2 * CHUNK * CHUNK


def _band_buckets():
    rel = CHUNK + jnp.arange(CHUNK)[None, :] - jnp.arange(2 * CHUNK)[:, None]
    band = (rel >= 0) & (rel <= CHUNK)
    buckets = [_t5_bucket(jnp.clip(rel, 0, CHUNK) * d) for d in DILATIONS]
    return jnp.stack(buckets), band


def _bucket_onehot():
    buckets, _ = _band_buckets()
    return (buckets.reshape(N_DIL, 1, TILE_ELEMS) == jnp.arange(N_BUCKETS)[None, :, None]).astype(F32)


def _bias_tiles(name, rel_bias, after=None):
    _, band = _band_buckets()
    own = band & (jnp.arange(2 * CHUNK) >= CHUNK)[:, None]
    masks = jnp.stack([own, band]).reshape(2, TILE_ELEMS).astype(F32)
    tables = jnp.transpose(rel_bias.reshape(N_BUCKETS, N_DIL, ATT_HEADS), (1, 2, 0))
    after_args, after_specs = _after_operand(after)

    def body(t_ref, oh_ref, m_ref, *rest):
        out_ref = rest[-1]
        for g in range(N_DIL):
            bias = lax.dot_general(t_ref[g], oh_ref[g], (NN, ((), ())), precision=lax.Precision.HIGHEST,
                                   preferred_element_type=F32)
            for f in range(2):
                out_ref[g, f] = jnp.where(m_ref[f:f + 1, :] > 0.5, bias, NEG_INF)

    whole = pl.BlockSpec(memory_space=pltpu.VMEM)
    out = pl.pallas_call(
        body, name=name, out_shape=jax.ShapeDtypeStruct((N_DIL, 2, ATT_HEADS, TILE_ELEMS), F32),
        in_specs=[whole, whole, whole] + after_specs, out_specs=whole,
        compiler_params=_params(),
    )(tables, _bucket_onehot(), masks, *after_args)
    out = out.reshape(N_DIL, 2, ATT_HEADS // 2, 2, 2 * CHUNK, CHUNK)
    return jnp.transpose(out, (0, 1, 2, 4, 3, 5)).reshape(N_DIL, 2, ATT_HEADS // 2, 2 * CHUNK, 2 * CHUNK)


def _att_specs(order):
    def column(part, ids):
        hp, g, _ = order(*ids)
        return part * 3 * 4 + g * 4 + hp

    def window(part):
        def index(*ids):
            c = order(*ids)[2]
            return pl.multiple_of(jnp.maximum(c - 1, 0) * ATT_ROWS, ATT_ROWS), column(part, ids) * LANES
        return pl.BlockSpec((pl.Element(2 * ATT_ROWS), pl.Element(LANES)), index)

    return [pl.BlockSpec((ATT_ROWS, LANES), lambda *ids: (order(*ids)[2], column(0, ids))), window(1), window(2)]


def _window_base(c):
    return jnp.where(c == 0, 0, ATT_ROWS)


def _rows(start, d):
    if d == 1:
        return pl.ds(pl.multiple_of(start, CHUNK), CHUNK)
    return pl.ds(start, CHUNK, stride=d)


def _att_tile_offsets(t, d):
    n = t // d
    r = t % d
    return n * (CHUNK * d) + r, n


def _head_pair_columns(x_t):
    zeros = jnp.zeros((HEAD_DIM, CHUNK), x_t.dtype)
    return jnp.concatenate([jnp.concatenate([x_t[:HEAD_DIM], zeros], axis=0),
                            jnp.concatenate([zeros, x_t[HEAD_DIM:]], axis=0)], axis=1)


def _head_pair_rows(y):
    return jnp.concatenate([y[:HEAD_DIM, :CHUNK], y[HEAD_DIM:, CHUNK:]], axis=0)


def _att_fwd(name, qkv, bias_tiles):
    S = qkv.shape[0]
    n_chunks = S // ATT_ROWS
    tiles = ATT_ROWS // CHUNK

    def body(q_ref, kk, vv, b_ref, o_ref, l_ref):
        c = pl.program_id(1)
        g = pl.program_id(2)
        base = _window_base(c)

        for gi, d in enumerate(DILATIONS):
            @pl.when(g == gi)
            def _(d=d):
                span = CHUNK * d

                def tile(t, carry):
                    q0, n = _att_tile_offsets(t, d)
                    first = (c == 0) & (n == 0)
                    rows = _rows(q0, d)
                    cur = _rows(base + q0, d)
                    prev = _rows(jnp.where(first, q0, base + q0 - span), d)
                    inner = jnp.where(first, 0, 1)
                    qq = _head_pair_columns(_bf(q_ref[rows, :] * ATT_SCALE).T)
                    s_p = _dot(_bf(kk[prev, :]), qq, NN) + b_ref[inner, 0:CHUNK, :]
                    s_c = _dot(_bf(kk[cur, :]), qq, NN) + b_ref[inner, CHUNK:2 * CHUNK, :]
                    m = jnp.maximum(jnp.max(s_p, axis=0, keepdims=True), jnp.max(s_c, axis=0, keepdims=True))
                    p_p = jnp.exp(s_p - m)
                    p_c = jnp.exp(s_c - m)
                    l = jnp.sum(p_p, axis=0, keepdims=True) + jnp.sum(p_c, axis=0, keepdims=True)
                    o2 = (_dot(_bf(vv[prev, :]).T, _bf(p_p), NN)
                          + _dot(_bf(vv[cur, :]).T, _bf(p_c), NN)) * (1.0 / l)
                    lse = m + jnp.log(l)
                    l_t = jnp.concatenate([jnp.broadcast_to(lse[:, :CHUNK], (HEAD_DIM, CHUNK)),
                                           jnp.broadcast_to(lse[:, CHUNK:], (HEAD_DIM, CHUNK))], axis=0)
                    o_ref[rows, :] = _head_pair_rows(o2).T
                    l_ref[rows, :] = l_t.T
                    return carry

                lax.fori_loop(0, tiles, tile, 0, unroll=8)

    order = lambda hp, c, g: (hp, g, c)
    out_spec = pl.BlockSpec((None, ATT_ROWS, LANES), lambda hp, c, g: (g, c, hp))
    shape = jax.ShapeDtypeStruct((N_DIL, S, ATT_WIDTH), F32)
    return pl.pallas_call(
        body, name=name, grid=(ATT_HEADS // 2, n_chunks, N_DIL),
        in_specs=_att_specs(order) + [
            pl.BlockSpec((None, 2, None, 2 * CHUNK, 2 * CHUNK), lambda hp, c, g: (g, 0, hp, 0, 0))],
        out_specs=[out_spec, out_spec],
        out_shape=[shape, shape],
        compiler_params=_params(("parallel", "parallel", "parallel")),
    )(qkv, qkv, qkv, bias_tiles)


def _att_merge(name, o_g, l_g, tm=512):
    _, S, W = o_g.shape

    def body(o_ref, l_ref, out_ref, lse_ref):
        ls = [l_ref[g] for g in range(N_DIL)]
        mx = functools.reduce(jnp.maximum, ls)
        ws = [jnp.exp(l - mx) for l in ls]
        tot = functools.reduce(lambda a, b: a + b, ws)
        acc = ws[0] * o_ref[0]
        for g in range(1, N_DIL):
            acc = acc + ws[g] * o_ref[g]
        out_ref[...] = acc / tot
        lse_ref[...] = mx + jnp.log(tot)

    blk = pl.BlockSpec((N_DIL, tm, W), lambda i: (0, i, 0))
    row = pl.BlockSpec((tm, W), lambda i: (i, 0))
    shape = jax.ShapeDtypeStruct((S, W), F32)
    return pl.pallas_call(
        body, name=name, grid=(S // tm,), in_specs=[blk, blk], out_specs=[row, row], out_shape=[shape, shape],
        compiler_params=_params(("parallel",)),
    )(o_g, l_g)


def _att_bwd(name, qkv, o, lse, d_o, bias_tiles):
    S = qkv.shape[0]
    n_chunks = S // ATT_ROWS
    tiles = ATT_ROWS // CHUNK

    def body(q_ref, kk, vv, o_ref, l_ref, do_ref, b_ref, dq_out, dk_out, dv_out, ds_ref, dq_ref, dk_ref, dv_ref):
        g = pl.program_id(1)
        c = pl.program_id(2)

        @pl.when(c == 0)
        def _():
            dk_ref[...] = jnp.zeros_like(dk_ref)
            dv_ref[...] = jnp.zeros_like(dv_ref)
            ds_ref[...] = jnp.zeros_like(ds_ref)

        base = _window_base(c)
        first_row = c * ATT_ROWS
        head0 = lax.broadcasted_iota(jnp.int32, (CHUNK, LANES), 1) < HEAD_DIM

        def head_pair_stack(x):
            zero = jnp.zeros_like(x)
            return jnp.concatenate([jnp.where(head0, x, zero), jnp.where(head0, zero, x)], axis=0)

        for gi, d in enumerate(DILATIONS):
            @pl.when(g == gi)
            def _(d=d):
                span = CHUNK * d

                def tile(t, carry):
                    q0, n = _att_tile_offsets(t, d)
                    first = (c == 0) & (n == 0)
                    rows = _rows(q0, d)
                    cur = _rows(base + q0, d)
                    prev = _rows(jnp.where(first, q0, base + q0 - span), d)
                    inner = jnp.where(first, 0, 1)
                    g_cur = _rows(first_row + q0, d)
                    g_prev = _rows(jnp.where(first, q0, first_row + q0 - span), d)
                    q2 = _bf(q_ref[rows, :] * ATT_SCALE)
                    q_t = q2.T
                    k2 = _bf(jnp.concatenate([kk[prev, :], kk[cur, :]], axis=0))
                    k_t = k2.T
                    v2 = _bf(jnp.concatenate([vv[prev, :], vv[cur, :]], axis=0))
                    do2 = do_ref[rows, :]
                    do_b = _bf(do2)
                    do_t = do_b.T
                    lse_t = l_ref[rows, :].T
                    dd_t = (do2 * o_ref[rows, :]).T
                    lse = jnp.concatenate([lse_t[0:1], lse_t[HEAD_DIM:HEAD_DIM + 1]], axis=1)
                    delta = jnp.concatenate([jnp.sum(dd_t[:HEAD_DIM], axis=0, keepdims=True),
                                             jnp.sum(dd_t[HEAD_DIM:], axis=0, keepdims=True)], axis=1)
                    s = _dot(k2, _head_pair_columns(q_t), NN) + b_ref[inner]
                    p = jnp.exp(s - lse)
                    ds = p * (_dot(v2, _head_pair_columns(do_t), NN) - delta)
                    ds_ref[...] += ds
                    ds_b = _bf(ds)
                    dq_t = _head_pair_rows(_dot(k_t, ds_b, NN))
                    dk2 = _dot(ds_b, head_pair_stack(q2), NN)
                    dv2 = _dot(_bf(p), head_pair_stack(do_b), NN)
                    dq_ref[rows, :] = (dq_t * ATT_SCALE).T
                    dk_ref[g_prev, :] += dk2[0:CHUNK]
                    dk_ref[g_cur, :] += dk2[CHUNK:2 * CHUNK]
                    dv_ref[g_prev, :] += dv2[0:CHUNK]
                    dv_ref[g_cur, :] += dv2[CHUNK:2 * CHUNK]
                    return carry

                lax.fori_loop(0, tiles, tile, 0, unroll=4)

        dq_out[...] = _bf(dq_ref[...])

        @pl.when(c == n_chunks - 1)
        def _():
            dk_out[...] = _bf(dk_ref[...])
            dv_out[...] = _bf(dv_ref[...])

    order = lambda hp, g, c: (hp, g, c)
    chunk = pl.BlockSpec((ATT_ROWS, LANES), lambda hp, g, c: (c, hp))
    slab = pl.BlockSpec((S, LANES), lambda hp, g, c: (0, g * 4 + hp))
    width = N_DIL * ATT_WIDTH
    dq, dk, dv, ds_sums = pl.pallas_call(
        body, name=name, grid=(ATT_HEADS // 2, N_DIL, n_chunks),
        in_specs=_att_specs(order) + [chunk, chunk, chunk,
                                      pl.BlockSpec((None, 2, None, 2 * CHUNK, 2 * CHUNK),
                                                   lambda hp, g, c: (g, 0, hp, 0, 0))],
        out_specs=[pl.BlockSpec((ATT_ROWS, LANES), lambda hp, g, c: (c, g * 4 + hp)), slab, slab,
                   pl.BlockSpec((None, None, 2 * CHUNK, 2 * CHUNK), lambda hp, g, c: (g, hp, 0, 0))],
        out_shape=[jax.ShapeDtypeStruct((S, width), BF16), jax.ShapeDtypeStruct((S, width), BF16),
                   jax.ShapeDtypeStruct((S, width), BF16),
                   jax.ShapeDtypeStruct((N_DIL, ATT_HEADS // 2, 2 * CHUNK, 2 * CHUNK), F32)],
        scratch_shapes=[pltpu.VMEM((ATT_ROWS, LANES), F32), pltpu.VMEM((S, LANES), F32),
                        pltpu.VMEM((S, LANES), F32)],
        compiler_params=_params(("parallel", "parallel", "arbitrary")),
    )(qkv, qkv, qkv, o, lse, d_o, bias_tiles)
    ds_sums = ds_sums.reshape(N_DIL, ATT_HEADS // 2, 2 * CHUNK, 2, CHUNK)
    ds_sums = jnp.transpose(ds_sums, (0, 1, 3, 2, 4)).reshape(N_DIL, ATT_HEADS, 2 * CHUNK, CHUNK)
    return dq, dk, dv, ds_sums


def _bias_grad(name, ds_sums):
    flat = ds_sums.reshape(N_DIL, ATT_HEADS, TILE_ELEMS)

    def body(oh_ref, ds_ref, out_ref):
        for g in range(N_DIL):
            out_ref[g] = lax.dot_general(oh_ref[g], ds_ref[g], (NT, ((), ())), precision=lax.Precision.HIGHEST,
                                         preferred_element_type=F32)

    out = pl.pallas_call(
        body, name=name, out_shape=jax.ShapeDtypeStruct((N_DIL, N_BUCKETS, ATT_HEADS), F32),
        compiler_params=_params(),
    )(_bucket_onehot(), flat)
    return jnp.transpose(out, (1, 0, 2)).reshape(N_BUCKETS, N_DIL * ATT_HEADS)


def _peers():
    x, y, c = lax.axis_index("x"), lax.axis_index("y"), lax.axis_index("c")
    me = 4 * x + 2 * y + c
    others = [(x, y, 1 - c), (1 - x, y, c), (x, 1 - y, c), (1 - x, 1 - y, c),
              (1 - x, y, 1 - c), (x, 1 - y, 1 - c), (1 - x, 1 - y, 1 - c)]
    return me, others


def _slot(dev):
    return 4 * dev[0] + 2 * dev[1] + dev[2]


_HBM =pl.BlockSpec(memory_space=pltpu.HBM)
_SEM = pl.BlockSpec(memory_space=pltpu.SEMAPHORE)
_EFFECT = pltpu.SideEffectType.DATAFLOW_SIDE_EFFECTING


def _my_slot():
    return 4 * lax.axis_index("x") + 2 * lax.axis_index("y") + lax.axis_index("c")


def _exchange_copy(src_ref, land_ref, send_sems, recv_sems, k, dev, me, scatter, arriving):
    src = src_ref.at[me if arriving else _slot(dev)] if scatter else src_ref
    dst = land_ref.at[_slot(dev) if arriving else me]
    return pltpu.make_async_remote_copy(src_ref=src, dst_ref=dst, send_sem=send_sems.at[k], recv_sem=recv_sems.at[k],
                                        device_id=dev, device_id_type=MESH)


def _exchange_start(name, srcs, scatter):
    n = len(srcs)
    me = _my_slot()
    landings = []
    for src in srcs:
        own = lax.dynamic_index_in_dim(src, me, 0, keepdims=True) if scatter else src[None]
        landings.append(lax.dynamic_update_slice(lax.empty((N_DEV,) + src.shape[-2:], src.dtype), own, (me, 0, 0)))

    def body(*refs):
        src_refs, land_refs = refs[:n], refs[n:2 * n]
        send_sems, recv_sems = refs[2 * n:2 * n + 2]
        token = refs[-1]
        me, others = _peers()
        for p in range(n):
            for k, dev in enumerate(others):
                _exchange_copy(src_refs[p], land_refs[p], send_sems, recv_sems, p * (N_DEV - 1) + k, dev, me,
                               scatter, False).start()
        token[...] = jnp.zeros_like(token)

    sems = pltpu.SemaphoreType.DMA((n * (N_DEV - 1),))
    hbm = lambda a: pltpu.with_memory_space_constraint(a, pltpu.HBM)
    outs = pl.pallas_call(
        body, name=name,
        out_shape=(sems, sems, *[pltpu.HBM(a.shape, a.dtype) for a in srcs + landings],
                   jax.ShapeDtypeStruct((8, LANES), F32)),
        in_specs=(_HBM,) * (2 * n), out_specs=(_SEM, _SEM) + (_HBM,) * (2 * n) + (pl.BlockSpec(memory_space=pltpu.VMEM),),
        input_output_aliases={i: 2 + i for i in range(2 * n)},
        compiler_params=pltpu.CompilerParams(has_side_effects=_EFFECT),
    )(*[hbm(a) for a in srcs + landings])
    return (outs[0], outs[1], list(outs[2:2 + n]), list(outs[2 + n:2 + 2 * n]), scatter), outs[-1]


def _exchange_wait(name, handle, after):
    send_sems, recv_sems, src_thru, land_thru, scatter = handle
    n = len(src_thru)

    def body(*refs):
        src_refs, land_refs = refs[:n], refs[n:2 * n]
        send_sems, recv_sems = refs[2 * n:2 * n + 2]
        me, others = _peers()
        for p in range(n):
            for k, dev in enumerate(others):
                cp = _exchange_copy(src_refs[p], land_refs[p], send_sems, recv_sems, p * (N_DEV - 1) + k, dev, me,
                                    scatter, True)
                cp.wait_send()
                cp.wait_recv()

    outs = pl.pallas_call(
        body, name=name,
        out_shape=tuple(pltpu.HBM(a.shape, a.dtype) for a in src_thru + land_thru),
        in_specs=(_HBM,) * (2 * n) + (_SEM, _SEM, pl.BlockSpec(memory_space=pl.ANY)), out_specs=(_HBM,) * (2 * n),
        input_output_aliases={i: i for i in range(2 * n)},
        compiler_params=pltpu.CompilerParams(has_side_effects=_EFFECT),
    )(*src_thru, *land_thru, send_sems, recv_sems, after)
    return list(outs[n:])


def _adamw_math(w, g, m, v):
    m = ADAM_B1 * m + (1.0 - ADAM_B1) * g
    v = ADAM_B2 * v + (1.0 - ADAM_B2) * (g * g)
    m_hat = m / (1.0 - ADAM_B1 ** ADAM_STEP)
    v_hat = v / (1.0 - ADAM_B2 ** ADAM_STEP)
    delta = -ADAM_LR * (m_hat / (jnp.sqrt(v_hat) + ADAM_EPS) + ADAM_WD * w)
    return delta, m, v


def _adamw(name, parts, w, m, v, tr=128):
    P, R, W = parts.shape
    tr = min(tr, R)

    def body(p_ref, w_ref, m_ref, v_ref, g_out, d_out, m_out, v_out):
        g = p_ref[0].astype(F32)
        for j in range(1, P):
            g = g + p_ref[j].astype(F32)
        delta, m_new, v_new = _adamw_math(w_ref[...], g, m_ref[...], v_ref[...])
        g_out[...] = g
        d_out[...] = delta
        m_out[...] = m_new
        v_out[...] = v_new

    row = pl.BlockSpec((tr, W), lambda i: (i, 0))
    shape = jax.ShapeDtypeStruct((R, W), F32)
    return pl.pallas_call(
        body, name=name, grid=(R // tr,),
        in_specs=[pl.BlockSpec((P, tr, W), lambda i: (0, i, 0)), row, row, row],
        out_specs=[row, row, row, row],
        out_shape=[shape, shape, shape, shape],
        compiler_params=_params(("parallel",)),
    )(parts, w, m, v)


def _adamw_shard(name, parts, w, m, v, layer, earlier=None, tr=256):
    L, K, N = w.shape
    tr = min(tr, K)
    n_prev = 0 if earlier is None else 4

    def body(p_ref, w_ref, m_ref, v_ref, *rest):
        g_out, d_out, m_out, v_out = rest[n_prev:]
        g = p_ref[0].astype(F32)
        for j in range(1, N_DEV):
            g = g + p_ref[j].astype(F32)
        delta, m_new, v_new = _adamw_math(w_ref[...], g, m_ref[...], v_ref[...])
        g_out[...] = g
        d_out[...] = delta
        m_out[...] = m_new
        v_out[...] = v_new

    row = pl.BlockSpec((None, tr, N), lambda i: (layer, i, 0))
    shape = jax.ShapeDtypeStruct((L, K, N), F32)
    return pl.pallas_call(
        body, name=name, grid=(K // tr,),
        in_specs=[pl.BlockSpec((N_DEV, tr, N), lambda i: (0, i, 0)), row, row, row]
        + [pl.BlockSpec(memory_space=pl.ANY)] * n_prev,
        out_specs=[row, row, row, row],
        out_shape=[shape, shape, shape, shape],
        input_output_aliases={4 + j: j for j in range(n_prev)},
        compiler_params=_params(("parallel",)),
    )(parts, w, m, v, *(earlier or ()))


def _column_slots(full):
    K, N = full.shape
    return jnp.transpose(full.reshape(K, N_DEV, N // N_DEV), (1, 0, 2))


def _from_column_slots(slots):
    _, K, n = slots.shape
    return jnp.transpose(slots, (1, 0, 2)).reshape(K, N_DEV * n)


_SMALL =("mix_norm_g", "mlp_norm_g", "final_norm_g", "a_ln_g", "a_ln_b", "a_w_s", "a_b_s", "rel_bias")


def _pack_small(vals, scalar=None):
    pieces = []
    for n in _SMALL:
        flat = vals[n].reshape(-1)
        pad = (-flat.shape[0]) % (8 * LANES)
        pieces.append(jnp.pad(flat, (0, pad)).reshape(-1, LANES))
    last = jnp.zeros((8 * LANES,), F32) if scalar is None else jnp.pad(scalar.reshape(1), (0, 8 * LANES - 1))
    pieces.append(last.reshape(8, LANES))
    return jnp.concatenate(pieces, axis=0)


def _unpack_small(buf, like):
    out = {}
    r = 0
    for n in _SMALL:
        size = like[n].size
        nrows = -(-size // (8 * LANES)) * 8
        out[n] = buf[r:r + nrows].reshape(-1)[:size].reshape(like[n].shape)
        r += nrows
    return out


_STAGES = (("gate", ("a_w_in", "a_w_out"), 0),
           ("mlp0", ("w_up", "w_down"), 0),
           ("att", ("b_w_qkv", "b_w_out"), 0),
           ("mlp1", ("w_up", "w_down"), 1))


def kernel(x, mix_norm_g, mlp_norm_g, final_norm_g, a_w_in, a_ln_g, a_ln_b, a_w_s, a_b_s, a_w_out, b_w_qkv, b_w_out, rel_bias, w_up, w_down, loss_target, m_mix_norm_g, m_mlp_norm_g, m_final_norm_g, m_a_w_in, m_a_ln_g, m_a_ln_b, m_a_w_s, m_a_b_s, m_a_w_out, m_b_w_qkv, m_b_w_out, m_rel_bias, m_w_up, m_w_down, v_mix_norm_g, v_mlp_norm_g, v_final_norm_g, v_a_w_in, v_a_ln_g, v_a_ln_b, v_a_w_s, v_a_b_s, v_a_w_out, v_b_w_qkv, v_b_w_out, v_rel_bias, v_w_up, v_w_down):
    w = dict(mix_norm_g=mix_norm_g, mlp_norm_g=mlp_norm_g, final_norm_g=final_norm_g, a_w_in=a_w_in, a_ln_g=a_ln_g,
             a_ln_b=a_ln_b, a_w_s=a_w_s, a_b_s=a_b_s, a_w_out=a_w_out, b_w_qkv=b_w_qkv, b_w_out=b_w_out,
             rel_bias=rel_bias, w_up=w_up, w_down=w_down)
    m = dict(mix_norm_g=m_mix_norm_g, mlp_norm_g=m_mlp_norm_g, final_norm_g=m_final_norm_g, a_w_in=m_a_w_in,
             a_ln_g=m_a_ln_g, a_ln_b=m_a_ln_b, a_w_s=m_a_w_s, a_b_s=m_a_b_s, a_w_out=m_a_w_out, b_w_qkv=m_b_w_qkv,
             b_w_out=m_b_w_out, rel_bias=m_rel_bias, w_up=m_w_up, w_down=m_w_down)
    v = dict(mix_norm_g=v_mix_norm_g, mlp_norm_g=v_mlp_norm_g, final_norm_g=v_final_norm_g, a_w_in=v_a_w_in,
             a_ln_g=v_a_ln_g, a_ln_b=v_a_ln_b, a_w_s=v_a_w_s, a_b_s=v_a_b_s, a_w_out=v_a_w_out, b_w_qkv=v_b_w_qkv,
             b_w_out=v_b_w_out, rel_bias=v_rel_bias, w_up=v_w_up, w_down=v_w_down)

    stages = {s: (names, layer) for s, names, layer in _STAGES}
    order = [s for s, _, _ in _STAGES]

    def shards_of(stage):
        names, layer = stages[stage]
        return [_bf(w[n][layer]) for n in names]

    pending = {}
    pending[order[0]], first_token = _exchange_start("gather_" + order[0] + "_start", shards_of(order[0]), False)

    def get_weights(stage, dep):
        gathered = _exchange_wait("gather_" + stage + "_wait", pending.pop(stage), dep)
        nxt = order.index(stage) + 1
        token = None
        if nxt < len(order):
            shards, gathered = lax.optimization_barrier((shards_of(order[nxt]), gathered))
            pending[order[nxt]], token = _exchange_start("gather_" + order[nxt] + "_start", shards, False)
        return gathered, token

    sent = {}

    def put_grads(stage, slot_grads):
        sent[stage], token = _exchange_start("scatter_" + stage + "_start", slot_grads, True)
        return token

    loss_local, grad_x, small_g = _local_step(
        x[0], loss_target[0], mix_norm_g, mlp_norm_g, final_norm_g, a_ln_g, a_ln_b, a_w_s, a_b_s, rel_bias,
        get_weights, put_grads, first_token)

    small_sent, token = _exchange_start("gather_small_start", [_pack_small(small_g, loss_local)], False)

    results = {}
    for stage in reversed(order):
        names, layer = stages[stage]
        received = _exchange_wait("scatter_" + stage + "_wait", sent[stage], token)
        for n, parts in zip(names, received):
            results[n] = _adamw_shard("adamw_%s_%s" % (stage, n), parts, w[n], m[n], v[n], layer, results.get(n))

    gathered, = _exchange_wait("gather_small_wait", small_sent, results[stages[order[0]][0][0]][0])
    bufs = _adamw("adamw_small", gathered, _pack_small(w), _pack_small(m), _pack_small(v), tr=gathered.shape[1])
    small = [_unpack_small(b, w) for b in bufs]
    loss = bufs[0][-8, 0]

    outs = []
    for j in range(4):
        outs.extend(small[j][n] if n in _SMALL else results[n][j] for n in w)
    return (loss, grad_x[None], *outs)


def _local_step(xs, tgt, mix_norm_g, mlp_norm_g, final_norm_g, a_ln_g, a_ln_b, a_w_s, a_b_s, rel_bias,
                get_weights, put_grads, first_token=None):
    D = xs.shape[-1]
    g_mix = [mix_norm_g[l][None, :] for l in range(2)]
    g_mlp = [mlp_norm_g[l][None, :] for l in range(2)]
    g_fin = final_norm_g[None, :]
    ln_g, ln_b = a_ln_g, a_ln_b
    causal = jnp.tril(jnp.ones((CHUNK, CHUNK), dtype=bool))
    wm = _bf(jnp.where(causal[None], a_w_s[0], 0.0))
    bs_full = jnp.repeat(a_b_s[0].T, D // GROUPS, axis=1)
    bias_tiles = _bias_tiles("att_bias", rel_bias, after=first_token)

    y0 = _rms_fwd("rms_mix0", xs, g_mix[0], after=bias_tiles)
    (win, wout), token = get_weights("gate", y0)
    wout = wout.reshape(-1, D)
    uvp = _mm_nn("gate_in", y0, win, tm=512, nc=win.shape[2], shards=True, after=token)
    z = _gate_fwd("gate_mid", uvp, ln_g, ln_b, wm, bs_full)
    h1, y1 = _mm_nn("gate_out", z, wout, tm=512, nc=512, epi="res", extra=xs, norm_g=g_mlp[0])
    (wup0, wdn0), token = get_weights("mlp0", h1)
    wdn0 = wdn0.reshape(-1, D)
    a0, f0 = _mm_nn("mlp0_up", y1, wup0, tm=512, nc=wup0.shape[2], epi="relu2", shards=True, after=token)
    h2, y2 = _mm_nn("mlp0_down", f0, wdn0, tm=512, nc=512, epi="res", extra=h1, norm_g=g_mix[1])
    (wqkv, wo), token = get_weights("att", h2)
    wqkv, wo = _from_column_slots(wqkv), _from_column_slots(wo)
    qkv = _mm_nn("att_qkv", y2, wqkv, tm=256, nc=512, after=token)
    o_att, lse = _att_merge("att_merge", *_att_fwd("att_fwd", qkv, bias_tiles))
    h3, y3 = _mm_nn("att_out", o_att, wo, tm=512, nc=512, epi="res", extra=h2, norm_g=g_mlp[1])
    (wup1, wdn1), _ = get_weights("mlp1", h3)
    wdn1 = wdn1.reshape(-1, D)
    a1, f1 = _mm_nn("mlp1_up", y3, wup1, tm=512, nc=wup1.shape[2], epi="relu2", shards=True)
    dh, dg_fin, err2 = _mm_res_loss("mlp1_down_loss", f1, wdn1, h3, g_fin, tgt, tm=512, nc=512)
    loss_local = 0.5 * jnp.sum(err2) / D

    def mlp_bwd(tag, dh, h_in, y, a, f, wup_l, wdn_l, g_row, after):
        da = _mm_nt(tag + "_dact", dh, wdn_l, tm=512, nc=512, epi="mask2relu", extra=a, after=after)
        g_dn = _mm_tn(tag + "_dwdown", f, dh, t1=1024, tn=1024)
        g_up = _mm_tn(tag + "_dwup", y, da, t1=1024, tn=1024, slot_cols=wup_l.shape[2])
        dh_in, dg = _mm_nt_rms_bwd(tag + "_dy", [(da, wup_l, *_whole(wup_l))], h_in, g_row, dh, tm=512, nc=512,
                                   shards=True)
        return dh_in, dg, put_grads(tag, [g_up, g_dn.reshape(N_DEV, -1, D)])

    dh3, dg_mlp1, token = mlp_bwd("mlp1", dh, h3, y3, a1, f1, wup1, wdn1, g_mlp[1], None)

    d_o = _mm_nt("att_dout", dh3, wo, tm=512, nc=512, after=token)
    g_wo = _mm_tn("att_dwo", o_att, dh3, t1=512, tn=1024)
    dq, dk, dv, ds_sums = _att_bwd("att_bwd", qkv, o_att, lse, d_o, bias_tiles)
    part_w = N_DIL * ATT_WIDTH
    g_qkv = [_mm_tn("att_dwqkv%d" % p, y2, t, t1=1024, tn=part_w) for p, t in enumerate((dq, dk, dv))]
    dh2, dg_mix1 = _mm_nt_rms_bwd("att_dy", [(t, wqkv, (D, part_w), (0, p)) for p, t in enumerate((dq, dk, dv))],
                                  h2, g_mix[1], dh3, tm=512, nc=512)
    token = put_grads("att", [_column_slots(jnp.concatenate(g_qkv, axis=1)), _column_slots(g_wo)])

    dh1, dg_mlp0, token = mlp_bwd("mlp0", dh2, h1, y1, a0, f0, wup0, wdn0, g_mlp[0], token)

    dz = _mm_nt("gate_dz", dh1, wout, tm=512, nc=512, after=token)
    g_wout = _mm_tn("gate_dwout", z, dh1, t1=1024, tn=1024)
    duvp, d_wm, d_mixed, d_lng, d_lnb = _gate_bwd("gate_dmid", uvp, dz, ln_g, ln_b, wm, bs_full)
    g_win = _mm_tn("gate_dwin", y0, duvp, t1=1024, tn=1024, slot_cols=win.shape[2])
    token = put_grads("gate", [g_win, g_wout.reshape(N_DEV, -1, D)])
    grad_x, dg_mix0 = _mm_nt_rms_bwd("gate_dy", [(duvp, win, *_whole(win))], xs, g_mix[0], dh1, tm=512, nc=512,
                                     after=token, shards=True)

    small_g = dict(
        mix_norm_g=jnp.concatenate([dg_mix0, dg_mix1], axis=0),
        mlp_norm_g=jnp.concatenate([dg_mlp0, dg_mlp1], axis=0),
        final_norm_g=dg_fin[0], a_ln_g=d_lng, a_ln_b=d_lnb, a_w_s=d_wm[None],
        a_b_s=jnp.sum(d_mixed.reshape(CHUNK, GROUPS, D // GROUPS), axis=2).T[None],
        rel_bias=_bias_grad("att_dbias", ds_sums))
    return loss_local, grad_x, small_g
```

```python
import functools
import math

import jax
import jax.numpy as jnp
from jax import lax
from jax.experimental import pallas as pl
from jax.experimental.pallas import tpu as pltpu

F32 = jnp.float32
BF16 = jnp.bfloat16
MESH = pl.DeviceIdType.MESH

N_DEV = 8
EPS = 1e-6
NEG_INF = -1e30
CHUNK = 128
GROUPS = 8
HEAD_DIM = 64
ATT_HEADS = 8
ATT_WIDTH = ATT_HEADS * HEAD_DIM
DILATIONS = (1, 4, 16)
N_DIL = len(DILATIONS)
N_BUCKETS = 32
MAX_EXACT = N_BUCKETS // 2
REL_MAX_DISTANCE = 2048
ATT_ROWS = 2048
ATT_SCALE = HEAD_DIM ** -0.5
LANES = 128

ADAM_LR = 0.001
ADAM_B1 = 0.9
ADAM_B2 = 0.999
ADAM_EPS = 1e-08
ADAM_WD = 0.01
ADAM_STEP = 10

VMEM_LIMIT_BYTES = 56 * 1024 * 1024


def _params(semantics=None):
    return pltpu.CompilerParams(dimension_semantics=semantics, vmem_limit_bytes=VMEM_LIMIT_BYTES)


def _bf(v):
    return v.astype(BF16)


def _dot(a, b, dims):
    return lax.dot_general(a, b, (dims, ((), ())), preferred_element_type=F32)


NN = ((1,), (0,))
NT = ((1,), (1,))
TN = ((0,), (0,))


def _after_operand(after):
    if after is None:
        return [], []
    return [after], [pl.BlockSpec(memory_space=pl.ANY)]


def _rms_fwd(name, x, g, tm=512, after=None):
    S, D = x.shape
    after_args, after_specs = _after_operand(after)

    def body(x_ref, g_ref, *rest):
        y_ref = rest[-1]
        xv = x_ref[...]
        r = lax.rsqrt(jnp.mean(xv * xv, axis=-1, keepdims=True) + EPS)
        y_ref[...] = _bf(xv * r * g_ref[...])

    return pl.pallas_call(
        body, name=name, grid=(S // tm,),
        in_specs=[pl.BlockSpec((tm, D), lambda i: (i, 0)), pl.BlockSpec((1, D), lambda i: (0, 0))] + after_specs,
        out_specs=pl.BlockSpec((tm, D), lambda i: (i, 0)),
        out_shape=jax.ShapeDtypeStruct((S, D), BF16),
        compiler_params=_params(("parallel",)),
    )(x, g, *after_args)


def _mm_res_loss(name, a, w, res, g, target, *, tm, nc):
    M, D = res.shape

    def body(a_ref, w_ref, r_ref, g_ref, t_ref, dh_ref, dg_ref, l_ref, h_sc):
        i = pl.program_id(0)
        a_v = _bf(a_ref[...])
        for j in range(D // nc):
            cols, acc = _chunk_product([a_v], [w_ref], j, nc, False, False)
            h_sc[:, cols] = r_ref[:, cols] + acc
        xv = h_sc[...]
        r = lax.rsqrt(jnp.mean(xv * xv, axis=-1, keepdims=True) + EPS)
        xh = xv * r
        gv = g_ref[...]
        e = xh * gv - t_ref[...]
        dout = e / D
        dyg = dout * gv
        c = jnp.mean(dyg * xh, axis=-1, keepdims=True)
        dh_ref[...] = r * (dyg - xh * c)
        dg_part = jnp.sum(dout * xh, axis=0, keepdims=True)
        l_part = jnp.sum(e * e, axis=0, keepdims=True)

        @pl.when(i == 0)
        def _():
            dg_ref[...] = dg_part
            l_ref[...] = l_part

        @pl.when(i > 0)
        def _():
            dg_ref[...] += dg_part
            l_ref[...] += l_part

    row = pl.BlockSpec((tm, D), lambda i: (i, 0))
    vec = pl.BlockSpec((1, D), lambda i: (0, 0))
    return pl.pallas_call(
        body, name=name, grid=(M // tm,),
        in_specs=[pl.BlockSpec((tm, a.shape[1]), lambda i: (i, 0)), pl.BlockSpec(w.shape, lambda i: (0, 0)),
                  row, vec, row],
        out_specs=[row, vec, vec],
        out_shape=[jax.ShapeDtypeStruct((M, D), F32), jax.ShapeDtypeStruct((1, D), F32),
                   jax.ShapeDtypeStruct((1, D), F32)],
        scratch_shapes=[pltpu.VMEM((tm, D), F32)],
        compiler_params=_params(("arbitrary",)),
    )(a, w, res, g, target)


def _chunk_product(a_vals, w_refs, j, nc, nt, shards):
    cols = slice(j * nc, (j + 1) * nc)
    acc = None
    for a_v, w_ref in zip(a_vals, w_refs):
        if not shards:
            terms = [_dot(a_v, w_ref[cols, :], NT) if nt else _dot(a_v, w_ref[:, cols], NN)]
        elif nt:
            nl = w_ref.shape[2]
            terms = [_dot(a_v[:, k * nl:(k + 1) * nl], w_ref[k, cols, :], NT) for k in range(N_DEV)]
        else:
            terms = [_dot(a_v, w_ref[j], NN)]
        for t in terms:
            acc = t if acc is None else acc + t
    return cols, acc


def _mm_rows(name, pairs, n_out, *, nt, tm, nc, epi="plain", extra=None, out_dtype=F32, after=None, shards=False,
             norm_g=None):
    M = pairs[0][0].shape[0]
    np_ = len(pairs)
    after_args, after_specs = _after_operand(after)

    def body(*refs):
        a_refs = refs[:np_]
        w_refs = refs[np_:2 * np_]
        pos = 2 * np_
        e_ref = None
        if extra is not None:
            e_ref = refs[pos]
            pos += 1
        if norm_g is not None:
            g_ref = refs[pos]
            pos += 1
        pos += len(after_args)
        outs = refs[pos:]
        a_vals = [_bf(a[...]) for a in a_refs]
        for j in range(n_out // nc):
            cols, acc = _chunk_product(a_vals, w_refs, j, nc, nt, shards)
            if epi == "plain":
                outs[0][:, cols] = acc.astype(out_dtype)
            elif epi == "res":
                outs[0][:, cols] = e_ref[:, cols] + acc
            elif epi == "relu2":
                outs[0][:, cols] = _bf(acc)
                rl = jnp.maximum(acc, 0.0)
                outs[1][:, cols] = _bf(rl * rl)
            elif epi == "mask2relu":
                outs[0][:, cols] = _bf(acc * (2.0 * jnp.maximum(e_ref[:, cols].astype(F32), 0.0)))
        if norm_g is not None:
            hv = outs[0][...]
            r = lax.rsqrt(jnp.mean(hv * hv, axis=-1, keepdims=True) + EPS)
            outs[1][...] = _bf(hv * r * g_ref[...])

    in_specs = [pl.BlockSpec((tm, a.shape[1]), lambda i: (i, 0)) for a, _, _, _ in pairs]
    for _, _, wshape, widx in pairs:
        in_specs.append(pl.BlockSpec(wshape, functools.partial(lambda i, widx: widx, widx=widx)))
    args = [a for a, _, _, _ in pairs] + [w for _, w, _, _ in pairs]
    if extra is not None:
        in_specs.append(pl.BlockSpec((tm, n_out), lambda i: (i, 0)))
        args.append(extra)
    if norm_g is not None:
        in_specs.append(pl.BlockSpec((1, n_out), lambda i: (0, 0)))
        args.append(norm_g)
    in_specs += after_specs
    args += after_args
    row_out = pl.BlockSpec((tm, n_out), lambda i: (i, 0))
    if epi == "relu2":
        out_specs = [row_out, row_out]
        out_shape = [jax.ShapeDtypeStruct((M, n_out), BF16), jax.ShapeDtypeStruct((M, n_out), BF16)]
    elif norm_g is not None:
        out_specs = [row_out, row_out]
        out_shape = [jax.ShapeDtypeStruct((M, n_out), F32), jax.ShapeDtypeStruct((M, n_out), BF16)]
    else:
        dt = BF16 if epi == "mask2relu" else (F32 if epi == "res" else out_dtype)
        out_specs = row_out
        out_shape = jax.ShapeDtypeStruct((M, n_out), dt)
    return pl.pallas_call(
        body, name=name, grid=(M // tm,), in_specs=in_specs, out_specs=out_specs, out_shape=out_shape,
        compiler_params=_params(("parallel",)),
    )(*args)


def _whole(w):
    return w.shape, (0,) * w.ndim


def _mm_nn(name, a, w, **kw):
    n_out = w.shape[0] * w.shape[2] if w.ndim == 3 else w.shape[1]
    return _mm_rows(name, [(a, w, *_whole(w))], n_out, nt=False, **kw)


def _mm_nt(name, a, w, **kw):
    return _mm_rows(name, [(a, w, *_whole(w))], w.shape[0], nt=True, **kw)


def _mm_nt_rms_bwd(name, pairs, x, g, dres, *, tm, nc, after=None, shards=False):
    M, D = x.shape
    np_ = len(pairs)
    after_args, after_specs = _after_operand(after)

    def body(*refs):
        a_refs = refs[:np_]
        w_refs = refs[np_:2 * np_]
        x_ref, g_ref, r_ref = refs[2 * np_:2 * np_ + 3]
        dx_ref, dg_ref, dy_sc = refs[-3:]
        i = pl.program_id(0)
        a_vals = [_bf(a[...]) for a in a_refs]
        for j in range(D // nc):
            cols, acc = _chunk_product(a_vals, w_refs, j, nc, True, shards)
            dy_sc[:, cols] = acc
        xv = x_ref[...]
        r = lax.rsqrt(jnp.mean(xv * xv, axis=-1, keepdims=True) + EPS)
        xh = xv * r
        dy_v = dy_sc[...]
        dyg = dy_v * g_ref[...]
        c = jnp.mean(dyg * xh, axis=-1, keepdims=True)
        dx_ref[...] = r_ref[...] + r * (dyg - xh * c)
        part = jnp.sum(dy_v * xh, axis=0, keepdims=True)

        @pl.when(i == 0)
        def _():
            dg_ref[...] = part

        @pl.when(i > 0)
        def _():
            dg_ref[...] += part

    row = pl.BlockSpec((tm, D), lambda i: (i, 0))
    vec = pl.BlockSpec((1, D), lambda i: (0, 0))
    in_specs = [pl.BlockSpec((tm, a.shape[1]), lambda i: (i, 0)) for a, _, _, _ in pairs]
    for _, _, wshape, widx in pairs:
        in_specs.append(pl.BlockSpec(wshape, functools.partial(lambda i, widx: widx, widx=widx)))
    args = [a for a, _, _, _ in pairs] + [w for _, w, _, _ in pairs]
    return pl.pallas_call(
        body, name=name, grid=(M // tm,),
        in_specs=in_specs + [row, vec, row] + after_specs,
        out_specs=[row, vec],
        out_shape=[jax.ShapeDtypeStruct((M, D), F32), jax.ShapeDtypeStruct((1, D), F32)],
        scratch_shapes=[pltpu.VMEM((tm, D), F32)],
        compiler_params=_params(("arbitrary",)),
    )(*args, x, g, dres, *after_args)


def _mm_tn(name, a, b, *, t1, tn, tm=2048, slot_cols=None):
    M, K1 = a.shape
    N = b.shape[1]
    nm = M // tm

    def body(a_ref, b_ref, o_ref, acc_ref):
        m = pl.program_id(2)
        t = _dot(_bf(a_ref[...]), _bf(b_ref[...]), TN)

        @pl.when(m == 0)
        def _():
            acc_ref[...] = t

        @pl.when(m > 0)
        def _():
            acc_ref[...] += t

        @pl.when(m == nm - 1)
        def _():
            if slot_cols is None:
                o_ref[...] = _bf(acc_ref[...])
            else:
                for k in range(tn // slot_cols):
                    o_ref[k] = _bf(acc_ref[:, k * slot_cols:(k + 1) * slot_cols])

    if slot_cols is not None:
        out_spec = pl.BlockSpec((tn // slot_cols, t1, slot_cols), lambda i, j, m: (j, i, 0))
        out_shape = jax.ShapeDtypeStruct((N // slot_cols, K1, slot_cols), BF16)
    else:
        out_spec = pl.BlockSpec((t1, tn), lambda i, j, m: (i, j))
        out_shape = jax.ShapeDtypeStruct((K1, N), BF16)
    return pl.pallas_call(
        body, name=name, grid=(K1 // t1, N // tn, nm),
        in_specs=[pl.BlockSpec((tm, t1), lambda i, j, m: (m, i)), pl.BlockSpec((tm, tn), lambda i, j, m: (m, j))],
        out_specs=out_spec, out_shape=out_shape,
        scratch_shapes=[pltpu.VMEM((t1, tn), F32)],
        compiler_params=_params(("parallel", "parallel", "arbitrary")),
    )(a, b)


_INV_SQRT2 = 1.0 / math.sqrt(2.0)
_INV_SQRT2PI = 1.0 / math.sqrt(2.0 * math.pi)


def _gelu(x):
    return 0.5 * x * (1.0 + lax.erf(x * _INV_SQRT2))


def _gelu_and_grad(x):
    cdf = 0.5 * (1.0 + lax.erf(x * _INV_SQRT2))
    return x * cdf, cdf + x * (_INV_SQRT2PI * jnp.exp(-0.5 * x * x))


def _layer_norm_parts(v):
    mu = jnp.mean(v, axis=-1, keepdims=True)
    xc = v - mu
    rs = lax.rsqrt(jnp.mean(xc * xc, axis=-1, keepdims=True) + EPS)
    return xc * rs, rs


def _gate_fwd(name, uvp, ln_g, ln_b, wm, bs_full, tr=512):
    S, W2 = uvp.shape
    W = W2 // 2
    gd = W // GROUPS

    def body(u_ref, v_ref, lg_ref, lb_ref, wm_ref, bs_ref, z_ref):
        vh, _ = _layer_norm_parts(_gelu(v_ref[...]))
        vn = _bf(vh * lg_ref[...] + lb_ref[...])
        for ci in range(tr // CHUNK):
            rows = slice(ci * CHUNK, (ci + 1) * CHUNK)
            for g in range(GROUPS):
                cols = slice(g * gd, (g + 1) * gd)
                mixed = _dot(wm_ref[g], vn[rows, cols], NN) + bs_ref[:, cols]
                z_ref[rows, cols] = _bf(_gelu(u_ref[rows, cols]) * mixed)

    vec = pl.BlockSpec((1, W), lambda i: (0, 0))
    return pl.pallas_call(
        body, name=name, grid=(S // tr,),
        in_specs=[pl.BlockSpec((tr, W), lambda i: (i, 0)), pl.BlockSpec((tr, W), lambda i: (i, 1)), vec, vec,
                  pl.BlockSpec((GROUPS, CHUNK, CHUNK), lambda i: (0, 0, 0)),
                  pl.BlockSpec((CHUNK, W), lambda i: (0, 0))],
        out_specs=pl.BlockSpec((tr, W), lambda i: (i, 0)),
        out_shape=jax.ShapeDtypeStruct((S, W), BF16),
        compiler_params=_params(("parallel",)),
    )(uvp, uvp, ln_g, ln_b, wm, bs_full)


def _gate_bwd(name, uvp, dz, ln_g, ln_b, wm, bs_full, tr=256):
    S, W2 = uvp.shape
    W = W2 // 2
    gd = W // GROUPS
    n_steps = S // tr

    def body(u_ref, v_ref, dz_ref, lg_ref, lb_ref, wm_ref, bs_ref, duv_ref, dwm_ref, dmx_ref, dlg_ref, dlb_ref,
             dvn_ref):
        i = pl.program_id(0)
        v, dv_dvp = _gelu_and_grad(v_ref[...])
        vh, rs = _layer_norm_parts(v)
        lg = lg_ref[...]
        vn = _bf(vh * lg + lb_ref[...])

        @pl.when(i == 0)
        def _():
            dwm_ref[...] = jnp.zeros_like(dwm_ref)
            dmx_ref[...] = jnp.zeros_like(dmx_ref)
            dlg_ref[...] = jnp.zeros_like(dlg_ref)
            dlb_ref[...] = jnp.zeros_like(dlb_ref)

        for ci in range(tr // CHUNK):
            rows = slice(ci * CHUNK, (ci + 1) * CHUNK)
            for g in range(GROUPS):
                cols = slice(g * gd, (g + 1) * gd)
                u, du_dup = _gelu_and_grad(u_ref[rows, cols])
                dz_v = dz_ref[rows, cols]
                dmixed = dz_v * u
                dmx_ref[:, cols] += dmixed
                dmixed_b = _bf(dmixed)
                mixed = _dot(wm_ref[g], vn[rows, cols], NN) + bs_ref[:, cols]
                duv_ref[rows, cols] = _bf(dz_v * mixed * du_dup)
                dwm_ref[g] += _dot(dmixed_b, vn[rows, cols], NT)
                dvn_ref[rows, cols] = _dot(wm_ref[g], dmixed_b, TN)
        dvn = dvn_ref[...]
        dlg_ref[...] += jnp.sum(dvn * vh, axis=0, keepdims=True)
        dlb_ref[...] += jnp.sum(dvn, axis=0, keepdims=True)
        dvh = dvn * lg
        dv = rs * (dvh - jnp.mean(dvh, axis=-1, keepdims=True) - vh * jnp.mean(dvh * vh, axis=-1, keepdims=True))
        duv_ref[:, W:] = _bf(dv * dv_dvp)

        @pl.when(i == n_steps - 1)
        def _():
            t_idx = lax.broadcasted_iota(jnp.int32, (CHUNK, CHUNK), 0)
            s_idx = lax.broadcasted_iota(jnp.int32, (CHUNK, CHUNK), 1)
            keep = (s_idx <= t_idx).astype(F32)
            for g in range(GROUPS):
                dwm_ref[g] = dwm_ref[g] * keep

    vec = pl.BlockSpec((1, W), lambda i: (0, 0))
    row = pl.BlockSpec((tr, W), lambda i: (i, 0))
    return pl.pallas_call(
        body, name=name, grid=(n_steps,),
        in_specs=[row, pl.BlockSpec((tr, W), lambda i: (i, 1)), row, vec, vec,
                  pl.BlockSpec((GROUPS, CHUNK, CHUNK), lambda i: (0, 0, 0)),
                  pl.BlockSpec((CHUNK, W), lambda i: (0, 0))],
        out_specs=[pl.BlockSpec((tr, W2), lambda i: (i, 0)),
                   pl.BlockSpec((GROUPS, CHUNK, CHUNK), lambda i: (0, 0, 0)),
                   pl.BlockSpec((CHUNK, W), lambda i: (0, 0)), vec, vec],
        out_shape=[jax.ShapeDtypeStruct((S, W2), BF16), jax.ShapeDtypeStruct((GROUPS, CHUNK, CHUNK), F32),
                   jax.ShapeDtypeStruct((CHUNK, W), F32), jax.ShapeDtypeStruct((1, W), F32),
                   jax.ShapeDtypeStruct((1, W), F32)],
        scratch_shapes=[pltpu.VMEM((tr, W), F32)],
        compiler_params=_params(("arbitrary",)),
    )(uvp, uvp, dz, ln_g, ln_b, wm, bs_full)


def _t5_bucket(distance):
    small = distance < MAX_EXACT
    nf = jnp.maximum(distance, 1).astype(F32)
    large = MAX_EXACT + (jnp.log(nf / MAX_EXACT) / math.log(REL_MAX_DISTANCE / MAX_EXACT)
                         * (N_BUCKETS - MAX_EXACT)).astype(jnp.int32)
    large = jnp.minimum(large, N_BUCKETS - 1)
    return jnp.where(small, distance, large)


TILE_ELEMS = 2 * CHUNK * CHUNK


def _band_buckets():
    rel = CHUNK + jnp.arange(CHUNK)[None, :] - jnp.arange(2 * CHUNK)[:, None]
    band = (rel >= 0) & (rel <= CHUNK)
    buckets = [_t5_bucket(jnp.clip(rel, 0, CHUNK) * d) for d in DILATIONS]
    return jnp.stack(buckets), band


def _bucket_onehot():
    buckets, _ = _band_buckets()
    return (buckets.reshape(N_DIL, 1, TILE_ELEMS) == jnp.arange(N_BUCKETS)[None, :, None]).astype(F32)


def _bias_tiles(name, rel_bias, after=None):
    _, band = _band_buckets()
    own = band & (jnp.arange(2 * CHUNK) >= CHUNK)[:, None]
    masks = jnp.stack([own, band]).reshape(2, TILE_ELEMS).astype(F32)
    tables = jnp.transpose(rel_bias.reshape(N_BUCKETS, N_DIL, ATT_HEADS), (1, 2, 0))
    after_args, after_specs = _after_operand(after)

    def body(t_ref, oh_ref, m_ref, *rest):
        out_ref = rest[-1]
        for g in range(N_DIL):
            bias = lax.dot_general(t_ref[g], oh_ref[g], (NN, ((), ())), precision=lax.Precision.HIGHEST,
                                   preferred_element_type=F32)
            for f in range(2):
                out_ref[g, f] = jnp.where(m_ref[f:f + 1, :] > 0.5, bias, NEG_INF)

    whole = pl.BlockSpec(memory_space=pltpu.VMEM)
    out = pl.pallas_call(
        body, name=name, out_shape=jax.ShapeDtypeStruct((N_DIL, 2, ATT_HEADS, TILE_ELEMS), F32),
        in_specs=[whole, whole, whole] + after_specs, out_specs=whole,
        compiler_params=_params(),
    )(tables, _bucket_onehot(), masks, *after_args)
    out = out.reshape(N_DIL, 2, ATT_HEADS // 2, 2, 2 * CHUNK, CHUNK)
    return jnp.transpose(out, (0, 1, 2, 4, 3, 5)).reshape(N_DIL, 2, ATT_HEADS // 2, 2 * CHUNK, 2 * CHUNK)


def _att_specs(order):
    def column(part, ids):
        hp, g, _ = order(*ids)
        return part * 3 * 4 + g * 4 + hp

    def window(part):
        def index(*ids):
            c = order(*ids)[2]
            return pl.multiple_of(jnp.maximum(c - 1, 0) * ATT_ROWS, ATT_ROWS), column(part, ids) * LANES
        return pl.BlockSpec((pl.Element(2 * ATT_ROWS), pl.Element(LANES)), index)

    return [pl.BlockSpec((ATT_ROWS, LANES), lambda *ids: (order(*ids)[2], column(0, ids))), window(1), window(2)]


def _window_base(c):
    return jnp.where(c == 0, 0, ATT_ROWS)


def _rows(start, d):
    if d == 1:
        return pl.ds(pl.multiple_of(start, CHUNK), CHUNK)
    return pl.ds(start, CHUNK, stride=d)


def _att_tile_offsets(t, d):
    n = t // d
    r = t % d
    return n * (CHUNK * d) + r, n


def _head_pair_columns(x_t):
    zeros = jnp.zeros((HEAD_DIM, CHUNK), x_t.dtype)
    return jnp.concatenate([jnp.concatenate([x_t[:HEAD_DIM], zeros], axis=0),
                            jnp.concatenate([zeros, x_t[HEAD_DIM:]], axis=0)], axis=1)


def _head_pair_rows(y):
    return jnp.concatenate([y[:HEAD_DIM, :CHUNK], y[HEAD_DIM:, CHUNK:]], axis=0)


def _att_fwd(name, qkv, bias_tiles):
    S = qkv.shape[0]
    n_chunks = S // ATT_ROWS
    tiles = ATT_ROWS // CHUNK

    def body(q_ref, kk, vv, b_ref, o_ref, l_ref):
        c = pl.program_id(1)
        g = pl.program_id(2)
        base = _window_base(c)

        for gi, d in enumerate(DILATIONS):
            @pl.when(g == gi)
            def _(d=d):
                span = CHUNK * d

                def tile(t, carry):
                    q0, n = _att_tile_offsets(t, d)
                    first = (c == 0) & (n == 0)
                    rows = _rows(q0, d)
                    cur = _rows(base + q0, d)
                    prev = _rows(jnp.where(first, q0, base + q0 - span), d)
                    inner = jnp.where(first, 0, 1)
                    qq = _head_pair_columns(_bf(q_ref[rows, :] * ATT_SCALE).T)
                    s_p = _dot(_bf(kk[prev, :]), qq, NN) + b_ref[inner, 0:CHUNK, :]
                    s_c = _dot(_bf(kk[cur, :]), qq, NN) + b_ref[inner, CHUNK:2 * CHUNK, :]
                    m = jnp.maximum(jnp.max(s_p, axis=0, keepdims=True), jnp.max(s_c, axis=0, keepdims=True))
                    p_p = jnp.exp(s_p - m)
                    p_c = jnp.exp(s_c - m)
                    l = jnp.sum(p_p, axis=0, keepdims=True) + jnp.sum(p_c, axis=0, keepdims=True)
                    o2 = (_dot(_bf(vv[prev, :]).T, _bf(p_p), NN)
                          + _dot(_bf(vv[cur, :]).T, _bf(p_c), NN)) * (1.0 / l)
                    lse = m + jnp.log(l)
                    l_t = jnp.concatenate([jnp.broadcast_to(lse[:, :CHUNK], (HEAD_DIM, CHUNK)),
                                           jnp.broadcast_to(lse[:, CHUNK:], (HEAD_DIM, CHUNK))], axis=0)
                    o_ref[rows, :] = _head_pair_rows(o2).T
                    l_ref[rows, :] = l_t.T
                    return carry

                lax.fori_loop(0, tiles, tile, 0, unroll=8)

    order = lambda hp, c, g: (hp, g, c)
    out_spec = pl.BlockSpec((None, ATT_ROWS, LANES), lambda hp, c, g: (g, c, hp))
    shape = jax.ShapeDtypeStruct((N_DIL, S, ATT_WIDTH), F32)
    return pl.pallas_call(
        body, name=name, grid=(ATT_HEADS // 2, n_chunks, N_DIL),
        in_specs=_att_specs(order) + [
            pl.BlockSpec((None, 2, None, 2 * CHUNK, 2 * CHUNK), lambda hp, c, g: (g, 0, hp, 0, 0))],
        out_specs=[out_spec, out_spec],
        out_shape=[shape, shape],
        compiler_params=_params(("parallel", "parallel", "parallel")),
    )(qkv, qkv, qkv, bias_tiles)


def _att_merge(name, o_g, l_g, tm=512):
    _, S, W = o_g.shape

    def body(o_ref, l_ref, out_ref, lse_ref):
        ls = [l_ref[g] for g in range(N_DIL)]
        mx = functools.reduce(jnp.maximum, ls)
        ws = [jnp.exp(l - mx) for l in ls]
        tot = functools.reduce(lambda a, b: a + b, ws)
        acc = ws[0] * o_ref[0]
        for g in range(1, N_DIL):
            acc = acc + ws[g] * o_ref[g]
        out_ref[...] = acc / tot
        lse_ref[...] = mx + jnp.log(tot)

    blk = pl.BlockSpec((N_DIL, tm, W), lambda i: (0, i, 0))
    row = pl.BlockSpec((tm, W), lambda i: (i, 0))
    shape = jax.ShapeDtypeStruct((S, W), F32)
    return pl.pallas_call(
        body, name=name, grid=(S // tm,), in_specs=[blk, blk], out_specs=[row, row], out_shape=[shape, shape],
        compiler_params=_params(("parallel",)),
    )(o_g, l_g)


def _att_bwd(name, qkv, o, lse, d_o, bias_tiles):
    S = qkv.shape[0]
    n_chunks = S // ATT_ROWS
    tiles = ATT_ROWS // CHUNK

    def body(q_ref, kk, vv, o_ref, l_ref, do_ref, b_ref, dq_out, dk_out, dv_out, ds_ref, dq_ref, dk_ref, dv_ref):
        g = pl.program_id(1)
        c = pl.program_id(2)

        @pl.when(c == 0)
        def _():
            dk_ref[...] = jnp.zeros_like(dk_ref)
            dv_ref[...] = jnp.zeros_like(dv_ref)
            ds_ref[...] = jnp.zeros_like(ds_ref)

        base = _window_base(c)
        first_row = c * ATT_ROWS
        head0 = lax.broadcasted_iota(jnp.int32, (CHUNK, LANES), 1) < HEAD_DIM

        def head_pair_stack(x):
            zero = jnp.zeros_like(x)
            return jnp.concatenate([jnp.where(head0, x, zero), jnp.where(head0, zero, x)], axis=0)

        for gi, d in enumerate(DILATIONS):
            @pl.when(g == gi)
            def _(d=d):
                span = CHUNK * d

                def tile(t, carry):
                    q0, n = _att_tile_offsets(t, d)
                    first = (c == 0) & (n == 0)
                    rows = _rows(q0, d)
                    cur = _rows(base + q0, d)
                    prev = _rows(jnp.where(first, q0, base + q0 - span), d)
                    inner = jnp.where(first, 0, 1)
                    g_cur = _rows(first_row + q0, d)
                    g_prev = _rows(jnp.where(first, q0, first_row + q0 - span), d)
                    q2 = _bf(q_ref[rows, :] * ATT_SCALE)
                    q_t = q2.T
                    k2 = _bf(jnp.concatenate([kk[prev, :], kk[cur, :]], axis=0))
                    k_t = k2.T
                    v2 = _bf(jnp.concatenate([vv[prev, :], vv[cur, :]], axis=0))
                    do2 = do_ref[rows, :]
                    do_b = _bf(do2)
                    do_t = do_b.T
                    lse_t = l_ref[rows, :].T
                    dd_t = (do2 * o_ref[rows, :]).T
                    lse = jnp.concatenate([lse_t[0:1], lse_t[HEAD_DIM:HEAD_DIM + 1]], axis=1)
                    delta = jnp.concatenate([jnp.sum(dd_t[:HEAD_DIM], axis=0, keepdims=True),
                                             jnp.sum(dd_t[HEAD_DIM:], axis=0, keepdims=True)], axis=1)
                    s = _dot(k2, _head_pair_columns(q_t), NN) + b_ref[inner]
                    p = jnp.exp(s - lse)
                    ds = p * (_dot(v2, _head_pair_columns(do_t), NN) - delta)
                    ds_ref[...] += ds
                    ds_b = _bf(ds)
                    dq_t = _head_pair_rows(_dot(k_t, ds_b, NN))
                    dk2 = _dot(ds_b, head_pair_stack(q2), NN)
                    dv2 = _dot(_bf(p), head_pair_stack(do_b), NN)
                    dq_ref[rows, :] = (dq_t * ATT_SCALE).T
                    dk_ref[g_prev, :] += dk2[0:CHUNK]
                    dk_ref[g_cur, :] += dk2[CHUNK:2 * CHUNK]
                    dv_ref[g_prev, :] += dv2[0:CHUNK]
                    dv_ref[g_cur, :] += dv2[CHUNK:2 * CHUNK]
                    return carry

                lax.fori_loop(0, tiles, tile, 0, unroll=4)

        dq_out[...] = _bf(dq_ref[...])

        @pl.when(c == n_chunks - 1)
        def _():
            dk_out[...] = _bf(dk_ref[...])
            dv_out[...] = _bf(dv_ref[...])

    order = lambda hp, g, c: (hp, g, c)
    chunk = pl.BlockSpec((ATT_ROWS, LANES), lambda hp, g, c: (c, hp))
    slab = pl.BlockSpec((S, LANES), lambda hp, g, c: (0, g * 4 + hp))
    width = N_DIL * ATT_WIDTH
    dq, dk, dv, ds_sums = pl.pallas_call(
        body, name=name, grid=(ATT_HEADS // 2, N_DIL, n_chunks),
        in_specs=_att_specs(order) + [chunk, chunk, chunk,
                                      pl.BlockSpec((None, 2, None, 2 * CHUNK, 2 * CHUNK),
                                                   lambda hp, g, c: (g, 0, hp, 0, 0))],
        out_specs=[pl.BlockSpec((ATT_ROWS, LANES), lambda hp, g, c: (c, g * 4 + hp)), slab, slab,
                   pl.BlockSpec((None, None, 2 * CHUNK, 2 * CHUNK), lambda hp, g, c: (g, hp, 0, 0))],
        out_shape=[jax.ShapeDtypeStruct((S, width), BF16), jax.ShapeDtypeStruct((S, width), BF16),
                   jax.ShapeDtypeStruct((S, width), BF16),
                   jax.ShapeDtypeStruct((N_DIL, ATT_HEADS // 2, 2 * CHUNK, 2 * CHUNK), F32)],
        scratch_shapes=[pltpu.VMEM((ATT_ROWS, LANES), F32), pltpu.VMEM((S, LANES), F32),
                        pltpu.VMEM((S, LANES), F32)],
        compiler_params=_params(("parallel", "parallel", "arbitrary")),
    )(qkv, qkv, qkv, o, lse, d_o, bias_tiles)
    ds_sums = ds_sums.reshape(N_DIL, ATT_HEADS // 2, 2 * CHUNK, 2, CHUNK)
    ds_sums = jnp.transpose(ds_sums, (0, 1, 3, 2, 4)).reshape(N_DIL, ATT_HEADS, 2 * CHUNK, CHUNK)
    return dq, dk, dv, ds_sums


def _bias_grad(name, ds_sums):
    flat = ds_sums.reshape(N_DIL, ATT_HEADS, TILE_ELEMS)

    def body(oh_ref, ds_ref, out_ref):
        for g in range(N_DIL):
            out_ref[g] = lax.dot_general(oh_ref[g], ds_ref[g], (NT, ((), ())), precision=lax.Precision.HIGHEST,
                                         preferred_element_type=F32)

    out = pl.pallas_call(
        body, name=name, out_shape=jax.ShapeDtypeStruct((N_DIL, N_BUCKETS, ATT_HEADS), F32),
        compiler_params=_params(),
    )(_bucket_onehot(), flat)
    return jnp.transpose(out, (1, 0, 2)).reshape(N_BUCKETS, N_DIL * ATT_HEADS)


def _peers():
    x, y, c = lax.axis_index("x"), lax.axis_index("y"), lax.axis_index("c")
    me = 4 * x + 2 * y + c
    others = [(x, y, 1 - c), (1 - x, y, c), (x, 1 - y, c), (1 - x, 1 - y, c),
              (1 - x, y, 1 - c), (x, 1 - y, 1 - c), (1 - x, 1 - y, 1 - c)]
    return me, others


def _slot(dev):
    return 4 * dev[0] + 2 * dev[1] + dev[2]


_HBM =pl.BlockSpec(memory_space=pltpu.HBM)
_SEM = pl.BlockSpec(memory_space=pltpu.SEMAPHORE)
_EFFECT = pltpu.SideEffectType.DATAFLOW_SIDE_EFFECTING


def _my_slot():
    return 4 * lax.axis_index("x") + 2 * lax.axis_index("y") + lax.axis_index("c")


def _exchange_copy(src_ref, land_ref, send_sems, recv_sems, k, dev, me, scatter, arriving):
    src = src_ref.at[me if arriving else _slot(dev)] if scatter else src_ref
    dst = land_ref.at[_slot(dev) if arriving else me]
    return pltpu.make_async_remote_copy(src_ref=src, dst_ref=dst, send_sem=send_sems.at[k], recv_sem=recv_sems.at[k],
                                        device_id=dev, device_id_type=MESH)


def _exchange_start(name, srcs, scatter):
    n = len(srcs)
    me = _my_slot()
    landings = []
    for src in srcs:
        own = lax.dynamic_index_in_dim(src, me, 0, keepdims=True) if scatter else src[None]
        landings.append(lax.dynamic_update_slice(lax.empty((N_DEV,) + src.shape[-2:], src.dtype), own, (me, 0, 0)))

    def body(*refs):
        src_refs, land_refs = refs[:n], refs[n:2 * n]
        send_sems, recv_sems = refs[2 * n:2 * n + 2]
        token = refs[-1]
        me, others = _peers()
        for p in range(n):
            for k, dev in enumerate(others):
                _exchange_copy(src_refs[p], land_refs[p], send_sems, recv_sems, p * (N_DEV - 1) + k, dev, me,
                               scatter, False).start()
        token[...] = jnp.zeros_like(token)

    sems = pltpu.SemaphoreType.DMA((n * (N_DEV - 1),))
    hbm = lambda a: pltpu.with_memory_space_constraint(a, pltpu.HBM)
    outs = pl.pallas_call(
        body, name=name,
        out_shape=(sems, sems, *[pltpu.HBM(a.shape, a.dtype) for a in srcs + landings],
                   jax.ShapeDtypeStruct((8, LANES), F32)),
        in_specs=(_HBM,) * (2 * n), out_specs=(_SEM, _SEM) + (_HBM,) * (2 * n) + (pl.BlockSpec(memory_space=pltpu.VMEM),),
        input_output_aliases={i: 2 + i for i in range(2 * n)},
        compiler_params=pltpu.CompilerParams(has_side_effects=_EFFECT),
    )(*[hbm(a) for a in srcs + landings])
    return (outs[0], outs[1], list(outs[2:2 + n]), list(outs[2 + n:2 + 2 * n]), scatter), outs[-1]


def _exchange_wait(name, handle, after):
    send_sems, recv_sems, src_thru, land_thru, scatter = handle
    n = len(src_thru)

    def body(*refs):
        src_refs, land_refs = refs[:n], refs[n:2 * n]
        send_sems, recv_sems = refs[2 * n:2 * n + 2]
        me, others = _peers()
        for p in range(n):
            for k, dev in enumerate(others):
                cp = _exchange_copy(src_refs[p], land_refs[p], send_sems, recv_sems, p * (N_DEV - 1) + k, dev, me,
                                    scatter, True)
                cp.wait_send()
                cp.wait_recv()

    outs = pl.pallas_call(
        body, name=name,
        out_shape=tuple(pltpu.HBM(a.shape, a.dtype) for a in src_thru + land_thru),
        in_specs=(_HBM,) * (2 * n) + (_SEM, _SEM, pl.BlockSpec(memory_space=pl.ANY)), out_specs=(_HBM,) * (2 * n),
        input_output_aliases={i: i for i in range(2 * n)},
        compiler_params=pltpu.CompilerParams(has_side_effects=_EFFECT),
    )(*src_thru, *land_thru, send_sems, recv_sems, after)
    return list(outs[n:])


def _adamw_math(w, g, m, v):
    m = ADAM_B1 * m + (1.0 - ADAM_B1) * g
    v = ADAM_B2 * v + (1.0 - ADAM_B2) * (g * g)
    m_hat = m / (1.0 - ADAM_B1 ** ADAM_STEP)
    v_hat = v / (1.0 - ADAM_B2 ** ADAM_STEP)
    delta = -ADAM_LR * (m_hat / (jnp.sqrt(v_hat) + ADAM_EPS) + ADAM_WD * w)
    return delta, m, v


def _adamw(name, parts, w, m, v, tr=128):
    P, R, W = parts.shape
    tr = min(tr, R)

    def body(p_ref, w_ref, m_ref, v_ref, g_out, d_out, m_out, v_out):
        g = p_ref[0].astype(F32)
        for j in range(1, P):
            g = g + p_ref[j].astype(F32)
        delta, m_new, v_new = _adamw_math(w_ref[...], g, m_ref[...], v_ref[...])
        g_out[...] = g
        d_out[...] = delta
        m_out[...] = m_new
        v_out[...] = v_new

    row = pl.BlockSpec((tr, W), lambda i: (i, 0))
    shape = jax.ShapeDtypeStruct((R, W), F32)
    return pl.pallas_call(
        body, name=name, grid=(R // tr,),
        in_specs=[pl.BlockSpec((P, tr, W), lambda i: (0, i, 0)), row, row, row],
        out_specs=[row, row, row, row],
        out_shape=[shape, shape, shape, shape],
        compiler_params=_params(("parallel",)),
    )(parts, w, m, v)


def _adamw_shard(name, parts, w, m, v, layer, earlier=None, after=None, tr=256):
    L, K, N = w.shape
    tr = min(tr, K)
    n_prev = 0 if earlier is None else 4
    after_args, after_specs = _after_operand(after)

    def body(p_ref, w_ref, m_ref, v_ref, *rest):
        g_out, d_out, m_out, v_out = rest[n_prev + len(after_args):]
        g = p_ref[0].astype(F32)
        for j in range(1, N_DEV):
            g = g + p_ref[j].astype(F32)
        delta, m_new, v_new = _adamw_math(w_ref[...], g, m_ref[...], v_ref[...])
        g_out[...] = g
        d_out[...] = delta
        m_out[...] = m_new
        v_out[...] = v_new

    row = pl.BlockSpec((None, tr, N), lambda i: (layer, i, 0))
    shape = jax.ShapeDtypeStruct((L, K, N), F32)
    return pl.pallas_call(
        body, name=name, grid=(K // tr,),
        in_specs=[pl.BlockSpec((N_DEV, tr, N), lambda i: (0, i, 0)), row, row, row]
        + [pl.BlockSpec(memory_space=pl.ANY)] * n_prev + after_specs,
        out_specs=[row, row, row, row],
        out_shape=[shape, shape, shape, shape],
        input_output_aliases={4 + j: j for j in range(n_prev)},
        compiler_params=_params(("parallel",)),
    )(parts, w, m, v, *(earlier or ()), *after_args)


def _column_slots(full):
    K, N = full.shape
    return jnp.transpose(full.reshape(K, N_DEV, N // N_DEV), (1, 0, 2))


def _from_column_slots(slots):
    _, K, n = slots.shape
    return jnp.transpose(slots, (1, 0, 2)).reshape(K, N_DEV * n)


_SMALL =("mix_norm_g", "mlp_norm_g", "final_norm_g", "a_ln_g", "a_ln_b", "a_w_s", "a_b_s", "rel_bias")


def _pack_small(vals, scalar=None):
    pieces = []
    for n in _SMALL:
        flat = vals[n].reshape(-1)
        pad = (-flat.shape[0]) % (8 * LANES)
        pieces.append(jnp.pad(flat, (0, pad)).reshape(-1, LANES))
    last = jnp.zeros((8 * LANES,), F32) if scalar is None else jnp.pad(scalar.reshape(1), (0, 8 * LANES - 1))
    pieces.append(last.reshape(8, LANES))
    return jnp.concatenate(pieces, axis=0)


def _unpack_small(buf, like):
    out = {}
    r = 0
    for n in _SMALL:
        size = like[n].size
        nrows = -(-size // (8 * LANES)) * 8
        out[n] = buf[r:r + nrows].reshape(-1)[:size].reshape(like[n].shape)
        r += nrows
    return out


_STAGES = (("gate", ("a_w_in", "a_w_out"), 0),
           ("mlp0", ("w_up", "w_down"), 0),
           ("att", ("b_w_qkv", "b_w_out"), 0),
           ("mlp1", ("w_up", "w_down"), 1))


def kernel(x, mix_norm_g, mlp_norm_g, final_norm_g, a_w_in, a_ln_g, a_ln_b, a_w_s, a_b_s, a_w_out, b_w_qkv, b_w_out, rel_bias, w_up, w_down, loss_target, m_mix_norm_g, m_mlp_norm_g, m_final_norm_g, m_a_w_in, m_a_ln_g, m_a_ln_b, m_a_w_s, m_a_b_s, m_a_w_out, m_b_w_qkv, m_b_w_out, m_rel_bias, m_w_up, m_w_down, v_mix_norm_g, v_mlp_norm_g, v_final_norm_g, v_a_w_in, v_a_ln_g, v_a_ln_b, v_a_w_s, v_a_b_s, v_a_w_out, v_b_w_qkv, v_b_w_out, v_rel_bias, v_w_up, v_w_down):
    w = dict(mix_norm_g=mix_norm_g, mlp_norm_g=mlp_norm_g, final_norm_g=final_norm_g, a_w_in=a_w_in, a_ln_g=a_ln_g,
             a_ln_b=a_ln_b, a_w_s=a_w_s, a_b_s=a_b_s, a_w_out=a_w_out, b_w_qkv=b_w_qkv, b_w_out=b_w_out,
             rel_bias=rel_bias, w_up=w_up, w_down=w_down)
    m = dict(mix_norm_g=m_mix_norm_g, mlp_norm_g=m_mlp_norm_g, final_norm_g=m_final_norm_g, a_w_in=m_a_w_in,
             a_ln_g=m_a_ln_g, a_ln_b=m_a_ln_b, a_w_s=m_a_w_s, a_b_s=m_a_b_s, a_w_out=m_a_w_out, b_w_qkv=m_b_w_qkv,
             b_w_out=m_b_w_out, rel_bias=m_rel_bias, w_up=m_w_up, w_down=m_w_down)
    v = dict(mix_norm_g=v_mix_norm_g, mlp_norm_g=v_mlp_norm_g, final_norm_g=v_final_norm_g, a_w_in=v_a_w_in,
             a_ln_g=v_a_ln_g, a_ln_b=v_a_ln_b, a_w_s=v_a_w_s, a_b_s=v_a_b_s, a_w_out=v_a_w_out, b_w_qkv=v_b_w_qkv,
             b_w_out=v_b_w_out, rel_bias=v_rel_bias, w_up=v_w_up, w_down=v_w_down)

    stages = {s: (names, layer) for s, names, layer in _STAGES}
    order = [s for s, _, _ in _STAGES]

    def shards_of(stage):
        names, layer = stages[stage]
        return [_bf(w[n][layer]) for n in names]

    pending = {}
    pending[order[0]], first_token = _exchange_start("gather_" + order[0] + "_start", shards_of(order[0]), False)

    def get_weights(stage, dep):
        gathered = _exchange_wait("gather_" + stage + "_wait", pending.pop(stage), dep)
        nxt = order.index(stage) + 1
        token = None
        if nxt < len(order):
            shards, gathered = lax.optimization_barrier((shards_of(order[nxt]), gathered))
            pending[order[nxt]], token = _exchange_start("gather_" + order[nxt] + "_start", shards, False)
        return gathered, token

    sent = {}

    def put_grads(stage, slot_grads):
        sent[stage], token = _exchange_start("scatter_" + stage + "_start", slot_grads, True)
        return token

    loss_local, grad_x, small_g = _local_step(
        x[0], loss_target[0], mix_norm_g, mlp_norm_g, final_norm_g, a_ln_g, a_ln_b, a_w_s, a_b_s, rel_bias,
        get_weights, put_grads, first_token)

    small_sent, token = _exchange_start("gather_small_start", [_pack_small(small_g, loss_local)], False)

    results = {}
    prev = token
    for stage in reversed(order):
        names, layer = stages[stage]
        received = _exchange_wait("scatter_" + stage + "_wait", sent[stage], prev)
        for n, parts in zip(names, received):
            results[n] = _adamw_shard("adamw_%s_%s" % (stage, n), parts, w[n], m[n], v[n], layer, results.get(n),
                                      after=prev)
            prev = results[n][0]

    gathered, = _exchange_wait("gather_small_wait", small_sent, prev)
    bufs = _adamw("adamw_small", gathered, _pack_small(w), _pack_small(m), _pack_small(v), tr=gathered.shape[1])
    small = [_unpack_small(b, w) for b in bufs]
    loss = bufs[0][-8, 0]

    outs = []
    for j in range(4):
        outs.extend(small[j][n] if n in _SMALL else results[n][j] for n in w)
    return (loss, grad_x[None], *outs)


def _local_step(xs, tgt, mix_norm_g, mlp_norm_g, final_norm_g, a_ln_g, a_ln_b, a_w_s, a_b_s, rel_bias,
                get_weights, put_grads, first_token=None):
    D = xs.shape[-1]
    g_mix = [mix_norm_g[l][None, :] for l in range(2)]
    g_mlp = [mlp_norm_g[l][None, :] for l in range(2)]
    g_fin = final_norm_g[None, :]
    ln_g, ln_b = a_ln_g, a_ln_b
    causal = jnp.tril(jnp.ones((CHUNK, CHUNK), dtype=bool))
    wm = _bf(jnp.where(causal[None], a_w_s[0], 0.0))
    bs_full = jnp.repeat(a_b_s[0].T, D // GROUPS, axis=1)
    bias_tiles = _bias_tiles("att_bias", rel_bias, after=first_token)

    y0 = _rms_fwd("rms_mix0", xs, g_mix[0], after=bias_tiles)
    (win, wout), token = get_weights("gate", y0)
    wout = wout.reshape(-1, D)
    uvp = _mm_nn("gate_in", y0, win, tm=512, nc=win.shape[2], shards=True, after=token)
    z = _gate_fwd("gate_mid", uvp, ln_g, ln_b, wm, bs_full)
    h1, y1 = _mm_nn("gate_out", z, wout, tm=512, nc=512, epi="res", extra=xs, norm_g=g_mlp[0])
    (wup0, wdn0), token = get_weights("mlp0", h1)
    wdn0 = wdn0.reshape(-1, D)
    a0, f0 = _mm_nn("mlp0_up", y1, wup0, tm=512, nc=wup0.shape[2], epi="relu2", shards=True, after=token)
    h2, y2 = _mm_nn("mlp0_down", f0, wdn0, tm=512, nc=512, epi="res", extra=h1, norm_g=g_mix[1])
    (wqkv, wo), token = get_weights("att", h2)
    wqkv, wo = _from_column_slots(wqkv), _from_column_slots(wo)
    qkv = _mm_nn("att_qkv", y2, wqkv, tm=256, nc=512, after=token)
    o_att, lse = _att_merge("att_merge", *_att_fwd("att_fwd", qkv, bias_tiles))
    h3, y3 = _mm_nn("att_out", o_att, wo, tm=512, nc=512, epi="res", extra=h2, norm_g=g_mlp[1])
    (wup1, wdn1), _ = get_weights("mlp1", h3)
    wdn1 = wdn1.reshape(-1, D)
    a1, f1 = _mm_nn("mlp1_up", y3, wup1, tm=512, nc=wup1.shape[2], epi="relu2", shards=True)
    dh, dg_fin, err2 = _mm_res_loss("mlp1_down_loss", f1, wdn1, h3, g_fin, tgt, tm=512, nc=512)
    loss_local = 0.5 * jnp.sum(err2) / D

    def mlp_bwd(tag, dh, h_in, y, a, f, wup_l, wdn_l, g_row, after):
        da = _mm_nt(tag + "_dact", dh, wdn_l, tm=512, nc=512, epi="mask2relu", extra=a, after=after)
        g_dn = _mm_tn(tag + "_dwdown", f, dh, t1=1024, tn=1024)
        g_up = _mm_tn(tag + "_dwup", y, da, t1=1024, tn=1024, slot_cols=wup_l.shape[2])
        dh_in, dg = _mm_nt_rms_bwd(tag + "_dy", [(da, wup_l, *_whole(wup_l))], h_in, g_row, dh, tm=512, nc=512,
                                   shards=True)
        return dh_in, dg, put_grads(tag, [g_up, g_dn.reshape(N_DEV, -1, D)])

    dh3, dg_mlp1, token = mlp_bwd("mlp1", dh, h3, y3, a1, f1, wup1, wdn1, g_mlp[1], None)

    d_o = _mm_nt("att_dout", dh3, wo, tm=512, nc=512, after=token)
    g_wo = _mm_tn("att_dwo", o_att, dh3, t1=512, tn=1024)
    dq, dk, dv, ds_sums = _att_bwd("att_bwd", qkv, o_att, lse, d_o, bias_tiles)
    part_w = N_DIL * ATT_WIDTH
    g_qkv = [_mm_tn("att_dwqkv%d" % p, y2, t, t1=1024, tn=part_w) for p, t in enumerate((dq, dk, dv))]
    dh2, dg_mix1 = _mm_nt_rms_bwd("att_dy", [(t, wqkv, (D, part_w), (0, p)) for p, t in enumerate((dq, dk, dv))],
                                  h2, g_mix[1], dh3, tm=512, nc=512)
    token = put_grads("att", [_column_slots(jnp.concatenate(g_qkv, axis=1)), _column_slots(g_wo)])

    dh1, dg_mlp0, token = mlp_bwd("mlp0", dh2, h1, y1, a0, f0, wup0, wdn0, g_mlp[0], token)

    dz = _mm_nt("gate_dz", dh1, wout, tm=512, nc=512, after=token)
    g_wout = _mm_tn("gate_dwout", z, dh1, t1=1024, tn=1024)
    duvp, d_wm, d_mixed, d_lng, d_lnb = _gate_bwd("gate_dmid", uvp, dz, ln_g, ln_b, wm, bs_full)
    g_win = _mm_tn("gate_dwin", y0, duvp, t1=1024, tn=1024, slot_cols=win.shape[2])
    token = put_grads("gate", [g_win, g_wout.reshape(N_DEV, -1, D)])
    grad_x, dg_mix0 = _mm_nt_rms_bwd("gate_dy", [(duvp, win, *_whole(win))], xs, g_mix[0], dh1, tm=512, nc=512,
                                     after=token, shards=True)

    small_g = dict(
        mix_norm_g=jnp.concatenate([dg_mix0, dg_mix1], axis=0),
        mlp_norm_g=jnp.concatenate([dg_mlp0, dg_mlp1], axis=0),
        final_norm_g=dg_fin[0], a_ln_g=d_lng, a_ln_b=d_lnb, a_w_s=d_wm[None],
        a_b_s=jnp.sum(d_mixed.reshape(CHUNK, GROUPS, D // GROUPS), axis=2).T[None],
        rel_bias=_bias_grad("att_dbias", ds_sums))
    return loss_local, grad_x, small_g
```

```python
import functools
import math

import jax
import jax.numpy as jnp
from jax import lax
from jax.experimental import pallas as pl
from jax.experimental.pallas import tpu as pltpu

F32 = jnp.float32
BF16 = jnp.bfloat16
MESH = pl.DeviceIdType.MESH

N_DEV = 8
EPS = 1e-6
NEG_INF = -1e30
CHUNK = 128
GROUPS = 8
HEAD_DIM = 64
ATT_HEADS = 8
ATT_WIDTH = ATT_HEADS * HEAD_DIM
DILATIONS = (1, 4, 16)
N_DIL = len(DILATIONS)
N_BUCKETS = 32
MAX_EXACT = N_BUCKETS // 2
REL_MAX_DISTANCE = 2048
ATT_ROWS = 2048
ATT_SCALE = HEAD_DIM ** -0.5
LANES = 128

ADAM_LR = 0.001
ADAM_B1 = 0.9
ADAM_B2 = 0.999
ADAM_EPS = 1e-08
ADAM_WD = 0.01
ADAM_STEP = 10

VMEM_LIMIT_BYTES = 56 * 1024 * 1024


def _params(semantics=None):
    return pltpu.CompilerParams(dimension_semantics=semantics, vmem_limit_bytes=VMEM_LIMIT_BYTES)


def _bf(v):
    return v.astype(BF16)


def _dot(a, b, dims):
    return lax.dot_general(a, b, (dims, ((), ())), preferred_element_type=F32)


NN = ((1,), (0,))
NT = ((1,), (1,))
TN = ((0,), (0,))


def _after_operand(after):
    if after is None:
        return [], []
    return [after], [pl.BlockSpec(memory_space=pl.ANY)]


def _rms_fwd(name, x, g, tm=512, after=None):
    S, D = x.shape
    after_args, after_specs = _after_operand(after)

    def body(x_ref, g_ref, *rest):
        y_ref = rest[-1]
        xv = x_ref[...]
        r = lax.rsqrt(jnp.mean(xv * xv, axis=-1, keepdims=True) + EPS)
        y_ref[...] = _bf(xv * r * g_ref[...])

    return pl.pallas_call(
        body, name=name, grid=(S // tm,),
        in_specs=[pl.BlockSpec((tm, D), lambda i: (i, 0)), pl.BlockSpec((1, D), lambda i: (0, 0))] + after_specs,
        out_specs=pl.BlockSpec((tm, D), lambda i: (i, 0)),
        out_shape=jax.ShapeDtypeStruct((S, D), BF16),
        compiler_params=_params(("parallel",)),
    )(x, g, *after_args)


def _mm_res_loss(name, a, w, res, g, target, *, tm, nc):
    M, D = res.shape

    def body(a_ref, w_ref, r_ref, g_ref, t_ref, dh_ref, dg_ref, l_ref, h_sc):
        i = pl.program_id(0)
        a_v = _bf(a_ref[...])
        for j in range(D // nc):
            cols, acc = _chunk_product([a_v], [w_ref], j, nc, False, False)
            h_sc[:, cols] = r_ref[:, cols] + acc
        xv = h_sc[...]
        r = lax.rsqrt(jnp.mean(xv * xv, axis=-1, keepdims=True) + EPS)
        xh = xv * r
        gv = g_ref[...]
        e = xh * gv - t_ref[...]
        dout = e / D
        dyg = dout * gv
        c = jnp.mean(dyg * xh, axis=-1, keepdims=True)
        dh_ref[...] = r * (dyg - xh * c)
        dg_part = jnp.sum(dout * xh, axis=0, keepdims=True)
        l_part = jnp.sum(e * e, axis=0, keepdims=True)

        @pl.when(i == 0)
        def _():
            dg_ref[...] = dg_part
            l_ref[...] = l_part

        @pl.when(i > 0)
        def _():
            dg_ref[...] += dg_part
            l_ref[...] += l_part

    row = pl.BlockSpec((tm, D), lambda i: (i, 0))
    vec = pl.BlockSpec((1, D), lambda i: (0, 0))
    return pl.pallas_call(
        body, name=name, grid=(M // tm,),
        in_specs=[pl.BlockSpec((tm, a.shape[1]), lambda i: (i, 0)), pl.BlockSpec(w.shape, lambda i: (0, 0)),
                  row, vec, row],
        out_specs=[row, vec, vec],
        out_shape=[jax.ShapeDtypeStruct((M, D), F32), jax.ShapeDtypeStruct((1, D), F32),
                   jax.ShapeDtypeStruct((1, D), F32)],
        scratch_shapes=[pltpu.VMEM((tm, D), F32)],
        compiler_params=_params(("arbitrary",)),
    )(a, w, res, g, target)


def _chunk_product(a_vals, w_refs, j, nc, nt, shards):
    cols = slice(j * nc, (j + 1) * nc)
    acc = None
    for a_v, w_ref in zip(a_vals, w_refs):
        if not shards:
            terms = [_dot(a_v, w_ref[cols, :], NT) if nt else _dot(a_v, w_ref[:, cols], NN)]
        elif nt:
            nl = w_ref.shape[2]
            terms = [_dot(a_v[:, k * nl:(k + 1) * nl], w_ref[k, cols, :], NT) for k in range(N_DEV)]
        else:
            terms = [_dot(a_v, w_ref[j], NN)]
        for t in terms:
            acc = t if acc is None else acc + t
    return cols, acc


def _mm_rows(name, pairs, n_out, *, nt, tm, nc, epi="plain", extra=None, out_dtype=F32, after=None, shards=False,
             norm_g=None):
    M = pairs[0][0].shape[0]
    np_ = len(pairs)
    after_args, after_specs = _after_operand(after)

    def body(*refs):
        a_refs = refs[:np_]
        w_refs = refs[np_:2 * np_]
        pos = 2 * np_
        e_ref = None
        if extra is not None:
            e_ref = refs[pos]
            pos += 1
        if norm_g is not None:
            g_ref = refs[pos]
            pos += 1
        pos += len(after_args)
        outs = refs[pos:]
        a_vals = [_bf(a[...]) for a in a_refs]
        for j in range(n_out // nc):
            cols, acc = _chunk_product(a_vals, w_refs, j, nc, nt, shards)
            if epi == "plain":
                outs[0][:, cols] = acc.astype(out_dtype)
            elif epi == "res":
                outs[0][:, cols] = e_ref[:, cols] + acc
            elif epi == "relu2":
                outs[0][:, cols] = _bf(acc)
                rl = jnp.maximum(acc, 0.0)
                outs[1][:, cols] = _bf(rl * rl)
            elif epi == "mask2relu":
                outs[0][:, cols] = _bf(acc * (2.0 * jnp.maximum(e_ref[:, cols].astype(F32), 0.0)))
        if norm_g is not None:
            hv = outs[0][...]
            r = lax.rsqrt(jnp.mean(hv * hv, axis=-1, keepdims=True) + EPS)
            outs[1][...] = _bf(hv * r * g_ref[...])

    in_specs = [pl.BlockSpec((tm, a.shape[1]), lambda i: (i, 0)) for a, _, _, _ in pairs]
    for _, _, wshape, widx in pairs:
        in_specs.append(pl.BlockSpec(wshape, functools.partial(lambda i, widx: widx, widx=widx)))
    args = [a for a, _, _, _ in pairs] + [w for _, w, _, _ in pairs]
    if extra is not None:
        in_specs.append(pl.BlockSpec((tm, n_out), lambda i: (i, 0)))
        args.append(extra)
    if norm_g is not None:
        in_specs.append(pl.BlockSpec((1, n_out), lambda i: (0, 0)))
        args.append(norm_g)
    in_specs += after_specs
    args += after_args
    row_out = pl.BlockSpec((tm, n_out), lambda i: (i, 0))
    if epi == "relu2":
        out_specs = [row_out, row_out]
        out_shape = [jax.ShapeDtypeStruct((M, n_out), BF16), jax.ShapeDtypeStruct((M, n_out), BF16)]
    elif norm_g is not None:
        out_specs = [row_out, row_out]
        out_shape = [jax.ShapeDtypeStruct((M, n_out), F32), jax.ShapeDtypeStruct((M, n_out), BF16)]
    else:
        dt = BF16 if epi == "mask2relu" else (F32 if epi == "res" else out_dtype)
        out_specs = row_out
        out_shape = jax.ShapeDtypeStruct((M, n_out), dt)
    return pl.pallas_call(
        body, name=name, grid=(M // tm,), in_specs=in_specs, out_specs=out_specs, out_shape=out_shape,
        compiler_params=_params(("parallel",)),
    )(*args)


def _whole(w):
    return w.shape, (0,) * w.ndim


def _mm_nn(name, a, w, **kw):
    n_out = w.shape[0] * w.shape[2] if w.ndim == 3 else w.shape[1]
    return _mm_rows(name, [(a, w, *_whole(w))], n_out, nt=False, **kw)


def _mm_nt(name, a, w, **kw):
    return _mm_rows(name, [(a, w, *_whole(w))], w.shape[0], nt=True, **kw)


def _mm_nt_rms_bwd(name, pairs, x, g, dres, *, tm, nc, after=None, shards=False):
    M, D = x.shape
    np_ = len(pairs)
    after_args, after_specs = _after_operand(after)

    def body(*refs):
        a_refs = refs[:np_]
        w_refs = refs[np_:2 * np_]
        x_ref, g_ref, r_ref = refs[2 * np_:2 * np_ + 3]
        dx_ref, dg_ref, dy_sc = refs[-3:]
        i = pl.program_id(0)
        a_vals = [_bf(a[...]) for a in a_refs]
        for j in range(D // nc):
            cols, acc = _chunk_product(a_vals, w_refs, j, nc, True, shards)
            dy_sc[:, cols] = acc
        xv = x_ref[...]
        r = lax.rsqrt(jnp.mean(xv * xv, axis=-1, keepdims=True) + EPS)
        xh = xv * r
        dy_v = dy_sc[...]
        dyg = dy_v * g_ref[...]
        c = jnp.mean(dyg * xh, axis=-1, keepdims=True)
        dx_ref[...] = r_ref[...] + r * (dyg - xh * c)
        part = jnp.sum(dy_v * xh, axis=0, keepdims=True)

        @pl.when(i == 0)
        def _():
            dg_ref[...] = part

        @pl.when(i > 0)
        def _():
            dg_ref[...] += part

    row = pl.BlockSpec((tm, D), lambda i: (i, 0))
    vec = pl.BlockSpec((1, D), lambda i: (0, 0))
    in_specs = [pl.BlockSpec((tm, a.shape[1]), lambda i: (i, 0)) for a, _, _, _ in pairs]
    for _, _, wshape, widx in pairs:
        in_specs.append(pl.BlockSpec(wshape, functools.partial(lambda i, widx: widx, widx=widx)))
    args = [a for a, _, _, _ in pairs] + [w for _, w, _, _ in pairs]
    return pl.pallas_call(
        body, name=name, grid=(M // tm,),
        in_specs=in_specs + [row, vec, row] + after_specs,
        out_specs=[row, vec],
        out_shape=[jax.ShapeDtypeStruct((M, D), F32), jax.ShapeDtypeStruct((1, D), F32)],
        scratch_shapes=[pltpu.VMEM((tm, D), F32)],
        compiler_params=_params(("arbitrary",)),
    )(*args, x, g, dres, *after_args)


def _mm_tn(name, a, b, *, t1, tn, tm=2048, slot_cols=None):
    M, K1 = a.shape
    N = b.shape[1]
    nm = M // tm

    def body(a_ref, b_ref, o_ref, acc_ref):
        m = pl.program_id(2)
        t = _dot(_bf(a_ref[...]), _bf(b_ref[...]), TN)

        @pl.when(m == 0)
        def _():
            acc_ref[...] = t

        @pl.when(m > 0)
        def _():
            acc_ref[...] += t

        @pl.when(m == nm - 1)
        def _():
            if slot_cols is None:
                o_ref[...] = _bf(acc_ref[...])
            else:
                for k in range(tn // slot_cols):
                    o_ref[k] = _bf(acc_ref[:, k * slot_cols:(k + 1) * slot_cols])

    if slot_cols is not None:
        out_spec = pl.BlockSpec((tn // slot_cols, t1, slot_cols), lambda i, j, m: (j, i, 0))
        out_shape = jax.ShapeDtypeStruct((N // slot_cols, K1, slot_cols), BF16)
    else:
        out_spec = pl.BlockSpec((t1, tn), lambda i, j, m: (i, j))
        out_shape = jax.ShapeDtypeStruct((K1, N), BF16)
    return pl.pallas_call(
        body, name=name, grid=(K1 // t1, N // tn, nm),
        in_specs=[pl.BlockSpec((tm, t1), lambda i, j, m: (m, i)), pl.BlockSpec((tm, tn), lambda i, j, m: (m, j))],
        out_specs=out_spec, out_shape=out_shape,
        scratch_shapes=[pltpu.VMEM((t1, tn), F32)],
        compiler_params=_params(("parallel", "parallel", "arbitrary")),
    )(a, b)


_INV_SQRT2 = 1.0 / math.sqrt(2.0)
_INV_SQRT2PI = 1.0 / math.sqrt(2.0 * math.pi)


def _gelu(x):
    return 0.5 * x * (1.0 + lax.erf(x * _INV_SQRT2))


def _gelu_and_grad(x):
    cdf = 0.5 * (1.0 + lax.erf(x * _INV_SQRT2))
    return x * cdf, cdf + x * (_INV_SQRT2PI * jnp.exp(-0.5 * x * x))


def _layer_norm_parts(v):
    mu = jnp.mean(v, axis=-1, keepdims=True)
    xc = v - mu
    rs = lax.rsqrt(jnp.mean(xc * xc, axis=-1, keepdims=True) + EPS)
    return xc * rs, rs


def _gate_fwd(name, uvp, ln_g, ln_b, wm, bs_full, tr=512):
    S, W2 = uvp.shape
    W = W2 // 2
    gd = W // GROUPS

    def body(u_ref, v_ref, lg_ref, lb_ref, wm_ref, bs_ref, z_ref):
        vh, _ = _layer_norm_parts(_gelu(v_ref[...]))
        vn = _bf(vh * lg_ref[...] + lb_ref[...])
        for ci in range(tr // CHUNK):
            rows = slice(ci * CHUNK, (ci + 1) * CHUNK)
            for g in range(GROUPS):
                cols = slice(g * gd, (g + 1) * gd)
                mixed = _dot(wm_ref[g], vn[rows, cols], NN) + bs_ref[:, cols]
                z_ref[rows, cols] = _bf(_gelu(u_ref[rows, cols]) * mixed)

    vec = pl.BlockSpec((1, W), lambda i: (0, 0))
    return pl.pallas_call(
        body, name=name, grid=(S // tr,),
        in_specs=[pl.BlockSpec((tr, W), lambda i: (i, 0)), pl.BlockSpec((tr, W), lambda i: (i, 1)), vec, vec,
                  pl.BlockSpec((GROUPS, CHUNK, CHUNK), lambda i: (0, 0, 0)),
                  pl.BlockSpec((CHUNK, W), lambda i: (0, 0))],
        out_specs=pl.BlockSpec((tr, W), lambda i: (i, 0)),
        out_shape=jax.ShapeDtypeStruct((S, W), BF16),
        compiler_params=_params(("parallel",)),
    )(uvp, uvp, ln_g, ln_b, wm, bs_full)


def _gate_bwd(name, uvp, dz, ln_g, ln_b, wm, bs_full, tr=256):
    S, W2 = uvp.shape
    W = W2 // 2
    gd = W // GROUPS
    n_steps = S // tr

    def body(u_ref, v_ref, dz_ref, lg_ref, lb_ref, wm_ref, bs_ref, duv_ref, dwm_ref, dmx_ref, dlg_ref, dlb_ref,
             dvn_ref):
        i = pl.program_id(0)
        v, dv_dvp = _gelu_and_grad(v_ref[...])
        vh, rs = _layer_norm_parts(v)
        lg = lg_ref[...]
        vn = _bf(vh * lg + lb_ref[...])

        @pl.when(i == 0)
        def _():
            dwm_ref[...] = jnp.zeros_like(dwm_ref)
            dmx_ref[...] = jnp.zeros_like(dmx_ref)
            dlg_ref[...] = jnp.zeros_like(dlg_ref)
            dlb_ref[...] = jnp.zeros_like(dlb_ref)

        for ci in range(tr // CHUNK):
            rows = slice(ci * CHUNK, (ci + 1) * CHUNK)
            for g in range(GROUPS):
                cols = slice(g * gd, (g + 1) * gd)
                u, du_dup = _gelu_and_grad(u_ref[rows, cols])
                dz_v = dz_ref[rows, cols]
                dmixed = dz_v * u
                dmx_ref[:, cols] += dmixed
                dmixed_b = _bf(dmixed)
                mixed = _dot(wm_ref[g], vn[rows, cols], NN) + bs_ref[:, cols]
                duv_ref[rows, cols] = _bf(dz_v * mixed * du_dup)
                dwm_ref[g] += _dot(dmixed_b, vn[rows, cols], NT)
                dvn_ref[rows, cols] = _dot(wm_ref[g], dmixed_b, TN)
        dvn = dvn_ref[...]
        dlg_ref[...] += jnp.sum(dvn * vh, axis=0, keepdims=True)
        dlb_ref[...] += jnp.sum(dvn, axis=0, keepdims=True)
        dvh = dvn * lg
        dv = rs * (dvh - jnp.mean(dvh, axis=-1, keepdims=True) - vh * jnp.mean(dvh * vh, axis=-1, keepdims=True))
        duv_ref[:, W:] = _bf(dv * dv_dvp)

        @pl.when(i == n_steps - 1)
        def _():
            t_idx = lax.broadcasted_iota(jnp.int32, (CHUNK, CHUNK), 0)
            s_idx = lax.broadcasted_iota(jnp.int32, (CHUNK, CHUNK), 1)
            keep = (s_idx <= t_idx).astype(F32)
            for g in range(GROUPS):
                dwm_ref[g] = dwm_ref[g] * keep

    vec = pl.BlockSpec((1, W), lambda i: (0, 0))
    row = pl.BlockSpec((tr, W), lambda i: (i, 0))
    return pl.pallas_call(
        body, name=name, grid=(n_steps,),
        in_specs=[row, pl.BlockSpec((tr, W), lambda i: (i, 1)), row, vec, vec,
                  pl.BlockSpec((GROUPS, CHUNK, CHUNK), lambda i: (0, 0, 0)),
                  pl.BlockSpec((CHUNK, W), lambda i: (0, 0))],
        out_specs=[pl.BlockSpec((tr, W2), lambda i: (i, 0)),
                   pl.BlockSpec((GROUPS, CHUNK, CHUNK), lambda i: (0, 0, 0)),
                   pl.BlockSpec((CHUNK, W), lambda i: (0, 0)), vec, vec],
        out_shape=[jax.ShapeDtypeStruct((S, W2), BF16), jax.ShapeDtypeStruct((GROUPS, CHUNK, CHUNK), F32),
                   jax.ShapeDtypeStruct((CHUNK, W), F32), jax.ShapeDtypeStruct((1, W), F32),
                   jax.ShapeDtypeStruct((1, W), F32)],
        scratch_shapes=[pltpu.VMEM((tr, W), F32)],
        compiler_params=_params(("arbitrary",)),
    )(uvp, uvp, dz, ln_g, ln_b, wm, bs_full)


def _t5_bucket(distance):
    small = distance < MAX_EXACT
    nf = jnp.maximum(distance, 1).astype(F32)
    large = MAX_EXACT + (jnp.log(nf / MAX_EXACT) / math.log(REL_MAX_DISTANCE / MAX_EXACT)
                         * (N_BUCKETS - MAX_EXACT)).astype(jnp.int32)
    large = jnp.minimum(large, N_BUCKETS - 1)
    return jnp.where(small, distance, large)


TILE_ELEMS = 2 * CHUNK * CHUNK


def _band_buckets():
    rel = CHUNK + jnp.arange(CHUNK)[None, :] - jnp.arange(2 * CHUNK)[:, None]
    band = (rel >= 0) & (rel <= CHUNK)
    buckets = [_t5_bucket(jnp.clip(rel, 0, CHUNK) * d) for d in DILATIONS]
    return jnp.stack(buckets), band


def _bucket_onehot():
    buckets, _ = _band_buckets()
    return (buckets.reshape(N_DIL, 1, TILE_ELEMS) == jnp.arange(N_BUCKETS)[None, :, None]).astype(F32)


def _bias_tiles(name, rel_bias, after=None):
    _, band = _band_buckets()
    own = band & (jnp.arange(2 * CHUNK) >= CHUNK)[:, None]
    masks = jnp.stack([own, band]).reshape(2, TILE_ELEMS).astype(F32)
    tables = jnp.transpose(rel_bias.reshape(N_BUCKETS, N_DIL, ATT_HEADS), (1, 2, 0))
    after_args, after_specs = _after_operand(after)

    def body(t_ref, oh_ref, m_ref, *rest):
        out_ref = rest[-1]
        for g in range(N_DIL):
            bias = lax.dot_general(t_ref[g], oh_ref[g], (NN, ((), ())), precision=lax.Precision.HIGHEST,
                                   preferred_element_type=F32)
            for f in range(2):
                out_ref[g, f] = jnp.where(m_ref[f:f + 1, :] > 0.5, bias, NEG_INF)

    whole = pl.BlockSpec(memory_space=pltpu.VMEM)
    out = pl.pallas_call(
        body, name=name, out_shape=jax.ShapeDtypeStruct((N_DIL, 2, ATT_HEADS, TILE_ELEMS), F32),
        in_specs=[whole, whole, whole] + after_specs, out_specs=whole,
        compiler_params=_params(),
    )(tables, _bucket_onehot(), masks, *after_args)
    out = out.reshape(N_DIL, 2, ATT_HEADS // 2, 2, 2 * CHUNK, CHUNK)
    return jnp.transpose(out, (0, 1, 2, 4, 3, 5)).reshape(N_DIL, 2, ATT_HEADS // 2, 2 * CHUNK, 2 * CHUNK)


def _att_specs(order):
    def column(part, ids):
        hp, g, _ = order(*ids)
        return part * 3 * 4 + g * 4 + hp

    def window(part):
        def index(*ids):
            c = order(*ids)[2]
            return pl.multiple_of(jnp.maximum(c - 1, 0) * ATT_ROWS, ATT_ROWS), column(part, ids) * LANES
        return pl.BlockSpec((pl.Element(2 * ATT_ROWS), pl.Element(LANES)), index)

    return [pl.BlockSpec((ATT_ROWS, LANES), lambda *ids: (order(*ids)[2], column(0, ids))), window(1), window(2)]


def _window_base(c):
    return jnp.where(c == 0, 0, ATT_ROWS)


def _rows(start, d):
    if d == 1:
        return pl.ds(pl.multiple_of(start, CHUNK), CHUNK)
    return pl.ds(start, CHUNK, stride=d)


def _att_tile_offsets(t, d):
    n = t // d
    r = t % d
    return n * (CHUNK * d) + r, n


def _head_pair_columns(x_t):
    zeros = jnp.zeros((HEAD_DIM, CHUNK), x_t.dtype)
    return jnp.concatenate([jnp.concatenate([x_t[:HEAD_DIM], zeros], axis=0),
                            jnp.concatenate([zeros, x_t[HEAD_DIM:]], axis=0)], axis=1)


def _head_pair_rows(y):
    return jnp.concatenate([y[:HEAD_DIM, :CHUNK], y[HEAD_DIM:, CHUNK:]], axis=0)


def _att_fwd(name, qkv, bias_tiles):
    S = qkv.shape[0]
    n_chunks = S // ATT_ROWS
    tiles = ATT_ROWS // CHUNK

    def body(q_ref, kk, vv, b_ref, o_ref, l_ref):
        c = pl.program_id(1)
        g = pl.program_id(2)
        base = _window_base(c)

        for gi, d in enumerate(DILATIONS):
            @pl.when(g == gi)
            def _(d=d):
                span = CHUNK * d

                def tile(t, carry):
                    q0, n = _att_tile_offsets(t, d)
                    first = (c == 0) & (n == 0)
                    rows = _rows(q0, d)
                    cur = _rows(base + q0, d)
                    prev = _rows(jnp.where(first, q0, base + q0 - span), d)
                    inner = jnp.where(first, 0, 1)
                    qq = _head_pair_columns(_bf(q_ref[rows, :] * ATT_SCALE).T)
                    s_p = _dot(_bf(kk[prev, :]), qq, NN) + b_ref[inner, 0:CHUNK, :]
                    s_c = _dot(_bf(kk[cur, :]), qq, NN) + b_ref[inner, CHUNK:2 * CHUNK, :]
                    m = jnp.maximum(jnp.max(s_p, axis=0, keepdims=True), jnp.max(s_c, axis=0, keepdims=True))
                    p_p = jnp.exp(s_p - m)
                    p_c = jnp.exp(s_c - m)
                    l = jnp.sum(p_p, axis=0, keepdims=True) + jnp.sum(p_c, axis=0, keepdims=True)
                    o2 = (_dot(_bf(vv[prev, :]).T, _bf(p_p), NN)
                          + _dot(_bf(vv[cur, :]).T, _bf(p_c), NN)) * (1.0 / l)
                    lse = m + jnp.log(l)
                    l_t = jnp.concatenate([jnp.broadcast_to(lse[:, :CHUNK], (HEAD_DIM, CHUNK)),
                                           jnp.broadcast_to(lse[:, CHUNK:], (HEAD_DIM, CHUNK))], axis=0)
                    o_ref[rows, :] = _head_pair_rows(o2).T
                    l_ref[rows, :] = l_t.T
                    return carry

                lax.fori_loop(0, tiles, tile, 0, unroll=8)

    order = lambda hp, c, g: (hp, g, c)
    out_spec = pl.BlockSpec((None, ATT_ROWS, LANES), lambda hp, c, g: (g, c, hp))
    shape = jax.ShapeDtypeStruct((N_DIL, S, ATT_WIDTH), F32)
    return pl.pallas_call(
        body, name=name, grid=(ATT_HEADS // 2, n_chunks, N_DIL),
        in_specs=_att_specs(order) + [
            pl.BlockSpec((None, 2, None, 2 * CHUNK, 2 * CHUNK), lambda hp, c, g: (g, 0, hp, 0, 0))],
        out_specs=[out_spec, out_spec],
        out_shape=[shape, shape],
        compiler_params=_params(("parallel", "parallel", "parallel")),
    )(qkv, qkv, qkv, bias_tiles)


def _att_merge(name, o_g, l_g, tm=512):
    _, S, W = o_g.shape

    def body(o_ref, l_ref, out_ref, lse_ref):
        ls = [l_ref[g] for g in range(N_DIL)]
        mx = functools.reduce(jnp.maximum, ls)
        ws = [jnp.exp(l - mx) for l in ls]
        tot = functools.reduce(lambda a, b: a + b, ws)
        acc = ws[0] * o_ref[0]
        for g in range(1, N_DIL):
            acc = acc + ws[g] * o_ref[g]
        out_ref[...] = acc / tot
        lse_ref[...] = mx + jnp.log(tot)

    blk = pl.BlockSpec((N_DIL, tm, W), lambda i: (0, i, 0))
    row = pl.BlockSpec((tm, W), lambda i: (i, 0))
    shape = jax.ShapeDtypeStruct((S, W), F32)
    return pl.pallas_call(
        body, name=name, grid=(S // tm,), in_specs=[blk, blk], out_specs=[row, row], out_shape=[shape, shape],
        compiler_params=_params(("parallel",)),
    )(o_g, l_g)


def _att_bwd(name, qkv, o, lse, d_o, bias_tiles):
    S = qkv.shape[0]
    n_chunks = S // ATT_ROWS
    tiles = ATT_ROWS // CHUNK

    def body(q_ref, kk, vv, o_ref, l_ref, do_ref, b_ref, dq_out, dk_out, dv_out, ds_ref, dq_ref, dk_ref, dv_ref):
        g = pl.program_id(1)
        c = pl.program_id(2)

        @pl.when(c == 0)
        def _():
            dk_ref[...] = jnp.zeros_like(dk_ref)
            dv_ref[...] = jnp.zeros_like(dv_ref)
            ds_ref[...] = jnp.zeros_like(ds_ref)

        base = _window_base(c)
        first_row = c * ATT_ROWS
        head0 = lax.broadcasted_iota(jnp.int32, (CHUNK, LANES), 1) < HEAD_DIM

        def head_pair_stack(x):
            zero = jnp.zeros_like(x)
            return jnp.concatenate([jnp.where(head0, x, zero), jnp.where(head0, zero, x)], axis=0)

        for gi, d in enumerate(DILATIONS):
            @pl.when(g == gi)
            def _(d=d):
                span = CHUNK * d

                def tile(t, carry):
                    q0, n = _att_tile_offsets(t, d)
                    first = (c == 0) & (n == 0)
                    rows = _rows(q0, d)
                    cur = _rows(base + q0, d)
                    prev = _rows(jnp.where(first, q0, base + q0 - span), d)
                    inner = jnp.where(first, 0, 1)
                    g_cur = _rows(first_row + q0, d)
                    g_prev = _rows(jnp.where(first, q0, first_row + q0 - span), d)
                    q2 = _bf(q_ref[rows, :] * ATT_SCALE)
                    q_t = q2.T
                    k2 = _bf(jnp.concatenate([kk[prev, :], kk[cur, :]], axis=0))
                    k_t = k2.T
                    v2 = _bf(jnp.concatenate([vv[prev, :], vv[cur, :]], axis=0))
                    do2 = do_ref[rows, :]
                    do_b = _bf(do2)
                    do_t = do_b.T
                    lse_t = l_ref[rows, :].T
                    dd_t = (do2 * o_ref[rows, :]).T
                    lse = jnp.concatenate([lse_t[0:1], lse_t[HEAD_DIM:HEAD_DIM + 1]], axis=1)
                    delta = jnp.concatenate([jnp.sum(dd_t[:HEAD_DIM], axis=0, keepdims=True),
                                             jnp.sum(dd_t[HEAD_DIM:], axis=0, keepdims=True)], axis=1)
                    s = _dot(k2, _head_pair_columns(q_t), NN) + b_ref[inner]
                    p = jnp.exp(s - lse)
                    ds = p * (_dot(v2, _head_pair_columns(do_t), NN) - delta)
                    ds_ref[...] += ds
                    ds_b = _bf(ds)
                    dq_t = _head_pair_rows(_dot(k_t, ds_b, NN))
                    dk2 = _dot(ds_b, head_pair_stack(q2), NN)
                    dv2 = _dot(_bf(p), head_pair_stack(do_b), NN)
                    dq_ref[rows, :] = (dq_t * ATT_SCALE).T
                    dk_ref[g_prev, :] += dk2[0:CHUNK]
                    dk_ref[g_cur, :] += dk2[CHUNK:2 * CHUNK]
                    dv_ref[g_prev, :] += dv2[0:CHUNK]
                    dv_ref[g_cur, :] += dv2[CHUNK:2 * CHUNK]
                    return carry

                lax.fori_loop(0, tiles, tile, 0, unroll=8)

        dq_out[...] = _bf(dq_ref[...])

        @pl.when(c == n_chunks - 1)
        def _():
            dk_out[...] = _bf(dk_ref[...])
            dv_out[...] = _bf(dv_ref[...])

    order = lambda hp, g, c: (hp, g, c)
    chunk = pl.BlockSpec((ATT_ROWS, LANES), lambda hp, g, c: (c, hp))
    slab = pl.BlockSpec((S, LANES), lambda hp, g, c: (0, g * 4 + hp))
    width = N_DIL * ATT_WIDTH
    dq, dk, dv, ds_sums = pl.pallas_call(
        body, name=name, grid=(ATT_HEADS // 2, N_DIL, n_chunks),
        in_specs=_att_specs(order) + [chunk, chunk, chunk,
                                      pl.BlockSpec((None, 2, None, 2 * CHUNK, 2 * CHUNK),
                                                   lambda hp, g, c: (g, 0, hp, 0, 0))],
        out_specs=[pl.BlockSpec((ATT_ROWS, LANES), lambda hp, g, c: (c, g * 4 + hp)), slab, slab,
                   pl.BlockSpec((None, None, 2 * CHUNK, 2 * CHUNK), lambda hp, g, c: (g, hp, 0, 0))],
        out_shape=[jax.ShapeDtypeStruct((S, width), BF16), jax.ShapeDtypeStruct((S, width), BF16),
                   jax.ShapeDtypeStruct((S, width), BF16),
                   jax.ShapeDtypeStruct((N_DIL, ATT_HEADS // 2, 2 * CHUNK, 2 * CHUNK), F32)],
        scratch_shapes=[pltpu.VMEM((ATT_ROWS, LANES), F32), pltpu.VMEM((S, LANES), F32),
                        pltpu.VMEM((S, LANES), F32)],
        compiler_params=_params(("parallel", "parallel", "arbitrary")),
    )(qkv, qkv, qkv, o, lse, d_o, bias_tiles)
    ds_sums = ds_sums.reshape(N_DIL, ATT_HEADS // 2, 2 * CHUNK, 2, CHUNK)
    ds_sums = jnp.transpose(ds_sums, (0, 1, 3, 2, 4)).reshape(N_DIL, ATT_HEADS, 2 * CHUNK, CHUNK)
    return dq, dk, dv, ds_sums


def _bias_grad(name, ds_sums):
    flat = ds_sums.reshape(N_DIL, ATT_HEADS, TILE_ELEMS)

    def body(oh_ref, ds_ref, out_ref):
        for g in range(N_DIL):
            out_ref[g] = lax.dot_general(oh_ref[g], ds_ref[g], (NT, ((), ())), precision=lax.Precision.HIGHEST,
                                         preferred_element_type=F32)

    out = pl.pallas_call(
        body, name=name, out_shape=jax.ShapeDtypeStruct((N_DIL, N_BUCKETS, ATT_HEADS), F32),
        compiler_params=_params(),
    )(_bucket_onehot(), flat)
    return jnp.transpose(out, (1, 0, 2)).reshape(N_BUCKETS, N_DIL * ATT_HEADS)


def _peers():
    x, y, c = lax.axis_index("x"), lax.axis_index("y"), lax.axis_index("c")
    me = 4 * x + 2 * y + c
    others = [(x, y, 1 - c), (1 - x, y, c), (x, 1 - y, c), (1 - x, 1 - y, c),
              (1 - x, y, 1 - c), (x, 1 - y, 1 - c), (1 - x, 1 - y, 1 - c)]
    return me, others


def _slot(dev):
    return 4 * dev[0] + 2 * dev[1] + dev[2]


_HBM =pl.BlockSpec(memory_space=pltpu.HBM)
_SEM = pl.BlockSpec(memory_space=pltpu.SEMAPHORE)
_EFFECT = pltpu.SideEffectType.DATAFLOW_SIDE_EFFECTING


def _my_slot():
    return 4 * lax.axis_index("x") + 2 * lax.axis_index("y") + lax.axis_index("c")


def _exchange_copy(src_ref, land_ref, send_sems, recv_sems, k, dev, me, scatter, arriving):
    src = src_ref.at[me if arriving else _slot(dev)] if scatter else src_ref
    dst = land_ref.at[_slot(dev) if arriving else me]
    return pltpu.make_async_remote_copy(src_ref=src, dst_ref=dst, send_sem=send_sems.at[k], recv_sem=recv_sems.at[k],
                                        device_id=dev, device_id_type=MESH)


def _exchange_start(name, srcs, scatter):
    n = len(srcs)
    me = _my_slot()
    landings = []
    for src in srcs:
        own = lax.dynamic_index_in_dim(src, me, 0, keepdims=True) if scatter else src[None]
        landings.append(lax.dynamic_update_slice(lax.empty((N_DEV,) + src.shape[-2:], src.dtype), own, (me, 0, 0)))

    def body(*refs):
        src_refs, land_refs = refs[:n], refs[n:2 * n]
        send_sems, recv_sems = refs[2 * n:2 * n + 2]
        token = refs[-1]
        me, others = _peers()
        for p in range(n):
            for k, dev in enumerate(others):
                _exchange_copy(src_refs[p], land_refs[p], send_sems, recv_sems, p * (N_DEV - 1) + k, dev, me,
                               scatter, False).start()
        token[...] = jnp.zeros_like(token)

    sems = pltpu.SemaphoreType.DMA((n * (N_DEV - 1),))
    hbm = lambda a: pltpu.with_memory_space_constraint(a, pltpu.HBM)
    outs = pl.pallas_call(
        body, name=name,
        out_shape=(sems, sems, *[pltpu.HBM(a.shape, a.dtype) for a in srcs + landings],
                   jax.ShapeDtypeStruct((8, LANES), F32)),
        in_specs=(_HBM,) * (2 * n), out_specs=(_SEM, _SEM) + (_HBM,) * (2 * n) + (pl.BlockSpec(memory_space=pltpu.VMEM),),
        input_output_aliases={i: 2 + i for i in range(2 * n)},
        compiler_params=pltpu.CompilerParams(has_side_effects=_EFFECT),
    )(*[hbm(a) for a in srcs + landings])
    return (outs[0], outs[1], list(outs[2:2 + n]), list(outs[2 + n:2 + 2 * n]), scatter), outs[-1]


def _exchange_wait(name, handle, after):
    send_sems, recv_sems, src_thru, land_thru, scatter = handle
    n = len(src_thru)

    def body(*refs):
        src_refs, land_refs = refs[:n], refs[n:2 * n]
        send_sems, recv_sems = refs[2 * n:2 * n + 2]
        me, others = _peers()
        for p in range(n):
            for k, dev in enumerate(others):
                cp = _exchange_copy(src_refs[p], land_refs[p], send_sems, recv_sems, p * (N_DEV - 1) + k, dev, me,
                                    scatter, True)
                cp.wait_send()
                cp.wait_recv()

    outs = pl.pallas_call(
        body, name=name,
        out_shape=tuple(pltpu.HBM(a.shape, a.dtype) for a in src_thru + land_thru),
        in_specs=(_HBM,) * (2 * n) + (_SEM, _SEM, pl.BlockSpec(memory_space=pl.ANY)), out_specs=(_HBM,) * (2 * n),
        input_output_aliases={i: i for i in range(2 * n)},
        compiler_params=pltpu.CompilerParams(has_side_effects=_EFFECT),
    )(*src_thru, *land_thru, send_sems, recv_sems, after)
    return list(outs[n:])


def _adamw_math(w, g, m, v):
    m = ADAM_B1 * m + (1.0 - ADAM_B1) * g
    v = ADAM_B2 * v + (1.0 - ADAM_B2) * (g * g)
    m_hat = m / (1.0 - ADAM_B1 ** ADAM_STEP)
    v_hat = v / (1.0 - ADAM_B2 ** ADAM_STEP)
    delta = -ADAM_LR * (m_hat / (jnp.sqrt(v_hat) + ADAM_EPS) + ADAM_WD * w)
    return delta, m, v


def _adamw(name, parts, w, m, v, tr=128):
    P, R, W = parts.shape
    tr = min(tr, R)

    def body(p_ref, w_ref, m_ref, v_ref, g_out, d_out, m_out, v_out):
        g = p_ref[0].astype(F32)
        for j in range(1, P):
            g = g + p_ref[j].astype(F32)
        delta, m_new, v_new = _adamw_math(w_ref[...], g, m_ref[...], v_ref[...])
        g_out[...] = g
        d_out[...] = delta
        m_out[...] = m_new
        v_out[...] = v_new

    row = pl.BlockSpec((tr, W), lambda i: (i, 0))
    shape = jax.ShapeDtypeStruct((R, W), F32)
    return pl.pallas_call(
        body, name=name, grid=(R // tr,),
        in_specs=[pl.BlockSpec((P, tr, W), lambda i: (0, i, 0)), row, row, row],
        out_specs=[row, row, row, row],
        out_shape=[shape, shape, shape, shape],
        compiler_params=_params(("parallel",)),
    )(parts, w, m, v)


def _adamw_shard(name, parts, w, m, v, layer, earlier=None, after=None, tr=256):
    L, K, N = w.shape
    tr = min(tr, K)
    n_prev = 0 if earlier is None else 4
    after_args, after_specs = _after_operand(after)

    def body(p_ref, w_ref, m_ref, v_ref, *rest):
        g_out, d_out, m_out, v_out = rest[n_prev + len(after_args):]
        g = p_ref[0].astype(F32)
        for j in range(1, N_DEV):
            g = g + p_ref[j].astype(F32)
        delta, m_new, v_new = _adamw_math(w_ref[...], g, m_ref[...], v_ref[...])
        g_out[...] = g
        d_out[...] = delta
        m_out[...] = m_new
        v_out[...] = v_new

    row = pl.BlockSpec((None, tr, N), lambda i: (layer, i, 0))
    shape = jax.ShapeDtypeStruct((L, K, N), F32)
    return pl.pallas_call(
        body, name=name, grid=(K // tr,),
        in_specs=[pl.BlockSpec((N_DEV, tr, N), lambda i: (0, i, 0)), row, row, row]
        + [pl.BlockSpec(memory_space=pl.ANY)] * n_prev + after_specs,
        out_specs=[row, row, row, row],
        out_shape=[shape, shape, shape, shape],
        input_output_aliases={4 + j: j for j in range(n_prev)},
        compiler_params=_params(("parallel",)),
    )(parts, w, m, v, *(earlier or ()), *after_args)


def _column_slots(full):
    K, N = full.shape
    return jnp.transpose(full.reshape(K, N_DEV, N // N_DEV), (1, 0, 2))


def _from_column_slots(slots):
    _, K, n = slots.shape
    return jnp.transpose(slots, (1, 0, 2)).reshape(K, N_DEV * n)


_SMALL =("mix_norm_g", "mlp_norm_g", "final_norm_g", "a_ln_g", "a_ln_b", "a_w_s", "a_b_s", "rel_bias")


def _pack_small(vals, scalar=None):
    pieces = []
    for n in _SMALL:
        flat = vals[n].reshape(-1)
        pad = (-flat.shape[0]) % (8 * LANES)
        pieces.append(jnp.pad(flat, (0, pad)).reshape(-1, LANES))
    last = jnp.zeros((8 * LANES,), F32) if scalar is None else jnp.pad(scalar.reshape(1), (0, 8 * LANES - 1))
    pieces.append(last.reshape(8, LANES))
    return jnp.concatenate(pieces, axis=0)


def _unpack_small(buf, like):
    out = {}
    r = 0
    for n in _SMALL:
        size = like[n].size
        nrows = -(-size // (8 * LANES)) * 8
        out[n] = buf[r:r + nrows].reshape(-1)[:size].reshape(like[n].shape)
        r += nrows
    return out


_STAGES = (("gate", ("a_w_in", "a_w_out"), 0),
           ("mlp0", ("w_up", "w_down"), 0),
           ("att", ("b_w_qkv", "b_w_out"), 0),
           ("mlp1", ("w_up", "w_down"), 1))


def kernel(x, mix_norm_g, mlp_norm_g, final_norm_g, a_w_in, a_ln_g, a_ln_b, a_w_s, a_b_s, a_w_out, b_w_qkv, b_w_out, rel_bias, w_up, w_down, loss_target, m_mix_norm_g, m_mlp_norm_g, m_final_norm_g, m_a_w_in, m_a_ln_g, m_a_ln_b, m_a_w_s, m_a_b_s, m_a_w_out, m_b_w_qkv, m_b_w_out, m_rel_bias, m_w_up, m_w_down, v_mix_norm_g, v_mlp_norm_g, v_final_norm_g, v_a_w_in, v_a_ln_g, v_a_ln_b, v_a_w_s, v_a_b_s, v_a_w_out, v_b_w_qkv, v_b_w_out, v_rel_bias, v_w_up, v_w_down):
    w = dict(mix_norm_g=mix_norm_g, mlp_norm_g=mlp_norm_g, final_norm_g=final_norm_g, a_w_in=a_w_in, a_ln_g=a_ln_g,
             a_ln_b=a_ln_b, a_w_s=a_w_s, a_b_s=a_b_s, a_w_out=a_w_out, b_w_qkv=b_w_qkv, b_w_out=b_w_out,
             rel_bias=rel_bias, w_up=w_up, w_down=w_down)
    m = dict(mix_norm_g=m_mix_norm_g, mlp_norm_g=m_mlp_norm_g, final_norm_g=m_final_norm_g, a_w_in=m_a_w_in,
             a_ln_g=m_a_ln_g, a_ln_b=m_a_ln_b, a_w_s=m_a_w_s, a_b_s=m_a_b_s, a_w_out=m_a_w_out, b_w_qkv=m_b_w_qkv,
             b_w_out=m_b_w_out, rel_bias=m_rel_bias, w_up=m_w_up, w_down=m_w_down)
    v = dict(mix_norm_g=v_mix_norm_g, mlp_norm_g=v_mlp_norm_g, final_norm_g=v_final_norm_g, a_w_in=v_a_w_in,
             a_ln_g=v_a_ln_g, a_ln_b=v_a_ln_b, a_w_s=v_a_w_s, a_b_s=v_a_b_s, a_w_out=v_a_w_out, b_w_qkv=v_b_w_qkv,
             b_w_out=v_b_w_out, rel_bias=v_rel_bias, w_up=v_w_up, w_down=v_w_down)

    stages = {s: (names, layer) for s, names, layer in _STAGES}
    order = [s for s, _, _ in _STAGES]

    def shards_of(stage):
        names, layer = stages[stage]
        return [_bf(w[n][layer]) for n in names]

    pending = {}
    pending[order[0]], first_token = _exchange_start("gather_" + order[0] + "_start", shards_of(order[0]), False)

    def get_weights(stage, dep):
        gathered = _exchange_wait("gather_" + stage + "_wait", pending.pop(stage), dep)
        nxt = order.index(stage) + 1
        token = None
        if nxt < len(order):
            shards, gathered = lax.optimization_barrier((shards_of(order[nxt]), gathered))
            pending[order[nxt]], token = _exchange_start("gather_" + order[nxt] + "_start", shards, False)
        return gathered, token

    sent = {}

    def put_grads(stage, slot_grads):
        sent[stage], token = _exchange_start("scatter_" + stage + "_start", slot_grads, True)
        return token

    loss_local, grad_x, small_g = _local_step(
        x[0], loss_target[0], mix_norm_g, mlp_norm_g, final_norm_g, a_ln_g, a_ln_b, a_w_s, a_b_s, rel_bias,
        get_weights, put_grads, first_token)

    small_sent, token = _exchange_start("gather_small_start", [_pack_small(small_g, loss_local)], False)

    results = {}
    prev = token
    for stage in reversed(order):
        names, layer = stages[stage]
        received = _exchange_wait("scatter_" + stage + "_wait", sent[stage], prev)
        for n, parts in zip(names, received):
            results[n] = _adamw_shard("adamw_%s_%s" % (stage, n), parts, w[n], m[n], v[n], layer, results.get(n),
                                      after=prev)
            prev = results[n][0]

    gathered, = _exchange_wait("gather_small_wait", small_sent, prev)
    bufs = _adamw("adamw_small", gathered, _pack_small(w), _pack_small(m), _pack_small(v), tr=gathered.shape[1])
    small = [_unpack_small(b, w) for b in bufs]
    loss = bufs[0][-8, 0]

    outs = []
    for j in range(4):
        outs.extend(small[j][n] if n in _SMALL else results[n][j] for n in w)
    return (loss, grad_x[None], *outs)


def _local_step(xs, tgt, mix_norm_g, mlp_norm_g, final_norm_g, a_ln_g, a_ln_b, a_w_s, a_b_s, rel_bias,
                get_weights, put_grads, first_token=None):
    D = xs.shape[-1]
    g_mix = [mix_norm_g[l][None, :] for l in range(2)]
    g_mlp = [mlp_norm_g[l][None, :] for l in range(2)]
    g_fin = final_norm_g[None, :]
    ln_g, ln_b = a_ln_g, a_ln_b
    causal = jnp.tril(jnp.ones((CHUNK, CHUNK), dtype=bool))
    wm = _bf(jnp.where(causal[None], a_w_s[0], 0.0))
    bs_full = jnp.repeat(a_b_s[0].T, D // GROUPS, axis=1)
    bias_tiles = _bias_tiles("att_bias", rel_bias, after=first_token)

    (win, wout), token = get_weights("gate", bias_tiles)
    wout = wout.reshape(-1, D)
    y0 = _rms_fwd("rms_mix0", xs, g_mix[0], after=token)
    uvp = _mm_nn("gate_in", y0, win, tm=512, nc=win.shape[2], shards=True)
    z = _gate_fwd("gate_mid", uvp, ln_g, ln_b, wm, bs_full)
    h1, y1 = _mm_nn("gate_out", z, wout, tm=512, nc=512, epi="res", extra=xs, norm_g=g_mlp[0])
    (wup0, wdn0), token = get_weights("mlp0", h1)
    wdn0 = wdn0.reshape(-1, D)
    a0, f0 = _mm_nn("mlp0_up", y1, wup0, tm=512, nc=wup0.shape[2], epi="relu2", shards=True, after=token)
    h2, y2 = _mm_nn("mlp0_down", f0, wdn0, tm=512, nc=512, epi="res", extra=h1, norm_g=g_mix[1])
    (wqkv, wo), token = get_weights("att", h2)
    wqkv, wo = _from_column_slots(wqkv), _from_column_slots(wo)
    qkv = _mm_nn("att_qkv", y2, wqkv, tm=512, nc=512, after=token)
    o_att, lse = _att_merge("att_merge", *_att_fwd("att_fwd", qkv, bias_tiles))
    h3, y3 = _mm_nn("att_out", o_att, wo, tm=512, nc=512, epi="res", extra=h2, norm_g=g_mlp[1])
    (wup1, wdn1), _ = get_weights("mlp1", h3)
    wdn1 = wdn1.reshape(-1, D)
    a1, f1 = _mm_nn("mlp1_up", y3, wup1, tm=512, nc=wup1.shape[2], epi="relu2", shards=True)
    dh, dg_fin, err2 = _mm_res_loss("mlp1_down_loss", f1, wdn1, h3, g_fin, tgt, tm=512, nc=512)
    loss_local = 0.5 * jnp.sum(err2) / D

    def mlp_bwd(tag, dh, h_in, y, a, f, wup_l, wdn_l, g_row, after):
        da = _mm_nt(tag + "_dact", dh, wdn_l, tm=512, nc=512, epi="mask2relu", extra=a, after=after)
        g_dn = _mm_tn(tag + "_dwdown", f, dh, t1=1024, tn=1024)
        g_up = _mm_tn(tag + "_dwup", y, da, t1=1024, tn=1024, slot_cols=wup_l.shape[2])
        dh_in, dg = _mm_nt_rms_bwd(tag + "_dy", [(da, wup_l, *_whole(wup_l))], h_in, g_row, dh, tm=512, nc=512,
                                   shards=True)
        return dh_in, dg, put_grads(tag, [g_up, g_dn.reshape(N_DEV, -1, D)])

    dh3, dg_mlp1, token = mlp_bwd("mlp1", dh, h3, y3, a1, f1, wup1, wdn1, g_mlp[1], None)

    d_o = _mm_nt("att_dout", dh3, wo, tm=512, nc=512, after=token)
    g_wo = _mm_tn("att_dwo", o_att, dh3, t1=512, tn=1024)
    dq, dk, dv, ds_sums = _att_bwd("att_bwd", qkv, o_att, lse, d_o, bias_tiles)
    part_w = N_DIL * ATT_WIDTH
    g_qkv = [_mm_tn("att_dwqkv%d" % p, y2, t, t1=1024, tn=part_w) for p, t in enumerate((dq, dk, dv))]
    dh2, dg_mix1 = _mm_nt_rms_bwd("att_dy", [(t, wqkv, (D, part_w), (0, p)) for p, t in enumerate((dq, dk, dv))],
                                  h2, g_mix[1], dh3, tm=512, nc=512)
    token = put_grads("att", [_column_slots(jnp.concatenate(g_qkv, axis=1)), _column_slots(g_wo)])

    dh1, dg_mlp0, token = mlp_bwd("mlp0", dh2, h1, y1, a0, f0, wup0, wdn0, g_mlp[0], token)

    dz = _mm_nt("gate_dz", dh1, wout, tm=512, nc=512, after=token)
    g_wout = _mm_tn("gate_dwout", z, dh1, t1=1024, tn=1024)
    duvp, d_wm, d_mixed, d_lng, d_lnb = _gate_bwd("gate_dmid", uvp, dz, ln_g, ln_b, wm, bs_full)
    g_win = _mm_tn("gate_dwin", y0, duvp, t1=1024, tn=1024, slot_cols=win.shape[2])
    token = put_grads("gate", [g_win, g_wout.reshape(N_DEV, -1, D)])
    grad_x, dg_mix0 = _mm_nt_rms_bwd("gate_dy", [(duvp, win, *_whole(win))], xs, g_mix[0], dh1, tm=512, nc=512,
                                     after=token, shards=True)

    small_g = dict(
        mix_norm_g=jnp.concatenate([dg_mix0, dg_mix1], axis=0),
        mlp_norm_g=jnp.concatenate([dg_mlp0, dg_mlp1], axis=0),
        final_norm_g=dg_fin[0], a_ln_g=d_lng, a_ln_b=d_lnb, a_w_s=d_wm[None],
        a_b_s=jnp.sum(d_mixed.reshape(CHUNK, GROUPS, D // GROUPS), axis=2).T[None],
        rel_bias=_bias_grad("att_dbias", ds_sums))
    return loss_local, grad_x, small_g
```

```python
import functools
import math

import jax
import jax.numpy as jnp
from jax import lax
from jax.experimental import pallas as pl
from jax.experimental.pallas import tpu as pltpu

F32 = jnp.float32
BF16 = jnp.bfloat16
MESH = pl.DeviceIdType.MESH

N_DEV = 8
EPS = 1e-6
NEG_INF = -1e30
CHUNK = 128
GROUPS = 8
HEAD_DIM = 64
ATT_HEADS = 8
ATT_WIDTH = ATT_HEADS * HEAD_DIM
DILATIONS = (1, 4, 16)
N_DIL = len(DILATIONS)
N_BUCKETS = 32
MAX_EXACT = N_BUCKETS // 2
REL_MAX_DISTANCE = 2048
ATT_ROWS = 2048
ATT_SCALE = HEAD_DIM ** -0.5
LANES = 128

ADAM_LR = 0.001
ADAM_B1 = 0.9
ADAM_B2 = 0.999
ADAM_EPS = 1e-08
ADAM_WD = 0.01
ADAM_STEP = 10

VMEM_LIMIT_BYTES = 56 * 1024 * 1024


def _params(semantics=None):
    return pltpu.CompilerParams(dimension_semantics=semantics, vmem_limit_bytes=VMEM_LIMIT_BYTES)


def _bf(v):
    return v.astype(BF16)


def _dot(a, b, dims):
    return lax.dot_general(a, b, (dims, ((), ())), preferred_element_type=F32)


NN = ((1,), (0,))
NT = ((1,), (1,))
TN = ((0,), (0,))


def _after_operand(after):
    if after is None:
        return [], []
    return [after], [pl.BlockSpec(memory_space=pl.ANY)]


def _rms_fwd(name, x, g, tm=512, after=None):
    S, D = x.shape
    after_args, after_specs = _after_operand(after)

    def body(x_ref, g_ref, *rest):
        y_ref = rest[-1]
        xv = x_ref[...]
        r = lax.rsqrt(jnp.mean(xv * xv, axis=-1, keepdims=True) + EPS)
        y_ref[...] = _bf(xv * r * g_ref[...])

    return pl.pallas_call(
        body, name=name, grid=(S // tm,),
        in_specs=[pl.BlockSpec((tm, D), lambda i: (i, 0)), pl.BlockSpec((1, D), lambda i: (0, 0))] + after_specs,
        out_specs=pl.BlockSpec((tm, D), lambda i: (i, 0)),
        out_shape=jax.ShapeDtypeStruct((S, D), BF16),
        compiler_params=_params(("parallel",)),
    )(x, g, *after_args)


def _mm_res_loss(name, a, w, res, g, target, *, tm, nc):
    M, D = res.shape

    def body(a_ref, w_ref, r_ref, g_ref, t_ref, dh_ref, dg_ref, l_ref, h_sc):
        i = pl.program_id(0)
        a_v = _bf(a_ref[...])
        for j in range(D // nc):
            cols, acc = _chunk_product([a_v], [w_ref], j, nc, False, False)
            h_sc[:, cols] = r_ref[:, cols] + acc
        xv = h_sc[...]
        r = lax.rsqrt(jnp.mean(xv * xv, axis=-1, keepdims=True) + EPS)
        xh = xv * r
        gv = g_ref[...]
        e = xh * gv - t_ref[...]
        dout = e / D
        dyg = dout * gv
        c = jnp.mean(dyg * xh, axis=-1, keepdims=True)
        dh_ref[...] = r * (dyg - xh * c)
        dg_part = jnp.sum(dout * xh, axis=0, keepdims=True)
        l_part = jnp.sum(e * e, axis=0, keepdims=True)

        @pl.when(i == 0)
        def _():
            dg_ref[...] = dg_part
            l_ref[...] = l_part

        @pl.when(i > 0)
        def _():
            dg_ref[...] += dg_part
            l_ref[...] += l_part

    row = pl.BlockSpec((tm, D), lambda i: (i, 0))
    vec = pl.BlockSpec((1, D), lambda i: (0, 0))
    return pl.pallas_call(
        body, name=name, grid=(M // tm,),
        in_specs=[pl.BlockSpec((tm, a.shape[1]), lambda i: (i, 0)), pl.BlockSpec(w.shape, lambda i: (0, 0)),
                  row, vec, row],
        out_specs=[row, vec, vec],
        out_shape=[jax.ShapeDtypeStruct((M, D), F32), jax.ShapeDtypeStruct((1, D), F32),
                   jax.ShapeDtypeStruct((1, D), F32)],
        scratch_shapes=[pltpu.VMEM((tm, D), F32)],
        compiler_params=_params(("arbitrary",)),
    )(a, w, res, g, target)


def _chunk_product(a_vals, w_refs, j, nc, nt, shards):
    cols = slice(j * nc, (j + 1) * nc)
    acc = None
    for a_v, w_ref in zip(a_vals, w_refs):
        if not shards:
            terms = [_dot(a_v, w_ref[cols, :], NT) if nt else _dot(a_v, w_ref[:, cols], NN)]
        elif nt:
            nl = w_ref.shape[2]
            terms = [_dot(a_v[:, k * nl:(k + 1) * nl], w_ref[k, cols, :], NT) for k in range(N_DEV)]
        else:
            terms = [_dot(a_v, w_ref[j], NN)]
        for t in terms:
            acc = t if acc is None else acc + t
    return cols, acc


def _mm_rows(name, pairs, n_out, *, nt, tm, nc, epi="plain", extra=None, out_dtype=F32, after=None, shards=False,
             norm_g=None):
    M = pairs[0][0].shape[0]
    np_ = len(pairs)
    after_args, after_specs = _after_operand(after)

    def body(*refs):
        a_refs = refs[:np_]
        w_refs = refs[np_:2 * np_]
        pos = 2 * np_
        e_ref = None
        if extra is not None:
            e_ref = refs[pos]
            pos += 1
        if norm_g is not None:
            g_ref = refs[pos]
            pos += 1
        pos += len(after_args)
        outs = refs[pos:]
        a_vals = [_bf(a[...]) for a in a_refs]
        for j in range(n_out // nc):
            cols, acc = _chunk_product(a_vals, w_refs, j, nc, nt, shards)
            if epi == "plain":
                outs[0][:, cols] = acc.astype(out_dtype)
            elif epi == "res":
                outs[0][:, cols] = e_ref[:, cols] + acc
            elif epi == "relu2":
                outs[0][:, cols] = _bf(acc)
                rl = jnp.maximum(acc, 0.0)
                outs[1][:, cols] = _bf(rl * rl)
            elif epi == "mask2relu":
                outs[0][:, cols] = _bf(acc * (2.0 * jnp.maximum(e_ref[:, cols].astype(F32), 0.0)))
        if norm_g is not None:
            hv = outs[0][...]
            r = lax.rsqrt(jnp.mean(hv * hv, axis=-1, keepdims=True) + EPS)
            outs[1][...] = _bf(hv * r * g_ref[...])

    in_specs = [pl.BlockSpec((tm, a.shape[1]), lambda i: (i, 0)) for a, _, _, _ in pairs]
    for _, _, wshape, widx in pairs:
        in_specs.append(pl.BlockSpec(wshape, functools.partial(lambda i, widx: widx, widx=widx)))
    args = [a for a, _, _, _ in pairs] + [w for _, w, _, _ in pairs]
    if extra is not None:
        in_specs.append(pl.BlockSpec((tm, n_out), lambda i: (i, 0)))
        args.append(extra)
    if norm_g is not None:
        in_specs.append(pl.BlockSpec((1, n_out), lambda i: (0, 0)))
        args.append(norm_g)
    in_specs += after_specs
    args += after_args
    row_out = pl.BlockSpec((tm, n_out), lambda i: (i, 0))
    if epi == "relu2":
        out_specs = [row_out, row_out]
        out_shape = [jax.ShapeDtypeStruct((M, n_out), BF16), jax.ShapeDtypeStruct((M, n_out), BF16)]
    elif norm_g is not None:
        out_specs = [row_out, row_out]
        out_shape = [jax.ShapeDtypeStruct((M, n_out), F32), jax.ShapeDtypeStruct((M, n_out), BF16)]
    else:
        dt = BF16 if epi == "mask2relu" else (F32 if epi == "res" else out_dtype)
        out_specs = row_out
        out_shape = jax.ShapeDtypeStruct((M, n_out), dt)
    return pl.pallas_call(
        body, name=name, grid=(M // tm,), in_specs=in_specs, out_specs=out_specs, out_shape=out_shape,
        compiler_params=_params(("parallel",)),
    )(*args)


def _whole(w):
    return w.shape, (0,) * w.ndim


def _mm_nn(name, a, w, **kw):
    n_out = w.shape[0] * w.shape[2] if w.ndim == 3 else w.shape[1]
    return _mm_rows(name, [(a, w, *_whole(w))], n_out, nt=False, **kw)


def _mm_nt(name, a, w, **kw):
    return _mm_rows(name, [(a, w, *_whole(w))], w.shape[0], nt=True, **kw)


def _mm_nt_rms_bwd(name, pairs, x, g, dres, *, tm, nc, after=None, shards=False):
    M, D = x.shape
    np_ = len(pairs)
    after_args, after_specs = _after_operand(after)

    def body(*refs):
        a_refs = refs[:np_]
        w_refs = refs[np_:2 * np_]
        x_ref, g_ref, r_ref = refs[2 * np_:2 * np_ + 3]
        dx_ref, dg_ref, dy_sc = refs[-3:]
        i = pl.program_id(0)
        a_vals = [_bf(a[...]) for a in a_refs]
        for j in range(D // nc):
            cols, acc = _chunk_product(a_vals, w_refs, j, nc, True, shards)
            dy_sc[:, cols] = acc
        xv = x_ref[...]
        r = lax.rsqrt(jnp.mean(xv * xv, axis=-1, keepdims=True) + EPS)
        xh = xv * r
        dy_v = dy_sc[...]
        dyg = dy_v * g_ref[...]
        c = jnp.mean(dyg * xh, axis=-1, keepdims=True)
        dx_ref[...] = r_ref[...] + r * (dyg - xh * c)
        part = jnp.sum(dy_v * xh, axis=0, keepdims=True)

        @pl.when(i == 0)
        def _():
            dg_ref[...] = part

        @pl.when(i > 0)
        def _():
            dg_ref[...] += part

    row = pl.BlockSpec((tm, D), lambda i: (i, 0))
    vec = pl.BlockSpec((1, D), lambda i: (0, 0))
    in_specs = [pl.BlockSpec((tm, a.shape[1]), lambda i: (i, 0)) for a, _, _, _ in pairs]
    for _, _, wshape, widx in pairs:
        in_specs.append(pl.BlockSpec(wshape, functools.partial(lambda i, widx: widx, widx=widx)))
    args = [a for a, _, _, _ in pairs] + [w for _, w, _, _ in pairs]
    return pl.pallas_call(
        body, name=name, grid=(M // tm,),
        in_specs=in_specs + [row, vec, row] + after_specs,
        out_specs=[row, vec],
        out_shape=[jax.ShapeDtypeStruct((M, D), F32), jax.ShapeDtypeStruct((1, D), F32)],
        scratch_shapes=[pltpu.VMEM((tm, D), F32)],
        compiler_params=_params(("arbitrary",)),
    )(*args, x, g, dres, *after_args)


def _mm_tn(name, a, b, *, t1, tn, tm=2048, slot_cols=None):
    M, K1 = a.shape
    N = b.shape[1]
    nm = M // tm

    def body(a_ref, b_ref, o_ref, acc_ref):
        m = pl.program_id(2)
        t = _dot(_bf(a_ref[...]), _bf(b_ref[...]), TN)

        @pl.when(m == 0)
        def _():
            acc_ref[...] = t

        @pl.when(m > 0)
        def _():
            acc_ref[...] += t

        @pl.when(m == nm - 1)
        def _():
            if slot_cols is None:
                o_ref[...] = _bf(acc_ref[...])
            else:
                for k in range(tn // slot_cols):
                    o_ref[k] = _bf(acc_ref[:, k * slot_cols:(k + 1) * slot_cols])

    if slot_cols is not None:
        out_spec = pl.BlockSpec((tn // slot_cols, t1, slot_cols), lambda i, j, m: (j, i, 0))
        out_shape = jax.ShapeDtypeStruct((N // slot_cols, K1, slot_cols), BF16)
    else:
        out_spec = pl.BlockSpec((t1, tn), lambda i, j, m: (i, j))
        out_shape = jax.ShapeDtypeStruct((K1, N), BF16)
    return pl.pallas_call(
        body, name=name, grid=(K1 // t1, N // tn, nm),
        in_specs=[pl.BlockSpec((tm, t1), lambda i, j, m: (m, i)), pl.BlockSpec((tm, tn), lambda i, j, m: (m, j))],
        out_specs=out_spec, out_shape=out_shape,
        scratch_shapes=[pltpu.VMEM((t1, tn), F32)],
        compiler_params=_params(("parallel", "parallel", "arbitrary")),
    )(a, b)


_INV_SQRT2 = 1.0 / math.sqrt(2.0)
_INV_SQRT2PI = 1.0 / math.sqrt(2.0 * math.pi)


def _gelu(x):
    return 0.5 * x * (1.0 + lax.erf(x * _INV_SQRT2))


def _gelu_and_grad(x):
    cdf = 0.5 * (1.0 + lax.erf(x * _INV_SQRT2))
    return x * cdf, cdf + x * (_INV_SQRT2PI * jnp.exp(-0.5 * x * x))


def _layer_norm_parts(v):
    mu = jnp.mean(v, axis=-1, keepdims=True)
    xc = v - mu
    rs = lax.rsqrt(jnp.mean(xc * xc, axis=-1, keepdims=True) + EPS)
    return xc * rs, rs


def _gate_fwd(name, uvp, ln_g, ln_b, wm, bs_full, tr=512):
    S, W2 = uvp.shape
    W = W2 // 2
    gd = W // GROUPS

    def body(u_ref, v_ref, lg_ref, lb_ref, wm_ref, bs_ref, z_ref):
        vh, _ = _layer_norm_parts(_gelu(v_ref[...]))
        vn = _bf(vh * lg_ref[...] + lb_ref[...])
        for ci in range(tr // CHUNK):
            rows = slice(ci * CHUNK, (ci + 1) * CHUNK)
            for g in range(GROUPS):
                cols = slice(g * gd, (g + 1) * gd)
                mixed = _dot(wm_ref[g], vn[rows, cols], NN) + bs_ref[:, cols]
                z_ref[rows, cols] = _bf(_gelu(u_ref[rows, cols]) * mixed)

    vec = pl.BlockSpec((1, W), lambda i: (0, 0))
    return pl.pallas_call(
        body, name=name, grid=(S // tr,),
        in_specs=[pl.BlockSpec((tr, W), lambda i: (i, 0)), pl.BlockSpec((tr, W), lambda i: (i, 1)), vec, vec,
                  pl.BlockSpec((GROUPS, CHUNK, CHUNK), lambda i: (0, 0, 0)),
                  pl.BlockSpec((CHUNK, W), lambda i: (0, 0))],
        out_specs=pl.BlockSpec((tr, W), lambda i: (i, 0)),
        out_shape=jax.ShapeDtypeStruct((S, W), BF16),
        compiler_params=_params(("parallel",)),
    )(uvp, uvp, ln_g, ln_b, wm, bs_full)


def _gate_bwd(name, uvp, dz, ln_g, ln_b, wm, bs_full, tr=256):
    S, W2 = uvp.shape
    W = W2 // 2
    gd = W // GROUPS
    n_steps = S // tr

    def body(u_ref, v_ref, dz_ref, lg_ref, lb_ref, wm_ref, bs_ref, duv_ref, dwm_ref, dmx_ref, dlg_ref, dlb_ref,
             dvn_ref):
        i = pl.program_id(0)
        v, dv_dvp = _gelu_and_grad(v_ref[...])
        vh, rs = _layer_norm_parts(v)
        lg = lg_ref[...]
        vn = _bf(vh * lg + lb_ref[...])

        @pl.when(i == 0)
        def _():
            dwm_ref[...] = jnp.zeros_like(dwm_ref)
            dmx_ref[...] = jnp.zeros_like(dmx_ref)
            dlg_ref[...] = jnp.zeros_like(dlg_ref)
            dlb_ref[...] = jnp.zeros_like(dlb_ref)

        for ci in range(tr // CHUNK):
            rows = slice(ci * CHUNK, (ci + 1) * CHUNK)
            for g in range(GROUPS):
                cols = slice(g * gd, (g + 1) * gd)
                u, du_dup = _gelu_and_grad(u_ref[rows, cols])
                dz_v = dz_ref[rows, cols]
                dmixed = dz_v * u
                dmx_ref[:, cols] += dmixed
                dmixed_b = _bf(dmixed)
                mixed = _dot(wm_ref[g], vn[rows, cols], NN) + bs_ref[:, cols]
                duv_ref[rows, cols] = _bf(dz_v * mixed * du_dup)
                dwm_ref[g] += _dot(dmixed_b, vn[rows, cols], NT)
                dvn_ref[rows, cols] = _dot(wm_ref[g], dmixed_b, TN)
        dvn = dvn_ref[...]
        dlg_ref[...] += jnp.sum(dvn * vh, axis=0, keepdims=True)
        dlb_ref[...] += jnp.sum(dvn, axis=0, keepdims=True)
        dvh = dvn * lg
        dv = rs * (dvh - jnp.mean(dvh, axis=-1, keepdims=True) - vh * jnp.mean(dvh * vh, axis=-1, keepdims=True))
        duv_ref[:, W:] = _bf(dv * dv_dvp)

        @pl.when(i == n_steps - 1)
        def _():
            t_idx = lax.broadcasted_iota(jnp.int32, (CHUNK, CHUNK), 0)
            s_idx = lax.broadcasted_iota(jnp.int32, (CHUNK, CHUNK), 1)
            keep = (s_idx <= t_idx).astype(F32)
            for g in range(GROUPS):
                dwm_ref[g] = dwm_ref[g] * keep

    vec = pl.BlockSpec((1, W), lambda i: (0, 0))
    row = pl.BlockSpec((tr, W), lambda i: (i, 0))
    return pl.pallas_call(
        body, name=name, grid=(n_steps,),
        in_specs=[row, pl.BlockSpec((tr, W), lambda i: (i, 1)), row, vec, vec,
                  pl.BlockSpec((GROUPS, CHUNK, CHUNK), lambda i: (0, 0, 0)),
                  pl.BlockSpec((CHUNK, W), lambda i: (0, 0))],
        out_specs=[pl.BlockSpec((tr, W2), lambda i: (i, 0)),
                   pl.BlockSpec((GROUPS, CHUNK, CHUNK), lambda i: (0, 0, 0)),
                   pl.BlockSpec((CHUNK, W), lambda i: (0, 0)), vec, vec],
        out_shape=[jax.ShapeDtypeStruct((S, W2), BF16), jax.ShapeDtypeStruct((GROUPS, CHUNK, CHUNK), F32),
                   jax.ShapeDtypeStruct((CHUNK, W), F32), jax.ShapeDtypeStruct((1, W), F32),
                   jax.ShapeDtypeStruct((1, W), F32)],
        scratch_shapes=[pltpu.VMEM((tr, W), F32)],
        compiler_params=_params(("arbitrary",)),
    )(uvp, uvp, dz, ln_g, ln_b, wm, bs_full)


def _t5_bucket(distance):
    small = distance < MAX_EXACT
    nf = jnp.maximum(distance, 1).astype(F32)
    large = MAX_EXACT + (jnp.log(nf / MAX_EXACT) / math.log(REL_MAX_DISTANCE / MAX_EXACT)
                         * (N_BUCKETS - MAX_EXACT)).astype(jnp.int32)
    large = jnp.minimum(large, N_BUCKETS - 1)
    return jnp.where(small, distance, large)


TILE_ELEMS = 2 * CHUNK * CHUNK


def _band_buckets():
    rel = CHUNK + jnp.arange(CHUNK)[None, :] - jnp.arange(2 * CHUNK)[:, None]
    band = (rel >= 0) & (rel <= CHUNK)
    buckets = [_t5_bucket(jnp.clip(rel, 0, CHUNK) * d) for d in DILATIONS]
    return jnp.stack(buckets), band


def _bucket_onehot():
    buckets, _ = _band_buckets()
    return (buckets.reshape(N_DIL, 1, TILE_ELEMS) == jnp.arange(N_BUCKETS)[None, :, None]).astype(F32)


def _bias_tiles(name, rel_bias, after=None):
    _, band = _band_buckets()
    own = band & (jnp.arange(2 * CHUNK) >= CHUNK)[:, None]
    masks = jnp.stack([own, band]).reshape(2, TILE_ELEMS).astype(F32)
    tables = jnp.transpose(rel_bias.reshape(N_BUCKETS, N_DIL, ATT_HEADS), (1, 2, 0))
    after_args, after_specs = _after_operand(after)

    def body(t_ref, oh_ref, m_ref, *rest):
        out_ref = rest[-1]
        for g in range(N_DIL):
            bias = lax.dot_general(t_ref[g], oh_ref[g], (NN, ((), ())), precision=lax.Precision.HIGHEST,
                                   preferred_element_type=F32)
            for f in range(2):
                out_ref[g, f] = jnp.where(m_ref[f:f + 1, :] > 0.5, bias, NEG_INF)

    whole = pl.BlockSpec(memory_space=pltpu.VMEM)
    out = pl.pallas_call(
        body, name=name, out_shape=jax.ShapeDtypeStruct((N_DIL, 2, ATT_HEADS, TILE_ELEMS), F32),
        in_specs=[whole, whole, whole] + after_specs, out_specs=whole,
        compiler_params=_params(),
    )(tables, _bucket_onehot(), masks, *after_args)
    out = out.reshape(N_DIL, 2, ATT_HEADS // 2, 2, 2 * CHUNK, CHUNK)
    return jnp.transpose(out, (0, 1, 2, 4, 3, 5)).reshape(N_DIL, 2, ATT_HEADS // 2, 2 * CHUNK, 2 * CHUNK)


def _att_specs(order):
    def column(part, ids):
        hp, g, _ = order(*ids)
        return part * 3 * 4 + g * 4 + hp

    def window(part):
        def index(*ids):
            c = order(*ids)[2]
            return pl.multiple_of(jnp.maximum(c - 1, 0) * ATT_ROWS, ATT_ROWS), column(part, ids) * LANES
        return pl.BlockSpec((pl.Element(2 * ATT_ROWS), pl.Element(LANES)), index)

    return [pl.BlockSpec((ATT_ROWS, LANES), lambda *ids: (order(*ids)[2], column(0, ids))), window(1), window(2)]


def _window_base(c):
    return jnp.where(c == 0, 0, ATT_ROWS)


def _rows(start, d):
    if d == 1:
        return pl.ds(pl.multiple_of(start, CHUNK), CHUNK)
    return pl.ds(start, CHUNK, stride=d)


def _att_tile_offsets(t, d):
    n = t // d
    r = t % d
    return n * (CHUNK * d) + r, n


def _head_pair_columns(x_t):
    zeros = jnp.zeros((HEAD_DIM, CHUNK), x_t.dtype)
    return jnp.concatenate([jnp.concatenate([x_t[:HEAD_DIM], zeros], axis=0),
                            jnp.concatenate([zeros, x_t[HEAD_DIM:]], axis=0)], axis=1)


def _head_pair_rows(y):
    return jnp.concatenate([y[:HEAD_DIM, :CHUNK], y[HEAD_DIM:, CHUNK:]], axis=0)


def _att_fwd(name, qkv, bias_tiles):
    S = qkv.shape[0]
    n_chunks = S // ATT_ROWS
    tiles = ATT_ROWS // CHUNK

    def body(q_ref, kk, vv, b_ref, o_ref, l_ref):
        c = pl.program_id(1)
        g = pl.program_id(2)
        base = _window_base(c)

        for gi, d in enumerate(DILATIONS):
            @pl.when(g == gi)
            def _(d=d):
                span = CHUNK * d

                def tile(t, carry):
                    q0, n = _att_tile_offsets(t, d)
                    first = (c == 0) & (n == 0)
                    rows = _rows(q0, d)
                    cur = _rows(base + q0, d)
                    prev = _rows(jnp.where(first, q0, base + q0 - span), d)
                    inner = jnp.where(first, 0, 1)
                    qq = _head_pair_columns(_bf(q_ref[rows, :] * ATT_SCALE).T)
                    s_p = _dot(_bf(kk[prev, :]), qq, NN) + b_ref[inner, 0:CHUNK, :]
                    s_c = _dot(_bf(kk[cur, :]), qq, NN) + b_ref[inner, CHUNK:2 * CHUNK, :]
                    m = jnp.maximum(jnp.max(s_p, axis=0, keepdims=True), jnp.max(s_c, axis=0, keepdims=True))
                    p_p = jnp.exp(s_p - m)
                    p_c = jnp.exp(s_c - m)
                    l = jnp.sum(p_p, axis=0, keepdims=True) + jnp.sum(p_c, axis=0, keepdims=True)
                    o2 = (_dot(_bf(vv[prev, :]).T, _bf(p_p), NN)
                          + _dot(_bf(vv[cur, :]).T, _bf(p_c), NN)) * (1.0 / l)
                    lse = m + jnp.log(l)
                    l_t = jnp.concatenate([jnp.broadcast_to(lse[:, :CHUNK], (HEAD_DIM, CHUNK)),
                                           jnp.broadcast_to(lse[:, CHUNK:], (HEAD_DIM, CHUNK))], axis=0)
                    o_ref[rows, :] = _head_pair_rows(o2).T
                    l_ref[rows, :] = l_t.T
                    return carry

                lax.fori_loop(0, tiles, tile, 0, unroll=8)

    order = lambda hp, c, g: (hp, g, c)
    out_spec = pl.BlockSpec((None, ATT_ROWS, LANES), lambda hp, c, g: (g, c, hp))
    shape = jax.ShapeDtypeStruct((N_DIL, S, ATT_WIDTH), F32)
    return pl.pallas_call(
        body, name=name, grid=(ATT_HEADS // 2, n_chunks, N_DIL),
        in_specs=_att_specs(order) + [
            pl.BlockSpec((None, 2, None, 2 * CHUNK, 2 * CHUNK), lambda hp, c, g: (g, 0, hp, 0, 0))],
        out_specs=[out_spec, out_spec],
        out_shape=[shape, shape],
        compiler_params=_params(("parallel", "parallel", "parallel")),
    )(qkv, qkv, qkv, bias_tiles)


def _att_merge(name, o_g, l_g, tm=512):
    _, S, W = o_g.shape

    def body(o_ref, l_ref, out_ref, lse_ref):
        ls = [l_ref[g] for g in range(N_DIL)]
        mx = functools.reduce(jnp.maximum, ls)
        ws = [jnp.exp(l - mx) for l in ls]
        tot = functools.reduce(lambda a, b: a + b, ws)
        acc = ws[0] * o_ref[0]
        for g in range(1, N_DIL):
            acc = acc + ws[g] * o_ref[g]
        out_ref[...] = acc / tot
        lse_ref[...] = mx + jnp.log(tot)

    blk = pl.BlockSpec((N_DIL, tm, W), lambda i: (0, i, 0))
    row = pl.BlockSpec((tm, W), lambda i: (i, 0))
    shape = jax.ShapeDtypeStruct((S, W), F32)
    return pl.pallas_call(
        body, name=name, grid=(S // tm,), in_specs=[blk, blk], out_specs=[row, row], out_shape=[shape, shape],
        compiler_params=_params(("parallel",)),
    )(o_g, l_g)


def _att_bwd(name, qkv, o, lse, d_o, bias_tiles):
    S = qkv.shape[0]
    n_chunks = S // ATT_ROWS
    tiles = ATT_ROWS // CHUNK

    def body(q_ref, kk, vv, o_ref, l_ref, do_ref, b_ref, dq_out, dk_out, dv_out, ds_ref, dq_ref, dk_ref, dv_ref):
        g = pl.program_id(1)
        c = pl.program_id(2)

        @pl.when(c == 0)
        def _():
            dk_ref[...] = jnp.zeros_like(dk_ref)
            dv_ref[...] = jnp.zeros_like(dv_ref)
            ds_ref[...] = jnp.zeros_like(ds_ref)

        base = _window_base(c)
        first_row = c * ATT_ROWS
        head0 = lax.broadcasted_iota(jnp.int32, (CHUNK, LANES), 1) < HEAD_DIM

        def head_pair_stack(x):
            zero = jnp.zeros_like(x)
            return jnp.concatenate([jnp.where(head0, x, zero), jnp.where(head0, zero, x)], axis=0)

        for gi, d in enumerate(DILATIONS):
            @pl.when(g == gi)
            def _(d=d):
                span = CHUNK * d

                def tile(t, carry):
                    q0, n = _att_tile_offsets(t, d)
                    first = (c == 0) & (n == 0)
                    rows = _rows(q0, d)
                    cur = _rows(base + q0, d)
                    prev = _rows(jnp.where(first, q0, base + q0 - span), d)
                    inner = jnp.where(first, 0, 1)
                    g_cur = _rows(first_row + q0, d)
                    g_prev = _rows(jnp.where(first, q0, first_row + q0 - span), d)
                    q2 = _bf(q_ref[rows, :] * ATT_SCALE)
                    q_t = q2.T
                    k2 = _bf(jnp.concatenate([kk[prev, :], kk[cur, :]], axis=0))
                    k_t = k2.T
                    v2 = _bf(jnp.concatenate([vv[prev, :], vv[cur, :]], axis=0))
                    do2 = do_ref[rows, :]
                    do_b = _bf(do2)
                    do_t = do_b.T
                    lse_t = l_ref[rows, :].T
                    dd_t = (do2 * o_ref[rows, :]).T
                    lse = jnp.concatenate([lse_t[0:1], lse_t[HEAD_DIM:HEAD_DIM + 1]], axis=1)
                    delta = jnp.concatenate([jnp.sum(dd_t[:HEAD_DIM], axis=0, keepdims=True),
                                             jnp.sum(dd_t[HEAD_DIM:], axis=0, keepdims=True)], axis=1)
                    s = _dot(k2, _head_pair_columns(q_t), NN) + b_ref[inner]
                    p = jnp.exp(s - lse)
                    ds = p * (_dot(v2, _head_pair_columns(do_t), NN) - delta)
                    ds_ref[...] += ds
                    ds_b = _bf(ds)
                    dq_t = _head_pair_rows(_dot(k_t, ds_b, NN))
                    dk2 = _dot(ds_b, head_pair_stack(q2), NN)
                    dv2 = _dot(_bf(p), head_pair_stack(do_b), NN)
                    dq_ref[rows, :] = (dq_t * ATT_SCALE).T
                    dk_ref[g_prev, :] += dk2[0:CHUNK]
                    dk_ref[g_cur, :] += dk2[CHUNK:2 * CHUNK]
                    dv_ref[g_prev, :] += dv2[0:CHUNK]
                    dv_ref[g_cur, :] += dv2[CHUNK:2 * CHUNK]
                    return carry

                lax.fori_loop(0, tiles, tile, 0, unroll=8)

        dq_out[...] = _bf(dq_ref[...])

        @pl.when(c == n_chunks - 1)
        def _():
            dk_out[...] = _bf(dk_ref[...])
            dv_out[...] = _bf(dv_ref[...])

    order = lambda hp, g, c: (hp, g, c)
    chunk = pl.BlockSpec((ATT_ROWS, LANES), lambda hp, g, c: (c, hp))
    slab = pl.BlockSpec((S, LANES), lambda hp, g, c: (0, g * 4 + hp))
    width = N_DIL * ATT_WIDTH
    dq, dk, dv, ds_sums = pl.pallas_call(
        body, name=name, grid=(ATT_HEADS // 2, N_DIL, n_chunks),
        in_specs=_att_specs(order) + [chunk, chunk, chunk,
                                      pl.BlockSpec((None, 2, None, 2 * CHUNK, 2 * CHUNK),
                                                   lambda hp, g, c: (g, 0, hp, 0, 0))],
        out_specs=[pl.BlockSpec((ATT_ROWS, LANES), lambda hp, g, c: (c, g * 4 + hp)), slab, slab,
                   pl.BlockSpec((None, None, 2 * CHUNK, 2 * CHUNK), lambda hp, g, c: (g, hp, 0, 0))],
        out_shape=[jax.ShapeDtypeStruct((S, width), BF16), jax.ShapeDtypeStruct((S, width), BF16),
                   jax.ShapeDtypeStruct((S, width), BF16),
                   jax.ShapeDtypeStruct((N_DIL, ATT_HEADS // 2, 2 * CHUNK, 2 * CHUNK), F32)],
        scratch_shapes=[pltpu.VMEM((ATT_ROWS, LANES), F32), pltpu.VMEM((S, LANES), F32),
                        pltpu.VMEM((S, LANES), F32)],
        compiler_params=_params(("parallel", "parallel", "arbitrary")),
    )(qkv, qkv, qkv, o, lse, d_o, bias_tiles)
    ds_sums = ds_sums.reshape(N_DIL, ATT_HEADS // 2, 2 * CHUNK, 2, CHUNK)
    ds_sums = jnp.transpose(ds_sums, (0, 1, 3, 2, 4)).reshape(N_DIL, ATT_HEADS, 2 * CHUNK, CHUNK)
    return dq, dk, dv, ds_sums


def _bias_grad(name, ds_sums):
    flat = ds_sums.reshape(N_DIL, ATT_HEADS, TILE_ELEMS)

    def body(oh_ref, ds_ref, out_ref):
        for g in range(N_DIL):
            out_ref[g] = lax.dot_general(oh_ref[g], ds_ref[g], (NT, ((), ())), precision=lax.Precision.HIGHEST,
                                         preferred_element_type=F32)

    out = pl.pallas_call(
        body, name=name, out_shape=jax.ShapeDtypeStruct((N_DIL, N_BUCKETS, ATT_HEADS), F32),
        compiler_params=_params(),
    )(_bucket_onehot(), flat)
    return jnp.transpose(out, (1, 0, 2)).reshape(N_BUCKETS, N_DIL * ATT_HEADS)


def _peers():
    x, y, c = lax.axis_index("x"), lax.axis_index("y"), lax.axis_index("c")
    me = 4 * x + 2 * y + c
    others = [(x, y, 1 - c), (1 - x, y, c), (x, 1 - y, c), (1 - x, 1 - y, c),
              (1 - x, y, 1 - c), (x, 1 - y, 1 - c), (1 - x, 1 - y, 1 - c)]
    return me, others


def _slot(dev):
    return 4 * dev[0] + 2 * dev[1] + dev[2]


_HBM =pl.BlockSpec(memory_space=pltpu.HBM)
_SEM = pl.BlockSpec(memory_space=pltpu.SEMAPHORE)
_EFFECT = pltpu.SideEffectType.DATAFLOW_SIDE_EFFECTING


def _my_slot():
    return 4 * lax.axis_index("x") + 2 * lax.axis_index("y") + lax.axis_index("c")


def _exchange_copy(src_ref, land_ref, send_sems, recv_sems, k, dev, me, scatter, arriving):
    src = src_ref.at[me if arriving else _slot(dev)] if scatter else src_ref
    dst = land_ref.at[_slot(dev) if arriving else me]
    return pltpu.make_async_remote_copy(src_ref=src, dst_ref=dst, send_sem=send_sems.at[k], recv_sem=recv_sems.at[k],
                                        device_id=dev, device_id_type=MESH)


def _exchange_start(name, srcs, scatter):
    n = len(srcs)
    me = _my_slot()
    landings = []
    for src in srcs:
        own = lax.dynamic_index_in_dim(src, me, 0, keepdims=True) if scatter else src[None]
        landings.append(lax.dynamic_update_slice(lax.empty((N_DEV,) + src.shape[-2:], src.dtype), own, (me, 0, 0)))

    def body(*refs):
        src_refs, land_refs = refs[:n], refs[n:2 * n]
        send_sems, recv_sems = refs[2 * n:2 * n + 2]
        token = refs[-1]
        me, others = _peers()
        for p in range(n):
            for k, dev in enumerate(others):
                _exchange_copy(src_refs[p], land_refs[p], send_sems, recv_sems, p * (N_DEV - 1) + k, dev, me,
                               scatter, False).start()
        token[...] = jnp.zeros_like(token)

    sems = pltpu.SemaphoreType.DMA((n * (N_DEV - 1),))
    hbm = lambda a: pltpu.with_memory_space_constraint(a, pltpu.HBM)
    outs = pl.pallas_call(
        body, name=name,
        out_shape=(sems, sems, *[pltpu.HBM(a.shape, a.dtype) for a in srcs + landings],
                   jax.ShapeDtypeStruct((8, LANES), F32)),
        in_specs=(_HBM,) * (2 * n), out_specs=(_SEM, _SEM) + (_HBM,) * (2 * n) + (pl.BlockSpec(memory_space=pltpu.VMEM),),
        input_output_aliases={i: 2 + i for i in range(2 * n)},
        compiler_params=pltpu.CompilerParams(has_side_effects=_EFFECT),
    )(*[hbm(a) for a in srcs + landings])
    return (outs[0], outs[1], list(outs[2:2 + n]), list(outs[2 + n:2 + 2 * n]), scatter), outs[-1]


def _exchange_wait(name, handle, after):
    send_sems, recv_sems, src_thru, land_thru, scatter = handle
    n = len(src_thru)

    def body(*refs):
        src_refs, land_refs = refs[:n], refs[n:2 * n]
        send_sems, recv_sems = refs[2 * n:2 * n + 2]
        me, others = _peers()
        for p in range(n):
            for k, dev in enumerate(others):
                cp = _exchange_copy(src_refs[p], land_refs[p], send_sems, recv_sems, p * (N_DEV - 1) + k, dev, me,
                                    scatter, True)
                cp.wait_send()
                cp.wait_recv()

    outs = pl.pallas_call(
        body, name=name,
        out_shape=tuple(pltpu.HBM(a.shape, a.dtype) for a in src_thru + land_thru),
        in_specs=(_HBM,) * (2 * n) + (_SEM, _SEM, pl.BlockSpec(memory_space=pl.ANY)), out_specs=(_HBM,) * (2 * n),
        input_output_aliases={i: i for i in range(2 * n)},
        compiler_params=pltpu.CompilerParams(has_side_effects=_EFFECT),
    )(*src_thru, *land_thru, send_sems, recv_sems, after)
    return list(outs[n:])


def _adamw_math(w, g, m, v):
    m = ADAM_B1 * m + (1.0 - ADAM_B1) * g
    v = ADAM_B2 * v + (1.0 - ADAM_B2) * (g * g)
    m_hat = m / (1.0 - ADAM_B1 ** ADAM_STEP)
    v_hat = v / (1.0 - ADAM_B2 ** ADAM_STEP)
    delta = -ADAM_LR * (m_hat / (jnp.sqrt(v_hat) + ADAM_EPS) + ADAM_WD * w)
    return delta, m, v


_SMALL_WIDE = (("mix_norm_g", 2), ("mlp_norm_g", 2), ("final_norm_g", 1), ("a_ln_g", 1), ("a_ln_b", 1))
_SMALL_NARROW = (("a_w_s", GROUPS * CHUNK), ("a_b_s", GROUPS), ("rel_bias", N_BUCKETS))
_SMALL = tuple(n for n, _ in _SMALL_WIDE + _SMALL_NARROW)
_BIAS_COLS = N_DIL * ATT_HEADS


def _pack_small_grads(grads, loss_term):
    D = grads["a_ln_g"].shape[-1]
    tiles = [jnp.pad(grads[n].reshape(k, D), ((0, 8 - k), (0, 0))) for n, k in _SMALL_WIDE]
    tiles.append(jnp.pad(loss_term.reshape(1, 1), ((0, 7), (0, D - 1))))
    narrow = [grads["a_w_s"].reshape(-1, LANES), grads["a_b_s"].reshape(-1, LANES),
              jnp.pad(grads["rel_bias"], ((0, 0), (0, LANES - _BIAS_COLS)))]
    return jnp.concatenate(tiles, axis=0), jnp.concatenate(narrow, axis=0)


def _adamw_small(name, g_wide, g_narrow, w, m, v):
    D = g_wide.shape[-1]
    shapes = {n: (k, D) for n, k in _SMALL_WIDE}
    shapes.update({n: (k, LANES) for n, k in _SMALL_NARROW})
    shapes["rel_bias"] = (N_BUCKETS, _BIAS_COLS)
    n_t = len(_SMALL)

    def body(gw_ref, gn_ref, *rest):
        params = rest[:3 * n_t]
        outs = rest[3 * n_t:3 * n_t + 4 * n_t]
        loss_ref, sw, sn = rest[-3:]
        sw[...] = functools.reduce(lambda a, b: a + b, [gw_ref[j] for j in range(N_DEV)])
        sn[...] = functools.reduce(lambda a, b: a + b, [gn_ref[j] for j in range(N_DEV)])
        row = 0
        for i, n in enumerate(_SMALL):
            k, cols = shapes[n]
            if i < len(_SMALL_WIDE):
                g = sw[8 * i:8 * i + k, :]
            else:
                g = sn[row:row + k, 0:cols]
                row += k
            w_ref, m_ref, v_ref = params[3 * i:3 * i + 3]
            delta, m_new, v_new = _adamw_math(w_ref[...], g, m_ref[...], v_ref[...])
            for out, val in zip(outs[4 * i:4 * i + 4], (g, delta, m_new, v_new)):
                out[...] = val
        loss_ref[...] = sw[8 * len(_SMALL_WIDE):8 * len(_SMALL_WIDE) + 8, 0:LANES]

    whole = pl.BlockSpec(memory_space=pltpu.VMEM)
    args = [t[n].reshape(shapes[n]) for n in _SMALL for t in (w, m, v)]
    res = pl.pallas_call(
        body, name=name,
        in_specs=[whole] * (2 + len(args)), out_specs=[whole] * (4 * n_t + 1),
        out_shape=[jax.ShapeDtypeStruct(shapes[n], F32) for n in _SMALL for _ in range(4)]
        + [jax.ShapeDtypeStruct((8, LANES), F32)],
        scratch_shapes=[pltpu.VMEM(g_wide.shape[1:], F32), pltpu.VMEM(g_narrow.shape[1:], F32)],
        compiler_params=_params(),
    )(g_wide, g_narrow, *args)
    small = {n: tuple(r.reshape(w[n].shape) for r in res[4 * i:4 * i + 4]) for i, n in enumerate(_SMALL)}
    return small, res[-1][0, 0]


def _adamw_shard(name, parts, w, m, v, layer, earlier=None, after=None, tr=256):
    L, K, N = w.shape
    tr = min(tr, K)
    n_prev = 0 if earlier is None else 4
    after_args, after_specs = _after_operand(after)

    def body(p_ref, w_ref, m_ref, v_ref, *rest):
        g_out, d_out, m_out, v_out = rest[n_prev + len(after_args):]
        g = p_ref[0].astype(F32)
        for j in range(1, N_DEV):
            g = g + p_ref[j].astype(F32)
        delta, m_new, v_new = _adamw_math(w_ref[...], g, m_ref[...], v_ref[...])
        g_out[...] = g
        d_out[...] = delta
        m_out[...] = m_new
        v_out[...] = v_new

    row = pl.BlockSpec((None, tr, N), lambda i: (layer, i, 0))
    shape = jax.ShapeDtypeStruct((L, K, N), F32)
    return pl.pallas_call(
        body, name=name, grid=(K // tr,),
        in_specs=[pl.BlockSpec((N_DEV, tr, N), lambda i: (0, i, 0)), row, row, row]
        + [pl.BlockSpec(memory_space=pl.ANY)] * n_prev + after_specs,
        out_specs=[row, row, row, row],
        out_shape=[shape, shape, shape, shape],
        input_output_aliases={4 + j: j for j in range(n_prev)},
        compiler_params=_params(("parallel",)),
    )(parts, w, m, v, *(earlier or ()), *after_args)


def _column_slots(full):
    K, N = full.shape
    return jnp.transpose(full.reshape(K, N_DEV, N // N_DEV), (1, 0, 2))


def _from_column_slots(slots):
    _, K, n = slots.shape
    return jnp.transpose(slots, (1, 0, 2)).reshape(K, N_DEV * n)


_STAGES = (("gate", ("a_w_in", "a_w_out"), 0),
           ("mlp0", ("w_up", "w_down"), 0),
           ("att", ("b_w_qkv", "b_w_out"), 0),
           ("mlp1", ("w_up", "w_down"), 1))


def kernel(x, mix_norm_g, mlp_norm_g, final_norm_g, a_w_in, a_ln_g, a_ln_b, a_w_s, a_b_s, a_w_out, b_w_qkv, b_w_out, rel_bias, w_up, w_down, loss_target, m_mix_norm_g, m_mlp_norm_g, m_final_norm_g, m_a_w_in, m_a_ln_g, m_a_ln_b, m_a_w_s, m_a_b_s, m_a_w_out, m_b_w_qkv, m_b_w_out, m_rel_bias, m_w_up, m_w_down, v_mix_norm_g, v_mlp_norm_g, v_final_norm_g, v_a_w_in, v_a_ln_g, v_a_ln_b, v_a_w_s, v_a_b_s, v_a_w_out, v_b_w_qkv, v_b_w_out, v_rel_bias, v_w_up, v_w_down):
    w = dict(mix_norm_g=mix_norm_g, mlp_norm_g=mlp_norm_g, final_norm_g=final_norm_g, a_w_in=a_w_in, a_ln_g=a_ln_g,
             a_ln_b=a_ln_b, a_w_s=a_w_s, a_b_s=a_b_s, a_w_out=a_w_out, b_w_qkv=b_w_qkv, b_w_out=b_w_out,
             rel_bias=rel_bias, w_up=w_up, w_down=w_down)
    m = dict(mix_norm_g=m_mix_norm_g, mlp_norm_g=m_mlp_norm_g, final_norm_g=m_final_norm_g, a_w_in=m_a_w_in,
             a_ln_g=m_a_ln_g, a_ln_b=m_a_ln_b, a_w_s=m_a_w_s, a_b_s=m_a_b_s, a_w_out=m_a_w_out, b_w_qkv=m_b_w_qkv,
             b_w_out=m_b_w_out, rel_bias=m_rel_bias, w_up=m_w_up, w_down=m_w_down)
    v = dict(mix_norm_g=v_mix_norm_g, mlp_norm_g=v_mlp_norm_g, final_norm_g=v_final_norm_g, a_w_in=v_a_w_in,
             a_ln_g=v_a_ln_g, a_ln_b=v_a_ln_b, a_w_s=v_a_w_s, a_b_s=v_a_b_s, a_w_out=v_a_w_out, b_w_qkv=v_b_w_qkv,
             b_w_out=v_b_w_out, rel_bias=v_rel_bias, w_up=v_w_up, w_down=v_w_down)

    stages = {s: (names, layer) for s, names, layer in _STAGES}
    order = [s for s, _, _ in _STAGES]

    def shards_of(stage):
        names, layer = stages[stage]
        return [_bf(w[n][layer]) for n in names]

    pending = {}
    pending[order[0]], first_token = _exchange_start("gather_" + order[0] + "_start", shards_of(order[0]), False)

    def get_weights(stage, dep):
        gathered = _exchange_wait("gather_" + stage + "_wait", pending.pop(stage), dep)
        nxt = order.index(stage) + 1
        token = None
        if nxt < len(order):
            shards, gathered = lax.optimization_barrier((shards_of(order[nxt]), gathered))
            pending[order[nxt]], token = _exchange_start("gather_" + order[nxt] + "_start", shards, False)
        return gathered, token

    sent = {}

    def put_grads(stage, slot_grads):
        sent[stage], token = _exchange_start("scatter_" + stage + "_start", slot_grads, True)
        return token

    loss_local, grad_x, small_g = _local_step(
        x[0], loss_target[0], mix_norm_g, mlp_norm_g, final_norm_g, a_ln_g, a_ln_b, a_w_s, a_b_s, rel_bias,
        get_weights, put_grads, first_token)

    small_sent, token = _exchange_start("gather_small_start", list(_pack_small_grads(small_g, loss_local)), False)

    results = {}
    prev = token
    for stage in reversed(order):
        names, layer = stages[stage]
        received = _exchange_wait("scatter_" + stage + "_wait", sent[stage], prev)
        for n, parts in zip(names, received):
            results[n] = _adamw_shard("adamw_%s_%s" % (stage, n), parts, w[n], m[n], v[n], layer, results.get(n),
                                      after=prev)
            prev = results[n][0]

    g_wide, g_narrow = _exchange_wait("gather_small_wait", small_sent, prev)
    small, loss = _adamw_small("adamw_small", g_wide, g_narrow, w, m, v)

    outs = []
    for j in range(4):
        outs.extend(small[n][j] if n in _SMALL else results[n][j] for n in w)
    return (loss, grad_x[None], *outs)


def _local_step(xs, tgt, mix_norm_g, mlp_norm_g, final_norm_g, a_ln_g, a_ln_b, a_w_s, a_b_s, rel_bias,
                get_weights, put_grads, first_token=None):
    D = xs.shape[-1]
    g_mix = [mix_norm_g[l][None, :] for l in range(2)]
    g_mlp = [mlp_norm_g[l][None, :] for l in range(2)]
    g_fin = final_norm_g[None, :]
    ln_g, ln_b = a_ln_g, a_ln_b
    causal = jnp.tril(jnp.ones((CHUNK, CHUNK), dtype=bool))
    wm = _bf(jnp.where(causal[None], a_w_s[0], 0.0))
    bs_full = jnp.repeat(a_b_s[0].T, D // GROUPS, axis=1)
    bias_tiles = _bias_tiles("att_bias", rel_bias, after=first_token)

    (win, wout), token = get_weights("gate", bias_tiles)
    wout = wout.reshape(-1, D)
    y0 = _rms_fwd("rms_mix0", xs, g_mix[0], after=token)
    uvp = _mm_nn("gate_in", y0, win, tm=512, nc=win.shape[2], shards=True)
    z = _gate_fwd("gate_mid", uvp, ln_g, ln_b, wm, bs_full)
    h1, y1 = _mm_nn("gate_out", z, wout, tm=512, nc=512, epi="res", extra=xs, norm_g=g_mlp[0])
    (wup0, wdn0), token = get_weights("mlp0", h1)
    wdn0 = wdn0.reshape(-1, D)
    a0, f0 = _mm_nn("mlp0_up", y1, wup0, tm=512, nc=wup0.shape[2], epi="relu2", shards=True, after=token)
    h2, y2 = _mm_nn("mlp0_down", f0, wdn0, tm=512, nc=512, epi="res", extra=h1, norm_g=g_mix[1])
    (wqkv, wo), token = get_weights("att", h2)
    wqkv, wo = _from_column_slots(wqkv), _from_column_slots(wo)
    qkv = _mm_nn("att_qkv", y2, wqkv, tm=512, nc=512, after=token)
    o_att, lse = _att_merge("att_merge", *_att_fwd("att_fwd", qkv, bias_tiles))
    h3, y3 = _mm_nn("att_out", o_att, wo, tm=512, nc=512, epi="res", extra=h2, norm_g=g_mlp[1])
    (wup1, wdn1), _ = get_weights("mlp1", h3)
    wdn1 = wdn1.reshape(-1, D)
    a1, f1 = _mm_nn("mlp1_up", y3, wup1, tm=512, nc=wup1.shape[2], epi="relu2", shards=True)
    dh, dg_fin, err2 = _mm_res_loss("mlp1_down_loss", f1, wdn1, h3, g_fin, tgt, tm=512, nc=512)
    loss_local = 0.5 * jnp.sum(err2) / D

    def mlp_bwd(tag, dh, h_in, y, a, f, wup_l, wdn_l, g_row, after):
        da = _mm_nt(tag + "_dact", dh, wdn_l, tm=512, nc=512, epi="mask2relu", extra=a, after=after)
        g_dn = _mm_tn(tag + "_dwdown", f, dh, t1=1024, tn=1024)
        g_up = _mm_tn(tag + "_dwup", y, da, t1=1024, tn=1024, slot_cols=wup_l.shape[2])
        dh_in, dg = _mm_nt_rms_bwd(tag + "_dy", [(da, wup_l, *_whole(wup_l))], h_in, g_row, dh, tm=512, nc=512,
                                   shards=True)
        return dh_in, dg, put_grads(tag, [g_up, g_dn.reshape(N_DEV, -1, D)])

    dh3, dg_mlp1, token = mlp_bwd("mlp1", dh, h3, y3, a1, f1, wup1, wdn1, g_mlp[1], None)

    d_o = _mm_nt("att_dout", dh3, wo, tm=512, nc=512, after=token)
    g_wo = _mm_tn("att_dwo", o_att, dh3, t1=512, tn=1024)
    dq, dk, dv, ds_sums = _att_bwd("att_bwd", qkv, o_att, lse, d_o, bias_tiles)
    part_w = N_DIL * ATT_WIDTH
    g_qkv = [_mm_tn("att_dwqkv%d" % p, y2, t, t1=1024, tn=part_w) for p, t in enumerate((dq, dk, dv))]
    dh2, dg_mix1 = _mm_nt_rms_bwd("att_dy", [(t, wqkv, (D, part_w), (0, p)) for p, t in enumerate((dq, dk, dv))],
                                  h2, g_mix[1], dh3, tm=512, nc=512)
    token = put_grads("att", [_column_slots(jnp.concatenate(g_qkv, axis=1)), _column_slots(g_wo)])

    dh1, dg_mlp0, token = mlp_bwd("mlp0", dh2, h1, y1, a0, f0, wup0, wdn0, g_mlp[0], token)

    dz = _mm_nt("gate_dz", dh1, wout, tm=512, nc=512, after=token)
    g_wout = _mm_tn("gate_dwout", z, dh1, t1=1024, tn=1024)
    duvp, d_wm, d_mixed, d_lng, d_lnb = _gate_bwd("gate_dmid", uvp, dz, ln_g, ln_b, wm, bs_full)
    g_win = _mm_tn("gate_dwin", y0, duvp, t1=1024, tn=1024, slot_cols=win.shape[2])
    token = put_grads("gate", [g_win, g_wout.reshape(N_DEV, -1, D)])
    grad_x, dg_mix0 = _mm_nt_rms_bwd("gate_dy", [(duvp, win, *_whole(win))], xs, g_mix[0], dh1, tm=512, nc=512,
                                     after=token, shards=True)

    small_g = dict(
        mix_norm_g=jnp.concatenate([dg_mix0, dg_mix1], axis=0),
        mlp_norm_g=jnp.concatenate([dg_mlp0, dg_mlp1], axis=0),
        final_norm_g=dg_fin[0], a_ln_g=d_lng, a_ln_b=d_lnb, a_w_s=d_wm[None],
        a_b_s=jnp.sum(d_mixed.reshape(CHUNK, GROUPS, D // GROUPS), axis=2).T[None],
        rel_bias=_bias_grad("att_dbias", ds_sums))
    return loss_local, grad_x, small_g
```

```python
import functools
import math

import jax
import jax.numpy as jnp
from jax import lax
from jax.experimental import pallas as pl
from jax.experimental.pallas import tpu as pltpu

F32 = jnp.float32
BF16 = jnp.bfloat16
MESH = pl.DeviceIdType.MESH

N_DEV = 8
EPS = 1e-6
NEG_INF = -1e30
CHUNK = 128
GROUPS = 8
HEAD_DIM = 64
ATT_HEADS = 8
ATT_WIDTH = ATT_HEADS * HEAD_DIM
DILATIONS = (1, 4, 16)
N_DIL = len(DILATIONS)
N_BUCKETS = 32
MAX_EXACT = N_BUCKETS // 2
REL_MAX_DISTANCE = 2048
ATT_ROWS = 2048
ATT_SCALE = HEAD_DIM ** -0.5
DW_TOKENS = 4096
LANES = 128

ADAM_LR = 0.001
ADAM_B1 = 0.9
ADAM_B2 = 0.999
ADAM_EPS = 1e-08
ADAM_WD = 0.01
ADAM_STEP = 10

VMEM_LIMIT_BYTES = 56 * 1024 * 1024


def _params(semantics=None):
    return pltpu.CompilerParams(dimension_semantics=semantics, vmem_limit_bytes=VMEM_LIMIT_BYTES)


def _bf(v):
    return v.astype(BF16)


def _dot(a, b, dims):
    return lax.dot_general(a, b, (dims, ((), ())), preferred_element_type=F32)


NN = ((1,), (0,))
NT = ((1,), (1,))
TN = ((0,), (0,))


def _after_operand(after):
    if after is None:
        return [], []
    return [after], [pl.BlockSpec(memory_space=pl.ANY)]


def _rms_fwd(name, x, g, tm=512, after=None):
    S, D = x.shape
    after_args, after_specs = _after_operand(after)

    def body(x_ref, g_ref, *rest):
        y_ref = rest[-1]
        xv = x_ref[...]
        r = lax.rsqrt(jnp.mean(xv * xv, axis=-1, keepdims=True) + EPS)
        y_ref[...] = _bf(xv * r * g_ref[...])

    return pl.pallas_call(
        body, name=name, grid=(S // tm,),
        in_specs=[pl.BlockSpec((tm, D), lambda i: (i, 0)), pl.BlockSpec((1, D), lambda i: (0, 0))] + after_specs,
        out_specs=pl.BlockSpec((tm, D), lambda i: (i, 0)),
        out_shape=jax.ShapeDtypeStruct((S, D), BF16),
        compiler_params=_params(("parallel",)),
    )(x, g, *after_args)


def _mm_res_loss(name, a, w, res, g, target, *, tm, nc):
    M, D = res.shape

    def body(a_ref, w_ref, r_ref, g_ref, t_ref, dh_ref, dg_ref, l_ref, dhb_ref, h_sc):
        i = pl.program_id(0)
        a_v = _bf(a_ref[...])
        for j in range(D // nc):
            cols, acc = _chunk_product([a_v], [w_ref], j, nc, False, False)
            h_sc[:, cols] = r_ref[:, cols] + acc
        xv = h_sc[...]
        r = lax.rsqrt(jnp.mean(xv * xv, axis=-1, keepdims=True) + EPS)
        xh = xv * r
        gv = g_ref[...]
        e = xh * gv - t_ref[...]
        dout = e / D
        dyg = dout * gv
        c = jnp.mean(dyg * xh, axis=-1, keepdims=True)
        dh = r * (dyg - xh * c)
        dh_ref[...] = dh
        dhb_ref[...] = _bf(dh)
        dg_part = jnp.sum(dout * xh, axis=0, keepdims=True)
        l_part = jnp.sum(e * e, axis=0, keepdims=True)

        @pl.when(i == 0)
        def _():
            dg_ref[...] = dg_part
            l_ref[...] = l_part

        @pl.when(i > 0)
        def _():
            dg_ref[...] += dg_part
            l_ref[...] += l_part

    row = pl.BlockSpec((tm, D), lambda i: (i, 0))
    vec = pl.BlockSpec((1, D), lambda i: (0, 0))
    return pl.pallas_call(
        body, name=name, grid=(M // tm,),
        in_specs=[pl.BlockSpec((tm, a.shape[1]), lambda i: (i, 0)), pl.BlockSpec(w.shape, lambda i: (0, 0)),
                  row, vec, row],
        out_specs=[row, vec, vec, row],
        out_shape=[jax.ShapeDtypeStruct((M, D), F32), jax.ShapeDtypeStruct((1, D), F32),
                   jax.ShapeDtypeStruct((1, D), F32), jax.ShapeDtypeStruct((M, D), BF16)],
        scratch_shapes=[pltpu.VMEM((tm, D), F32)],
        compiler_params=_params(("arbitrary",)),
    )(a, w, res, g, target)


def _chunk_product(a_vals, w_refs, j, nc, nt, shards):
    cols = slice(j * nc, (j + 1) * nc)
    acc = None
    for a_v, w_ref in zip(a_vals, w_refs):
        if not shards:
            terms = [_dot(a_v, w_ref[cols, :], NT) if nt else _dot(a_v, w_ref[:, cols], NN)]
        elif nt:
            nl = w_ref.shape[2]
            terms = [_dot(a_v[:, k * nl:(k + 1) * nl], w_ref[k, cols, :], NT) for k in range(N_DEV)]
        else:
            terms = [_dot(a_v, w_ref[j], NN)]
        for t in terms:
            acc = t if acc is None else acc + t
    return cols, acc


def _mm_rows(name, pairs, n_out, *, nt, tm, nc, epi="plain", extra=None, out_dtype=F32, after=None, shards=False,
             norm_g=None):
    M = pairs[0][0].shape[0]
    np_ = len(pairs)
    after_args, after_specs = _after_operand(after)

    def body(*refs):
        a_refs = refs[:np_]
        w_refs = refs[np_:2 * np_]
        pos = 2 * np_
        e_ref = None
        if extra is not None:
            e_ref = refs[pos]
            pos += 1
        if norm_g is not None:
            g_ref = refs[pos]
            pos += 1
        pos += len(after_args)
        outs = refs[pos:]
        a_vals = [_bf(a[...]) for a in a_refs]
        for j in range(n_out // nc):
            cols, acc = _chunk_product(a_vals, w_refs, j, nc, nt, shards)
            if epi == "plain":
                outs[0][:, cols] = acc.astype(out_dtype)
            elif epi == "res":
                outs[0][:, cols] = e_ref[:, cols] + acc
            elif epi == "relu2":
                outs[0][:, cols] = _bf(acc)
                rl = jnp.maximum(acc, 0.0)
                outs[1][:, cols] = _bf(rl * rl)
            elif epi == "mask2relu":
                outs[0][:, cols] = _bf(acc * (2.0 * jnp.maximum(e_ref[:, cols].astype(F32), 0.0)))
        if norm_g is not None:
            hv = outs[0][...]
            r = lax.rsqrt(jnp.mean(hv * hv, axis=-1, keepdims=True) + EPS)
            outs[1][...] = _bf(hv * r * g_ref[...])

    in_specs = [pl.BlockSpec((tm, a.shape[1]), lambda i: (i, 0)) for a, _, _, _ in pairs]
    for _, _, wshape, widx in pairs:
        in_specs.append(pl.BlockSpec(wshape, functools.partial(lambda i, widx: widx, widx=widx)))
    args = [a for a, _, _, _ in pairs] + [w for _, w, _, _ in pairs]
    if extra is not None:
        in_specs.append(pl.BlockSpec((tm, n_out), lambda i: (i, 0)))
        args.append(extra)
    if norm_g is not None:
        in_specs.append(pl.BlockSpec((1, n_out), lambda i: (0, 0)))
        args.append(norm_g)
    in_specs += after_specs
    args += after_args
    row_out = pl.BlockSpec((tm, n_out), lambda i: (i, 0))
    if epi == "relu2":
        out_specs = [row_out, row_out]
        out_shape = [jax.ShapeDtypeStruct((M, n_out), BF16), jax.ShapeDtypeStruct((M, n_out), BF16)]
    elif norm_g is not None:
        out_specs = [row_out, row_out]
        out_shape = [jax.ShapeDtypeStruct((M, n_out), F32), jax.ShapeDtypeStruct((M, n_out), BF16)]
    else:
        dt = BF16 if epi == "mask2relu" else (F32 if epi == "res" else out_dtype)
        out_specs = row_out
        out_shape = jax.ShapeDtypeStruct((M, n_out), dt)
    return pl.pallas_call(
        body, name=name, grid=(M // tm,), in_specs=in_specs, out_specs=out_specs, out_shape=out_shape,
        compiler_params=_params(("parallel",)),
    )(*args)


def _whole(w):
    return w.shape, (0,) * w.ndim


def _mm_nn(name, a, w, **kw):
    n_out = w.shape[0] * w.shape[2] if w.ndim == 3 else w.shape[1]
    return _mm_rows(name, [(a, w, *_whole(w))], n_out, nt=False, **kw)


def _mm_nt(name, a, w, **kw):
    return _mm_rows(name, [(a, w, *_whole(w))], w.shape[0], nt=True, **kw)


def _mm_nt_rms_bwd(name, pairs, x, g, dres, *, tm, nc, after=None, shards=False, emit_bf16=True):
    M, D = x.shape
    np_ = len(pairs)
    n_out = 3 if emit_bf16 else 2
    after_args, after_specs = _after_operand(after)

    def body(*refs):
        a_refs = refs[:np_]
        w_refs = refs[np_:2 * np_]
        x_ref, g_ref, r_ref = refs[2 * np_:2 * np_ + 3]
        dy_sc = refs[-1]
        outs = refs[-1 - n_out:-1]
        dx_ref, dg_ref = outs[0], outs[1]
        i = pl.program_id(0)
        a_vals = [_bf(a[...]) for a in a_refs]
        for j in range(D // nc):
            cols, acc = _chunk_product(a_vals, w_refs, j, nc, True, shards)
            dy_sc[:, cols] = acc
        xv = x_ref[...]
        r = lax.rsqrt(jnp.mean(xv * xv, axis=-1, keepdims=True) + EPS)
        xh = xv * r
        dy_v = dy_sc[...]
        dyg = dy_v * g_ref[...]
        c = jnp.mean(dyg * xh, axis=-1, keepdims=True)
        dx = r_ref[...] + r * (dyg - xh * c)
        dx_ref[...] = dx
        if emit_bf16:
            outs[2][...] = _bf(dx)
        part = jnp.sum(dy_v * xh, axis=0, keepdims=True)

        @pl.when(i == 0)
        def _():
            dg_ref[...] = part

        @pl.when(i > 0)
        def _():
            dg_ref[...] += part

    row = pl.BlockSpec((tm, D), lambda i: (i, 0))
    vec = pl.BlockSpec((1, D), lambda i: (0, 0))
    in_specs = [pl.BlockSpec((tm, a.shape[1]), lambda i: (i, 0)) for a, _, _, _ in pairs]
    for _, _, wshape, widx in pairs:
        in_specs.append(pl.BlockSpec(wshape, functools.partial(lambda i, widx: widx, widx=widx)))
    args = [a for a, _, _, _ in pairs] + [w for _, w, _, _ in pairs]
    return pl.pallas_call(
        body, name=name, grid=(M // tm,),
        in_specs=in_specs + [row, vec, row] + after_specs,
        out_specs=[row, vec] + [row] * (n_out - 2),
        out_shape=[jax.ShapeDtypeStruct((M, D), F32), jax.ShapeDtypeStruct((1, D), F32)]
        + [jax.ShapeDtypeStruct((M, D), BF16)] * (n_out - 2),
        scratch_shapes=[pltpu.VMEM((tm, D), F32)],
        compiler_params=_params(("arbitrary",)),
    )(*args, x, g, dres, *after_args)


def _mm_tn(name, a, b, *, t1, tn, tm=2048, slot_cols=None):
    M, K1 = a.shape
    N = b.shape[1]
    nm = M // tm

    def body(a_ref, b_ref, o_ref, acc_ref):
        m = pl.program_id(2)
        t = _dot(_bf(a_ref[...]), _bf(b_ref[...]), TN)

        @pl.when(m == 0)
        def _():
            acc_ref[...] = t

        @pl.when(m > 0)
        def _():
            acc_ref[...] += t

        @pl.when(m == nm - 1)
        def _():
            if slot_cols is None:
                o_ref[...] = _bf(acc_ref[...])
            else:
                for k in range(tn // slot_cols):
                    o_ref[k] = _bf(acc_ref[:, k * slot_cols:(k + 1) * slot_cols])

    if slot_cols is not None:
        out_spec = pl.BlockSpec((tn // slot_cols, t1, slot_cols), lambda i, j, m: (j, i, 0))
        out_shape = jax.ShapeDtypeStruct((N // slot_cols, K1, slot_cols), BF16)
    else:
        out_spec = pl.BlockSpec((t1, tn), lambda i, j, m: (i, j))
        out_shape = jax.ShapeDtypeStruct((K1, N), BF16)
    return pl.pallas_call(
        body, name=name, grid=(K1 // t1, N // tn, nm),
        in_specs=[pl.BlockSpec((tm, t1), lambda i, j, m: (m, i)), pl.BlockSpec((tm, tn), lambda i, j, m: (m, j))],
        out_specs=out_spec, out_shape=out_shape,
        scratch_shapes=[pltpu.VMEM((t1, tn), F32)],
        compiler_params=_params(("parallel", "parallel", "arbitrary")),
    )(a, b)


_INV_SQRT2 = 1.0 / math.sqrt(2.0)
_INV_SQRT2PI = 1.0 / math.sqrt(2.0 * math.pi)


def _gelu(x):
    return 0.5 * x * (1.0 + lax.erf(x * _INV_SQRT2))


def _gelu_and_grad(x):
    cdf = 0.5 * (1.0 + lax.erf(x * _INV_SQRT2))
    return x * cdf, cdf + x * (_INV_SQRT2PI * jnp.exp(-0.5 * x * x))


def _layer_norm_parts(v):
    mu = jnp.mean(v, axis=-1, keepdims=True)
    xc = v - mu
    rs = lax.rsqrt(jnp.mean(xc * xc, axis=-1, keepdims=True) + EPS)
    return xc * rs, rs


def _gate_fwd(name, uvp, ln_g, ln_b, wm, bs_full, tr=512):
    S, W2 = uvp.shape
    W = W2 // 2
    gd = W // GROUPS

    def body(u_ref, v_ref, lg_ref, lb_ref, wm_ref, bs_ref, z_ref):
        vh, _ = _layer_norm_parts(_gelu(v_ref[...]))
        vn = _bf(vh * lg_ref[...] + lb_ref[...])
        for ci in range(tr // CHUNK):
            rows = slice(ci * CHUNK, (ci + 1) * CHUNK)
            for g in range(GROUPS):
                cols = slice(g * gd, (g + 1) * gd)
                mixed = _dot(wm_ref[g], vn[rows, cols], NN) + bs_ref[:, cols]
                z_ref[rows, cols] = _bf(_gelu(u_ref[rows, cols]) * mixed)

    vec = pl.BlockSpec((1, W), lambda i: (0, 0))
    return pl.pallas_call(
        body, name=name, grid=(S // tr,),
        in_specs=[pl.BlockSpec((tr, W), lambda i: (i, 0)), pl.BlockSpec((tr, W), lambda i: (i, 1)), vec, vec,
                  pl.BlockSpec((GROUPS, CHUNK, CHUNK), lambda i: (0, 0, 0)),
                  pl.BlockSpec((CHUNK, W), lambda i: (0, 0))],
        out_specs=pl.BlockSpec((tr, W), lambda i: (i, 0)),
        out_shape=jax.ShapeDtypeStruct((S, W), BF16),
        compiler_params=_params(("parallel",)),
    )(uvp, uvp, ln_g, ln_b, wm, bs_full)


def _gate_bwd(name, uvp, dz, ln_g, ln_b, wm, bs_full, tr=256):
    S, W2 = uvp.shape
    W = W2 // 2
    gd = W // GROUPS
    n_steps = S // tr

    def body(u_ref, v_ref, dz_ref, lg_ref, lb_ref, wm_ref, bs_ref, duv_ref, dwm_ref, dmx_ref, dlg_ref, dlb_ref,
             dvn_ref):
        i = pl.program_id(0)
        v, dv_dvp = _gelu_and_grad(v_ref[...])
        vh, rs = _layer_norm_parts(v)
        lg = lg_ref[...]
        vn = _bf(vh * lg + lb_ref[...])

        @pl.when(i == 0)
        def _():
            dwm_ref[...] = jnp.zeros_like(dwm_ref)
            dmx_ref[...] = jnp.zeros_like(dmx_ref)
            dlg_ref[...] = jnp.zeros_like(dlg_ref)
            dlb_ref[...] = jnp.zeros_like(dlb_ref)

        for ci in range(tr // CHUNK):
            rows = slice(ci * CHUNK, (ci + 1) * CHUNK)
            for g in range(GROUPS):
                cols = slice(g * gd, (g + 1) * gd)
                u, du_dup = _gelu_and_grad(u_ref[rows, cols])
                dz_v = dz_ref[rows, cols]
                dmixed = dz_v * u
                dmx_ref[:, cols] += dmixed
                dmixed_b = _bf(dmixed)
                mixed = _dot(wm_ref[g], vn[rows, cols], NN) + bs_ref[:, cols]
                duv_ref[rows, cols] = _bf(dz_v * mixed * du_dup)
                dwm_ref[g] += _dot(dmixed_b, vn[rows, cols], NT)
                dvn_ref[rows, cols] = _dot(wm_ref[g], dmixed_b, TN)
        dvn = dvn_ref[...]
        dlg_ref[...] += jnp.sum(dvn * vh, axis=0, keepdims=True)
        dlb_ref[...] += jnp.sum(dvn, axis=0, keepdims=True)
        dvh = dvn * lg
        dv = rs * (dvh - jnp.mean(dvh, axis=-1, keepdims=True) - vh * jnp.mean(dvh * vh, axis=-1, keepdims=True))
        duv_ref[:, W:] = _bf(dv * dv_dvp)

        @pl.when(i == n_steps - 1)
        def _():
            t_idx = lax.broadcasted_iota(jnp.int32, (CHUNK, CHUNK), 0)
            s_idx = lax.broadcasted_iota(jnp.int32, (CHUNK, CHUNK), 1)
            keep = (s_idx <= t_idx).astype(F32)
            for g in range(GROUPS):
                dwm_ref[g] = dwm_ref[g] * keep

    vec = pl.BlockSpec((1, W), lambda i: (0, 0))
    row = pl.BlockSpec((tr, W), lambda i: (i, 0))
    return pl.pallas_call(
        body, name=name, grid=(n_steps,),
        in_specs=[row, pl.BlockSpec((tr, W), lambda i: (i, 1)), row, vec, vec,
                  pl.BlockSpec((GROUPS, CHUNK, CHUNK), lambda i: (0, 0, 0)),
                  pl.BlockSpec((CHUNK, W), lambda i: (0, 0))],
        out_specs=[pl.BlockSpec((tr, W2), lambda i: (i, 0)),
                   pl.BlockSpec((GROUPS, CHUNK, CHUNK), lambda i: (0, 0, 0)),
                   pl.BlockSpec((CHUNK, W), lambda i: (0, 0)), vec, vec],
        out_shape=[jax.ShapeDtypeStruct((S, W2), BF16), jax.ShapeDtypeStruct((GROUPS, CHUNK, CHUNK), F32),
                   jax.ShapeDtypeStruct((CHUNK, W), F32), jax.ShapeDtypeStruct((1, W), F32),
                   jax.ShapeDtypeStruct((1, W), F32)],
        scratch_shapes=[pltpu.VMEM((tr, W), F32)],
        compiler_params=_params(("arbitrary",)),
    )(uvp, uvp, dz, ln_g, ln_b, wm, bs_full)


def _t5_bucket(distance):
    small = distance < MAX_EXACT
    nf = jnp.maximum(distance, 1).astype(F32)
    large = MAX_EXACT + (jnp.log(nf / MAX_EXACT) / math.log(REL_MAX_DISTANCE / MAX_EXACT)
                         * (N_BUCKETS - MAX_EXACT)).astype(jnp.int32)
    large = jnp.minimum(large, N_BUCKETS - 1)
    return jnp.where(small, distance, large)


TILE_ELEMS = 2 * CHUNK * CHUNK


def _band_buckets():
    rel = CHUNK + jnp.arange(CHUNK)[None, :] - jnp.arange(2 * CHUNK)[:, None]
    band = (rel >= 0) & (rel <= CHUNK)
    buckets = [_t5_bucket(jnp.clip(rel, 0, CHUNK) * d) for d in DILATIONS]
    return jnp.stack(buckets), band


def _bucket_onehot():
    buckets, _ = _band_buckets()
    return (buckets.reshape(N_DIL, 1, TILE_ELEMS) == jnp.arange(N_BUCKETS)[None, :, None]).astype(F32)


def _bias_tiles(name, rel_bias, after=None):
    _, band = _band_buckets()
    own = band & (jnp.arange(2 * CHUNK) >= CHUNK)[:, None]
    masks = jnp.stack([own, band]).reshape(2, TILE_ELEMS).astype(F32)
    tables = jnp.transpose(rel_bias.reshape(N_BUCKETS, N_DIL, ATT_HEADS), (1, 2, 0))
    after_args, after_specs = _after_operand(after)

    def body(t_ref, oh_ref, m_ref, *rest):
        out_ref = rest[-1]
        for g in range(N_DIL):
            bias = lax.dot_general(t_ref[g], oh_ref[g], (NN, ((), ())), precision=lax.Precision.HIGHEST,
                                   preferred_element_type=F32)
            for f in range(2):
                out_ref[g, f] = jnp.where(m_ref[f:f + 1, :] > 0.5, bias, NEG_INF)

    whole = pl.BlockSpec(memory_space=pltpu.VMEM)
    out = pl.pallas_call(
        body, name=name, out_shape=jax.ShapeDtypeStruct((N_DIL, 2, ATT_HEADS, TILE_ELEMS), F32),
        in_specs=[whole, whole, whole] + after_specs, out_specs=whole,
        compiler_params=_params(),
    )(tables, _bucket_onehot(), masks, *after_args)
    out = out.reshape(N_DIL, 2, ATT_HEADS // 2, 2, 2 * CHUNK, CHUNK)
    return jnp.transpose(out, (0, 1, 2, 4, 3, 5)).reshape(N_DIL, 2, ATT_HEADS // 2, 2 * CHUNK, 2 * CHUNK)


def _att_specs(order):
    def column(part, ids):
        hp, g, _ = order(*ids)
        return part * 3 * 4 + g * 4 + hp

    def window(part):
        def index(*ids):
            c = order(*ids)[2]
            return pl.multiple_of(jnp.maximum(c - 1, 0) * ATT_ROWS, ATT_ROWS), column(part, ids) * LANES
        return pl.BlockSpec((pl.Element(2 * ATT_ROWS), pl.Element(LANES)), index)

    return [pl.BlockSpec((ATT_ROWS, LANES), lambda *ids: (order(*ids)[2], column(0, ids))), window(1), window(2)]


def _window_base(c):
    return jnp.where(c == 0, 0, ATT_ROWS)


def _rows(start, d):
    if d == 1:
        return pl.ds(pl.multiple_of(start, CHUNK), CHUNK)
    return pl.ds(start, CHUNK, stride=d)


def _att_tile_offsets(t, d):
    n = t // d
    r = t % d
    return n * (CHUNK * d) + r, n


def _head_pair_columns(x_t):
    zeros = jnp.zeros((HEAD_DIM, CHUNK), x_t.dtype)
    return jnp.concatenate([jnp.concatenate([x_t[:HEAD_DIM], zeros], axis=0),
                            jnp.concatenate([zeros, x_t[HEAD_DIM:]], axis=0)], axis=1)


def _head_pair_rows(y):
    return jnp.concatenate([y[:HEAD_DIM, :CHUNK], y[HEAD_DIM:, CHUNK:]], axis=0)


def _att_fwd(name, qkv, bias_tiles):
    S = qkv.shape[0]
    n_chunks = S // ATT_ROWS
    tiles = ATT_ROWS // CHUNK

    def body(q_ref, kk, vv, b_ref, o_ref, l_ref):
        c = pl.program_id(1)
        g = pl.program_id(2)
        base = _window_base(c)

        for gi, d in enumerate(DILATIONS):
            @pl.when(g == gi)
            def _(d=d):
                span = CHUNK * d

                def tile(t, carry):
                    q0, n = _att_tile_offsets(t, d)
                    first = (c == 0) & (n == 0)
                    rows = _rows(q0, d)
                    cur = _rows(base + q0, d)
                    prev = _rows(jnp.where(first, q0, base + q0 - span), d)
                    inner = jnp.where(first, 0, 1)
                    qq = _head_pair_columns(_bf(q_ref[rows, :] * ATT_SCALE).T)
                    s_p = _dot(_bf(kk[prev, :]), qq, NN) + b_ref[inner, 0:CHUNK, :]
                    s_c = _dot(_bf(kk[cur, :]), qq, NN) + b_ref[inner, CHUNK:2 * CHUNK, :]
                    m = jnp.maximum(jnp.max(s_p, axis=0, keepdims=True), jnp.max(s_c, axis=0, keepdims=True))
                    p_p = jnp.exp(s_p - m)
                    p_c = jnp.exp(s_c - m)
                    l = jnp.sum(p_p, axis=0, keepdims=True) + jnp.sum(p_c, axis=0, keepdims=True)
                    o2 = (_dot(_bf(vv[prev, :]).T, _bf(p_p), NN)
                          + _dot(_bf(vv[cur, :]).T, _bf(p_c), NN)) * (1.0 / l)
                    lse = m + jnp.log(l)
                    l_t = jnp.concatenate([jnp.broadcast_to(lse[:, :CHUNK], (HEAD_DIM, CHUNK)),
                                           jnp.broadcast_to(lse[:, CHUNK:], (HEAD_DIM, CHUNK))], axis=0)
                    o_ref[rows, :] = _head_pair_rows(o2).T
                    l_ref[rows, :] = l_t.T
                    return carry

                lax.fori_loop(0, tiles, tile, 0, unroll=8)

    order = lambda hp, c, g: (hp, g, c)
    out_spec = pl.BlockSpec((None, ATT_ROWS, LANES), lambda hp, c, g: (g, c, hp))
    shape = jax.ShapeDtypeStruct((N_DIL, S, ATT_WIDTH), F32)
    return pl.pallas_call(
        body, name=name, grid=(ATT_HEADS // 2, n_chunks, N_DIL),
        in_specs=_att_specs(order) + [
            pl.BlockSpec((None, 2, None, 2 * CHUNK, 2 * CHUNK), lambda hp, c, g: (g, 0, hp, 0, 0))],
        out_specs=[out_spec, out_spec],
        out_shape=[shape, shape],
        compiler_params=_params(("parallel", "parallel", "parallel")),
    )(qkv, qkv, qkv, bias_tiles)


def _att_merge(name, o_g, l_g, tm=512):
    _, S, W = o_g.shape

    def body(o_ref, l_ref, out_ref, lse_ref):
        ls = [l_ref[g] for g in range(N_DIL)]
        mx = functools.reduce(jnp.maximum, ls)
        ws = [jnp.exp(l - mx) for l in ls]
        tot = functools.reduce(lambda a, b: a + b, ws)
        acc = ws[0] * o_ref[0]
        for g in range(1, N_DIL):
            acc = acc + ws[g] * o_ref[g]
        out_ref[...] = acc / tot
        lse_ref[...] = mx + jnp.log(tot)

    blk = pl.BlockSpec((N_DIL, tm, W), lambda i: (0, i, 0))
    row = pl.BlockSpec((tm, W), lambda i: (i, 0))
    shape = jax.ShapeDtypeStruct((S, W), F32)
    return pl.pallas_call(
        body, name=name, grid=(S // tm,), in_specs=[blk, blk], out_specs=[row, row], out_shape=[shape, shape],
        compiler_params=_params(("parallel",)),
    )(o_g, l_g)


def _att_bwd(name, qkv, o, lse, d_o, bias_tiles):
    S = qkv.shape[0]
    n_chunks = S // ATT_ROWS
    tiles = ATT_ROWS // CHUNK

    def body(q_ref, kk, vv, o_ref, l_ref, do_ref, b_ref, dq_out, dk_out, dv_out, ds_ref, dq_ref, dk_ref, dv_ref):
        g = pl.program_id(1)
        c = pl.program_id(2)

        @pl.when(c == 0)
        def _():
            dk_ref[...] = jnp.zeros_like(dk_ref)
            dv_ref[...] = jnp.zeros_like(dv_ref)
            ds_ref[...] = jnp.zeros_like(ds_ref)

        base = _window_base(c)
        first_row = c * ATT_ROWS
        head0 = lax.broadcasted_iota(jnp.int32, (CHUNK, LANES), 1) < HEAD_DIM

        def head_pair_stack(x):
            zero = jnp.zeros_like(x)
            return jnp.concatenate([jnp.where(head0, x, zero), jnp.where(head0, zero, x)], axis=0)

        for gi, d in enumerate(DILATIONS):
            @pl.when(g == gi)
            def _(d=d):
                span = CHUNK * d

                def tile(t, carry):
                    q0, n = _att_tile_offsets(t, d)
                    first = (c == 0) & (n == 0)
                    rows = _rows(q0, d)
                    cur = _rows(base + q0, d)
                    prev = _rows(jnp.where(first, q0, base + q0 - span), d)
                    inner = jnp.where(first, 0, 1)
                    g_cur = _rows(first_row + q0, d)
                    g_prev = _rows(jnp.where(first, q0, first_row + q0 - span), d)
                    q2 = _bf(q_ref[rows, :] * ATT_SCALE)
                    q_t = q2.T
                    k2 = _bf(jnp.concatenate([kk[prev, :], kk[cur, :]], axis=0))
                    k_t = k2.T
                    v2 = _bf(jnp.concatenate([vv[prev, :], vv[cur, :]], axis=0))
                    do2 = do_ref[rows, :]
                    do_b = _bf(do2)
                    do_t = do_b.T
                    lse_t = l_ref[rows, :].T
                    dd_t = (do2 * o_ref[rows, :]).T
                    lse = jnp.concatenate([lse_t[0:1], lse_t[HEAD_DIM:HEAD_DIM + 1]], axis=1)
                    delta = jnp.concatenate([jnp.sum(dd_t[:HEAD_DIM], axis=0, keepdims=True),
                                             jnp.sum(dd_t[HEAD_DIM:], axis=0, keepdims=True)], axis=1)
                    s = _dot(k2, _head_pair_columns(q_t), NN) + b_ref[inner]
                    p = jnp.exp(s - lse)
                    ds = p * (_dot(v2, _head_pair_columns(do_t), NN) - delta)
                    ds_ref[...] += ds
                    ds_b = _bf(ds)
                    dq_t = _head_pair_rows(_dot(k_t, ds_b, NN))
                    dk2 = _dot(ds_b, head_pair_stack(q2), NN)
                    dv2 = _dot(_bf(p), head_pair_stack(do_b), NN)
                    dq_ref[rows, :] = (dq_t * ATT_SCALE).T
                    dk_ref[g_prev, :] += dk2[0:CHUNK]
                    dk_ref[g_cur, :] += dk2[CHUNK:2 * CHUNK]
                    dv_ref[g_prev, :] += dv2[0:CHUNK]
                    dv_ref[g_cur, :] += dv2[CHUNK:2 * CHUNK]
                    return carry

                lax.fori_loop(0, tiles, tile, 0, unroll=8)

        dq_out[...] = _bf(dq_ref[...])

        @pl.when(c == n_chunks - 1)
        def _():
            dk_out[...] = _bf(dk_ref[...])
            dv_out[...] = _bf(dv_ref[...])

    order = lambda hp, g, c: (hp, g, c)
    chunk = pl.BlockSpec((ATT_ROWS, LANES), lambda hp, g, c: (c, hp))
    slab = pl.BlockSpec((S, LANES), lambda hp, g, c: (0, g * 4 + hp))
    width = N_DIL * ATT_WIDTH
    dq, dk, dv, ds_sums = pl.pallas_call(
        body, name=name, grid=(ATT_HEADS // 2, N_DIL, n_chunks),
        in_specs=_att_specs(order) + [chunk, chunk, chunk,
                                      pl.BlockSpec((None, 2, None, 2 * CHUNK, 2 * CHUNK),
                                                   lambda hp, g, c: (g, 0, hp, 0, 0))],
        out_specs=[pl.BlockSpec((ATT_ROWS, LANES), lambda hp, g, c: (c, g * 4 + hp)), slab, slab,
                   pl.BlockSpec((None, None, 2 * CHUNK, 2 * CHUNK), lambda hp, g, c: (g, hp, 0, 0))],
        out_shape=[jax.ShapeDtypeStruct((S, width), BF16), jax.ShapeDtypeStruct((S, width), BF16),
                   jax.ShapeDtypeStruct((S, width), BF16),
                   jax.ShapeDtypeStruct((N_DIL, ATT_HEADS // 2, 2 * CHUNK, 2 * CHUNK), F32)],
        scratch_shapes=[pltpu.VMEM((ATT_ROWS, LANES), F32), pltpu.VMEM((S, LANES), F32),
                        pltpu.VMEM((S, LANES), F32)],
        compiler_params=_params(("parallel", "parallel", "arbitrary")),
    )(qkv, qkv, qkv, o, lse, d_o, bias_tiles)
    ds_sums = ds_sums.reshape(N_DIL, ATT_HEADS // 2, 2 * CHUNK, 2, CHUNK)
    ds_sums = jnp.transpose(ds_sums, (0, 1, 3, 2, 4)).reshape(N_DIL, ATT_HEADS, 2 * CHUNK, CHUNK)
    return dq, dk, dv, ds_sums


def _bias_grad(name, ds_sums):
    flat = ds_sums.reshape(N_DIL, ATT_HEADS, TILE_ELEMS)

    def body(oh_ref, ds_ref, out_ref):
        for g in range(N_DIL):
            out_ref[g] = lax.dot_general(oh_ref[g], ds_ref[g], (NT, ((), ())), precision=lax.Precision.HIGHEST,
                                         preferred_element_type=F32)

    out = pl.pallas_call(
        body, name=name, out_shape=jax.ShapeDtypeStruct((N_DIL, N_BUCKETS, ATT_HEADS), F32),
        compiler_params=_params(),
    )(_bucket_onehot(), flat)
    return jnp.transpose(out, (1, 0, 2)).reshape(N_BUCKETS, N_DIL * ATT_HEADS)


def _peers():
    x, y, c = lax.axis_index("x"), lax.axis_index("y"), lax.axis_index("c")
    me = 4 * x + 2 * y + c
    others = [(x, y, 1 - c), (1 - x, y, c), (x, 1 - y, c), (1 - x, 1 - y, c),
              (1 - x, y, 1 - c), (x, 1 - y, 1 - c), (1 - x, 1 - y, 1 - c)]
    return me, others


def _slot(dev):
    return 4 * dev[0] + 2 * dev[1] + dev[2]


_HBM =pl.BlockSpec(memory_space=pltpu.HBM)
_SEM = pl.BlockSpec(memory_space=pltpu.SEMAPHORE)
_EFFECT = pltpu.SideEffectType.DATAFLOW_SIDE_EFFECTING


def _my_slot():
    return 4 * lax.axis_index("x") + 2 * lax.axis_index("y") + lax.axis_index("c")


def _exchange_copy(src_ref, land_ref, send_sems, recv_sems, k, dev, me, scatter, arriving):
    src = src_ref.at[me if arriving else _slot(dev)] if scatter else src_ref
    dst = land_ref.at[_slot(dev) if arriving else me]
    return pltpu.make_async_remote_copy(src_ref=src, dst_ref=dst, send_sem=send_sems.at[k], recv_sem=recv_sems.at[k],
                                        device_id=dev, device_id_type=MESH)


def _exchange_start(name, srcs, scatter):
    n = len(srcs)
    me = _my_slot()
    landings = []
    for src in srcs:
        own = lax.dynamic_index_in_dim(src, me, 0, keepdims=True) if scatter else src[None]
        landings.append(lax.dynamic_update_slice(lax.empty((N_DEV,) + src.shape[-2:], src.dtype), own, (me, 0, 0)))

    def body(*refs):
        src_refs, land_refs = refs[:n], refs[n:2 * n]
        send_sems, recv_sems = refs[2 * n:2 * n + 2]
        token = refs[-1]
        me, others = _peers()
        for p in range(n):
            for k, dev in enumerate(others):
                _exchange_copy(src_refs[p], land_refs[p], send_sems, recv_sems, p * (N_DEV - 1) + k, dev, me,
                               scatter, False).start()
        token[...] = jnp.zeros_like(token)

    sems = pltpu.SemaphoreType.DMA((n * (N_DEV - 1),))
    hbm = lambda a: pltpu.with_memory_space_constraint(a, pltpu.HBM)
    outs = pl.pallas_call(
        body, name=name,
        out_shape=(sems, sems, *[pltpu.HBM(a.shape, a.dtype) for a in srcs + landings],
                   jax.ShapeDtypeStruct((8, LANES), F32)),
        in_specs=(_HBM,) * (2 * n), out_specs=(_SEM, _SEM) + (_HBM,) * (2 * n) + (pl.BlockSpec(memory_space=pltpu.VMEM),),
        input_output_aliases={i: 2 + i for i in range(2 * n)},
        compiler_params=pltpu.CompilerParams(has_side_effects=_EFFECT),
    )(*[hbm(a) for a in srcs + landings])
    return (outs[0], outs[1], list(outs[2:2 + n]), list(outs[2 + n:2 + 2 * n]), scatter), outs[-1]


def _exchange_wait(name, handle, after):
    send_sems, recv_sems, src_thru, land_thru, scatter = handle
    n = len(src_thru)

    def body(*refs):
        src_refs, land_refs = refs[:n], refs[n:2 * n]
        send_sems, recv_sems = refs[2 * n:2 * n + 2]
        me, others = _peers()
        for p in range(n):
            for k, dev in enumerate(others):
                cp = _exchange_copy(src_refs[p], land_refs[p], send_sems, recv_sems, p * (N_DEV - 1) + k, dev, me,
                                    scatter, True)
                cp.wait_send()
                cp.wait_recv()

    outs = pl.pallas_call(
        body, name=name,
        out_shape=tuple(pltpu.HBM(a.shape, a.dtype) for a in src_thru + land_thru),
        in_specs=(_HBM,) * (2 * n) + (_SEM, _SEM, pl.BlockSpec(memory_space=pl.ANY)), out_specs=(_HBM,) * (2 * n),
        input_output_aliases={i: i for i in range(2 * n)},
        compiler_params=pltpu.CompilerParams(has_side_effects=_EFFECT),
    )(*src_thru, *land_thru, send_sems, recv_sems, after)
    return list(outs[n:])


def _adamw_math(w, g, m, v):
    m = ADAM_B1 * m + (1.0 - ADAM_B1) * g
    v = ADAM_B2 * v + (1.0 - ADAM_B2) * (g * g)
    m_hat = m / (1.0 - ADAM_B1 ** ADAM_STEP)
    v_hat = v / (1.0 - ADAM_B2 ** ADAM_STEP)
    delta = -ADAM_LR * (m_hat / (jnp.sqrt(v_hat) + ADAM_EPS) + ADAM_WD * w)
    return delta, m, v


_SMALL_WIDE = (("mix_norm_g", 2), ("mlp_norm_g", 2), ("final_norm_g", 1), ("a_ln_g", 1), ("a_ln_b", 1))
_SMALL_NARROW = (("a_w_s", GROUPS * CHUNK), ("a_b_s", GROUPS), ("rel_bias", N_BUCKETS))
_SMALL = tuple(n for n, _ in _SMALL_WIDE + _SMALL_NARROW)
_BIAS_COLS = N_DIL * ATT_HEADS


def _pack_small_grads(grads, loss_term):
    D = grads["a_ln_g"].shape[-1]
    tiles = [jnp.pad(grads[n].reshape(k, D), ((0, 8 - k), (0, 0))) for n, k in _SMALL_WIDE]
    tiles.append(jnp.pad(loss_term.reshape(1, 1), ((0, 7), (0, D - 1))))
    narrow = [grads["a_w_s"].reshape(-1, LANES), grads["a_b_s"].reshape(-1, LANES),
              jnp.pad(grads["rel_bias"], ((0, 0), (0, LANES - _BIAS_COLS)))]
    return jnp.concatenate(tiles, axis=0), jnp.concatenate(narrow, axis=0)


def _adamw_small(name, g_wide, g_narrow, w, m, v):
    D = g_wide.shape[-1]
    shapes = {n: (k, D) for n, k in _SMALL_WIDE}
    shapes.update({n: (k, LANES) for n, k in _SMALL_NARROW})
    shapes["rel_bias"] = (N_BUCKETS, _BIAS_COLS)
    n_t = len(_SMALL)

    def body(gw_ref, gn_ref, *rest):
        params = rest[:3 * n_t]
        outs = rest[3 * n_t:3 * n_t + 4 * n_t]
        loss_ref, sw, sn = rest[-3:]
        sw[...] = functools.reduce(lambda a, b: a + b, [gw_ref[j] for j in range(N_DEV)])
        sn[...] = functools.reduce(lambda a, b: a + b, [gn_ref[j] for j in range(N_DEV)])
        row = 0
        for i, n in enumerate(_SMALL):
            k, cols = shapes[n]
            if i < len(_SMALL_WIDE):
                g = sw[8 * i:8 * i + k, :]
            else:
                g = sn[row:row + k, 0:cols]
                row += k
            w_ref, m_ref, v_ref = params[3 * i:3 * i + 3]
            delta, m_new, v_new = _adamw_math(w_ref[...], g, m_ref[...], v_ref[...])
            for out, val in zip(outs[4 * i:4 * i + 4], (g, delta, m_new, v_new)):
                out[...] = val
        loss_ref[...] = sw[8 * len(_SMALL_WIDE):8 * len(_SMALL_WIDE) + 8, 0:LANES]

    whole = pl.BlockSpec(memory_space=pltpu.VMEM)
    args = [t[n].reshape(shapes[n]) for n in _SMALL for t in (w, m, v)]
    res = pl.pallas_call(
        body, name=name,
        in_specs=[whole] * (2 + len(args)), out_specs=[whole] * (4 * n_t + 1),
        out_shape=[jax.ShapeDtypeStruct(shapes[n], F32) for n in _SMALL for _ in range(4)]
        + [jax.ShapeDtypeStruct((8, LANES), F32)],
        scratch_shapes=[pltpu.VMEM(g_wide.shape[1:], F32), pltpu.VMEM(g_narrow.shape[1:], F32)],
        compiler_params=_params(),
    )(g_wide, g_narrow, *args)
    small = {n: tuple(r.reshape(w[n].shape) for r in res[4 * i:4 * i + 4]) for i, n in enumerate(_SMALL)}
    return small, res[-1][0, 0]


def _adamw_shard(name, parts, w, m, v, layer, earlier=None, after=None, tr=256):
    L, K, N = w.shape
    tr = min(tr, K)
    n_prev = 0 if earlier is None else 4
    after_args, after_specs = _after_operand(after)

    def body(p_ref, w_ref, m_ref, v_ref, *rest):
        g_out, d_out, m_out, v_out = rest[n_prev + len(after_args):]
        g = p_ref[0].astype(F32)
        for j in range(1, N_DEV):
            g = g + p_ref[j].astype(F32)
        delta, m_new, v_new = _adamw_math(w_ref[...], g, m_ref[...], v_ref[...])
        g_out[...] = g
        d_out[...] = delta
        m_out[...] = m_new
        v_out[...] = v_new

    row = pl.BlockSpec((None, tr, N), lambda i: (layer, i, 0))
    shape = jax.ShapeDtypeStruct((L, K, N), F32)
    return pl.pallas_call(
        body, name=name, grid=(K // tr,),
        in_specs=[pl.BlockSpec((N_DEV, tr, N), lambda i: (0, i, 0)), row, row, row]
        + [pl.BlockSpec(memory_space=pl.ANY)] * n_prev + after_specs,
        out_specs=[row, row, row, row],
        out_shape=[shape, shape, shape, shape],
        input_output_aliases={4 + j: j for j in range(n_prev)},
        compiler_params=_params(("parallel",)),
    )(parts, w, m, v, *(earlier or ()), *after_args)


def _column_slots(full):
    K, N = full.shape
    return jnp.transpose(full.reshape(K, N_DEV, N // N_DEV), (1, 0, 2))


def _from_column_slots(slots):
    _, K, n = slots.shape
    return jnp.transpose(slots, (1, 0, 2)).reshape(K, N_DEV * n)


_STAGES = (("gate", ("a_w_in", "a_w_out"), 0),
           ("mlp0", ("w_up", "w_down"), 0),
           ("att", ("b_w_qkv", "b_w_out"), 0),
           ("mlp1", ("w_up", "w_down"), 1))


def kernel(x, mix_norm_g, mlp_norm_g, final_norm_g, a_w_in, a_ln_g, a_ln_b, a_w_s, a_b_s, a_w_out, b_w_qkv, b_w_out, rel_bias, w_up, w_down, loss_target, m_mix_norm_g, m_mlp_norm_g, m_final_norm_g, m_a_w_in, m_a_ln_g, m_a_ln_b, m_a_w_s, m_a_b_s, m_a_w_out, m_b_w_qkv, m_b_w_out, m_rel_bias, m_w_up, m_w_down, v_mix_norm_g, v_mlp_norm_g, v_final_norm_g, v_a_w_in, v_a_ln_g, v_a_ln_b, v_a_w_s, v_a_b_s, v_a_w_out, v_b_w_qkv, v_b_w_out, v_rel_bias, v_w_up, v_w_down):
    w = dict(mix_norm_g=mix_norm_g, mlp_norm_g=mlp_norm_g, final_norm_g=final_norm_g, a_w_in=a_w_in, a_ln_g=a_ln_g,
             a_ln_b=a_ln_b, a_w_s=a_w_s, a_b_s=a_b_s, a_w_out=a_w_out, b_w_qkv=b_w_qkv, b_w_out=b_w_out,
             rel_bias=rel_bias, w_up=w_up, w_down=w_down)
    m = dict(mix_norm_g=m_mix_norm_g, mlp_norm_g=m_mlp_norm_g, final_norm_g=m_final_norm_g, a_w_in=m_a_w_in,
             a_ln_g=m_a_ln_g, a_ln_b=m_a_ln_b, a_w_s=m_a_w_s, a_b_s=m_a_b_s, a_w_out=m_a_w_out, b_w_qkv=m_b_w_qkv,
             b_w_out=m_b_w_out, rel_bias=m_rel_bias, w_up=m_w_up, w_down=m_w_down)
    v = dict(mix_norm_g=v_mix_norm_g, mlp_norm_g=v_mlp_norm_g, final_norm_g=v_final_norm_g, a_w_in=v_a_w_in,
             a_ln_g=v_a_ln_g, a_ln_b=v_a_ln_b, a_w_s=v_a_w_s, a_b_s=v_a_b_s, a_w_out=v_a_w_out, b_w_qkv=v_b_w_qkv,
             b_w_out=v_b_w_out, rel_bias=v_rel_bias, w_up=v_w_up, w_down=v_w_down)

    stages = {s: (names, layer) for s, names, layer in _STAGES}
    order = [s for s, _, _ in _STAGES]

    def shards_of(stage):
        names, layer = stages[stage]
        return [_bf(w[n][layer]) for n in names]

    pending = {}
    pending[order[0]], first_token = _exchange_start("gather_" + order[0] + "_start", shards_of(order[0]), False)

    def get_weights(stage, dep):
        gathered = _exchange_wait("gather_" + stage + "_wait", pending.pop(stage), dep)
        nxt = order.index(stage) + 1
        token = None
        if nxt < len(order):
            shards, gathered = lax.optimization_barrier((shards_of(order[nxt]), gathered))
            pending[order[nxt]], token = _exchange_start("gather_" + order[nxt] + "_start", shards, False)
        return gathered, token

    sent = {}

    def put_grads(stage, slot_grads):
        sent[stage], token = _exchange_start("scatter_" + stage + "_start", slot_grads, True)
        return token

    loss_local, grad_x, small_g = _local_step(
        x[0], loss_target[0], mix_norm_g, mlp_norm_g, final_norm_g, a_ln_g, a_ln_b, a_w_s, a_b_s, rel_bias,
        get_weights, put_grads, first_token)

    small_sent, token = _exchange_start("gather_small_start", list(_pack_small_grads(small_g, loss_local)), False)

    results = {}
    prev = token
    for stage in reversed(order):
        names, layer = stages[stage]
        received = _exchange_wait("scatter_" + stage + "_wait", sent[stage], prev)
        for n, parts in zip(names, received):
            results[n] = _adamw_shard("adamw_%s_%s" % (stage, n), parts, w[n], m[n], v[n], layer, results.get(n),
                                      after=prev)
            prev = results[n][0]

    g_wide, g_narrow = _exchange_wait("gather_small_wait", small_sent, prev)
    small, loss = _adamw_small("adamw_small", g_wide, g_narrow, w, m, v)

    outs = []
    for j in range(4):
        outs.extend(small[n][j] if n in _SMALL else results[n][j] for n in w)
    return (loss, grad_x[None], *outs)


def _local_step(xs, tgt, mix_norm_g, mlp_norm_g, final_norm_g, a_ln_g, a_ln_b, a_w_s, a_b_s, rel_bias,
                get_weights, put_grads, first_token=None):
    D = xs.shape[-1]
    g_mix = [mix_norm_g[l][None, :] for l in range(2)]
    g_mlp = [mlp_norm_g[l][None, :] for l in range(2)]
    g_fin = final_norm_g[None, :]
    ln_g, ln_b = a_ln_g, a_ln_b
    causal = jnp.tril(jnp.ones((CHUNK, CHUNK), dtype=bool))
    wm = _bf(jnp.where(causal[None], a_w_s[0], 0.0))
    bs_full = jnp.repeat(a_b_s[0].T, D // GROUPS, axis=1)
    bias_tiles = _bias_tiles("att_bias", rel_bias, after=first_token)

    (win, wout), token = get_weights("gate", bias_tiles)
    wout = wout.reshape(-1, D)
    y0 = _rms_fwd("rms_mix0", xs, g_mix[0], after=token)
    uvp = _mm_nn("gate_in", y0, win, tm=512, nc=win.shape[2], shards=True)
    z = _gate_fwd("gate_mid", uvp, ln_g, ln_b, wm, bs_full)
    h1, y1 = _mm_nn("gate_out", z, wout, tm=512, nc=512, epi="res", extra=xs, norm_g=g_mlp[0])
    (wup0, wdn0), token = get_weights("mlp0", h1)
    wdn0 = wdn0.reshape(-1, D)
    a0, f0 = _mm_nn("mlp0_up", y1, wup0, tm=512, nc=wup0.shape[2], epi="relu2", shards=True, after=token)
    h2, y2 = _mm_nn("mlp0_down", f0, wdn0, tm=512, nc=512, epi="res", extra=h1, norm_g=g_mix[1])
    (wqkv, wo), token = get_weights("att", h2)
    wqkv, wo = _from_column_slots(wqkv), _from_column_slots(wo)
    qkv = _mm_nn("att_qkv", y2, wqkv, tm=512, nc=512, after=token)
    o_att, lse = _att_merge("att_merge", *_att_fwd("att_fwd", qkv, bias_tiles))
    h3, y3 = _mm_nn("att_out", o_att, wo, tm=512, nc=512, epi="res", extra=h2, norm_g=g_mlp[1])
    (wup1, wdn1), _ = get_weights("mlp1", h3)
    wdn1 = wdn1.reshape(-1, D)
    a1, f1 = _mm_nn("mlp1_up", y3, wup1, tm=512, nc=wup1.shape[2], epi="relu2", shards=True)
    dh, dg_fin, err2, dh_b = _mm_res_loss("mlp1_down_loss", f1, wdn1, h3, g_fin, tgt, tm=512, nc=512)
    loss_local = 0.5 * jnp.sum(err2) / D

    def mlp_bwd(tag, dh, dh_b, h_in, y, a, f, wup_l, wdn_l, g_row, after):
        da = _mm_nt(tag + "_dact", dh_b, wdn_l, tm=512, nc=512, epi="mask2relu", extra=a, after=after)
        g_dn = _mm_tn(tag + "_dwdown", f, dh_b, t1=1024, tn=1024, tm=DW_TOKENS)
        g_up = _mm_tn(tag + "_dwup", y, da, t1=1024, tn=1024, tm=DW_TOKENS, slot_cols=wup_l.shape[2])
        dh_in, dg, dh_in_b = _mm_nt_rms_bwd(tag + "_dy", [(da, wup_l, *_whole(wup_l))], h_in, g_row, dh, tm=512,
                                            nc=512, shards=True)
        return dh_in, dh_in_b, dg, put_grads(tag, [g_up, g_dn.reshape(N_DEV, -1, D)])

    dh3, dh3_b, dg_mlp1, token = mlp_bwd("mlp1", dh, dh_b, h3, y3, a1, f1, wup1, wdn1, g_mlp[1], None)

    d_o = _mm_nt("att_dout", dh3_b, wo, tm=512, nc=512, after=token)
    g_wo = _mm_tn("att_dwo", o_att, dh3_b, t1=512, tn=1024, tm=DW_TOKENS)
    dq, dk, dv, ds_sums = _att_bwd("att_bwd", qkv, o_att, lse, d_o, bias_tiles)
    part_w = N_DIL * ATT_WIDTH
    g_qkv = [_mm_tn("att_dwqkv%d" % p, y2, t, t1=1024, tn=part_w) for p, t in enumerate((dq, dk, dv))]
    dh2, dg_mix1, dh2_b = _mm_nt_rms_bwd(
        "att_dy", [(t, wqkv, (D, part_w), (0, p)) for p, t in enumerate((dq, dk, dv))], h2, g_mix[1], dh3, tm=512,
        nc=512)
    token = put_grads("att", [_column_slots(jnp.concatenate(g_qkv, axis=1)), _column_slots(g_wo)])

    dh1, dh1_b, dg_mlp0, token = mlp_bwd("mlp0", dh2, dh2_b, h1, y1, a0, f0, wup0, wdn0, g_mlp[0], token)

    dz = _mm_nt("gate_dz", dh1_b, wout, tm=512, nc=512, after=token)
    g_wout = _mm_tn("gate_dwout", z, dh1_b, t1=1024, tn=1024, tm=DW_TOKENS)
    duvp, d_wm, d_mixed, d_lng, d_lnb = _gate_bwd("gate_dmid", uvp, dz, ln_g, ln_b, wm, bs_full)
    g_win = _mm_tn("gate_dwin", y0, duvp, t1=1024, tn=1024, tm=DW_TOKENS, slot_cols=win.shape[2])
    token = put_grads("gate", [g_win, g_wout.reshape(N_DEV, -1, D)])
    grad_x, dg_mix0 = _mm_nt_rms_bwd("gate_dy", [(duvp, win, *_whole(win))], xs, g_mix[0], dh1, tm=512, nc=512,
                                     after=token, shards=True, emit_bf16=False)

    small_g = dict(
        mix_norm_g=jnp.concatenate([dg_mix0, dg_mix1], axis=0),
        mlp_norm_g=jnp.concatenate([dg_mlp0, dg_mlp1], axis=0),
        final_norm_g=dg_fin[0], a_ln_g=d_lng, a_ln_b=d_lnb, a_w_s=d_wm[None],
        a_b_s=jnp.sum(d_mixed.reshape(CHUNK, GROUPS, D // GROUPS), axis=2).T[None],
        rel_bias=_bias_grad("att_dbias", ds_sums))
    return loss_local, grad_x, small_g
```

```python
import functools
import math

import jax
import jax.numpy as jnp
from jax import lax
from jax.experimental import pallas as pl
from jax.experimental.pallas import tpu as pltpu

F32 = jnp.float32
BF16 = jnp.bfloat16
MESH = pl.DeviceIdType.MESH

N_DEV = 8
EPS = 1e-6
NEG_INF = -1e30
CHUNK = 128
GROUPS = 8
HEAD_DIM = 64
ATT_HEADS = 8
ATT_WIDTH = ATT_HEADS * HEAD_DIM
DILATIONS = (1, 4, 16)
N_DIL = len(DILATIONS)
N_BUCKETS = 32
MAX_EXACT = N_BUCKETS // 2
REL_MAX_DISTANCE = 2048
ATT_ROWS = 2048
ATT_SCALE = HEAD_DIM ** -0.5
DW_TOKENS = 4096
LANES = 128

ADAM_LR = 0.001
ADAM_B1 = 0.9
ADAM_B2 = 0.999
ADAM_EPS = 1e-08
ADAM_WD = 0.01
ADAM_STEP = 10

VMEM_LIMIT_BYTES = 56 * 1024 * 1024


def _params(semantics=None):
    return pltpu.CompilerParams(dimension_semantics=semantics, vmem_limit_bytes=VMEM_LIMIT_BYTES)


def _bf(v):
    return v.astype(BF16)


def _dot(a, b, dims):
    return lax.dot_general(a, b, (dims, ((), ())), preferred_element_type=F32)


NN = ((1,), (0,))
NT = ((1,), (1,))
TN = ((0,), (0,))


def _after_operand(after):
    if after is None:
        return [], []
    return [after], [pl.BlockSpec(memory_space=pl.ANY)]


def _rms_fwd(name, x, g, tm=512, after=None):
    S, D = x.shape
    after_args, after_specs = _after_operand(after)

    def body(x_ref, g_ref, *rest):
        y_ref = rest[-1]
        xv = x_ref[...]
        r = lax.rsqrt(jnp.mean(xv * xv, axis=-1, keepdims=True) + EPS)
        y_ref[...] = _bf(xv * r * g_ref[...])

    return pl.pallas_call(
        body, name=name, grid=(S // tm,),
        in_specs=[pl.BlockSpec((tm, D), lambda i: (i, 0)), pl.BlockSpec((1, D), lambda i: (0, 0))] + after_specs,
        out_specs=pl.BlockSpec((tm, D), lambda i: (i, 0)),
        out_shape=jax.ShapeDtypeStruct((S, D), BF16),
        compiler_params=_params(("parallel",)),
    )(x, g, *after_args)


def _mm_res_loss(name, a, w, res, g, target, *, tm, nc):
    M, D = res.shape

    def body(a_ref, w_ref, r_ref, g_ref, t_ref, dh_ref, dg_ref, l_ref, dhb_ref, h_sc):
        i = pl.program_id(0)
        a_v = _bf(a_ref[...])
        for j in range(D // nc):
            cols, acc = _chunk_product([a_v], [w_ref], j, nc, False, False)
            h_sc[:, cols] = r_ref[:, cols] + acc
        xv = h_sc[...]
        r = lax.rsqrt(jnp.mean(xv * xv, axis=-1, keepdims=True) + EPS)
        xh = xv * r
        gv = g_ref[...]
        e = xh * gv - t_ref[...]
        dout = e / D
        dyg = dout * gv
        c = jnp.mean(dyg * xh, axis=-1, keepdims=True)
        dh = r * (dyg - xh * c)
        dh_ref[...] = dh
        dhb_ref[...] = _bf(dh)
        dg_part = jnp.sum(dout * xh, axis=0, keepdims=True)
        l_part = jnp.sum(e * e, axis=0, keepdims=True)

        @pl.when(i == 0)
        def _():
            dg_ref[...] = dg_part
            l_ref[...] = l_part

        @pl.when(i > 0)
        def _():
            dg_ref[...] += dg_part
            l_ref[...] += l_part

    row = pl.BlockSpec((tm, D), lambda i: (i, 0))
    vec = pl.BlockSpec((1, D), lambda i: (0, 0))
    return pl.pallas_call(
        body, name=name, grid=(M // tm,),
        in_specs=[pl.BlockSpec((tm, a.shape[1]), lambda i: (i, 0)), pl.BlockSpec(w.shape, lambda i: (0, 0)),
                  row, vec, row],
        out_specs=[row, vec, vec, row],
        out_shape=[jax.ShapeDtypeStruct((M, D), F32), jax.ShapeDtypeStruct((1, D), F32),
                   jax.ShapeDtypeStruct((1, D), F32), jax.ShapeDtypeStruct((M, D), BF16)],
        scratch_shapes=[pltpu.VMEM((tm, D), F32)],
        compiler_params=_params(("arbitrary",)),
    )(a, w, res, g, target)


def _chunk_product(a_vals, w_refs, j, nc, nt, shards):
    cols = slice(j * nc, (j + 1) * nc)
    acc = None
    for a_v, w_ref in zip(a_vals, w_refs):
        if shards:
            t = _dot(a_v, w_ref[j], NN)
        else:
            t = _dot(a_v, w_ref[cols, :], NT) if nt else _dot(a_v, w_ref[:, cols], NN)
        acc = t if acc is None else acc + t
    return cols, acc


def _mm_rows(name, pairs, n_out, *, nt, tm, nc, epi="plain", extra=None, out_dtype=F32, after=None, shards=False,
             norm_g=None):
    M = pairs[0][0].shape[0]
    np_ = len(pairs)
    after_args, after_specs = _after_operand(after)

    def body(*refs):
        a_refs = refs[:np_]
        w_refs = refs[np_:2 * np_]
        pos = 2 * np_
        e_ref = None
        if extra is not None:
            e_ref = refs[pos]
            pos += 1
        if norm_g is not None:
            g_ref = refs[pos]
            pos += 1
        pos += len(after_args)
        outs = refs[pos:]
        a_vals = [_bf(a[...]) for a in a_refs]
        for j in range(n_out // nc):
            cols, acc = _chunk_product(a_vals, w_refs, j, nc, nt, shards)
            if epi == "plain":
                outs[0][:, cols] = acc.astype(out_dtype)
            elif epi == "res":
                outs[0][:, cols] = e_ref[:, cols] + acc
            elif epi == "relu2":
                outs[0][:, cols] = _bf(acc)
                rl = jnp.maximum(acc, 0.0)
                outs[1][:, cols] = _bf(rl * rl)
            elif epi == "mask2relu":
                outs[0][:, cols] = _bf(acc * (2.0 * jnp.maximum(e_ref[:, cols].astype(F32), 0.0)))
        if norm_g is not None:
            hv = outs[0][...]
            r = lax.rsqrt(jnp.mean(hv * hv, axis=-1, keepdims=True) + EPS)
            outs[1][...] = _bf(hv * r * g_ref[...])

    in_specs = [pl.BlockSpec((tm, a.shape[1]), lambda i: (i, 0)) for a, _, _, _ in pairs]
    for _, _, wshape, widx in pairs:
        in_specs.append(pl.BlockSpec(wshape, functools.partial(lambda i, widx: widx, widx=widx)))
    args = [a for a, _, _, _ in pairs] + [w for _, w, _, _ in pairs]
    if extra is not None:
        in_specs.append(pl.BlockSpec((tm, n_out), lambda i: (i, 0)))
        args.append(extra)
    if norm_g is not None:
        in_specs.append(pl.BlockSpec((1, n_out), lambda i: (0, 0)))
        args.append(norm_g)
    in_specs += after_specs
    args += after_args
    row_out = pl.BlockSpec((tm, n_out), lambda i: (i, 0))
    if epi == "relu2":
        out_specs = [row_out, row_out]
        out_shape = [jax.ShapeDtypeStruct((M, n_out), BF16), jax.ShapeDtypeStruct((M, n_out), BF16)]
    elif norm_g is not None:
        out_specs = [row_out, row_out]
        out_shape = [jax.ShapeDtypeStruct((M, n_out), F32), jax.ShapeDtypeStruct((M, n_out), BF16)]
    else:
        dt = BF16 if epi == "mask2relu" else (F32 if epi == "res" else out_dtype)
        out_specs = row_out
        out_shape = jax.ShapeDtypeStruct((M, n_out), dt)
    return pl.pallas_call(
        body, name=name, grid=(M // tm,), in_specs=in_specs, out_specs=out_specs, out_shape=out_shape,
        compiler_params=_params(("parallel",)),
    )(*args)


def _whole(w):
    return w.shape, (0,) * w.ndim


def _mm_nn(name, a, w, **kw):
    n_out = w.shape[0] * w.shape[2] if w.ndim == 3 else w.shape[1]
    return _mm_rows(name, [(a, w, *_whole(w))], n_out, nt=False, **kw)


def _mm_nt(name, a, w, **kw):
    return _mm_rows(name, [(a, w, *_whole(w))], w.shape[0], nt=True, **kw)


def _mm_nt_rms_bwd(name, pairs, x, g, dres, *, tm, nc, after=None, shards=False, emit_bf16=True):
    M, D = x.shape
    np_ = len(pairs)
    n_out = 3 if emit_bf16 else 2
    after_args, after_specs = _after_operand(after)
    scratch_shapes = [pltpu.VMEM((tm, D), F32)]
    if shards:
        slots = pairs[0][1]
        scratch_shapes.append(pltpu.VMEM((slots.shape[1], N_DEV * slots.shape[2]), slots.dtype))
    n_scratch = len(scratch_shapes)

    def body(*refs):
        a_refs = refs[:np_]
        w_refs = refs[np_:2 * np_]
        x_ref, g_ref, r_ref = refs[2 * np_:2 * np_ + 3]
        scratch = refs[len(refs) - n_scratch:]
        dy_sc = scratch[0]
        outs = refs[len(refs) - n_scratch - n_out:len(refs) - n_scratch]
        dx_ref, dg_ref = outs[0], outs[1]
        i = pl.program_id(0)
        if shards:
            w_nat = scratch[1]
            nl = w_refs[0].shape[2]

            @pl.when(i == 0)
            def _():
                for k in range(N_DEV):
                    w_nat[:, k * nl:(k + 1) * nl] = w_refs[0][k]

            w_refs = [w_nat]
        a_vals = [_bf(a[...]) for a in a_refs]
        for j in range(D // nc):
            cols, acc = _chunk_product(a_vals, w_refs, j, nc, True, False)
            dy_sc[:, cols] = acc
        xv = x_ref[...]
        r = lax.rsqrt(jnp.mean(xv * xv, axis=-1, keepdims=True) + EPS)
        xh = xv * r
        dy_v = dy_sc[...]
        dyg = dy_v * g_ref[...]
        c = jnp.mean(dyg * xh, axis=-1, keepdims=True)
        dx = r_ref[...] + r * (dyg - xh * c)
        dx_ref[...] = dx
        if emit_bf16:
            outs[2][...] = _bf(dx)
        part = jnp.sum(dy_v * xh, axis=0, keepdims=True)

        @pl.when(i == 0)
        def _():
            dg_ref[...] = part

        @pl.when(i > 0)
        def _():
            dg_ref[...] += part

    row = pl.BlockSpec((tm, D), lambda i: (i, 0))
    vec = pl.BlockSpec((1, D), lambda i: (0, 0))
    in_specs = [pl.BlockSpec((tm, a.shape[1]), lambda i: (i, 0)) for a, _, _, _ in pairs]
    for _, _, wshape, widx in pairs:
        in_specs.append(pl.BlockSpec(wshape, functools.partial(lambda i, widx: widx, widx=widx)))
    args = [a for a, _, _, _ in pairs] + [w for _, w, _, _ in pairs]
    return pl.pallas_call(
        body, name=name, grid=(M // tm,),
        in_specs=in_specs + [row, vec, row] + after_specs,
        out_specs=[row, vec] + [row] * (n_out - 2),
        out_shape=[jax.ShapeDtypeStruct((M, D), F32), jax.ShapeDtypeStruct((1, D), F32)]
        + [jax.ShapeDtypeStruct((M, D), BF16)] * (n_out - 2),
        scratch_shapes=scratch_shapes,
        compiler_params=_params(("arbitrary",)),
    )(*args, x, g, dres, *after_args)


def _mm_tn(name, a, b, *, t1, tn, tm=2048, slot_cols=None):
    M, K1 = a.shape
    N = b.shape[1]
    nm = M // tm

    def body(a_ref, b_ref, o_ref, acc_ref):
        m = pl.program_id(2)
        t = _dot(_bf(a_ref[...]), _bf(b_ref[...]), TN)

        @pl.when(m == 0)
        def _():
            acc_ref[...] = t

        @pl.when(m > 0)
        def _():
            acc_ref[...] += t

        @pl.when(m == nm - 1)
        def _():
            if slot_cols is None:
                o_ref[...] = _bf(acc_ref[...])
            else:
                for k in range(tn // slot_cols):
                    o_ref[k] = _bf(acc_ref[:, k * slot_cols:(k + 1) * slot_cols])

    if slot_cols is not None:
        out_spec = pl.BlockSpec((tn // slot_cols, t1, slot_cols), lambda i, j, m: (j, i, 0))
        out_shape = jax.ShapeDtypeStruct((N // slot_cols, K1, slot_cols), BF16)
    else:
        out_spec = pl.BlockSpec((t1, tn), lambda i, j, m: (i, j))
        out_shape = jax.ShapeDtypeStruct((K1, N), BF16)
    return pl.pallas_call(
        body, name=name, grid=(K1 // t1, N // tn, nm),
        in_specs=[pl.BlockSpec((tm, t1), lambda i, j, m: (m, i)), pl.BlockSpec((tm, tn), lambda i, j, m: (m, j))],
        out_specs=out_spec, out_shape=out_shape,
        scratch_shapes=[pltpu.VMEM((t1, tn), F32)],
        compiler_params=_params(("parallel", "parallel", "arbitrary")),
    )(a, b)


_INV_SQRT2 = 1.0 / math.sqrt(2.0)
_INV_SQRT2PI = 1.0 / math.sqrt(2.0 * math.pi)


def _gelu(x):
    return 0.5 * x * (1.0 + lax.erf(x * _INV_SQRT2))


def _gelu_and_grad(x):
    cdf = 0.5 * (1.0 + lax.erf(x * _INV_SQRT2))
    return x * cdf, cdf + x * (_INV_SQRT2PI * jnp.exp(-0.5 * x * x))


def _layer_norm_parts(v):
    mu = jnp.mean(v, axis=-1, keepdims=True)
    xc = v - mu
    rs = lax.rsqrt(jnp.mean(xc * xc, axis=-1, keepdims=True) + EPS)
    return xc * rs, rs


def _gate_fwd(name, uvp, ln_g, ln_b, wm, bs_full, tr=512):
    S, W2 = uvp.shape
    W = W2 // 2
    gd = W // GROUPS

    def body(u_ref, v_ref, lg_ref, lb_ref, wm_ref, bs_ref, z_ref):
        vh, _ = _layer_norm_parts(_gelu(v_ref[...]))
        vn = _bf(vh * lg_ref[...] + lb_ref[...])
        for ci in range(tr // CHUNK):
            rows = slice(ci * CHUNK, (ci + 1) * CHUNK)
            for g in range(GROUPS):
                cols = slice(g * gd, (g + 1) * gd)
                mixed = _dot(wm_ref[g], vn[rows, cols], NN) + bs_ref[:, cols]
                z_ref[rows, cols] = _bf(_gelu(u_ref[rows, cols]) * mixed)

    vec = pl.BlockSpec((1, W), lambda i: (0, 0))
    return pl.pallas_call(
        body, name=name, grid=(S // tr,),
        in_specs=[pl.BlockSpec((tr, W), lambda i: (i, 0)), pl.BlockSpec((tr, W), lambda i: (i, 1)), vec, vec,
                  pl.BlockSpec((GROUPS, CHUNK, CHUNK), lambda i: (0, 0, 0)),
                  pl.BlockSpec((CHUNK, W), lambda i: (0, 0))],
        out_specs=pl.BlockSpec((tr, W), lambda i: (i, 0)),
        out_shape=jax.ShapeDtypeStruct((S, W), BF16),
        compiler_params=_params(("parallel",)),
    )(uvp, uvp, ln_g, ln_b, wm, bs_full)


def _gate_bwd(name, uvp, dz, ln_g, ln_b, wm, bs_full, tr=256):
    S, W2 = uvp.shape
    W = W2 // 2
    gd = W // GROUPS
    n_steps = S // tr

    def body(u_ref, v_ref, dz_ref, lg_ref, lb_ref, wm_ref, bs_ref, duv_ref, dwm_ref, dmx_ref, dlg_ref, dlb_ref,
             dvn_ref):
        i = pl.program_id(0)
        v, dv_dvp = _gelu_and_grad(v_ref[...])
        vh, rs = _layer_norm_parts(v)
        lg = lg_ref[...]
        vn = _bf(vh * lg + lb_ref[...])

        @pl.when(i == 0)
        def _():
            dwm_ref[...] = jnp.zeros_like(dwm_ref)
            dmx_ref[...] = jnp.zeros_like(dmx_ref)
            dlg_ref[...] = jnp.zeros_like(dlg_ref)
            dlb_ref[...] = jnp.zeros_like(dlb_ref)

        for ci in range(tr // CHUNK):
            rows = slice(ci * CHUNK, (ci + 1) * CHUNK)
            for g in range(GROUPS):
                cols = slice(g * gd, (g + 1) * gd)
                u, du_dup = _gelu_and_grad(u_ref[rows, cols])
                dz_v = dz_ref[rows, cols]
                dmixed = dz_v * u
                dmx_ref[:, cols] += dmixed
                dmixed_b = _bf(dmixed)
                mixed = _dot(wm_ref[g], vn[rows, cols], NN) + bs_ref[:, cols]
                duv_ref[rows, cols] = _bf(dz_v * mixed * du_dup)
                dwm_ref[g] += _dot(dmixed_b, vn[rows, cols], NT)
                dvn_ref[rows, cols] = _dot(wm_ref[g], dmixed_b, TN)
        dvn = dvn_ref[...]
        dlg_ref[...] += jnp.sum(dvn * vh, axis=0, keepdims=True)
        dlb_ref[...] += jnp.sum(dvn, axis=0, keepdims=True)
        dvh = dvn * lg
        dv = rs * (dvh - jnp.mean(dvh, axis=-1, keepdims=True) - vh * jnp.mean(dvh * vh, axis=-1, keepdims=True))
        duv_ref[:, W:] = _bf(dv * dv_dvp)

        @pl.when(i == n_steps - 1)
        def _():
            t_idx = lax.broadcasted_iota(jnp.int32, (CHUNK, CHUNK), 0)
            s_idx = lax.broadcasted_iota(jnp.int32, (CHUNK, CHUNK), 1)
            keep = (s_idx <= t_idx).astype(F32)
            for g in range(GROUPS):
                dwm_ref[g] = dwm_ref[g] * keep

    vec = pl.BlockSpec((1, W), lambda i: (0, 0))
    row = pl.BlockSpec((tr, W), lambda i: (i, 0))
    return pl.pallas_call(
        body, name=name, grid=(n_steps,),
        in_specs=[row, pl.BlockSpec((tr, W), lambda i: (i, 1)), row, vec, vec,
                  pl.BlockSpec((GROUPS, CHUNK, CHUNK), lambda i: (0, 0, 0)),
                  pl.BlockSpec((CHUNK, W), lambda i: (0, 0))],
        out_specs=[pl.BlockSpec((tr, W2), lambda i: (i, 0)),
                   pl.BlockSpec((GROUPS, CHUNK, CHUNK), lambda i: (0, 0, 0)),
                   pl.BlockSpec((CHUNK, W), lambda i: (0, 0)), vec, vec],
        out_shape=[jax.ShapeDtypeStruct((S, W2), BF16), jax.ShapeDtypeStruct((GROUPS, CHUNK, CHUNK), F32),
                   jax.ShapeDtypeStruct((CHUNK, W), F32), jax.ShapeDtypeStruct((1, W), F32),
                   jax.ShapeDtypeStruct((1, W), F32)],
        scratch_shapes=[pltpu.VMEM((tr, W), F32)],
        compiler_params=_params(("arbitrary",)),
    )(uvp, uvp, dz, ln_g, ln_b, wm, bs_full)


def _t5_bucket(distance):
    small = distance < MAX_EXACT
    nf = jnp.maximum(distance, 1).astype(F32)
    large = MAX_EXACT + (jnp.log(nf / MAX_EXACT) / math.log(REL_MAX_DISTANCE / MAX_EXACT)
                         * (N_BUCKETS - MAX_EXACT)).astype(jnp.int32)
    large = jnp.minimum(large, N_BUCKETS - 1)
    return jnp.where(small, distance, large)


TILE_ELEMS = 2 * CHUNK * CHUNK


def _band_buckets():
    rel = CHUNK + jnp.arange(CHUNK)[None, :] - jnp.arange(2 * CHUNK)[:, None]
    band = (rel >= 0) & (rel <= CHUNK)
    buckets = [_t5_bucket(jnp.clip(rel, 0, CHUNK) * d) for d in DILATIONS]
    return jnp.stack(buckets), band


def _bucket_onehot():
    buckets, _ = _band_buckets()
    return (buckets.reshape(N_DIL, 1, TILE_ELEMS) == jnp.arange(N_BUCKETS)[None, :, None]).astype(F32)


def _bias_tiles(name, rel_bias, after=None):
    _, band = _band_buckets()
    own = band & (jnp.arange(2 * CHUNK) >= CHUNK)[:, None]
    masks = jnp.stack([own, band]).reshape(2, TILE_ELEMS).astype(F32)
    tables = jnp.transpose(rel_bias.reshape(N_BUCKETS, N_DIL, ATT_HEADS), (1, 2, 0))
    after_args, after_specs = _after_operand(after)

    def body(t_ref, oh_ref, m_ref, *rest):
        out_ref = rest[-1]
        for g in range(N_DIL):
            bias = lax.dot_general(t_ref[g], oh_ref[g], (NN, ((), ())), precision=lax.Precision.HIGHEST,
                                   preferred_element_type=F32)
            for f in range(2):
                out_ref[g, f] = jnp.where(m_ref[f:f + 1, :] > 0.5, bias, NEG_INF)

    whole = pl.BlockSpec(memory_space=pltpu.VMEM)
    out = pl.pallas_call(
        body, name=name, out_shape=jax.ShapeDtypeStruct((N_DIL, 2, ATT_HEADS, TILE_ELEMS), F32),
        in_specs=[whole, whole, whole] + after_specs, out_specs=whole,
        compiler_params=_params(),
    )(tables, _bucket_onehot(), masks, *after_args)
    out = out.reshape(N_DIL, 2, ATT_HEADS // 2, 2, 2 * CHUNK, CHUNK)
    return jnp.transpose(out, (0, 1, 2, 4, 3, 5)).reshape(N_DIL, 2, ATT_HEADS // 2, 2 * CHUNK, 2 * CHUNK)


def _att_specs(order):
    def column(part, ids):
        hp, g, _ = order(*ids)
        return part * 3 * 4 + g * 4 + hp

    def window(part):
        def index(*ids):
            c = order(*ids)[2]
            return pl.multiple_of(jnp.maximum(c - 1, 0) * ATT_ROWS, ATT_ROWS), column(part, ids) * LANES
        return pl.BlockSpec((pl.Element(2 * ATT_ROWS), pl.Element(LANES)), index)

    return [pl.BlockSpec((ATT_ROWS, LANES), lambda *ids: (order(*ids)[2], column(0, ids))), window(1), window(2)]


def _window_base(c):
    return jnp.where(c == 0, 0, ATT_ROWS)


def _rows(start, d):
    if d == 1:
        return pl.ds(pl.multiple_of(start, CHUNK), CHUNK)
    return pl.ds(start, CHUNK, stride=d)


def _att_tile_offsets(t, d):
    n = t // d
    r = t % d
    return n * (CHUNK * d) + r, n


def _head_pair_columns(x_t):
    zeros = jnp.zeros((HEAD_DIM, CHUNK), x_t.dtype)
    return jnp.concatenate([jnp.concatenate([x_t[:HEAD_DIM], zeros], axis=0),
                            jnp.concatenate([zeros, x_t[HEAD_DIM:]], axis=0)], axis=1)


def _head_pair_rows(y):
    return jnp.concatenate([y[:HEAD_DIM, :CHUNK], y[HEAD_DIM:, CHUNK:]], axis=0)


def _att_fwd(name, qkv, bias_tiles):
    S = qkv.shape[0]
    n_chunks = S // ATT_ROWS
    tiles = ATT_ROWS // CHUNK

    def body(q_ref, kk, vv, b_ref, o_ref, l_ref):
        c = pl.program_id(1)
        g = pl.program_id(2)
        base = _window_base(c)

        for gi, d in enumerate(DILATIONS):
            @pl.when(g == gi)
            def _(d=d):
                span = CHUNK * d

                def tile(t, carry):
                    q0, n = _att_tile_offsets(t, d)
                    first = (c == 0) & (n == 0)
                    rows = _rows(q0, d)
                    cur = _rows(base + q0, d)
                    prev = _rows(jnp.where(first, q0, base + q0 - span), d)
                    inner = jnp.where(first, 0, 1)
                    qq = _head_pair_columns(_bf(q_ref[rows, :] * ATT_SCALE).T)
                    s_p = _dot(_bf(kk[prev, :]), qq, NN) + b_ref[inner, 0:CHUNK, :]
                    s_c = _dot(_bf(kk[cur, :]), qq, NN) + b_ref[inner, CHUNK:2 * CHUNK, :]
                    m = jnp.maximum(jnp.max(s_p, axis=0, keepdims=True), jnp.max(s_c, axis=0, keepdims=True))
                    p_p = jnp.exp(s_p - m)
                    p_c = jnp.exp(s_c - m)
                    l = jnp.sum(p_p, axis=0, keepdims=True) + jnp.sum(p_c, axis=0, keepdims=True)
                    o2 = (_dot(_bf(vv[prev, :]).T, _bf(p_p), NN)
                          + _dot(_bf(vv[cur, :]).T, _bf(p_c), NN)) * (1.0 / l)
                    lse = m + jnp.log(l)
                    l_t = jnp.concatenate([jnp.broadcast_to(lse[:, :CHUNK], (HEAD_DIM, CHUNK)),
                                           jnp.broadcast_to(lse[:, CHUNK:], (HEAD_DIM, CHUNK))], axis=0)
                    o_ref[rows, :] = _head_pair_rows(o2).T
                    l_ref[rows, :] = l_t.T
                    return carry

                lax.fori_loop(0, tiles, tile, 0, unroll=8)

    order = lambda hp, c, g: (hp, g, c)
    out_spec = pl.BlockSpec((None, ATT_ROWS, LANES), lambda hp, c, g: (g, c, hp))
    shape = jax.ShapeDtypeStruct((N_DIL, S, ATT_WIDTH), F32)
    return pl.pallas_call(
        body, name=name, grid=(ATT_HEADS // 2, n_chunks, N_DIL),
        in_specs=_att_specs(order) + [
            pl.BlockSpec((None, 2, None, 2 * CHUNK, 2 * CHUNK), lambda hp, c, g: (g, 0, hp, 0, 0))],
        out_specs=[out_spec, out_spec],
        out_shape=[shape, shape],
        compiler_params=_params(("parallel", "parallel", "parallel")),
    )(qkv, qkv, qkv, bias_tiles)


def _att_merge(name, o_g, l_g, tm=512):
    _, S, W = o_g.shape

    def body(o_ref, l_ref, out_ref, lse_ref):
        ls = [l_ref[g] for g in range(N_DIL)]
        mx = functools.reduce(jnp.maximum, ls)
        ws = [jnp.exp(l - mx) for l in ls]
        tot = functools.reduce(lambda a, b: a + b, ws)
        acc = ws[0] * o_ref[0]
        for g in range(1, N_DIL):
            acc = acc + ws[g] * o_ref[g]
        out_ref[...] = acc / tot
        lse_ref[...] = mx + jnp.log(tot)

    blk = pl.BlockSpec((N_DIL, tm, W), lambda i: (0, i, 0))
    row = pl.BlockSpec((tm, W), lambda i: (i, 0))
    shape = jax.ShapeDtypeStruct((S, W), F32)
    return pl.pallas_call(
        body, name=name, grid=(S // tm,), in_specs=[blk, blk], out_specs=[row, row], out_shape=[shape, shape],
        compiler_params=_params(("parallel",)),
    )(o_g, l_g)


def _att_bwd(name, qkv, o, lse, d_o, bias_tiles):
    S = qkv.shape[0]
    n_chunks = S // ATT_ROWS
    tiles = ATT_ROWS // CHUNK

    def body(q_ref, kk, vv, o_ref, l_ref, do_ref, b_ref, dq_out, dk_out, dv_out, ds_ref, dq_ref, dk_ref, dv_ref):
        g = pl.program_id(1)
        c = pl.program_id(2)

        @pl.when(c == 0)
        def _():
            dk_ref[...] = jnp.zeros_like(dk_ref)
            dv_ref[...] = jnp.zeros_like(dv_ref)
            ds_ref[...] = jnp.zeros_like(ds_ref)

        base = _window_base(c)
        first_row = c * ATT_ROWS
        head0 = lax.broadcasted_iota(jnp.int32, (CHUNK, LANES), 1) < HEAD_DIM

        def head_pair_stack(x):
            zero = jnp.zeros_like(x)
            return jnp.concatenate([jnp.where(head0, x, zero), jnp.where(head0, zero, x)], axis=0)

        for gi, d in enumerate(DILATIONS):
            @pl.when(g == gi)
            def _(d=d):
                span = CHUNK * d

                def tile(t, carry):
                    q0, n = _att_tile_offsets(t, d)
                    first = (c == 0) & (n == 0)
                    rows = _rows(q0, d)
                    cur = _rows(base + q0, d)
                    prev = _rows(jnp.where(first, q0, base + q0 - span), d)
                    inner = jnp.where(first, 0, 1)
                    g_cur = _rows(first_row + q0, d)
                    g_prev = _rows(jnp.where(first, q0, first_row + q0 - span), d)
                    q2 = _bf(q_ref[rows, :] * ATT_SCALE)
                    q_t = q2.T
                    k2 = _bf(jnp.concatenate([kk[prev, :], kk[cur, :]], axis=0))
                    k_t = k2.T
                    v2 = _bf(jnp.concatenate([vv[prev, :], vv[cur, :]], axis=0))
                    do2 = do_ref[rows, :]
                    do_b = _bf(do2)
                    do_t = do_b.T
                    lse_t = l_ref[rows, :].T
                    dd_t = (do2 * o_ref[rows, :]).T
                    lse = jnp.concatenate([lse_t[0:1], lse_t[HEAD_DIM:HEAD_DIM + 1]], axis=1)
                    delta = jnp.concatenate([jnp.sum(dd_t[:HEAD_DIM], axis=0, keepdims=True),
                                             jnp.sum(dd_t[HEAD_DIM:], axis=0, keepdims=True)], axis=1)
                    s = _dot(k2, _head_pair_columns(q_t), NN) + b_ref[inner]
                    p = jnp.exp(s - lse)
                    ds = p * (_dot(v2, _head_pair_columns(do_t), NN) - delta)
                    ds_ref[...] += ds
                    ds_b = _bf(ds)
                    dq_t = _head_pair_rows(_dot(k_t, ds_b, NN))
                    dk2 = _dot(ds_b, head_pair_stack(q2), NN)
                    dv2 = _dot(_bf(p), head_pair_stack(do_b), NN)
                    dq_ref[rows, :] = (dq_t * ATT_SCALE).T
                    dk_ref[g_prev, :] += dk2[0:CHUNK]
                    dk_ref[g_cur, :] += dk2[CHUNK:2 * CHUNK]
                    dv_ref[g_prev, :] += dv2[0:CHUNK]
                    dv_ref[g_cur, :] += dv2[CHUNK:2 * CHUNK]
                    return carry

                lax.fori_loop(0, tiles, tile, 0, unroll=8)

        dq_out[...] = _bf(dq_ref[...])

        @pl.when(c == n_chunks - 1)
        def _():
            dk_out[...] = _bf(dk_ref[...])
            dv_out[...] = _bf(dv_ref[...])

    order = lambda hp, g, c: (hp, g, c)
    chunk = pl.BlockSpec((ATT_ROWS, LANES), lambda hp, g, c: (c, hp))
    slab = pl.BlockSpec((S, LANES), lambda hp, g, c: (0, g * 4 + hp))
    width = N_DIL * ATT_WIDTH
    dq, dk, dv, ds_sums = pl.pallas_call(
        body, name=name, grid=(ATT_HEADS // 2, N_DIL, n_chunks),
        in_specs=_att_specs(order) + [chunk, chunk, chunk,
                                      pl.BlockSpec((None, 2, None, 2 * CHUNK, 2 * CHUNK),
                                                   lambda hp, g, c: (g, 0, hp, 0, 0))],
        out_specs=[pl.BlockSpec((ATT_ROWS, LANES), lambda hp, g, c: (c, g * 4 + hp)), slab, slab,
                   pl.BlockSpec((None, None, 2 * CHUNK, 2 * CHUNK), lambda hp, g, c: (g, hp, 0, 0))],
        out_shape=[jax.ShapeDtypeStruct((S, width), BF16), jax.ShapeDtypeStruct((S, width), BF16),
                   jax.ShapeDtypeStruct((S, width), BF16),
                   jax.ShapeDtypeStruct((N_DIL, ATT_HEADS // 2, 2 * CHUNK, 2 * CHUNK), F32)],
        scratch_shapes=[pltpu.VMEM((ATT_ROWS, LANES), F32), pltpu.VMEM((S, LANES), F32),
                        pltpu.VMEM((S, LANES), F32)],
        compiler_params=_params(("parallel", "parallel", "arbitrary")),
    )(qkv, qkv, qkv, o, lse, d_o, bias_tiles)
    ds_sums = ds_sums.reshape(N_DIL, ATT_HEADS // 2, 2 * CHUNK, 2, CHUNK)
    ds_sums = jnp.transpose(ds_sums, (0, 1, 3, 2, 4)).reshape(N_DIL, ATT_HEADS, 2 * CHUNK, CHUNK)
    return dq, dk, dv, ds_sums


def _bias_grad(name, ds_sums):
    flat = ds_sums.reshape(N_DIL, ATT_HEADS, TILE_ELEMS)

    def body(oh_ref, ds_ref, out_ref):
        for g in range(N_DIL):
            out_ref[g] = lax.dot_general(oh_ref[g], ds_ref[g], (NT, ((), ())), precision=lax.Precision.HIGHEST,
                                         preferred_element_type=F32)

    out = pl.pallas_call(
        body, name=name, out_shape=jax.ShapeDtypeStruct((N_DIL, N_BUCKETS, ATT_HEADS), F32),
        compiler_params=_params(),
    )(_bucket_onehot(), flat)
    return jnp.transpose(out, (1, 0, 2)).reshape(N_BUCKETS, N_DIL * ATT_HEADS)


def _peers():
    x, y, c = lax.axis_index("x"), lax.axis_index("y"), lax.axis_index("c")
    me = 4 * x + 2 * y + c
    others = [(x, y, 1 - c), (1 - x, y, c), (x, 1 - y, c), (1 - x, 1 - y, c),
              (1 - x, y, 1 - c), (x, 1 - y, 1 - c), (1 - x, 1 - y, 1 - c)]
    return me, others


def _slot(dev):
    return 4 * dev[0] + 2 * dev[1] + dev[2]


_HBM =pl.BlockSpec(memory_space=pltpu.HBM)
_SEM = pl.BlockSpec(memory_space=pltpu.SEMAPHORE)
_EFFECT = pltpu.SideEffectType.DATAFLOW_SIDE_EFFECTING


def _my_slot():
    return 4 * lax.axis_index("x") + 2 * lax.axis_index("y") + lax.axis_index("c")


def _exchange_copy(src_ref, land_ref, send_sems, recv_sems, k, dev, me, scatter, arriving):
    src = src_ref.at[me if arriving else _slot(dev)] if scatter else src_ref
    dst = land_ref.at[_slot(dev) if arriving else me]
    return pltpu.make_async_remote_copy(src_ref=src, dst_ref=dst, send_sem=send_sems.at[k], recv_sem=recv_sems.at[k],
                                        device_id=dev, device_id_type=MESH)


def _exchange_start(name, srcs, scatter):
    n = len(srcs)
    me = _my_slot()
    landings = []
    for src in srcs:
        own = lax.dynamic_index_in_dim(src, me, 0, keepdims=True) if scatter else src[None]
        landings.append(lax.dynamic_update_slice(lax.empty((N_DEV,) + src.shape[-2:], src.dtype), own, (me, 0, 0)))

    def body(*refs):
        src_refs, land_refs = refs[:n], refs[n:2 * n]
        send_sems, recv_sems = refs[2 * n:2 * n + 2]
        token = refs[-1]
        me, others = _peers()
        for p in range(n):
            for k, dev in enumerate(others):
                _exchange_copy(src_refs[p], land_refs[p], send_sems, recv_sems, p * (N_DEV - 1) + k, dev, me,
                               scatter, False).start()
        token[...] = jnp.zeros_like(token)

    sems = pltpu.SemaphoreType.DMA((n * (N_DEV - 1),))
    hbm = lambda a: pltpu.with_memory_space_constraint(a, pltpu.HBM)
    outs = pl.pallas_call(
        body, name=name,
        out_shape=(sems, sems, *[pltpu.HBM(a.shape, a.dtype) for a in srcs + landings],
                   jax.ShapeDtypeStruct((8, LANES), F32)),
        in_specs=(_HBM,) * (2 * n), out_specs=(_SEM, _SEM) + (_HBM,) * (2 * n) + (pl.BlockSpec(memory_space=pltpu.VMEM),),
        input_output_aliases={i: 2 + i for i in range(2 * n)},
        compiler_params=pltpu.CompilerParams(has_side_effects=_EFFECT),
    )(*[hbm(a) for a in srcs + landings])
    return (outs[0], outs[1], list(outs[2:2 + n]), list(outs[2 + n:2 + 2 * n]), scatter), outs[-1]


def _exchange_wait(name, handle, after):
    send_sems, recv_sems, src_thru, land_thru, scatter = handle
    n = len(src_thru)

    def body(*refs):
        src_refs, land_refs = refs[:n], refs[n:2 * n]
        send_sems, recv_sems = refs[2 * n:2 * n + 2]
        me, others = _peers()
        for p in range(n):
            for k, dev in enumerate(others):
                cp = _exchange_copy(src_refs[p], land_refs[p], send_sems, recv_sems, p * (N_DEV - 1) + k, dev, me,
                                    scatter, True)
                cp.wait_send()
                cp.wait_recv()

    outs = pl.pallas_call(
        body, name=name,
        out_shape=tuple(pltpu.HBM(a.shape, a.dtype) for a in src_thru + land_thru),
        in_specs=(_HBM,) * (2 * n) + (_SEM, _SEM, pl.BlockSpec(memory_space=pl.ANY)), out_specs=(_HBM,) * (2 * n),
        input_output_aliases={i: i for i in range(2 * n)},
        compiler_params=pltpu.CompilerParams(has_side_effects=_EFFECT),
    )(*src_thru, *land_thru, send_sems, recv_sems, after)
    return list(outs[n:])


def _adamw_math(w, g, m, v):
    m = ADAM_B1 * m + (1.0 - ADAM_B1) * g
    v = ADAM_B2 * v + (1.0 - ADAM_B2) * (g * g)
    m_hat = m / (1.0 - ADAM_B1 ** ADAM_STEP)
    v_hat = v / (1.0 - ADAM_B2 ** ADAM_STEP)
    delta = -ADAM_LR * (m_hat / (jnp.sqrt(v_hat) + ADAM_EPS) + ADAM_WD * w)
    return delta, m, v


_SMALL_WIDE = (("mix_norm_g", 2), ("mlp_norm_g", 2), ("final_norm_g", 1), ("a_ln_g", 1), ("a_ln_b", 1))
_SMALL_NARROW = (("a_w_s", GROUPS * CHUNK), ("a_b_s", GROUPS), ("rel_bias", N_BUCKETS))
_SMALL = tuple(n for n, _ in _SMALL_WIDE + _SMALL_NARROW)
_BIAS_COLS = N_DIL * ATT_HEADS


def _pack_small_grads(grads, loss_term):
    D = grads["a_ln_g"].shape[-1]
    tiles = [jnp.pad(grads[n].reshape(k, D), ((0, 8 - k), (0, 0))) for n, k in _SMALL_WIDE]
    tiles.append(jnp.pad(loss_term.reshape(1, 1), ((0, 7), (0, D - 1))))
    narrow = [grads["a_w_s"].reshape(-1, LANES), grads["a_b_s"].reshape(-1, LANES),
              jnp.pad(grads["rel_bias"], ((0, 0), (0, LANES - _BIAS_COLS)))]
    return jnp.concatenate(tiles, axis=0), jnp.concatenate(narrow, axis=0)


def _adamw_small(name, g_wide, g_narrow, w, m, v):
    D = g_wide.shape[-1]
    shapes = {n: (k, D) for n, k in _SMALL_WIDE}
    shapes.update({n: (k, LANES) for n, k in _SMALL_NARROW})
    shapes["rel_bias"] = (N_BUCKETS, _BIAS_COLS)
    n_t = len(_SMALL)

    def body(gw_ref, gn_ref, *rest):
        params = rest[:3 * n_t]
        outs = rest[3 * n_t:3 * n_t + 4 * n_t]
        loss_ref, sw, sn = rest[-3:]
        sw[...] = functools.reduce(lambda a, b: a + b, [gw_ref[j] for j in range(N_DEV)])
        sn[...] = functools.reduce(lambda a, b: a + b, [gn_ref[j] for j in range(N_DEV)])
        row = 0
        for i, n in enumerate(_SMALL):
            k, cols = shapes[n]
            if i < len(_SMALL_WIDE):
                g = sw[8 * i:8 * i + k, :]
            else:
                g = sn[row:row + k, 0:cols]
                row += k
            w_ref, m_ref, v_ref = params[3 * i:3 * i + 3]
            delta, m_new, v_new = _adamw_math(w_ref[...], g, m_ref[...], v_ref[...])
            for out, val in zip(outs[4 * i:4 * i + 4], (g, delta, m_new, v_new)):
                out[...] = val
        loss_ref[...] = sw[8 * len(_SMALL_WIDE):8 * len(_SMALL_WIDE) + 8, 0:LANES]

    whole = pl.BlockSpec(memory_space=pltpu.VMEM)
    args = [t[n].reshape(shapes[n]) for n in _SMALL for t in (w, m, v)]
    res = pl.pallas_call(
        body, name=name,
        in_specs=[whole] * (2 + len(args)), out_specs=[whole] * (4 * n_t + 1),
        out_shape=[jax.ShapeDtypeStruct(shapes[n], F32) for n in _SMALL for _ in range(4)]
        + [jax.ShapeDtypeStruct((8, LANES), F32)],
        scratch_shapes=[pltpu.VMEM(g_wide.shape[1:], F32), pltpu.VMEM(g_narrow.shape[1:], F32)],
        compiler_params=_params(),
    )(g_wide, g_narrow, *args)
    small = {n: tuple(r.reshape(w[n].shape) for r in res[4 * i:4 * i + 4]) for i, n in enumerate(_SMALL)}
    return small, res[-1][0, 0]


def _adamw_shard(name, parts, w, m, v, layer, earlier=None, after=None, tr=256):
    L, K, N = w.shape
    tr = min(tr, K)
    n_prev = 0 if earlier is None else 4
    after_args, after_specs = _after_operand(after)

    def body(p_ref, w_ref, m_ref, v_ref, *rest):
        g_out, d_out, m_out, v_out = rest[n_prev + len(after_args):]
        g = p_ref[0].astype(F32)
        for j in range(1, N_DEV):
            g = g + p_ref[j].astype(F32)
        delta, m_new, v_new = _adamw_math(w_ref[...], g, m_ref[...], v_ref[...])
        g_out[...] = g
        d_out[...] = delta
        m_out[...] = m_new
        v_out[...] = v_new

    row = pl.BlockSpec((None, tr, N), lambda i: (layer, i, 0))
    shape = jax.ShapeDtypeStruct((L, K, N), F32)
    return pl.pallas_call(
        body, name=name, grid=(K // tr,),
        in_specs=[pl.BlockSpec((N_DEV, tr, N), lambda i: (0, i, 0)), row, row, row]
        + [pl.BlockSpec(memory_space=pl.ANY)] * n_prev + after_specs,
        out_specs=[row, row, row, row],
        out_shape=[shape, shape, shape, shape],
        input_output_aliases={4 + j: j for j in range(n_prev)},
        compiler_params=_params(("parallel",)),
    )(parts, w, m, v, *(earlier or ()), *after_args)


def _column_slots(full):
    K, N = full.shape
    return jnp.transpose(full.reshape(K, N_DEV, N // N_DEV), (1, 0, 2))


def _from_column_slots(slots):
    _, K, n = slots.shape
    return jnp.transpose(slots, (1, 0, 2)).reshape(K, N_DEV * n)


_STAGES = (("gate", ("a_w_in", "a_w_out"), 0),
           ("mlp0", ("w_up", "w_down"), 0),
           ("att", ("b_w_qkv", "b_w_out"), 0),
           ("mlp1", ("w_up", "w_down"), 1))


def kernel(x, mix_norm_g, mlp_norm_g, final_norm_g, a_w_in, a_ln_g, a_ln_b, a_w_s, a_b_s, a_w_out, b_w_qkv, b_w_out, rel_bias, w_up, w_down, loss_target, m_mix_norm_g, m_mlp_norm_g, m_final_norm_g, m_a_w_in, m_a_ln_g, m_a_ln_b, m_a_w_s, m_a_b_s, m_a_w_out, m_b_w_qkv, m_b_w_out, m_rel_bias, m_w_up, m_w_down, v_mix_norm_g, v_mlp_norm_g, v_final_norm_g, v_a_w_in, v_a_ln_g, v_a_ln_b, v_a_w_s, v_a_b_s, v_a_w_out, v_b_w_qkv, v_b_w_out, v_rel_bias, v_w_up, v_w_down):
    w = dict(mix_norm_g=mix_norm_g, mlp_norm_g=mlp_norm_g, final_norm_g=final_norm_g, a_w_in=a_w_in, a_ln_g=a_ln_g,
             a_ln_b=a_ln_b, a_w_s=a_w_s, a_b_s=a_b_s, a_w_out=a_w_out, b_w_qkv=b_w_qkv, b_w_out=b_w_out,
             rel_bias=rel_bias, w_up=w_up, w_down=w_down)
    m = dict(mix_norm_g=m_mix_norm_g, mlp_norm_g=m_mlp_norm_g, final_norm_g=m_final_norm_g, a_w_in=m_a_w_in,
             a_ln_g=m_a_ln_g, a_ln_b=m_a_ln_b, a_w_s=m_a_w_s, a_b_s=m_a_b_s, a_w_out=m_a_w_out, b_w_qkv=m_b_w_qkv,
             b_w_out=m_b_w_out, rel_bias=m_rel_bias, w_up=m_w_up, w_down=m_w_down)
    v = dict(mix_norm_g=v_mix_norm_g, mlp_norm_g=v_mlp_norm_g, final_norm_g=v_final_norm_g, a_w_in=v_a_w_in,
             a_ln_g=v_a_ln_g, a_ln_b=v_a_ln_b, a_w_s=v_a_w_s, a_b_s=v_a_b_s, a_w_out=v_a_w_out, b_w_qkv=v_b_w_qkv,
             b_w_out=v_b_w_out, rel_bias=v_rel_bias, w_up=v_w_up, w_down=v_w_down)

    stages = {s: (names, layer) for s, names, layer in _STAGES}
    order = [s for s, _, _ in _STAGES]

    def shards_of(stage):
        names, layer = stages[stage]
        return [_bf(w[n][layer]) for n in names]

    pending = {}
    pending[order[0]], first_token = _exchange_start("gather_" + order[0] + "_start", shards_of(order[0]), False)

    def get_weights(stage, dep):
        gathered = _exchange_wait("gather_" + stage + "_wait", pending.pop(stage), dep)
        nxt = order.index(stage) + 1
        token = None
        if nxt < len(order):
            shards, gathered = lax.optimization_barrier((shards_of(order[nxt]), gathered))
            pending[order[nxt]], token = _exchange_start("gather_" + order[nxt] + "_start", shards, False)
        return gathered, token

    sent = {}

    def put_grads(stage, slot_grads):
        sent[stage], token = _exchange_start("scatter_" + stage + "_start", slot_grads, True)
        return token

    loss_local, grad_x, small_g = _local_step(
        x[0], loss_target[0], mix_norm_g, mlp_norm_g, final_norm_g, a_ln_g, a_ln_b, a_w_s, a_b_s, rel_bias,
        get_weights, put_grads, first_token)

    small_sent, token = _exchange_start("gather_small_start", list(_pack_small_grads(small_g, loss_local)), False)

    results = {}
    prev = token
    for stage in reversed(order):
        names, layer = stages[stage]
        received = _exchange_wait("scatter_" + stage + "_wait", sent[stage], prev)
        for n, parts in zip(names, received):
            results[n] = _adamw_shard("adamw_%s_%s" % (stage, n), parts, w[n], m[n], v[n], layer, results.get(n),
                                      after=prev)
            prev = results[n][0]

    g_wide, g_narrow = _exchange_wait("gather_small_wait", small_sent, prev)
    small, loss = _adamw_small("adamw_small", g_wide, g_narrow, w, m, v)

    outs = []
    for j in range(4):
        outs.extend(small[n][j] if n in _SMALL else results[n][j] for n in w)
    return (loss, grad_x[None], *outs)


def _local_step(xs, tgt, mix_norm_g, mlp_norm_g, final_norm_g, a_ln_g, a_ln_b, a_w_s, a_b_s, rel_bias,
                get_weights, put_grads, first_token=None):
    D = xs.shape[-1]
    g_mix = [mix_norm_g[l][None, :] for l in range(2)]
    g_mlp = [mlp_norm_g[l][None, :] for l in range(2)]
    g_fin = final_norm_g[None, :]
    ln_g, ln_b = a_ln_g, a_ln_b
    causal = jnp.tril(jnp.ones((CHUNK, CHUNK), dtype=bool))
    wm = _bf(jnp.where(causal[None], a_w_s[0], 0.0))
    bs_full = jnp.repeat(a_b_s[0].T, D // GROUPS, axis=1)
    bias_tiles = _bias_tiles("att_bias", rel_bias, after=first_token)

    (win, wout), token = get_weights("gate", bias_tiles)
    wout = wout.reshape(-1, D)
    y0 = _rms_fwd("rms_mix0", xs, g_mix[0], after=token)
    uvp = _mm_nn("gate_in", y0, win, tm=512, nc=win.shape[2], shards=True)
    z = _gate_fwd("gate_mid", uvp, ln_g, ln_b, wm, bs_full)
    h1, y1 = _mm_nn("gate_out", z, wout, tm=512, nc=512, epi="res", extra=xs, norm_g=g_mlp[0])
    (wup0, wdn0), token = get_weights("mlp0", h1)
    wdn0 = wdn0.reshape(-1, D)
    a0, f0 = _mm_nn("mlp0_up", y1, wup0, tm=512, nc=wup0.shape[2], epi="relu2", shards=True, after=token)
    h2, y2 = _mm_nn("mlp0_down", f0, wdn0, tm=512, nc=512, epi="res", extra=h1, norm_g=g_mix[1])
    (wqkv, wo), token = get_weights("att", h2)
    wqkv, wo = _from_column_slots(wqkv), _from_column_slots(wo)
    qkv = _mm_nn("att_qkv", y2, wqkv, tm=512, nc=512, after=token)
    o_att, lse = _att_merge("att_merge", *_att_fwd("att_fwd", qkv, bias_tiles))
    h3, y3 = _mm_nn("att_out", o_att, wo, tm=512, nc=512, epi="res", extra=h2, norm_g=g_mlp[1])
    (wup1, wdn1), _ = get_weights("mlp1", h3)
    wdn1 = wdn1.reshape(-1, D)
    a1, f1 = _mm_nn("mlp1_up", y3, wup1, tm=512, nc=wup1.shape[2], epi="relu2", shards=True)
    dh, dg_fin, err2, dh_b = _mm_res_loss("mlp1_down_loss", f1, wdn1, h3, g_fin, tgt, tm=512, nc=512)
    loss_local = 0.5 * jnp.sum(err2) / D

    def mlp_bwd(tag, dh, dh_b, h_in, y, a, f, wup_l, wdn_l, g_row, after):
        da = _mm_nt(tag + "_dact", dh_b, wdn_l, tm=512, nc=512, epi="mask2relu", extra=a, after=after)
        g_dn = _mm_tn(tag + "_dwdown", f, dh_b, t1=1024, tn=1024, tm=DW_TOKENS)
        g_up = _mm_tn(tag + "_dwup", y, da, t1=1024, tn=1024, tm=DW_TOKENS, slot_cols=wup_l.shape[2])
        dh_in, dg, dh_in_b = _mm_nt_rms_bwd(tag + "_dy", [(da, wup_l, *_whole(wup_l))], h_in, g_row, dh, tm=512,
                                            nc=512, shards=True)
        return dh_in, dh_in_b, dg, put_grads(tag, [g_up, g_dn.reshape(N_DEV, -1, D)])

    dh3, dh3_b, dg_mlp1, token = mlp_bwd("mlp1", dh, dh_b, h3, y3, a1, f1, wup1, wdn1, g_mlp[1], None)

    d_o = _mm_nt("att_dout", dh3_b, wo, tm=512, nc=512, after=token)
    g_wo = _mm_tn("att_dwo", o_att, dh3_b, t1=512, tn=1024, tm=DW_TOKENS)
    dq, dk, dv, ds_sums = _att_bwd("att_bwd", qkv, o_att, lse, d_o, bias_tiles)
    part_w = N_DIL * ATT_WIDTH
    g_qkv = [_mm_tn("att_dwqkv%d" % p, y2, t, t1=1024, tn=part_w) for p, t in enumerate((dq, dk, dv))]
    dh2, dg_mix1, dh2_b = _mm_nt_rms_bwd(
        "att_dy", [(t, wqkv, (D, part_w), (0, p)) for p, t in enumerate((dq, dk, dv))], h2, g_mix[1], dh3, tm=512,
        nc=512)
    token = put_grads("att", [_column_slots(jnp.concatenate(g_qkv, axis=1)), _column_slots(g_wo)])

    dh1, dh1_b, dg_mlp0, token = mlp_bwd("mlp0", dh2, dh2_b, h1, y1, a0, f0, wup0, wdn0, g_mlp[0], token)

    dz = _mm_nt("gate_dz", dh1_b, wout, tm=512, nc=512, after=token)
    g_wout = _mm_tn("gate_dwout", z, dh1_b, t1=1024, tn=1024, tm=DW_TOKENS)
    duvp, d_wm, d_mixed, d_lng, d_lnb = _gate_bwd("gate_dmid", uvp, dz, ln_g, ln_b, wm, bs_full)
    g_win = _mm_tn("gate_dwin", y0, duvp, t1=1024, tn=1024, tm=DW_TOKENS, slot_cols=win.shape[2])
    token = put_grads("gate", [g_win, g_wout.reshape(N_DEV, -1, D)])
    grad_x, dg_mix0 = _mm_nt_rms_bwd("gate_dy", [(duvp, win, *_whole(win))], xs, g_mix[0], dh1, tm=512, nc=512,
                                     after=token, shards=True, emit_bf16=False)

    small_g = dict(
        mix_norm_g=jnp.concatenate([dg_mix0, dg_mix1], axis=0),
        mlp_norm_g=jnp.concatenate([dg_mlp0, dg_mlp1], axis=0),
        final_norm_g=dg_fin[0], a_ln_g=d_lng, a_ln_b=d_lnb, a_w_s=d_wm[None],
        a_b_s=jnp.sum(d_mixed.reshape(CHUNK, GROUPS, D // GROUPS), axis=2).T[None],
        rel_bias=_bias_grad("att_dbias", ds_sums))
    return loss_local, grad_x, small_g
```

```python
import functools
import math

import jax
import jax.numpy as jnp
from jax import lax
from jax.experimental import pallas as pl
from jax.experimental.pallas import tpu as pltpu

F32 = jnp.float32
BF16 = jnp.bfloat16
MESH = pl.DeviceIdType.MESH

N_DEV = 8
EPS = 1e-6
NEG_INF = -1e30
CHUNK = 128
GROUPS = 8
HEAD_DIM = 64
ATT_HEADS = 8
ATT_WIDTH = ATT_HEADS * HEAD_DIM
DILATIONS = (1, 4, 16)
N_DIL = len(DILATIONS)
N_BUCKETS = 32
MAX_EXACT = N_BUCKETS // 2
REL_MAX_DISTANCE = 2048
ATT_ROWS = 2048
ATT_SCALE = HEAD_DIM ** -0.5
DW_TOKENS = 4096
LANES = 128

ADAM_LR = 0.001
ADAM_B1 = 0.9
ADAM_B2 = 0.999
ADAM_EPS = 1e-08
ADAM_WD = 0.01
ADAM_STEP = 10

VMEM_LIMIT_BYTES = 56 * 1024 * 1024


def _params(semantics=None):
    return pltpu.CompilerParams(dimension_semantics=semantics, vmem_limit_bytes=VMEM_LIMIT_BYTES)


def _bf(v):
    return v.astype(BF16)


def _dot(a, b, dims):
    return lax.dot_general(a, b, (dims, ((), ())), preferred_element_type=F32)


NN = ((1,), (0,))
NT = ((1,), (1,))
TN = ((0,), (0,))


def _after_operand(after):
    if after is None:
        return [], []
    return [after], [pl.BlockSpec(memory_space=pl.ANY)]


def _rms_fwd(name, x, g, tm=512, after=None):
    S, D = x.shape
    after_args, after_specs = _after_operand(after)

    def body(x_ref, g_ref, *rest):
        y_ref = rest[-1]
        xv = x_ref[...]
        r = lax.rsqrt(jnp.mean(xv * xv, axis=-1, keepdims=True) + EPS)
        y_ref[...] = _bf(xv * r * g_ref[...])

    return pl.pallas_call(
        body, name=name, grid=(S // tm,),
        in_specs=[pl.BlockSpec((tm, D), lambda i: (i, 0)), pl.BlockSpec((1, D), lambda i: (0, 0))] + after_specs,
        out_specs=pl.BlockSpec((tm, D), lambda i: (i, 0)),
        out_shape=jax.ShapeDtypeStruct((S, D), BF16),
        compiler_params=_params(("parallel",)),
    )(x, g, *after_args)


def _mm_res_loss(name, a, w, res, g, target, *, tm, nc):
    M, D = res.shape

    def body(a_ref, w_ref, r_ref, g_ref, t_ref, dh_ref, dg_ref, l_ref, dhb_ref, h_sc):
        i = pl.program_id(0)
        a_v = _bf(a_ref[...])
        for j in range(D // nc):
            cols, acc = _chunk_product([a_v], [w_ref], j, nc, False, False)
            h_sc[:, cols] = r_ref[:, cols] + acc
        xv = h_sc[...]
        r = lax.rsqrt(jnp.mean(xv * xv, axis=-1, keepdims=True) + EPS)
        xh = xv * r
        gv = g_ref[...]
        e = xh * gv - t_ref[...]
        dout = e / D
        dyg = dout * gv
        c = jnp.mean(dyg * xh, axis=-1, keepdims=True)
        dh = r * (dyg - xh * c)
        dh_ref[...] = dh
        dhb_ref[...] = _bf(dh)
        dg_part = jnp.sum(dout * xh, axis=0, keepdims=True)
        l_part = jnp.sum(e * e, axis=0, keepdims=True)

        @pl.when(i == 0)
        def _():
            dg_ref[...] = dg_part
            l_ref[...] = l_part

        @pl.when(i > 0)
        def _():
            dg_ref[...] += dg_part
            l_ref[...] += l_part

    row = pl.BlockSpec((tm, D), lambda i: (i, 0))
    vec = pl.BlockSpec((1, D), lambda i: (0, 0))
    return pl.pallas_call(
        body, name=name, grid=(M // tm,),
        in_specs=[pl.BlockSpec((tm, a.shape[1]), lambda i: (i, 0)), pl.BlockSpec(w.shape, lambda i: (0, 0)),
                  row, vec, row],
        out_specs=[row, vec, vec, row],
        out_shape=[jax.ShapeDtypeStruct((M, D), F32), jax.ShapeDtypeStruct((1, D), F32),
                   jax.ShapeDtypeStruct((1, D), F32), jax.ShapeDtypeStruct((M, D), BF16)],
        scratch_shapes=[pltpu.VMEM((tm, D), F32)],
        compiler_params=_params(("arbitrary",)),
    )(a, w, res, g, target)


def _chunk_product(a_vals, w_refs, j, nc, nt, shards):
    cols = slice(j * nc, (j + 1) * nc)
    acc = None
    for a_v, w_ref in zip(a_vals, w_refs):
        if not shards:
            terms = [_dot(a_v, w_ref[cols, :], NT) if nt else _dot(a_v, w_ref[:, cols], NN)]
        elif nt:
            nl = w_ref.shape[2]
            terms = [_dot(a_v[:, k * nl:(k + 1) * nl], w_ref[k, cols, :], NT) for k in range(N_DEV)]
        else:
            terms = [_dot(a_v, w_ref[j], NN)]
        for t in terms:
            acc = t if acc is None else acc + t
    return cols, acc


def _mm_rows(name, pairs, n_out, *, nt, tm, nc, epi="plain", extra=None, out_dtype=F32, after=None, shards=False,
             norm_g=None):
    M = pairs[0][0].shape[0]
    np_ = len(pairs)
    after_args, after_specs = _after_operand(after)

    def body(*refs):
        a_refs = refs[:np_]
        w_refs = refs[np_:2 * np_]
        pos = 2 * np_
        e_ref = None
        if extra is not None:
            e_ref = refs[pos]
            pos += 1
        if norm_g is not None:
            g_ref = refs[pos]
            pos += 1
        pos += len(after_args)
        outs = refs[pos:]
        a_vals = [_bf(a[...]) for a in a_refs]
        for j in range(n_out // nc):
            cols, acc = _chunk_product(a_vals, w_refs, j, nc, nt, shards)
            if epi == "plain":
                outs[0][:, cols] = acc.astype(out_dtype)
            elif epi == "res":
                outs[0][:, cols] = e_ref[:, cols] + acc
            elif epi == "relu2":
                outs[0][:, cols] = _bf(acc)
                rl = jnp.maximum(acc, 0.0)
                outs[1][:, cols] = _bf(rl * rl)
            elif epi == "mask2relu":
                outs[0][:, cols] = _bf(acc * (2.0 * jnp.maximum(e_ref[:, cols].astype(F32), 0.0)))
        if norm_g is not None:
            hv = outs[0][...]
            r = lax.rsqrt(jnp.mean(hv * hv, axis=-1, keepdims=True) + EPS)
            outs[1][...] = _bf(hv * r * g_ref[...])

    in_specs = [pl.BlockSpec((tm, a.shape[1]), lambda i: (i, 0)) for a, _, _, _ in pairs]
    for _, _, wshape, widx in pairs:
        in_specs.append(pl.BlockSpec(wshape, functools.partial(lambda i, widx: widx, widx=widx)))
    args = [a for a, _, _, _ in pairs] + [w for _, w, _, _ in pairs]
    if extra is not None:
        in_specs.append(pl.BlockSpec((tm, n_out), lambda i: (i, 0)))
        args.append(extra)
    if norm_g is not None:
        in_specs.append(pl.BlockSpec((1, n_out), lambda i: (0, 0)))
        args.append(norm_g)
    in_specs += after_specs
    args += after_args
    row_out = pl.BlockSpec((tm, n_out), lambda i: (i, 0))
    if epi == "relu2":
        out_specs = [row_out, row_out]
        out_shape = [jax.ShapeDtypeStruct((M, n_out), BF16), jax.ShapeDtypeStruct((M, n_out), BF16)]
    elif norm_g is not None:
        out_specs = [row_out, row_out]
        out_shape = [jax.ShapeDtypeStruct((M, n_out), F32), jax.ShapeDtypeStruct((M, n_out), BF16)]
    else:
        dt = BF16 if epi == "mask2relu" else (F32 if epi == "res" else out_dtype)
        out_specs = row_out
        out_shape = jax.ShapeDtypeStruct((M, n_out), dt)
    return pl.pallas_call(
        body, name=name, grid=(M // tm,), in_specs=in_specs, out_specs=out_specs, out_shape=out_shape,
        compiler_params=_params(("parallel",)),
    )(*args)


def _whole(w):
    return w.shape, (0,) * w.ndim


def _mm_nn(name, a, w, **kw):
    n_out = w.shape[0] * w.shape[2] if w.ndim == 3 else w.shape[1]
    return _mm_rows(name, [(a, w, *_whole(w))], n_out, nt=False, **kw)


def _mm_nt(name, a, w, **kw):
    return _mm_rows(name, [(a, w, *_whole(w))], w.shape[0], nt=True, **kw)


def _mm_nt_rms_bwd(name, pairs, x, g, dres, *, tm, nc, after=None, shards=False, emit_bf16=True):
    M, D = x.shape
    np_ = len(pairs)
    n_out = 3 if emit_bf16 else 2
    after_args, after_specs = _after_operand(after)

    def body(*refs):
        a_refs = refs[:np_]
        w_refs = refs[np_:2 * np_]
        x_ref, g_ref, r_ref = refs[2 * np_:2 * np_ + 3]
        dy_sc = refs[-1]
        outs = refs[-1 - n_out:-1]
        dx_ref, dg_ref = outs[0], outs[1]
        i = pl.program_id(0)
        a_vals = [_bf(a[...]) for a in a_refs]
        for j in range(D // nc):
            cols, acc = _chunk_product(a_vals, w_refs, j, nc, True, shards)
            dy_sc[:, cols] = acc
        xv = x_ref[...]
        r = lax.rsqrt(jnp.mean(xv * xv, axis=-1, keepdims=True) + EPS)
        xh = xv * r
        dy_v = dy_sc[...]
        dyg = dy_v * g_ref[...]
        c = jnp.mean(dyg * xh, axis=-1, keepdims=True)
        dx = r_ref[...] + r * (dyg - xh * c)
        dx_ref[...] = dx
        if emit_bf16:
            outs[2][...] = _bf(dx)
        part = jnp.sum(dy_v * xh, axis=0, keepdims=True)

        @pl.when(i == 0)
        def _():
            dg_ref[...] = part

        @pl.when(i > 0)
        def _():
            dg_ref[...] += part

    row = pl.BlockSpec((tm, D), lambda i: (i, 0))
    vec = pl.BlockSpec((1, D), lambda i: (0, 0))
    in_specs = [pl.BlockSpec((tm, a.shape[1]), lambda i: (i, 0)) for a, _, _, _ in pairs]
    for _, _, wshape, widx in pairs:
        in_specs.append(pl.BlockSpec(wshape, functools.partial(lambda i, widx: widx, widx=widx)))
    args = [a for a, _, _, _ in pairs] + [w for _, w, _, _ in pairs]
    return pl.pallas_call(
        body, name=name, grid=(M // tm,),
        in_specs=in_specs + [row, vec, row] + after_specs,
        out_specs=[row, vec] + [row] * (n_out - 2),
        out_shape=[jax.ShapeDtypeStruct((M, D), F32), jax.ShapeDtypeStruct((1, D), F32)]
        + [jax.ShapeDtypeStruct((M, D), BF16)] * (n_out - 2),
        scratch_shapes=[pltpu.VMEM((tm, D), F32)],
        compiler_params=_params(("arbitrary",)),
    )(*args, x, g, dres, *after_args)


def _mm_tn(name, a, b, *, t1, tn, tm=2048, slot_cols=None):
    M, K1 = a.shape
    N = b.shape[1]
    nm = M // tm

    def body(a_ref, b_ref, o_ref, acc_ref):
        m = pl.program_id(2)
        t = _dot(_bf(a_ref[...]), _bf(b_ref[...]), TN)

        @pl.when(m == 0)
        def _():
            acc_ref[...] = t

        @pl.when(m > 0)
        def _():
            acc_ref[...] += t

        @pl.when(m == nm - 1)
        def _():
            if slot_cols is None:
                o_ref[...] = _bf(acc_ref[...])
            else:
                for k in range(tn // slot_cols):
                    o_ref[k] = _bf(acc_ref[:, k * slot_cols:(k + 1) * slot_cols])

    if slot_cols is not None:
        out_spec = pl.BlockSpec((tn // slot_cols, t1, slot_cols), lambda i, j, m: (j, i, 0))
        out_shape = jax.ShapeDtypeStruct((N // slot_cols, K1, slot_cols), BF16)
    else:
        out_spec = pl.BlockSpec((t1, tn), lambda i, j, m: (i, j))
        out_shape = jax.ShapeDtypeStruct((K1, N), BF16)
    return pl.pallas_call(
        body, name=name, grid=(K1 // t1, N // tn, nm),
        in_specs=[pl.BlockSpec((tm, t1), lambda i, j, m: (m, i)), pl.BlockSpec((tm, tn), lambda i, j, m: (m, j))],
        out_specs=out_spec, out_shape=out_shape,
        scratch_shapes=[pltpu.VMEM((t1, tn), F32)],
        compiler_params=_params(("parallel", "parallel", "arbitrary")),
    )(a, b)


_INV_SQRT2 = 1.0 / math.sqrt(2.0)
_INV_SQRT2PI = 1.0 / math.sqrt(2.0 * math.pi)


def _gelu(x):
    return 0.5 * x * (1.0 + lax.erf(x * _INV_SQRT2))


def _gelu_and_grad(x):
    cdf = 0.5 * (1.0 + lax.erf(x * _INV_SQRT2))
    return x * cdf, cdf + x * (_INV_SQRT2PI * jnp.exp(-0.5 * x * x))


def _layer_norm_parts(v):
    mu = jnp.mean(v, axis=-1, keepdims=True)
    xc = v - mu
    rs = lax.rsqrt(jnp.mean(xc * xc, axis=-1, keepdims=True) + EPS)
    return xc * rs, rs


def _gate_fwd(name, uvp, ln_g, ln_b, wm, bs_full, tr=512):
    S, W2 = uvp.shape
    W = W2 // 2
    gd = W // GROUPS

    def body(u_ref, v_ref, lg_ref, lb_ref, wm_ref, bs_ref, z_ref):
        vh, _ = _layer_norm_parts(_gelu(v_ref[...]))
        vn = _bf(vh * lg_ref[...] + lb_ref[...])
        for ci in range(tr // CHUNK):
            rows = slice(ci * CHUNK, (ci + 1) * CHUNK)
            for g in range(GROUPS):
                cols = slice(g * gd, (g + 1) * gd)
                mixed = _dot(wm_ref[g], vn[rows, cols], NN) + bs_ref[:, cols]
                z_ref[rows, cols] = _bf(_gelu(u_ref[rows, cols]) * mixed)

    vec = pl.BlockSpec((1, W), lambda i: (0, 0))
    return pl.pallas_call(
        body, name=name, grid=(S // tr,),
        in_specs=[pl.BlockSpec((tr, W), lambda i: (i, 0)), pl.BlockSpec((tr, W), lambda i: (i, 1)), vec, vec,
                  pl.BlockSpec((GROUPS, CHUNK, CHUNK), lambda i: (0, 0, 0)),
                  pl.BlockSpec((CHUNK, W), lambda i: (0, 0))],
        out_specs=pl.BlockSpec((tr, W), lambda i: (i, 0)),
        out_shape=jax.ShapeDtypeStruct((S, W), BF16),
        compiler_params=_params(("parallel",)),
    )(uvp, uvp, ln_g, ln_b, wm, bs_full)


def _gate_bwd(name, uvp, dz, ln_g, ln_b, wm, bs_full, tr=256):
    S, W2 = uvp.shape
    W = W2 // 2
    gd = W // GROUPS
    n_steps = S // tr

    def body(u_ref, v_ref, dz_ref, lg_ref, lb_ref, wm_ref, bs_ref, duv_ref, dwm_ref, dmx_ref, dlg_ref, dlb_ref,
             dvn_ref):
        i = pl.program_id(0)
        v, dv_dvp = _gelu_and_grad(v_ref[...])
        vh, rs = _layer_norm_parts(v)
        lg = lg_ref[...]
        vn = _bf(vh * lg + lb_ref[...])

        @pl.when(i == 0)
        def _():
            dwm_ref[...] = jnp.zeros_like(dwm_ref)
            dmx_ref[...] = jnp.zeros_like(dmx_ref)
            dlg_ref[...] = jnp.zeros_like(dlg_ref)
            dlb_ref[...] = jnp.zeros_like(dlb_ref)

        for ci in range(tr // CHUNK):
            rows = slice(ci * CHUNK, (ci + 1) * CHUNK)
            for g in range(GROUPS):
                cols = slice(g * gd, (g + 1) * gd)
                u, du_dup = _gelu_and_grad(u_ref[rows, cols])
                dz_v = dz_ref[rows, cols]
                dmixed = dz_v * u
                dmx_ref[:, cols] += dmixed
                dmixed_b = _bf(dmixed)
                mixed = _dot(wm_ref[g], vn[rows, cols], NN) + bs_ref[:, cols]
                duv_ref[rows, cols] = _bf(dz_v * mixed * du_dup)
                dwm_ref[g] += _dot(dmixed_b, vn[rows, cols], NT)
                dvn_ref[rows, cols] = _dot(wm_ref[g], dmixed_b, TN)
        dvn = dvn_ref[...]
        dlg_ref[...] += jnp.sum(dvn * vh, axis=0, keepdims=True)
        dlb_ref[...] += jnp.sum(dvn, axis=0, keepdims=True)
        dvh = dvn * lg
        dv = rs * (dvh - jnp.mean(dvh, axis=-1, keepdims=True) - vh * jnp.mean(dvh * vh, axis=-1, keepdims=True))
        duv_ref[:, W:] = _bf(dv * dv_dvp)

        @pl.when(i == n_steps - 1)
        def _():
            t_idx = lax.broadcasted_iota(jnp.int32, (CHUNK, CHUNK), 0)
            s_idx = lax.broadcasted_iota(jnp.int32, (CHUNK, CHUNK), 1)
            keep = (s_idx <= t_idx).astype(F32)
            for g in range(GROUPS):
                dwm_ref[g] = dwm_ref[g] * keep

    vec = pl.BlockSpec((1, W), lambda i: (0, 0))
    row = pl.BlockSpec((tr, W), lambda i: (i, 0))
    return pl.pallas_call(
        body, name=name, grid=(n_steps,),
        in_specs=[row, pl.BlockSpec((tr, W), lambda i: (i, 1)), row, vec, vec,
                  pl.BlockSpec((GROUPS, CHUNK, CHUNK), lambda i: (0, 0, 0)),
                  pl.BlockSpec((CHUNK, W), lambda i: (0, 0))],
        out_specs=[pl.BlockSpec((tr, W2), lambda i: (i, 0)),
                   pl.BlockSpec((GROUPS, CHUNK, CHUNK), lambda i: (0, 0, 0)),
                   pl.BlockSpec((CHUNK, W), lambda i: (0, 0)), vec, vec],
        out_shape=[jax.ShapeDtypeStruct((S, W2), BF16), jax.ShapeDtypeStruct((GROUPS, CHUNK, CHUNK), F32),
                   jax.ShapeDtypeStruct((CHUNK, W), F32), jax.ShapeDtypeStruct((1, W), F32),
                   jax.ShapeDtypeStruct((1, W), F32)],
        scratch_shapes=[pltpu.VMEM((tr, W), F32)],
        compiler_params=_params(("arbitrary",)),
    )(uvp, uvp, dz, ln_g, ln_b, wm, bs_full)


def _t5_bucket(distance):
    small = distance < MAX_EXACT
    nf = jnp.maximum(distance, 1).astype(F32)
    large = MAX_EXACT + (jnp.log(nf / MAX_EXACT) / math.log(REL_MAX_DISTANCE / MAX_EXACT)
                         * (N_BUCKETS - MAX_EXACT)).astype(jnp.int32)
    large = jnp.minimum(large, N_BUCKETS - 1)
    return jnp.where(small, distance, large)


TILE_ELEMS = 2 * CHUNK * CHUNK


def _band_buckets():
    rel = CHUNK + jnp.arange(CHUNK)[None, :] - jnp.arange(2 * CHUNK)[:, None]
    band = (rel >= 0) & (rel <= CHUNK)
    buckets = [_t5_bucket(jnp.clip(rel, 0, CHUNK) * d) for d in DILATIONS]
    return jnp.stack(buckets), band


def _bucket_onehot():
    buckets, _ = _band_buckets()
    return (buckets.reshape(N_DIL, 1, TILE_ELEMS) == jnp.arange(N_BUCKETS)[None, :, None]).astype(F32)


def _bias_tiles(name, rel_bias, after=None):
    _, band = _band_buckets()
    own = band & (jnp.arange(2 * CHUNK) >= CHUNK)[:, None]
    masks = jnp.stack([own, band]).reshape(2, TILE_ELEMS).astype(F32)
    tables = jnp.transpose(rel_bias.reshape(N_BUCKETS, N_DIL, ATT_HEADS), (1, 2, 0))
    after_args, after_specs = _after_operand(after)

    def body(t_ref, oh_ref, m_ref, *rest):
        out_ref = rest[-1]
        for g in range(N_DIL):
            bias = lax.dot_general(t_ref[g], oh_ref[g], (NN, ((), ())), precision=lax.Precision.HIGHEST,
                                   preferred_element_type=F32)
            for f in range(2):
                out_ref[g, f] = jnp.where(m_ref[f:f + 1, :] > 0.5, bias, NEG_INF)

    whole = pl.BlockSpec(memory_space=pltpu.VMEM)
    out = pl.pallas_call(
        body, name=name, out_shape=jax.ShapeDtypeStruct((N_DIL, 2, ATT_HEADS, TILE_ELEMS), F32),
        in_specs=[whole, whole, whole] + after_specs, out_specs=whole,
        compiler_params=_params(),
    )(tables, _bucket_onehot(), masks, *after_args)
    out = out.reshape(N_DIL, 2, ATT_HEADS // 2, 2, 2 * CHUNK, CHUNK)
    return jnp.transpose(out, (0, 1, 2, 4, 3, 5)).reshape(N_DIL, 2, ATT_HEADS // 2, 2 * CHUNK, 2 * CHUNK)


def _att_specs(order):
    def column(part, ids):
        hp, g, _ = order(*ids)
        return part * 3 * 4 + g * 4 + hp

    def window(part):
        def index(*ids):
            c = order(*ids)[2]
            return pl.multiple_of(jnp.maximum(c - 1, 0) * ATT_ROWS, ATT_ROWS), column(part, ids) * LANES
        return pl.BlockSpec((pl.Element(2 * ATT_ROWS), pl.Element(LANES)), index)

    return [pl.BlockSpec((ATT_ROWS, LANES), lambda *ids: (order(*ids)[2], column(0, ids))), window(1), window(2)]


def _window_base(c):
    return jnp.where(c == 0, 0, ATT_ROWS)


def _rows(start, d):
    if d == 1:
        return pl.ds(pl.multiple_of(start, CHUNK), CHUNK)
    return pl.ds(start, CHUNK, stride=d)


def _att_tile_offsets(t, d):
    n = t // d
    r = t % d
    return n * (CHUNK * d) + r, n


def _head_pair_columns(x_t):
    zeros = jnp.zeros((HEAD_DIM, CHUNK), x_t.dtype)
    return jnp.concatenate([jnp.concatenate([x_t[:HEAD_DIM], zeros], axis=0),
                            jnp.concatenate([zeros, x_t[HEAD_DIM:]], axis=0)], axis=1)


def _head_pair_rows(y):
    return jnp.concatenate([y[:HEAD_DIM, :CHUNK], y[HEAD_DIM:, CHUNK:]], axis=0)


def _att_fwd(name, qkv, bias_tiles):
    S = qkv.shape[0]
    n_chunks = S // ATT_ROWS
    tiles = ATT_ROWS // CHUNK

    def body(q_ref, kk, vv, b_ref, o_ref, l_ref):
        c = pl.program_id(1)
        g = pl.program_id(2)
        base = _window_base(c)

        for gi, d in enumerate(DILATIONS):
            @pl.when(g == gi)
            def _(d=d):
                span = CHUNK * d

                def tile(t, carry):
                    q0, n = _att_tile_offsets(t, d)
                    first = (c == 0) & (n == 0)
                    rows = _rows(q0, d)
                    cur = _rows(base + q0, d)
                    prev = _rows(jnp.where(first, q0, base + q0 - span), d)
                    inner = jnp.where(first, 0, 1)
                    qq = _head_pair_columns(_bf(q_ref[rows, :] * ATT_SCALE).T)
                    s_p = _dot(_bf(kk[prev, :]), qq, NN) + b_ref[inner, 0:CHUNK, :]
                    s_c = _dot(_bf(kk[cur, :]), qq, NN) + b_ref[inner, CHUNK:2 * CHUNK, :]
                    m = jnp.maximum(jnp.max(s_p, axis=0, keepdims=True), jnp.max(s_c, axis=0, keepdims=True))
                    p_p = jnp.exp(s_p - m)
                    p_c = jnp.exp(s_c - m)
                    l = jnp.sum(p_p, axis=0, keepdims=True) + jnp.sum(p_c, axis=0, keepdims=True)
                    o2 = (_dot(_bf(vv[prev, :]).T, _bf(p_p), NN)
                          + _dot(_bf(vv[cur, :]).T, _bf(p_c), NN)) * (1.0 / l)
                    lse = m + jnp.log(l)
                    l_t = jnp.concatenate([jnp.broadcast_to(lse[:, :CHUNK], (HEAD_DIM, CHUNK)),
                                           jnp.broadcast_to(lse[:, CHUNK:], (HEAD_DIM, CHUNK))], axis=0)
                    o_ref[rows, :] = _head_pair_rows(o2).T
                    l_ref[rows, :] = l_t.T
                    return carry

                lax.fori_loop(0, tiles, tile, 0, unroll=16)

    order = lambda hp, c, g: (hp, g, c)
    out_spec = pl.BlockSpec((None, ATT_ROWS, LANES), lambda hp, c, g: (g, c, hp))
    shape = jax.ShapeDtypeStruct((N_DIL, S, ATT_WIDTH), F32)
    return pl.pallas_call(
        body, name=name, grid=(ATT_HEADS // 2, n_chunks, N_DIL),
        in_specs=_att_specs(order) + [
            pl.BlockSpec((None, 2, None, 2 * CHUNK, 2 * CHUNK), lambda hp, c, g: (g, 0, hp, 0, 0))],
        out_specs=[out_spec, out_spec],
        out_shape=[shape, shape],
        compiler_params=_params(("parallel", "parallel", "parallel")),
    )(qkv, qkv, qkv, bias_tiles)


def _att_merge(name, o_g, l_g, tm=512):
    _, S, W = o_g.shape

    def body(o_ref, l_ref, out_ref, lse_ref):
        ls = [l_ref[g] for g in range(N_DIL)]
        mx = functools.reduce(jnp.maximum, ls)
        ws = [jnp.exp(l - mx) for l in ls]
        tot = functools.reduce(lambda a, b: a + b, ws)
        acc = ws[0] * o_ref[0]
        for g in range(1, N_DIL):
            acc = acc + ws[g] * o_ref[g]
        out_ref[...] = acc / tot
        lse_ref[...] = mx + jnp.log(tot)

    blk = pl.BlockSpec((N_DIL, tm, W), lambda i: (0, i, 0))
    row = pl.BlockSpec((tm, W), lambda i: (i, 0))
    shape = jax.ShapeDtypeStruct((S, W), F32)
    return pl.pallas_call(
        body, name=name, grid=(S // tm,), in_specs=[blk, blk], out_specs=[row, row], out_shape=[shape, shape],
        compiler_params=_params(("parallel",)),
    )(o_g, l_g)


def _att_bwd(name, qkv, o, lse, d_o, bias_tiles):
    S = qkv.shape[0]
    n_chunks = S // ATT_ROWS
    tiles = ATT_ROWS // CHUNK

    def body(q_ref, kk, vv, o_ref, l_ref, do_ref, b_ref, dq_out, dk_out, dv_out, ds_ref, dq_ref, dk_ref, dv_ref):
        g = pl.program_id(1)
        c = pl.program_id(2)

        @pl.when(c == 0)
        def _():
            dk_ref[...] = jnp.zeros_like(dk_ref)
            dv_ref[...] = jnp.zeros_like(dv_ref)
            ds_ref[...] = jnp.zeros_like(ds_ref)

        base = _window_base(c)
        first_row = c * ATT_ROWS
        head0 = lax.broadcasted_iota(jnp.int32, (CHUNK, LANES), 1) < HEAD_DIM

        def head_pair_stack(x):
            zero = jnp.zeros_like(x)
            return jnp.concatenate([jnp.where(head0, x, zero), jnp.where(head0, zero, x)], axis=0)

        for gi, d in enumerate(DILATIONS):
            @pl.when(g == gi)
            def _(d=d):
                span = CHUNK * d

                def tile(t, carry):
                    q0, n = _att_tile_offsets(t, d)
                    first = (c == 0) & (n == 0)
                    rows = _rows(q0, d)
                    cur = _rows(base + q0, d)
                    prev = _rows(jnp.where(first, q0, base + q0 - span), d)
                    inner = jnp.where(first, 0, 1)
                    g_cur = _rows(first_row + q0, d)
                    g_prev = _rows(jnp.where(first, q0, first_row + q0 - span), d)
                    q2 = _bf(q_ref[rows, :] * ATT_SCALE)
                    q_t = q2.T
                    k2 = _bf(jnp.concatenate([kk[prev, :], kk[cur, :]], axis=0))
                    k_t = k2.T
                    v2 = _bf(jnp.concatenate([vv[prev, :], vv[cur, :]], axis=0))
                    do2 = do_ref[rows, :]
                    do_b = _bf(do2)
                    do_t = do_b.T
                    lse_t = l_ref[rows, :].T
                    dd_t = (do2 * o_ref[rows, :]).T
                    lse = jnp.concatenate([lse_t[0:1], lse_t[HEAD_DIM:HEAD_DIM + 1]], axis=1)
                    delta = jnp.concatenate([jnp.sum(dd_t[:HEAD_DIM], axis=0, keepdims=True),
                                             jnp.sum(dd_t[HEAD_DIM:], axis=0, keepdims=True)], axis=1)
                    s = _dot(k2, _head_pair_columns(q_t), NN) + b_ref[inner]
                    p = jnp.exp(s - lse)
                    ds = p * (_dot(v2, _head_pair_columns(do_t), NN) - delta)
                    ds_ref[...] += ds
                    ds_b = _bf(ds)
                    dq_t = _head_pair_rows(_dot(k_t, ds_b, NN))
                    dk2 = _dot(ds_b, head_pair_stack(q2), NN)
                    dv2 = _dot(_bf(p), head_pair_stack(do_b), NN)
                    dq_ref[rows, :] = (dq_t * ATT_SCALE).T
                    dk_ref[g_prev, :] += dk2[0:CHUNK]
                    dk_ref[g_cur, :] += dk2[CHUNK:2 * CHUNK]
                    dv_ref[g_prev, :] += dv2[0:CHUNK]
                    dv_ref[g_cur, :] += dv2[CHUNK:2 * CHUNK]
                    return carry

                lax.fori_loop(0, tiles, tile, 0, unroll=16)

        dq_out[...] = _bf(dq_ref[...])

        @pl.when(c == n_chunks - 1)
        def _():
            dk_out[...] = _bf(dk_ref[...])
            dv_out[...] = _bf(dv_ref[...])

    order = lambda hp, g, c: (hp, g, c)
    chunk = pl.BlockSpec((ATT_ROWS, LANES), lambda hp, g, c: (c, hp))
    slab = pl.BlockSpec((S, LANES), lambda hp, g, c: (0, g * 4 + hp))
    width = N_DIL * ATT_WIDTH
    dq, dk, dv, ds_sums = pl.pallas_call(
        body, name=name, grid=(ATT_HEADS // 2, N_DIL, n_chunks),
        in_specs=_att_specs(order) + [chunk, chunk, chunk,
                                      pl.BlockSpec((None, 2, None, 2 * CHUNK, 2 * CHUNK),
                                                   lambda hp, g, c: (g, 0, hp, 0, 0))],
        out_specs=[pl.BlockSpec((ATT_ROWS, LANES), lambda hp, g, c: (c, g * 4 + hp)), slab, slab,
                   pl.BlockSpec((None, None, 2 * CHUNK, 2 * CHUNK), lambda hp, g, c: (g, hp, 0, 0))],
        out_shape=[jax.ShapeDtypeStruct((S, width), BF16), jax.ShapeDtypeStruct((S, width), BF16),
                   jax.ShapeDtypeStruct((S, width), BF16),
                   jax.ShapeDtypeStruct((N_DIL, ATT_HEADS // 2, 2 * CHUNK, 2 * CHUNK), F32)],
        scratch_shapes=[pltpu.VMEM((ATT_ROWS, LANES), F32), pltpu.VMEM((S, LANES), F32),
                        pltpu.VMEM((S, LANES), F32)],
        compiler_params=_params(("parallel", "parallel", "arbitrary")),
    )(qkv, qkv, qkv, o, lse, d_o, bias_tiles)
    ds_sums = ds_sums.reshape(N_DIL, ATT_HEADS // 2, 2 * CHUNK, 2, CHUNK)
    ds_sums = jnp.transpose(ds_sums, (0, 1, 3, 2, 4)).reshape(N_DIL, ATT_HEADS, 2 * CHUNK, CHUNK)
    return dq, dk, dv, ds_sums


def _bias_grad(name, ds_sums):
    flat = ds_sums.reshape(N_DIL, ATT_HEADS, TILE_ELEMS)

    def body(oh_ref, ds_ref, out_ref):
        for g in range(N_DIL):
            out_ref[g] = lax.dot_general(oh_ref[g], ds_ref[g], (NT, ((), ())), precision=lax.Precision.HIGHEST,
                                         preferred_element_type=F32)

    out = pl.pallas_call(
        body, name=name, out_shape=jax.ShapeDtypeStruct((N_DIL, N_BUCKETS, ATT_HEADS), F32),
        compiler_params=_params(),
    )(_bucket_onehot(), flat)
    return jnp.transpose(out, (1, 0, 2)).reshape(N_BUCKETS, N_DIL * ATT_HEADS)


def _peers():
    x, y, c = lax.axis_index("x"), lax.axis_index("y"), lax.axis_index("c")
    me = 4 * x + 2 * y + c
    others = [(x, y, 1 - c), (1 - x, y, c), (x, 1 - y, c), (1 - x, 1 - y, c),
              (1 - x, y, 1 - c), (x, 1 - y, 1 - c), (1 - x, 1 - y, 1 - c)]
    return me, others


def _slot(dev):
    return 4 * dev[0] + 2 * dev[1] + dev[2]


_HBM =pl.BlockSpec(memory_space=pltpu.HBM)
_SEM = pl.BlockSpec(memory_space=pltpu.SEMAPHORE)
_EFFECT = pltpu.SideEffectType.DATAFLOW_SIDE_EFFECTING


def _my_slot():
    return 4 * lax.axis_index("x") + 2 * lax.axis_index("y") + lax.axis_index("c")


def _exchange_copy(src_ref, land_ref, send_sems, recv_sems, k, dev, me, scatter, arriving):
    src = src_ref.at[me if arriving else _slot(dev)] if scatter else src_ref
    dst = land_ref.at[_slot(dev) if arriving else me]
    return pltpu.make_async_remote_copy(src_ref=src, dst_ref=dst, send_sem=send_sems.at[k], recv_sem=recv_sems.at[k],
                                        device_id=dev, device_id_type=MESH)


def _exchange_start(name, srcs, scatter):
    n = len(srcs)
    me = _my_slot()
    landings = []
    for src in srcs:
        own = lax.dynamic_index_in_dim(src, me, 0, keepdims=True) if scatter else src[None]
        landings.append(lax.dynamic_update_slice(lax.empty((N_DEV,) + src.shape[-2:], src.dtype), own, (me, 0, 0)))

    def body(*refs):
        src_refs, land_refs = refs[:n], refs[n:2 * n]
        send_sems, recv_sems = refs[2 * n:2 * n + 2]
        token = refs[-1]
        me, others = _peers()
        for p in range(n):
            for k, dev in enumerate(others):
                _exchange_copy(src_refs[p], land_refs[p], send_sems, recv_sems, p * (N_DEV - 1) + k, dev, me,
                               scatter, False).start()
        token[...] = jnp.zeros_like(token)

    sems = pltpu.SemaphoreType.DMA((n * (N_DEV - 1),))
    hbm = lambda a: pltpu.with_memory_space_constraint(a, pltpu.HBM)
    outs = pl.pallas_call(
        body, name=name,
        out_shape=(sems, sems, *[pltpu.HBM(a.shape, a.dtype) for a in srcs + landings],
                   jax.ShapeDtypeStruct((8, LANES), F32)),
        in_specs=(_HBM,) * (2 * n), out_specs=(_SEM, _SEM) + (_HBM,) * (2 * n) + (pl.BlockSpec(memory_space=pltpu.VMEM),),
        input_output_aliases={i: 2 + i for i in range(2 * n)},
        compiler_params=pltpu.CompilerParams(has_side_effects=_EFFECT),
    )(*[hbm(a) for a in srcs + landings])
    return (outs[0], outs[1], list(outs[2:2 + n]), list(outs[2 + n:2 + 2 * n]), scatter), outs[-1]


def _exchange_wait(name, handle, after):
    send_sems, recv_sems, src_thru, land_thru, scatter = handle
    n = len(src_thru)

    def body(*refs):
        src_refs, land_refs = refs[:n], refs[n:2 * n]
        send_sems, recv_sems = refs[2 * n:2 * n + 2]
        me, others = _peers()
        for p in range(n):
            for k, dev in enumerate(others):
                cp = _exchange_copy(src_refs[p], land_refs[p], send_sems, recv_sems, p * (N_DEV - 1) + k, dev, me,
                                    scatter, True)
                cp.wait_send()
                cp.wait_recv()

    outs = pl.pallas_call(
        body, name=name,
        out_shape=tuple(pltpu.HBM(a.shape, a.dtype) for a in src_thru + land_thru),
        in_specs=(_HBM,) * (2 * n) + (_SEM, _SEM, pl.BlockSpec(memory_space=pl.ANY)), out_specs=(_HBM,) * (2 * n),
        input_output_aliases={i: i for i in range(2 * n)},
        compiler_params=pltpu.CompilerParams(has_side_effects=_EFFECT),
    )(*src_thru, *land_thru, send_sems, recv_sems, after)
    return list(outs[n:])


def _adamw_math(w, g, m, v):
    m = ADAM_B1 * m + (1.0 - ADAM_B1) * g
    v = ADAM_B2 * v + (1.0 - ADAM_B2) * (g * g)
    m_hat = m / (1.0 - ADAM_B1 ** ADAM_STEP)
    v_hat = v / (1.0 - ADAM_B2 ** ADAM_STEP)
    delta = -ADAM_LR * (m_hat / (jnp.sqrt(v_hat) + ADAM_EPS) + ADAM_WD * w)
    return delta, m, v


_SMALL_WIDE = (("mix_norm_g", 2), ("mlp_norm_g", 2), ("final_norm_g", 1), ("a_ln_g", 1), ("a_ln_b", 1))
_SMALL_NARROW = (("a_w_s", GROUPS * CHUNK), ("a_b_s", GROUPS), ("rel_bias", N_BUCKETS))
_SMALL = tuple(n for n, _ in _SMALL_WIDE + _SMALL_NARROW)
_BIAS_COLS = N_DIL * ATT_HEADS


def _pack_small_grads(grads, loss_term):
    D = grads["a_ln_g"].shape[-1]
    tiles = [jnp.pad(grads[n].reshape(k, D), ((0, 8 - k), (0, 0))) for n, k in _SMALL_WIDE]
    tiles.append(jnp.pad(loss_term.reshape(1, 1), ((0, 7), (0, D - 1))))
    narrow = [grads["a_w_s"].reshape(-1, LANES), grads["a_b_s"].reshape(-1, LANES),
              jnp.pad(grads["rel_bias"], ((0, 0), (0, LANES - _BIAS_COLS)))]
    return jnp.concatenate(tiles, axis=0), jnp.concatenate(narrow, axis=0)


def _adamw_small(name, g_wide, g_narrow, w, m, v):
    D = g_wide.shape[-1]
    shapes = {n: (k, D) for n, k in _SMALL_WIDE}
    shapes.update({n: (k, LANES) for n, k in _SMALL_NARROW})
    shapes["rel_bias"] = (N_BUCKETS, _BIAS_COLS)
    n_t = len(_SMALL)

    def body(gw_ref, gn_ref, *rest):
        params = rest[:3 * n_t]
        outs = rest[3 * n_t:3 * n_t + 4 * n_t]
        loss_ref, sw, sn = rest[-3:]
        sw[...] = functools.reduce(lambda a, b: a + b, [gw_ref[j] for j in range(N_DEV)])
        sn[...] = functools.reduce(lambda a, b: a + b, [gn_ref[j] for j in range(N_DEV)])
        row = 0
        for i, n in enumerate(_SMALL):
            k, cols = shapes[n]
            if i < len(_SMALL_WIDE):
                g = sw[8 * i:8 * i + k, :]
            else:
                g = sn[row:row + k, 0:cols]
                row += k
            w_ref, m_ref, v_ref = params[3 * i:3 * i + 3]
            delta, m_new, v_new = _adamw_math(w_ref[...], g, m_ref[...], v_ref[...])
            for out, val in zip(outs[4 * i:4 * i + 4], (g, delta, m_new, v_new)):
                out[...] = val
        loss_ref[...] = sw[8 * len(_SMALL_WIDE):8 * len(_SMALL_WIDE) + 8, 0:LANES]

    whole = pl.BlockSpec(memory_space=pltpu.VMEM)
    args = [t[n].reshape(shapes[n]) for n in _SMALL for t in (w, m, v)]
    res = pl.pallas_call(
        body, name=name,
        in_specs=[whole] * (2 + len(args)), out_specs=[whole] * (4 * n_t + 1),
        out_shape=[jax.ShapeDtypeStruct(shapes[n], F32) for n in _SMALL for _ in range(4)]
        + [jax.ShapeDtypeStruct((8, LANES), F32)],
        scratch_shapes=[pltpu.VMEM(g_wide.shape[1:], F32), pltpu.VMEM(g_narrow.shape[1:], F32)],
        compiler_params=_params(),
    )(g_wide, g_narrow, *args)
    small = {n: tuple(r.reshape(w[n].shape) for r in res[4 * i:4 * i + 4]) for i, n in enumerate(_SMALL)}
    return small, res[-1][0, 0]


def _adamw_shard(name, parts, w, m, v, layer, earlier=None, after=None, tr=256):
    L, K, N = w.shape
    tr = min(tr, K)
    n_prev = 0 if earlier is None else 4
    after_args, after_specs = _after_operand(after)

    def body(p_ref, w_ref, m_ref, v_ref, *rest):
        g_out, d_out, m_out, v_out = rest[n_prev + len(after_args):]
        g = p_ref[0].astype(F32)
        for j in range(1, N_DEV):
            g = g + p_ref[j].astype(F32)
        delta, m_new, v_new = _adamw_math(w_ref[...], g, m_ref[...], v_ref[...])
        g_out[...] = g
        d_out[...] = delta
        m_out[...] = m_new
        v_out[...] = v_new

    row = pl.BlockSpec((None, tr, N), lambda i: (layer, i, 0))
    shape = jax.ShapeDtypeStruct((L, K, N), F32)
    return pl.pallas_call(
        body, name=name, grid=(K // tr,),
        in_specs=[pl.BlockSpec((N_DEV, tr, N), lambda i: (0, i, 0)), row, row, row]
        + [pl.BlockSpec(memory_space=pl.ANY)] * n_prev + after_specs,
        out_specs=[row, row, row, row],
        out_shape=[shape, shape, shape, shape],
        input_output_aliases={4 + j: j for j in range(n_prev)},
        compiler_params=_params(("parallel",)),
    )(parts, w, m, v, *(earlier or ()), *after_args)


def _column_slots(full):
    K, N = full.shape
    return jnp.transpose(full.reshape(K, N_DEV, N // N_DEV), (1, 0, 2))


def _from_column_slots(slots):
    _, K, n = slots.shape
    return jnp.transpose(slots, (1, 0, 2)).reshape(K, N_DEV * n)


_STAGES = (("gate", ("a_w_in", "a_w_out"), 0),
           ("mlp0", ("w_up", "w_down"), 0),
           ("att", ("b_w_qkv", "b_w_out"), 0),
           ("mlp1", ("w_up", "w_down"), 1))


def kernel(x, mix_norm_g, mlp_norm_g, final_norm_g, a_w_in, a_ln_g, a_ln_b, a_w_s, a_b_s, a_w_out, b_w_qkv, b_w_out, rel_bias, w_up, w_down, loss_target, m_mix_norm_g, m_mlp_norm_g, m_final_norm_g, m_a_w_in, m_a_ln_g, m_a_ln_b, m_a_w_s, m_a_b_s, m_a_w_out, m_b_w_qkv, m_b_w_out, m_rel_bias, m_w_up, m_w_down, v_mix_norm_g, v_mlp_norm_g, v_final_norm_g, v_a_w_in, v_a_ln_g, v_a_ln_b, v_a_w_s, v_a_b_s, v_a_w_out, v_b_w_qkv, v_b_w_out, v_rel_bias, v_w_up, v_w_down):
    w = dict(mix_norm_g=mix_norm_g, mlp_norm_g=mlp_norm_g, final_norm_g=final_norm_g, a_w_in=a_w_in, a_ln_g=a_ln_g,
             a_ln_b=a_ln_b, a_w_s=a_w_s, a_b_s=a_b_s, a_w_out=a_w_out, b_w_qkv=b_w_qkv, b_w_out=b_w_out,
             rel_bias=rel_bias, w_up=w_up, w_down=w_down)
    m = dict(mix_norm_g=m_mix_norm_g, mlp_norm_g=m_mlp_norm_g, final_norm_g=m_final_norm_g, a_w_in=m_a_w_in,
             a_ln_g=m_a_ln_g, a_ln_b=m_a_ln_b, a_w_s=m_a_w_s, a_b_s=m_a_b_s, a_w_out=m_a_w_out, b_w_qkv=m_b_w_qkv,
             b_w_out=m_b_w_out, rel_bias=m_rel_bias, w_up=m_w_up, w_down=m_w_down)
    v = dict(mix_norm_g=v_mix_norm_g, mlp_norm_g=v_mlp_norm_g, final_norm_g=v_final_norm_g, a_w_in=v_a_w_in,
             a_ln_g=v_a_ln_g, a_ln_b=v_a_ln_b, a_w_s=v_a_w_s, a_b_s=v_a_b_s, a_w_out=v_a_w_out, b_w_qkv=v_b_w_qkv,
             b_w_out=v_b_w_out, rel_bias=v_rel_bias, w_up=v_w_up, w_down=v_w_down)

    stages = {s: (names, layer) for s, names, layer in _STAGES}
    order = [s for s, _, _ in _STAGES]

    def shards_of(stage):
        names, layer = stages[stage]
        return [_bf(w[n][layer]) for n in names]

    pending = {}
    pending[order[0]], first_token = _exchange_start("gather_" + order[0] + "_start", shards_of(order[0]), False)

    def get_weights(stage, dep):
        gathered = _exchange_wait("gather_" + stage + "_wait", pending.pop(stage), dep)
        nxt = order.index(stage) + 1
        token = None
        if nxt < len(order):
            shards, gathered = lax.optimization_barrier((shards_of(order[nxt]), gathered))
            pending[order[nxt]], token = _exchange_start("gather_" + order[nxt] + "_start", shards, False)
        return gathered, token

    sent = {}

    def put_grads(stage, slot_grads):
        sent[stage], token = _exchange_start("scatter_" + stage + "_start", slot_grads, True)
        return token

    loss_local, grad_x, small_g = _local_step(
        x[0], loss_target[0], mix_norm_g, mlp_norm_g, final_norm_g, a_ln_g, a_ln_b, a_w_s, a_b_s, rel_bias,
        get_weights, put_grads, first_token)

    small_sent, token = _exchange_start("gather_small_start", list(_pack_small_grads(small_g, loss_local)), False)

    results = {}
    prev = token
    for stage in reversed(order):
        names, layer = stages[stage]
        received = _exchange_wait("scatter_" + stage + "_wait", sent[stage], prev)
        for n, parts in zip(names, received):
            results[n] = _adamw_shard("adamw_%s_%s" % (stage, n), parts, w[n], m[n], v[n], layer, results.get(n),
                                      after=prev)
            prev = results[n][0]

    g_wide, g_narrow = _exchange_wait("gather_small_wait", small_sent, prev)
    small, loss = _adamw_small("adamw_small", g_wide, g_narrow, w, m, v)

    outs = []
    for j in range(4):
        outs.extend(small[n][j] if n in _SMALL else results[n][j] for n in w)
    return (loss, grad_x[None], *outs)


def _local_step(xs, tgt, mix_norm_g, mlp_norm_g, final_norm_g, a_ln_g, a_ln_b, a_w_s, a_b_s, rel_bias,
                get_weights, put_grads, first_token=None):
    D = xs.shape[-1]
    g_mix = [mix_norm_g[l][None, :] for l in range(2)]
    g_mlp = [mlp_norm_g[l][None, :] for l in range(2)]
    g_fin = final_norm_g[None, :]
    ln_g, ln_b = a_ln_g, a_ln_b
    causal = jnp.tril(jnp.ones((CHUNK, CHUNK), dtype=bool))
    wm = _bf(jnp.where(causal[None], a_w_s[0], 0.0))
    bs_full = jnp.repeat(a_b_s[0].T, D // GROUPS, axis=1)
    bias_tiles = _bias_tiles("att_bias", rel_bias, after=first_token)

    (win, wout), token = get_weights("gate", bias_tiles)
    wout = wout.reshape(-1, D)
    y0 = _rms_fwd("rms_mix0", xs, g_mix[0], after=token)
    uvp = _mm_nn("gate_in", y0, win, tm=512, nc=win.shape[2], shards=True)
    z = _gate_fwd("gate_mid", uvp, ln_g, ln_b, wm, bs_full)
    h1, y1 = _mm_nn("gate_out", z, wout, tm=512, nc=512, epi="res", extra=xs, norm_g=g_mlp[0])
    (wup0, wdn0), token = get_weights("mlp0", h1)
    wdn0 = wdn0.reshape(-1, D)
    a0, f0 = _mm_nn("mlp0_up", y1, wup0, tm=512, nc=wup0.shape[2], epi="relu2", shards=True, after=token)
    h2, y2 = _mm_nn("mlp0_down", f0, wdn0, tm=512, nc=512, epi="res", extra=h1, norm_g=g_mix[1])
    (wqkv, wo), token = get_weights("att", h2)
    wqkv, wo = _from_column_slots(wqkv), _from_column_slots(wo)
    qkv = _mm_nn("att_qkv", y2, wqkv, tm=512, nc=512, after=token)
    o_att, lse = _att_merge("att_merge", *_att_fwd("att_fwd", qkv, bias_tiles))
    h3, y3 = _mm_nn("att_out", o_att, wo, tm=512, nc=512, epi="res", extra=h2, norm_g=g_mlp[1])
    (wup1, wdn1), _ = get_weights("mlp1", h3)
    wdn1 = wdn1.reshape(-1, D)
    a1, f1 = _mm_nn("mlp1_up", y3, wup1, tm=512, nc=wup1.shape[2], epi="relu2", shards=True)
    dh, dg_fin, err2, dh_b = _mm_res_loss("mlp1_down_loss", f1, wdn1, h3, g_fin, tgt, tm=512, nc=512)
    loss_local = 0.5 * jnp.sum(err2) / D

    def mlp_bwd(tag, dh, dh_b, h_in, y, a, f, wup_l, wdn_l, g_row, after):
        da = _mm_nt(tag + "_dact", dh_b, wdn_l, tm=512, nc=512, epi="mask2relu", extra=a, after=after)
        g_dn = _mm_tn(tag + "_dwdown", f, dh_b, t1=1024, tn=1024, tm=DW_TOKENS)
        g_up = _mm_tn(tag + "_dwup", y, da, t1=1024, tn=1024, tm=DW_TOKENS, slot_cols=wup_l.shape[2])
        dh_in, dg, dh_in_b = _mm_nt_rms_bwd(tag + "_dy", [(da, wup_l, *_whole(wup_l))], h_in, g_row, dh, tm=512,
                                            nc=512, shards=True)
        return dh_in, dh_in_b, dg, put_grads(tag, [g_up, g_dn.reshape(N_DEV, -1, D)])

    dh3, dh3_b, dg_mlp1, token = mlp_bwd("mlp1", dh, dh_b, h3, y3, a1, f1, wup1, wdn1, g_mlp[1], None)

    d_o = _mm_nt("att_dout", dh3_b, wo, tm=512, nc=512, after=token)
    g_wo = _mm_tn("att_dwo", o_att, dh3_b, t1=512, tn=1024, tm=DW_TOKENS)
    dq, dk, dv, ds_sums = _att_bwd("att_bwd", qkv, o_att, lse, d_o, bias_tiles)
    part_w = N_DIL * ATT_WIDTH
    g_qkv = [_mm_tn("att_dwqkv%d" % p, y2, t, t1=1024, tn=part_w) for p, t in enumerate((dq, dk, dv))]
    dh2, dg_mix1, dh2_b = _mm_nt_rms_bwd(
        "att_dy", [(t, wqkv, (D, part_w), (0, p)) for p, t in enumerate((dq, dk, dv))], h2, g_mix[1], dh3, tm=512,
        nc=512)
    token = put_grads("att", [_column_slots(jnp.concatenate(g_qkv, axis=1)), _column_slots(g_wo)])

    dh1, dh1_b, dg_mlp0, token = mlp_bwd("mlp0", dh2, dh2_b, h1, y1, a0, f0, wup0, wdn0, g_mlp[0], token)

    dz = _mm_nt("gate_dz", dh1_b, wout, tm=512, nc=512, after=token)
    g_wout = _mm_tn("gate_dwout", z, dh1_b, t1=1024, tn=1024, tm=DW_TOKENS)
    duvp, d_wm, d_mixed, d_lng, d_lnb = _gate_bwd("gate_dmid", uvp, dz, ln_g, ln_b, wm, bs_full)
    g_win = _mm_tn("gate_dwin", y0, duvp, t1=1024, tn=1024, tm=DW_TOKENS, slot_cols=win.shape[2])
    token = put_grads("gate", [g_win, g_wout.reshape(N_DEV, -1, D)])
    grad_x, dg_mix0 = _mm_nt_rms_bwd("gate_dy", [(duvp, win, *_whole(win))], xs, g_mix[0], dh1, tm=512, nc=512,
                                     after=token, shards=True, emit_bf16=False)

    small_g = dict(
        mix_norm_g=jnp.concatenate([dg_mix0, dg_mix1], axis=0),
        mlp_norm_g=jnp.concatenate([dg_mlp0, dg_mlp1], axis=0),
        final_norm_g=dg_fin[0], a_ln_g=d_lng, a_ln_b=d_lnb, a_w_s=d_wm[None],
        a_b_s=jnp.sum(d_mixed.reshape(CHUNK, GROUPS, D // GROUPS), axis=2).T[None],
        rel_bias=_bias_grad("att_dbias", ds_sums))
    return loss_local, grad_x, small_g
```

```python
import functools
import math

import jax
import jax.numpy as jnp
from jax import lax
from jax.experimental import pallas as pl
from jax.experimental.pallas import tpu as pltpu

F32 = jnp.float32
BF16 = jnp.bfloat16
MESH = pl.DeviceIdType.MESH

N_DEV = 8
EPS = 1e-6
NEG_INF = -1e30
CHUNK = 128
GROUPS = 8
HEAD_DIM = 64
ATT_HEADS = 8
ATT_WIDTH = ATT_HEADS * HEAD_DIM
DILATIONS = (1, 4, 16)
N_DIL = len(DILATIONS)
N_BUCKETS = 32
MAX_EXACT = N_BUCKETS // 2
REL_MAX_DISTANCE = 2048
ATT_ROWS = 2048
ATT_SCALE = HEAD_DIM ** -0.5
DW_TOKENS = 4096
LANES = 128

ADAM_LR = 0.001
ADAM_B1 = 0.9
ADAM_B2 = 0.999
ADAM_EPS = 1e-08
ADAM_WD = 0.01
ADAM_STEP = 10

VMEM_LIMIT_BYTES = 56 * 1024 * 1024


def _params(semantics=None):
    return pltpu.CompilerParams(dimension_semantics=semantics, vmem_limit_bytes=VMEM_LIMIT_BYTES)


def _bf(v):
    return v.astype(BF16)


def _dot(a, b, dims):
    return lax.dot_general(a, b, (dims, ((), ())), preferred_element_type=F32)


NN = ((1,), (0,))
NT = ((1,), (1,))
TN = ((0,), (0,))


def _after_operand(after):
    if after is None:
        return [], []
    return [after], [pl.BlockSpec(memory_space=pl.ANY)]


def _rms_fwd(name, x, g, tm=512, after=None):
    S, D = x.shape
    after_args, after_specs = _after_operand(after)

    def body(x_ref, g_ref, *rest):
        y_ref = rest[-1]
        xv = x_ref[...]
        r = lax.rsqrt(jnp.mean(xv * xv, axis=-1, keepdims=True) + EPS)
        y_ref[...] = _bf(xv * r * g_ref[...])

    return pl.pallas_call(
        body, name=name, grid=(S // tm,),
        in_specs=[pl.BlockSpec((tm, D), lambda i: (i, 0)), pl.BlockSpec((1, D), lambda i: (0, 0))] + after_specs,
        out_specs=pl.BlockSpec((tm, D), lambda i: (i, 0)),
        out_shape=jax.ShapeDtypeStruct((S, D), BF16),
        compiler_params=_params(("parallel",)),
    )(x, g, *after_args)


def _mm_res_loss(name, a, w, res, g, target, *, tm, nc):
    M, D = res.shape

    def body(a_ref, w_ref, r_ref, g_ref, t_ref, dh_ref, dg_ref, l_ref, dhb_ref, h_sc):
        i = pl.program_id(0)
        a_v = _bf(a_ref[...])
        for j in range(D // nc):
            cols, acc = _chunk_product([a_v], [w_ref], j, nc, False, False)
            h_sc[:, cols] = r_ref[:, cols] + acc
        xv = h_sc[...]
        r = lax.rsqrt(jnp.mean(xv * xv, axis=-1, keepdims=True) + EPS)
        xh = xv * r
        gv = g_ref[...]
        e = xh * gv - t_ref[...]
        dout = e / D
        dyg = dout * gv
        c = jnp.mean(dyg * xh, axis=-1, keepdims=True)
        dh = r * (dyg - xh * c)
        dh_ref[...] = dh
        dhb_ref[...] = _bf(dh)
        dg_part = jnp.sum(dout * xh, axis=0, keepdims=True)
        l_part = jnp.sum(e * e, axis=0, keepdims=True)

        @pl.when(i == 0)
        def _():
            dg_ref[...] = dg_part
            l_ref[...] = l_part

        @pl.when(i > 0)
        def _():
            dg_ref[...] += dg_part
            l_ref[...] += l_part

    row = pl.BlockSpec((tm, D), lambda i: (i, 0))
    vec = pl.BlockSpec((1, D), lambda i: (0, 0))
    return pl.pallas_call(
        body, name=name, grid=(M // tm,),
        in_specs=[pl.BlockSpec((tm, a.shape[1]), lambda i: (i, 0)), pl.BlockSpec(w.shape, lambda i: (0, 0)),
                  row, vec, row],
        out_specs=[row, vec, vec, row],
        out_shape=[jax.ShapeDtypeStruct((M, D), F32), jax.ShapeDtypeStruct((1, D), F32),
                   jax.ShapeDtypeStruct((1, D), F32), jax.ShapeDtypeStruct((M, D), BF16)],
        scratch_shapes=[pltpu.VMEM((tm, D), F32)],
        compiler_params=_params(("arbitrary",)),
    )(a, w, res, g, target)


def _chunk_product(a_vals, w_refs, j, nc, nt, shards):
    cols = slice(j * nc, (j + 1) * nc)
    acc = None
    for a_v, w_ref in zip(a_vals, w_refs):
        if not shards:
            terms = [_dot(a_v, w_ref[cols, :], NT) if nt else _dot(a_v, w_ref[:, cols], NN)]
        elif nt:
            nl = w_ref.shape[2]
            terms = [_dot(a_v[:, k * nl:(k + 1) * nl], w_ref[k, cols, :], NT) for k in range(N_DEV)]
        else:
            terms = [_dot(a_v, w_ref[j], NN)]
        for t in terms:
            acc = t if acc is None else acc + t
    return cols, acc


def _mm_rows(name, pairs, n_out, *, nt, tm, nc, epi="plain", extra=None, out_dtype=F32, after=None, shards=False,
             norm_g=None):
    M = pairs[0][0].shape[0]
    np_ = len(pairs)
    after_args, after_specs = _after_operand(after)

    def body(*refs):
        a_refs = refs[:np_]
        w_refs = refs[np_:2 * np_]
        pos = 2 * np_
        e_ref = None
        if extra is not None:
            e_ref = refs[pos]
            pos += 1
        if norm_g is not None:
            g_ref = refs[pos]
            pos += 1
        pos += len(after_args)
        outs = refs[pos:]
        a_vals = [_bf(a[...]) for a in a_refs]
        for j in range(n_out // nc):
            cols, acc = _chunk_product(a_vals, w_refs, j, nc, nt, shards)
            if epi == "plain":
                outs[0][:, cols] = acc.astype(out_dtype)
            elif epi == "res":
                outs[0][:, cols] = e_ref[:, cols] + acc
            elif epi == "relu2":
                outs[0][:, cols] = _bf(acc)
                rl = jnp.maximum(acc, 0.0)
                outs[1][:, cols] = _bf(rl * rl)
            elif epi == "mask2relu":
                outs[0][:, cols] = _bf(acc * (2.0 * jnp.maximum(e_ref[:, cols].astype(F32), 0.0)))
        if norm_g is not None:
            hv = outs[0][...]
            r = lax.rsqrt(jnp.mean(hv * hv, axis=-1, keepdims=True) + EPS)
            outs[1][...] = _bf(hv * r * g_ref[...])

    in_specs = [pl.BlockSpec((tm, a.shape[1]), lambda i: (i, 0)) for a, _, _, _ in pairs]
    for _, _, wshape, widx in pairs:
        in_specs.append(pl.BlockSpec(wshape, functools.partial(lambda i, widx: widx, widx=widx)))
    args = [a for a, _, _, _ in pairs] + [w for _, w, _, _ in pairs]
    if extra is not None:
        in_specs.append(pl.BlockSpec((tm, n_out), lambda i: (i, 0)))
        args.append(extra)
    if norm_g is not None:
        in_specs.append(pl.BlockSpec((1, n_out), lambda i: (0, 0)))
        args.append(norm_g)
    in_specs += after_specs
    args += after_args
    row_out = pl.BlockSpec((tm, n_out), lambda i: (i, 0))
    if epi == "relu2":
        out_specs = [row_out, row_out]
        out_shape = [jax.ShapeDtypeStruct((M, n_out), BF16), jax.ShapeDtypeStruct((M, n_out), BF16)]
    elif norm_g is not None:
        out_specs = [row_out, row_out]
        out_shape = [jax.ShapeDtypeStruct((M, n_out), F32), jax.ShapeDtypeStruct((M, n_out), BF16)]
    else:
        dt = BF16 if epi == "mask2relu" else (F32 if epi == "res" else out_dtype)
        out_specs = row_out
        out_shape = jax.ShapeDtypeStruct((M, n_out), dt)
    return pl.pallas_call(
        body, name=name, grid=(M // tm,), in_specs=in_specs, out_specs=out_specs, out_shape=out_shape,
        compiler_params=_params(("parallel",)),
    )(*args)


def _whole(w):
    return w.shape, (0,) * w.ndim


def _mm_nn(name, a, w, **kw):
    n_out = w.shape[0] * w.shape[2] if w.ndim == 3 else w.shape[1]
    return _mm_rows(name, [(a, w, *_whole(w))], n_out, nt=False, **kw)


def _mm_nt(name, a, w, **kw):
    return _mm_rows(name, [(a, w, *_whole(w))], w.shape[0], nt=True, **kw)


def _mm_nt_rms_bwd(name, pairs, x, g, dres, *, tm, nc, after=None, shards=False, emit_bf16=True):
    M, D = x.shape
    np_ = len(pairs)
    n_out = 3 if emit_bf16 else 2
    after_args, after_specs = _after_operand(after)

    def body(*refs):
        a_refs = refs[:np_]
        w_refs = refs[np_:2 * np_]
        x_ref, g_ref, r_ref = refs[2 * np_:2 * np_ + 3]
        dy_sc = refs[-1]
        outs = refs[-1 - n_out:-1]
        dx_ref, dg_ref = outs[0], outs[1]
        i = pl.program_id(0)
        a_vals = [_bf(a[...]) for a in a_refs]
        for j in range(D // nc):
            cols, acc = _chunk_product(a_vals, w_refs, j, nc, True, shards)
            dy_sc[:, cols] = acc
        xv = x_ref[...]
        r = lax.rsqrt(jnp.mean(xv * xv, axis=-1, keepdims=True) + EPS)
        xh = xv * r
        dy_v = dy_sc[...]
        dyg = dy_v * g_ref[...]
        c = jnp.mean(dyg * xh, axis=-1, keepdims=True)
        dx = r_ref[...] + r * (dyg - xh * c)
        dx_ref[...] = dx
        if emit_bf16:
            outs[2][...] = _bf(dx)
        part = jnp.sum(dy_v * xh, axis=0, keepdims=True)

        @pl.when(i == 0)
        def _():
            dg_ref[...] = part

        @pl.when(i > 0)
        def _():
            dg_ref[...] += part

    row = pl.BlockSpec((tm, D), lambda i: (i, 0))
    vec = pl.BlockSpec((1, D), lambda i: (0, 0))
    in_specs = [pl.BlockSpec((tm, a.shape[1]), lambda i: (i, 0)) for a, _, _, _ in pairs]
    for _, _, wshape, widx in pairs:
        in_specs.append(pl.BlockSpec(wshape, functools.partial(lambda i, widx: widx, widx=widx)))
    args = [a for a, _, _, _ in pairs] + [w for _, w, _, _ in pairs]
    return pl.pallas_call(
        body, name=name, grid=(M // tm,),
        in_specs=in_specs + [row, vec, row] + after_specs,
        out_specs=[row, vec] + [row] * (n_out - 2),
        out_shape=[jax.ShapeDtypeStruct((M, D), F32), jax.ShapeDtypeStruct((1, D), F32)]
        + [jax.ShapeDtypeStruct((M, D), BF16)] * (n_out - 2),
        scratch_shapes=[pltpu.VMEM((tm, D), F32)],
        compiler_params=_params(("arbitrary",)),
    )(*args, x, g, dres, *after_args)


def _mm_tn(name, a, b, *, t1, tn, tm=2048, slot_cols=None):
    M, K1 = a.shape
    N = b.shape[1]
    nm = M // tm

    def body(a_ref, b_ref, o_ref, acc_ref):
        m = pl.program_id(2)
        t = _dot(_bf(a_ref[...]), _bf(b_ref[...]), TN)

        @pl.when(m == 0)
        def _():
            acc_ref[...] = t

        @pl.when(m > 0)
        def _():
            acc_ref[...] += t

        @pl.when(m == nm - 1)
        def _():
            if slot_cols is None:
                o_ref[...] = _bf(acc_ref[...])
            else:
                for k in range(tn // slot_cols):
                    o_ref[k] = _bf(acc_ref[:, k * slot_cols:(k + 1) * slot_cols])

    if slot_cols is not None:
        out_spec = pl.BlockSpec((tn // slot_cols, t1, slot_cols), lambda i, j, m: (j, i, 0))
        out_shape = jax.ShapeDtypeStruct((N // slot_cols, K1, slot_cols), BF16)
    else:
        out_spec = pl.BlockSpec((t1, tn), lambda i, j, m: (i, j))
        out_shape = jax.ShapeDtypeStruct((K1, N), BF16)
    return pl.pallas_call(
        body, name=name, grid=(K1 // t1, N // tn, nm),
        in_specs=[pl.BlockSpec((tm, t1), lambda i, j, m: (m, i)), pl.BlockSpec((tm, tn), lambda i, j, m: (m, j))],
        out_specs=out_spec, out_shape=out_shape,
        scratch_shapes=[pltpu.VMEM((t1, tn), F32)],
        compiler_params=_params(("parallel", "parallel", "arbitrary")),
    )(a, b)


_INV_SQRT2 = 1.0 / math.sqrt(2.0)
_INV_SQRT2PI = 1.0 / math.sqrt(2.0 * math.pi)


def _gelu(x):
    return 0.5 * x * (1.0 + lax.erf(x * _INV_SQRT2))


def _gelu_and_grad(x):
    cdf = 0.5 * (1.0 + lax.erf(x * _INV_SQRT2))
    return x * cdf, cdf + x * (_INV_SQRT2PI * jnp.exp(-0.5 * x * x))


def _layer_norm_parts(v):
    mu = jnp.mean(v, axis=-1, keepdims=True)
    xc = v - mu
    rs = lax.rsqrt(jnp.mean(xc * xc, axis=-1, keepdims=True) + EPS)
    return xc * rs, rs


def _gate_fwd(name, uvp, ln_g, ln_b, wm, bs_full, tr=512):
    S, W2 = uvp.shape
    W = W2 // 2
    gd = W // GROUPS

    def body(u_ref, v_ref, lg_ref, lb_ref, wm_ref, bs_ref, z_ref):
        vh, _ = _layer_norm_parts(_gelu(v_ref[...]))
        vn = _bf(vh * lg_ref[...] + lb_ref[...])
        for ci in range(tr // CHUNK):
            rows = slice(ci * CHUNK, (ci + 1) * CHUNK)
            for g in range(GROUPS):
                cols = slice(g * gd, (g + 1) * gd)
                mixed = _dot(wm_ref[g], vn[rows, cols], NN) + bs_ref[:, cols]
                z_ref[rows, cols] = _bf(_gelu(u_ref[rows, cols]) * mixed)

    vec = pl.BlockSpec((1, W), lambda i: (0, 0))
    return pl.pallas_call(
        body, name=name, grid=(S // tr,),
        in_specs=[pl.BlockSpec((tr, W), lambda i: (i, 0)), pl.BlockSpec((tr, W), lambda i: (i, 1)), vec, vec,
                  pl.BlockSpec((GROUPS, CHUNK, CHUNK), lambda i: (0, 0, 0)),
                  pl.BlockSpec((CHUNK, W), lambda i: (0, 0))],
        out_specs=pl.BlockSpec((tr, W), lambda i: (i, 0)),
        out_shape=jax.ShapeDtypeStruct((S, W), BF16),
        compiler_params=_params(("parallel",)),
    )(uvp, uvp, ln_g, ln_b, wm, bs_full)


def _gate_bwd(name, uvp, dz, ln_g, ln_b, wm, bs_full, tr=256):
    S, W2 = uvp.shape
    W = W2 // 2
    gd = W // GROUPS
    n_steps = S // tr

    def body(u_ref, v_ref, dz_ref, lg_ref, lb_ref, wm_ref, bs_ref, duv_ref, dwm_ref, dmx_ref, dlg_ref, dlb_ref,
             dvn_ref):
        i = pl.program_id(0)
        v, dv_dvp = _gelu_and_grad(v_ref[...])
        vh, rs = _layer_norm_parts(v)
        lg = lg_ref[...]
        vn = _bf(vh * lg + lb_ref[...])

        @pl.when(i == 0)
        def _():
            dwm_ref[...] = jnp.zeros_like(dwm_ref)
            dmx_ref[...] = jnp.zeros_like(dmx_ref)
            dlg_ref[...] = jnp.zeros_like(dlg_ref)
            dlb_ref[...] = jnp.zeros_like(dlb_ref)

        for ci in range(tr // CHUNK):
            rows = slice(ci * CHUNK, (ci + 1) * CHUNK)
            for g in range(GROUPS):
                cols = slice(g * gd, (g + 1) * gd)
                u, du_dup = _gelu_and_grad(u_ref[rows, cols])
                dz_v = dz_ref[rows, cols]
                dmixed = dz_v * u
                dmx_ref[:, cols] += dmixed
                dmixed_b = _bf(dmixed)
                mixed = _dot(wm_ref[g], vn[rows, cols], NN) + bs_ref[:, cols]
                duv_ref[rows, cols] = _bf(dz_v * mixed * du_dup)
                dwm_ref[g] += _dot(dmixed_b, vn[rows, cols], NT)
                dvn_ref[rows, cols] = _dot(wm_ref[g], dmixed_b, TN)
        dvn = dvn_ref[...]
        dlg_ref[...] += jnp.sum(dvn * vh, axis=0, keepdims=True)
        dlb_ref[...] += jnp.sum(dvn, axis=0, keepdims=True)
        dvh = dvn * lg
        dv = rs * (dvh - jnp.mean(dvh, axis=-1, keepdims=True) - vh * jnp.mean(dvh * vh, axis=-1, keepdims=True))
        duv_ref[:, W:] = _bf(dv * dv_dvp)

        @pl.when(i == n_steps - 1)
        def _():
            t_idx = lax.broadcasted_iota(jnp.int32, (CHUNK, CHUNK), 0)
            s_idx = lax.broadcasted_iota(jnp.int32, (CHUNK, CHUNK), 1)
            keep = (s_idx <= t_idx).astype(F32)
            for g in range(GROUPS):
                dwm_ref[g] = dwm_ref[g] * keep

    vec = pl.BlockSpec((1, W), lambda i: (0, 0))
    row = pl.BlockSpec((tr, W), lambda i: (i, 0))
    return pl.pallas_call(
        body, name=name, grid=(n_steps,),
        in_specs=[row, pl.BlockSpec((tr, W), lambda i: (i, 1)), row, vec, vec,
                  pl.BlockSpec((GROUPS, CHUNK, CHUNK), lambda i: (0, 0, 0)),
                  pl.BlockSpec((CHUNK, W), lambda i: (0, 0))],
        out_specs=[pl.BlockSpec((tr, W2), lambda i: (i, 0)),
                   pl.BlockSpec((GROUPS, CHUNK, CHUNK), lambda i: (0, 0, 0)),
                   pl.BlockSpec((CHUNK, W), lambda i: (0, 0)), vec, vec],
        out_shape=[jax.ShapeDtypeStruct((S, W2), BF16), jax.ShapeDtypeStruct((GROUPS, CHUNK, CHUNK), F32),
                   jax.ShapeDtypeStruct((CHUNK, W), F32), jax.ShapeDtypeStruct((1, W), F32),
                   jax.ShapeDtypeStruct((1, W), F32)],
        scratch_shapes=[pltpu.VMEM((tr, W), F32)],
        compiler_params=_params(("arbitrary",)),
    )(uvp, uvp, dz, ln_g, ln_b, wm, bs_full)


def _t5_bucket(distance):
    small = distance < MAX_EXACT
    nf = jnp.maximum(distance, 1).astype(F32)
    large = MAX_EXACT + (jnp.log(nf / MAX_EXACT) / math.log(REL_MAX_DISTANCE / MAX_EXACT)
                         * (N_BUCKETS - MAX_EXACT)).astype(jnp.int32)
    large = jnp.minimum(large, N_BUCKETS - 1)
    return jnp.where(small, distance, large)


TILE_ELEMS = 2 * CHUNK * CHUNK


def _band_buckets():
    rel = CHUNK + jnp.arange(CHUNK)[None, :] - jnp.arange(2 * CHUNK)[:, None]
    band = (rel >= 0) & (rel <= CHUNK)
    buckets = [_t5_bucket(jnp.clip(rel, 0, CHUNK) * d) for d in DILATIONS]
    return jnp.stack(buckets), band


def _bucket_onehot():
    buckets, _ = _band_buckets()
    return (buckets.reshape(N_DIL, 1, TILE_ELEMS) == jnp.arange(N_BUCKETS)[None, :, None]).astype(F32)


def _bias_tiles(name, rel_bias, after=None):
    _, band = _band_buckets()
    own = band & (jnp.arange(2 * CHUNK) >= CHUNK)[:, None]
    masks = jnp.stack([own, band]).reshape(2, TILE_ELEMS).astype(F32)
    tables = jnp.transpose(rel_bias.reshape(N_BUCKETS, N_DIL, ATT_HEADS), (1, 2, 0))
    after_args, after_specs = _after_operand(after)

    def body(t_ref, oh_ref, m_ref, *rest):
        out_ref = rest[-1]
        for g in range(N_DIL):
            bias = lax.dot_general(t_ref[g], oh_ref[g], (NN, ((), ())), precision=lax.Precision.HIGHEST,
                                   preferred_element_type=F32)
            for f in range(2):
                out_ref[g, f] = jnp.where(m_ref[f:f + 1, :] > 0.5, bias, NEG_INF)

    whole = pl.BlockSpec(memory_space=pltpu.VMEM)
    out = pl.pallas_call(
        body, name=name, out_shape=jax.ShapeDtypeStruct((N_DIL, 2, ATT_HEADS, TILE_ELEMS), F32),
        in_specs=[whole, whole, whole] + after_specs, out_specs=whole,
        compiler_params=_params(),
    )(tables, _bucket_onehot(), masks, *after_args)
    out = out.reshape(N_DIL, 2, ATT_HEADS // 2, 2, 2 * CHUNK, CHUNK)
    return jnp.transpose(out, (0, 1, 2, 4, 3, 5)).reshape(N_DIL, 2, ATT_HEADS // 2, 2 * CHUNK, 2 * CHUNK)


def _att_specs(order):
    def column(part, ids):
        hp, g, _ = order(*ids)
        return part * 3 * 4 + g * 4 + hp

    def window(part):
        def index(*ids):
            c = order(*ids)[2]
            return pl.multiple_of(jnp.maximum(c - 1, 0) * ATT_ROWS, ATT_ROWS), column(part, ids) * LANES
        return pl.BlockSpec((pl.Element(2 * ATT_ROWS), pl.Element(LANES)), index)

    return [pl.BlockSpec((ATT_ROWS, LANES), lambda *ids: (order(*ids)[2], column(0, ids))), window(1), window(2)]


def _window_base(c):
    return jnp.where(c == 0, 0, ATT_ROWS)


def _rows(start, d):
    if d == 1:
        return pl.ds(pl.multiple_of(start, CHUNK), CHUNK)
    return pl.ds(start, CHUNK, stride=d)


def _att_tile_offsets(t, d):
    n = t // d
    r = t % d
    return n * (CHUNK * d) + r, n


def _head_pair_columns(x_t):
    zeros = jnp.zeros((HEAD_DIM, CHUNK), x_t.dtype)
    return jnp.concatenate([jnp.concatenate([x_t[:HEAD_DIM], zeros], axis=0),
                            jnp.concatenate([zeros, x_t[HEAD_DIM:]], axis=0)], axis=1)


def _head_pair_rows(y):
    return jnp.concatenate([y[:HEAD_DIM, :CHUNK], y[HEAD_DIM:, CHUNK:]], axis=0)


def _att_fwd(name, qkv, bias_tiles):
    S = qkv.shape[0]
    n_chunks = S // ATT_ROWS
    tiles = ATT_ROWS // CHUNK

    def body(q_ref, kk, vv, b_ref, out_ref, lse_ref, o_sc, l_sc):
        c = pl.program_id(1)
        g = pl.program_id(2)
        base = _window_base(c)

        for gi, d in enumerate(DILATIONS):
            @pl.when(g == gi)
            def _(gi=gi, d=d):
                span = CHUNK * d

                def tile(t, carry):
                    q0, n = _att_tile_offsets(t, d)
                    first = (c == 0) & (n == 0)
                    rows = _rows(q0, d)
                    cur = _rows(base + q0, d)
                    prev = _rows(jnp.where(first, q0, base + q0 - span), d)
                    inner = jnp.where(first, 0, 1)
                    qq = _head_pair_columns(_bf(q_ref[rows, :] * ATT_SCALE).T)
                    s_p = _dot(_bf(kk[prev, :]), qq, NN) + b_ref[inner, 0:CHUNK, :]
                    s_c = _dot(_bf(kk[cur, :]), qq, NN) + b_ref[inner, CHUNK:2 * CHUNK, :]
                    m = jnp.maximum(jnp.max(s_p, axis=0, keepdims=True), jnp.max(s_c, axis=0, keepdims=True))
                    p_p = jnp.exp(s_p - m)
                    p_c = jnp.exp(s_c - m)
                    l = jnp.sum(p_p, axis=0, keepdims=True) + jnp.sum(p_c, axis=0, keepdims=True)
                    o2 = (_dot(_bf(vv[prev, :]).T, _bf(p_p), NN)
                          + _dot(_bf(vv[cur, :]).T, _bf(p_c), NN)) * (1.0 / l)
                    lse = m + jnp.log(l)
                    l_t = jnp.concatenate([jnp.broadcast_to(lse[:, :CHUNK], (HEAD_DIM, CHUNK)),
                                           jnp.broadcast_to(lse[:, CHUNK:], (HEAD_DIM, CHUNK))], axis=0)
                    o_sc[gi, rows, :] = _head_pair_rows(o2).T
                    l_sc[gi, rows, :] = l_t.T
                    return carry

                lax.fori_loop(0, tiles, tile, 0, unroll=16)

        @pl.when(g == N_DIL - 1)
        def _():
            for rows in (slice(i * 4 * CHUNK, (i + 1) * 4 * CHUNK) for i in range(ATT_ROWS // (4 * CHUNK))):
                ls = [l_sc[gi, rows, :] for gi in range(N_DIL)]
                mx = functools.reduce(jnp.maximum, ls)
                ws = [jnp.exp(l - mx) for l in ls]
                tot = functools.reduce(lambda a, b: a + b, ws)
                acc = ws[0] * o_sc[0, rows, :]
                for gi in range(1, N_DIL):
                    acc = acc + ws[gi] * o_sc[gi, rows, :]
                out_ref[rows, :] = acc / tot
                lse_ref[rows, :] = mx + jnp.log(tot)

    order = lambda hp, c, g: (hp, g, c)
    out_spec = pl.BlockSpec((ATT_ROWS, LANES), lambda hp, c, g: (c, hp))
    shape = jax.ShapeDtypeStruct((S, ATT_WIDTH), F32)
    return pl.pallas_call(
        body, name=name, grid=(ATT_HEADS // 2, n_chunks, N_DIL),
        in_specs=_att_specs(order) + [
            pl.BlockSpec((None, 2, None, 2 * CHUNK, 2 * CHUNK), lambda hp, c, g: (g, 0, hp, 0, 0))],
        out_specs=[out_spec, out_spec],
        out_shape=[shape, shape],
        scratch_shapes=[pltpu.VMEM((N_DIL, ATT_ROWS, LANES), F32), pltpu.VMEM((N_DIL, ATT_ROWS, LANES), F32)],
        compiler_params=_params(("parallel", "parallel", "arbitrary")),
    )(qkv, qkv, qkv, bias_tiles)


def _att_bwd(name, qkv, o, lse, d_o, bias_tiles):
    S = qkv.shape[0]
    n_chunks = S // ATT_ROWS
    tiles = ATT_ROWS // CHUNK

    def body(q_ref, kk, vv, o_ref, l_ref, do_ref, b_ref, dq_out, dk_out, dv_out, ds_ref, dq_ref, dk_ref, dv_ref):
        g = pl.program_id(1)
        c = pl.program_id(2)

        @pl.when(c == 0)
        def _():
            dk_ref[...] = jnp.zeros_like(dk_ref)
            dv_ref[...] = jnp.zeros_like(dv_ref)
            ds_ref[...] = jnp.zeros_like(ds_ref)

        base = _window_base(c)
        first_row = c * ATT_ROWS
        head0 = lax.broadcasted_iota(jnp.int32, (CHUNK, LANES), 1) < HEAD_DIM

        def head_pair_stack(x):
            zero = jnp.zeros_like(x)
            return jnp.concatenate([jnp.where(head0, x, zero), jnp.where(head0, zero, x)], axis=0)

        for gi, d in enumerate(DILATIONS):
            @pl.when(g == gi)
            def _(d=d):
                span = CHUNK * d

                def tile(t, carry):
                    q0, n = _att_tile_offsets(t, d)
                    first = (c == 0) & (n == 0)
                    rows = _rows(q0, d)
                    cur = _rows(base + q0, d)
                    prev = _rows(jnp.where(first, q0, base + q0 - span), d)
                    inner = jnp.where(first, 0, 1)
                    g_cur = _rows(first_row + q0, d)
                    g_prev = _rows(jnp.where(first, q0, first_row + q0 - span), d)
                    q2 = _bf(q_ref[rows, :] * ATT_SCALE)
                    q_t = q2.T
                    k2 = _bf(jnp.concatenate([kk[prev, :], kk[cur, :]], axis=0))
                    k_t = k2.T
                    v2 = _bf(jnp.concatenate([vv[prev, :], vv[cur, :]], axis=0))
                    do2 = do_ref[rows, :]
                    do_b = _bf(do2)
                    do_t = do_b.T
                    lse_t = l_ref[rows, :].T
                    dd_t = (do2 * o_ref[rows, :]).T
                    lse = jnp.concatenate([lse_t[0:1], lse_t[HEAD_DIM:HEAD_DIM + 1]], axis=1)
                    delta = jnp.concatenate([jnp.sum(dd_t[:HEAD_DIM], axis=0, keepdims=True),
                                             jnp.sum(dd_t[HEAD_DIM:], axis=0, keepdims=True)], axis=1)
                    s = _dot(k2, _head_pair_columns(q_t), NN) + b_ref[inner]
                    p = jnp.exp(s - lse)
                    ds = p * (_dot(v2, _head_pair_columns(do_t), NN) - delta)
                    ds_ref[...] += ds
                    ds_b = _bf(ds)
                    dq_t = _head_pair_rows(_dot(k_t, ds_b, NN))
                    dk2 = _dot(ds_b, head_pair_stack(q2), NN)
                    dv2 = _dot(_bf(p), head_pair_stack(do_b), NN)
                    dq_ref[rows, :] = (dq_t * ATT_SCALE).T
                    dk_ref[g_prev, :] += dk2[0:CHUNK]
                    dk_ref[g_cur, :] += dk2[CHUNK:2 * CHUNK]
                    dv_ref[g_prev, :] += dv2[0:CHUNK]
                    dv_ref[g_cur, :] += dv2[CHUNK:2 * CHUNK]
                    return carry

                lax.fori_loop(0, tiles, tile, 0, unroll=16)

        dq_out[...] = _bf(dq_ref[...])

        @pl.when(c == n_chunks - 1)
        def _():
            dk_out[...] = _bf(dk_ref[...])
            dv_out[...] = _bf(dv_ref[...])

    order = lambda hp, g, c: (hp, g, c)
    chunk = pl.BlockSpec((ATT_ROWS, LANES), lambda hp, g, c: (c, hp))
    slab = pl.BlockSpec((S, LANES), lambda hp, g, c: (0, g * 4 + hp))
    width = N_DIL * ATT_WIDTH
    dq, dk, dv, ds_sums = pl.pallas_call(
        body, name=name, grid=(ATT_HEADS // 2, N_DIL, n_chunks),
        in_specs=_att_specs(order) + [chunk, chunk, chunk,
                                      pl.BlockSpec((None, 2, None, 2 * CHUNK, 2 * CHUNK),
                                                   lambda hp, g, c: (g, 0, hp, 0, 0))],
        out_specs=[pl.BlockSpec((ATT_ROWS, LANES), lambda hp, g, c: (c, g * 4 + hp)), slab, slab,
                   pl.BlockSpec((None, None, 2 * CHUNK, 2 * CHUNK), lambda hp, g, c: (g, hp, 0, 0))],
        out_shape=[jax.ShapeDtypeStruct((S, width), BF16), jax.ShapeDtypeStruct((S, width), BF16),
                   jax.ShapeDtypeStruct((S, width), BF16),
                   jax.ShapeDtypeStruct((N_DIL, ATT_HEADS // 2, 2 * CHUNK, 2 * CHUNK), F32)],
        scratch_shapes=[pltpu.VMEM((ATT_ROWS, LANES), F32), pltpu.VMEM((S, LANES), F32),
                        pltpu.VMEM((S, LANES), F32)],
        compiler_params=_params(("parallel", "parallel", "arbitrary")),
    )(qkv, qkv, qkv, o, lse, d_o, bias_tiles)
    ds_sums = ds_sums.reshape(N_DIL, ATT_HEADS // 2, 2 * CHUNK, 2, CHUNK)
    ds_sums = jnp.transpose(ds_sums, (0, 1, 3, 2, 4)).reshape(N_DIL, ATT_HEADS, 2 * CHUNK, CHUNK)
    return dq, dk, dv, ds_sums


def _bias_grad(name, ds_sums):
    flat = ds_sums.reshape(N_DIL, ATT_HEADS, TILE_ELEMS)

    def body(oh_ref, ds_ref, out_ref):
        for g in range(N_DIL):
            out_ref[g] = lax.dot_general(oh_ref[g], ds_ref[g], (NT, ((), ())), precision=lax.Precision.HIGHEST,
                                         preferred_element_type=F32)

    out = pl.pallas_call(
        body, name=name, out_shape=jax.ShapeDtypeStruct((N_DIL, N_BUCKETS, ATT_HEADS), F32),
        compiler_params=_params(),
    )(_bucket_onehot(), flat)
    return jnp.transpose(out, (1, 0, 2)).reshape(N_BUCKETS, N_DIL * ATT_HEADS)


def _peers():
    x, y, c = lax.axis_index("x"), lax.axis_index("y"), lax.axis_index("c")
    me = 4 * x + 2 * y + c
    others = [(x, y, 1 - c), (1 - x, y, c), (x, 1 - y, c), (1 - x, 1 - y, c),
              (1 - x, y, 1 - c), (x, 1 - y, 1 - c), (1 - x, 1 - y, 1 - c)]
    return me, others


def _slot(dev):
    return 4 * dev[0] + 2 * dev[1] + dev[2]


_HBM =pl.BlockSpec(memory_space=pltpu.HBM)
_SEM = pl.BlockSpec(memory_space=pltpu.SEMAPHORE)
_EFFECT = pltpu.SideEffectType.DATAFLOW_SIDE_EFFECTING


def _my_slot():
    return 4 * lax.axis_index("x") + 2 * lax.axis_index("y") + lax.axis_index("c")


def _exchange_copy(src_ref, land_ref, send_sems, recv_sems, k, dev, me, scatter, arriving):
    src = src_ref.at[me if arriving else _slot(dev)] if scatter else src_ref
    dst = land_ref.at[_slot(dev) if arriving else me]
    return pltpu.make_async_remote_copy(src_ref=src, dst_ref=dst, send_sem=send_sems.at[k], recv_sem=recv_sems.at[k],
                                        device_id=dev, device_id_type=MESH)


def _exchange_start(name, srcs, scatter):
    n = len(srcs)
    me = _my_slot()
    landings = []
    for src in srcs:
        own = lax.dynamic_index_in_dim(src, me, 0, keepdims=True) if scatter else src[None]
        landings.append(lax.dynamic_update_slice(lax.empty((N_DEV,) + src.shape[-2:], src.dtype), own, (me, 0, 0)))

    def body(*refs):
        src_refs, land_refs = refs[:n], refs[n:2 * n]
        send_sems, recv_sems = refs[2 * n:2 * n + 2]
        token = refs[-1]
        me, others = _peers()
        for p in range(n):
            for k, dev in enumerate(others):
                _exchange_copy(src_refs[p], land_refs[p], send_sems, recv_sems, p * (N_DEV - 1) + k, dev, me,
                               scatter, False).start()
        token[...] = jnp.zeros_like(token)

    sems = pltpu.SemaphoreType.DMA((n * (N_DEV - 1),))
    hbm = lambda a: pltpu.with_memory_space_constraint(a, pltpu.HBM)
    outs = pl.pallas_call(
        body, name=name,
        out_shape=(sems, sems, *[pltpu.HBM(a.shape, a.dtype) for a in srcs + landings],
                   jax.ShapeDtypeStruct((8, LANES), F32)),
        in_specs=(_HBM,) * (2 * n), out_specs=(_SEM, _SEM) + (_HBM,) * (2 * n) + (pl.BlockSpec(memory_space=pltpu.VMEM),),
        input_output_aliases={i: 2 + i for i in range(2 * n)},
        compiler_params=pltpu.CompilerParams(has_side_effects=_EFFECT),
    )(*[hbm(a) for a in srcs + landings])
    return (outs[0], outs[1], list(outs[2:2 + n]), list(outs[2 + n:2 + 2 * n]), scatter), outs[-1]


def _exchange_wait(name, handle, after):
    send_sems, recv_sems, src_thru, land_thru, scatter = handle
    n = len(src_thru)

    def body(*refs):
        src_refs, land_refs = refs[:n], refs[n:2 * n]
        send_sems, recv_sems = refs[2 * n:2 * n + 2]
        me, others = _peers()
        for p in range(n):
            for k, dev in enumerate(others):
                cp = _exchange_copy(src_refs[p], land_refs[p], send_sems, recv_sems, p * (N_DEV - 1) + k, dev, me,
                                    scatter, True)
                cp.wait_send()
                cp.wait_recv()

    outs = pl.pallas_call(
        body, name=name,
        out_shape=tuple(pltpu.HBM(a.shape, a.dtype) for a in src_thru + land_thru),
        in_specs=(_HBM,) * (2 * n) + (_SEM, _SEM, pl.BlockSpec(memory_space=pl.ANY)), out_specs=(_HBM,) * (2 * n),
        input_output_aliases={i: i for i in range(2 * n)},
        compiler_params=pltpu.CompilerParams(has_side_effects=_EFFECT),
    )(*src_thru, *land_thru, send_sems, recv_sems, after)
    return list(outs[n:])


def _adamw_math(w, g, m, v):
    m = ADAM_B1 * m + (1.0 - ADAM_B1) * g
    v = ADAM_B2 * v + (1.0 - ADAM_B2) * (g * g)
    m_hat = m / (1.0 - ADAM_B1 ** ADAM_STEP)
    v_hat = v / (1.0 - ADAM_B2 ** ADAM_STEP)
    delta = -ADAM_LR * (m_hat / (jnp.sqrt(v_hat) + ADAM_EPS) + ADAM_WD * w)
    return delta, m, v


_SMALL_WIDE = (("mix_norm_g", 2), ("mlp_norm_g", 2), ("final_norm_g", 1), ("a_ln_g", 1), ("a_ln_b", 1))
_SMALL_NARROW = (("a_w_s", GROUPS * CHUNK), ("a_b_s", GROUPS), ("rel_bias", N_BUCKETS))
_SMALL = tuple(n for n, _ in _SMALL_WIDE + _SMALL_NARROW)
_BIAS_COLS = N_DIL * ATT_HEADS


def _pack_small_grads(grads, loss_term):
    D = grads["a_ln_g"].shape[-1]
    tiles = [jnp.pad(grads[n].reshape(k, D), ((0, 8 - k), (0, 0))) for n, k in _SMALL_WIDE]
    tiles.append(jnp.pad(loss_term.reshape(1, 1), ((0, 7), (0, D - 1))))
    narrow = [grads["a_w_s"].reshape(-1, LANES), grads["a_b_s"].reshape(-1, LANES),
              jnp.pad(grads["rel_bias"], ((0, 0), (0, LANES - _BIAS_COLS)))]
    return jnp.concatenate(tiles, axis=0), jnp.concatenate(narrow, axis=0)


def _adamw_small(name, g_wide, g_narrow, w, m, v):
    D = g_wide.shape[-1]
    shapes = {n: (k, D) for n, k in _SMALL_WIDE}
    shapes.update({n: (k, LANES) for n, k in _SMALL_NARROW})
    shapes["rel_bias"] = (N_BUCKETS, _BIAS_COLS)
    n_t = len(_SMALL)

    def body(gw_ref, gn_ref, *rest):
        params = rest[:3 * n_t]
        outs = rest[3 * n_t:3 * n_t + 4 * n_t]
        loss_ref, sw, sn = rest[-3:]
        sw[...] = functools.reduce(lambda a, b: a + b, [gw_ref[j] for j in range(N_DEV)])
        sn[...] = functools.reduce(lambda a, b: a + b, [gn_ref[j] for j in range(N_DEV)])
        row = 0
        for i, n in enumerate(_SMALL):
            k, cols = shapes[n]
            if i < len(_SMALL_WIDE):
                g = sw[8 * i:8 * i + k, :]
            else:
                g = sn[row:row + k, 0:cols]
                row += k
            w_ref, m_ref, v_ref = params[3 * i:3 * i + 3]
            delta, m_new, v_new = _adamw_math(w_ref[...], g, m_ref[...], v_ref[...])
            for out, val in zip(outs[4 * i:4 * i + 4], (g, delta, m_new, v_new)):
                out[...] = val
        loss_ref[...] = sw[8 * len(_SMALL_WIDE):8 * len(_SMALL_WIDE) + 8, 0:LANES]

    whole = pl.BlockSpec(memory_space=pltpu.VMEM)
    args = [t[n].reshape(shapes[n]) for n in _SMALL for t in (w, m, v)]
    res = pl.pallas_call(
        body, name=name,
        in_specs=[whole] * (2 + len(args)), out_specs=[whole] * (4 * n_t + 1),
        out_shape=[jax.ShapeDtypeStruct(shapes[n], F32) for n in _SMALL for _ in range(4)]
        + [jax.ShapeDtypeStruct((8, LANES), F32)],
        scratch_shapes=[pltpu.VMEM(g_wide.shape[1:], F32), pltpu.VMEM(g_narrow.shape[1:], F32)],
        compiler_params=_params(),
    )(g_wide, g_narrow, *args)
    small = {n: tuple(r.reshape(w[n].shape) for r in res[4 * i:4 * i + 4]) for i, n in enumerate(_SMALL)}
    return small, res[-1][0, 0]


def _adamw_shard(name, parts, w, m, v, layer, earlier=None, after=None, tr=256):
    L, K, N = w.shape
    tr = min(tr, K)
    n_prev = 0 if earlier is None else 4
    after_args, after_specs = _after_operand(after)

    def body(p_ref, w_ref, m_ref, v_ref, *rest):
        g_out, d_out, m_out, v_out = rest[n_prev + len(after_args):]
        g = p_ref[0].astype(F32)
        for j in range(1, N_DEV):
            g = g + p_ref[j].astype(F32)
        delta, m_new, v_new = _adamw_math(w_ref[...], g, m_ref[...], v_ref[...])
        g_out[...] = g
        d_out[...] = delta
        m_out[...] = m_new
        v_out[...] = v_new

    row = pl.BlockSpec((None, tr, N), lambda i: (layer, i, 0))
    shape = jax.ShapeDtypeStruct((L, K, N), F32)
    return pl.pallas_call(
        body, name=name, grid=(K // tr,),
        in_specs=[pl.BlockSpec((N_DEV, tr, N), lambda i: (0, i, 0)), row, row, row]
        + [pl.BlockSpec(memory_space=pl.ANY)] * n_prev + after_specs,
        out_specs=[row, row, row, row],
        out_shape=[shape, shape, shape, shape],
        input_output_aliases={4 + j: j for j in range(n_prev)},
        compiler_params=_params(("parallel",)),
    )(parts, w, m, v, *(earlier or ()), *after_args)


def _column_slots(full):
    K, N = full.shape
    return jnp.transpose(full.reshape(K, N_DEV, N // N_DEV), (1, 0, 2))


def _from_column_slots(slots):
    _, K, n = slots.shape
    return jnp.transpose(slots, (1, 0, 2)).reshape(K, N_DEV * n)


_STAGES = (("gate", ("a_w_in", "a_w_out"), 0),
           ("mlp0", ("w_up", "w_down"), 0),
           ("att", ("b_w_qkv", "b_w_out"), 0),
           ("mlp1", ("w_up", "w_down"), 1))


def kernel(x, mix_norm_g, mlp_norm_g, final_norm_g, a_w_in, a_ln_g, a_ln_b, a_w_s, a_b_s, a_w_out, b_w_qkv, b_w_out, rel_bias, w_up, w_down, loss_target, m_mix_norm_g, m_mlp_norm_g, m_final_norm_g, m_a_w_in, m_a_ln_g, m_a_ln_b, m_a_w_s, m_a_b_s, m_a_w_out, m_b_w_qkv, m_b_w_out, m_rel_bias, m_w_up, m_w_down, v_mix_norm_g, v_mlp_norm_g, v_final_norm_g, v_a_w_in, v_a_ln_g, v_a_ln_b, v_a_w_s, v_a_b_s, v_a_w_out, v_b_w_qkv, v_b_w_out, v_rel_bias, v_w_up, v_w_down):
    w = dict(mix_norm_g=mix_norm_g, mlp_norm_g=mlp_norm_g, final_norm_g=final_norm_g, a_w_in=a_w_in, a_ln_g=a_ln_g,
             a_ln_b=a_ln_b, a_w_s=a_w_s, a_b_s=a_b_s, a_w_out=a_w_out, b_w_qkv=b_w_qkv, b_w_out=b_w_out,
             rel_bias=rel_bias, w_up=w_up, w_down=w_down)
    m = dict(mix_norm_g=m_mix_norm_g, mlp_norm_g=m_mlp_norm_g, final_norm_g=m_final_norm_g, a_w_in=m_a_w_in,
             a_ln_g=m_a_ln_g, a_ln_b=m_a_ln_b, a_w_s=m_a_w_s, a_b_s=m_a_b_s, a_w_out=m_a_w_out, b_w_qkv=m_b_w_qkv,
             b_w_out=m_b_w_out, rel_bias=m_rel_bias, w_up=m_w_up, w_down=m_w_down)
    v = dict(mix_norm_g=v_mix_norm_g, mlp_norm_g=v_mlp_norm_g, final_norm_g=v_final_norm_g, a_w_in=v_a_w_in,
             a_ln_g=v_a_ln_g, a_ln_b=v_a_ln_b, a_w_s=v_a_w_s, a_b_s=v_a_b_s, a_w_out=v_a_w_out, b_w_qkv=v_b_w_qkv,
             b_w_out=v_b_w_out, rel_bias=v_rel_bias, w_up=v_w_up, w_down=v_w_down)

    stages = {s: (names, layer) for s, names, layer in _STAGES}
    order = [s for s, _, _ in _STAGES]

    def shards_of(stage):
        names, layer = stages[stage]
        return [_bf(w[n][layer]) for n in names]

    pending = {}
    pending[order[0]], first_token = _exchange_start("gather_" + order[0] + "_start", shards_of(order[0]), False)

    def get_weights(stage, dep):
        gathered = _exchange_wait("gather_" + stage + "_wait", pending.pop(stage), dep)
        nxt = order.index(stage) + 1
        token = None
        if nxt < len(order):
            shards, gathered = lax.optimization_barrier((shards_of(order[nxt]), gathered))
            pending[order[nxt]], token = _exchange_start("gather_" + order[nxt] + "_start", shards, False)
        return gathered, token

    sent = {}

    def put_grads(stage, slot_grads):
        sent[stage], token = _exchange_start("scatter_" + stage + "_start", slot_grads, True)
        return token

    loss_local, grad_x, small_g = _local_step(
        x[0], loss_target[0], mix_norm_g, mlp_norm_g, final_norm_g, a_ln_g, a_ln_b, a_w_s, a_b_s, rel_bias,
        get_weights, put_grads, first_token)

    small_sent, token = _exchange_start("gather_small_start", list(_pack_small_grads(small_g, loss_local)), False)

    results = {}
    prev = token
    for stage in reversed(order):
        names, layer = stages[stage]
        received = _exchange_wait("scatter_" + stage + "_wait", sent[stage], prev)
        for n, parts in zip(names, received):
            results[n] = _adamw_shard("adamw_%s_%s" % (stage, n), parts, w[n], m[n], v[n], layer, results.get(n),
                                      after=prev)
            prev = results[n][0]

    g_wide, g_narrow = _exchange_wait("gather_small_wait", small_sent, prev)
    small, loss = _adamw_small("adamw_small", g_wide, g_narrow, w, m, v)

    outs = []
    for j in range(4):
        outs.extend(small[n][j] if n in _SMALL else results[n][j] for n in w)
    return (loss, grad_x[None], *outs)


def _local_step(xs, tgt, mix_norm_g, mlp_norm_g, final_norm_g, a_ln_g, a_ln_b, a_w_s, a_b_s, rel_bias,
                get_weights, put_grads, first_token=None):
    D = xs.shape[-1]
    g_mix = [mix_norm_g[l][None, :] for l in range(2)]
    g_mlp = [mlp_norm_g[l][None, :] for l in range(2)]
    g_fin = final_norm_g[None, :]
    ln_g, ln_b = a_ln_g, a_ln_b
    causal = jnp.tril(jnp.ones((CHUNK, CHUNK), dtype=bool))
    wm = _bf(jnp.where(causal[None], a_w_s[0], 0.0))
    bs_full = jnp.repeat(a_b_s[0].T, D // GROUPS, axis=1)
    bias_tiles = _bias_tiles("att_bias", rel_bias, after=first_token)

    (win, wout), token = get_weights("gate", bias_tiles)
    wout = wout.reshape(-1, D)
    y0 = _rms_fwd("rms_mix0", xs, g_mix[0], after=token)
    uvp = _mm_nn("gate_in", y0, win, tm=512, nc=win.shape[2], shards=True)
    z = _gate_fwd("gate_mid", uvp, ln_g, ln_b, wm, bs_full)
    h1, y1 = _mm_nn("gate_out", z, wout, tm=512, nc=512, epi="res", extra=xs, norm_g=g_mlp[0])
    (wup0, wdn0), token = get_weights("mlp0", h1)
    wdn0 = wdn0.reshape(-1, D)
    a0, f0 = _mm_nn("mlp0_up", y1, wup0, tm=512, nc=wup0.shape[2], epi="relu2", shards=True, after=token)
    h2, y2 = _mm_nn("mlp0_down", f0, wdn0, tm=512, nc=512, epi="res", extra=h1, norm_g=g_mix[1])
    (wqkv, wo), token = get_weights("att", h2)
    wqkv, wo = _from_column_slots(wqkv), _from_column_slots(wo)
    qkv = _mm_nn("att_qkv", y2, wqkv, tm=512, nc=512, after=token)
    o_att, lse = _att_fwd("att_fwd", qkv, bias_tiles)
    h3, y3 = _mm_nn("att_out", o_att, wo, tm=512, nc=512, epi="res", extra=h2, norm_g=g_mlp[1])
    (wup1, wdn1), _ = get_weights("mlp1", h3)
    wdn1 = wdn1.reshape(-1, D)
    a1, f1 = _mm_nn("mlp1_up", y3, wup1, tm=512, nc=wup1.shape[2], epi="relu2", shards=True)
    dh, dg_fin, err2, dh_b = _mm_res_loss("mlp1_down_loss", f1, wdn1, h3, g_fin, tgt, tm=512, nc=512)
    loss_local = 0.5 * jnp.sum(err2) / D

    def mlp_bwd(tag, dh, dh_b, h_in, y, a, f, wup_l, wdn_l, g_row, after):
        da = _mm_nt(tag + "_dact", dh_b, wdn_l, tm=512, nc=512, epi="mask2relu", extra=a, after=after)
        g_dn = _mm_tn(tag + "_dwdown", f, dh_b, t1=1024, tn=1024, tm=DW_TOKENS)
        g_up = _mm_tn(tag + "_dwup", y, da, t1=1024, tn=1024, tm=DW_TOKENS, slot_cols=wup_l.shape[2])
        dh_in, dg, dh_in_b = _mm_nt_rms_bwd(tag + "_dy", [(da, wup_l, *_whole(wup_l))], h_in, g_row, dh, tm=512,
                                            nc=512, shards=True)
        return dh_in, dh_in_b, dg, put_grads(tag, [g_up, g_dn.reshape(N_DEV, -1, D)])

    dh3, dh3_b, dg_mlp1, token = mlp_bwd("mlp1", dh, dh_b, h3, y3, a1, f1, wup1, wdn1, g_mlp[1], None)

    d_o = _mm_nt("att_dout", dh3_b, wo, tm=512, nc=512, after=token)
    g_wo = _mm_tn("att_dwo", o_att, dh3_b, t1=512, tn=1024, tm=DW_TOKENS)
    dq, dk, dv, ds_sums = _att_bwd("att_bwd", qkv, o_att, lse, d_o, bias_tiles)
    part_w = N_DIL * ATT_WIDTH
    g_qkv = [_mm_tn("att_dwqkv%d" % p, y2, t, t1=1024, tn=part_w) for p, t in enumerate((dq, dk, dv))]
    dh2, dg_mix1, dh2_b = _mm_nt_rms_bwd(
        "att_dy", [(t, wqkv, (D, part_w), (0, p)) for p, t in enumerate((dq, dk, dv))], h2, g_mix[1], dh3, tm=512,
        nc=512)
    token = put_grads("att", [_column_slots(jnp.concatenate(g_qkv, axis=1)), _column_slots(g_wo)])

    dh1, dh1_b, dg_mlp0, token = mlp_bwd("mlp0", dh2, dh2_b, h1, y1, a0, f0, wup0, wdn0, g_mlp[0], token)

    dz = _mm_nt("gate_dz", dh1_b, wout, tm=512, nc=512, after=token)
    g_wout = _mm_tn("gate_dwout", z, dh1_b, t1=1024, tn=1024, tm=DW_TOKENS)
    duvp, d_wm, d_mixed, d_lng, d_lnb = _gate_bwd("gate_dmid", uvp, dz, ln_g, ln_b, wm, bs_full)
    g_win = _mm_tn("gate_dwin", y0, duvp, t1=1024, tn=1024, tm=DW_TOKENS, slot_cols=win.shape[2])
    token = put_grads("gate", [g_win, g_wout.reshape(N_DEV, -1, D)])
    grad_x, dg_mix0 = _mm_nt_rms_bwd("gate_dy", [(duvp, win, *_whole(win))], xs, g_mix[0], dh1, tm=512, nc=512,
                                     after=token, shards=True, emit_bf16=False)

    small_g = dict(
        mix_norm_g=jnp.concatenate([dg_mix0, dg_mix1], axis=0),
        mlp_norm_g=jnp.concatenate([dg_mlp0, dg_mlp1], axis=0),
        final_norm_g=dg_fin[0], a_ln_g=d_lng, a_ln_b=d_lnb, a_w_s=d_wm[None],
        a_b_s=jnp.sum(d_mixed.reshape(CHUNK, GROUPS, D // GROUPS), axis=2).T[None],
        rel_bias=_bias_grad("att_dbias", ds_sums))
    return loss_local, grad_x, small_g
```

```python
import functools
import math

import jax
import jax.numpy as jnp
from jax import lax
from jax.experimental import pallas as pl
from jax.experimental.pallas import tpu as pltpu

F32 = jnp.float32
BF16 = jnp.bfloat16
MESH = pl.DeviceIdType.MESH

N_DEV = 8
EPS = 1e-6
NEG_INF = -1e30
CHUNK = 128
GROUPS = 8
HEAD_DIM = 64
ATT_HEADS = 8
ATT_WIDTH = ATT_HEADS * HEAD_DIM
DILATIONS = (1, 4, 16)
N_DIL = len(DILATIONS)
N_BUCKETS = 32
MAX_EXACT = N_BUCKETS // 2
REL_MAX_DISTANCE = 2048
ATT_ROWS = 2048
ATT_SCALE = HEAD_DIM ** -0.5
DW_TOKENS = 4096
LANES = 128

ADAM_LR = 0.001
ADAM_B1 = 0.9
ADAM_B2 = 0.999
ADAM_EPS = 1e-08
ADAM_WD = 0.01
ADAM_STEP = 10

VMEM_LIMIT_BYTES = 56 * 1024 * 1024


def _params(semantics=None):
    return pltpu.CompilerParams(dimension_semantics=semantics, vmem_limit_bytes=VMEM_LIMIT_BYTES)


def _bf(v):
    return v.astype(BF16)


def _dot(a, b, dims):
    return lax.dot_general(a, b, (dims, ((), ())), preferred_element_type=F32)


NN = ((1,), (0,))
NT = ((1,), (1,))
TN = ((0,), (0,))


def _after_operand(after):
    if after is None:
        return [], []
    return [after], [pl.BlockSpec(memory_space=pl.ANY)]


def _rms_fwd(name, x, g, tm=512, after=None):
    S, D = x.shape
    after_args, after_specs = _after_operand(after)

    def body(x_ref, g_ref, *rest):
        y_ref = rest[-1]
        xv = x_ref[...]
        r = lax.rsqrt(jnp.mean(xv * xv, axis=-1, keepdims=True) + EPS)
        y_ref[...] = _bf(xv * r * g_ref[...])

    return pl.pallas_call(
        body, name=name, grid=(S // tm,),
        in_specs=[pl.BlockSpec((tm, D), lambda i: (i, 0)), pl.BlockSpec((1, D), lambda i: (0, 0))] + after_specs,
        out_specs=pl.BlockSpec((tm, D), lambda i: (i, 0)),
        out_shape=jax.ShapeDtypeStruct((S, D), BF16),
        compiler_params=_params(("parallel",)),
    )(x, g, *after_args)


def _mm_res_loss(name, a, w, res, g, target, *, tm, nc):
    M, D = res.shape

    def body(a_ref, w_ref, r_ref, g_ref, t_ref, dh_ref, dg_ref, l_ref, dhb_ref, h_sc):
        i = pl.program_id(0)
        a_v = _bf(a_ref[...])
        for j in range(D // nc):
            cols, acc = _chunk_product([a_v], [w_ref], j, nc, False, False)
            h_sc[:, cols] = r_ref[:, cols] + acc
        xv = h_sc[...]
        r = lax.rsqrt(jnp.mean(xv * xv, axis=-1, keepdims=True) + EPS)
        xh = xv * r
        gv = g_ref[...]
        e = xh * gv - t_ref[...]
        dout = e / D
        dyg = dout * gv
        c = jnp.mean(dyg * xh, axis=-1, keepdims=True)
        dh = r * (dyg - xh * c)
        dh_ref[...] = dh
        dhb_ref[...] = _bf(dh)
        dg_part = jnp.sum(dout * xh, axis=0, keepdims=True)
        l_part = jnp.sum(e * e, axis=0, keepdims=True)

        @pl.when(i == 0)
        def _():
            dg_ref[...] = dg_part
            l_ref[...] = l_part

        @pl.when(i > 0)
        def _():
            dg_ref[...] += dg_part
            l_ref[...] += l_part

    row = pl.BlockSpec((tm, D), lambda i: (i, 0))
    vec = pl.BlockSpec((1, D), lambda i: (0, 0))
    return pl.pallas_call(
        body, name=name, grid=(M // tm,),
        in_specs=[pl.BlockSpec((tm, a.shape[1]), lambda i: (i, 0)), pl.BlockSpec(w.shape, lambda i: (0, 0)),
                  row, vec, row],
        out_specs=[row, vec, vec, row],
        out_shape=[jax.ShapeDtypeStruct((M, D), F32), jax.ShapeDtypeStruct((1, D), F32),
                   jax.ShapeDtypeStruct((1, D), F32), jax.ShapeDtypeStruct((M, D), BF16)],
        scratch_shapes=[pltpu.VMEM((tm, D), F32)],
        compiler_params=_params(("arbitrary",)),
    )(a, w, res, g, target)


def _chunk_product(a_vals, w_refs, j, nc, nt, shards):
    cols = slice(j * nc, (j + 1) * nc)
    acc = None
    for a_v, w_ref in zip(a_vals, w_refs):
        if not shards:
            terms = [_dot(a_v, w_ref[cols, :], NT) if nt else _dot(a_v, w_ref[:, cols], NN)]
        elif nt:
            nl = w_ref.shape[2]
            terms = [_dot(a_v[:, k * nl:(k + 1) * nl], w_ref[k, cols, :], NT) for k in range(N_DEV)]
        else:
            terms = [_dot(a_v, w_ref[j], NN)]
        for t in terms:
            acc = t if acc is None else acc + t
    return cols, acc


def _mm_rows(name, pairs, n_out, *, nt, tm, nc, epi="plain", extra=None, out_dtype=F32, after=None, shards=False,
             norm_g=None):
    M = pairs[0][0].shape[0]
    np_ = len(pairs)
    after_args, after_specs = _after_operand(after)

    def body(*refs):
        a_refs = refs[:np_]
        w_refs = refs[np_:2 * np_]
        pos = 2 * np_
        e_ref = None
        if extra is not None:
            e_ref = refs[pos]
            pos += 1
        if norm_g is not None:
            g_ref = refs[pos]
            pos += 1
        pos += len(after_args)
        outs = refs[pos:]
        a_vals = [_bf(a[...]) for a in a_refs]
        for j in range(n_out // nc):
            cols, acc = _chunk_product(a_vals, w_refs, j, nc, nt, shards)
            if epi == "plain":
                outs[0][:, cols] = acc.astype(out_dtype)
            elif epi == "res":
                outs[0][:, cols] = e_ref[:, cols] + acc
            elif epi == "relu2":
                outs[0][:, cols] = _bf(acc)
                rl = jnp.maximum(acc, 0.0)
                outs[1][:, cols] = _bf(rl * rl)
            elif epi == "mask2relu":
                outs[0][:, cols] = _bf(acc * (2.0 * jnp.maximum(e_ref[:, cols].astype(F32), 0.0)))
        if norm_g is not None:
            hv = outs[0][...]
            r = lax.rsqrt(jnp.mean(hv * hv, axis=-1, keepdims=True) + EPS)
            outs[1][...] = _bf(hv * r * g_ref[...])

    in_specs = [pl.BlockSpec((tm, a.shape[1]), lambda i: (i, 0)) for a, _, _, _ in pairs]
    for _, _, wshape, widx in pairs:
        in_specs.append(pl.BlockSpec(wshape, functools.partial(lambda i, widx: widx, widx=widx)))
    args = [a for a, _, _, _ in pairs] + [w for _, w, _, _ in pairs]
    if extra is not None:
        in_specs.append(pl.BlockSpec((tm, n_out), lambda i: (i, 0)))
        args.append(extra)
    if norm_g is not None:
        in_specs.append(pl.BlockSpec((1, n_out), lambda i: (0, 0)))
        args.append(norm_g)
    in_specs += after_specs
    args += after_args
    row_out = pl.BlockSpec((tm, n_out), lambda i: (i, 0))
    if epi == "relu2":
        out_specs = [row_out, row_out]
        out_shape = [jax.ShapeDtypeStruct((M, n_out), BF16), jax.ShapeDtypeStruct((M, n_out), BF16)]
    elif norm_g is not None:
        out_specs = [row_out, row_out]
        out_shape = [jax.ShapeDtypeStruct((M, n_out), F32), jax.ShapeDtypeStruct((M, n_out), BF16)]
    else:
        dt = BF16 if epi == "mask2relu" else (F32 if epi == "res" else out_dtype)
        out_specs = row_out
        out_shape = jax.ShapeDtypeStruct((M, n_out), dt)
    return pl.pallas_call(
        body, name=name, grid=(M // tm,), in_specs=in_specs, out_specs=out_specs, out_shape=out_shape,
        compiler_params=_params(("parallel",)),
    )(*args)


def _whole(w):
    return w.shape, (0,) * w.ndim


def _mm_nn(name, a, w, **kw):
    n_out = w.shape[0] * w.shape[2] if w.ndim == 3 else w.shape[1]
    return _mm_rows(name, [(a, w, *_whole(w))], n_out, nt=False, **kw)


def _mm_nt(name, a, w, **kw):
    return _mm_rows(name, [(a, w, *_whole(w))], w.shape[0], nt=True, **kw)


def _mm_nt_rms_bwd(name, pairs, x, g, dres, *, tm, nc, after=None, shards=False, emit_bf16=True):
    M, D = x.shape
    np_ = len(pairs)
    n_out = 3 if emit_bf16 else 2
    after_args, after_specs = _after_operand(after)

    def body(*refs):
        a_refs = refs[:np_]
        w_refs = refs[np_:2 * np_]
        x_ref, g_ref, r_ref = refs[2 * np_:2 * np_ + 3]
        dy_sc = refs[-1]
        outs = refs[-1 - n_out:-1]
        dx_ref, dg_ref = outs[0], outs[1]
        i = pl.program_id(0)
        a_vals = [_bf(a[...]) for a in a_refs]
        for j in range(D // nc):
            cols, acc = _chunk_product(a_vals, w_refs, j, nc, True, shards)
            dy_sc[:, cols] = acc
        xv = x_ref[...]
        r = lax.rsqrt(jnp.mean(xv * xv, axis=-1, keepdims=True) + EPS)
        xh = xv * r
        dy_v = dy_sc[...]
        dyg = dy_v * g_ref[...]
        c = jnp.mean(dyg * xh, axis=-1, keepdims=True)
        dx = r_ref[...] + r * (dyg - xh * c)
        dx_ref[...] = dx
        if emit_bf16:
            outs[2][...] = _bf(dx)
        part = jnp.sum(dy_v * xh, axis=0, keepdims=True)

        @pl.when(i == 0)
        def _():
            dg_ref[...] = part

        @pl.when(i > 0)
        def _():
            dg_ref[...] += part

    row = pl.BlockSpec((tm, D), lambda i: (i, 0))
    vec = pl.BlockSpec((1, D), lambda i: (0, 0))
    in_specs = [pl.BlockSpec((tm, a.shape[1]), lambda i: (i, 0)) for a, _, _, _ in pairs]
    for _, _, wshape, widx in pairs:
        in_specs.append(pl.BlockSpec(wshape, functools.partial(lambda i, widx: widx, widx=widx)))
    args = [a for a, _, _, _ in pairs] + [w for _, w, _, _ in pairs]
    return pl.pallas_call(
        body, name=name, grid=(M // tm,),
        in_specs=in_specs + [row, vec, row] + after_specs,
        out_specs=[row, vec] + [row] * (n_out - 2),
        out_shape=[jax.ShapeDtypeStruct((M, D), F32), jax.ShapeDtypeStruct((1, D), F32)]
        + [jax.ShapeDtypeStruct((M, D), BF16)] * (n_out - 2),
        scratch_shapes=[pltpu.VMEM((tm, D), F32)],
        compiler_params=_params(("arbitrary",)),
    )(*args, x, g, dres, *after_args)


def _mm_tn(name, a, b, *, t1, tn, tm=2048, slot_cols=None):
    M, K1 = a.shape
    N = b.shape[1]
    nm = M // tm

    def body(a_ref, b_ref, o_ref, acc_ref):
        m = pl.program_id(2)
        t = _dot(_bf(a_ref[...]), _bf(b_ref[...]), TN)

        @pl.when(m == 0)
        def _():
            acc_ref[...] = t

        @pl.when(m > 0)
        def _():
            acc_ref[...] += t

        @pl.when(m == nm - 1)
        def _():
            if slot_cols is None:
                o_ref[...] = _bf(acc_ref[...])
            else:
                for k in range(tn // slot_cols):
                    o_ref[k] = _bf(acc_ref[:, k * slot_cols:(k + 1) * slot_cols])

    if slot_cols is not None:
        out_spec = pl.BlockSpec((tn // slot_cols, t1, slot_cols), lambda i, j, m: (j, i, 0))
        out_shape = jax.ShapeDtypeStruct((N // slot_cols, K1, slot_cols), BF16)
    else:
        out_spec = pl.BlockSpec((t1, tn), lambda i, j, m: (i, j))
        out_shape = jax.ShapeDtypeStruct((K1, N), BF16)
    return pl.pallas_call(
        body, name=name, grid=(K1 // t1, N // tn, nm),
        in_specs=[pl.BlockSpec((tm, t1), lambda i, j, m: (m, i)), pl.BlockSpec((tm, tn), lambda i, j, m: (m, j))],
        out_specs=out_spec, out_shape=out_shape,
        scratch_shapes=[pltpu.VMEM((t1, tn), F32)],
        compiler_params=_params(("parallel", "parallel", "arbitrary")),
    )(a, b)


_INV_SQRT2 = 1.0 / math.sqrt(2.0)
_INV_SQRT2PI = 1.0 / math.sqrt(2.0 * math.pi)


def _gelu(x):
    return 0.5 * x * (1.0 + lax.erf(x * _INV_SQRT2))


def _gelu_and_grad(x):
    cdf = 0.5 * (1.0 + lax.erf(x * _INV_SQRT2))
    return x * cdf, cdf + x * (_INV_SQRT2PI * jnp.exp(-0.5 * x * x))


def _layer_norm_parts(v):
    mu = jnp.mean(v, axis=-1, keepdims=True)
    xc = v - mu
    rs = lax.rsqrt(jnp.mean(xc * xc, axis=-1, keepdims=True) + EPS)
    return xc * rs, rs


def _gate_fwd(name, uvp, ln_g, ln_b, wm, bs_full, tr=512):
    S, W2 = uvp.shape
    W = W2 // 2
    gd = W // GROUPS

    def body(u_ref, v_ref, lg_ref, lb_ref, wm_ref, bs_ref, z_ref):
        vh, _ = _layer_norm_parts(_gelu(v_ref[...]))
        vn = _bf(vh * lg_ref[...] + lb_ref[...])
        for ci in range(tr // CHUNK):
            rows = slice(ci * CHUNK, (ci + 1) * CHUNK)
            for g in range(GROUPS):
                cols = slice(g * gd, (g + 1) * gd)
                mixed = _dot(wm_ref[g], vn[rows, cols], NN) + bs_ref[:, cols]
                z_ref[rows, cols] = _bf(_gelu(u_ref[rows, cols]) * mixed)

    vec = pl.BlockSpec((1, W), lambda i: (0, 0))
    return pl.pallas_call(
        body, name=name, grid=(S // tr,),
        in_specs=[pl.BlockSpec((tr, W), lambda i: (i, 0)), pl.BlockSpec((tr, W), lambda i: (i, 1)), vec, vec,
                  pl.BlockSpec((GROUPS, CHUNK, CHUNK), lambda i: (0, 0, 0)),
                  pl.BlockSpec((CHUNK, W), lambda i: (0, 0))],
        out_specs=pl.BlockSpec((tr, W), lambda i: (i, 0)),
        out_shape=jax.ShapeDtypeStruct((S, W), BF16),
        compiler_params=_params(("parallel",)),
    )(uvp, uvp, ln_g, ln_b, wm, bs_full)


def _gate_bwd(name, uvp, dz, ln_g, ln_b, wm, bs_full, tr=256):
    S, W2 = uvp.shape
    W = W2 // 2
    gd = W // GROUPS
    n_steps = S // tr

    def body(u_ref, v_ref, dz_ref, lg_ref, lb_ref, wm_ref, bs_ref, duv_ref, dwm_ref, dmx_ref, dlg_ref, dlb_ref,
             dvn_ref):
        i = pl.program_id(0)
        v, dv_dvp = _gelu_and_grad(v_ref[...])
        vh, rs = _layer_norm_parts(v)
        lg = lg_ref[...]
        vn = _bf(vh * lg + lb_ref[...])

        @pl.when(i == 0)
        def _():
            dwm_ref[...] = jnp.zeros_like(dwm_ref)
            dmx_ref[...] = jnp.zeros_like(dmx_ref)
            dlg_ref[...] = jnp.zeros_like(dlg_ref)
            dlb_ref[...] = jnp.zeros_like(dlb_ref)

        for ci in range(tr // CHUNK):
            rows = slice(ci * CHUNK, (ci + 1) * CHUNK)
            for g in range(GROUPS):
                cols = slice(g * gd, (g + 1) * gd)
                u, du_dup = _gelu_and_grad(u_ref[rows, cols])
                dz_v = dz_ref[rows, cols]
                dmixed = dz_v * u
                dmx_ref[:, cols] += dmixed
                dmixed_b = _bf(dmixed)
                mixed = _dot(wm_ref[g], vn[rows, cols], NN) + bs_ref[:, cols]
                duv_ref[rows, cols] = _bf(dz_v * mixed * du_dup)
                dwm_ref[g] += _dot(dmixed_b, vn[rows, cols], NT)
                dvn_ref[rows, cols] = _dot(wm_ref[g], dmixed_b, TN)
        dvn = dvn_ref[...]
        dlg_ref[...] += jnp.sum(dvn * vh, axis=0, keepdims=True)
        dlb_ref[...] += jnp.sum(dvn, axis=0, keepdims=True)
        dvh = dvn * lg
        dv = rs * (dvh - jnp.mean(dvh, axis=-1, keepdims=True) - vh * jnp.mean(dvh * vh, axis=-1, keepdims=True))
        duv_ref[:, W:] = _bf(dv * dv_dvp)

        @pl.when(i == n_steps - 1)
        def _():
            t_idx = lax.broadcasted_iota(jnp.int32, (CHUNK, CHUNK), 0)
            s_idx = lax.broadcasted_iota(jnp.int32, (CHUNK, CHUNK), 1)
            keep = (s_idx <= t_idx).astype(F32)
            for g in range(GROUPS):
                dwm_ref[g] = dwm_ref[g] * keep

    vec = pl.BlockSpec((1, W), lambda i: (0, 0))
    row = pl.BlockSpec((tr, W), lambda i: (i, 0))
    return pl.pallas_call(
        body, name=name, grid=(n_steps,),
        in_specs=[row, pl.BlockSpec((tr, W), lambda i: (i, 1)), row, vec, vec,
                  pl.BlockSpec((GROUPS, CHUNK, CHUNK), lambda i: (0, 0, 0)),
                  pl.BlockSpec((CHUNK, W), lambda i: (0, 0))],
        out_specs=[pl.BlockSpec((tr, W2), lambda i: (i, 0)),
                   pl.BlockSpec((GROUPS, CHUNK, CHUNK), lambda i: (0, 0, 0)),
                   pl.BlockSpec((CHUNK, W), lambda i: (0, 0)), vec, vec],
        out_shape=[jax.ShapeDtypeStruct((S, W2), BF16), jax.ShapeDtypeStruct((GROUPS, CHUNK, CHUNK), F32),
                   jax.ShapeDtypeStruct((CHUNK, W), F32), jax.ShapeDtypeStruct((1, W), F32),
                   jax.ShapeDtypeStruct((1, W), F32)],
        scratch_shapes=[pltpu.VMEM((tr, W), F32)],
        compiler_params=_params(("arbitrary",)),
    )(uvp, uvp, dz, ln_g, ln_b, wm, bs_full)


def _t5_bucket(distance):
    small = distance < MAX_EXACT
    nf = jnp.maximum(distance, 1).astype(F32)
    large = MAX_EXACT + (jnp.log(nf / MAX_EXACT) / math.log(REL_MAX_DISTANCE / MAX_EXACT)
                         * (N_BUCKETS - MAX_EXACT)).astype(jnp.int32)
    large = jnp.minimum(large, N_BUCKETS - 1)
    return jnp.where(small, distance, large)


TILE_ELEMS = 2 * CHUNK * CHUNK


def _band_buckets():
    rel = CHUNK + jnp.arange(CHUNK)[None, :] - jnp.arange(2 * CHUNK)[:, None]
    band = (rel >= 0) & (rel <= CHUNK)
    buckets = [_t5_bucket(jnp.clip(rel, 0, CHUNK) * d) for d in DILATIONS]
    return jnp.stack(buckets), band


def _bucket_onehot():
    buckets, _ = _band_buckets()
    return (buckets.reshape(N_DIL, 1, TILE_ELEMS) == jnp.arange(N_BUCKETS)[None, :, None]).astype(F32)


def _bias_tiles(name, rel_bias, after=None):
    _, band = _band_buckets()
    own = band & (jnp.arange(2 * CHUNK) >= CHUNK)[:, None]
    masks = jnp.stack([own, band]).reshape(2, TILE_ELEMS).astype(F32)
    tables = jnp.transpose(rel_bias.reshape(N_BUCKETS, N_DIL, ATT_HEADS), (1, 2, 0))
    after_args, after_specs = _after_operand(after)

    def body(t_ref, oh_ref, m_ref, *rest):
        out_ref = rest[-1]
        for g in range(N_DIL):
            bias = lax.dot_general(t_ref[g], oh_ref[g], (NN, ((), ())), precision=lax.Precision.HIGHEST,
                                   preferred_element_type=F32)
            for f in range(2):
                out_ref[g, f] = jnp.where(m_ref[f:f + 1, :] > 0.5, bias, NEG_INF)

    whole = pl.BlockSpec(memory_space=pltpu.VMEM)
    out = pl.pallas_call(
        body, name=name, out_shape=jax.ShapeDtypeStruct((N_DIL, 2, ATT_HEADS, TILE_ELEMS), F32),
        in_specs=[whole, whole, whole] + after_specs, out_specs=whole,
        compiler_params=_params(),
    )(tables, _bucket_onehot(), masks, *after_args)
    out = out.reshape(N_DIL, 2, ATT_HEADS // 2, 2, 2 * CHUNK, CHUNK)
    return jnp.transpose(out, (0, 1, 2, 4, 3, 5)).reshape(N_DIL, 2, ATT_HEADS // 2, 2 * CHUNK, 2 * CHUNK)


def _att_specs(order):
    def column(part, ids):
        hp, g, _ = order(*ids)
        return part * 3 * 4 + g * 4 + hp

    def window(part):
        def index(*ids):
            c = order(*ids)[2]
            return pl.multiple_of(jnp.maximum(c - 1, 0) * ATT_ROWS, ATT_ROWS), column(part, ids) * LANES
        return pl.BlockSpec((pl.Element(2 * ATT_ROWS), pl.Element(LANES)), index)

    return [pl.BlockSpec((ATT_ROWS, LANES), lambda *ids: (order(*ids)[2], column(0, ids))), window(1), window(2)]


def _window_base(c):
    return jnp.where(c == 0, 0, ATT_ROWS)


def _rows(start, d):
    if d == 1:
        return pl.ds(pl.multiple_of(start, CHUNK), CHUNK)
    return pl.ds(start, CHUNK, stride=d)


def _att_tile_offsets(t, d):
    n = t // d
    r = t % d
    return n * (CHUNK * d) + r, n


def _head_pair_columns(x_t):
    zeros = jnp.zeros((HEAD_DIM, CHUNK), x_t.dtype)
    return jnp.concatenate([jnp.concatenate([x_t[:HEAD_DIM], zeros], axis=0),
                            jnp.concatenate([zeros, x_t[HEAD_DIM:]], axis=0)], axis=1)


def _head_pair_rows(y):
    return jnp.concatenate([y[:HEAD_DIM, :CHUNK], y[HEAD_DIM:, CHUNK:]], axis=0)


def _att_fwd(name, qkv, bias_tiles):
    S = qkv.shape[0]
    n_chunks = S // ATT_ROWS
    tiles = ATT_ROWS // CHUNK

    def body(q_ref, kk, vv, b_ref, out_ref, lse_ref, o_sc, l_sc):
        c = pl.program_id(1)
        g = pl.program_id(2)
        base = _window_base(c)

        for gi, d in enumerate(DILATIONS):
            @pl.when(g == gi)
            def _(gi=gi, d=d):
                span = CHUNK * d

                def tile(t, carry):
                    q0, n = _att_tile_offsets(t, d)
                    first = (c == 0) & (n == 0)
                    rows = _rows(q0, d)
                    cur = _rows(base + q0, d)
                    prev = _rows(jnp.where(first, q0, base + q0 - span), d)
                    inner = jnp.where(first, 0, 1)
                    qq = _head_pair_columns(_bf(q_ref[rows, :] * ATT_SCALE).T)
                    s_p = _dot(_bf(kk[prev, :]), qq, NN) + b_ref[inner, 0:CHUNK, :]
                    s_c = _dot(_bf(kk[cur, :]), qq, NN) + b_ref[inner, CHUNK:2 * CHUNK, :]
                    m = jnp.maximum(jnp.max(s_p, axis=0, keepdims=True), jnp.max(s_c, axis=0, keepdims=True))
                    p_p = jnp.exp(s_p - m)
                    p_c = jnp.exp(s_c - m)
                    l = jnp.sum(p_p, axis=0, keepdims=True) + jnp.sum(p_c, axis=0, keepdims=True)
                    o2 = (_dot(_bf(vv[prev, :]).T, _bf(p_p), NN)
                          + _dot(_bf(vv[cur, :]).T, _bf(p_c), NN)) * (1.0 / l)
                    lse = m + jnp.log(l)
                    l_t = jnp.concatenate([jnp.broadcast_to(lse[:, :CHUNK], (HEAD_DIM, CHUNK)),
                                           jnp.broadcast_to(lse[:, CHUNK:], (HEAD_DIM, CHUNK))], axis=0)
                    o_sc[gi, rows, :] = _head_pair_rows(o2).T
                    l_sc[gi, rows, :] = l_t.T
                    return carry

                lax.fori_loop(0, tiles, tile, 0, unroll=16)

        @pl.when(g == N_DIL - 1)
        def _():
            for rows in (slice(i * 4 * CHUNK, (i + 1) * 4 * CHUNK) for i in range(ATT_ROWS // (4 * CHUNK))):
                ls = [l_sc[gi, rows, :] for gi in range(N_DIL)]
                mx = functools.reduce(jnp.maximum, ls)
                ws = [jnp.exp(l - mx) for l in ls]
                tot = functools.reduce(lambda a, b: a + b, ws)
                acc = ws[0] * o_sc[0, rows, :]
                for gi in range(1, N_DIL):
                    acc = acc + ws[gi] * o_sc[gi, rows, :]
                out_ref[rows, :] = acc / tot
                lse_ref[rows, :] = mx + jnp.log(tot)

    order = lambda hp, c, g: (hp, g, c)
    out_spec = pl.BlockSpec((ATT_ROWS, LANES), lambda hp, c, g: (c, hp))
    shape = jax.ShapeDtypeStruct((S, ATT_WIDTH), F32)
    return pl.pallas_call(
        body, name=name, grid=(ATT_HEADS // 2, n_chunks, N_DIL),
        in_specs=_att_specs(order) + [
            pl.BlockSpec((None, 2, None, 2 * CHUNK, 2 * CHUNK), lambda hp, c, g: (g, 0, hp, 0, 0))],
        out_specs=[out_spec, out_spec],
        out_shape=[shape, shape],
        scratch_shapes=[pltpu.VMEM((N_DIL, ATT_ROWS, LANES), F32), pltpu.VMEM((N_DIL, ATT_ROWS, LANES), F32)],
        compiler_params=_params(("parallel", "parallel", "arbitrary")),
    )(qkv, qkv, qkv, bias_tiles)


def _att_bwd(name, qkv, o, lse, d_o, bias_tiles):
    S = qkv.shape[0]
    n_chunks = S // ATT_ROWS
    tiles = ATT_ROWS // CHUNK

    def body(q_ref, kk, vv, o_ref, l_ref, do_ref, b_ref, dq_out, dk_out, dv_out, ds_ref, dq_ref, dk_ref, dv_ref):
        g = pl.program_id(1)
        c = pl.program_id(2)

        @pl.when(c == 0)
        def _():
            dk_ref[...] = jnp.zeros_like(dk_ref)
            dv_ref[...] = jnp.zeros_like(dv_ref)
            ds_ref[...] = jnp.zeros_like(ds_ref)

        base = _window_base(c)
        first_row = c * ATT_ROWS
        head0 = lax.broadcasted_iota(jnp.int32, (CHUNK, LANES), 1) < HEAD_DIM

        def head_pair_stack(x):
            zero = jnp.zeros_like(x)
            return jnp.concatenate([jnp.where(head0, x, zero), jnp.where(head0, zero, x)], axis=0)

        for gi, d in enumerate(DILATIONS):
            @pl.when(g == gi)
            def _(d=d):
                span = CHUNK * d

                def tile(t, carry):
                    q0, n = _att_tile_offsets(t, d)
                    first = (c == 0) & (n == 0)
                    rows = _rows(q0, d)
                    cur = _rows(base + q0, d)
                    prev = _rows(jnp.where(first, q0, base + q0 - span), d)
                    inner = jnp.where(first, 0, 1)
                    g_cur = _rows(first_row + q0, d)
                    g_prev = _rows(jnp.where(first, q0, first_row + q0 - span), d)
                    q2 = _bf(q_ref[rows, :] * ATT_SCALE)
                    q_t = q2.T
                    k2 = _bf(jnp.concatenate([kk[prev, :], kk[cur, :]], axis=0))
                    k_t = k2.T
                    v2 = _bf(jnp.concatenate([vv[prev, :], vv[cur, :]], axis=0))
                    do2 = do_ref[rows, :]
                    do_b = _bf(do2)
                    do_t = do_b.T
                    lse_t = l_ref[rows, :].T
                    dd_t = (do2 * o_ref[rows, :]).T
                    lse = jnp.concatenate([lse_t[0:1], lse_t[HEAD_DIM:HEAD_DIM + 1]], axis=1)
                    delta = jnp.concatenate([jnp.sum(dd_t[:HEAD_DIM], axis=0, keepdims=True),
                                             jnp.sum(dd_t[HEAD_DIM:], axis=0, keepdims=True)], axis=1)
                    s = _dot(k2, _head_pair_columns(q_t), NN) + b_ref[inner]
                    p = jnp.exp(s - lse)
                    ds = p * (_dot(v2, _head_pair_columns(do_t), NN) - delta)
                    ds_ref[...] += ds
                    ds_b = _bf(ds)
                    dq_t = _head_pair_rows(_dot(k_t, ds_b, NN))
                    dk2 = _dot(ds_b, head_pair_stack(q2), NN)
                    dv2 = _dot(_bf(p), head_pair_stack(do_b), NN)
                    dq_ref[rows, :] = (dq_t * ATT_SCALE).T
                    dk_ref[g_prev, :] += dk2[0:CHUNK]
                    dk_ref[g_cur, :] += dk2[CHUNK:2 * CHUNK]
                    dv_ref[g_prev, :] += dv2[0:CHUNK]
                    dv_ref[g_cur, :] += dv2[CHUNK:2 * CHUNK]
                    return carry

                lax.fori_loop(0, tiles, tile, 0, unroll=16)

        dq_out[...] = _bf(dq_ref[...])

        @pl.when(c == n_chunks - 1)
        def _():
            dk_out[...] = _bf(dk_ref[...])
            dv_out[...] = _bf(dv_ref[...])

    order = lambda hp, g, c: (hp, g, c)
    chunk = pl.BlockSpec((ATT_ROWS, LANES), lambda hp, g, c: (c, hp))
    slab = pl.BlockSpec((S, LANES), lambda hp, g, c: (0, g * 4 + hp))
    width = N_DIL * ATT_WIDTH
    dq, dk, dv, ds_sums = pl.pallas_call(
        body, name=name, grid=(ATT_HEADS // 2, N_DIL, n_chunks),
        in_specs=_att_specs(order) + [chunk, chunk, chunk,
                                      pl.BlockSpec((None, 2, None, 2 * CHUNK, 2 * CHUNK),
                                                   lambda hp, g, c: (g, 0, hp, 0, 0))],
        out_specs=[pl.BlockSpec((ATT_ROWS, LANES), lambda hp, g, c: (c, g * 4 + hp)), slab, slab,
                   pl.BlockSpec((None, None, 2 * CHUNK, 2 * CHUNK), lambda hp, g, c: (g, hp, 0, 0))],
        out_shape=[jax.ShapeDtypeStruct((S, width), BF16), jax.ShapeDtypeStruct((S, width), BF16),
                   jax.ShapeDtypeStruct((S, width), BF16),
                   jax.ShapeDtypeStruct((N_DIL, ATT_HEADS // 2, 2 * CHUNK, 2 * CHUNK), F32)],
        scratch_shapes=[pltpu.VMEM((ATT_ROWS, LANES), F32), pltpu.VMEM((S, LANES), F32),
                        pltpu.VMEM((S, LANES), F32)],
        compiler_params=_params(("parallel", "parallel", "arbitrary")),
    )(qkv, qkv, qkv, o, lse, d_o, bias_tiles)
    ds_sums = ds_sums.reshape(N_DIL, ATT_HEADS // 2, 2 * CHUNK, 2, CHUNK)
    ds_sums = jnp.transpose(ds_sums, (0, 1, 3, 2, 4)).reshape(N_DIL, ATT_HEADS, 2 * CHUNK, CHUNK)
    return dq, dk, dv, ds_sums


def _bias_grad(name, ds_sums):
    flat = ds_sums.reshape(N_DIL, ATT_HEADS, TILE_ELEMS)

    def body(oh_ref, ds_ref, out_ref):
        for g in range(N_DIL):
            out_ref[g] = lax.dot_general(oh_ref[g], ds_ref[g], (NT, ((), ())), precision=lax.Precision.HIGHEST,
                                         preferred_element_type=F32)

    out = pl.pallas_call(
        body, name=name, out_shape=jax.ShapeDtypeStruct((N_DIL, N_BUCKETS, ATT_HEADS), F32),
        compiler_params=_params(),
    )(_bucket_onehot(), flat)
    return jnp.transpose(out, (1, 0, 2)).reshape(N_BUCKETS, N_DIL * ATT_HEADS)


def _peers():
    x, y, c = lax.axis_index("x"), lax.axis_index("y"), lax.axis_index("c")
    me = 4 * x + 2 * y + c
    others = [(x, y, 1 - c), (1 - x, y, c), (x, 1 - y, c), (1 - x, 1 - y, c),
              (1 - x, y, 1 - c), (x, 1 - y, 1 - c), (1 - x, 1 - y, 1 - c)]
    return me, others


def _slot(dev):
    return 4 * dev[0] + 2 * dev[1] + dev[2]


_HBM =pl.BlockSpec(memory_space=pltpu.HBM)
_SEM = pl.BlockSpec(memory_space=pltpu.SEMAPHORE)
_EFFECT = pltpu.SideEffectType.DATAFLOW_SIDE_EFFECTING


def _exchange_copy(src_ref, land_ref, send_sems, recv_sems, k, dev, me, scatter, arriving):
    src = src_ref.at[me if arriving else _slot(dev)] if scatter else src_ref
    dst = land_ref.at[_slot(dev) if arriving else me]
    return pltpu.make_async_remote_copy(src_ref=src, dst_ref=dst, send_sem=send_sems.at[k], recv_sem=recv_sems.at[k],
                                        device_id=dev, device_id_type=MESH)


def _own_copy(src_ref, land_ref, local_sems, p, me, scatter):
    return pltpu.make_async_copy(src_ref.at[me] if scatter else src_ref, land_ref.at[me], local_sems.at[p])


def _exchange_start(name, srcs, scatter):
    n = len(srcs)
    landings = [lax.empty((N_DEV,) + src.shape[-2:], src.dtype) for src in srcs]

    def body(*refs):
        src_refs, land_refs = refs[:n], refs[n:2 * n]
        send_sems, recv_sems, local_sems = refs[2 * n:2 * n + 3]
        token = refs[-1]
        me, others = _peers()
        for p in range(n):
            _own_copy(src_refs[p], land_refs[p], local_sems, p, me, scatter).start()
            for k, dev in enumerate(others):
                _exchange_copy(src_refs[p], land_refs[p], send_sems, recv_sems, p * (N_DEV - 1) + k, dev, me,
                               scatter, False).start()
        token[...] = jnp.zeros_like(token)

    sems = pltpu.SemaphoreType.DMA((n * (N_DEV - 1),))
    hbm = lambda a: pltpu.with_memory_space_constraint(a, pltpu.HBM)
    outs = pl.pallas_call(
        body, name=name,
        out_shape=(sems, sems, pltpu.SemaphoreType.DMA((n,)), *[pltpu.HBM(a.shape, a.dtype) for a in srcs + landings],
                   jax.ShapeDtypeStruct((8, LANES), F32)),
        in_specs=(_HBM,) * (2 * n),
        out_specs=(_SEM, _SEM, _SEM) + (_HBM,) * (2 * n) + (pl.BlockSpec(memory_space=pltpu.VMEM),),
        input_output_aliases={i: 3 + i for i in range(2 * n)},
        compiler_params=pltpu.CompilerParams(has_side_effects=_EFFECT),
    )(*[hbm(a) for a in srcs + landings])
    return (outs[0], outs[1], outs[2], list(outs[3:3 + n]), list(outs[3 + n:3 + 2 * n]), scatter), outs[-1]


def _exchange_wait(name, handle, after):
    send_sems, recv_sems, local_sems, src_thru, land_thru, scatter = handle
    n = len(src_thru)

    def body(*refs):
        src_refs, land_refs = refs[:n], refs[n:2 * n]
        send_sems, recv_sems, local_sems = refs[2 * n:2 * n + 3]
        me, others = _peers()
        for p in range(n):
            _own_copy(src_refs[p], land_refs[p], local_sems, p, me, scatter).wait()
            for k, dev in enumerate(others):
                cp = _exchange_copy(src_refs[p], land_refs[p], send_sems, recv_sems, p * (N_DEV - 1) + k, dev, me,
                                    scatter, True)
                cp.wait_send()
                cp.wait_recv()

    outs = pl.pallas_call(
        body, name=name,
        out_shape=tuple(pltpu.HBM(a.shape, a.dtype) for a in src_thru + land_thru),
        in_specs=(_HBM,) * (2 * n) + (_SEM, _SEM, _SEM, pl.BlockSpec(memory_space=pl.ANY)),
        out_specs=(_HBM,) * (2 * n),
        input_output_aliases={i: i for i in range(2 * n)},
        compiler_params=pltpu.CompilerParams(has_side_effects=_EFFECT),
    )(*src_thru, *land_thru, send_sems, recv_sems, local_sems, after)
    return list(outs[n:])


def _adamw_math(w, g, m, v):
    m = ADAM_B1 * m + (1.0 - ADAM_B1) * g
    v = ADAM_B2 * v + (1.0 - ADAM_B2) * (g * g)
    m_hat = m / (1.0 - ADAM_B1 ** ADAM_STEP)
    v_hat = v / (1.0 - ADAM_B2 ** ADAM_STEP)
    delta = -ADAM_LR * (m_hat / (jnp.sqrt(v_hat) + ADAM_EPS) + ADAM_WD * w)
    return delta, m, v


_SMALL_WIDE = (("mix_norm_g", 2), ("mlp_norm_g", 2), ("final_norm_g", 1), ("a_ln_g", 1), ("a_ln_b", 1))
_SMALL_NARROW = (("a_w_s", GROUPS * CHUNK), ("a_b_s", GROUPS), ("rel_bias", N_BUCKETS))
_SMALL = tuple(n for n, _ in _SMALL_WIDE + _SMALL_NARROW)
_BIAS_COLS = N_DIL * ATT_HEADS


def _pack_small_grads(grads, loss_term):
    D = grads["a_ln_g"].shape[-1]
    tiles = [jnp.pad(grads[n].reshape(k, D), ((0, 8 - k), (0, 0))) for n, k in _SMALL_WIDE]
    tiles.append(jnp.pad(loss_term.reshape(1, 1), ((0, 7), (0, D - 1))))
    narrow = [grads["a_w_s"].reshape(-1, LANES), grads["a_b_s"].reshape(-1, LANES),
              jnp.pad(grads["rel_bias"], ((0, 0), (0, LANES - _BIAS_COLS)))]
    return jnp.concatenate(tiles, axis=0), jnp.concatenate(narrow, axis=0)


def _adamw_small(name, g_wide, g_narrow, w, m, v):
    D = g_wide.shape[-1]
    shapes = {n: (k, D) for n, k in _SMALL_WIDE}
    shapes.update({n: (k, LANES) for n, k in _SMALL_NARROW})
    shapes["rel_bias"] = (N_BUCKETS, _BIAS_COLS)
    n_t = len(_SMALL)

    def body(gw_ref, gn_ref, *rest):
        params = rest[:3 * n_t]
        outs = rest[3 * n_t:3 * n_t + 4 * n_t]
        loss_ref, sw, sn = rest[-3:]
        sw[...] = functools.reduce(lambda a, b: a + b, [gw_ref[j] for j in range(N_DEV)])
        sn[...] = functools.reduce(lambda a, b: a + b, [gn_ref[j] for j in range(N_DEV)])
        row = 0
        for i, n in enumerate(_SMALL):
            k, cols = shapes[n]
            if i < len(_SMALL_WIDE):
                g = sw[8 * i:8 * i + k, :]
            else:
                g = sn[row:row + k, 0:cols]
                row += k
            w_ref, m_ref, v_ref = params[3 * i:3 * i + 3]
            delta, m_new, v_new = _adamw_math(w_ref[...], g, m_ref[...], v_ref[...])
            for out, val in zip(outs[4 * i:4 * i + 4], (g, delta, m_new, v_new)):
                out[...] = val
        loss_ref[...] = sw[8 * len(_SMALL_WIDE):8 * len(_SMALL_WIDE) + 8, 0:LANES]

    whole = pl.BlockSpec(memory_space=pltpu.VMEM)
    args = [t[n].reshape(shapes[n]) for n in _SMALL for t in (w, m, v)]
    res = pl.pallas_call(
        body, name=name,
        in_specs=[whole] * (2 + len(args)), out_specs=[whole] * (4 * n_t + 1),
        out_shape=[jax.ShapeDtypeStruct(shapes[n], F32) for n in _SMALL for _ in range(4)]
        + [jax.ShapeDtypeStruct((8, LANES), F32)],
        scratch_shapes=[pltpu.VMEM(g_wide.shape[1:], F32), pltpu.VMEM(g_narrow.shape[1:], F32)],
        compiler_params=_params(),
    )(g_wide, g_narrow, *args)
    small = {n: tuple(r.reshape(w[n].shape) for r in res[4 * i:4 * i + 4]) for i, n in enumerate(_SMALL)}
    return small, res[-1][0, 0]


def _adamw_shard(name, parts, w, m, v, layer, earlier=None, after=None, tr=256):
    L, K, N = w.shape
    tr = min(tr, K)
    n_prev = 0 if earlier is None else 4
    after_args, after_specs = _after_operand(after)

    def body(p_ref, w_ref, m_ref, v_ref, *rest):
        g_out, d_out, m_out, v_out = rest[n_prev + len(after_args):]
        g = p_ref[0].astype(F32)
        for j in range(1, N_DEV):
            g = g + p_ref[j].astype(F32)
        delta, m_new, v_new = _adamw_math(w_ref[...], g, m_ref[...], v_ref[...])
        g_out[...] = g
        d_out[...] = delta
        m_out[...] = m_new
        v_out[...] = v_new

    row = pl.BlockSpec((None, tr, N), lambda i: (layer, i, 0))
    shape = jax.ShapeDtypeStruct((L, K, N), F32)
    return pl.pallas_call(
        body, name=name, grid=(K // tr,),
        in_specs=[pl.BlockSpec((N_DEV, tr, N), lambda i: (0, i, 0)), row, row, row]
        + [pl.BlockSpec(memory_space=pl.ANY)] * n_prev + after_specs,
        out_specs=[row, row, row, row],
        out_shape=[shape, shape, shape, shape],
        input_output_aliases={4 + j: j for j in range(n_prev)},
        compiler_params=_params(("parallel",)),
    )(parts, w, m, v, *(earlier or ()), *after_args)


def _column_slots(full):
    K, N = full.shape
    return jnp.transpose(full.reshape(K, N_DEV, N // N_DEV), (1, 0, 2))


def _from_column_slots(slots):
    _, K, n = slots.shape
    return jnp.transpose(slots, (1, 0, 2)).reshape(K, N_DEV * n)


_STAGES = (("gate", ("a_w_in", "a_w_out"), 0),
           ("mlp0", ("w_up", "w_down"), 0),
           ("att", ("b_w_qkv", "b_w_out"), 0),
           ("mlp1", ("w_up", "w_down"), 1))


def kernel(x, mix_norm_g, mlp_norm_g, final_norm_g, a_w_in, a_ln_g, a_ln_b, a_w_s, a_b_s, a_w_out, b_w_qkv, b_w_out, rel_bias, w_up, w_down, loss_target, m_mix_norm_g, m_mlp_norm_g, m_final_norm_g, m_a_w_in, m_a_ln_g, m_a_ln_b, m_a_w_s, m_a_b_s, m_a_w_out, m_b_w_qkv, m_b_w_out, m_rel_bias, m_w_up, m_w_down, v_mix_norm_g, v_mlp_norm_g, v_final_norm_g, v_a_w_in, v_a_ln_g, v_a_ln_b, v_a_w_s, v_a_b_s, v_a_w_out, v_b_w_qkv, v_b_w_out, v_rel_bias, v_w_up, v_w_down):
    w = dict(mix_norm_g=mix_norm_g, mlp_norm_g=mlp_norm_g, final_norm_g=final_norm_g, a_w_in=a_w_in, a_ln_g=a_ln_g,
             a_ln_b=a_ln_b, a_w_s=a_w_s, a_b_s=a_b_s, a_w_out=a_w_out, b_w_qkv=b_w_qkv, b_w_out=b_w_out,
             rel_bias=rel_bias, w_up=w_up, w_down=w_down)
    m = dict(mix_norm_g=m_mix_norm_g, mlp_norm_g=m_mlp_norm_g, final_norm_g=m_final_norm_g, a_w_in=m_a_w_in,
             a_ln_g=m_a_ln_g, a_ln_b=m_a_ln_b, a_w_s=m_a_w_s, a_b_s=m_a_b_s, a_w_out=m_a_w_out, b_w_qkv=m_b_w_qkv,
             b_w_out=m_b_w_out, rel_bias=m_rel_bias, w_up=m_w_up, w_down=m_w_down)
    v = dict(mix_norm_g=v_mix_norm_g, mlp_norm_g=v_mlp_norm_g, final_norm_g=v_final_norm_g, a_w_in=v_a_w_in,
             a_ln_g=v_a_ln_g, a_ln_b=v_a_ln_b, a_w_s=v_a_w_s, a_b_s=v_a_b_s, a_w_out=v_a_w_out, b_w_qkv=v_b_w_qkv,
             b_w_out=v_b_w_out, rel_bias=v_rel_bias, w_up=v_w_up, w_down=v_w_down)

    stages = {s: (names, layer) for s, names, layer in _STAGES}
    order = [s for s, _, _ in _STAGES]

    def shards_of(stage):
        names, layer = stages[stage]
        return [_bf(w[n][layer]) for n in names]

    pending = {}
    pending[order[0]], first_token = _exchange_start("gather_" + order[0] + "_start", shards_of(order[0]), False)

    def get_weights(stage, dep):
        gathered = _exchange_wait("gather_" + stage + "_wait", pending.pop(stage), dep)
        nxt = order.index(stage) + 1
        token = None
        if nxt < len(order):
            shards, gathered = lax.optimization_barrier((shards_of(order[nxt]), gathered))
            pending[order[nxt]], token = _exchange_start("gather_" + order[nxt] + "_start", shards, False)
        return gathered, token

    sent = {}

    def put_grads(stage, slot_grads):
        sent[stage], token = _exchange_start("scatter_" + stage + "_start", slot_grads, True)
        return token

    loss_local, grad_x, small_g = _local_step(
        x[0], loss_target[0], mix_norm_g, mlp_norm_g, final_norm_g, a_ln_g, a_ln_b, a_w_s, a_b_s, rel_bias,
        get_weights, put_grads, first_token)

    small_sent, token = _exchange_start("gather_small_start", list(_pack_small_grads(small_g, loss_local)), False)

    results = {}
    prev = token
    for stage in reversed(order):
        names, layer = stages[stage]
        received = _exchange_wait("scatter_" + stage + "_wait", sent[stage], prev)
        for n, parts in zip(names, received):
            results[n] = _adamw_shard("adamw_%s_%s" % (stage, n), parts, w[n], m[n], v[n], layer, results.get(n),
                                      after=prev)
            prev = results[n][0]

    g_wide, g_narrow = _exchange_wait("gather_small_wait", small_sent, prev)
    small, loss = _adamw_small("adamw_small", g_wide, g_narrow, w, m, v)

    outs = []
    for j in range(4):
        outs.extend(small[n][j] if n in _SMALL else results[n][j] for n in w)
    return (loss, grad_x[None], *outs)


def _local_step(xs, tgt, mix_norm_g, mlp_norm_g, final_norm_g, a_ln_g, a_ln_b, a_w_s, a_b_s, rel_bias,
                get_weights, put_grads, first_token=None):
    D = xs.shape[-1]
    g_mix = [mix_norm_g[l][None, :] for l in range(2)]
    g_mlp = [mlp_norm_g[l][None, :] for l in range(2)]
    g_fin = final_norm_g[None, :]
    ln_g, ln_b = a_ln_g, a_ln_b
    causal = jnp.tril(jnp.ones((CHUNK, CHUNK), dtype=bool))
    wm = _bf(jnp.where(causal[None], a_w_s[0], 0.0))
    bs_full = jnp.repeat(a_b_s[0].T, D // GROUPS, axis=1)
    bias_tiles = _bias_tiles("att_bias", rel_bias, after=first_token)

    (win, wout), token = get_weights("gate", bias_tiles)
    wout = wout.reshape(-1, D)
    y0 = _rms_fwd("rms_mix0", xs, g_mix[0], after=token)
    uvp = _mm_nn("gate_in", y0, win, tm=512, nc=win.shape[2], shards=True)
    z = _gate_fwd("gate_mid", uvp, ln_g, ln_b, wm, bs_full)
    h1, y1 = _mm_nn("gate_out", z, wout, tm=512, nc=512, epi="res", extra=xs, norm_g=g_mlp[0])
    (wup0, wdn0), token = get_weights("mlp0", h1)
    wdn0 = wdn0.reshape(-1, D)
    a0, f0 = _mm_nn("mlp0_up", y1, wup0, tm=512, nc=wup0.shape[2], epi="relu2", shards=True, after=token)
    h2, y2 = _mm_nn("mlp0_down", f0, wdn0, tm=512, nc=512, epi="res", extra=h1, norm_g=g_mix[1])
    (wqkv, wo), token = get_weights("att", h2)
    wqkv, wo = _from_column_slots(wqkv), _from_column_slots(wo)
    qkv = _mm_nn("att_qkv", y2, wqkv, tm=512, nc=512, after=token)
    o_att, lse = _att_fwd("att_fwd", qkv, bias_tiles)
    h3, y3 = _mm_nn("att_out", o_att, wo, tm=512, nc=512, epi="res", extra=h2, norm_g=g_mlp[1])
    (wup1, wdn1), _ = get_weights("mlp1", h3)
    wdn1 = wdn1.reshape(-1, D)
    a1, f1 = _mm_nn("mlp1_up", y3, wup1, tm=512, nc=wup1.shape[2], epi="relu2", shards=True)
    dh, dg_fin, err2, dh_b = _mm_res_loss("mlp1_down_loss", f1, wdn1, h3, g_fin, tgt, tm=512, nc=512)
    loss_local = 0.5 * jnp.sum(err2) / D

    def mlp_bwd(tag, dh, dh_b, h_in, y, a, f, wup_l, wdn_l, g_row, after):
        da = _mm_nt(tag + "_dact", dh_b, wdn_l, tm=512, nc=512, epi="mask2relu", extra=a, after=after)
        g_dn = _mm_tn(tag + "_dwdown", f, dh_b, t1=1024, tn=1024, tm=DW_TOKENS)
        g_up = _mm_tn(tag + "_dwup", y, da, t1=1024, tn=1024, tm=DW_TOKENS, slot_cols=wup_l.shape[2])
        dh_in, dg, dh_in_b = _mm_nt_rms_bwd(tag + "_dy", [(da, wup_l, *_whole(wup_l))], h_in, g_row, dh, tm=512,
                                            nc=512, shards=True)
        return dh_in, dh_in_b, dg, put_grads(tag, [g_up, g_dn.reshape(N_DEV, -1, D)])

    dh3, dh3_b, dg_mlp1, token = mlp_bwd("mlp1", dh, dh_b, h3, y3, a1, f1, wup1, wdn1, g_mlp[1], None)

    d_o = _mm_nt("att_dout", dh3_b, wo, tm=512, nc=512, after=token)
    g_wo = _mm_tn("att_dwo", o_att, dh3_b, t1=512, tn=1024, tm=DW_TOKENS)
    dq, dk, dv, ds_sums = _att_bwd("att_bwd", qkv, o_att, lse, d_o, bias_tiles)
    part_w = N_DIL * ATT_WIDTH
    g_qkv = [_mm_tn("att_dwqkv%d" % p, y2, t, t1=1024, tn=part_w) for p, t in enumerate((dq, dk, dv))]
    dh2, dg_mix1, dh2_b = _mm_nt_rms_bwd(
        "att_dy", [(t, wqkv, (D, part_w), (0, p)) for p, t in enumerate((dq, dk, dv))], h2, g_mix[1], dh3, tm=512,
        nc=512)
    token = put_grads("att", [_column_slots(jnp.concatenate(g_qkv, axis=1)), _column_slots(g_wo)])

    dh1, dh1_b, dg_mlp0, token = mlp_bwd("mlp0", dh2, dh2_b, h1, y1, a0, f0, wup0, wdn0, g_mlp[0], token)

    dz = _mm_nt("gate_dz", dh1_b, wout, tm=512, nc=512, after=token)
    g_wout = _mm_tn("gate_dwout", z, dh1_b, t1=1024, tn=1024, tm=DW_TOKENS)
    duvp, d_wm, d_mixed, d_lng, d_lnb = _gate_bwd("gate_dmid", uvp, dz, ln_g, ln_b, wm, bs_full)
    g_win = _mm_tn("gate_dwin", y0, duvp, t1=1024, tn=1024, tm=DW_TOKENS, slot_cols=win.shape[2])
    token = put_grads("gate", [g_win, g_wout.reshape(N_DEV, -1, D)])
    grad_x, dg_mix0 = _mm_nt_rms_bwd("gate_dy", [(duvp, win, *_whole(win))], xs, g_mix[0], dh1, tm=512, nc=512,
                                     after=token, shards=True, emit_bf16=False)

    small_g = dict(
        mix_norm_g=jnp.concatenate([dg_mix0, dg_mix1], axis=0),
        mlp_norm_g=jnp.concatenate([dg_mlp0, dg_mlp1], axis=0),
        final_norm_g=dg_fin[0], a_ln_g=d_lng, a_ln_b=d_lnb, a_w_s=d_wm[None],
        a_b_s=jnp.sum(d_mixed.reshape(CHUNK, GROUPS, D // GROUPS), axis=2).T[None],
        rel_bias=_bias_grad("att_dbias", ds_sums))
    return loss_local, grad_x, small_g
```

```python
import functools
import math

import jax
import jax.numpy as jnp
from jax import lax
from jax.experimental import pallas as pl
from jax.experimental.pallas import tpu as pltpu

F32 = jnp.float32
BF16 = jnp.bfloat16
MESH = pl.DeviceIdType.MESH

N_DEV = 8
EPS = 1e-6
NEG_INF = -1e30
CHUNK = 128
GROUPS = 8
HEAD_DIM = 64
ATT_HEADS = 8
ATT_WIDTH = ATT_HEADS * HEAD_DIM
DILATIONS = (1, 4, 16)
N_DIL = len(DILATIONS)
N_BUCKETS = 32
MAX_EXACT = N_BUCKETS // 2
REL_MAX_DISTANCE = 2048
ATT_ROWS = 2048
ATT_SCALE = HEAD_DIM ** -0.5
DW_TOKENS = 4096
LANES = 128

ADAM_LR = 0.001
ADAM_B1 = 0.9
ADAM_B2 = 0.999
ADAM_EPS = 1e-08
ADAM_WD = 0.01
ADAM_STEP = 10

VMEM_LIMIT_BYTES = 56 * 1024 * 1024


def _params(semantics=None):
    return pltpu.CompilerParams(dimension_semantics=semantics, vmem_limit_bytes=VMEM_LIMIT_BYTES)


def _bf(v):
    return v.astype(BF16)


def _dot(a, b, dims):
    return lax.dot_general(a, b, (dims, ((), ())), preferred_element_type=F32)


NN = ((1,), (0,))
NT = ((1,), (1,))
TN = ((0,), (0,))


def _after_operand(after):
    if after is None:
        return [], []
    return [after], [pl.BlockSpec(memory_space=pl.ANY)]


def _rms_fwd(name, x, g, tm=512, after=None):
    S, D = x.shape
    after_args, after_specs = _after_operand(after)

    def body(x_ref, g_ref, *rest):
        y_ref = rest[-1]
        xv = x_ref[...]
        r = lax.rsqrt(jnp.mean(xv * xv, axis=-1, keepdims=True) + EPS)
        y_ref[...] = _bf(xv * r * g_ref[...])

    return pl.pallas_call(
        body, name=name, grid=(S // tm,),
        in_specs=[pl.BlockSpec((tm, D), lambda i: (i, 0)), pl.BlockSpec((1, D), lambda i: (0, 0))] + after_specs,
        out_specs=pl.BlockSpec((tm, D), lambda i: (i, 0)),
        out_shape=jax.ShapeDtypeStruct((S, D), BF16),
        compiler_params=_params(("parallel",)),
    )(x, g, *after_args)


def _mm_res_loss(name, a, w, res, g, target, *, tm, nc):
    M, D = res.shape

    def body(a_ref, w_ref, r_ref, g_ref, t_ref, dh_ref, dg_ref, l_ref, dhb_ref, h_sc):
        i = pl.program_id(0)
        a_v = _bf(a_ref[...])
        for j in range(D // nc):
            cols, acc = _chunk_product([a_v], [w_ref], j, nc, False, False)
            h_sc[:, cols] = r_ref[:, cols] + acc
        xv = h_sc[...]
        r = lax.rsqrt(jnp.mean(xv * xv, axis=-1, keepdims=True) + EPS)
        xh = xv * r
        gv = g_ref[...]
        e = xh * gv - t_ref[...]
        dout = e / D
        dyg = dout * gv
        c = jnp.mean(dyg * xh, axis=-1, keepdims=True)
        dh = r * (dyg - xh * c)
        dh_ref[...] = dh
        dhb_ref[...] = _bf(dh)
        dg_part = jnp.sum(dout * xh, axis=0, keepdims=True)
        l_part = jnp.sum(e * e, axis=0, keepdims=True)

        @pl.when(i == 0)
        def _():
            dg_ref[...] = dg_part
            l_ref[...] = l_part

        @pl.when(i > 0)
        def _():
            dg_ref[...] += dg_part
            l_ref[...] += l_part

    row = pl.BlockSpec((tm, D), lambda i: (i, 0))
    vec = pl.BlockSpec((1, D), lambda i: (0, 0))
    return pl.pallas_call(
        body, name=name, grid=(M // tm,),
        in_specs=[pl.BlockSpec((tm, a.shape[1]), lambda i: (i, 0)), pl.BlockSpec(w.shape, lambda i: (0, 0)),
                  row, vec, row],
        out_specs=[row, vec, vec, row],
        out_shape=[jax.ShapeDtypeStruct((M, D), F32), jax.ShapeDtypeStruct((1, D), F32),
                   jax.ShapeDtypeStruct((1, D), F32), jax.ShapeDtypeStruct((M, D), BF16)],
        scratch_shapes=[pltpu.VMEM((tm, D), F32)],
        compiler_params=_params(("arbitrary",)),
    )(a, w, res, g, target)


def _chunk_product(a_vals, w_refs, j, nc, nt, shards):
    cols = slice(j * nc, (j + 1) * nc)
    acc = None
    for a_v, w_ref in zip(a_vals, w_refs):
        if not shards:
            terms = [_dot(a_v, w_ref[cols, :], NT) if nt else _dot(a_v, w_ref[:, cols], NN)]
        elif nt:
            nl = w_ref.shape[2]
            terms = [_dot(a_v[:, k * nl:(k + 1) * nl], w_ref[k, cols, :], NT) for k in range(N_DEV)]
        else:
            terms = [_dot(a_v, w_ref[j], NN)]
        for t in terms:
            acc = t if acc is None else acc + t
    return cols, acc


def _mm_rows(name, pairs, n_out, *, nt, tm, nc, epi="plain", extra=None, out_dtype=F32, after=None, shards=False,
             norm_g=None):
    M = pairs[0][0].shape[0]
    np_ = len(pairs)
    after_args, after_specs = _after_operand(after)

    def body(*refs):
        a_refs = refs[:np_]
        w_refs = refs[np_:2 * np_]
        pos = 2 * np_
        e_ref = None
        if extra is not None:
            e_ref = refs[pos]
            pos += 1
        if norm_g is not None:
            g_ref = refs[pos]
            pos += 1
        pos += len(after_args)
        outs = refs[pos:]
        a_vals = [_bf(a[...]) for a in a_refs]
        for j in range(n_out // nc):
            cols, acc = _chunk_product(a_vals, w_refs, j, nc, nt, shards)
            if epi == "plain":
                outs[0][:, cols] = acc.astype(out_dtype)
            elif epi == "res":
                outs[0][:, cols] = e_ref[:, cols] + acc
            elif epi == "relu2":
                outs[0][:, cols] = _bf(acc)
                rl = jnp.maximum(acc, 0.0)
                outs[1][:, cols] = _bf(rl * rl)
            elif epi == "mask2relu":
                outs[0][:, cols] = _bf(acc * (2.0 * jnp.maximum(e_ref[:, cols].astype(F32), 0.0)))
        if norm_g is not None:
            hv = outs[0][...]
            r = lax.rsqrt(jnp.mean(hv * hv, axis=-1, keepdims=True) + EPS)
            outs[1][...] = _bf(hv * r * g_ref[...])

    in_specs = [pl.BlockSpec((tm, a.shape[1]), lambda i: (i, 0)) for a, _, _, _ in pairs]
    for _, _, wshape, widx in pairs:
        in_specs.append(pl.BlockSpec(wshape, functools.partial(lambda i, widx: widx, widx=widx)))
    args = [a for a, _, _, _ in pairs] + [w for _, w, _, _ in pairs]
    if extra is not None:
        in_specs.append(pl.BlockSpec((tm, n_out), lambda i: (i, 0)))
        args.append(extra)
    if norm_g is not None:
        in_specs.append(pl.BlockSpec((1, n_out), lambda i: (0, 0)))
        args.append(norm_g)
    in_specs += after_specs
    args += after_args
    row_out = pl.BlockSpec((tm, n_out), lambda i: (i, 0))
    if epi == "relu2":
        out_specs = [row_out, row_out]
        out_shape = [jax.ShapeDtypeStruct((M, n_out), BF16), jax.ShapeDtypeStruct((M, n_out), BF16)]
    elif norm_g is not None:
        out_specs = [row_out, row_out]
        out_shape = [jax.ShapeDtypeStruct((M, n_out), F32), jax.ShapeDtypeStruct((M, n_out), BF16)]
    else:
        dt = BF16 if epi == "mask2relu" else (F32 if epi == "res" else out_dtype)
        out_specs = row_out
        out_shape = jax.ShapeDtypeStruct((M, n_out), dt)
    return pl.pallas_call(
        body, name=name, grid=(M // tm,), in_specs=in_specs, out_specs=out_specs, out_shape=out_shape,
        compiler_params=_params(("parallel",)),
    )(*args)


def _whole(w):
    return w.shape, (0,) * w.ndim


def _mm_nn(name, a, w, **kw):
    n_out = w.shape[0] * w.shape[2] if w.ndim == 3 else w.shape[1]
    return _mm_rows(name, [(a, w, *_whole(w))], n_out, nt=False, **kw)


def _mm_nt(name, a, w, **kw):
    return _mm_rows(name, [(a, w, *_whole(w))], w.shape[0], nt=True, **kw)


def _mm_nt_rms_bwd(name, pairs, x, g, dres, *, tm, nc, after=None, shards=False, emit_bf16=True):
    M, D = x.shape
    np_ = len(pairs)
    n_out = 3 if emit_bf16 else 2
    after_args, after_specs = _after_operand(after)

    def body(*refs):
        a_refs = refs[:np_]
        w_refs = refs[np_:2 * np_]
        x_ref, g_ref, r_ref = refs[2 * np_:2 * np_ + 3]
        dy_sc = refs[-1]
        outs = refs[-1 - n_out:-1]
        dx_ref, dg_ref = outs[0], outs[1]
        i = pl.program_id(0)
        a_vals = [_bf(a[...]) for a in a_refs]
        for j in range(D // nc):
            cols, acc = _chunk_product(a_vals, w_refs, j, nc, True, shards)
            dy_sc[:, cols] = acc
        xv = x_ref[...]
        r = lax.rsqrt(jnp.mean(xv * xv, axis=-1, keepdims=True) + EPS)
        xh = xv * r
        dy_v = dy_sc[...]
        dyg = dy_v * g_ref[...]
        c = jnp.mean(dyg * xh, axis=-1, keepdims=True)
        dx = r_ref[...] + r * (dyg - xh * c)
        dx_ref[...] = dx
        if emit_bf16:
            outs[2][...] = _bf(dx)
        part = jnp.sum(dy_v * xh, axis=0, keepdims=True)

        @pl.when(i == 0)
        def _():
            dg_ref[...] = part

        @pl.when(i > 0)
        def _():
            dg_ref[...] += part

    row = pl.BlockSpec((tm, D), lambda i: (i, 0))
    vec = pl.BlockSpec((1, D), lambda i: (0, 0))
    in_specs = [pl.BlockSpec((tm, a.shape[1]), lambda i: (i, 0)) for a, _, _, _ in pairs]
    for _, _, wshape, widx in pairs:
        in_specs.append(pl.BlockSpec(wshape, functools.partial(lambda i, widx: widx, widx=widx)))
    args = [a for a, _, _, _ in pairs] + [w for _, w, _, _ in pairs]
    return pl.pallas_call(
        body, name=name, grid=(M // tm,),
        in_specs=in_specs + [row, vec, row] + after_specs,
        out_specs=[row, vec] + [row] * (n_out - 2),
        out_shape=[jax.ShapeDtypeStruct((M, D), F32), jax.ShapeDtypeStruct((1, D), F32)]
        + [jax.ShapeDtypeStruct((M, D), BF16)] * (n_out - 2),
        scratch_shapes=[pltpu.VMEM((tm, D), F32)],
        compiler_params=_params(("arbitrary",)),
    )(*args, x, g, dres, *after_args)


def _mm_tn(name, a, b, *, t1, tn, tm=2048, slot_cols=None):
    M, K1 = a.shape
    N = b.shape[1]
    nm = M // tm

    def body(a_ref, b_ref, o_ref, acc_ref):
        m = pl.program_id(2)
        t = _dot(_bf(a_ref[...]), _bf(b_ref[...]), TN)

        @pl.when(m == 0)
        def _():
            acc_ref[...] = t

        @pl.when(m > 0)
        def _():
            acc_ref[...] += t

        @pl.when(m == nm - 1)
        def _():
            if slot_cols is None:
                o_ref[...] = _bf(acc_ref[...])
            else:
                for k in range(tn // slot_cols):
                    o_ref[k] = _bf(acc_ref[:, k * slot_cols:(k + 1) * slot_cols])

    if slot_cols is not None:
        out_spec = pl.BlockSpec((tn // slot_cols, t1, slot_cols), lambda i, j, m: (j, i, 0))
        out_shape = jax.ShapeDtypeStruct((N // slot_cols, K1, slot_cols), BF16)
    else:
        out_spec = pl.BlockSpec((t1, tn), lambda i, j, m: (i, j))
        out_shape = jax.ShapeDtypeStruct((K1, N), BF16)
    return pl.pallas_call(
        body, name=name, grid=(K1 // t1, N // tn, nm),
        in_specs=[pl.BlockSpec((tm, t1), lambda i, j, m: (m, i)), pl.BlockSpec((tm, tn), lambda i, j, m: (m, j))],
        out_specs=out_spec, out_shape=out_shape,
        scratch_shapes=[pltpu.VMEM((t1, tn), F32)],
        compiler_params=_params(("parallel", "parallel", "arbitrary")),
    )(a, b)


_INV_SQRT2 = 1.0 / math.sqrt(2.0)
_INV_SQRT2PI = 1.0 / math.sqrt(2.0 * math.pi)


def _gelu(x):
    return 0.5 * x * (1.0 + lax.erf(x * _INV_SQRT2))


def _gelu_and_grad(x):
    cdf = 0.5 * (1.0 + lax.erf(x * _INV_SQRT2))
    return x * cdf, cdf + x * (_INV_SQRT2PI * jnp.exp(-0.5 * x * x))


def _layer_norm_parts(v):
    mu = jnp.mean(v, axis=-1, keepdims=True)
    xc = v - mu
    rs = lax.rsqrt(jnp.mean(xc * xc, axis=-1, keepdims=True) + EPS)
    return xc * rs, rs


def _gate_fwd(name, uvp, ln_g, ln_b, wm, bs_full, tr=512):
    S, W2 = uvp.shape
    W = W2 // 2
    gd = W // GROUPS

    def body(u_ref, v_ref, lg_ref, lb_ref, wm_ref, bs_ref, z_ref):
        vh, _ = _layer_norm_parts(_gelu(v_ref[...]))
        vn = _bf(vh * lg_ref[...] + lb_ref[...])
        for ci in range(tr // CHUNK):
            rows = slice(ci * CHUNK, (ci + 1) * CHUNK)
            for g in range(GROUPS):
                cols = slice(g * gd, (g + 1) * gd)
                mixed = _dot(wm_ref[g], vn[rows, cols], NN) + bs_ref[:, cols]
                z_ref[rows, cols] = _bf(_gelu(u_ref[rows, cols]) * mixed)

    vec = pl.BlockSpec((1, W), lambda i: (0, 0))
    return pl.pallas_call(
        body, name=name, grid=(S // tr,),
        in_specs=[pl.BlockSpec((tr, W), lambda i: (i, 0)), pl.BlockSpec((tr, W), lambda i: (i, 1)), vec, vec,
                  pl.BlockSpec((GROUPS, CHUNK, CHUNK), lambda i: (0, 0, 0)),
                  pl.BlockSpec((CHUNK, W), lambda i: (0, 0))],
        out_specs=pl.BlockSpec((tr, W), lambda i: (i, 0)),
        out_shape=jax.ShapeDtypeStruct((S, W), BF16),
        compiler_params=_params(("parallel",)),
    )(uvp, uvp, ln_g, ln_b, wm, bs_full)


def _gate_bwd(name, uvp, dz, ln_g, ln_b, wm, bs_full, tr=256):
    S, W2 = uvp.shape
    W = W2 // 2
    gd = W // GROUPS
    n_steps = S // tr

    def body(u_ref, v_ref, dz_ref, lg_ref, lb_ref, wm_ref, bs_ref, duv_ref, dwm_ref, dmx_ref, dlg_ref, dlb_ref,
             dvn_ref):
        i = pl.program_id(0)
        v, dv_dvp = _gelu_and_grad(v_ref[...])
        vh, rs = _layer_norm_parts(v)
        lg = lg_ref[...]
        vn = _bf(vh * lg + lb_ref[...])

        @pl.when(i == 0)
        def _():
            dwm_ref[...] = jnp.zeros_like(dwm_ref)
            dmx_ref[...] = jnp.zeros_like(dmx_ref)
            dlg_ref[...] = jnp.zeros_like(dlg_ref)
            dlb_ref[...] = jnp.zeros_like(dlb_ref)

        for ci in range(tr // CHUNK):
            rows = slice(ci * CHUNK, (ci + 1) * CHUNK)
            for g in range(GROUPS):
                cols = slice(g * gd, (g + 1) * gd)
                u, du_dup = _gelu_and_grad(u_ref[rows, cols])
                dz_v = dz_ref[rows, cols]
                dmixed = dz_v * u
                dmx_ref[:, cols] += dmixed
                dmixed_b = _bf(dmixed)
                mixed = _dot(wm_ref[g], vn[rows, cols], NN) + bs_ref[:, cols]
                duv_ref[rows, cols] = _bf(dz_v * mixed * du_dup)
                dwm_ref[g] += _dot(dmixed_b, vn[rows, cols], NT)
                dvn_ref[rows, cols] = _dot(wm_ref[g], dmixed_b, TN)
        dvn = dvn_ref[...]
        dlg_ref[...] += jnp.sum(dvn * vh, axis=0, keepdims=True)
        dlb_ref[...] += jnp.sum(dvn, axis=0, keepdims=True)
        dvh = dvn * lg
        dv = rs * (dvh - jnp.mean(dvh, axis=-1, keepdims=True) - vh * jnp.mean(dvh * vh, axis=-1, keepdims=True))
        duv_ref[:, W:] = _bf(dv * dv_dvp)

        @pl.when(i == n_steps - 1)
        def _():
            t_idx = lax.broadcasted_iota(jnp.int32, (CHUNK, CHUNK), 0)
            s_idx = lax.broadcasted_iota(jnp.int32, (CHUNK, CHUNK), 1)
            keep = (s_idx <= t_idx).astype(F32)
            for g in range(GROUPS):
                dwm_ref[g] = dwm_ref[g] * keep

    vec = pl.BlockSpec((1, W), lambda i: (0, 0))
    row = pl.BlockSpec((tr, W), lambda i: (i, 0))
    return pl.pallas_call(
        body, name=name, grid=(n_steps,),
        in_specs=[row, pl.BlockSpec((tr, W), lambda i: (i, 1)), row, vec, vec,
                  pl.BlockSpec((GROUPS, CHUNK, CHUNK), lambda i: (0, 0, 0)),
                  pl.BlockSpec((CHUNK, W), lambda i: (0, 0))],
        out_specs=[pl.BlockSpec((tr, W2), lambda i: (i, 0)),
                   pl.BlockSpec((GROUPS, CHUNK, CHUNK), lambda i: (0, 0, 0)),
                   pl.BlockSpec((CHUNK, W), lambda i: (0, 0)), vec, vec],
        out_shape=[jax.ShapeDtypeStruct((S, W2), BF16), jax.ShapeDtypeStruct((GROUPS, CHUNK, CHUNK), F32),
                   jax.ShapeDtypeStruct((CHUNK, W), F32), jax.ShapeDtypeStruct((1, W), F32),
                   jax.ShapeDtypeStruct((1, W), F32)],
        scratch_shapes=[pltpu.VMEM((tr, W), F32)],
        compiler_params=_params(("arbitrary",)),
    )(uvp, uvp, dz, ln_g, ln_b, wm, bs_full)


def _t5_bucket(distance):
    small = distance < MAX_EXACT
    nf = jnp.maximum(distance, 1).astype(F32)
    large = MAX_EXACT + (jnp.log(nf / MAX_EXACT) / math.log(REL_MAX_DISTANCE / MAX_EXACT)
                         * (N_BUCKETS - MAX_EXACT)).astype(jnp.int32)
    large = jnp.minimum(large, N_BUCKETS - 1)
    return jnp.where(small, distance, large)


TILE_ELEMS = 2 * CHUNK * CHUNK


def _band_buckets():
    rel = CHUNK + jnp.arange(CHUNK)[None, :] - jnp.arange(2 * CHUNK)[:, None]
    band = (rel >= 0) & (rel <= CHUNK)
    buckets = [_t5_bucket(jnp.clip(rel, 0, CHUNK) * d) for d in DILATIONS]
    return jnp.stack(buckets), band


def _bucket_onehot():
    buckets, _ = _band_buckets()
    return (buckets.reshape(N_DIL, 1, TILE_ELEMS) == jnp.arange(N_BUCKETS)[None, :, None]).astype(F32)


def _bias_tiles(name, rel_bias, after=None):
    _, band = _band_buckets()
    own = band & (jnp.arange(2 * CHUNK) >= CHUNK)[:, None]
    masks = jnp.stack([own, band]).reshape(2, TILE_ELEMS).astype(F32)
    tables = jnp.transpose(rel_bias.reshape(N_BUCKETS, N_DIL, ATT_HEADS), (1, 2, 0))
    after_args, after_specs = _after_operand(after)

    def body(t_ref, oh_ref, m_ref, *rest):
        out_ref = rest[-1]
        for g in range(N_DIL):
            bias = lax.dot_general(t_ref[g], oh_ref[g], (NN, ((), ())), precision=lax.Precision.HIGHEST,
                                   preferred_element_type=F32)
            for f in range(2):
                out_ref[g, f] = jnp.where(m_ref[f:f + 1, :] > 0.5, bias, NEG_INF)

    whole = pl.BlockSpec(memory_space=pltpu.VMEM)
    out = pl.pallas_call(
        body, name=name, out_shape=jax.ShapeDtypeStruct((N_DIL, 2, ATT_HEADS, TILE_ELEMS), F32),
        in_specs=[whole, whole, whole] + after_specs, out_specs=whole,
        compiler_params=_params(),
    )(tables, _bucket_onehot(), masks, *after_args)
    return out.reshape(N_DIL, 2, ATT_HEADS, 2 * CHUNK, CHUNK)


def _pair_bias(b_ref, inner, keys=slice(None)):
    return jnp.concatenate([b_ref[inner, 0, keys, :], b_ref[inner, 1, keys, :]], axis=1)


def _att_specs(order):
    def column(part, ids):
        hp, g, _ = order(*ids)
        return part * 3 * 4 + g * 4 + hp

    def window(part):
        def index(*ids):
            c = order(*ids)[2]
            return pl.multiple_of(jnp.maximum(c - 1, 0) * ATT_ROWS, ATT_ROWS), column(part, ids) * LANES
        return pl.BlockSpec((pl.Element(2 * ATT_ROWS), pl.Element(LANES)), index)

    return [pl.BlockSpec((ATT_ROWS, LANES), lambda *ids: (order(*ids)[2], column(0, ids))), window(1), window(2)]


def _window_base(c):
    return jnp.where(c == 0, 0, ATT_ROWS)


def _rows(start, d):
    if d == 1:
        return pl.ds(pl.multiple_of(start, CHUNK), CHUNK)
    return pl.ds(start, CHUNK, stride=d)


def _att_tile_offsets(t, d):
    n = t // d
    r = t % d
    return n * (CHUNK * d) + r, n


def _head_pair_columns(x_t):
    zeros = jnp.zeros((HEAD_DIM, CHUNK), x_t.dtype)
    return jnp.concatenate([jnp.concatenate([x_t[:HEAD_DIM], zeros], axis=0),
                            jnp.concatenate([zeros, x_t[HEAD_DIM:]], axis=0)], axis=1)


def _head_pair_rows(y):
    return jnp.concatenate([y[:HEAD_DIM, :CHUNK], y[HEAD_DIM:, CHUNK:]], axis=0)


def _att_fwd(name, qkv, bias_tiles):
    S = qkv.shape[0]
    n_chunks = S // ATT_ROWS
    tiles = ATT_ROWS // CHUNK

    def body(q_ref, kk, vv, b_ref, out_ref, lse_ref, o_sc, l_sc):
        c = pl.program_id(1)
        g = pl.program_id(2)
        base = _window_base(c)

        for gi, d in enumerate(DILATIONS):
            @pl.when(g == gi)
            def _(gi=gi, d=d):
                span = CHUNK * d

                def tile(t, carry):
                    q0, n = _att_tile_offsets(t, d)
                    first = (c == 0) & (n == 0)
                    rows = _rows(q0, d)
                    cur = _rows(base + q0, d)
                    prev = _rows(jnp.where(first, q0, base + q0 - span), d)
                    inner = jnp.where(first, 0, 1)
                    qq = _head_pair_columns(_bf(q_ref[rows, :] * ATT_SCALE).T)
                    s_p = _dot(_bf(kk[prev, :]), qq, NN) + _pair_bias(b_ref, inner, slice(0, CHUNK))
                    s_c = _dot(_bf(kk[cur, :]), qq, NN) + _pair_bias(b_ref, inner, slice(CHUNK, 2 * CHUNK))
                    m = jnp.maximum(jnp.max(s_p, axis=0, keepdims=True), jnp.max(s_c, axis=0, keepdims=True))
                    p_p = jnp.exp(s_p - m)
                    p_c = jnp.exp(s_c - m)
                    l = jnp.sum(p_p, axis=0, keepdims=True) + jnp.sum(p_c, axis=0, keepdims=True)
                    o2 = (_dot(_bf(vv[prev, :]).T, _bf(p_p), NN)
                          + _dot(_bf(vv[cur, :]).T, _bf(p_c), NN)) * (1.0 / l)
                    lse = m + jnp.log(l)
                    l_t = jnp.concatenate([jnp.broadcast_to(lse[:, :CHUNK], (HEAD_DIM, CHUNK)),
                                           jnp.broadcast_to(lse[:, CHUNK:], (HEAD_DIM, CHUNK))], axis=0)
                    o_sc[gi, rows, :] = _head_pair_rows(o2).T
                    l_sc[gi, rows, :] = l_t.T
                    return carry

                lax.fori_loop(0, tiles, tile, 0, unroll=16)

        @pl.when(g == N_DIL - 1)
        def _():
            for rows in (slice(i * 4 * CHUNK, (i + 1) * 4 * CHUNK) for i in range(ATT_ROWS // (4 * CHUNK))):
                ls = [l_sc[gi, rows, :] for gi in range(N_DIL)]
                mx = functools.reduce(jnp.maximum, ls)
                ws = [jnp.exp(l - mx) for l in ls]
                tot = functools.reduce(lambda a, b: a + b, ws)
                acc = ws[0] * o_sc[0, rows, :]
                for gi in range(1, N_DIL):
                    acc = acc + ws[gi] * o_sc[gi, rows, :]
                out_ref[rows, :] = acc / tot
                lse_ref[rows, :] = mx + jnp.log(tot)

    order = lambda hp, c, g: (hp, g, c)
    out_spec = pl.BlockSpec((ATT_ROWS, LANES), lambda hp, c, g: (c, hp))
    shape = jax.ShapeDtypeStruct((S, ATT_WIDTH), F32)
    return pl.pallas_call(
        body, name=name, grid=(ATT_HEADS // 2, n_chunks, N_DIL),
        in_specs=_att_specs(order) + [
            pl.BlockSpec((None, 2, 2, 2 * CHUNK, CHUNK), lambda hp, c, g: (g, 0, hp, 0, 0))],
        out_specs=[out_spec, out_spec],
        out_shape=[shape, shape],
        scratch_shapes=[pltpu.VMEM((N_DIL, ATT_ROWS, LANES), F32), pltpu.VMEM((N_DIL, ATT_ROWS, LANES), F32)],
        compiler_params=_params(("parallel", "parallel", "arbitrary")),
    )(qkv, qkv, qkv, bias_tiles)


def _att_bwd(name, qkv, o, lse, d_o, bias_tiles):
    S = qkv.shape[0]
    n_chunks = S // ATT_ROWS
    tiles = ATT_ROWS // CHUNK

    def body(q_ref, kk, vv, o_ref, l_ref, do_ref, b_ref, dq_out, dk_out, dv_out, ds_ref, dq_ref, dk_ref, dv_ref):
        g = pl.program_id(1)
        c = pl.program_id(2)

        @pl.when(c == 0)
        def _():
            dk_ref[...] = jnp.zeros_like(dk_ref)
            dv_ref[...] = jnp.zeros_like(dv_ref)
            ds_ref[...] = jnp.zeros_like(ds_ref)

        base = _window_base(c)
        first_row = c * ATT_ROWS
        head0 = lax.broadcasted_iota(jnp.int32, (CHUNK, LANES), 1) < HEAD_DIM

        def head_pair_stack(x):
            zero = jnp.zeros_like(x)
            return jnp.concatenate([jnp.where(head0, x, zero), jnp.where(head0, zero, x)], axis=0)

        for gi, d in enumerate(DILATIONS):
            @pl.when(g == gi)
            def _(d=d):
                span = CHUNK * d

                def tile(t, carry):
                    q0, n = _att_tile_offsets(t, d)
                    first = (c == 0) & (n == 0)
                    rows = _rows(q0, d)
                    cur = _rows(base + q0, d)
                    prev = _rows(jnp.where(first, q0, base + q0 - span), d)
                    inner = jnp.where(first, 0, 1)
                    g_cur = _rows(first_row + q0, d)
                    g_prev = _rows(jnp.where(first, q0, first_row + q0 - span), d)
                    q2 = _bf(q_ref[rows, :] * ATT_SCALE)
                    q_t = q2.T
                    k2 = _bf(jnp.concatenate([kk[prev, :], kk[cur, :]], axis=0))
                    k_t = k2.T
                    v2 = _bf(jnp.concatenate([vv[prev, :], vv[cur, :]], axis=0))
                    do2 = do_ref[rows, :]
                    do_b = _bf(do2)
                    do_t = do_b.T
                    lse_t = l_ref[rows, :].T
                    dd_t = (do2 * o_ref[rows, :]).T
                    lse = jnp.concatenate([lse_t[0:1], lse_t[HEAD_DIM:HEAD_DIM + 1]], axis=1)
                    delta = jnp.concatenate([jnp.sum(dd_t[:HEAD_DIM], axis=0, keepdims=True),
                                             jnp.sum(dd_t[HEAD_DIM:], axis=0, keepdims=True)], axis=1)
                    s = _dot(k2, _head_pair_columns(q_t), NN) + _pair_bias(b_ref, inner)
                    p = jnp.exp(s - lse)
                    ds = p * (_dot(v2, _head_pair_columns(do_t), NN) - delta)
                    ds_ref[...] += ds
                    ds_b = _bf(ds)
                    dq_t = _head_pair_rows(_dot(k_t, ds_b, NN))
                    dk2 = _dot(ds_b, head_pair_stack(q2), NN)
                    dv2 = _dot(_bf(p), head_pair_stack(do_b), NN)
                    dq_ref[rows, :] = (dq_t * ATT_SCALE).T
                    dk_ref[g_prev, :] += dk2[0:CHUNK]
                    dk_ref[g_cur, :] += dk2[CHUNK:2 * CHUNK]
                    dv_ref[g_prev, :] += dv2[0:CHUNK]
                    dv_ref[g_cur, :] += dv2[CHUNK:2 * CHUNK]
                    return carry

                lax.fori_loop(0, tiles, tile, 0, unroll=16)

        dq_out[...] = _bf(dq_ref[...])

        @pl.when(c == n_chunks - 1)
        def _():
            dk_out[...] = _bf(dk_ref[...])
            dv_out[...] = _bf(dv_ref[...])

    order = lambda hp, g, c: (hp, g, c)
    chunk = pl.BlockSpec((ATT_ROWS, LANES), lambda hp, g, c: (c, hp))
    slab = pl.BlockSpec((S, LANES), lambda hp, g, c: (0, g * 4 + hp))
    width = N_DIL * ATT_WIDTH
    dq, dk, dv, ds_sums = pl.pallas_call(
        body, name=name, grid=(ATT_HEADS // 2, N_DIL, n_chunks),
        in_specs=_att_specs(order) + [chunk, chunk, chunk,
                                      pl.BlockSpec((None, 2, 2, 2 * CHUNK, CHUNK),
                                                   lambda hp, g, c: (g, 0, hp, 0, 0))],
        out_specs=[pl.BlockSpec((ATT_ROWS, LANES), lambda hp, g, c: (c, g * 4 + hp)), slab, slab,
                   pl.BlockSpec((None, None, 2 * CHUNK, 2 * CHUNK), lambda hp, g, c: (g, hp, 0, 0))],
        out_shape=[jax.ShapeDtypeStruct((S, width), BF16), jax.ShapeDtypeStruct((S, width), BF16),
                   jax.ShapeDtypeStruct((S, width), BF16),
                   jax.ShapeDtypeStruct((N_DIL, ATT_HEADS // 2, 2 * CHUNK, 2 * CHUNK), F32)],
        scratch_shapes=[pltpu.VMEM((ATT_ROWS, LANES), F32), pltpu.VMEM((S, LANES), F32),
                        pltpu.VMEM((S, LANES), F32)],
        compiler_params=_params(("parallel", "parallel", "arbitrary")),
    )(qkv, qkv, qkv, o, lse, d_o, bias_tiles)
    ds_sums = ds_sums.reshape(N_DIL, ATT_HEADS // 2, 2 * CHUNK, 2, CHUNK)
    ds_sums = jnp.transpose(ds_sums, (0, 1, 3, 2, 4)).reshape(N_DIL, ATT_HEADS, 2 * CHUNK, CHUNK)
    return dq, dk, dv, ds_sums


def _bias_grad(name, ds_sums):
    flat = ds_sums.reshape(N_DIL, ATT_HEADS, TILE_ELEMS)

    def body(oh_ref, ds_ref, out_ref):
        for g in range(N_DIL):
            out_ref[g] = lax.dot_general(oh_ref[g], ds_ref[g], (NT, ((), ())), precision=lax.Precision.HIGHEST,
                                         preferred_element_type=F32)

    out = pl.pallas_call(
        body, name=name, out_shape=jax.ShapeDtypeStruct((N_DIL, N_BUCKETS, ATT_HEADS), F32),
        compiler_params=_params(),
    )(_bucket_onehot(), flat)
    return jnp.transpose(out, (1, 0, 2)).reshape(N_BUCKETS, N_DIL * ATT_HEADS)


def _peers():
    x, y, c = lax.axis_index("x"), lax.axis_index("y"), lax.axis_index("c")
    me = 4 * x + 2 * y + c
    others = [(x, y, 1 - c), (1 - x, y, c), (x, 1 - y, c), (1 - x, 1 - y, c),
              (1 - x, y, 1 - c), (x, 1 - y, 1 - c), (1 - x, 1 - y, 1 - c)]
    return me, others


def _slot(dev):
    return 4 * dev[0] + 2 * dev[1] + dev[2]


_HBM =pl.BlockSpec(memory_space=pltpu.HBM)
_SEM = pl.BlockSpec(memory_space=pltpu.SEMAPHORE)
_EFFECT = pltpu.SideEffectType.DATAFLOW_SIDE_EFFECTING


def _exchange_copy(src_ref, land_ref, send_sems, recv_sems, k, dev, me, scatter, arriving):
    src = src_ref.at[me if arriving else _slot(dev)] if scatter else src_ref
    dst = land_ref.at[_slot(dev) if arriving else me]
    return pltpu.make_async_remote_copy(src_ref=src, dst_ref=dst, send_sem=send_sems.at[k], recv_sem=recv_sems.at[k],
                                        device_id=dev, device_id_type=MESH)


def _own_copy(src_ref, land_ref, local_sems, p, me, scatter):
    return pltpu.make_async_copy(src_ref.at[me] if scatter else src_ref, land_ref.at[me], local_sems.at[p])


def _exchange_start(name, srcs, scatter):
    n = len(srcs)
    landings = [lax.empty((N_DEV,) + src.shape[-2:], src.dtype) for src in srcs]

    def body(*refs):
        src_refs, land_refs = refs[:n], refs[n:2 * n]
        send_sems, recv_sems, local_sems = refs[2 * n:2 * n + 3]
        token = refs[-1]
        me, others = _peers()
        for p in range(n):
            _own_copy(src_refs[p], land_refs[p], local_sems, p, me, scatter).start()
            for k, dev in enumerate(others):
                _exchange_copy(src_refs[p], land_refs[p], send_sems, recv_sems, p * (N_DEV - 1) + k, dev, me,
                               scatter, False).start()
        token[...] = jnp.zeros_like(token)

    sems = pltpu.SemaphoreType.DMA((n * (N_DEV - 1),))
    hbm = lambda a: pltpu.with_memory_space_constraint(a, pltpu.HBM)
    outs = pl.pallas_call(
        body, name=name,
        out_shape=(sems, sems, pltpu.SemaphoreType.DMA((n,)), *[pltpu.HBM(a.shape, a.dtype) for a in srcs + landings],
                   jax.ShapeDtypeStruct((8, LANES), F32)),
        in_specs=(_HBM,) * (2 * n),
        out_specs=(_SEM, _SEM, _SEM) + (_HBM,) * (2 * n) + (pl.BlockSpec(memory_space=pltpu.VMEM),),
        input_output_aliases={i: 3 + i for i in range(2 * n)},
        compiler_params=pltpu.CompilerParams(has_side_effects=_EFFECT),
    )(*[hbm(a) for a in srcs + landings])
    return (outs[0], outs[1], outs[2], list(outs[3:3 + n]), list(outs[3 + n:3 + 2 * n]), scatter), outs[-1]


def _exchange_wait(name, handle, after):
    send_sems, recv_sems, local_sems, src_thru, land_thru, scatter = handle
    n = len(src_thru)

    def body(*refs):
        src_refs, land_refs = refs[:n], refs[n:2 * n]
        send_sems, recv_sems, local_sems = refs[2 * n:2 * n + 3]
        me, others = _peers()
        for p in range(n):
            _own_copy(src_refs[p], land_refs[p], local_sems, p, me, scatter).wait()
            for k, dev in enumerate(others):
                cp = _exchange_copy(src_refs[p], land_refs[p], send_sems, recv_sems, p * (N_DEV - 1) + k, dev, me,
                                    scatter, True)
                cp.wait_send()
                cp.wait_recv()

    outs = pl.pallas_call(
        body, name=name,
        out_shape=tuple(pltpu.HBM(a.shape, a.dtype) for a in src_thru + land_thru),
        in_specs=(_HBM,) * (2 * n) + (_SEM, _SEM, _SEM, pl.BlockSpec(memory_space=pl.ANY)),
        out_specs=(_HBM,) * (2 * n),
        input_output_aliases={i: i for i in range(2 * n)},
        compiler_params=pltpu.CompilerParams(has_side_effects=_EFFECT),
    )(*src_thru, *land_thru, send_sems, recv_sems, local_sems, after)
    return list(outs[n:])


def _adamw_math(w, g, m, v):
    m = ADAM_B1 * m + (1.0 - ADAM_B1) * g
    v = ADAM_B2 * v + (1.0 - ADAM_B2) * (g * g)
    m_hat = m / (1.0 - ADAM_B1 ** ADAM_STEP)
    v_hat = v / (1.0 - ADAM_B2 ** ADAM_STEP)
    delta = -ADAM_LR * (m_hat / (jnp.sqrt(v_hat) + ADAM_EPS) + ADAM_WD * w)
    return delta, m, v


_SMALL_WIDE = (("mix_norm_g", 2), ("mlp_norm_g", 2), ("final_norm_g", 1), ("a_ln_g", 1), ("a_ln_b", 1))
_SMALL_NARROW = (("a_w_s", GROUPS * CHUNK), ("a_b_s", GROUPS), ("rel_bias", N_BUCKETS))
_SMALL = tuple(n for n, _ in _SMALL_WIDE + _SMALL_NARROW)
_BIAS_COLS = N_DIL * ATT_HEADS


def _pack_small_grads(grads, loss_term):
    D = grads["a_ln_g"].shape[-1]
    tiles = [jnp.pad(grads[n].reshape(k, D), ((0, 8 - k), (0, 0))) for n, k in _SMALL_WIDE]
    tiles.append(jnp.pad(loss_term.reshape(1, 1), ((0, 7), (0, D - 1))))
    narrow = [grads["a_w_s"].reshape(-1, LANES), grads["a_b_s"].reshape(-1, LANES),
              jnp.pad(grads["rel_bias"], ((0, 0), (0, LANES - _BIAS_COLS)))]
    return jnp.concatenate(tiles, axis=0), jnp.concatenate(narrow, axis=0)


def _adamw_small(name, g_wide, g_narrow, w, m, v):
    D = g_wide.shape[-1]
    shapes = {n: (k, D) for n, k in _SMALL_WIDE}
    shapes.update({n: (k, LANES) for n, k in _SMALL_NARROW})
    shapes["rel_bias"] = (N_BUCKETS, _BIAS_COLS)
    n_t = len(_SMALL)

    def body(gw_ref, gn_ref, *rest):
        params = rest[:3 * n_t]
        outs = rest[3 * n_t:3 * n_t + 4 * n_t]
        loss_ref, sw, sn = rest[-3:]
        sw[...] = functools.reduce(lambda a, b: a + b, [gw_ref[j] for j in range(N_DEV)])
        sn[...] = functools.reduce(lambda a, b: a + b, [gn_ref[j] for j in range(N_DEV)])
        row = 0
        for i, n in enumerate(_SMALL):
            k, cols = shapes[n]
            if i < len(_SMALL_WIDE):
                g = sw[8 * i:8 * i + k, :]
            else:
                g = sn[row:row + k, 0:cols]
                row += k
            w_ref, m_ref, v_ref = params[3 * i:3 * i + 3]
            delta, m_new, v_new = _adamw_math(w_ref[...], g, m_ref[...], v_ref[...])
            for out, val in zip(outs[4 * i:4 * i + 4], (g, delta, m_new, v_new)):
                out[...] = val
        loss_ref[...] = sw[8 * len(_SMALL_WIDE):8 * len(_SMALL_WIDE) + 8, 0:LANES]

    whole = pl.BlockSpec(memory_space=pltpu.VMEM)
    args = [t[n].reshape(shapes[n]) for n in _SMALL for t in (w, m, v)]
    res = pl.pallas_call(
        body, name=name,
        in_specs=[whole] * (2 + len(args)), out_specs=[whole] * (4 * n_t + 1),
        out_shape=[jax.ShapeDtypeStruct(shapes[n], F32) for n in _SMALL for _ in range(4)]
        + [jax.ShapeDtypeStruct((8, LANES), F32)],
        scratch_shapes=[pltpu.VMEM(g_wide.shape[1:], F32), pltpu.VMEM(g_narrow.shape[1:], F32)],
        compiler_params=_params(),
    )(g_wide, g_narrow, *args)
    small = {n: tuple(r.reshape(w[n].shape) for r in res[4 * i:4 * i + 4]) for i, n in enumerate(_SMALL)}
    return small, res[-1][0, 0]


def _adamw_shard(name, parts, w, m, v, layer, earlier=None, after=None, tr=256):
    L, K, N = w.shape
    tr = min(tr, K)
    n_prev = 0 if earlier is None else 4
    after_args, after_specs = _after_operand(after)

    def body(p_ref, w_ref, m_ref, v_ref, *rest):
        g_out, d_out, m_out, v_out = rest[n_prev + len(after_args):]
        g = p_ref[0].astype(F32)
        for j in range(1, N_DEV):
            g = g + p_ref[j].astype(F32)
        delta, m_new, v_new = _adamw_math(w_ref[...], g, m_ref[...], v_ref[...])
        g_out[...] = g
        d_out[...] = delta
        m_out[...] = m_new
        v_out[...] = v_new

    row = pl.BlockSpec((None, tr, N), lambda i: (layer, i, 0))
    shape = jax.ShapeDtypeStruct((L, K, N), F32)
    return pl.pallas_call(
        body, name=name, grid=(K // tr,),
        in_specs=[pl.BlockSpec((N_DEV, tr, N), lambda i: (0, i, 0)), row, row, row]
        + [pl.BlockSpec(memory_space=pl.ANY)] * n_prev + after_specs,
        out_specs=[row, row, row, row],
        out_shape=[shape, shape, shape, shape],
        input_output_aliases={4 + j: j for j in range(n_prev)},
        compiler_params=_params(("parallel",)),
    )(parts, w, m, v, *(earlier or ()), *after_args)


def _column_slots(full):
    K, N = full.shape
    return jnp.transpose(full.reshape(K, N_DEV, N // N_DEV), (1, 0, 2))


def _from_column_slots(slots):
    _, K, n = slots.shape
    return jnp.transpose(slots, (1, 0, 2)).reshape(K, N_DEV * n)


_STAGES = (("gate", ("a_w_in", "a_w_out"), 0),
           ("mlp0", ("w_up", "w_down"), 0),
           ("att", ("b_w_qkv", "b_w_out"), 0),
           ("mlp1", ("w_up", "w_down"), 1))


def kernel(x, mix_norm_g, mlp_norm_g, final_norm_g, a_w_in, a_ln_g, a_ln_b, a_w_s, a_b_s, a_w_out, b_w_qkv, b_w_out, rel_bias, w_up, w_down, loss_target, m_mix_norm_g, m_mlp_norm_g, m_final_norm_g, m_a_w_in, m_a_ln_g, m_a_ln_b, m_a_w_s, m_a_b_s, m_a_w_out, m_b_w_qkv, m_b_w_out, m_rel_bias, m_w_up, m_w_down, v_mix_norm_g, v_mlp_norm_g, v_final_norm_g, v_a_w_in, v_a_ln_g, v_a_ln_b, v_a_w_s, v_a_b_s, v_a_w_out, v_b_w_qkv, v_b_w_out, v_rel_bias, v_w_up, v_w_down):
    w = dict(mix_norm_g=mix_norm_g, mlp_norm_g=mlp_norm_g, final_norm_g=final_norm_g, a_w_in=a_w_in, a_ln_g=a_ln_g,
             a_ln_b=a_ln_b, a_w_s=a_w_s, a_b_s=a_b_s, a_w_out=a_w_out, b_w_qkv=b_w_qkv, b_w_out=b_w_out,
             rel_bias=rel_bias, w_up=w_up, w_down=w_down)
    m = dict(mix_norm_g=m_mix_norm_g, mlp_norm_g=m_mlp_norm_g, final_norm_g=m_final_norm_g, a_w_in=m_a_w_in,
             a_ln_g=m_a_ln_g, a_ln_b=m_a_ln_b, a_w_s=m_a_w_s, a_b_s=m_a_b_s, a_w_out=m_a_w_out, b_w_qkv=m_b_w_qkv,
             b_w_out=m_b_w_out, rel_bias=m_rel_bias, w_up=m_w_up, w_down=m_w_down)
    v = dict(mix_norm_g=v_mix_norm_g, mlp_norm_g=v_mlp_norm_g, final_norm_g=v_final_norm_g, a_w_in=v_a_w_in,
             a_ln_g=v_a_ln_g, a_ln_b=v_a_ln_b, a_w_s=v_a_w_s, a_b_s=v_a_b_s, a_w_out=v_a_w_out, b_w_qkv=v_b_w_qkv,
             b_w_out=v_b_w_out, rel_bias=v_rel_bias, w_up=v_w_up, w_down=v_w_down)

    stages = {s: (names, layer) for s, names, layer in _STAGES}
    order = [s for s, _, _ in _STAGES]

    def shards_of(stage):
        names, layer = stages[stage]
        return [_bf(w[n][layer]) for n in names]

    pending = {}
    pending[order[0]], first_token = _exchange_start("gather_" + order[0] + "_start", shards_of(order[0]), False)

    def get_weights(stage, dep):
        gathered = _exchange_wait("gather_" + stage + "_wait", pending.pop(stage), dep)
        nxt = order.index(stage) + 1
        token = None
        if nxt < len(order):
            shards, gathered = lax.optimization_barrier((shards_of(order[nxt]), gathered))
            pending[order[nxt]], token = _exchange_start("gather_" + order[nxt] + "_start", shards, False)
        return gathered, token

    sent = {}

    def put_grads(stage, slot_grads):
        sent[stage], token = _exchange_start("scatter_" + stage + "_start", slot_grads, True)
        return token

    loss_local, grad_x, small_g = _local_step(
        x[0], loss_target[0], mix_norm_g, mlp_norm_g, final_norm_g, a_ln_g, a_ln_b, a_w_s, a_b_s, rel_bias,
        get_weights, put_grads, first_token)

    small_sent, token = _exchange_start("gather_small_start", list(_pack_small_grads(small_g, loss_local)), False)

    results = {}
    prev = token
    for stage in reversed(order):
        names, layer = stages[stage]
        received = _exchange_wait("scatter_" + stage + "_wait", sent[stage], prev)
        for n, parts in zip(names, received):
            results[n] = _adamw_shard("adamw_%s_%s" % (stage, n), parts, w[n], m[n], v[n], layer, results.get(n),
                                      after=prev)
            prev = results[n][0]

    g_wide, g_narrow = _exchange_wait("gather_small_wait", small_sent, prev)
    small, loss = _adamw_small("adamw_small", g_wide, g_narrow, w, m, v)

    outs = []
    for j in range(4):
        outs.extend(small[n][j] if n in _SMALL else results[n][j] for n in w)
    return (loss, grad_x[None], *outs)


def _local_step(xs, tgt, mix_norm_g, mlp_norm_g, final_norm_g, a_ln_g, a_ln_b, a_w_s, a_b_s, rel_bias,
                get_weights, put_grads, first_token=None):
    D = xs.shape[-1]
    g_mix = [mix_norm_g[l][None, :] for l in range(2)]
    g_mlp = [mlp_norm_g[l][None, :] for l in range(2)]
    g_fin = final_norm_g[None, :]
    ln_g, ln_b = a_ln_g, a_ln_b
    causal = jnp.tril(jnp.ones((CHUNK, CHUNK), dtype=bool))
    wm = _bf(jnp.where(causal[None], a_w_s[0], 0.0))
    bs_full = jnp.repeat(a_b_s[0].T, D // GROUPS, axis=1)
    bias_tiles = _bias_tiles("att_bias", rel_bias, after=first_token)

    (win, wout), token = get_weights("gate", bias_tiles)
    wout = wout.reshape(-1, D)
    y0 = _rms_fwd("rms_mix0", xs, g_mix[0], after=token)
    uvp = _mm_nn("gate_in", y0, win, tm=512, nc=win.shape[2], shards=True)
    z = _gate_fwd("gate_mid", uvp, ln_g, ln_b, wm, bs_full)
    h1, y1 = _mm_nn("gate_out", z, wout, tm=512, nc=512, epi="res", extra=xs, norm_g=g_mlp[0])
    (wup0, wdn0), token = get_weights("mlp0", h1)
    wdn0 = wdn0.reshape(-1, D)
    a0, f0 = _mm_nn("mlp0_up", y1, wup0, tm=512, nc=wup0.shape[2], epi="relu2", shards=True, after=token)
    h2, y2 = _mm_nn("mlp0_down", f0, wdn0, tm=512, nc=512, epi="res", extra=h1, norm_g=g_mix[1])
    (wqkv, wo), token = get_weights("att", h2)
    wqkv, wo = _from_column_slots(wqkv), _from_column_slots(wo)
    qkv = _mm_nn("att_qkv", y2, wqkv, tm=512, nc=512, after=token)
    o_att, lse = _att_fwd("att_fwd", qkv, bias_tiles)
    h3, y3 = _mm_nn("att_out", o_att, wo, tm=512, nc=512, epi="res", extra=h2, norm_g=g_mlp[1])
    (wup1, wdn1), _ = get_weights("mlp1", h3)
    wdn1 = wdn1.reshape(-1, D)
    a1, f1 = _mm_nn("mlp1_up", y3, wup1, tm=512, nc=wup1.shape[2], epi="relu2", shards=True)
    dh, dg_fin, err2, dh_b = _mm_res_loss("mlp1_down_loss", f1, wdn1, h3, g_fin, tgt, tm=512, nc=512)
    loss_local = 0.5 * jnp.sum(err2) / D

    def mlp_bwd(tag, dh, dh_b, h_in, y, a, f, wup_l, wdn_l, g_row, after):
        da = _mm_nt(tag + "_dact", dh_b, wdn_l, tm=512, nc=512, epi="mask2relu", extra=a, after=after)
        g_dn = _mm_tn(tag + "_dwdown", f, dh_b, t1=1024, tn=1024, tm=DW_TOKENS)
        g_up = _mm_tn(tag + "_dwup", y, da, t1=1024, tn=1024, tm=DW_TOKENS, slot_cols=wup_l.shape[2])
        dh_in, dg, dh_in_b = _mm_nt_rms_bwd(tag + "_dy", [(da, wup_l, *_whole(wup_l))], h_in, g_row, dh, tm=512,
                                            nc=512, shards=True)
        return dh_in, dh_in_b, dg, put_grads(tag, [g_up, g_dn.reshape(N_DEV, -1, D)])

    dh3, dh3_b, dg_mlp1, token = mlp_bwd("mlp1", dh, dh_b, h3, y3, a1, f1, wup1, wdn1, g_mlp[1], None)

    d_o = _mm_nt("att_dout", dh3_b, wo, tm=512, nc=512, after=token)
    g_wo = _mm_tn("att_dwo", o_att, dh3_b, t1=512, tn=1024, tm=DW_TOKENS)
    dq, dk, dv, ds_sums = _att_bwd("att_bwd", qkv, o_att, lse, d_o, bias_tiles)
    part_w = N_DIL * ATT_WIDTH
    g_qkv = [_mm_tn("att_dwqkv%d" % p, y2, t, t1=1024, tn=part_w) for p, t in enumerate((dq, dk, dv))]
    dh2, dg_mix1, dh2_b = _mm_nt_rms_bwd(
        "att_dy", [(t, wqkv, (D, part_w), (0, p)) for p, t in enumerate((dq, dk, dv))], h2, g_mix[1], dh3, tm=512,
        nc=512)
    token = put_grads("att", [_column_slots(jnp.concatenate(g_qkv, axis=1)), _column_slots(g_wo)])

    dh1, dh1_b, dg_mlp0, token = mlp_bwd("mlp0", dh2, dh2_b, h1, y1, a0, f0, wup0, wdn0, g_mlp[0], token)

    dz = _mm_nt("gate_dz", dh1_b, wout, tm=512, nc=512, after=token)
    g_wout = _mm_tn("gate_dwout", z, dh1_b, t1=1024, tn=1024, tm=DW_TOKENS)
    duvp, d_wm, d_mixed, d_lng, d_lnb = _gate_bwd("gate_dmid", uvp, dz, ln_g, ln_b, wm, bs_full)
    g_win = _mm_tn("gate_dwin", y0, duvp, t1=1024, tn=1024, tm=DW_TOKENS, slot_cols=win.shape[2])
    token = put_grads("gate", [g_win, g_wout.reshape(N_DEV, -1, D)])
    grad_x, dg_mix0 = _mm_nt_rms_bwd("gate_dy", [(duvp, win, *_whole(win))], xs, g_mix[0], dh1, tm=512, nc=512,
                                     after=token, shards=True, emit_bf16=False)

    small_g = dict(
        mix_norm_g=jnp.concatenate([dg_mix0, dg_mix1], axis=0),
        mlp_norm_g=jnp.concatenate([dg_mlp0, dg_mlp1], axis=0),
        final_norm_g=dg_fin[0], a_ln_g=d_lng, a_ln_b=d_lnb, a_w_s=d_wm[None],
        a_b_s=jnp.sum(d_mixed.reshape(CHUNK, GROUPS, D // GROUPS), axis=2).T[None],
        rel_bias=_bias_grad("att_dbias", ds_sums))
    return loss_local, grad_x, small_g
```

```python
import functools
import math

import jax
import jax.numpy as jnp
from jax import lax
from jax.experimental import pallas as pl
from jax.experimental.pallas import tpu as pltpu

F32 = jnp.float32
BF16 = jnp.bfloat16
MESH = pl.DeviceIdType.MESH

N_DEV = 8
EPS = 1e-6
NEG_INF = -1e30
CHUNK = 128
GROUPS = 8
HEAD_DIM = 64
ATT_HEADS = 8
ATT_WIDTH = ATT_HEADS * HEAD_DIM
DILATIONS = (1, 4, 16)
N_DIL = len(DILATIONS)
N_BUCKETS = 32
MAX_EXACT = N_BUCKETS // 2
REL_MAX_DISTANCE = 2048
ATT_ROWS = 2048
ATT_SCALE = HEAD_DIM ** -0.5
DW_TOKENS = 4096
LANES = 128

ADAM_LR = 0.001
ADAM_B1 = 0.9
ADAM_B2 = 0.999
ADAM_EPS = 1e-08
ADAM_WD = 0.01
ADAM_STEP = 10

VMEM_LIMIT_BYTES = 56 * 1024 * 1024


def _params(semantics=None):
    return pltpu.CompilerParams(dimension_semantics=semantics, vmem_limit_bytes=VMEM_LIMIT_BYTES)


def _bf(v):
    return v.astype(BF16)


def _dot(a, b, dims):
    return lax.dot_general(a, b, (dims, ((), ())), preferred_element_type=F32)


NN = ((1,), (0,))
NT = ((1,), (1,))
TN = ((0,), (0,))


def _after_operand(after):
    if after is None:
        return [], []
    return [after], [pl.BlockSpec(memory_space=pl.ANY)]


def _rms_fwd(name, x, g, tm=512, after=None):
    S, D = x.shape
    after_args, after_specs = _after_operand(after)

    def body(x_ref, g_ref, *rest):
        y_ref = rest[-1]
        xv = x_ref[...]
        r = lax.rsqrt(jnp.mean(xv * xv, axis=-1, keepdims=True) + EPS)
        y_ref[...] = _bf(xv * r * g_ref[...])

    return pl.pallas_call(
        body, name=name, grid=(S // tm,),
        in_specs=[pl.BlockSpec((tm, D), lambda i: (i, 0)), pl.BlockSpec((1, D), lambda i: (0, 0))] + after_specs,
        out_specs=pl.BlockSpec((tm, D), lambda i: (i, 0)),
        out_shape=jax.ShapeDtypeStruct((S, D), BF16),
        compiler_params=_params(("parallel",)),
    )(x, g, *after_args)


def _mm_res_loss(name, a, w, res, g, target, *, tm, nc):
    M, D = res.shape

    def body(a_ref, w_ref, r_ref, g_ref, t_ref, dh_ref, dg_ref, l_ref, dhb_ref, h_sc):
        i = pl.program_id(0)
        a_v = _bf(a_ref[...])
        for j in range(D // nc):
            cols, acc = _chunk_product([a_v], [w_ref], j, nc, False, False)
            h_sc[:, cols] = r_ref[:, cols] + acc
        xv = h_sc[...]
        r = lax.rsqrt(jnp.mean(xv * xv, axis=-1, keepdims=True) + EPS)
        xh = xv * r
        gv = g_ref[...]
        e = xh * gv - t_ref[...]
        dout = e / D
        dyg = dout * gv
        c = jnp.mean(dyg * xh, axis=-1, keepdims=True)
        dh = r * (dyg - xh * c)
        dh_ref[...] = dh
        dhb_ref[...] = _bf(dh)
        dg_part = jnp.sum(dout * xh, axis=0, keepdims=True)
        l_part = jnp.sum(e * e, axis=0, keepdims=True)

        @pl.when(i == 0)
        def _():
            dg_ref[...] = dg_part
            l_ref[...] = l_part

        @pl.when(i > 0)
        def _():
            dg_ref[...] += dg_part
            l_ref[...] += l_part

    row = pl.BlockSpec((tm, D), lambda i: (i, 0))
    vec = pl.BlockSpec((1, D), lambda i: (0, 0))
    return pl.pallas_call(
        body, name=name, grid=(M // tm,),
        in_specs=[pl.BlockSpec((tm, a.shape[1]), lambda i: (i, 0)), pl.BlockSpec(w.shape, lambda i: (0, 0)),
                  row, vec, row],
        out_specs=[row, vec, vec, row],
        out_shape=[jax.ShapeDtypeStruct((M, D), F32), jax.ShapeDtypeStruct((1, D), F32),
                   jax.ShapeDtypeStruct((1, D), F32), jax.ShapeDtypeStruct((M, D), BF16)],
        scratch_shapes=[pltpu.VMEM((tm, D), F32)],
        compiler_params=_params(("arbitrary",)),
    )(a, w, res, g, target)


def _chunk_product(a_vals, w_refs, j, nc, nt, shards):
    cols = slice(j * nc, (j + 1) * nc)
    acc = None
    for a_v, w_ref in zip(a_vals, w_refs):
        if not shards:
            terms = [_dot(a_v, w_ref[cols, :], NT) if nt else _dot(a_v, w_ref[:, cols], NN)]
        elif nt:
            nl = w_ref.shape[2]
            terms = [_dot(a_v[:, k * nl:(k + 1) * nl], w_ref[k, cols, :], NT) for k in range(N_DEV)]
        else:
            terms = [_dot(a_v, w_ref[j], NN)]
        for t in terms:
            acc = t if acc is None else acc + t
    return cols, acc


def _mm_rows(name, pairs, n_out, *, nt, tm, nc, epi="plain", extra=None, out_dtype=F32, after=None, shards=False,
             norm_g=None):
    M = pairs[0][0].shape[0]
    np_ = len(pairs)
    after_args, after_specs = _after_operand(after)

    def body(*refs):
        a_refs = refs[:np_]
        w_refs = refs[np_:2 * np_]
        pos = 2 * np_
        e_ref = None
        if extra is not None:
            e_ref = refs[pos]
            pos += 1
        if norm_g is not None:
            g_ref = refs[pos]
            pos += 1
        pos += len(after_args)
        outs = refs[pos:]
        a_vals = [_bf(a[...]) for a in a_refs]
        for j in range(n_out // nc):
            cols, acc = _chunk_product(a_vals, w_refs, j, nc, nt, shards)
            if epi == "plain":
                outs[0][:, cols] = acc.astype(out_dtype)
            elif epi == "res":
                outs[0][:, cols] = e_ref[:, cols] + acc
            elif epi == "relu2":
                outs[0][:, cols] = _bf(acc)
                rl = jnp.maximum(acc, 0.0)
                outs[1][:, cols] = _bf(rl * rl)
            elif epi == "mask2relu":
                outs[0][:, cols] = _bf(acc * (2.0 * jnp.maximum(e_ref[:, cols].astype(F32), 0.0)))
        if norm_g is not None:
            hv = outs[0][...]
            r = lax.rsqrt(jnp.mean(hv * hv, axis=-1, keepdims=True) + EPS)
            outs[1][...] = _bf(hv * r * g_ref[...])

    in_specs = [pl.BlockSpec((tm, a.shape[1]), lambda i: (i, 0)) for a, _, _, _ in pairs]
    for _, _, wshape, widx in pairs:
        in_specs.append(pl.BlockSpec(wshape, functools.partial(lambda i, widx: widx, widx=widx)))
    args = [a for a, _, _, _ in pairs] + [w for _, w, _, _ in pairs]
    if extra is not None:
        in_specs.append(pl.BlockSpec((tm, n_out), lambda i: (i, 0)))
        args.append(extra)
    if norm_g is not None:
        in_specs.append(pl.BlockSpec((1, n_out), lambda i: (0, 0)))
        args.append(norm_g)
    in_specs += after_specs
    args += after_args
    row_out = pl.BlockSpec((tm, n_out), lambda i: (i, 0))
    if epi == "relu2":
        out_specs = [row_out, row_out]
        out_shape = [jax.ShapeDtypeStruct((M, n_out), BF16), jax.ShapeDtypeStruct((M, n_out), BF16)]
    elif norm_g is not None:
        out_specs = [row_out, row_out]
        out_shape = [jax.ShapeDtypeStruct((M, n_out), F32), jax.ShapeDtypeStruct((M, n_out), BF16)]
    else:
        dt = BF16 if epi == "mask2relu" else (F32 if epi == "res" else out_dtype)
        out_specs = row_out
        out_shape = jax.ShapeDtypeStruct((M, n_out), dt)
    return pl.pallas_call(
        body, name=name, grid=(M // tm,), in_specs=in_specs, out_specs=out_specs, out_shape=out_shape,
        compiler_params=_params(("parallel",)),
    )(*args)


def _whole(w):
    return w.shape, (0,) * w.ndim


def _mm_nn(name, a, w, **kw):
    n_out = w.shape[0] * w.shape[2] if w.ndim == 3 else w.shape[1]
    return _mm_rows(name, [(a, w, *_whole(w))], n_out, nt=False, **kw)


def _mm_nt(name, a, w, **kw):
    return _mm_rows(name, [(a, w, *_whole(w))], w.shape[0], nt=True, **kw)


def _mm_nt_rms_bwd(name, pairs, x, g, dres, *, tm, nc, after=None, shards=False, emit_bf16=True):
    M, D = x.shape
    np_ = len(pairs)
    n_out = 3 if emit_bf16 else 2
    after_args, after_specs = _after_operand(after)

    def body(*refs):
        a_refs = refs[:np_]
        w_refs = refs[np_:2 * np_]
        x_ref, g_ref, r_ref = refs[2 * np_:2 * np_ + 3]
        dy_sc = refs[-1]
        outs = refs[-1 - n_out:-1]
        dx_ref, dg_ref = outs[0], outs[1]
        i = pl.program_id(0)
        a_vals = [_bf(a[...]) for a in a_refs]
        for j in range(D // nc):
            cols, acc = _chunk_product(a_vals, w_refs, j, nc, True, shards)
            dy_sc[:, cols] = acc
        xv = x_ref[...]
        r = lax.rsqrt(jnp.mean(xv * xv, axis=-1, keepdims=True) + EPS)
        xh = xv * r
        dy_v = dy_sc[...]
        dyg = dy_v * g_ref[...]
        c = jnp.mean(dyg * xh, axis=-1, keepdims=True)
        dx = r_ref[...] + r * (dyg - xh * c)
        dx_ref[...] = dx
        if emit_bf16:
            outs[2][...] = _bf(dx)
        part = jnp.sum(dy_v * xh, axis=0, keepdims=True)

        @pl.when(i == 0)
        def _():
            dg_ref[...] = part

        @pl.when(i > 0)
        def _():
            dg_ref[...] += part

    row = pl.BlockSpec((tm, D), lambda i: (i, 0))
    vec = pl.BlockSpec((1, D), lambda i: (0, 0))
    in_specs = [pl.BlockSpec((tm, a.shape[1]), lambda i: (i, 0)) for a, _, _, _ in pairs]
    for _, _, wshape, widx in pairs:
        in_specs.append(pl.BlockSpec(wshape, functools.partial(lambda i, widx: widx, widx=widx)))
    args = [a for a, _, _, _ in pairs] + [w for _, w, _, _ in pairs]
    return pl.pallas_call(
        body, name=name, grid=(M // tm,),
        in_specs=in_specs + [row, vec, row] + after_specs,
        out_specs=[row, vec] + [row] * (n_out - 2),
        out_shape=[jax.ShapeDtypeStruct((M, D), F32), jax.ShapeDtypeStruct((1, D), F32)]
        + [jax.ShapeDtypeStruct((M, D), BF16)] * (n_out - 2),
        scratch_shapes=[pltpu.VMEM((tm, D), F32)],
        compiler_params=_params(("arbitrary",)),
    )(*args, x, g, dres, *after_args)


def _mm_tn(name, a, b, *, t1, tn, tm=2048, slot_cols=None):
    M, K1 = a.shape
    N = b.shape[1]
    nm = M // tm

    def body(a_ref, b_ref, o_ref, acc_ref):
        m = pl.program_id(2)
        t = _dot(_bf(a_ref[...]), _bf(b_ref[...]), TN)

        @pl.when(m == 0)
        def _():
            acc_ref[...] = t

        @pl.when(m > 0)
        def _():
            acc_ref[...] += t

        @pl.when(m == nm - 1)
        def _():
            if slot_cols is None:
                o_ref[...] = _bf(acc_ref[...])
            else:
                for k in range(tn // slot_cols):
                    o_ref[k] = _bf(acc_ref[:, k * slot_cols:(k + 1) * slot_cols])

    if slot_cols is not None:
        out_spec = pl.BlockSpec((tn // slot_cols, t1, slot_cols), lambda i, j, m: (j, i, 0))
        out_shape = jax.ShapeDtypeStruct((N // slot_cols, K1, slot_cols), BF16)
    else:
        out_spec = pl.BlockSpec((t1, tn), lambda i, j, m: (i, j))
        out_shape = jax.ShapeDtypeStruct((K1, N), BF16)
    return pl.pallas_call(
        body, name=name, grid=(K1 // t1, N // tn, nm),
        in_specs=[pl.BlockSpec((tm, t1), lambda i, j, m: (m, i)), pl.BlockSpec((tm, tn), lambda i, j, m: (m, j))],
        out_specs=out_spec, out_shape=out_shape,
        scratch_shapes=[pltpu.VMEM((t1, tn), F32)],
        compiler_params=_params(("parallel", "parallel", "arbitrary")),
    )(a, b)


_INV_SQRT2 = 1.0 / math.sqrt(2.0)
_INV_SQRT2PI = 1.0 / math.sqrt(2.0 * math.pi)


def _gelu(x):
    return 0.5 * x * (1.0 + lax.erf(x * _INV_SQRT2))


def _gelu_and_grad(x):
    cdf = 0.5 * (1.0 + lax.erf(x * _INV_SQRT2))
    return x * cdf, cdf + x * (_INV_SQRT2PI * jnp.exp(-0.5 * x * x))


def _layer_norm_parts(v):
    mu = jnp.mean(v, axis=-1, keepdims=True)
    xc = v - mu
    rs = lax.rsqrt(jnp.mean(xc * xc, axis=-1, keepdims=True) + EPS)
    return xc * rs, rs


def _gate_fwd(name, uvp, ln_g, ln_b, wm, bs_full, tr=512):
    S, W2 = uvp.shape
    W = W2 // 2
    gd = W // GROUPS

    def body(u_ref, v_ref, lg_ref, lb_ref, wm_ref, bs_ref, z_ref):
        vh, _ = _layer_norm_parts(_gelu(v_ref[...]))
        vn = _bf(vh * lg_ref[...] + lb_ref[...])
        for ci in range(tr // CHUNK):
            rows = slice(ci * CHUNK, (ci + 1) * CHUNK)
            for g in range(GROUPS):
                cols = slice(g * gd, (g + 1) * gd)
                mixed = _dot(wm_ref[g], vn[rows, cols], NN) + bs_ref[:, cols]
                z_ref[rows, cols] = _bf(_gelu(u_ref[rows, cols]) * mixed)

    vec = pl.BlockSpec((1, W), lambda i: (0, 0))
    return pl.pallas_call(
        body, name=name, grid=(S // tr,),
        in_specs=[pl.BlockSpec((tr, W), lambda i: (i, 0)), pl.BlockSpec((tr, W), lambda i: (i, 1)), vec, vec,
                  pl.BlockSpec((GROUPS, CHUNK, CHUNK), lambda i: (0, 0, 0)),
                  pl.BlockSpec((CHUNK, W), lambda i: (0, 0))],
        out_specs=pl.BlockSpec((tr, W), lambda i: (i, 0)),
        out_shape=jax.ShapeDtypeStruct((S, W), BF16),
        compiler_params=_params(("parallel",)),
    )(uvp, uvp, ln_g, ln_b, wm, bs_full)


def _gate_bwd(name, uvp, dz, ln_g, ln_b, wm, bs_full, tr=256):
    S, W2 = uvp.shape
    W = W2 // 2
    gd = W // GROUPS
    n_steps = S // tr

    def body(u_ref, v_ref, dz_ref, lg_ref, lb_ref, wm_ref, bs_ref, duv_ref, dwm_ref, dmx_ref, dlg_ref, dlb_ref,
             dvn_ref):
        i = pl.program_id(0)
        v, dv_dvp = _gelu_and_grad(v_ref[...])
        vh, rs = _layer_norm_parts(v)
        lg = lg_ref[...]
        vn = _bf(vh * lg + lb_ref[...])

        @pl.when(i == 0)
        def _():
            dwm_ref[...] = jnp.zeros_like(dwm_ref)
            dmx_ref[...] = jnp.zeros_like(dmx_ref)
            dlg_ref[...] = jnp.zeros_like(dlg_ref)
            dlb_ref[...] = jnp.zeros_like(dlb_ref)

        for ci in range(tr // CHUNK):
            rows = slice(ci * CHUNK, (ci + 1) * CHUNK)
            for g in range(GROUPS):
                cols = slice(g * gd, (g + 1) * gd)
                u, du_dup = _gelu_and_grad(u_ref[rows, cols])
                dz_v = dz_ref[rows, cols]
                dmixed = dz_v * u
                dmx_ref[:, cols] += dmixed
                dmixed_b = _bf(dmixed)
                mixed = _dot(wm_ref[g], vn[rows, cols], NN) + bs_ref[:, cols]
                duv_ref[rows, cols] = _bf(dz_v * mixed * du_dup)
                dwm_ref[g] += _dot(dmixed_b, vn[rows, cols], NT)
                dvn_ref[rows, cols] = _dot(wm_ref[g], dmixed_b, TN)
        dvn = dvn_ref[...]
        dlg_ref[...] += jnp.sum(dvn * vh, axis=0, keepdims=True)
        dlb_ref[...] += jnp.sum(dvn, axis=0, keepdims=True)
        dvh = dvn * lg
        dv = rs * (dvh - jnp.mean(dvh, axis=-1, keepdims=True) - vh * jnp.mean(dvh * vh, axis=-1, keepdims=True))
        duv_ref[:, W:] = _bf(dv * dv_dvp)

        @pl.when(i == n_steps - 1)
        def _():
            t_idx = lax.broadcasted_iota(jnp.int32, (CHUNK, CHUNK), 0)
            s_idx = lax.broadcasted_iota(jnp.int32, (CHUNK, CHUNK), 1)
            keep = (s_idx <= t_idx).astype(F32)
            for g in range(GROUPS):
                dwm_ref[g] = dwm_ref[g] * keep

    vec = pl.BlockSpec((1, W), lambda i: (0, 0))
    row = pl.BlockSpec((tr, W), lambda i: (i, 0))
    return pl.pallas_call(
        body, name=name, grid=(n_steps,),
        in_specs=[row, pl.BlockSpec((tr, W), lambda i: (i, 1)), row, vec, vec,
                  pl.BlockSpec((GROUPS, CHUNK, CHUNK), lambda i: (0, 0, 0)),
                  pl.BlockSpec((CHUNK, W), lambda i: (0, 0))],
        out_specs=[pl.BlockSpec((tr, W2), lambda i: (i, 0)),
                   pl.BlockSpec((GROUPS, CHUNK, CHUNK), lambda i: (0, 0, 0)),
                   pl.BlockSpec((CHUNK, W), lambda i: (0, 0)), vec, vec],
        out_shape=[jax.ShapeDtypeStruct((S, W2), BF16), jax.ShapeDtypeStruct((GROUPS, CHUNK, CHUNK), F32),
                   jax.ShapeDtypeStruct((CHUNK, W), F32), jax.ShapeDtypeStruct((1, W), F32),
                   jax.ShapeDtypeStruct((1, W), F32)],
        scratch_shapes=[pltpu.VMEM((tr, W), F32)],
        compiler_params=_params(("arbitrary",)),
    )(uvp, uvp, dz, ln_g, ln_b, wm, bs_full)


def _t5_bucket(distance):
    small = distance < MAX_EXACT
    nf = jnp.maximum(distance, 1).astype(F32)
    large = MAX_EXACT + (jnp.log(nf / MAX_EXACT) / math.log(REL_MAX_DISTANCE / MAX_EXACT)
                         * (N_BUCKETS - MAX_EXACT)).astype(jnp.int32)
    large = jnp.minimum(large, N_BUCKETS - 1)
    return jnp.where(small, distance, large)


TILE_ELEMS = 2 * CHUNK * CHUNK


def _band_buckets():
    rel = CHUNK + jnp.arange(CHUNK)[None, :] - jnp.arange(2 * CHUNK)[:, None]
    band = (rel >= 0) & (rel <= CHUNK)
    buckets = [_t5_bucket(jnp.clip(rel, 0, CHUNK) * d) for d in DILATIONS]
    return jnp.stack(buckets), band


def _bucket_onehot():
    buckets, _ = _band_buckets()
    return (buckets.reshape(N_DIL, 1, TILE_ELEMS) == jnp.arange(N_BUCKETS)[None, :, None]).astype(F32)


def _bias_tiles(name, rel_bias, after=None):
    _, band = _band_buckets()
    own = band & (jnp.arange(2 * CHUNK) >= CHUNK)[:, None]
    masks = jnp.stack([own, band]).reshape(2, TILE_ELEMS).astype(F32)
    tables = jnp.transpose(rel_bias.reshape(N_BUCKETS, N_DIL, ATT_HEADS), (1, 2, 0))
    after_args, after_specs = _after_operand(after)

    def body(t_ref, oh_ref, m_ref, *rest):
        out_ref = rest[-1]
        for g in range(N_DIL):
            bias = lax.dot_general(t_ref[g], oh_ref[g], (NN, ((), ())), precision=lax.Precision.HIGHEST,
                                   preferred_element_type=F32)
            for f in range(2):
                out_ref[g, f] = jnp.where(m_ref[f:f + 1, :] > 0.5, bias, NEG_INF)

    whole = pl.BlockSpec(memory_space=pltpu.VMEM)
    out = pl.pallas_call(
        body, name=name, out_shape=jax.ShapeDtypeStruct((N_DIL, 2, ATT_HEADS, TILE_ELEMS), F32),
        in_specs=[whole, whole, whole] + after_specs, out_specs=whole,
        compiler_params=_params(),
    )(tables, _bucket_onehot(), masks, *after_args)
    out = out.reshape(N_DIL, 2, ATT_HEADS // 2, 2, 2 * CHUNK, CHUNK)
    return jnp.transpose(out, (0, 1, 2, 4, 3, 5)).reshape(N_DIL, 2, ATT_HEADS // 2, 2 * CHUNK, 2 * CHUNK)


def _att_specs(order):
    def column(part, ids):
        hp, g, _ = order(*ids)
        return part * 3 * 4 + g * 4 + hp

    def window(part):
        def index(*ids):
            c = order(*ids)[2]
            return pl.multiple_of(jnp.maximum(c - 1, 0) * ATT_ROWS, ATT_ROWS), column(part, ids) * LANES
        return pl.BlockSpec((pl.Element(2 * ATT_ROWS), pl.Element(LANES)), index)

    return [pl.BlockSpec((ATT_ROWS, LANES), lambda *ids: (order(*ids)[2], column(0, ids))), window(1), window(2)]


def _window_base(c):
    return jnp.where(c == 0, 0, ATT_ROWS)


def _rows(start, d):
    if d == 1:
        return pl.ds(pl.multiple_of(start, CHUNK), CHUNK)
    return pl.ds(start, CHUNK, stride=d)


def _att_tile_offsets(t, d):
    n = t // d
    r = t % d
    return n * (CHUNK * d) + r, n


def _head_pair_columns(x_t):
    zeros = jnp.zeros((HEAD_DIM, CHUNK), x_t.dtype)
    return jnp.concatenate([jnp.concatenate([x_t[:HEAD_DIM], zeros], axis=0),
                            jnp.concatenate([zeros, x_t[HEAD_DIM:]], axis=0)], axis=1)


def _head_pair_rows(y):
    return jnp.concatenate([y[:HEAD_DIM, :CHUNK], y[HEAD_DIM:, CHUNK:]], axis=0)


def _att_fwd(name, qkv, bias_tiles):
    S = qkv.shape[0]
    n_chunks = S // ATT_ROWS
    tiles = ATT_ROWS // CHUNK

    def body(q_ref, kk, vv, b_ref, out_ref, lse_ref, o_sc, l_sc):
        c = pl.program_id(1)
        g = pl.program_id(2)
        base = _window_base(c)

        for gi, d in enumerate(DILATIONS):
            @pl.when(g == gi)
            def _(gi=gi, d=d):
                span = CHUNK * d

                def tile(t, carry):
                    q0, n = _att_tile_offsets(t, d)
                    first = (c == 0) & (n == 0)
                    rows = _rows(q0, d)
                    cur = _rows(base + q0, d)
                    prev = _rows(jnp.where(first, q0, base + q0 - span), d)
                    inner = jnp.where(first, 0, 1)
                    qq = _head_pair_columns(_bf(q_ref[rows, :] * ATT_SCALE).T)
                    s_p = _dot(_bf(kk[prev, :]), qq, NN) + b_ref[inner, 0:CHUNK, :]
                    s_c = _dot(_bf(kk[cur, :]), qq, NN) + b_ref[inner, CHUNK:2 * CHUNK, :]
                    m = jnp.maximum(jnp.max(s_p, axis=0, keepdims=True), jnp.max(s_c, axis=0, keepdims=True))
                    p_p = jnp.exp(s_p - m)
                    p_c = jnp.exp(s_c - m)
                    l = jnp.sum(p_p, axis=0, keepdims=True) + jnp.sum(p_c, axis=0, keepdims=True)
                    o2 = (_dot(_bf(vv[prev, :]).T, _bf(p_p), NN)
                          + _dot(_bf(vv[cur, :]).T, _bf(p_c), NN)) * (1.0 / l)
                    lse = m + jnp.log(l)
                    l_t = jnp.concatenate([jnp.broadcast_to(lse[:, :CHUNK], (HEAD_DIM, CHUNK)),
                                           jnp.broadcast_to(lse[:, CHUNK:], (HEAD_DIM, CHUNK))], axis=0)
                    o_sc[gi, rows, :] = _head_pair_rows(o2).T
                    l_sc[gi, rows, :] = l_t.T
                    return carry

                lax.fori_loop(0, tiles, tile, 0, unroll=16)

        @pl.when(g == N_DIL - 1)
        def _():
            for rows in (slice(i * 4 * CHUNK, (i + 1) * 4 * CHUNK) for i in range(ATT_ROWS // (4 * CHUNK))):
                ls = [l_sc[gi, rows, :] for gi in range(N_DIL)]
                mx = functools.reduce(jnp.maximum, ls)
                ws = [jnp.exp(l - mx) for l in ls]
                tot = functools.reduce(lambda a, b: a + b, ws)
                acc = ws[0] * o_sc[0, rows, :]
                for gi in range(1, N_DIL):
                    acc = acc + ws[gi] * o_sc[gi, rows, :]
                out_ref[rows, :] = acc / tot
                lse_ref[rows, :] = mx + jnp.log(tot)

    order = lambda hp, c, g: (hp, g, c)
    out_spec = pl.BlockSpec((ATT_ROWS, LANES), lambda hp, c, g: (c, hp))
    shape = jax.ShapeDtypeStruct((S, ATT_WIDTH), F32)
    return pl.pallas_call(
        body, name=name, grid=(ATT_HEADS // 2, n_chunks, N_DIL),
        in_specs=_att_specs(order) + [
            pl.BlockSpec((None, 2, None, 2 * CHUNK, 2 * CHUNK), lambda hp, c, g: (g, 0, hp, 0, 0))],
        out_specs=[out_spec, out_spec],
        out_shape=[shape, shape],
        scratch_shapes=[pltpu.VMEM((N_DIL, ATT_ROWS, LANES), F32), pltpu.VMEM((N_DIL, ATT_ROWS, LANES), F32)],
        compiler_params=_params(("parallel", "parallel", "arbitrary")),
    )(qkv, qkv, qkv, bias_tiles)


def _att_bwd(name, qkv, o, lse, d_o, bias_tiles):
    S = qkv.shape[0]
    n_chunks = S // ATT_ROWS
    tiles = ATT_ROWS // CHUNK

    def body(q_ref, kk, vv, o_ref, l_ref, do_ref, b_ref, dq_out, dk_out, dv_out, ds_ref, dq_ref, dk_ref, dv_ref):
        g = pl.program_id(1)
        c = pl.program_id(2)

        @pl.when(c == 0)
        def _():
            dk_ref[...] = jnp.zeros_like(dk_ref)
            dv_ref[...] = jnp.zeros_like(dv_ref)
            ds_ref[...] = jnp.zeros_like(ds_ref)

        base = _window_base(c)
        first_row = c * ATT_ROWS
        head0 = lax.broadcasted_iota(jnp.int32, (CHUNK, LANES), 1) < HEAD_DIM

        def head_pair_stack(x):
            zero = jnp.zeros_like(x)
            return jnp.concatenate([jnp.where(head0, x, zero), jnp.where(head0, zero, x)], axis=0)

        for gi, d in enumerate(DILATIONS):
            @pl.when(g == gi)
            def _(d=d):
                span = CHUNK * d

                def tile(t, carry):
                    q0, n = _att_tile_offsets(t, d)
                    first = (c == 0) & (n == 0)
                    rows = _rows(q0, d)
                    cur = _rows(base + q0, d)
                    prev = _rows(jnp.where(first, q0, base + q0 - span), d)
                    inner = jnp.where(first, 0, 1)
                    g_cur = _rows(first_row + q0, d)
                    g_prev = _rows(jnp.where(first, q0, first_row + q0 - span), d)
                    q2 = _bf(q_ref[rows, :] * ATT_SCALE)
                    q_t = q2.T
                    k2 = _bf(jnp.concatenate([kk[prev, :], kk[cur, :]], axis=0))
                    k_t = k2.T
                    v2 = _bf(jnp.concatenate([vv[prev, :], vv[cur, :]], axis=0))
                    do2 = do_ref[rows, :]
                    do_b = _bf(do2)
                    do_t = do_b.T
                    lse_t = l_ref[rows, :].T
                    dd_t = (do2 * o_ref[rows, :]).T
                    lse = jnp.concatenate([lse_t[0:1], lse_t[HEAD_DIM:HEAD_DIM + 1]], axis=1)
                    delta = jnp.concatenate([jnp.sum(dd_t[:HEAD_DIM], axis=0, keepdims=True),
                                             jnp.sum(dd_t[HEAD_DIM:], axis=0, keepdims=True)], axis=1)
                    s = _dot(k2, _head_pair_columns(q_t), NN) + b_ref[inner]
                    p = jnp.exp(s - lse)
                    ds = p * (_dot(v2, _head_pair_columns(do_t), NN) - delta)
                    ds_ref[...] += ds
                    ds_b = _bf(ds)
                    dq_t = _head_pair_rows(_dot(k_t, ds_b, NN))
                    dk2 = _dot(ds_b, head_pair_stack(q2), NN)
                    dv2 = _dot(_bf(p), head_pair_stack(do_b), NN)
                    dq_ref[rows, :] = (dq_t * ATT_SCALE).T
                    dk_ref[g_prev, :] += dk2[0:CHUNK]
                    dk_ref[g_cur, :] += dk2[CHUNK:2 * CHUNK]
                    dv_ref[g_prev, :] += dv2[0:CHUNK]
                    dv_ref[g_cur, :] += dv2[CHUNK:2 * CHUNK]
                    return carry

                lax.fori_loop(0, tiles, tile, 0, unroll=16)

        dq_out[...] = _bf(dq_ref[...])

        @pl.when(c == n_chunks - 1)
        def _():
            dk_out[...] = _bf(dk_ref[...])
            dv_out[...] = _bf(dv_ref[...])

    order = lambda hp, g, c: (hp, g, c)
    chunk = pl.BlockSpec((ATT_ROWS, LANES), lambda hp, g, c: (c, hp))
    slab = pl.BlockSpec((S, LANES), lambda hp, g, c: (0, g * 4 + hp))
    width = N_DIL * ATT_WIDTH
    dq, dk, dv, ds_sums = pl.pallas_call(
        body, name=name, grid=(ATT_HEADS // 2, N_DIL, n_chunks),
        in_specs=_att_specs(order) + [chunk, chunk, chunk,
                                      pl.BlockSpec((None, 2, None, 2 * CHUNK, 2 * CHUNK),
                                                   lambda hp, g, c: (g, 0, hp, 0, 0))],
        out_specs=[pl.BlockSpec((ATT_ROWS, LANES), lambda hp, g, c: (c, g * 4 + hp)), slab, slab,
                   pl.BlockSpec((None, None, 2 * CHUNK, 2 * CHUNK), lambda hp, g, c: (g, hp, 0, 0))],
        out_shape=[jax.ShapeDtypeStruct((S, width), BF16), jax.ShapeDtypeStruct((S, width), BF16),
                   jax.ShapeDtypeStruct((S, width), BF16),
                   jax.ShapeDtypeStruct((N_DIL, ATT_HEADS // 2, 2 * CHUNK, 2 * CHUNK), F32)],
        scratch_shapes=[pltpu.VMEM((ATT_ROWS, LANES), F32), pltpu.VMEM((S, LANES), F32),
                        pltpu.VMEM((S, LANES), F32)],
        compiler_params=_params(("parallel", "parallel", "arbitrary")),
    )(qkv, qkv, qkv, o, lse, d_o, bias_tiles)
    ds_sums = ds_sums.reshape(N_DIL, ATT_HEADS // 2, 2 * CHUNK, 2, CHUNK)
    ds_sums = jnp.transpose(ds_sums, (0, 1, 3, 2, 4)).reshape(N_DIL, ATT_HEADS, 2 * CHUNK, CHUNK)
    return dq, dk, dv, ds_sums


def _bias_grad(name, ds_sums):
    flat = ds_sums.reshape(N_DIL, ATT_HEADS, TILE_ELEMS)

    def body(oh_ref, ds_ref, out_ref):
        for g in range(N_DIL):
            out_ref[g] = lax.dot_general(oh_ref[g], ds_ref[g], (NT, ((), ())), precision=lax.Precision.HIGHEST,
                                         preferred_element_type=F32)

    out = pl.pallas_call(
        body, name=name, out_shape=jax.ShapeDtypeStruct((N_DIL, N_BUCKETS, ATT_HEADS), F32),
        compiler_params=_params(),
    )(_bucket_onehot(), flat)
    return jnp.transpose(out, (1, 0, 2)).reshape(N_BUCKETS, N_DIL * ATT_HEADS)


def _peers():
    x, y, c = lax.axis_index("x"), lax.axis_index("y"), lax.axis_index("c")
    me = 4 * x + 2 * y + c
    others = [(x, y, 1 - c), (1 - x, y, c), (x, 1 - y, c), (1 - x, 1 - y, c),
              (1 - x, y, 1 - c), (x, 1 - y, 1 - c), (1 - x, 1 - y, 1 - c)]
    return me, others


def _slot(dev):
    return 4 * dev[0] + 2 * dev[1] + dev[2]


_HBM =pl.BlockSpec(memory_space=pltpu.HBM)
_SEM = pl.BlockSpec(memory_space=pltpu.SEMAPHORE)
_EFFECT = pltpu.SideEffectType.DATAFLOW_SIDE_EFFECTING


def _exchange_copy(src_ref, land_ref, send_sems, recv_sems, k, dev, me, scatter, arriving):
    src = src_ref.at[me if arriving else _slot(dev)] if scatter else src_ref
    dst = land_ref.at[_slot(dev) if arriving else me]
    return pltpu.make_async_remote_copy(src_ref=src, dst_ref=dst, send_sem=send_sems.at[k], recv_sem=recv_sems.at[k],
                                        device_id=dev, device_id_type=MESH)


def _own_copy(src_ref, land_ref, local_sems, p, me, scatter):
    return pltpu.make_async_copy(src_ref.at[me] if scatter else src_ref, land_ref.at[me], local_sems.at[p])


def _exchange_start(name, srcs, scatter):
    n = len(srcs)
    landings = [lax.empty((N_DEV,) + src.shape[-2:], src.dtype) for src in srcs]

    def body(*refs):
        src_refs, land_refs = refs[:n], refs[n:2 * n]
        send_sems, recv_sems, local_sems = refs[2 * n:2 * n + 3]
        token = refs[-1]
        me, others = _peers()
        for p in range(n):
            _own_copy(src_refs[p], land_refs[p], local_sems, p, me, scatter).start()
            for k, dev in enumerate(others):
                _exchange_copy(src_refs[p], land_refs[p], send_sems, recv_sems, p * (N_DEV - 1) + k, dev, me,
                               scatter, False).start()
        token[...] = jnp.zeros_like(token)

    sems = pltpu.SemaphoreType.DMA((n * (N_DEV - 1),))
    hbm = lambda a: pltpu.with_memory_space_constraint(a, pltpu.HBM)
    outs = pl.pallas_call(
        body, name=name,
        out_shape=(sems, sems, pltpu.SemaphoreType.DMA((n,)), *[pltpu.HBM(a.shape, a.dtype) for a in srcs + landings],
                   jax.ShapeDtypeStruct((8, LANES), F32)),
        in_specs=(_HBM,) * (2 * n),
        out_specs=(_SEM, _SEM, _SEM) + (_HBM,) * (2 * n) + (pl.BlockSpec(memory_space=pltpu.VMEM),),
        input_output_aliases={i: 3 + i for i in range(2 * n)},
        compiler_params=pltpu.CompilerParams(has_side_effects=_EFFECT),
    )(*[hbm(a) for a in srcs + landings])
    return (outs[0], outs[1], outs[2], list(outs[3:3 + n]), list(outs[3 + n:3 + 2 * n]), scatter), outs[-1]


def _exchange_wait(name, handle, after):
    send_sems, recv_sems, local_sems, src_thru, land_thru, scatter = handle
    n = len(src_thru)

    def body(*refs):
        src_refs, land_refs = refs[:n], refs[n:2 * n]
        send_sems, recv_sems, local_sems = refs[2 * n:2 * n + 3]
        me, others = _peers()
        for p in range(n):
            _own_copy(src_refs[p], land_refs[p], local_sems, p, me, scatter).wait()
            for k, dev in enumerate(others):
                cp = _exchange_copy(src_refs[p], land_refs[p], send_sems, recv_sems, p * (N_DEV - 1) + k, dev, me,
                                    scatter, True)
                cp.wait_send()
                cp.wait_recv()

    outs = pl.pallas_call(
        body, name=name,
        out_shape=tuple(pltpu.HBM(a.shape, a.dtype) for a in src_thru + land_thru),
        in_specs=(_HBM,) * (2 * n) + (_SEM, _SEM, _SEM, pl.BlockSpec(memory_space=pl.ANY)),
        out_specs=(_HBM,) * (2 * n),
        input_output_aliases={i: i for i in range(2 * n)},
        compiler_params=pltpu.CompilerParams(has_side_effects=_EFFECT),
    )(*src_thru, *land_thru, send_sems, recv_sems, local_sems, after)
    return list(outs[n:])


def _adamw_math(w, g, m, v):
    m = ADAM_B1 * m + (1.0 - ADAM_B1) * g
    v = ADAM_B2 * v + (1.0 - ADAM_B2) * (g * g)
    m_hat = m / (1.0 - ADAM_B1 ** ADAM_STEP)
    v_hat = v / (1.0 - ADAM_B2 ** ADAM_STEP)
    delta = -ADAM_LR * (m_hat / (jnp.sqrt(v_hat) + ADAM_EPS) + ADAM_WD * w)
    return delta, m, v


_SMALL_WIDE = (("mix_norm_g", 2), ("mlp_norm_g", 2), ("final_norm_g", 1), ("a_ln_g", 1), ("a_ln_b", 1))
_SMALL_NARROW = (("a_w_s", GROUPS * CHUNK), ("a_b_s", GROUPS), ("rel_bias", N_BUCKETS))
_SMALL = tuple(n for n, _ in _SMALL_WIDE + _SMALL_NARROW)
_BIAS_COLS = N_DIL * ATT_HEADS


def _pack_small_grads(grads, loss_term):
    D = grads["a_ln_g"].shape[-1]
    tiles = [jnp.pad(grads[n].reshape(k, D), ((0, 8 - k), (0, 0))) for n, k in _SMALL_WIDE]
    tiles.append(jnp.pad(loss_term.reshape(1, 1), ((0, 7), (0, D - 1))))
    narrow = [grads["a_w_s"].reshape(-1, LANES), grads["a_b_s"].reshape(-1, LANES),
              jnp.pad(grads["rel_bias"], ((0, 0), (0, LANES - _BIAS_COLS)))]
    return jnp.concatenate(tiles, axis=0), jnp.concatenate(narrow, axis=0)


def _adamw_small(name, g_wide, g_narrow, w, m, v):
    D = g_wide.shape[-1]
    shapes = {n: (k, D) for n, k in _SMALL_WIDE}
    shapes.update({n: (k, LANES) for n, k in _SMALL_NARROW})
    shapes["rel_bias"] = (N_BUCKETS, _BIAS_COLS)
    n_t = len(_SMALL)

    def body(gw_ref, gn_ref, *rest):
        params = rest[:3 * n_t]
        outs = rest[3 * n_t:3 * n_t + 4 * n_t]
        loss_ref, sw, sn = rest[-3:]
        sw[...] = functools.reduce(lambda a, b: a + b, [gw_ref[j] for j in range(N_DEV)])
        sn[...] = functools.reduce(lambda a, b: a + b, [gn_ref[j] for j in range(N_DEV)])
        row = 0
        for i, n in enumerate(_SMALL):
            k, cols = shapes[n]
            if i < len(_SMALL_WIDE):
                g = sw[8 * i:8 * i + k, :]
            else:
                g = sn[row:row + k, 0:cols]
                row += k
            w_ref, m_ref, v_ref = params[3 * i:3 * i + 3]
            delta, m_new, v_new = _adamw_math(w_ref[...], g, m_ref[...], v_ref[...])
            for out, val in zip(outs[4 * i:4 * i + 4], (g, delta, m_new, v_new)):
                out[...] = val
        loss_ref[...] = sw[8 * len(_SMALL_WIDE):8 * len(_SMALL_WIDE) + 8, 0:LANES]

    whole = pl.BlockSpec(memory_space=pltpu.VMEM)
    args = [t[n].reshape(shapes[n]) for n in _SMALL for t in (w, m, v)]
    res = pl.pallas_call(
        body, name=name,
        in_specs=[whole] * (2 + len(args)), out_specs=[whole] * (4 * n_t + 1),
        out_shape=[jax.ShapeDtypeStruct(shapes[n], F32) for n in _SMALL for _ in range(4)]
        + [jax.ShapeDtypeStruct((8, LANES), F32)],
        scratch_shapes=[pltpu.VMEM(g_wide.shape[1:], F32), pltpu.VMEM(g_narrow.shape[1:], F32)],
        compiler_params=_params(),
    )(g_wide, g_narrow, *args)
    small = {n: tuple(r.reshape(w[n].shape) for r in res[4 * i:4 * i + 4]) for i, n in enumerate(_SMALL)}
    return small, res[-1][0, 0]


def _adamw_shard(name, parts, w, m, v, layer, earlier=None, after=None, tr=256):
    L, K, N = w.shape
    tr = min(tr, K)
    n_prev = 0 if earlier is None else 4
    after_args, after_specs = _after_operand(after)

    def body(p_ref, w_ref, m_ref, v_ref, *rest):
        g_out, d_out, m_out, v_out = rest[n_prev + len(after_args):]
        g = p_ref[0].astype(F32)
        for j in range(1, N_DEV):
            g = g + p_ref[j].astype(F32)
        delta, m_new, v_new = _adamw_math(w_ref[...], g, m_ref[...], v_ref[...])
        g_out[...] = g
        d_out[...] = delta
        m_out[...] = m_new
        v_out[...] = v_new

    row = pl.BlockSpec((None, tr, N), lambda i: (layer, i, 0))
    shape = jax.ShapeDtypeStruct((L, K, N), F32)
    return pl.pallas_call(
        body, name=name, grid=(K // tr,),
        in_specs=[pl.BlockSpec((N_DEV, tr, N), lambda i: (0, i, 0)), row, row, row]
        + [pl.BlockSpec(memory_space=pl.ANY)] * n_prev + after_specs,
        out_specs=[row, row, row, row],
        out_shape=[shape, shape, shape, shape],
        input_output_aliases={4 + j: j for j in range(n_prev)},
        compiler_params=_params(("parallel",)),
    )(parts, w, m, v, *(earlier or ()), *after_args)


def _column_slots(full):
    K, N = full.shape
    return jnp.transpose(full.reshape(K, N_DEV, N // N_DEV), (1, 0, 2))


def _from_column_slots(slots):
    _, K, n = slots.shape
    return jnp.transpose(slots, (1, 0, 2)).reshape(K, N_DEV * n)


_STAGES = (("gate", ("a_w_in", "a_w_out"), 0),
           ("mlp0", ("w_up", "w_down"), 0),
           ("att", ("b_w_qkv", "b_w_out"), 0),
           ("mlp1", ("w_up", "w_down"), 1))
_FETCHES = (("in", (("a_w_in", 0),)),
            ("out_up0", (("a_w_out", 0), ("w_up", 0))),
            ("down0", (("w_down", 0),)),
            ("att", (("b_w_qkv", 0), ("b_w_out", 0))),
            ("mlp1", (("w_up", 1), ("w_down", 1))))


def kernel(x, mix_norm_g, mlp_norm_g, final_norm_g, a_w_in, a_ln_g, a_ln_b, a_w_s, a_b_s, a_w_out, b_w_qkv, b_w_out, rel_bias, w_up, w_down, loss_target, m_mix_norm_g, m_mlp_norm_g, m_final_norm_g, m_a_w_in, m_a_ln_g, m_a_ln_b, m_a_w_s, m_a_b_s, m_a_w_out, m_b_w_qkv, m_b_w_out, m_rel_bias, m_w_up, m_w_down, v_mix_norm_g, v_mlp_norm_g, v_final_norm_g, v_a_w_in, v_a_ln_g, v_a_ln_b, v_a_w_s, v_a_b_s, v_a_w_out, v_b_w_qkv, v_b_w_out, v_rel_bias, v_w_up, v_w_down):
    w = dict(mix_norm_g=mix_norm_g, mlp_norm_g=mlp_norm_g, final_norm_g=final_norm_g, a_w_in=a_w_in, a_ln_g=a_ln_g,
             a_ln_b=a_ln_b, a_w_s=a_w_s, a_b_s=a_b_s, a_w_out=a_w_out, b_w_qkv=b_w_qkv, b_w_out=b_w_out,
             rel_bias=rel_bias, w_up=w_up, w_down=w_down)
    m = dict(mix_norm_g=m_mix_norm_g, mlp_norm_g=m_mlp_norm_g, final_norm_g=m_final_norm_g, a_w_in=m_a_w_in,
             a_ln_g=m_a_ln_g, a_ln_b=m_a_ln_b, a_w_s=m_a_w_s, a_b_s=m_a_b_s, a_w_out=m_a_w_out, b_w_qkv=m_b_w_qkv,
             b_w_out=m_b_w_out, rel_bias=m_rel_bias, w_up=m_w_up, w_down=m_w_down)
    v = dict(mix_norm_g=v_mix_norm_g, mlp_norm_g=v_mlp_norm_g, final_norm_g=v_final_norm_g, a_w_in=v_a_w_in,
             a_ln_g=v_a_ln_g, a_ln_b=v_a_ln_b, a_w_s=v_a_w_s, a_b_s=v_a_b_s, a_w_out=v_a_w_out, b_w_qkv=v_b_w_qkv,
             b_w_out=v_b_w_out, rel_bias=v_rel_bias, w_up=v_w_up, w_down=v_w_down)

    stages = {s: (names, layer) for s, names, layer in _STAGES}
    order = [s for s, _, _ in _STAGES]
    fetch = dict(_FETCHES)
    fetch_order = [s for s, _ in _FETCHES]

    def shards_of(step):
        return [_bf(w[n][layer]) for n, layer in fetch[step]]

    pending = {}
    pending[fetch_order[0]], first_token = _exchange_start("gather_" + fetch_order[0] + "_start",
                                                           shards_of(fetch_order[0]), False)

    def get_weights(step, dep):
        gathered = _exchange_wait("gather_" + step + "_wait", pending.pop(step), dep)
        nxt = fetch_order.index(step) + 1
        token = None
        if nxt < len(fetch_order):
            shards, gathered = lax.optimization_barrier((shards_of(fetch_order[nxt]), gathered))
            pending[fetch_order[nxt]], token = _exchange_start("gather_" + fetch_order[nxt] + "_start", shards, False)
        return gathered, token

    sent = {}

    def put_grads(stage, slot_grads):
        sent[stage], token = _exchange_start("scatter_" + stage + "_start", slot_grads, True)
        return token

    loss_local, grad_x, small_g = _local_step(
        x[0], loss_target[0], mix_norm_g, mlp_norm_g, final_norm_g, a_ln_g, a_ln_b, a_w_s, a_b_s, rel_bias,
        get_weights, put_grads, first_token)

    small_sent, token = _exchange_start("gather_small_start", list(_pack_small_grads(small_g, loss_local)), False)

    results = {}
    prev = token
    for stage in reversed(order):
        names, layer = stages[stage]
        received = _exchange_wait("scatter_" + stage + "_wait", sent[stage], prev)
        for n, parts in zip(names, received):
            results[n] = _adamw_shard("adamw_%s_%s" % (stage, n), parts, w[n], m[n], v[n], layer, results.get(n),
                                      after=prev)
            prev = results[n][0]

    g_wide, g_narrow = _exchange_wait("gather_small_wait", small_sent, prev)
    small, loss = _adamw_small("adamw_small", g_wide, g_narrow, w, m, v)

    outs = []
    for j in range(4):
        outs.extend(small[n][j] if n in _SMALL else results[n][j] for n in w)
    return (loss, grad_x[None], *outs)


def _local_step(xs, tgt, mix_norm_g, mlp_norm_g, final_norm_g, a_ln_g, a_ln_b, a_w_s, a_b_s, rel_bias,
                get_weights, put_grads, first_token=None):
    D = xs.shape[-1]
    g_mix = [mix_norm_g[l][None, :] for l in range(2)]
    g_mlp = [mlp_norm_g[l][None, :] for l in range(2)]
    g_fin = final_norm_g[None, :]
    ln_g, ln_b = a_ln_g, a_ln_b
    causal = jnp.tril(jnp.ones((CHUNK, CHUNK), dtype=bool))
    wm = _bf(jnp.where(causal[None], a_w_s[0], 0.0))
    bs_full = jnp.repeat(a_b_s[0].T, D // GROUPS, axis=1)
    bias_tiles = _bias_tiles("att_bias", rel_bias, after=first_token)

    (win,), token = get_weights("in", bias_tiles)
    y0 = _rms_fwd("rms_mix0", xs, g_mix[0], after=token)
    uvp = _mm_nn("gate_in", y0, win, tm=512, nc=win.shape[2], shards=True)
    z = _gate_fwd("gate_mid", uvp, ln_g, ln_b, wm, bs_full)
    (wout, wup0), token = get_weights("out_up0", z)
    wout = wout.reshape(-1, D)
    h1, y1 = _mm_nn("gate_out", z, wout, tm=512, nc=512, epi="res", extra=xs, norm_g=g_mlp[0], after=token)
    a0, f0 = _mm_nn("mlp0_up", y1, wup0, tm=512, nc=wup0.shape[2], epi="relu2", shards=True)
    (wdn0,), token = get_weights("down0", f0)
    wdn0 = wdn0.reshape(-1, D)
    h2, y2 = _mm_nn("mlp0_down", f0, wdn0, tm=512, nc=512, epi="res", extra=h1, norm_g=g_mix[1], after=token)
    (wqkv, wo), token = get_weights("att", h2)
    wqkv, wo = _from_column_slots(wqkv), _from_column_slots(wo)
    qkv = _mm_nn("att_qkv", y2, wqkv, tm=512, nc=512, after=token)
    o_att, lse = _att_fwd("att_fwd", qkv, bias_tiles)
    h3, y3 = _mm_nn("att_out", o_att, wo, tm=512, nc=512, epi="res", extra=h2, norm_g=g_mlp[1])
    (wup1, wdn1), _ = get_weights("mlp1", h3)
    wdn1 = wdn1.reshape(-1, D)
    a1, f1 = _mm_nn("mlp1_up", y3, wup1, tm=512, nc=wup1.shape[2], epi="relu2", shards=True)
    dh, dg_fin, err2, dh_b = _mm_res_loss("mlp1_down_loss", f1, wdn1, h3, g_fin, tgt, tm=512, nc=512)
    loss_local = 0.5 * jnp.sum(err2) / D

    def mlp_bwd(tag, dh, dh_b, h_in, y, a, f, wup_l, wdn_l, g_row, after):
        da = _mm_nt(tag + "_dact", dh_b, wdn_l, tm=512, nc=512, epi="mask2relu", extra=a, after=after)
        g_dn = _mm_tn(tag + "_dwdown", f, dh_b, t1=1024, tn=1024, tm=DW_TOKENS)
        g_up = _mm_tn(tag + "_dwup", y, da, t1=1024, tn=1024, tm=DW_TOKENS, slot_cols=wup_l.shape[2])
        dh_in, dg, dh_in_b = _mm_nt_rms_bwd(tag + "_dy", [(da, wup_l, *_whole(wup_l))], h_in, g_row, dh, tm=512,
                                            nc=512, shards=True)
        return dh_in, dh_in_b, dg, put_grads(tag, [g_up, g_dn.reshape(N_DEV, -1, D)])

    dh3, dh3_b, dg_mlp1, token = mlp_bwd("mlp1", dh, dh_b, h3, y3, a1, f1, wup1, wdn1, g_mlp[1], None)

    d_o = _mm_nt("att_dout", dh3_b, wo, tm=512, nc=512, after=token)
    g_wo = _mm_tn("att_dwo", o_att, dh3_b, t1=512, tn=1024, tm=DW_TOKENS)
    dq, dk, dv, ds_sums = _att_bwd("att_bwd", qkv, o_att, lse, d_o, bias_tiles)
    part_w = N_DIL * ATT_WIDTH
    g_qkv = [_mm_tn("att_dwqkv%d" % p, y2, t, t1=1024, tn=part_w) for p, t in enumerate((dq, dk, dv))]
    dh2, dg_mix1, dh2_b = _mm_nt_rms_bwd(
        "att_dy", [(t, wqkv, (D, part_w), (0, p)) for p, t in enumerate((dq, dk, dv))], h2, g_mix[1], dh3, tm=512,
        nc=512)
    token = put_grads("att", [_column_slots(jnp.concatenate(g_qkv, axis=1)), _column_slots(g_wo)])

    dh1, dh1_b, dg_mlp0, token = mlp_bwd("mlp0", dh2, dh2_b, h1, y1, a0, f0, wup0, wdn0, g_mlp[0], token)

    dz = _mm_nt("gate_dz", dh1_b, wout, tm=512, nc=512, after=token)
    g_wout = _mm_tn("gate_dwout", z, dh1_b, t1=1024, tn=1024, tm=DW_TOKENS)
    duvp, d_wm, d_mixed, d_lng, d_lnb = _gate_bwd("gate_dmid", uvp, dz, ln_g, ln_b, wm, bs_full)
    g_win = _mm_tn("gate_dwin", y0, duvp, t1=1024, tn=1024, tm=DW_TOKENS, slot_cols=win.shape[2])
    token = put_grads("gate", [g_win, g_wout.reshape(N_DEV, -1, D)])
    grad_x, dg_mix0 = _mm_nt_rms_bwd("gate_dy", [(duvp, win, *_whole(win))], xs, g_mix[0], dh1, tm=512, nc=512,
                                     after=token, shards=True, emit_bf16=False)

    small_g = dict(
        mix_norm_g=jnp.concatenate([dg_mix0, dg_mix1], axis=0),
        mlp_norm_g=jnp.concatenate([dg_mlp0, dg_mlp1], axis=0),
        final_norm_g=dg_fin[0], a_ln_g=d_lng, a_ln_b=d_lnb, a_w_s=d_wm[None],
        a_b_s=jnp.sum(d_mixed.reshape(CHUNK, GROUPS, D // GROUPS), axis=2).T[None],
        rel_bias=_bias_grad("att_dbias", ds_sums))
    return loss_local, grad_x, small_g
```

```python
import functools
import math

import jax
import jax.numpy as jnp
from jax import lax
from jax.experimental import pallas as pl
from jax.experimental.pallas import tpu as pltpu

F32 = jnp.float32
BF16 = jnp.bfloat16
MESH = pl.DeviceIdType.MESH

N_DEV = 8
EPS = 1e-6
NEG_INF = -1e30
CHUNK = 128
GROUPS = 8
HEAD_DIM = 64
ATT_HEADS = 8
ATT_WIDTH = ATT_HEADS * HEAD_DIM
DILATIONS = (1, 4, 16)
N_DIL = len(DILATIONS)
N_BUCKETS = 32
MAX_EXACT = N_BUCKETS // 2
REL_MAX_DISTANCE = 2048
ATT_ROWS = 2048
ATT_SCALE = HEAD_DIM ** -0.5
DW_TOKENS = 4096
LANES = 128

ADAM_LR = 0.001
ADAM_B1 = 0.9
ADAM_B2 = 0.999
ADAM_EPS = 1e-08
ADAM_WD = 0.01
ADAM_STEP = 10

VMEM_LIMIT_BYTES = 56 * 1024 * 1024


def _params(semantics=None):
    return pltpu.CompilerParams(dimension_semantics=semantics, vmem_limit_bytes=VMEM_LIMIT_BYTES)


def _bf(v):
    return v.astype(BF16)


def _dot(a, b, dims):
    return lax.dot_general(a, b, (dims, ((), ())), preferred_element_type=F32)


NN = ((1,), (0,))
NT = ((1,), (1,))
TN = ((0,), (0,))


def _after_operand(after):
    if after is None:
        return [], []
    return [after], [pl.BlockSpec(memory_space=pl.ANY)]


def _rms_fwd(name, x, g, tm=512, after=None):
    S, D = x.shape
    after_args, after_specs = _after_operand(after)

    def body(x_ref, g_ref, *rest):
        y_ref = rest[-1]
        xv = x_ref[...]
        r = lax.rsqrt(jnp.mean(xv * xv, axis=-1, keepdims=True) + EPS)
        y_ref[...] = _bf(xv * r * g_ref[...])

    return pl.pallas_call(
        body, name=name, grid=(S // tm,),
        in_specs=[pl.BlockSpec((tm, D), lambda i: (i, 0)), pl.BlockSpec((1, D), lambda i: (0, 0))] + after_specs,
        out_specs=pl.BlockSpec((tm, D), lambda i: (i, 0)),
        out_shape=jax.ShapeDtypeStruct((S, D), BF16),
        compiler_params=_params(("parallel",)),
    )(x, g, *after_args)


def _mm_res_loss(name, a, w, res, g, target, *, tm, nc):
    M, D = res.shape

    def body(a_ref, w_ref, r_ref, g_ref, t_ref, dh_ref, dg_ref, l_ref, dhb_ref, h_sc):
        i = pl.program_id(0)
        a_v = _bf(a_ref[...])
        for j in range(D // nc):
            cols, acc = _chunk_product([a_v], [w_ref], j, nc, False, False)
            h_sc[:, cols] = r_ref[:, cols] + acc
        xv = h_sc[...]
        r = lax.rsqrt(jnp.mean(xv * xv, axis=-1, keepdims=True) + EPS)
        xh = xv * r
        gv = g_ref[...]
        e = xh * gv - t_ref[...]
        dout = e / D
        dyg = dout * gv
        c = jnp.mean(dyg * xh, axis=-1, keepdims=True)
        dh = r * (dyg - xh * c)
        dh_ref[...] = dh
        dhb_ref[...] = _bf(dh)
        dg_part = jnp.sum(dout * xh, axis=0, keepdims=True)
        l_part = jnp.sum(e * e, axis=0, keepdims=True)

        @pl.when(i == 0)
        def _():
            dg_ref[...] = dg_part
            l_ref[...] = l_part

        @pl.when(i > 0)
        def _():
            dg_ref[...] += dg_part
            l_ref[...] += l_part

    row = pl.BlockSpec((tm, D), lambda i: (i, 0))
    vec = pl.BlockSpec((1, D), lambda i: (0, 0))
    return pl.pallas_call(
        body, name=name, grid=(M // tm,),
        in_specs=[pl.BlockSpec((tm, a.shape[1]), lambda i: (i, 0)), pl.BlockSpec(w.shape, lambda i: (0, 0)),
                  row, vec, row],
        out_specs=[row, vec, vec, row],
        out_shape=[jax.ShapeDtypeStruct((M, D), F32), jax.ShapeDtypeStruct((1, D), F32),
                   jax.ShapeDtypeStruct((1, D), F32), jax.ShapeDtypeStruct((M, D), BF16)],
        scratch_shapes=[pltpu.VMEM((tm, D), F32)],
        compiler_params=_params(("arbitrary",)),
    )(a, w, res, g, target)


def _chunk_product(a_vals, w_refs, j, nc, nt, shards):
    cols = slice(j * nc, (j + 1) * nc)
    acc = None
    for a_v, w_ref in zip(a_vals, w_refs):
        if not shards:
            terms = [_dot(a_v, w_ref[cols, :], NT) if nt else _dot(a_v, w_ref[:, cols], NN)]
        elif nt:
            nl = w_ref.shape[2]
            terms = [_dot(a_v[:, k * nl:(k + 1) * nl], w_ref[k, cols, :], NT) for k in range(N_DEV)]
        else:
            terms = [_dot(a_v, w_ref[j], NN)]
        for t in terms:
            acc = t if acc is None else acc + t
    return cols, acc


def _resident(shape, index):
    return pl.BlockSpec(shape, functools.partial(lambda i, index: index, index=index), pipeline_mode=pl.Buffered(1))


def _mm_rows(name, pairs, n_out, *, nt, tm, nc, epi="plain", extra=None, out_dtype=F32, after=None, shards=False,
             norm_g=None):
    M = pairs[0][0].shape[0]
    np_ = len(pairs)
    after_args, after_specs = _after_operand(after)

    def body(*refs):
        a_refs = refs[:np_]
        w_refs = refs[np_:2 * np_]
        pos = 2 * np_
        e_ref = None
        if extra is not None:
            e_ref = refs[pos]
            pos += 1
        if norm_g is not None:
            g_ref = refs[pos]
            pos += 1
        pos += len(after_args)
        outs = refs[pos:]
        a_vals = [_bf(a[...]) for a in a_refs]
        for j in range(n_out // nc):
            cols, acc = _chunk_product(a_vals, w_refs, j, nc, nt, shards)
            if epi == "plain":
                outs[0][:, cols] = acc.astype(out_dtype)
            elif epi == "res":
                outs[0][:, cols] = e_ref[:, cols] + acc
            elif epi == "relu2":
                outs[0][:, cols] = _bf(acc)
                rl = jnp.maximum(acc, 0.0)
                outs[1][:, cols] = _bf(rl * rl)
            elif epi == "mask2relu":
                outs[0][:, cols] = _bf(acc * (2.0 * jnp.maximum(e_ref[:, cols].astype(F32), 0.0)))
        if norm_g is not None:
            hv = outs[0][...]
            r = lax.rsqrt(jnp.mean(hv * hv, axis=-1, keepdims=True) + EPS)
            outs[1][...] = _bf(hv * r * g_ref[...])

    in_specs = [pl.BlockSpec((tm, a.shape[1]), lambda i: (i, 0)) for a, _, _, _ in pairs]
    for _, _, wshape, widx in pairs:
        in_specs.append(_resident(wshape, widx))
    args = [a for a, _, _, _ in pairs] + [w for _, w, _, _ in pairs]
    if extra is not None:
        in_specs.append(pl.BlockSpec((tm, n_out), lambda i: (i, 0)))
        args.append(extra)
    if norm_g is not None:
        in_specs.append(pl.BlockSpec((1, n_out), lambda i: (0, 0)))
        args.append(norm_g)
    in_specs += after_specs
    args += after_args
    row_out = pl.BlockSpec((tm, n_out), lambda i: (i, 0))
    if epi == "relu2":
        out_specs = [row_out, row_out]
        out_shape = [jax.ShapeDtypeStruct((M, n_out), BF16), jax.ShapeDtypeStruct((M, n_out), BF16)]
    elif norm_g is not None:
        out_specs = [row_out, row_out]
        out_shape = [jax.ShapeDtypeStruct((M, n_out), F32), jax.ShapeDtypeStruct((M, n_out), BF16)]
    else:
        dt = BF16 if epi == "mask2relu" else (F32 if epi == "res" else out_dtype)
        out_specs = row_out
        out_shape = jax.ShapeDtypeStruct((M, n_out), dt)
    return pl.pallas_call(
        body, name=name, grid=(M // tm,), in_specs=in_specs, out_specs=out_specs, out_shape=out_shape,
        compiler_params=_params(("parallel",)),
    )(*args)


def _whole(w):
    return w.shape, (0,) * w.ndim


def _mm_nn(name, a, w, **kw):
    n_out = w.shape[0] * w.shape[2] if w.ndim == 3 else w.shape[1]
    return _mm_rows(name, [(a, w, *_whole(w))], n_out, nt=False, **kw)


def _mm_nt(name, a, w, **kw):
    return _mm_rows(name, [(a, w, *_whole(w))], w.shape[0], nt=True, **kw)


def _mm_nt_rms_bwd(name, pairs, x, g, dres, *, tm, nc, after=None, shards=False, emit_bf16=True):
    M, D = x.shape
    np_ = len(pairs)
    n_out = 3 if emit_bf16 else 2
    after_args, after_specs = _after_operand(after)

    def body(*refs):
        a_refs = refs[:np_]
        w_refs = refs[np_:2 * np_]
        x_ref, g_ref, r_ref = refs[2 * np_:2 * np_ + 3]
        dy_sc = refs[-1]
        outs = refs[-1 - n_out:-1]
        dx_ref, dg_ref = outs[0], outs[1]
        i = pl.program_id(0)
        a_vals = [_bf(a[...]) for a in a_refs]
        for j in range(D // nc):
            cols, acc = _chunk_product(a_vals, w_refs, j, nc, True, shards)
            dy_sc[:, cols] = acc
        xv = x_ref[...]
        r = lax.rsqrt(jnp.mean(xv * xv, axis=-1, keepdims=True) + EPS)
        xh = xv * r
        dy_v = dy_sc[...]
        dyg = dy_v * g_ref[...]
        c = jnp.mean(dyg * xh, axis=-1, keepdims=True)
        dx = r_ref[...] + r * (dyg - xh * c)
        dx_ref[...] = dx
        if emit_bf16:
            outs[2][...] = _bf(dx)
        part = jnp.sum(dy_v * xh, axis=0, keepdims=True)

        @pl.when(i == 0)
        def _():
            dg_ref[...] = part

        @pl.when(i > 0)
        def _():
            dg_ref[...] += part

    row = pl.BlockSpec((tm, D), lambda i: (i, 0))
    vec = pl.BlockSpec((1, D), lambda i: (0, 0))
    in_specs = [pl.BlockSpec((tm, a.shape[1]), lambda i: (i, 0)) for a, _, _, _ in pairs]
    for _, _, wshape, widx in pairs:
        in_specs.append(_resident(wshape, widx))
    args = [a for a, _, _, _ in pairs] + [w for _, w, _, _ in pairs]
    return pl.pallas_call(
        body, name=name, grid=(M // tm,),
        in_specs=in_specs + [row, vec, row] + after_specs,
        out_specs=[row, vec] + [row] * (n_out - 2),
        out_shape=[jax.ShapeDtypeStruct((M, D), F32), jax.ShapeDtypeStruct((1, D), F32)]
        + [jax.ShapeDtypeStruct((M, D), BF16)] * (n_out - 2),
        scratch_shapes=[pltpu.VMEM((tm, D), F32)],
        compiler_params=_params(("arbitrary",)),
    )(*args, x, g, dres, *after_args)


def _mm_tn(name, a, b, *, t1, tn, tm=2048, slot_cols=None):
    M, K1 = a.shape
    N = b.shape[1]
    nm = M // tm

    def body(a_ref, b_ref, o_ref, acc_ref):
        m = pl.program_id(2)
        t = _dot(_bf(a_ref[...]), _bf(b_ref[...]), TN)

        @pl.when(m == 0)
        def _():
            acc_ref[...] = t

        @pl.when(m > 0)
        def _():
            acc_ref[...] += t

        @pl.when(m == nm - 1)
        def _():
            if slot_cols is None:
                o_ref[...] = _bf(acc_ref[...])
            else:
                for k in range(tn // slot_cols):
                    o_ref[k] = _bf(acc_ref[:, k * slot_cols:(k + 1) * slot_cols])

    if slot_cols is not None:
        out_spec = pl.BlockSpec((tn // slot_cols, t1, slot_cols), lambda i, j, m: (j, i, 0))
        out_shape = jax.ShapeDtypeStruct((N // slot_cols, K1, slot_cols), BF16)
    else:
        out_spec = pl.BlockSpec((t1, tn), lambda i, j, m: (i, j))
        out_shape = jax.ShapeDtypeStruct((K1, N), BF16)
    return pl.pallas_call(
        body, name=name, grid=(K1 // t1, N // tn, nm),
        in_specs=[pl.BlockSpec((tm, t1), lambda i, j, m: (m, i)), pl.BlockSpec((tm, tn), lambda i, j, m: (m, j))],
        out_specs=out_spec, out_shape=out_shape,
        scratch_shapes=[pltpu.VMEM((t1, tn), F32)],
        compiler_params=_params(("parallel", "parallel", "arbitrary")),
    )(a, b)


_INV_SQRT2 = 1.0 / math.sqrt(2.0)
_INV_SQRT2PI = 1.0 / math.sqrt(2.0 * math.pi)


def _gelu(x):
    return 0.5 * x * (1.0 + lax.erf(x * _INV_SQRT2))


def _gelu_and_grad(x):
    cdf = 0.5 * (1.0 + lax.erf(x * _INV_SQRT2))
    return x * cdf, cdf + x * (_INV_SQRT2PI * jnp.exp(-0.5 * x * x))


def _layer_norm_parts(v):
    mu = jnp.mean(v, axis=-1, keepdims=True)
    xc = v - mu
    rs = lax.rsqrt(jnp.mean(xc * xc, axis=-1, keepdims=True) + EPS)
    return xc * rs, rs


def _gate_fwd(name, uvp, ln_g, ln_b, wm, bs_full, tr=512):
    S, W2 = uvp.shape
    W = W2 // 2
    gd = W // GROUPS

    def body(u_ref, v_ref, lg_ref, lb_ref, wm_ref, bs_ref, z_ref):
        vh, _ = _layer_norm_parts(_gelu(v_ref[...]))
        vn = _bf(vh * lg_ref[...] + lb_ref[...])
        for ci in range(tr // CHUNK):
            rows = slice(ci * CHUNK, (ci + 1) * CHUNK)
            for g in range(GROUPS):
                cols = slice(g * gd, (g + 1) * gd)
                mixed = _dot(wm_ref[g], vn[rows, cols], NN) + bs_ref[:, cols]
                z_ref[rows, cols] = _bf(_gelu(u_ref[rows, cols]) * mixed)

    vec = pl.BlockSpec((1, W), lambda i: (0, 0))
    return pl.pallas_call(
        body, name=name, grid=(S // tr,),
        in_specs=[pl.BlockSpec((tr, W), lambda i: (i, 0)), pl.BlockSpec((tr, W), lambda i: (i, 1)), vec, vec,
                  pl.BlockSpec((GROUPS, CHUNK, CHUNK), lambda i: (0, 0, 0)),
                  pl.BlockSpec((CHUNK, W), lambda i: (0, 0))],
        out_specs=pl.BlockSpec((tr, W), lambda i: (i, 0)),
        out_shape=jax.ShapeDtypeStruct((S, W), BF16),
        compiler_params=_params(("parallel",)),
    )(uvp, uvp, ln_g, ln_b, wm, bs_full)


def _gate_bwd(name, uvp, dz, ln_g, ln_b, wm, bs_full, tr=256):
    S, W2 = uvp.shape
    W = W2 // 2
    gd = W // GROUPS
    n_steps = S // tr

    def body(u_ref, v_ref, dz_ref, lg_ref, lb_ref, wm_ref, bs_ref, duv_ref, dwm_ref, dmx_ref, dlg_ref, dlb_ref,
             dvn_ref):
        i = pl.program_id(0)
        v, dv_dvp = _gelu_and_grad(v_ref[...])
        vh, rs = _layer_norm_parts(v)
        lg = lg_ref[...]
        vn = _bf(vh * lg + lb_ref[...])

        @pl.when(i == 0)
        def _():
            dwm_ref[...] = jnp.zeros_like(dwm_ref)
            dmx_ref[...] = jnp.zeros_like(dmx_ref)
            dlg_ref[...] = jnp.zeros_like(dlg_ref)
            dlb_ref[...] = jnp.zeros_like(dlb_ref)

        for ci in range(tr // CHUNK):
            rows = slice(ci * CHUNK, (ci + 1) * CHUNK)
            for g in range(GROUPS):
                cols = slice(g * gd, (g + 1) * gd)
                u, du_dup = _gelu_and_grad(u_ref[rows, cols])
                dz_v = dz_ref[rows, cols]
                dmixed = dz_v * u
                dmx_ref[:, cols] += dmixed
                dmixed_b = _bf(dmixed)
                mixed = _dot(wm_ref[g], vn[rows, cols], NN) + bs_ref[:, cols]
                duv_ref[rows, cols] = _bf(dz_v * mixed * du_dup)
                dwm_ref[g] += _dot(dmixed_b, vn[rows, cols], NT)
                dvn_ref[rows, cols] = _dot(wm_ref[g], dmixed_b, TN)
        dvn = dvn_ref[...]
        dlg_ref[...] += jnp.sum(dvn * vh, axis=0, keepdims=True)
        dlb_ref[...] += jnp.sum(dvn, axis=0, keepdims=True)
        dvh = dvn * lg
        dv = rs * (dvh - jnp.mean(dvh, axis=-1, keepdims=True) - vh * jnp.mean(dvh * vh, axis=-1, keepdims=True))
        duv_ref[:, W:] = _bf(dv * dv_dvp)

        @pl.when(i == n_steps - 1)
        def _():
            t_idx = lax.broadcasted_iota(jnp.int32, (CHUNK, CHUNK), 0)
            s_idx = lax.broadcasted_iota(jnp.int32, (CHUNK, CHUNK), 1)
            keep = (s_idx <= t_idx).astype(F32)
            for g in range(GROUPS):
                dwm_ref[g] = dwm_ref[g] * keep

    vec = pl.BlockSpec((1, W), lambda i: (0, 0))
    row = pl.BlockSpec((tr, W), lambda i: (i, 0))
    return pl.pallas_call(
        body, name=name, grid=(n_steps,),
        in_specs=[row, pl.BlockSpec((tr, W), lambda i: (i, 1)), row, vec, vec,
                  pl.BlockSpec((GROUPS, CHUNK, CHUNK), lambda i: (0, 0, 0)),
                  pl.BlockSpec((CHUNK, W), lambda i: (0, 0))],
        out_specs=[pl.BlockSpec((tr, W2), lambda i: (i, 0)),
                   pl.BlockSpec((GROUPS, CHUNK, CHUNK), lambda i: (0, 0, 0)),
                   pl.BlockSpec((CHUNK, W), lambda i: (0, 0)), vec, vec],
        out_shape=[jax.ShapeDtypeStruct((S, W2), BF16), jax.ShapeDtypeStruct((GROUPS, CHUNK, CHUNK), F32),
                   jax.ShapeDtypeStruct((CHUNK, W), F32), jax.ShapeDtypeStruct((1, W), F32),
                   jax.ShapeDtypeStruct((1, W), F32)],
        scratch_shapes=[pltpu.VMEM((tr, W), F32)],
        compiler_params=_params(("arbitrary",)),
    )(uvp, uvp, dz, ln_g, ln_b, wm, bs_full)


def _t5_bucket(distance):
    small = distance < MAX_EXACT
    nf = jnp.maximum(distance, 1).astype(F32)
    large = MAX_EXACT + (jnp.log(nf / MAX_EXACT) / math.log(REL_MAX_DISTANCE / MAX_EXACT)
                         * (N_BUCKETS - MAX_EXACT)).astype(jnp.int32)
    large = jnp.minimum(large, N_BUCKETS - 1)
    return jnp.where(small, distance, large)


TILE_ELEMS = 2 * CHUNK * CHUNK


def _band_buckets():
    rel = CHUNK + jnp.arange(CHUNK)[None, :] - jnp.arange(2 * CHUNK)[:, None]
    band = (rel >= 0) & (rel <= CHUNK)
    buckets = [_t5_bucket(jnp.clip(rel, 0, CHUNK) * d) for d in DILATIONS]
    return jnp.stack(buckets), band


def _bucket_onehot():
    buckets, _ = _band_buckets()
    return (buckets.reshape(N_DIL, 1, TILE_ELEMS) == jnp.arange(N_BUCKETS)[None, :, None]).astype(F32)


def _bias_tiles(name, rel_bias, after=None):
    _, band = _band_buckets()
    own = band & (jnp.arange(2 * CHUNK) >= CHUNK)[:, None]
    masks = jnp.stack([own, band]).reshape(2, TILE_ELEMS).astype(F32)
    tables = jnp.transpose(rel_bias.reshape(N_BUCKETS, N_DIL, ATT_HEADS), (1, 2, 0))
    after_args, after_specs = _after_operand(after)

    def body(t_ref, oh_ref, m_ref, *rest):
        out_ref = rest[-1]
        for g in range(N_DIL):
            bias = lax.dot_general(t_ref[g], oh_ref[g], (NN, ((), ())), precision=lax.Precision.HIGHEST,
                                   preferred_element_type=F32)
            for f in range(2):
                out_ref[g, f] = jnp.where(m_ref[f:f + 1, :] > 0.5, bias, NEG_INF)

    whole = pl.BlockSpec(memory_space=pltpu.VMEM)
    out = pl.pallas_call(
        body, name=name, out_shape=jax.ShapeDtypeStruct((N_DIL, 2, ATT_HEADS, TILE_ELEMS), F32),
        in_specs=[whole, whole, whole] + after_specs, out_specs=whole,
        compiler_params=_params(),
    )(tables, _bucket_onehot(), masks, *after_args)
    out = out.reshape(N_DIL, 2, ATT_HEADS // 2, 2, 2 * CHUNK, CHUNK)
    return jnp.transpose(out, (0, 1, 2, 4, 3, 5)).reshape(N_DIL, 2, ATT_HEADS // 2, 2 * CHUNK, 2 * CHUNK)


def _att_specs(order):
    def column(part, ids):
        hp, g, _ = order(*ids)
        return part * 3 * 4 + g * 4 + hp

    def window(part):
        def index(*ids):
            c = order(*ids)[2]
            return pl.multiple_of(jnp.maximum(c - 1, 0) * ATT_ROWS, ATT_ROWS), column(part, ids) * LANES
        return pl.BlockSpec((pl.Element(2 * ATT_ROWS), pl.Element(LANES)), index)

    return [pl.BlockSpec((ATT_ROWS, LANES), lambda *ids: (order(*ids)[2], column(0, ids))), window(1), window(2)]


def _window_base(c):
    return jnp.where(c == 0, 0, ATT_ROWS)


def _rows(start, d):
    if d == 1:
        return pl.ds(pl.multiple_of(start, CHUNK), CHUNK)
    return pl.ds(start, CHUNK, stride=d)


def _att_tile_offsets(t, d):
    n = t // d
    r = t % d
    return n * (CHUNK * d) + r, n


def _head_pair_columns(x_t):
    zeros = jnp.zeros((HEAD_DIM, CHUNK), x_t.dtype)
    return jnp.concatenate([jnp.concatenate([x_t[:HEAD_DIM], zeros], axis=0),
                            jnp.concatenate([zeros, x_t[HEAD_DIM:]], axis=0)], axis=1)


def _head_pair_rows(y):
    return jnp.concatenate([y[:HEAD_DIM, :CHUNK], y[HEAD_DIM:, CHUNK:]], axis=0)


def _att_fwd(name, qkv, bias_tiles):
    S = qkv.shape[0]
    n_chunks = S // ATT_ROWS
    tiles = ATT_ROWS // CHUNK

    def body(q_ref, kk, vv, b_ref, out_ref, lse_ref, o_sc, l_sc):
        c = pl.program_id(1)
        g = pl.program_id(2)
        base = _window_base(c)

        for gi, d in enumerate(DILATIONS):
            @pl.when(g == gi)
            def _(gi=gi, d=d):
                span = CHUNK * d

                def tile(t, carry):
                    q0, n = _att_tile_offsets(t, d)
                    first = (c == 0) & (n == 0)
                    rows = _rows(q0, d)
                    cur = _rows(base + q0, d)
                    prev = _rows(jnp.where(first, q0, base + q0 - span), d)
                    inner = jnp.where(first, 0, 1)
                    qq = _head_pair_columns(_bf(q_ref[rows, :] * ATT_SCALE).T)
                    s_p = _dot(_bf(kk[prev, :]), qq, NN) + b_ref[inner, 0:CHUNK, :]
                    s_c = _dot(_bf(kk[cur, :]), qq, NN) + b_ref[inner, CHUNK:2 * CHUNK, :]
                    m = jnp.maximum(jnp.max(s_p, axis=0, keepdims=True), jnp.max(s_c, axis=0, keepdims=True))
                    p_p = jnp.exp(s_p - m)
                    p_c = jnp.exp(s_c - m)
                    l = jnp.sum(p_p, axis=0, keepdims=True) + jnp.sum(p_c, axis=0, keepdims=True)
                    o2 = (_dot(_bf(vv[prev, :]).T, _bf(p_p), NN)
                          + _dot(_bf(vv[cur, :]).T, _bf(p_c), NN)) * (1.0 / l)
                    lse = m + jnp.log(l)
                    l_t = jnp.concatenate([jnp.broadcast_to(lse[:, :CHUNK], (HEAD_DIM, CHUNK)),
                                           jnp.broadcast_to(lse[:, CHUNK:], (HEAD_DIM, CHUNK))], axis=0)
                    o_sc[gi, rows, :] = _head_pair_rows(o2).T
                    l_sc[gi, rows, :] = l_t.T
                    return carry

                lax.fori_loop(0, tiles, tile, 0, unroll=16)

        @pl.when(g == N_DIL - 1)
        def _():
            for rows in (slice(i * 4 * CHUNK, (i + 1) * 4 * CHUNK) for i in range(ATT_ROWS // (4 * CHUNK))):
                ls = [l_sc[gi, rows, :] for gi in range(N_DIL)]
                mx = functools.reduce(jnp.maximum, ls)
                ws = [jnp.exp(l - mx) for l in ls]
                tot = functools.reduce(lambda a, b: a + b, ws)
                acc = ws[0] * o_sc[0, rows, :]
                for gi in range(1, N_DIL):
                    acc = acc + ws[gi] * o_sc[gi, rows, :]
                out_ref[rows, :] = acc / tot
                lse_ref[rows, :] = mx + jnp.log(tot)

    order = lambda hp, c, g: (hp, g, c)
    out_spec = pl.BlockSpec((ATT_ROWS, LANES), lambda hp, c, g: (c, hp))
    shape = jax.ShapeDtypeStruct((S, ATT_WIDTH), F32)
    return pl.pallas_call(
        body, name=name, grid=(ATT_HEADS // 2, n_chunks, N_DIL),
        in_specs=_att_specs(order) + [
            pl.BlockSpec((None, 2, None, 2 * CHUNK, 2 * CHUNK), lambda hp, c, g: (g, 0, hp, 0, 0))],
        out_specs=[out_spec, out_spec],
        out_shape=[shape, shape],
        scratch_shapes=[pltpu.VMEM((N_DIL, ATT_ROWS, LANES), F32), pltpu.VMEM((N_DIL, ATT_ROWS, LANES), F32)],
        compiler_params=_params(("parallel", "parallel", "arbitrary")),
    )(qkv, qkv, qkv, bias_tiles)


def _att_bwd(name, qkv, o, lse, d_o, bias_tiles):
    S = qkv.shape[0]
    n_chunks = S // ATT_ROWS
    tiles = ATT_ROWS // CHUNK

    def body(q_ref, kk, vv, o_ref, l_ref, do_ref, b_ref, dq_out, dk_out, dv_out, ds_ref, dq_ref, dk_ref, dv_ref):
        g = pl.program_id(1)
        c = pl.program_id(2)

        @pl.when(c == 0)
        def _():
            dk_ref[...] = jnp.zeros_like(dk_ref)
            dv_ref[...] = jnp.zeros_like(dv_ref)
            ds_ref[...] = jnp.zeros_like(ds_ref)

        base = _window_base(c)
        first_row = c * ATT_ROWS
        head0 = lax.broadcasted_iota(jnp.int32, (CHUNK, LANES), 1) < HEAD_DIM

        def head_pair_stack(x):
            zero = jnp.zeros_like(x)
            return jnp.concatenate([jnp.where(head0, x, zero), jnp.where(head0, zero, x)], axis=0)

        for gi, d in enumerate(DILATIONS):
            @pl.when(g == gi)
            def _(d=d):
                span = CHUNK * d

                def tile(t, carry):
                    q0, n = _att_tile_offsets(t, d)
                    first = (c == 0) & (n == 0)
                    rows = _rows(q0, d)
                    cur = _rows(base + q0, d)
                    prev = _rows(jnp.where(first, q0, base + q0 - span), d)
                    inner = jnp.where(first, 0, 1)
                    g_cur = _rows(first_row + q0, d)
                    g_prev = _rows(jnp.where(first, q0, first_row + q0 - span), d)
                    q2 = _bf(q_ref[rows, :] * ATT_SCALE)
                    q_t = q2.T
                    k2 = _bf(jnp.concatenate([kk[prev, :], kk[cur, :]], axis=0))
                    k_t = k2.T
                    v2 = _bf(jnp.concatenate([vv[prev, :], vv[cur, :]], axis=0))
                    do2 = do_ref[rows, :]
                    do_b = _bf(do2)
                    do_t = do_b.T
                    lse_t = l_ref[rows, :].T
                    dd_t = (do2 * o_ref[rows, :]).T
                    lse = jnp.concatenate([lse_t[0:1], lse_t[HEAD_DIM:HEAD_DIM + 1]], axis=1)
                    delta = jnp.concatenate([jnp.sum(dd_t[:HEAD_DIM], axis=0, keepdims=True),
                                             jnp.sum(dd_t[HEAD_DIM:], axis=0, keepdims=True)], axis=1)
                    s = _dot(k2, _head_pair_columns(q_t), NN) + b_ref[inner]
                    p = jnp.exp(s - lse)
                    ds = p * (_dot(v2, _head_pair_columns(do_t), NN) - delta)
                    ds_ref[...] += ds
                    ds_b = _bf(ds)
                    dq_t = _head_pair_rows(_dot(k_t, ds_b, NN))
                    dk2 = _dot(ds_b, head_pair_stack(q2), NN)
                    dv2 = _dot(_bf(p), head_pair_stack(do_b), NN)
                    dq_ref[rows, :] = (dq_t * ATT_SCALE).T
                    dk_ref[g_prev, :] += dk2[0:CHUNK]
                    dk_ref[g_cur, :] += dk2[CHUNK:2 * CHUNK]
                    dv_ref[g_prev, :] += dv2[0:CHUNK]
                    dv_ref[g_cur, :] += dv2[CHUNK:2 * CHUNK]
                    return carry

                lax.fori_loop(0, tiles, tile, 0, unroll=16)

        dq_out[...] = _bf(dq_ref[...])

        @pl.when(c == n_chunks - 1)
        def _():
            dk_out[...] = _bf(dk_ref[...])
            dv_out[...] = _bf(dv_ref[...])

    order = lambda hp, g, c: (hp, g, c)
    chunk = pl.BlockSpec((ATT_ROWS, LANES), lambda hp, g, c: (c, hp))
    slab = pl.BlockSpec((S, LANES), lambda hp, g, c: (0, g * 4 + hp))
    width = N_DIL * ATT_WIDTH
    dq, dk, dv, ds_sums = pl.pallas_call(
        body, name=name, grid=(ATT_HEADS // 2, N_DIL, n_chunks),
        in_specs=_att_specs(order) + [chunk, chunk, chunk,
                                      pl.BlockSpec((None, 2, None, 2 * CHUNK, 2 * CHUNK),
                                                   lambda hp, g, c: (g, 0, hp, 0, 0))],
        out_specs=[pl.BlockSpec((ATT_ROWS, LANES), lambda hp, g, c: (c, g * 4 + hp)), slab, slab,
                   pl.BlockSpec((None, None, 2 * CHUNK, 2 * CHUNK), lambda hp, g, c: (g, hp, 0, 0))],
        out_shape=[jax.ShapeDtypeStruct((S, width), BF16), jax.ShapeDtypeStruct((S, width), BF16),
                   jax.ShapeDtypeStruct((S, width), BF16),
                   jax.ShapeDtypeStruct((N_DIL, ATT_HEADS // 2, 2 * CHUNK, 2 * CHUNK), F32)],
        scratch_shapes=[pltpu.VMEM((ATT_ROWS, LANES), F32), pltpu.VMEM((S, LANES), F32),
                        pltpu.VMEM((S, LANES), F32)],
        compiler_params=_params(("parallel", "parallel", "arbitrary")),
    )(qkv, qkv, qkv, o, lse, d_o, bias_tiles)
    ds_sums = ds_sums.reshape(N_DIL, ATT_HEADS // 2, 2 * CHUNK, 2, CHUNK)
    ds_sums = jnp.transpose(ds_sums, (0, 1, 3, 2, 4)).reshape(N_DIL, ATT_HEADS, 2 * CHUNK, CHUNK)
    return dq, dk, dv, ds_sums


def _bias_grad(name, ds_sums):
    flat = ds_sums.reshape(N_DIL, ATT_HEADS, TILE_ELEMS)

    def body(oh_ref, ds_ref, out_ref):
        for g in range(N_DIL):
            out_ref[g] = lax.dot_general(oh_ref[g], ds_ref[g], (NT, ((), ())), precision=lax.Precision.HIGHEST,
                                         preferred_element_type=F32)

    out = pl.pallas_call(
        body, name=name, out_shape=jax.ShapeDtypeStruct((N_DIL, N_BUCKETS, ATT_HEADS), F32),
        compiler_params=_params(),
    )(_bucket_onehot(), flat)
    return jnp.transpose(out, (1, 0, 2)).reshape(N_BUCKETS, N_DIL * ATT_HEADS)


def _peers():
    x, y, c = lax.axis_index("x"), lax.axis_index("y"), lax.axis_index("c")
    me = 4 * x + 2 * y + c
    others = [(x, y, 1 - c), (1 - x, y, c), (x, 1 - y, c), (1 - x, 1 - y, c),
              (1 - x, y, 1 - c), (x, 1 - y, 1 - c), (1 - x, 1 - y, 1 - c)]
    return me, others


def _slot(dev):
    return 4 * dev[0] + 2 * dev[1] + dev[2]


_HBM =pl.BlockSpec(memory_space=pltpu.HBM)
_SEM = pl.BlockSpec(memory_space=pltpu.SEMAPHORE)
_EFFECT = pltpu.SideEffectType.DATAFLOW_SIDE_EFFECTING


def _exchange_copy(src_ref, land_ref, send_sems, recv_sems, k, dev, me, scatter, arriving):
    src = src_ref.at[me if arriving else _slot(dev)] if scatter else src_ref
    dst = land_ref.at[_slot(dev) if arriving else me]
    return pltpu.make_async_remote_copy(src_ref=src, dst_ref=dst, send_sem=send_sems.at[k], recv_sem=recv_sems.at[k],
                                        device_id=dev, device_id_type=MESH)


def _own_copy(src_ref, land_ref, local_sems, p, me, scatter):
    return pltpu.make_async_copy(src_ref.at[me] if scatter else src_ref, land_ref.at[me], local_sems.at[p])


def _exchange_start(name, srcs, scatter):
    n = len(srcs)
    landings = [lax.empty((N_DEV,) + src.shape[-2:], src.dtype) for src in srcs]

    def body(*refs):
        src_refs, land_refs = refs[:n], refs[n:2 * n]
        send_sems, recv_sems, local_sems = refs[2 * n:2 * n + 3]
        token = refs[-1]
        me, others = _peers()
        for p in range(n):
            _own_copy(src_refs[p], land_refs[p], local_sems, p, me, scatter).start()
            for k, dev in enumerate(others):
                _exchange_copy(src_refs[p], land_refs[p], send_sems, recv_sems, p * (N_DEV - 1) + k, dev, me,
                               scatter, False).start()
        token[...] = jnp.zeros_like(token)

    sems = pltpu.SemaphoreType.DMA((n * (N_DEV - 1),))
    hbm = lambda a: pltpu.with_memory_space_constraint(a, pltpu.HBM)
    outs = pl.pallas_call(
        body, name=name,
        out_shape=(sems, sems, pltpu.SemaphoreType.DMA((n,)), *[pltpu.HBM(a.shape, a.dtype) for a in srcs + landings],
                   jax.ShapeDtypeStruct((8, LANES), F32)),
        in_specs=(_HBM,) * (2 * n),
        out_specs=(_SEM, _SEM, _SEM) + (_HBM,) * (2 * n) + (pl.BlockSpec(memory_space=pltpu.VMEM),),
        input_output_aliases={i: 3 + i for i in range(2 * n)},
        compiler_params=pltpu.CompilerParams(has_side_effects=_EFFECT),
    )(*[hbm(a) for a in srcs + landings])
    return (outs[0], outs[1], outs[2], list(outs[3:3 + n]), list(outs[3 + n:3 + 2 * n]), scatter), outs[-1]


def _exchange_wait(name, handle, after):
    send_sems, recv_sems, local_sems, src_thru, land_thru, scatter = handle
    n = len(src_thru)

    def body(*refs):
        src_refs, land_refs = refs[:n], refs[n:2 * n]
        send_sems, recv_sems, local_sems = refs[2 * n:2 * n + 3]
        me, others = _peers()
        for p in range(n):
            _own_copy(src_refs[p], land_refs[p], local_sems, p, me, scatter).wait()
            for k, dev in enumerate(others):
                cp = _exchange_copy(src_refs[p], land_refs[p], send_sems, recv_sems, p * (N_DEV - 1) + k, dev, me,
                                    scatter, True)
                cp.wait_send()
                cp.wait_recv()

    outs = pl.pallas_call(
        body, name=name,
        out_shape=tuple(pltpu.HBM(a.shape, a.dtype) for a in src_thru + land_thru),
        in_specs=(_HBM,) * (2 * n) + (_SEM, _SEM, _SEM, pl.BlockSpec(memory_space=pl.ANY)),
        out_specs=(_HBM,) * (2 * n),
        input_output_aliases={i: i for i in range(2 * n)},
        compiler_params=pltpu.CompilerParams(has_side_effects=_EFFECT),
    )(*src_thru, *land_thru, send_sems, recv_sems, local_sems, after)
    return list(outs[n:])


def _adamw_math(w, g, m, v):
    m = ADAM_B1 * m + (1.0 - ADAM_B1) * g
    v = ADAM_B2 * v + (1.0 - ADAM_B2) * (g * g)
    m_hat = m / (1.0 - ADAM_B1 ** ADAM_STEP)
    v_hat = v / (1.0 - ADAM_B2 ** ADAM_STEP)
    delta = -ADAM_LR * (m_hat / (jnp.sqrt(v_hat) + ADAM_EPS) + ADAM_WD * w)
    return delta, m, v


_SMALL_WIDE = (("mix_norm_g", 2), ("mlp_norm_g", 2), ("final_norm_g", 1), ("a_ln_g", 1), ("a_ln_b", 1))
_SMALL_NARROW = (("a_w_s", GROUPS * CHUNK), ("a_b_s", GROUPS), ("rel_bias", N_BUCKETS))
_SMALL = tuple(n for n, _ in _SMALL_WIDE + _SMALL_NARROW)
_BIAS_COLS = N_DIL * ATT_HEADS


def _pack_small_grads(grads, loss_term):
    D = grads["a_ln_g"].shape[-1]
    tiles = [jnp.pad(grads[n].reshape(k, D), ((0, 8 - k), (0, 0))) for n, k in _SMALL_WIDE]
    tiles.append(jnp.pad(loss_term.reshape(1, 1), ((0, 7), (0, D - 1))))
    narrow = [grads["a_w_s"].reshape(-1, LANES), grads["a_b_s"].reshape(-1, LANES),
              jnp.pad(grads["rel_bias"], ((0, 0), (0, LANES - _BIAS_COLS)))]
    return jnp.concatenate(tiles, axis=0), jnp.concatenate(narrow, axis=0)


def _adamw_small(name, g_wide, g_narrow, w, m, v):
    D = g_wide.shape[-1]
    shapes = {n: (k, D) for n, k in _SMALL_WIDE}
    shapes.update({n: (k, LANES) for n, k in _SMALL_NARROW})
    shapes["rel_bias"] = (N_BUCKETS, _BIAS_COLS)
    n_t = len(_SMALL)

    def body(gw_ref, gn_ref, *rest):
        params = rest[:3 * n_t]
        outs = rest[3 * n_t:3 * n_t + 4 * n_t]
        loss_ref, sw, sn = rest[-3:]
        sw[...] = functools.reduce(lambda a, b: a + b, [gw_ref[j] for j in range(N_DEV)])
        sn[...] = functools.reduce(lambda a, b: a + b, [gn_ref[j] for j in range(N_DEV)])
        row = 0
        for i, n in enumerate(_SMALL):
            k, cols = shapes[n]
            if i < len(_SMALL_WIDE):
                g = sw[8 * i:8 * i + k, :]
            else:
                g = sn[row:row + k, 0:cols]
                row += k
            w_ref, m_ref, v_ref = params[3 * i:3 * i + 3]
            delta, m_new, v_new = _adamw_math(w_ref[...], g, m_ref[...], v_ref[...])
            for out, val in zip(outs[4 * i:4 * i + 4], (g, delta, m_new, v_new)):
                out[...] = val
        loss_ref[...] = sw[8 * len(_SMALL_WIDE):8 * len(_SMALL_WIDE) + 8, 0:LANES]

    whole = pl.BlockSpec(memory_space=pltpu.VMEM)
    args = [t[n].reshape(shapes[n]) for n in _SMALL for t in (w, m, v)]
    res = pl.pallas_call(
        body, name=name,
        in_specs=[whole] * (2 + len(args)), out_specs=[whole] * (4 * n_t + 1),
        out_shape=[jax.ShapeDtypeStruct(shapes[n], F32) for n in _SMALL for _ in range(4)]
        + [jax.ShapeDtypeStruct((8, LANES), F32)],
        scratch_shapes=[pltpu.VMEM(g_wide.shape[1:], F32), pltpu.VMEM(g_narrow.shape[1:], F32)],
        compiler_params=_params(),
    )(g_wide, g_narrow, *args)
    small = {n: tuple(r.reshape(w[n].shape) for r in res[4 * i:4 * i + 4]) for i, n in enumerate(_SMALL)}
    return small, res[-1][0, 0]


def _adamw_shard(name, parts, w, m, v, layer, earlier=None, after=None, tr=256):
    L, K, N = w.shape
    tr = min(tr, K)
    n_prev = 0 if earlier is None else 4
    after_args, after_specs = _after_operand(after)

    def body(p_ref, w_ref, m_ref, v_ref, *rest):
        g_out, d_out, m_out, v_out = rest[n_prev + len(after_args):]
        g = p_ref[0].astype(F32)
        for j in range(1, N_DEV):
            g = g + p_ref[j].astype(F32)
        delta, m_new, v_new = _adamw_math(w_ref[...], g, m_ref[...], v_ref[...])
        g_out[...] = g
        d_out[...] = delta
        m_out[...] = m_new
        v_out[...] = v_new

    row = pl.BlockSpec((None, tr, N), lambda i: (layer, i, 0))
    shape = jax.ShapeDtypeStruct((L, K, N), F32)
    return pl.pallas_call(
        body, name=name, grid=(K // tr,),
        in_specs=[pl.BlockSpec((N_DEV, tr, N), lambda i: (0, i, 0)), row, row, row]
        + [pl.BlockSpec(memory_space=pl.ANY)] * n_prev + after_specs,
        out_specs=[row, row, row, row],
        out_shape=[shape, shape, shape, shape],
        input_output_aliases={4 + j: j for j in range(n_prev)},
        compiler_params=_params(("parallel",)),
    )(parts, w, m, v, *(earlier or ()), *after_args)


def _column_slots(full):
    K, N = full.shape
    return jnp.transpose(full.reshape(K, N_DEV, N // N_DEV), (1, 0, 2))


def _from_column_slots(slots):
    _, K, n = slots.shape
    return jnp.transpose(slots, (1, 0, 2)).reshape(K, N_DEV * n)


_STAGES = (("gate", ("a_w_in", "a_w_out"), 0),
           ("mlp0", ("w_up", "w_down"), 0),
           ("att", ("b_w_qkv", "b_w_out"), 0),
           ("mlp1", ("w_up", "w_down"), 1))


def kernel(x, mix_norm_g, mlp_norm_g, final_norm_g, a_w_in, a_ln_g, a_ln_b, a_w_s, a_b_s, a_w_out, b_w_qkv, b_w_out, rel_bias, w_up, w_down, loss_target, m_mix_norm_g, m_mlp_norm_g, m_final_norm_g, m_a_w_in, m_a_ln_g, m_a_ln_b, m_a_w_s, m_a_b_s, m_a_w_out, m_b_w_qkv, m_b_w_out, m_rel_bias, m_w_up, m_w_down, v_mix_norm_g, v_mlp_norm_g, v_final_norm_g, v_a_w_in, v_a_ln_g, v_a_ln_b, v_a_w_s, v_a_b_s, v_a_w_out, v_b_w_qkv, v_b_w_out, v_rel_bias, v_w_up, v_w_down):
    w = dict(mix_norm_g=mix_norm_g, mlp_norm_g=mlp_norm_g, final_norm_g=final_norm_g, a_w_in=a_w_in, a_ln_g=a_ln_g,
             a_ln_b=a_ln_b, a_w_s=a_w_s, a_b_s=a_b_s, a_w_out=a_w_out, b_w_qkv=b_w_qkv, b_w_out=b_w_out,
             rel_bias=rel_bias, w_up=w_up, w_down=w_down)
    m = dict(mix_norm_g=m_mix_norm_g, mlp_norm_g=m_mlp_norm_g, final_norm_g=m_final_norm_g, a_w_in=m_a_w_in,
             a_ln_g=m_a_ln_g, a_ln_b=m_a_ln_b, a_w_s=m_a_w_s, a_b_s=m_a_b_s, a_w_out=m_a_w_out, b_w_qkv=m_b_w_qkv,
             b_w_out=m_b_w_out, rel_bias=m_rel_bias, w_up=m_w_up, w_down=m_w_down)
    v = dict(mix_norm_g=v_mix_norm_g, mlp_norm_g=v_mlp_norm_g, final_norm_g=v_final_norm_g, a_w_in=v_a_w_in,
             a_ln_g=v_a_ln_g, a_ln_b=v_a_ln_b, a_w_s=v_a_w_s, a_b_s=v_a_b_s, a_w_out=v_a_w_out, b_w_qkv=v_b_w_qkv,
             b_w_out=v_b_w_out, rel_bias=v_rel_bias, w_up=v_w_up, w_down=v_w_down)

    stages = {s: (names, layer) for s, names, layer in _STAGES}
    order = [s for s, _, _ in _STAGES]

    def shards_of(stage):
        names, layer = stages[stage]
        return [_bf(w[n][layer]) for n in names]

    pending = {}
    pending[order[0]], first_token = _exchange_start("gather_" + order[0] + "_start", shards_of(order[0]), False)

    def get_weights(stage, dep):
        gathered = _exchange_wait("gather_" + stage + "_wait", pending.pop(stage), dep)
        nxt = order.index(stage) + 1
        token = None
        if nxt < len(order):
            shards, gathered = lax.optimization_barrier((shards_of(order[nxt]), gathered))
            pending[order[nxt]], token = _exchange_start("gather_" + order[nxt] + "_start", shards, False)
        return gathered, token

    sent = {}

    def put_grads(stage, slot_grads):
        sent[stage], token = _exchange_start("scatter_" + stage + "_start", slot_grads, True)
        return token

    loss_local, grad_x, small_g = _local_step(
        x[0], loss_target[0], mix_norm_g, mlp_norm_g, final_norm_g, a_ln_g, a_ln_b, a_w_s, a_b_s, rel_bias,
        get_weights, put_grads, first_token)

    small_sent, token = _exchange_start("gather_small_start", list(_pack_small_grads(small_g, loss_local)), False)

    results = {}
    prev = token
    for stage in reversed(order):
        names, layer = stages[stage]
        received = _exchange_wait("scatter_" + stage + "_wait", sent[stage], prev)
        for n, parts in zip(names, received):
            results[n] = _adamw_shard("adamw_%s_%s" % (stage, n), parts, w[n], m[n], v[n], layer, results.get(n),
                                      after=prev)
            prev = results[n][0]

    g_wide, g_narrow = _exchange_wait("gather_small_wait", small_sent, prev)
    small, loss = _adamw_small("adamw_small", g_wide, g_narrow, w, m, v)

    outs = []
    for j in range(4):
        outs.extend(small[n][j] if n in _SMALL else results[n][j] for n in w)
    return (loss, grad_x[None], *outs)


def _local_step(xs, tgt, mix_norm_g, mlp_norm_g, final_norm_g, a_ln_g, a_ln_b, a_w_s, a_b_s, rel_bias,
                get_weights, put_grads, first_token=None):
    D = xs.shape[-1]
    g_mix = [mix_norm_g[l][None, :] for l in range(2)]
    g_mlp = [mlp_norm_g[l][None, :] for l in range(2)]
    g_fin = final_norm_g[None, :]
    ln_g, ln_b = a_ln_g, a_ln_b
    causal = jnp.tril(jnp.ones((CHUNK, CHUNK), dtype=bool))
    wm = _bf(jnp.where(causal[None], a_w_s[0], 0.0))
    bs_full = jnp.repeat(a_b_s[0].T, D // GROUPS, axis=1)
    bias_tiles = _bias_tiles("att_bias", rel_bias, after=first_token)

    (win, wout), token = get_weights("gate", bias_tiles)
    wout = wout.reshape(-1, D)
    y0 = _rms_fwd("rms_mix0", xs, g_mix[0], after=token)
    uvp = _mm_nn("gate_in", y0, win, tm=512, nc=win.shape[2], shards=True)
    z = _gate_fwd("gate_mid", uvp, ln_g, ln_b, wm, bs_full)
    h1, y1 = _mm_nn("gate_out", z, wout, tm=512, nc=512, epi="res", extra=xs, norm_g=g_mlp[0])
    (wup0, wdn0), token = get_weights("mlp0", h1)
    wdn0 = wdn0.reshape(-1, D)
    a0, f0 = _mm_nn("mlp0_up", y1, wup0, tm=1024, nc=wup0.shape[2], epi="relu2", shards=True, after=token)
    h2, y2 = _mm_nn("mlp0_down", f0, wdn0, tm=1024, nc=512, epi="res", extra=h1, norm_g=g_mix[1])
    (wqkv, wo), token = get_weights("att", h2)
    wqkv, wo = _from_column_slots(wqkv), _from_column_slots(wo)
    qkv = _mm_nn("att_qkv", y2, wqkv, tm=512, nc=512, after=token)
    o_att, lse = _att_fwd("att_fwd", qkv, bias_tiles)
    h3, y3 = _mm_nn("att_out", o_att, wo, tm=512, nc=512, epi="res", extra=h2, norm_g=g_mlp[1])
    (wup1, wdn1), _ = get_weights("mlp1", h3)
    wdn1 = wdn1.reshape(-1, D)
    a1, f1 = _mm_nn("mlp1_up", y3, wup1, tm=1024, nc=wup1.shape[2], epi="relu2", shards=True)
    dh, dg_fin, err2, dh_b = _mm_res_loss("mlp1_down_loss", f1, wdn1, h3, g_fin, tgt, tm=512, nc=512)
    loss_local = 0.5 * jnp.sum(err2) / D

    def mlp_bwd(tag, dh, dh_b, h_in, y, a, f, wup_l, wdn_l, g_row, after):
        da = _mm_nt(tag + "_dact", dh_b, wdn_l, tm=1024, nc=512, epi="mask2relu", extra=a, after=after)
        g_dn = _mm_tn(tag + "_dwdown", f, dh_b, t1=1024, tn=1024, tm=DW_TOKENS)
        g_up = _mm_tn(tag + "_dwup", y, da, t1=1024, tn=1024, tm=DW_TOKENS, slot_cols=wup_l.shape[2])
        dh_in, dg, dh_in_b = _mm_nt_rms_bwd(tag + "_dy", [(da, wup_l, *_whole(wup_l))], h_in, g_row, dh, tm=512,
                                            nc=512, shards=True)
        return dh_in, dh_in_b, dg, put_grads(tag, [g_up, g_dn.reshape(N_DEV, -1, D)])

    dh3, dh3_b, dg_mlp1, token = mlp_bwd("mlp1", dh, dh_b, h3, y3, a1, f1, wup1, wdn1, g_mlp[1], None)

    d_o = _mm_nt("att_dout", dh3_b, wo, tm=512, nc=512, after=token)
    g_wo = _mm_tn("att_dwo", o_att, dh3_b, t1=512, tn=1024, tm=DW_TOKENS)
    dq, dk, dv, ds_sums = _att_bwd("att_bwd", qkv, o_att, lse, d_o, bias_tiles)
    part_w = N_DIL * ATT_WIDTH
    g_qkv = [_mm_tn("att_dwqkv%d" % p, y2, t, t1=1024, tn=part_w) for p, t in enumerate((dq, dk, dv))]
    dh2, dg_mix1, dh2_b = _mm_nt_rms_bwd(
        "att_dy", [(t, wqkv, (D, part_w), (0, p)) for p, t in enumerate((dq, dk, dv))], h2, g_mix[1], dh3, tm=512,
        nc=512)
    token = put_grads("att", [_column_slots(jnp.concatenate(g_qkv, axis=1)), _column_slots(g_wo)])

    dh1, dh1_b, dg_mlp0, token = mlp_bwd("mlp0", dh2, dh2_b, h1, y1, a0, f0, wup0, wdn0, g_mlp[0], token)

    dz = _mm_nt("gate_dz", dh1_b, wout, tm=512, nc=512, after=token)
    g_wout = _mm_tn("gate_dwout", z, dh1_b, t1=1024, tn=1024, tm=DW_TOKENS)
    duvp, d_wm, d_mixed, d_lng, d_lnb = _gate_bwd("gate_dmid", uvp, dz, ln_g, ln_b, wm, bs_full)
    g_win = _mm_tn("gate_dwin", y0, duvp, t1=1024, tn=1024, tm=DW_TOKENS, slot_cols=win.shape[2])
    token = put_grads("gate", [g_win, g_wout.reshape(N_DEV, -1, D)])
    grad_x, dg_mix0 = _mm_nt_rms_bwd("gate_dy", [(duvp, win, *_whole(win))], xs, g_mix[0], dh1, tm=512, nc=512,
                                     after=token, shards=True, emit_bf16=False)

    small_g = dict(
        mix_norm_g=jnp.concatenate([dg_mix0, dg_mix1], axis=0),
        mlp_norm_g=jnp.concatenate([dg_mlp0, dg_mlp1], axis=0),
        final_norm_g=dg_fin[0], a_ln_g=d_lng, a_ln_b=d_lnb, a_w_s=d_wm[None],
        a_b_s=jnp.sum(d_mixed.reshape(CHUNK, GROUPS, D // GROUPS), axis=2).T[None],
        rel_bias=_bias_grad("att_dbias", ds_sums))
    return loss_local, grad_x, small_g
```

```python
import functools
import math

import jax
import jax.numpy as jnp
from jax import lax
from jax.experimental import pallas as pl
from jax.experimental.pallas import tpu as pltpu

F32 = jnp.float32
BF16 = jnp.bfloat16
MESH = pl.DeviceIdType.MESH

N_DEV = 8
EPS = 1e-6
NEG_INF = -1e30
CHUNK = 128
GROUPS = 8
HEAD_DIM = 64
ATT_HEADS = 8
ATT_WIDTH = ATT_HEADS * HEAD_DIM
DILATIONS = (1, 4, 16)
N_DIL = len(DILATIONS)
N_BUCKETS = 32
MAX_EXACT = N_BUCKETS // 2
REL_MAX_DISTANCE = 2048
ATT_ROWS = 2048
ATT_SCALE = HEAD_DIM ** -0.5
DW_TOKENS = 4096
LANES = 128

ADAM_LR = 0.001
ADAM_B1 = 0.9
ADAM_B2 = 0.999
ADAM_EPS = 1e-08
ADAM_WD = 0.01
ADAM_STEP = 10

VMEM_LIMIT_BYTES = 56 * 1024 * 1024


def _params(semantics=None):
    return pltpu.CompilerParams(dimension_semantics=semantics, vmem_limit_bytes=VMEM_LIMIT_BYTES)


def _bf(v):
    return v.astype(BF16)


def _dot(a, b, dims):
    return lax.dot_general(a, b, (dims, ((), ())), preferred_element_type=F32)


NN = ((1,), (0,))
NT = ((1,), (1,))
TN = ((0,), (0,))


def _after_operand(after):
    if after is None:
        return [], []
    return [after], [pl.BlockSpec(memory_space=pl.ANY)]


def _rms_fwd(name, x, g, tm=512, after=None):
    S, D = x.shape
    after_args, after_specs = _after_operand(after)

    def body(x_ref, g_ref, *rest):
        y_ref = rest[-1]
        xv = x_ref[...]
        r = lax.rsqrt(jnp.mean(xv * xv, axis=-1, keepdims=True) + EPS)
        y_ref[...] = _bf(xv * r * g_ref[...])

    return pl.pallas_call(
        body, name=name, grid=(S // tm,),
        in_specs=[pl.BlockSpec((tm, D), lambda i: (i, 0)), pl.BlockSpec((1, D), lambda i: (0, 0))] + after_specs,
        out_specs=pl.BlockSpec((tm, D), lambda i: (i, 0)),
        out_shape=jax.ShapeDtypeStruct((S, D), BF16),
        compiler_params=_params(("parallel",)),
    )(x, g, *after_args)


def _mm_res_loss(name, a, w, res, g, target, *, tm, nc):
    M, D = res.shape

    def body(a_ref, w_ref, r_ref, g_ref, t_ref, dh_ref, dg_ref, l_ref, dhb_ref, h_sc):
        i = pl.program_id(0)
        a_v = _bf(a_ref[...])
        for j in range(D // nc):
            cols, acc = _chunk_product([a_v], [w_ref], j, nc, False, False)
            h_sc[:, cols] = r_ref[:, cols] + acc
        xv = h_sc[...]
        r = lax.rsqrt(jnp.mean(xv * xv, axis=-1, keepdims=True) + EPS)
        xh = xv * r
        gv = g_ref[...]
        e = xh * gv - t_ref[...]
        dout = e / D
        dyg = dout * gv
        c = jnp.mean(dyg * xh, axis=-1, keepdims=True)
        dh = r * (dyg - xh * c)
        dh_ref[...] = dh
        dhb_ref[...] = _bf(dh)
        dg_part = jnp.sum(dout * xh, axis=0, keepdims=True)
        l_part = jnp.sum(e * e, axis=0, keepdims=True)

        @pl.when(i == 0)
        def _():
            dg_ref[...] = dg_part
            l_ref[...] = l_part

        @pl.when(i > 0)
        def _():
            dg_ref[...] += dg_part
            l_ref[...] += l_part

    row = pl.BlockSpec((tm, D), lambda i: (i, 0))
    vec = pl.BlockSpec((1, D), lambda i: (0, 0))
    return pl.pallas_call(
        body, name=name, grid=(M // tm,),
        in_specs=[pl.BlockSpec((tm, a.shape[1]), lambda i: (i, 0)), pl.BlockSpec(w.shape, lambda i: (0, 0)),
                  row, vec, row],
        out_specs=[row, vec, vec, row],
        out_shape=[jax.ShapeDtypeStruct((M, D), F32), jax.ShapeDtypeStruct((1, D), F32),
                   jax.ShapeDtypeStruct((1, D), F32), jax.ShapeDtypeStruct((M, D), BF16)],
        scratch_shapes=[pltpu.VMEM((tm, D), F32)],
        compiler_params=_params(("arbitrary",)),
    )(a, w, res, g, target)


def _chunk_product(a_vals, w_refs, j, nc, nt, shards):
    cols = slice(j * nc, (j + 1) * nc)
    acc = None
    for a_v, w_ref in zip(a_vals, w_refs):
        if not shards:
            terms = [_dot(a_v, w_ref[cols, :], NT) if nt else _dot(a_v, w_ref[:, cols], NN)]
        elif nt:
            nl = w_ref.shape[2]
            terms = [_dot(a_v[:, k * nl:(k + 1) * nl], w_ref[k, cols, :], NT) for k in range(N_DEV)]
        else:
            terms = [_dot(a_v, w_ref[j], NN)]
        for t in terms:
            acc = t if acc is None else acc + t
    return cols, acc


def _mm_rows(name, pairs, n_out, *, nt, tm, nc, epi="plain", extra=None, out_dtype=F32, after=None, shards=False,
             norm_g=None):
    M = pairs[0][0].shape[0]
    np_ = len(pairs)
    after_args, after_specs = _after_operand(after)

    def body(*refs):
        a_refs = refs[:np_]
        w_refs = refs[np_:2 * np_]
        pos = 2 * np_
        e_ref = None
        if extra is not None:
            e_ref = refs[pos]
            pos += 1
        if norm_g is not None:
            g_ref = refs[pos]
            pos += 1
        pos += len(after_args)
        outs = refs[pos:]
        a_vals = [_bf(a[...]) for a in a_refs]
        for j in range(n_out // nc):
            cols, acc = _chunk_product(a_vals, w_refs, j, nc, nt, shards)
            if epi == "plain":
                outs[0][:, cols] = acc.astype(out_dtype)
            elif epi == "res":
                outs[0][:, cols] = e_ref[:, cols] + acc
            elif epi == "relu2":
                outs[0][:, cols] = _bf(acc)
                rl = jnp.maximum(acc, 0.0)
                outs[1][:, cols] = _bf(rl * rl)
            elif epi == "mask2relu":
                outs[0][:, cols] = _bf(acc * (2.0 * jnp.maximum(e_ref[:, cols].astype(F32), 0.0)))
        if norm_g is not None:
            hv = outs[0][...]
            r = lax.rsqrt(jnp.mean(hv * hv, axis=-1, keepdims=True) + EPS)
            outs[1][...] = _bf(hv * r * g_ref[...])

    in_specs = [pl.BlockSpec((tm, a.shape[1]), lambda i: (i, 0)) for a, _, _, _ in pairs]
    for _, _, wshape, widx in pairs:
        in_specs.append(pl.BlockSpec(wshape, functools.partial(lambda i, widx: widx, widx=widx)))
    args = [a for a, _, _, _ in pairs] + [w for _, w, _, _ in pairs]
    if extra is not None:
        in_specs.append(pl.BlockSpec((tm, n_out), lambda i: (i, 0)))
        args.append(extra)
    if norm_g is not None:
        in_specs.append(pl.BlockSpec((1, n_out), lambda i: (0, 0)))
        args.append(norm_g)
    in_specs += after_specs
    args += after_args
    row_out = pl.BlockSpec((tm, n_out), lambda i: (i, 0))
    if epi == "relu2":
        out_specs = [row_out, row_out]
        out_shape = [jax.ShapeDtypeStruct((M, n_out), BF16), jax.ShapeDtypeStruct((M, n_out), BF16)]
    elif norm_g is not None:
        out_specs = [row_out, row_out]
        out_shape = [jax.ShapeDtypeStruct((M, n_out), F32), jax.ShapeDtypeStruct((M, n_out), BF16)]
    else:
        dt = BF16 if epi == "mask2relu" else (F32 if epi == "res" else out_dtype)
        out_specs = row_out
        out_shape = jax.ShapeDtypeStruct((M, n_out), dt)
    return pl.pallas_call(
        body, name=name, grid=(M // tm,), in_specs=in_specs, out_specs=out_specs, out_shape=out_shape,
        compiler_params=_params(("parallel",)),
    )(*args)


def _whole(w):
    return w.shape, (0,) * w.ndim


def _mm_nn(name, a, w, **kw):
    n_out = w.shape[0] * w.shape[2] if w.ndim == 3 else w.shape[1]
    return _mm_rows(name, [(a, w, *_whole(w))], n_out, nt=False, **kw)


def _mm_nt(name, a, w, **kw):
    return _mm_rows(name, [(a, w, *_whole(w))], w.shape[0], nt=True, **kw)


def _mm_nt_rms_bwd(name, pairs, x, g, dres, *, tm, nc, after=None, shards=False, emit_bf16=True):
    M, D = x.shape
    np_ = len(pairs)
    n_out = 3 if emit_bf16 else 2
    after_args, after_specs = _after_operand(after)

    def body(*refs):
        a_refs = refs[:np_]
        w_refs = refs[np_:2 * np_]
        x_ref, g_ref, r_ref = refs[2 * np_:2 * np_ + 3]
        dy_sc = refs[-1]
        outs = refs[-1 - n_out:-1]
        dx_ref, dg_ref = outs[0], outs[1]
        i = pl.program_id(0)
        a_vals = [_bf(a[...]) for a in a_refs]
        for j in range(D // nc):
            cols, acc = _chunk_product(a_vals, w_refs, j, nc, True, shards)
            dy_sc[:, cols] = acc
        xv = x_ref[...]
        r = lax.rsqrt(jnp.mean(xv * xv, axis=-1, keepdims=True) + EPS)
        xh = xv * r
        dy_v = dy_sc[...]
        dyg = dy_v * g_ref[...]
        c = jnp.mean(dyg * xh, axis=-1, keepdims=True)
        dx = r_ref[...] + r * (dyg - xh * c)
        dx_ref[...] = dx
        if emit_bf16:
            outs[2][...] = _bf(dx)
        part = jnp.sum(dy_v * xh, axis=0, keepdims=True)

        @pl.when(i == 0)
        def _():
            dg_ref[...] = part

        @pl.when(i > 0)
        def _():
            dg_ref[...] += part

    row = pl.BlockSpec((tm, D), lambda i: (i, 0))
    vec = pl.BlockSpec((1, D), lambda i: (0, 0))
    in_specs = [pl.BlockSpec((tm, a.shape[1]), lambda i: (i, 0)) for a, _, _, _ in pairs]
    for _, _, wshape, widx in pairs:
        in_specs.append(pl.BlockSpec(wshape, functools.partial(lambda i, widx: widx, widx=widx)))
    args = [a for a, _, _, _ in pairs] + [w for _, w, _, _ in pairs]
    return pl.pallas_call(
        body, name=name, grid=(M // tm,),
        in_specs=in_specs + [row, vec, row] + after_specs,
        out_specs=[row, vec] + [row] * (n_out - 2),
        out_shape=[jax.ShapeDtypeStruct((M, D), F32), jax.ShapeDtypeStruct((1, D), F32)]
        + [jax.ShapeDtypeStruct((M, D), BF16)] * (n_out - 2),
        scratch_shapes=[pltpu.VMEM((tm, D), F32)],
        compiler_params=_params(("arbitrary",)),
    )(*args, x, g, dres, *after_args)


def _mm_tn(name, a, b, *, t1, tn, tm=2048, slot_cols=None):
    M, K1 = a.shape
    N = b.shape[1]
    nm = M // tm

    def body(a_ref, b_ref, o_ref, acc_ref):
        m = pl.program_id(2)
        t = _dot(_bf(a_ref[...]), _bf(b_ref[...]), TN)

        @pl.when(m == 0)
        def _():
            acc_ref[...] = t

        @pl.when(m > 0)
        def _():
            acc_ref[...] += t

        @pl.when(m == nm - 1)
        def _():
            if slot_cols is None:
                o_ref[...] = _bf(acc_ref[...])
            else:
                for k in range(tn // slot_cols):
                    o_ref[k] = _bf(acc_ref[:, k * slot_cols:(k + 1) * slot_cols])

    if slot_cols is not None:
        out_spec = pl.BlockSpec((tn // slot_cols, t1, slot_cols), lambda i, j, m: (j, i, 0))
        out_shape = jax.ShapeDtypeStruct((N // slot_cols, K1, slot_cols), BF16)
    else:
        out_spec = pl.BlockSpec((t1, tn), lambda i, j, m: (i, j))
        out_shape = jax.ShapeDtypeStruct((K1, N), BF16)
    return pl.pallas_call(
        body, name=name, grid=(K1 // t1, N // tn, nm),
        in_specs=[pl.BlockSpec((tm, t1), lambda i, j, m: (m, i)), pl.BlockSpec((tm, tn), lambda i, j, m: (m, j))],
        out_specs=out_spec, out_shape=out_shape,
        scratch_shapes=[pltpu.VMEM((t1, tn), F32)],
        compiler_params=_params(("parallel", "parallel", "arbitrary")),
    )(a, b)


_INV_SQRT2 = 1.0 / math.sqrt(2.0)
_INV_SQRT2PI = 1.0 / math.sqrt(2.0 * math.pi)


def _gelu(x):
    return 0.5 * x * (1.0 + lax.erf(x * _INV_SQRT2))


def _gelu_and_grad(x):
    cdf = 0.5 * (1.0 + lax.erf(x * _INV_SQRT2))
    return x * cdf, cdf + x * (_INV_SQRT2PI * jnp.exp(-0.5 * x * x))


def _layer_norm_parts(v):
    mu = jnp.mean(v, axis=-1, keepdims=True)
    xc = v - mu
    rs = lax.rsqrt(jnp.mean(xc * xc, axis=-1, keepdims=True) + EPS)
    return xc * rs, rs


def _gate_fwd(name, uvp, ln_g, ln_b, wm, bs_full, tr=512):
    S, W2 = uvp.shape
    W = W2 // 2
    gd = W // GROUPS

    def body(u_ref, v_ref, lg_ref, lb_ref, wm_ref, bs_ref, z_ref):
        vh, _ = _layer_norm_parts(_gelu(v_ref[...]))
        vn = _bf(vh * lg_ref[...] + lb_ref[...])
        for ci in range(tr // CHUNK):
            rows = slice(ci * CHUNK, (ci + 1) * CHUNK)
            for g in range(GROUPS):
                cols = slice(g * gd, (g + 1) * gd)
                mixed = _dot(wm_ref[g], vn[rows, cols], NN) + bs_ref[:, cols]
                z_ref[rows, cols] = _bf(_gelu(u_ref[rows, cols]) * mixed)

    vec = pl.BlockSpec((1, W), lambda i: (0, 0))
    return pl.pallas_call(
        body, name=name, grid=(S // tr,),
        in_specs=[pl.BlockSpec((tr, W), lambda i: (i, 0)), pl.BlockSpec((tr, W), lambda i: (i, 1)), vec, vec,
                  pl.BlockSpec((GROUPS, CHUNK, CHUNK), lambda i: (0, 0, 0)),
                  pl.BlockSpec((CHUNK, W), lambda i: (0, 0))],
        out_specs=pl.BlockSpec((tr, W), lambda i: (i, 0)),
        out_shape=jax.ShapeDtypeStruct((S, W), BF16),
        compiler_params=_params(("parallel",)),
    )(uvp, uvp, ln_g, ln_b, wm, bs_full)


def _gate_bwd(name, uvp, dz, ln_g, ln_b, wm, bs_full, tr=256):
    S, W2 = uvp.shape
    W = W2 // 2
    gd = W // GROUPS
    n_steps = S // tr

    def body(u_ref, v_ref, dz_ref, lg_ref, lb_ref, wm_ref, bs_ref, duv_ref, dwm_ref, dmx_ref, dlg_ref, dlb_ref,
             dvn_ref):
        i = pl.program_id(0)
        v, dv_dvp = _gelu_and_grad(v_ref[...])
        vh, rs = _layer_norm_parts(v)
        lg = lg_ref[...]
        vn = _bf(vh * lg + lb_ref[...])

        @pl.when(i == 0)
        def _():
            dwm_ref[...] = jnp.zeros_like(dwm_ref)
            dmx_ref[...] = jnp.zeros_like(dmx_ref)
            dlg_ref[...] = jnp.zeros_like(dlg_ref)
            dlb_ref[...] = jnp.zeros_like(dlb_ref)

        for ci in range(tr // CHUNK):
            rows = slice(ci * CHUNK, (ci + 1) * CHUNK)
            for g in range(GROUPS):
                cols = slice(g * gd, (g + 1) * gd)
                u, du_dup = _gelu_and_grad(u_ref[rows, cols])
                dz_v = dz_ref[rows, cols]
                dmixed = dz_v * u
                dmx_ref[:, cols] += dmixed
                dmixed_b = _bf(dmixed)
                mixed = _dot(wm_ref[g], vn[rows, cols], NN) + bs_ref[:, cols]
                duv_ref[rows, cols] = _bf(dz_v * mixed * du_dup)
                dwm_ref[g] += _dot(dmixed_b, vn[rows, cols], NT)
                dvn_ref[rows, cols] = _dot(wm_ref[g], dmixed_b, TN)
        dvn = dvn_ref[...]
        dlg_ref[...] += jnp.sum(dvn * vh, axis=0, keepdims=True)
        dlb_ref[...] += jnp.sum(dvn, axis=0, keepdims=True)
        dvh = dvn * lg
        dv = rs * (dvh - jnp.mean(dvh, axis=-1, keepdims=True) - vh * jnp.mean(dvh * vh, axis=-1, keepdims=True))
        duv_ref[:, W:] = _bf(dv * dv_dvp)

        @pl.when(i == n_steps - 1)
        def _():
            t_idx = lax.broadcasted_iota(jnp.int32, (CHUNK, CHUNK), 0)
            s_idx = lax.broadcasted_iota(jnp.int32, (CHUNK, CHUNK), 1)
            keep = (s_idx <= t_idx).astype(F32)
            for g in range(GROUPS):
                dwm_ref[g] = dwm_ref[g] * keep

    vec = pl.BlockSpec((1, W), lambda i: (0, 0))
    row = pl.BlockSpec((tr, W), lambda i: (i, 0))
    return pl.pallas_call(
        body, name=name, grid=(n_steps,),
        in_specs=[row, pl.BlockSpec((tr, W), lambda i: (i, 1)), row, vec, vec,
                  pl.BlockSpec((GROUPS, CHUNK, CHUNK), lambda i: (0, 0, 0)),
                  pl.BlockSpec((CHUNK, W), lambda i: (0, 0))],
        out_specs=[pl.BlockSpec((tr, W2), lambda i: (i, 0)),
                   pl.BlockSpec((GROUPS, CHUNK, CHUNK), lambda i: (0, 0, 0)),
                   pl.BlockSpec((CHUNK, W), lambda i: (0, 0)), vec, vec],
        out_shape=[jax.ShapeDtypeStruct((S, W2), BF16), jax.ShapeDtypeStruct((GROUPS, CHUNK, CHUNK), F32),
                   jax.ShapeDtypeStruct((CHUNK, W), F32), jax.ShapeDtypeStruct((1, W), F32),
                   jax.ShapeDtypeStruct((1, W), F32)],
        scratch_shapes=[pltpu.VMEM((tr, W), F32)],
        compiler_params=_params(("arbitrary",)),
    )(uvp, uvp, dz, ln_g, ln_b, wm, bs_full)


def _t5_bucket(distance):
    small = distance < MAX_EXACT
    nf = jnp.maximum(distance, 1).astype(F32)
    large = MAX_EXACT + (jnp.log(nf / MAX_EXACT) / math.log(REL_MAX_DISTANCE / MAX_EXACT)
                         * (N_BUCKETS - MAX_EXACT)).astype(jnp.int32)
    large = jnp.minimum(large, N_BUCKETS - 1)
    return jnp.where(small, distance, large)


TILE_ELEMS = 2 * CHUNK * CHUNK


def _band_buckets():
    rel = CHUNK + jnp.arange(CHUNK)[None, :] - jnp.arange(2 * CHUNK)[:, None]
    band = (rel >= 0) & (rel <= CHUNK)
    buckets = [_t5_bucket(jnp.clip(rel, 0, CHUNK) * d) for d in DILATIONS]
    return jnp.stack(buckets), band


def _bucket_onehot():
    buckets, _ = _band_buckets()
    return (buckets.reshape(N_DIL, 1, TILE_ELEMS) == jnp.arange(N_BUCKETS)[None, :, None]).astype(F32)


def _bias_tiles(name, rel_bias, after=None):
    _, band = _band_buckets()
    own = band & (jnp.arange(2 * CHUNK) >= CHUNK)[:, None]
    masks = jnp.stack([own, band]).reshape(2, TILE_ELEMS).astype(F32)
    tables = jnp.transpose(rel_bias.reshape(N_BUCKETS, N_DIL, ATT_HEADS), (1, 2, 0))
    after_args, after_specs = _after_operand(after)

    def body(t_ref, oh_ref, m_ref, *rest):
        out_ref = rest[-1]
        for g in range(N_DIL):
            bias = lax.dot_general(t_ref[g], oh_ref[g], (NN, ((), ())), precision=lax.Precision.HIGHEST,
                                   preferred_element_type=F32)
            for f in range(2):
                out_ref[g, f] = jnp.where(m_ref[f:f + 1, :] > 0.5, bias, NEG_INF)

    whole = pl.BlockSpec(memory_space=pltpu.VMEM)
    out = pl.pallas_call(
        body, name=name, out_shape=jax.ShapeDtypeStruct((N_DIL, 2, ATT_HEADS, TILE_ELEMS), F32),
        in_specs=[whole, whole, whole] + after_specs, out_specs=whole,
        compiler_params=_params(),
    )(tables, _bucket_onehot(), masks, *after_args)
    out = out.reshape(N_DIL, 2, ATT_HEADS // 2, 2, 2 * CHUNK, CHUNK)
    return jnp.transpose(out, (0, 1, 2, 4, 3, 5)).reshape(N_DIL, 2, ATT_HEADS // 2, 2 * CHUNK, 2 * CHUNK)


def _att_specs(order):
    def column(part, ids):
        hp, g, _ = order(*ids)
        return part * 3 * 4 + g * 4 + hp

    def window(part):
        def index(*ids):
            c = order(*ids)[2]
            return pl.multiple_of(jnp.maximum(c - 1, 0) * ATT_ROWS, ATT_ROWS), column(part, ids) * LANES
        return pl.BlockSpec((pl.Element(2 * ATT_ROWS), pl.Element(LANES)), index)

    return [pl.BlockSpec((ATT_ROWS, LANES), lambda *ids: (order(*ids)[2], column(0, ids))), window(1), window(2)]


def _window_base(c):
    return jnp.where(c == 0, 0, ATT_ROWS)


def _rows(start, d):
    if d == 1:
        return pl.ds(pl.multiple_of(start, CHUNK), CHUNK)
    return pl.ds(start, CHUNK, stride=d)


def _att_tile_offsets(t, d):
    n = t // d
    r = t % d
    return n * (CHUNK * d) + r, n


def _head_pair_columns(x_t):
    zeros = jnp.zeros((HEAD_DIM, CHUNK), x_t.dtype)
    return jnp.concatenate([jnp.concatenate([x_t[:HEAD_DIM], zeros], axis=0),
                            jnp.concatenate([zeros, x_t[HEAD_DIM:]], axis=0)], axis=1)


def _head_pair_rows(y):
    return jnp.concatenate([y[:HEAD_DIM, :CHUNK], y[HEAD_DIM:, CHUNK:]], axis=0)


def _att_fwd(name, qkv, bias_tiles):
    S = qkv.shape[0]
    n_chunks = S // ATT_ROWS
    tiles = ATT_ROWS // CHUNK

    def body(q_ref, kk, vv, b_ref, out_ref, lse_ref, o_sc, l_sc):
        c = pl.program_id(1)
        g = pl.program_id(2)
        base = _window_base(c)

        for gi, d in enumerate(DILATIONS):
            @pl.when(g == gi)
            def _(gi=gi, d=d):
                span = CHUNK * d

                def tile(t, carry):
                    q0, n = _att_tile_offsets(t, d)
                    first = (c == 0) & (n == 0)
                    rows = _rows(q0, d)
                    cur = _rows(base + q0, d)
                    prev = _rows(jnp.where(first, q0, base + q0 - span), d)
                    inner = jnp.where(first, 0, 1)
                    qq = _head_pair_columns(_bf(q_ref[rows, :] * ATT_SCALE).T)
                    s_p = _dot(_bf(kk[prev, :]), qq, NN) + b_ref[inner, 0:CHUNK, :]
                    s_c = _dot(_bf(kk[cur, :]), qq, NN) + b_ref[inner, CHUNK:2 * CHUNK, :]
                    m = jnp.maximum(jnp.max(s_p, axis=0, keepdims=True), jnp.max(s_c, axis=0, keepdims=True))
                    p_p = jnp.exp(s_p - m)
                    p_c = jnp.exp(s_c - m)
                    l = jnp.sum(p_p, axis=0, keepdims=True) + jnp.sum(p_c, axis=0, keepdims=True)
                    o2 = (_dot(_bf(vv[prev, :]).T, _bf(p_p), NN)
                          + _dot(_bf(vv[cur, :]).T, _bf(p_c), NN)) * (1.0 / l)
                    lse = m + jnp.log(l)
                    l_t = jnp.concatenate([jnp.broadcast_to(lse[:, :CHUNK], (HEAD_DIM, CHUNK)),
                                           jnp.broadcast_to(lse[:, CHUNK:], (HEAD_DIM, CHUNK))], axis=0)
                    o_sc[gi, rows, :] = _head_pair_rows(o2).T
                    l_sc[gi, rows, :] = l_t.T
                    return carry

                lax.fori_loop(0, tiles, tile, 0, unroll=16)

        @pl.when(g == N_DIL - 1)
        def _():
            for rows in (slice(i * 4 * CHUNK, (i + 1) * 4 * CHUNK) for i in range(ATT_ROWS // (4 * CHUNK))):
                ls = [l_sc[gi, rows, :] for gi in range(N_DIL)]
                mx = functools.reduce(jnp.maximum, ls)
                ws = [jnp.exp(l - mx) for l in ls]
                tot = functools.reduce(lambda a, b: a + b, ws)
                acc = ws[0] * o_sc[0, rows, :]
                for gi in range(1, N_DIL):
                    acc = acc + ws[gi] * o_sc[gi, rows, :]
                out_ref[rows, :] = acc / tot
                lse_ref[rows, :] = mx + jnp.log(tot)

    order = lambda hp, c, g: (hp, g, c)
    out_spec = pl.BlockSpec((ATT_ROWS, LANES), lambda hp, c, g: (c, hp))
    shape = jax.ShapeDtypeStruct((S, ATT_WIDTH), F32)
    return pl.pallas_call(
        body, name=name, grid=(ATT_HEADS // 2, n_chunks, N_DIL),
        in_specs=_att_specs(order) + [
            pl.BlockSpec((None, 2, None, 2 * CHUNK, 2 * CHUNK), lambda hp, c, g: (g, 0, hp, 0, 0))],
        out_specs=[out_spec, out_spec],
        out_shape=[shape, shape],
        scratch_shapes=[pltpu.VMEM((N_DIL, ATT_ROWS, LANES), F32), pltpu.VMEM((N_DIL, ATT_ROWS, LANES), F32)],
        compiler_params=_params(("parallel", "parallel", "arbitrary")),
    )(qkv, qkv, qkv, bias_tiles)


def _att_bwd(name, qkv, o, lse, d_o, bias_tiles):
    S = qkv.shape[0]
    n_chunks = S // ATT_ROWS
    tiles = ATT_ROWS // CHUNK

    def body(q_ref, kk, vv, o_ref, l_ref, do_ref, b_ref, dq_out, dk_out, dv_out, ds_ref, dq_ref, dk_ref, dv_ref):
        g = pl.program_id(1)
        c = pl.program_id(2)

        @pl.when(c == 0)
        def _():
            dk_ref[...] = jnp.zeros_like(dk_ref)
            dv_ref[...] = jnp.zeros_like(dv_ref)
            ds_ref[...] = jnp.zeros_like(ds_ref)

        base = _window_base(c)
        first_row = c * ATT_ROWS
        head0 = lax.broadcasted_iota(jnp.int32, (CHUNK, LANES), 1) < HEAD_DIM

        def head_pair_stack(x):
            zero = jnp.zeros_like(x)
            return jnp.concatenate([jnp.where(head0, x, zero), jnp.where(head0, zero, x)], axis=0)

        for gi, d in enumerate(DILATIONS):
            @pl.when(g == gi)
            def _(d=d):
                span = CHUNK * d

                def tile(t, carry):
                    q0, n = _att_tile_offsets(t, d)
                    first = (c == 0) & (n == 0)
                    rows = _rows(q0, d)
                    cur = _rows(base + q0, d)
                    prev = _rows(jnp.where(first, q0, base + q0 - span), d)
                    inner = jnp.where(first, 0, 1)
                    g_cur = _rows(first_row + q0, d)
                    g_prev = _rows(jnp.where(first, q0, first_row + q0 - span), d)
                    q2 = _bf(q_ref[rows, :] * ATT_SCALE)
                    q_t = q2.T
                    k2 = _bf(jnp.concatenate([kk[prev, :], kk[cur, :]], axis=0))
                    k_t = k2.T
                    v2 = _bf(jnp.concatenate([vv[prev, :], vv[cur, :]], axis=0))
                    do2 = do_ref[rows, :]
                    do_b = _bf(do2)
                    do_t = do_b.T
                    lse_t = l_ref[rows, :].T
                    dd_t = (do2 * o_ref[rows, :]).T
                    lse = jnp.concatenate([lse_t[0:1], lse_t[HEAD_DIM:HEAD_DIM + 1]], axis=1)
                    delta = jnp.concatenate([jnp.sum(dd_t[:HEAD_DIM], axis=0, keepdims=True),
                                             jnp.sum(dd_t[HEAD_DIM:], axis=0, keepdims=True)], axis=1)
                    s = _dot(k2, _head_pair_columns(q_t), NN) + b_ref[inner]
                    p = jnp.exp(s - lse)
                    ds = p * (_dot(v2, _head_pair_columns(do_t), NN) - delta)
                    ds_ref[...] += ds
                    ds_b = _bf(ds)
                    dq_t = _head_pair_rows(_dot(k_t, ds_b, NN))
                    dk2 = _dot(ds_b, head_pair_stack(q2), NN)
                    dv2 = _dot(_bf(p), head_pair_stack(do_b), NN)
                    dq_ref[rows, :] = (dq_t * ATT_SCALE).T
                    dk_ref[g_prev, :] += dk2[0:CHUNK]
                    dk_ref[g_cur, :] += dk2[CHUNK:2 * CHUNK]
                    dv_ref[g_prev, :] += dv2[0:CHUNK]
                    dv_ref[g_cur, :] += dv2[CHUNK:2 * CHUNK]
                    return carry

                lax.fori_loop(0, tiles, tile, 0, unroll=16)

        dq_out[...] = _bf(dq_ref[...])

        @pl.when(c == n_chunks - 1)
        def _():
            dk_out[...] = _bf(dk_ref[...])
            dv_out[...] = _bf(dv_ref[...])

    order = lambda hp, g, c: (hp, g, c)
    chunk = pl.BlockSpec((ATT_ROWS, LANES), lambda hp, g, c: (c, hp))
    slab = pl.BlockSpec((S, LANES), lambda hp, g, c: (0, g * 4 + hp))
    width = N_DIL * ATT_WIDTH
    dq, dk, dv, ds_sums = pl.pallas_call(
        body, name=name, grid=(ATT_HEADS // 2, N_DIL, n_chunks),
        in_specs=_att_specs(order) + [chunk, chunk, chunk,
                                      pl.BlockSpec((None, 2, None, 2 * CHUNK, 2 * CHUNK),
                                                   lambda hp, g, c: (g, 0, hp, 0, 0))],
        out_specs=[pl.BlockSpec((ATT_ROWS, LANES), lambda hp, g, c: (c, g * 4 + hp)), slab, slab,
                   pl.BlockSpec((None, None, 2 * CHUNK, 2 * CHUNK), lambda hp, g, c: (g, hp, 0, 0))],
        out_shape=[jax.ShapeDtypeStruct((S, width), BF16), jax.ShapeDtypeStruct((S, width), BF16),
                   jax.ShapeDtypeStruct((S, width), BF16),
                   jax.ShapeDtypeStruct((N_DIL, ATT_HEADS // 2, 2 * CHUNK, 2 * CHUNK), F32)],
        scratch_shapes=[pltpu.VMEM((ATT_ROWS, LANES), F32), pltpu.VMEM((S, LANES), F32),
                        pltpu.VMEM((S, LANES), F32)],
        compiler_params=_params(("parallel", "parallel", "arbitrary")),
    )(qkv, qkv, qkv, o, lse, d_o, bias_tiles)
    ds_sums = ds_sums.reshape(N_DIL, ATT_HEADS // 2, 2 * CHUNK, 2, CHUNK)
    ds_sums = jnp.transpose(ds_sums, (0, 1, 3, 2, 4)).reshape(N_DIL, ATT_HEADS, 2 * CHUNK, CHUNK)
    return dq, dk, dv, ds_sums


def _bias_grad(name, ds_sums):
    flat = ds_sums.reshape(N_DIL, ATT_HEADS, TILE_ELEMS)

    def body(oh_ref, ds_ref, out_ref):
        for g in range(N_DIL):
            out_ref[g] = lax.dot_general(oh_ref[g], ds_ref[g], (NT, ((), ())), precision=lax.Precision.HIGHEST,
                                         preferred_element_type=F32)

    out = pl.pallas_call(
        body, name=name, out_shape=jax.ShapeDtypeStruct((N_DIL, N_BUCKETS, ATT_HEADS), F32),
        compiler_params=_params(),
    )(_bucket_onehot(), flat)
    return jnp.transpose(out, (1, 0, 2)).reshape(N_BUCKETS, N_DIL * ATT_HEADS)


def _peers():
    x, y, c = lax.axis_index("x"), lax.axis_index("y"), lax.axis_index("c")
    me = 4 * x + 2 * y + c
    others = [(x, y, 1 - c), (1 - x, y, c), (x, 1 - y, c), (1 - x, 1 - y, c),
              (1 - x, y, 1 - c), (x, 1 - y, 1 - c), (1 - x, 1 - y, 1 - c)]
    return me, others


def _slot(dev):
    return 4 * dev[0] + 2 * dev[1] + dev[2]


_HBM =pl.BlockSpec(memory_space=pltpu.HBM)
_SEM = pl.BlockSpec(memory_space=pltpu.SEMAPHORE)
_EFFECT = pltpu.SideEffectType.DATAFLOW_SIDE_EFFECTING


def _exchange_copy(src_ref, land_ref, send_sems, recv_sems, k, dev, me, scatter, arriving):
    src = src_ref.at[me if arriving else _slot(dev)] if scatter else src_ref
    dst = land_ref.at[_slot(dev) if arriving else me]
    return pltpu.make_async_remote_copy(src_ref=src, dst_ref=dst, send_sem=send_sems.at[k], recv_sem=recv_sems.at[k],
                                        device_id=dev, device_id_type=MESH)


def _own_copy(src_ref, land_ref, local_sems, p, me, scatter):
    return pltpu.make_async_copy(src_ref.at[me] if scatter else src_ref, land_ref.at[me], local_sems.at[p])


def _exchange_start(name, srcs, scatter):
    n = len(srcs)
    landings = [lax.empty((N_DEV,) + src.shape[-2:], src.dtype) for src in srcs]

    def body(*refs):
        src_refs, land_refs = refs[:n], refs[n:2 * n]
        send_sems, recv_sems, local_sems = refs[2 * n:2 * n + 3]
        token = refs[-1]
        me, others = _peers()
        for p in range(n):
            _own_copy(src_refs[p], land_refs[p], local_sems, p, me, scatter).start()
            for k, dev in enumerate(others):
                _exchange_copy(src_refs[p], land_refs[p], send_sems, recv_sems, p * (N_DEV - 1) + k, dev, me,
                               scatter, False).start()
        token[...] = jnp.zeros_like(token)

    sems = pltpu.SemaphoreType.DMA((n * (N_DEV - 1),))
    hbm = lambda a: pltpu.with_memory_space_constraint(a, pltpu.HBM)
    outs = pl.pallas_call(
        body, name=name,
        out_shape=(sems, sems, pltpu.SemaphoreType.DMA((n,)), *[pltpu.HBM(a.shape, a.dtype) for a in srcs + landings],
                   jax.ShapeDtypeStruct((8, LANES), F32)),
        in_specs=(_HBM,) * (2 * n),
        out_specs=(_SEM, _SEM, _SEM) + (_HBM,) * (2 * n) + (pl.BlockSpec(memory_space=pltpu.VMEM),),
        input_output_aliases={i: 3 + i for i in range(2 * n)},
        compiler_params=pltpu.CompilerParams(has_side_effects=_EFFECT),
    )(*[hbm(a) for a in srcs + landings])
    return (outs[0], outs[1], outs[2], list(outs[3:3 + n]), list(outs[3 + n:3 + 2 * n]), scatter), outs[-1]


def _exchange_wait(name, handle, after):
    send_sems, recv_sems, local_sems, src_thru, land_thru, scatter = handle
    n = len(src_thru)

    def body(*refs):
        src_refs, land_refs = refs[:n], refs[n:2 * n]
        send_sems, recv_sems, local_sems = refs[2 * n:2 * n + 3]
        me, others = _peers()
        for p in range(n):
            _own_copy(src_refs[p], land_refs[p], local_sems, p, me, scatter).wait()
            for k, dev in enumerate(others):
                cp = _exchange_copy(src_refs[p], land_refs[p], send_sems, recv_sems, p * (N_DEV - 1) + k, dev, me,
                                    scatter, True)
                cp.wait_send()
                cp.wait_recv()

    outs = pl.pallas_call(
        body, name=name,
        out_shape=tuple(pltpu.HBM(a.shape, a.dtype) for a in src_thru + land_thru),
        in_specs=(_HBM,) * (2 * n) + (_SEM, _SEM, _SEM, pl.BlockSpec(memory_space=pl.ANY)),
        out_specs=(_HBM,) * (2 * n),
        input_output_aliases={i: i for i in range(2 * n)},
        compiler_params=pltpu.CompilerParams(has_side_effects=_EFFECT),
    )(*src_thru, *land_thru, send_sems, recv_sems, local_sems, after)
    return list(outs[n:])


def _adamw_math(w, g, m, v):
    m = ADAM_B1 * m + (1.0 - ADAM_B1) * g
    v = ADAM_B2 * v + (1.0 - ADAM_B2) * (g * g)
    m_hat = m / (1.0 - ADAM_B1 ** ADAM_STEP)
    v_hat = v / (1.0 - ADAM_B2 ** ADAM_STEP)
    delta = -ADAM_LR * (m_hat / (jnp.sqrt(v_hat) + ADAM_EPS) + ADAM_WD * w)
    return delta, m, v


_SMALL_WIDE = (("mix_norm_g", 2), ("mlp_norm_g", 2), ("final_norm_g", 1), ("a_ln_g", 1), ("a_ln_b", 1))
_SMALL_NARROW = (("a_w_s", GROUPS * CHUNK), ("a_b_s", GROUPS), ("rel_bias", N_BUCKETS))
_SMALL = tuple(n for n, _ in _SMALL_WIDE + _SMALL_NARROW)
_BIAS_COLS = N_DIL * ATT_HEADS


def _pack_small_grads(grads, loss_term):
    D = grads["a_ln_g"].shape[-1]
    tiles = [jnp.pad(grads[n].reshape(k, D), ((0, 8 - k), (0, 0))) for n, k in _SMALL_WIDE]
    tiles.append(jnp.pad(loss_term.reshape(1, 1), ((0, 7), (0, D - 1))))
    narrow = [grads["a_w_s"].reshape(-1, LANES), grads["a_b_s"].reshape(-1, LANES),
              jnp.pad(grads["rel_bias"], ((0, 0), (0, LANES - _BIAS_COLS)))]
    return jnp.concatenate(tiles, axis=0), jnp.concatenate(narrow, axis=0)


def _adamw_small(name, g_wide, g_narrow, w, m, v):
    D = g_wide.shape[-1]
    shapes = {n: (k, D) for n, k in _SMALL_WIDE}
    shapes.update({n: (k, LANES) for n, k in _SMALL_NARROW})
    shapes["rel_bias"] = (N_BUCKETS, _BIAS_COLS)
    n_t = len(_SMALL)

    def body(gw_ref, gn_ref, *rest):
        params = rest[:3 * n_t]
        outs = rest[3 * n_t:3 * n_t + 4 * n_t]
        loss_ref, sw, sn = rest[-3:]
        sw[...] = functools.reduce(lambda a, b: a + b, [gw_ref[j] for j in range(N_DEV)])
        sn[...] = functools.reduce(lambda a, b: a + b, [gn_ref[j] for j in range(N_DEV)])
        row = 0
        for i, n in enumerate(_SMALL):
            k, cols = shapes[n]
            if i < len(_SMALL_WIDE):
                g = sw[8 * i:8 * i + k, :]
            else:
                g = sn[row:row + k, 0:cols]
                row += k
            w_ref, m_ref, v_ref = params[3 * i:3 * i + 3]
            delta, m_new, v_new = _adamw_math(w_ref[...], g, m_ref[...], v_ref[...])
            for out, val in zip(outs[4 * i:4 * i + 4], (g, delta, m_new, v_new)):
                out[...] = val
        loss_ref[...] = sw[8 * len(_SMALL_WIDE):8 * len(_SMALL_WIDE) + 8, 0:LANES]

    whole = pl.BlockSpec(memory_space=pltpu.VMEM)
    args = [t[n].reshape(shapes[n]) for n in _SMALL for t in (w, m, v)]
    res = pl.pallas_call(
        body, name=name,
        in_specs=[whole] * (2 + len(args)), out_specs=[whole] * (4 * n_t + 1),
        out_shape=[jax.ShapeDtypeStruct(shapes[n], F32) for n in _SMALL for _ in range(4)]
        + [jax.ShapeDtypeStruct((8, LANES), F32)],
        scratch_shapes=[pltpu.VMEM(g_wide.shape[1:], F32), pltpu.VMEM(g_narrow.shape[1:], F32)],
        compiler_params=_params(),
    )(g_wide, g_narrow, *args)
    small = {n: tuple(r.reshape(w[n].shape) for r in res[4 * i:4 * i + 4]) for i, n in enumerate(_SMALL)}
    return small, res[-1][0, 0]


def _adamw_shard(name, parts, w, m, v, layer, earlier=None, after=None, tr=256):
    L, K, N = w.shape
    tr = min(tr, K)
    n_prev = 0 if earlier is None else 4
    after_args, after_specs = _after_operand(after)

    def body(p_ref, w_ref, m_ref, v_ref, *rest):
        g_out, d_out, m_out, v_out = rest[n_prev + len(after_args):]
        g = p_ref[0].astype(F32)
        for j in range(1, N_DEV):
            g = g + p_ref[j].astype(F32)
        delta, m_new, v_new = _adamw_math(w_ref[...], g, m_ref[...], v_ref[...])
        g_out[...] = g
        d_out[...] = delta
        m_out[...] = m_new
        v_out[...] = v_new

    row = pl.BlockSpec((None, tr, N), lambda i: (layer, i, 0))
    shape = jax.ShapeDtypeStruct((L, K, N), F32)
    return pl.pallas_call(
        body, name=name, grid=(K // tr,),
        in_specs=[pl.BlockSpec((N_DEV, tr, N), lambda i: (0, i, 0)), row, row, row]
        + [pl.BlockSpec(memory_space=pl.ANY)] * n_prev + after_specs,
        out_specs=[row, row, row, row],
        out_shape=[shape, shape, shape, shape],
        input_output_aliases={4 + j: j for j in range(n_prev)},
        compiler_params=_params(("parallel",)),
    )(parts, w, m, v, *(earlier or ()), *after_args)


def _column_slots(full):
    K, N = full.shape
    return jnp.transpose(full.reshape(K, N_DEV, N // N_DEV), (1, 0, 2))


def _from_column_slots(slots):
    _, K, n = slots.shape
    return jnp.transpose(slots, (1, 0, 2)).reshape(K, N_DEV * n)


_STAGES = (("gate", ("a_w_in", "a_w_out"), 0),
           ("mlp0", ("w_up", "w_down"), 0),
           ("att", ("b_w_qkv", "b_w_out"), 0),
           ("mlp1", ("w_up", "w_down"), 1))


def kernel(x, mix_norm_g, mlp_norm_g, final_norm_g, a_w_in, a_ln_g, a_ln_b, a_w_s, a_b_s, a_w_out, b_w_qkv, b_w_out, rel_bias, w_up, w_down, loss_target, m_mix_norm_g, m_mlp_norm_g, m_final_norm_g, m_a_w_in, m_a_ln_g, m_a_ln_b, m_a_w_s, m_a_b_s, m_a_w_out, m_b_w_qkv, m_b_w_out, m_rel_bias, m_w_up, m_w_down, v_mix_norm_g, v_mlp_norm_g, v_final_norm_g, v_a_w_in, v_a_ln_g, v_a_ln_b, v_a_w_s, v_a_b_s, v_a_w_out, v_b_w_qkv, v_b_w_out, v_rel_bias, v_w_up, v_w_down):
    w = dict(mix_norm_g=mix_norm_g, mlp_norm_g=mlp_norm_g, final_norm_g=final_norm_g, a_w_in=a_w_in, a_ln_g=a_ln_g,
             a_ln_b=a_ln_b, a_w_s=a_w_s, a_b_s=a_b_s, a_w_out=a_w_out, b_w_qkv=b_w_qkv, b_w_out=b_w_out,
             rel_bias=rel_bias, w_up=w_up, w_down=w_down)
    m = dict(mix_norm_g=m_mix_norm_g, mlp_norm_g=m_mlp_norm_g, final_norm_g=m_final_norm_g, a_w_in=m_a_w_in,
             a_ln_g=m_a_ln_g, a_ln_b=m_a_ln_b, a_w_s=m_a_w_s, a_b_s=m_a_b_s, a_w_out=m_a_w_out, b_w_qkv=m_b_w_qkv,
             b_w_out=m_b_w_out, rel_bias=m_rel_bias, w_up=m_w_up, w_down=m_w_down)
    v = dict(mix_norm_g=v_mix_norm_g, mlp_norm_g=v_mlp_norm_g, final_norm_g=v_final_norm_g, a_w_in=v_a_w_in,
             a_ln_g=v_a_ln_g, a_ln_b=v_a_ln_b, a_w_s=v_a_w_s, a_b_s=v_a_b_s, a_w_out=v_a_w_out, b_w_qkv=v_b_w_qkv,
             b_w_out=v_b_w_out, rel_bias=v_rel_bias, w_up=v_w_up, w_down=v_w_down)

    stages = {s: (names, layer) for s, names, layer in _STAGES}
    order = [s for s, _, _ in _STAGES]

    def shards_of(stage):
        names, layer = stages[stage]
        return [_bf(w[n][layer]) for n in names]

    pending = {}
    pending[order[0]], first_token = _exchange_start("gather_" + order[0] + "_start", shards_of(order[0]), False)

    def get_weights(stage, dep):
        gathered = _exchange_wait("gather_" + stage + "_wait", pending.pop(stage), dep)
        nxt = order.index(stage) + 1
        token = None
        if nxt < len(order):
            shards, gathered = lax.optimization_barrier((shards_of(order[nxt]), gathered))
            pending[order[nxt]], token = _exchange_start("gather_" + order[nxt] + "_start", shards, False)
        return gathered, token

    sent = {}

    def put_grads(stage, slot_grads):
        sent[stage], token = _exchange_start("scatter_" + stage + "_start", slot_grads, True)
        return token

    loss_local, grad_x, small_g = _local_step(
        x[0], loss_target[0], mix_norm_g, mlp_norm_g, final_norm_g, a_ln_g, a_ln_b, a_w_s, a_b_s, rel_bias,
        get_weights, put_grads, first_token)

    small_sent, token = _exchange_start("gather_small_start", list(_pack_small_grads(small_g, loss_local)), False)

    results = {}
    prev = token
    for stage in reversed(order):
        names, layer = stages[stage]
        received = _exchange_wait("scatter_" + stage + "_wait", sent[stage], prev)
        for n, parts in zip(names, received):
            results[n] = _adamw_shard("adamw_%s_%s" % (stage, n), parts, w[n], m[n], v[n], layer, results.get(n),
                                      after=prev)
            prev = results[n][0]

    g_wide, g_narrow = _exchange_wait("gather_small_wait", small_sent, prev)
    small, loss = _adamw_small("adamw_small", g_wide, g_narrow, w, m, v)

    outs = []
    for j in range(4):
        outs.extend(small[n][j] if n in _SMALL else results[n][j] for n in w)
    return (loss, grad_x[None], *outs)


def _local_step(xs, tgt, mix_norm_g, mlp_norm_g, final_norm_g, a_ln_g, a_ln_b, a_w_s, a_b_s, rel_bias,
                get_weights, put_grads, first_token=None):
    D = xs.shape[-1]
    g_mix = [mix_norm_g[l][None, :] for l in range(2)]
    g_mlp = [mlp_norm_g[l][None, :] for l in range(2)]
    g_fin = final_norm_g[None, :]
    ln_g, ln_b = a_ln_g, a_ln_b
    causal = jnp.tril(jnp.ones((CHUNK, CHUNK), dtype=bool))
    wm = _bf(jnp.where(causal[None], a_w_s[0], 0.0))
    bs_full = jnp.repeat(a_b_s[0].T, D // GROUPS, axis=1)
    bias_tiles = _bias_tiles("att_bias", rel_bias, after=first_token)

    (win, wout), token = get_weights("gate", bias_tiles)
    wout = wout.reshape(-1, D)
    y0 = _rms_fwd("rms_mix0", xs, g_mix[0], after=token)
    uvp = _mm_nn("gate_in", y0, win, tm=512, nc=win.shape[2], shards=True)
    z = _gate_fwd("gate_mid", uvp, ln_g, ln_b, wm, bs_full)
    h1, y1 = _mm_nn("gate_out", z, wout, tm=512, nc=512, epi="res", extra=xs, norm_g=g_mlp[0])
    (wup0, wdn0), token = get_weights("mlp0", h1)
    wdn0 = wdn0.reshape(-1, D)
    a0, f0 = _mm_nn("mlp0_up", y1, wup0, tm=512, nc=wup0.shape[2], epi="relu2", shards=True, after=token)
    h2, y2 = _mm_nn("mlp0_down", f0, wdn0, tm=512, nc=512, epi="res", extra=h1, norm_g=g_mix[1])
    (wqkv, wo), token = get_weights("att", h2)
    wqkv, wo = _from_column_slots(wqkv), _from_column_slots(wo)
    qkv = _mm_nn("att_qkv", y2, wqkv, tm=512, nc=512, after=token)
    o_att, lse = _att_fwd("att_fwd", qkv, bias_tiles)
    h3, y3 = _mm_nn("att_out", o_att, wo, tm=512, nc=512, epi="res", extra=h2, norm_g=g_mlp[1])
    (wup1, wdn1), _ = get_weights("mlp1", h3)
    wdn1 = wdn1.reshape(-1, D)
    a1, f1 = _mm_nn("mlp1_up", y3, wup1, tm=512, nc=wup1.shape[2], epi="relu2", shards=True)
    dh, dg_fin, err2, dh_b = _mm_res_loss("mlp1_down_loss", f1, wdn1, h3, g_fin, tgt, tm=512, nc=512)
    loss_local = 0.5 * jnp.sum(err2) / D

    def mlp_bwd(tag, dh, dh_b, h_in, y, a, f, wup_l, wdn_l, g_row, after):
        da = _mm_nt(tag + "_dact", dh_b, wdn_l, tm=512, nc=512, epi="mask2relu", extra=a)
        g_dn = _mm_tn(tag + "_dwdown", f, dh_b, t1=1024, tn=1024, tm=DW_TOKENS)
        g_up = _mm_tn(tag + "_dwup", y, da, t1=1024, tn=1024, tm=DW_TOKENS, slot_cols=wup_l.shape[2])
        dh_in, dg, dh_in_b = _mm_nt_rms_bwd(tag + "_dy", [(da, wup_l, *_whole(wup_l))], h_in, g_row, dh, tm=512,
                                            nc=512, shards=True, after=after)
        return dh_in, dh_in_b, dg, put_grads(tag, [g_up, g_dn.reshape(N_DEV, -1, D)])

    dh3, dh3_b, dg_mlp1, token = mlp_bwd("mlp1", dh, dh_b, h3, y3, a1, f1, wup1, wdn1, g_mlp[1], None)

    d_o = _mm_nt("att_dout", dh3_b, wo, tm=512, nc=512, after=token)
    g_wo = _mm_tn("att_dwo", o_att, dh3_b, t1=512, tn=1024, tm=DW_TOKENS)
    dq, dk, dv, ds_sums = _att_bwd("att_bwd", qkv, o_att, lse, d_o, bias_tiles)
    part_w = N_DIL * ATT_WIDTH
    g_qkv = [_mm_tn("att_dwqkv%d" % p, y2, t, t1=1024, tn=part_w) for p, t in enumerate((dq, dk, dv))]
    dh2, dg_mix1, dh2_b = _mm_nt_rms_bwd(
        "att_dy", [(t, wqkv, (D, part_w), (0, p)) for p, t in enumerate((dq, dk, dv))], h2, g_mix[1], dh3, tm=512,
        nc=512)
    token = put_grads("att", [_column_slots(jnp.concatenate(g_qkv, axis=1)), _column_slots(g_wo)])

    dh1, dh1_b, dg_mlp0, token = mlp_bwd("mlp0", dh2, dh2_b, h1, y1, a0, f0, wup0, wdn0, g_mlp[0], token)

    dz = _mm_nt("gate_dz", dh1_b, wout, tm=512, nc=512, after=token)
    g_wout = _mm_tn("gate_dwout", z, dh1_b, t1=1024, tn=1024, tm=DW_TOKENS)
    duvp, d_wm, d_mixed, d_lng, d_lnb = _gate_bwd("gate_dmid", uvp, dz, ln_g, ln_b, wm, bs_full)
    g_win = _mm_tn("gate_dwin", y0, duvp, t1=1024, tn=1024, tm=DW_TOKENS, slot_cols=win.shape[2])
    token = put_grads("gate", [g_win, g_wout.reshape(N_DEV, -1, D)])
    grad_x, dg_mix0 = _mm_nt_rms_bwd("gate_dy", [(duvp, win, *_whole(win))], xs, g_mix[0], dh1, tm=512, nc=512,
                                     after=token, shards=True, emit_bf16=False)

    small_g = dict(
        mix_norm_g=jnp.concatenate([dg_mix0, dg_mix1], axis=0),
        mlp_norm_g=jnp.concatenate([dg_mlp0, dg_mlp1], axis=0),
        final_norm_g=dg_fin[0], a_ln_g=d_lng, a_ln_b=d_lnb, a_w_s=d_wm[None],
        a_b_s=jnp.sum(d_mixed.reshape(CHUNK, GROUPS, D // GROUPS), axis=2).T[None],
        rel_bias=_bias_grad("att_dbias", ds_sums))
    return loss_local, grad_x, small_g
```

```python
import functools
import math

import jax
import jax.numpy as jnp
from jax import lax
from jax.experimental import pallas as pl
from jax.experimental.pallas import tpu as pltpu

F32 = jnp.float32
BF16 = jnp.bfloat16
MESH = pl.DeviceIdType.MESH

N_DEV = 8
EPS = 1e-6
NEG_INF = -1e30
CHUNK = 128
GROUPS = 8
HEAD_DIM = 64
ATT_HEADS = 8
ATT_WIDTH = ATT_HEADS * HEAD_DIM
DILATIONS = (1, 4, 16)
N_DIL = len(DILATIONS)
N_BUCKETS = 32
MAX_EXACT = N_BUCKETS // 2
REL_MAX_DISTANCE = 2048
ATT_ROWS = 2048
ATT_SCALE = HEAD_DIM ** -0.5
DW_TOKENS = 4096
LANES = 128

ADAM_LR = 0.001
ADAM_B1 = 0.9
ADAM_B2 = 0.999
ADAM_EPS = 1e-08
ADAM_WD = 0.01
ADAM_STEP = 10

VMEM_LIMIT_BYTES = 56 * 1024 * 1024


def _params(semantics=None):
    return pltpu.CompilerParams(dimension_semantics=semantics, vmem_limit_bytes=VMEM_LIMIT_BYTES)


def _bf(v):
    return v.astype(BF16)


def _dot(a, b, dims):
    return lax.dot_general(a, b, (dims, ((), ())), preferred_element_type=F32)


NN = ((1,), (0,))
NT = ((1,), (1,))
TN = ((0,), (0,))


def _after_operand(after):
    if after is None:
        return [], []
    return [after], [pl.BlockSpec(memory_space=pl.ANY)]


def _rms_fwd(name, x, g, tm=512, after=None):
    S, D = x.shape
    after_args, after_specs = _after_operand(after)

    def body(x_ref, g_ref, *rest):
        y_ref = rest[-1]
        xv = x_ref[...]
        r = lax.rsqrt(jnp.mean(xv * xv, axis=-1, keepdims=True) + EPS)
        y_ref[...] = _bf(xv * r * g_ref[...])

    return pl.pallas_call(
        body, name=name, grid=(S // tm,),
        in_specs=[pl.BlockSpec((tm, D), lambda i: (i, 0)), pl.BlockSpec((1, D), lambda i: (0, 0))] + after_specs,
        out_specs=pl.BlockSpec((tm, D), lambda i: (i, 0)),
        out_shape=jax.ShapeDtypeStruct((S, D), BF16),
        compiler_params=_params(("parallel",)),
    )(x, g, *after_args)


def _mm_res_loss(name, a, w, res, g, target, *, tm, nc):
    M, D = res.shape

    def body(a_ref, w_ref, r_ref, g_ref, t_ref, dh_ref, dg_ref, l_ref, dhb_ref, h_sc):
        i = pl.program_id(0)
        a_v = _bf(a_ref[...])
        for j in range(D // nc):
            cols, acc = _chunk_product([a_v], [w_ref], j, nc, False, False)
            h_sc[:, cols] = r_ref[:, cols] + acc
        rb = 16
        gv = g_ref[...]

        def fold(v):
            return jnp.sum(v.reshape(rb // 8, 8, D), axis=0)

        def block(b, carry):
            dg_acc, l_acc = carry
            rows = pl.ds(pl.multiple_of(b * rb, rb), rb)
            xv = h_sc[rows, :]
            r = lax.rsqrt(jnp.mean(xv * xv, axis=-1, keepdims=True) + EPS)
            xh = xv * r
            e = xh * gv - t_ref[rows, :]
            dout = e / D
            dyg = dout * gv
            c = jnp.mean(dyg * xh, axis=-1, keepdims=True)
            dh = r * (dyg - xh * c)
            dh_ref[rows, :] = dh
            dhb_ref[rows, :] = _bf(dh)
            return dg_acc + fold(dout * xh), l_acc + fold(e * e)

        zero = jnp.zeros((8, D), F32)
        dg8, l8 = lax.fori_loop(0, tm // rb, block, (zero, zero), unroll=2)
        dg_part = jnp.sum(dg8, axis=0, keepdims=True)
        l_part = jnp.sum(l8, axis=0, keepdims=True)

        @pl.when(i == 0)
        def _():
            dg_ref[...] = dg_part
            l_ref[...] = l_part

        @pl.when(i > 0)
        def _():
            dg_ref[...] += dg_part
            l_ref[...] += l_part

    row = pl.BlockSpec((tm, D), lambda i: (i, 0))
    vec = pl.BlockSpec((1, D), lambda i: (0, 0))
    return pl.pallas_call(
        body, name=name, grid=(M // tm,),
        in_specs=[pl.BlockSpec((tm, a.shape[1]), lambda i: (i, 0)), pl.BlockSpec(w.shape, lambda i: (0, 0)),
                  row, vec, row],
        out_specs=[row, vec, vec, row],
        out_shape=[jax.ShapeDtypeStruct((M, D), F32), jax.ShapeDtypeStruct((1, D), F32),
                   jax.ShapeDtypeStruct((1, D), F32), jax.ShapeDtypeStruct((M, D), BF16)],
        scratch_shapes=[pltpu.VMEM((tm, D), F32)],
        compiler_params=_params(("arbitrary",)),
    )(a, w, res, g, target)


def _chunk_product(a_vals, w_refs, j, nc, nt, shards):
    cols = slice(j * nc, (j + 1) * nc)
    acc = None
    for a_v, w_ref in zip(a_vals, w_refs):
        if not shards:
            terms = [_dot(a_v, w_ref[cols, :], NT) if nt else _dot(a_v, w_ref[:, cols], NN)]
        elif nt:
            nl = w_ref.shape[2]
            terms = [_dot(a_v[:, k * nl:(k + 1) * nl], w_ref[k, cols, :], NT) for k in range(N_DEV)]
        else:
            terms = [_dot(a_v, w_ref[j], NN)]
        for t in terms:
            acc = t if acc is None else acc + t
    return cols, acc


def _mm_rows(name, pairs, n_out, *, nt, tm, nc, epi="plain", extra=None, out_dtype=F32, after=None, shards=False,
             norm_g=None):
    M = pairs[0][0].shape[0]
    np_ = len(pairs)
    after_args, after_specs = _after_operand(after)

    def body(*refs):
        a_refs = refs[:np_]
        w_refs = refs[np_:2 * np_]
        pos = 2 * np_
        e_ref = None
        if extra is not None:
            e_ref = refs[pos]
            pos += 1
        if norm_g is not None:
            g_ref = refs[pos]
            pos += 1
        pos += len(after_args)
        outs = refs[pos:]
        a_vals = [_bf(a[...]) for a in a_refs]
        for j in range(n_out // nc):
            cols, acc = _chunk_product(a_vals, w_refs, j, nc, nt, shards)
            if epi == "plain":
                outs[0][:, cols] = acc.astype(out_dtype)
            elif epi == "res":
                outs[0][:, cols] = e_ref[:, cols] + acc
            elif epi == "relu2":
                outs[0][:, cols] = _bf(acc)
                rl = jnp.maximum(acc, 0.0)
                outs[1][:, cols] = _bf(rl * rl)
            elif epi == "mask2relu":
                outs[0][:, cols] = _bf(acc * (2.0 * jnp.maximum(e_ref[:, cols].astype(F32), 0.0)))
        if norm_g is not None:
            hv = outs[0][...]
            r = lax.rsqrt(jnp.mean(hv * hv, axis=-1, keepdims=True) + EPS)
            outs[1][...] = _bf(hv * r * g_ref[...])

    in_specs = [pl.BlockSpec((tm, a.shape[1]), lambda i: (i, 0)) for a, _, _, _ in pairs]
    for _, _, wshape, widx in pairs:
        in_specs.append(pl.BlockSpec(wshape, functools.partial(lambda i, widx: widx, widx=widx)))
    args = [a for a, _, _, _ in pairs] + [w for _, w, _, _ in pairs]
    if extra is not None:
        in_specs.append(pl.BlockSpec((tm, n_out), lambda i: (i, 0)))
        args.append(extra)
    if norm_g is not None:
        in_specs.append(pl.BlockSpec((1, n_out), lambda i: (0, 0)))
        args.append(norm_g)
    in_specs += after_specs
    args += after_args
    row_out = pl.BlockSpec((tm, n_out), lambda i: (i, 0))
    if epi == "relu2":
        out_specs = [row_out, row_out]
        out_shape = [jax.ShapeDtypeStruct((M, n_out), BF16), jax.ShapeDtypeStruct((M, n_out), BF16)]
    elif norm_g is not None:
        out_specs = [row_out, row_out]
        out_shape = [jax.ShapeDtypeStruct((M, n_out), F32), jax.ShapeDtypeStruct((M, n_out), BF16)]
    else:
        dt = BF16 if epi == "mask2relu" else (F32 if epi == "res" else out_dtype)
        out_specs = row_out
        out_shape = jax.ShapeDtypeStruct((M, n_out), dt)
    return pl.pallas_call(
        body, name=name, grid=(M // tm,), in_specs=in_specs, out_specs=out_specs, out_shape=out_shape,
        compiler_params=_params(("parallel",)),
    )(*args)


def _whole(w):
    return w.shape, (0,) * w.ndim


def _mm_nn(name, a, w, **kw):
    n_out = w.shape[0] * w.shape[2] if w.ndim == 3 else w.shape[1]
    return _mm_rows(name, [(a, w, *_whole(w))], n_out, nt=False, **kw)


def _mm_nt(name, a, w, **kw):
    return _mm_rows(name, [(a, w, *_whole(w))], w.shape[0], nt=True, **kw)


def _mm_nt_rms_bwd(name, pairs, x, g, dres, *, tm, nc, after=None, shards=False, emit_bf16=True):
    M, D = x.shape
    np_ = len(pairs)
    n_out = 3 if emit_bf16 else 2
    after_args, after_specs = _after_operand(after)

    def body(*refs):
        a_refs = refs[:np_]
        w_refs = refs[np_:2 * np_]
        x_ref, g_ref, r_ref = refs[2 * np_:2 * np_ + 3]
        dy_sc = refs[-1]
        outs = refs[-1 - n_out:-1]
        dx_ref, dg_ref = outs[0], outs[1]
        i = pl.program_id(0)
        a_vals = [_bf(a[...]) for a in a_refs]
        for j in range(D // nc):
            cols, acc = _chunk_product(a_vals, w_refs, j, nc, True, shards)
            dy_sc[:, cols] = acc
        xv = x_ref[...]
        r = lax.rsqrt(jnp.mean(xv * xv, axis=-1, keepdims=True) + EPS)
        xh = xv * r
        dy_v = dy_sc[...]
        dyg = dy_v * g_ref[...]
        c = jnp.mean(dyg * xh, axis=-1, keepdims=True)
        dx = r_ref[...] + r * (dyg - xh * c)
        dx_ref[...] = dx
        if emit_bf16:
            outs[2][...] = _bf(dx)
        part = jnp.sum(dy_v * xh, axis=0, keepdims=True)

        @pl.when(i == 0)
        def _():
            dg_ref[...] = part

        @pl.when(i > 0)
        def _():
            dg_ref[...] += part

    row = pl.BlockSpec((tm, D), lambda i: (i, 0))
    vec = pl.BlockSpec((1, D), lambda i: (0, 0))
    in_specs = [pl.BlockSpec((tm, a.shape[1]), lambda i: (i, 0)) for a, _, _, _ in pairs]
    for _, _, wshape, widx in pairs:
        in_specs.append(pl.BlockSpec(wshape, functools.partial(lambda i, widx: widx, widx=widx)))
    args = [a for a, _, _, _ in pairs] + [w for _, w, _, _ in pairs]
    return pl.pallas_call(
        body, name=name, grid=(M // tm,),
        in_specs=in_specs + [row, vec, row] + after_specs,
        out_specs=[row, vec] + [row] * (n_out - 2),
        out_shape=[jax.ShapeDtypeStruct((M, D), F32), jax.ShapeDtypeStruct((1, D), F32)]
        + [jax.ShapeDtypeStruct((M, D), BF16)] * (n_out - 2),
        scratch_shapes=[pltpu.VMEM((tm, D), F32)],
        compiler_params=_params(("arbitrary",)),
    )(*args, x, g, dres, *after_args)


def _mm_tn(name, a, b, *, t1, tn, tm=2048, slot_cols=None):
    M, K1 = a.shape
    N = b.shape[1]
    nm = M // tm

    def body(a_ref, b_ref, o_ref, acc_ref):
        m = pl.program_id(2)
        t = _dot(_bf(a_ref[...]), _bf(b_ref[...]), TN)

        @pl.when(m == 0)
        def _():
            acc_ref[...] = t

        @pl.when(m > 0)
        def _():
            acc_ref[...] += t

        @pl.when(m == nm - 1)
        def _():
            if slot_cols is None:
                o_ref[...] = _bf(acc_ref[...])
            else:
                for k in range(tn // slot_cols):
                    o_ref[k] = _bf(acc_ref[:, k * slot_cols:(k + 1) * slot_cols])

    if slot_cols is not None:
        out_spec = pl.BlockSpec((tn // slot_cols, t1, slot_cols), lambda i, j, m: (j, i, 0))
        out_shape = jax.ShapeDtypeStruct((N // slot_cols, K1, slot_cols), BF16)
    else:
        out_spec = pl.BlockSpec((t1, tn), lambda i, j, m: (i, j))
        out_shape = jax.ShapeDtypeStruct((K1, N), BF16)
    return pl.pallas_call(
        body, name=name, grid=(K1 // t1, N // tn, nm),
        in_specs=[pl.BlockSpec((tm, t1), lambda i, j, m: (m, i)), pl.BlockSpec((tm, tn), lambda i, j, m: (m, j))],
        out_specs=out_spec, out_shape=out_shape,
        scratch_shapes=[pltpu.VMEM((t1, tn), F32)],
        compiler_params=_params(("parallel", "parallel", "arbitrary")),
    )(a, b)


_INV_SQRT2 = 1.0 / math.sqrt(2.0)
_INV_SQRT2PI = 1.0 / math.sqrt(2.0 * math.pi)


def _gelu(x):
    return 0.5 * x * (1.0 + lax.erf(x * _INV_SQRT2))


def _gelu_and_grad(x):
    cdf = 0.5 * (1.0 + lax.erf(x * _INV_SQRT2))
    return x * cdf, cdf + x * (_INV_SQRT2PI * jnp.exp(-0.5 * x * x))


def _layer_norm_parts(v):
    mu = jnp.mean(v, axis=-1, keepdims=True)
    xc = v - mu
    rs = lax.rsqrt(jnp.mean(xc * xc, axis=-1, keepdims=True) + EPS)
    return xc * rs, rs


def _gate_fwd(name, uvp, ln_g, ln_b, wm, bs_full, tr=512):
    S, W2 = uvp.shape
    W = W2 // 2
    gd = W // GROUPS

    def body(u_ref, v_ref, lg_ref, lb_ref, wm_ref, bs_ref, z_ref):
        vh, _ = _layer_norm_parts(_gelu(v_ref[...]))
        vn = _bf(vh * lg_ref[...] + lb_ref[...])
        for ci in range(tr // CHUNK):
            rows = slice(ci * CHUNK, (ci + 1) * CHUNK)
            for g in range(GROUPS):
                cols = slice(g * gd, (g + 1) * gd)
                mixed = _dot(wm_ref[g], vn[rows, cols], NN) + bs_ref[:, cols]
                z_ref[rows, cols] = _bf(_gelu(u_ref[rows, cols]) * mixed)

    vec = pl.BlockSpec((1, W), lambda i: (0, 0))
    return pl.pallas_call(
        body, name=name, grid=(S // tr,),
        in_specs=[pl.BlockSpec((tr, W), lambda i: (i, 0)), pl.BlockSpec((tr, W), lambda i: (i, 1)), vec, vec,
                  pl.BlockSpec((GROUPS, CHUNK, CHUNK), lambda i: (0, 0, 0)),
                  pl.BlockSpec((CHUNK, W), lambda i: (0, 0))],
        out_specs=pl.BlockSpec((tr, W), lambda i: (i, 0)),
        out_shape=jax.ShapeDtypeStruct((S, W), BF16),
        compiler_params=_params(("parallel",)),
    )(uvp, uvp, ln_g, ln_b, wm, bs_full)


def _gate_bwd(name, uvp, dz, ln_g, ln_b, wm, bs_full, tr=256):
    S, W2 = uvp.shape
    W = W2 // 2
    gd = W // GROUPS
    n_steps = S // tr

    def body(u_ref, v_ref, dz_ref, lg_ref, lb_ref, wm_ref, bs_ref, duv_ref, dwm_ref, dmx_ref, dlg_ref, dlb_ref,
             dvn_ref):
        i = pl.program_id(0)
        v, dv_dvp = _gelu_and_grad(v_ref[...])
        vh, rs = _layer_norm_parts(v)
        lg = lg_ref[...]
        vn = _bf(vh * lg + lb_ref[...])

        @pl.when(i == 0)
        def _():
            dwm_ref[...] = jnp.zeros_like(dwm_ref)
            dmx_ref[...] = jnp.zeros_like(dmx_ref)
            dlg_ref[...] = jnp.zeros_like(dlg_ref)
            dlb_ref[...] = jnp.zeros_like(dlb_ref)

        for ci in range(tr // CHUNK):
            rows = slice(ci * CHUNK, (ci + 1) * CHUNK)
            for g in range(GROUPS):
                cols = slice(g * gd, (g + 1) * gd)
                u, du_dup = _gelu_and_grad(u_ref[rows, cols])
                dz_v = dz_ref[rows, cols]
                dmixed = dz_v * u
                dmx_ref[:, cols] += dmixed
                dmixed_b = _bf(dmixed)
                mixed = _dot(wm_ref[g], vn[rows, cols], NN) + bs_ref[:, cols]
                duv_ref[rows, cols] = _bf(dz_v * mixed * du_dup)
                dwm_ref[g] += _dot(dmixed_b, vn[rows, cols], NT)
                dvn_ref[rows, cols] = _dot(wm_ref[g], dmixed_b, TN)
        dvn = dvn_ref[...]
        dlg_ref[...] += jnp.sum(dvn * vh, axis=0, keepdims=True)
        dlb_ref[...] += jnp.sum(dvn, axis=0, keepdims=True)
        dvh = dvn * lg
        dv = rs * (dvh - jnp.mean(dvh, axis=-1, keepdims=True) - vh * jnp.mean(dvh * vh, axis=-1, keepdims=True))
        duv_ref[:, W:] = _bf(dv * dv_dvp)

        @pl.when(i == n_steps - 1)
        def _():
            t_idx = lax.broadcasted_iota(jnp.int32, (CHUNK, CHUNK), 0)
            s_idx = lax.broadcasted_iota(jnp.int32, (CHUNK, CHUNK), 1)
            keep = (s_idx <= t_idx).astype(F32)
            for g in range(GROUPS):
                dwm_ref[g] = dwm_ref[g] * keep

    vec = pl.BlockSpec((1, W), lambda i: (0, 0))
    row = pl.BlockSpec((tr, W), lambda i: (i, 0))
    return pl.pallas_call(
        body, name=name, grid=(n_steps,),
        in_specs=[row, pl.BlockSpec((tr, W), lambda i: (i, 1)), row, vec, vec,
                  pl.BlockSpec((GROUPS, CHUNK, CHUNK), lambda i: (0, 0, 0)),
                  pl.BlockSpec((CHUNK, W), lambda i: (0, 0))],
        out_specs=[pl.BlockSpec((tr, W2), lambda i: (i, 0)),
                   pl.BlockSpec((GROUPS, CHUNK, CHUNK), lambda i: (0, 0, 0)),
                   pl.BlockSpec((CHUNK, W), lambda i: (0, 0)), vec, vec],
        out_shape=[jax.ShapeDtypeStruct((S, W2), BF16), jax.ShapeDtypeStruct((GROUPS, CHUNK, CHUNK), F32),
                   jax.ShapeDtypeStruct((CHUNK, W), F32), jax.ShapeDtypeStruct((1, W), F32),
                   jax.ShapeDtypeStruct((1, W), F32)],
        scratch_shapes=[pltpu.VMEM((tr, W), F32)],
        compiler_params=_params(("arbitrary",)),
    )(uvp, uvp, dz, ln_g, ln_b, wm, bs_full)


def _t5_bucket(distance):
    small = distance < MAX_EXACT
    nf = jnp.maximum(distance, 1).astype(F32)
    large = MAX_EXACT + (jnp.log(nf / MAX_EXACT) / math.log(REL_MAX_DISTANCE / MAX_EXACT)
                         * (N_BUCKETS - MAX_EXACT)).astype(jnp.int32)
    large = jnp.minimum(large, N_BUCKETS - 1)
    return jnp.where(small, distance, large)


TILE_ELEMS = 2 * CHUNK * CHUNK


def _band_buckets():
    rel = CHUNK + jnp.arange(CHUNK)[None, :] - jnp.arange(2 * CHUNK)[:, None]
    band = (rel >= 0) & (rel <= CHUNK)
    buckets = [_t5_bucket(jnp.clip(rel, 0, CHUNK) * d) for d in DILATIONS]
    return jnp.stack(buckets), band


def _bucket_onehot():
    buckets, _ = _band_buckets()
    return (buckets.reshape(N_DIL, 1, TILE_ELEMS) == jnp.arange(N_BUCKETS)[None, :, None]).astype(F32)


def _bias_tiles(name, rel_bias, after=None):
    _, band = _band_buckets()
    own = band & (jnp.arange(2 * CHUNK) >= CHUNK)[:, None]
    masks = jnp.stack([own, band]).reshape(2, TILE_ELEMS).astype(F32)
    tables = jnp.transpose(rel_bias.reshape(N_BUCKETS, N_DIL, ATT_HEADS), (1, 2, 0))
    after_args, after_specs = _after_operand(after)

    def body(t_ref, oh_ref, m_ref, *rest):
        out_ref = rest[-1]
        for g in range(N_DIL):
            bias = lax.dot_general(t_ref[g], oh_ref[g], (NN, ((), ())), precision=lax.Precision.HIGHEST,
                                   preferred_element_type=F32)
            for f in range(2):
                out_ref[g, f] = jnp.where(m_ref[f:f + 1, :] > 0.5, bias, NEG_INF)

    whole = pl.BlockSpec(memory_space=pltpu.VMEM)
    out = pl.pallas_call(
        body, name=name, out_shape=jax.ShapeDtypeStruct((N_DIL, 2, ATT_HEADS, TILE_ELEMS), F32),
        in_specs=[whole, whole, whole] + after_specs, out_specs=whole,
        compiler_params=_params(),
    )(tables, _bucket_onehot(), masks, *after_args)
    out = out.reshape(N_DIL, 2, ATT_HEADS // 2, 2, 2 * CHUNK, CHUNK)
    return jnp.transpose(out, (0, 1, 2, 4, 3, 5)).reshape(N_DIL, 2, ATT_HEADS // 2, 2 * CHUNK, 2 * CHUNK)


def _att_specs(order):
    def column(part, ids):
        hp, g, _ = order(*ids)
        return part * 3 * 4 + g * 4 + hp

    def window(part):
        def index(*ids):
            c = order(*ids)[2]
            return pl.multiple_of(jnp.maximum(c - 1, 0) * ATT_ROWS, ATT_ROWS), column(part, ids) * LANES
        return pl.BlockSpec((pl.Element(2 * ATT_ROWS), pl.Element(LANES)), index)

    return [pl.BlockSpec((ATT_ROWS, LANES), lambda *ids: (order(*ids)[2], column(0, ids))), window(1), window(2)]


def _window_base(c):
    return jnp.where(c == 0, 0, ATT_ROWS)


def _rows(start, d):
    if d == 1:
        return pl.ds(pl.multiple_of(start, CHUNK), CHUNK)
    return pl.ds(start, CHUNK, stride=d)


def _att_tile_offsets(t, d):
    n = t // d
    r = t % d
    return n * (CHUNK * d) + r, n


def _head_pair_columns(x_t):
    zeros = jnp.zeros((HEAD_DIM, CHUNK), x_t.dtype)
    return jnp.concatenate([jnp.concatenate([x_t[:HEAD_DIM], zeros], axis=0),
                            jnp.concatenate([zeros, x_t[HEAD_DIM:]], axis=0)], axis=1)


def _head_pair_rows(y):
    return jnp.concatenate([y[:HEAD_DIM, :CHUNK], y[HEAD_DIM:, CHUNK:]], axis=0)


def _att_fwd(name, qkv, bias_tiles):
    S = qkv.shape[0]
    n_chunks = S // ATT_ROWS
    tiles = ATT_ROWS // CHUNK

    def body(q_ref, kk, vv, b_ref, out_ref, lse_ref, o_sc, l_sc):
        c = pl.program_id(1)
        g = pl.program_id(2)
        base = _window_base(c)

        for gi, d in enumerate(DILATIONS):
            @pl.when(g == gi)
            def _(gi=gi, d=d):
                span = CHUNK * d

                def tile(t, carry):
                    q0, n = _att_tile_offsets(t, d)
                    first = (c == 0) & (n == 0)
                    rows = _rows(q0, d)
                    cur = _rows(base + q0, d)
                    prev = _rows(jnp.where(first, q0, base + q0 - span), d)
                    inner = jnp.where(first, 0, 1)
                    qq = _head_pair_columns(_bf(q_ref[rows, :] * ATT_SCALE).T)
                    s_p = _dot(_bf(kk[prev, :]), qq, NN) + b_ref[inner, 0:CHUNK, :]
                    s_c = _dot(_bf(kk[cur, :]), qq, NN) + b_ref[inner, CHUNK:2 * CHUNK, :]
                    m = jnp.maximum(jnp.max(s_p, axis=0, keepdims=True), jnp.max(s_c, axis=0, keepdims=True))
                    p_p = jnp.exp(s_p - m)
                    p_c = jnp.exp(s_c - m)
                    l = jnp.sum(p_p, axis=0, keepdims=True) + jnp.sum(p_c, axis=0, keepdims=True)
                    o2 = (_dot(_bf(vv[prev, :]).T, _bf(p_p), NN)
                          + _dot(_bf(vv[cur, :]).T, _bf(p_c), NN)) * (1.0 / l)
                    lse = m + jnp.log(l)
                    l_t = jnp.concatenate([jnp.broadcast_to(lse[:, :CHUNK], (HEAD_DIM, CHUNK)),
                                           jnp.broadcast_to(lse[:, CHUNK:], (HEAD_DIM, CHUNK))], axis=0)
                    o_sc[gi, rows, :] = _head_pair_rows(o2).T
                    l_sc[gi, rows, :] = l_t.T
                    return carry

                lax.fori_loop(0, tiles, tile, 0, unroll=16)

        @pl.when(g == N_DIL - 1)
        def _():
            for rows in (slice(i * 4 * CHUNK, (i + 1) * 4 * CHUNK) for i in range(ATT_ROWS // (4 * CHUNK))):
                ls = [l_sc[gi, rows, :] for gi in range(N_DIL)]
                mx = functools.reduce(jnp.maximum, ls)
                ws = [jnp.exp(l - mx) for l in ls]
                tot = functools.reduce(lambda a, b: a + b, ws)
                acc = ws[0] * o_sc[0, rows, :]
                for gi in range(1, N_DIL):
                    acc = acc + ws[gi] * o_sc[gi, rows, :]
                out_ref[rows, :] = acc / tot
                lse_ref[rows, :] = mx + jnp.log(tot)

    order = lambda hp, c, g: (hp, g, c)
    out_spec = pl.BlockSpec((ATT_ROWS, LANES), lambda hp, c, g: (c, hp))
    shape = jax.ShapeDtypeStruct((S, ATT_WIDTH), F32)
    return pl.pallas_call(
        body, name=name, grid=(ATT_HEADS // 2, n_chunks, N_DIL),
        in_specs=_att_specs(order) + [
            pl.BlockSpec((None, 2, None, 2 * CHUNK, 2 * CHUNK), lambda hp, c, g: (g, 0, hp, 0, 0))],
        out_specs=[out_spec, out_spec],
        out_shape=[shape, shape],
        scratch_shapes=[pltpu.VMEM((N_DIL, ATT_ROWS, LANES), F32), pltpu.VMEM((N_DIL, ATT_ROWS, LANES), F32)],
        compiler_params=_params(("parallel", "parallel", "arbitrary")),
    )(qkv, qkv, qkv, bias_tiles)


def _att_bwd(name, qkv, o, lse, d_o, bias_tiles):
    S = qkv.shape[0]
    n_chunks = S // ATT_ROWS
    tiles = ATT_ROWS // CHUNK

    def body(q_ref, kk, vv, o_ref, l_ref, do_ref, b_ref, dq_out, dk_out, dv_out, ds_ref, dq_ref, dk_ref, dv_ref):
        g = pl.program_id(1)
        c = pl.program_id(2)

        @pl.when(c == 0)
        def _():
            dk_ref[...] = jnp.zeros_like(dk_ref)
            dv_ref[...] = jnp.zeros_like(dv_ref)
            ds_ref[...] = jnp.zeros_like(ds_ref)

        base = _window_base(c)
        first_row = c * ATT_ROWS
        head0 = lax.broadcasted_iota(jnp.int32, (CHUNK, LANES), 1) < HEAD_DIM

        def head_pair_stack(x):
            zero = jnp.zeros_like(x)
            return jnp.concatenate([jnp.where(head0, x, zero), jnp.where(head0, zero, x)], axis=0)

        for gi, d in enumerate(DILATIONS):
            @pl.when(g == gi)
            def _(d=d):
                span = CHUNK * d

                def tile(t, carry):
                    q0, n = _att_tile_offsets(t, d)
                    first = (c == 0) & (n == 0)
                    rows = _rows(q0, d)
                    cur = _rows(base + q0, d)
                    prev = _rows(jnp.where(first, q0, base + q0 - span), d)
                    inner = jnp.where(first, 0, 1)
                    g_cur = _rows(first_row + q0, d)
                    g_prev = _rows(jnp.where(first, q0, first_row + q0 - span), d)
                    q2 = _bf(q_ref[rows, :] * ATT_SCALE)
                    q_t = q2.T
                    k2 = _bf(jnp.concatenate([kk[prev, :], kk[cur, :]], axis=0))
                    k_t = k2.T
                    v2 = _bf(jnp.concatenate([vv[prev, :], vv[cur, :]], axis=0))
                    do2 = do_ref[rows, :]
                    do_b = _bf(do2)
                    do_t = do_b.T
                    lse_t = l_ref[rows, :].T
                    dd_t = (do2 * o_ref[rows, :]).T
                    lse = jnp.concatenate([lse_t[0:1], lse_t[HEAD_DIM:HEAD_DIM + 1]], axis=1)
                    delta = jnp.concatenate([jnp.sum(dd_t[:HEAD_DIM], axis=0, keepdims=True),
                                             jnp.sum(dd_t[HEAD_DIM:], axis=0, keepdims=True)], axis=1)
                    s = _dot(k2, _head_pair_columns(q_t), NN) + b_ref[inner]
                    p = jnp.exp(s - lse)
                    ds = p * (_dot(v2, _head_pair_columns(do_t), NN) - delta)
                    ds_ref[...] += ds
                    ds_b = _bf(ds)
                    dq_t = _head_pair_rows(_dot(k_t, ds_b, NN))
                    dk2 = _dot(ds_b, head_pair_stack(q2), NN)
                    dv2 = _dot(_bf(p), head_pair_stack(do_b), NN)
                    dq_ref[rows, :] = (dq_t * ATT_SCALE).T
                    dk_ref[g_prev, :] += dk2[0:CHUNK]
                    dk_ref[g_cur, :] += dk2[CHUNK:2 * CHUNK]
                    dv_ref[g_prev, :] += dv2[0:CHUNK]
                    dv_ref[g_cur, :] += dv2[CHUNK:2 * CHUNK]
                    return carry

                lax.fori_loop(0, tiles, tile, 0, unroll=16)

        dq_out[...] = _bf(dq_ref[...])

        @pl.when(c == n_chunks - 1)
        def _():
            dk_out[...] = _bf(dk_ref[...])
            dv_out[...] = _bf(dv_ref[...])

    order = lambda hp, g, c: (hp, g, c)
    chunk = pl.BlockSpec((ATT_ROWS, LANES), lambda hp, g, c: (c, hp))
    slab = pl.BlockSpec((S, LANES), lambda hp, g, c: (0, g * 4 + hp))
    width = N_DIL * ATT_WIDTH
    dq, dk, dv, ds_sums = pl.pallas_call(
        body, name=name, grid=(ATT_HEADS // 2, N_DIL, n_chunks),
        in_specs=_att_specs(order) + [chunk, chunk, chunk,
                                      pl.BlockSpec((None, 2, None, 2 * CHUNK, 2 * CHUNK),
                                                   lambda hp, g, c: (g, 0, hp, 0, 0))],
        out_specs=[pl.BlockSpec((ATT_ROWS, LANES), lambda hp, g, c: (c, g * 4 + hp)), slab, slab,
                   pl.BlockSpec((None, None, 2 * CHUNK, 2 * CHUNK), lambda hp, g, c: (g, hp, 0, 0))],
        out_shape=[jax.ShapeDtypeStruct((S, width), BF16), jax.ShapeDtypeStruct((S, width), BF16),
                   jax.ShapeDtypeStruct((S, width), BF16),
                   jax.ShapeDtypeStruct((N_DIL, ATT_HEADS // 2, 2 * CHUNK, 2 * CHUNK), F32)],
        scratch_shapes=[pltpu.VMEM((ATT_ROWS, LANES), F32), pltpu.VMEM((S, LANES), F32),
                        pltpu.VMEM((S, LANES), F32)],
        compiler_params=_params(("parallel", "parallel", "arbitrary")),
    )(qkv, qkv, qkv, o, lse, d_o, bias_tiles)
    ds_sums = ds_sums.reshape(N_DIL, ATT_HEADS // 2, 2 * CHUNK, 2, CHUNK)
    ds_sums = jnp.transpose(ds_sums, (0, 1, 3, 2, 4)).reshape(N_DIL, ATT_HEADS, 2 * CHUNK, CHUNK)
    return dq, dk, dv, ds_sums


def _bias_grad(name, ds_sums):
    flat = ds_sums.reshape(N_DIL, ATT_HEADS, TILE_ELEMS)

    def body(oh_ref, ds_ref, out_ref):
        for g in range(N_DIL):
            out_ref[g] = lax.dot_general(oh_ref[g], ds_ref[g], (NT, ((), ())), precision=lax.Precision.HIGHEST,
                                         preferred_element_type=F32)

    out = pl.pallas_call(
        body, name=name, out_shape=jax.ShapeDtypeStruct((N_DIL, N_BUCKETS, ATT_HEADS), F32),
        compiler_params=_params(),
    )(_bucket_onehot(), flat)
    return jnp.transpose(out, (1, 0, 2)).reshape(N_BUCKETS, N_DIL * ATT_HEADS)


def _peers():
    x, y, c = lax.axis_index("x"), lax.axis_index("y"), lax.axis_index("c")
    me = 4 * x + 2 * y + c
    others = [(x, y, 1 - c), (1 - x, y, c), (x, 1 - y, c), (1 - x, 1 - y, c),
              (1 - x, y, 1 - c), (x, 1 - y, 1 - c), (1 - x, 1 - y, 1 - c)]
    return me, others


def _slot(dev):
    return 4 * dev[0] + 2 * dev[1] + dev[2]


_HBM =pl.BlockSpec(memory_space=pltpu.HBM)
_SEM = pl.BlockSpec(memory_space=pltpu.SEMAPHORE)
_EFFECT = pltpu.SideEffectType.DATAFLOW_SIDE_EFFECTING


def _exchange_copy(src_ref, land_ref, send_sems, recv_sems, k, dev, me, scatter, arriving):
    src = src_ref.at[me if arriving else _slot(dev)] if scatter else src_ref
    dst = land_ref.at[_slot(dev) if arriving else me]
    return pltpu.make_async_remote_copy(src_ref=src, dst_ref=dst, send_sem=send_sems.at[k], recv_sem=recv_sems.at[k],
                                        device_id=dev, device_id_type=MESH)


def _own_copy(src_ref, land_ref, local_sems, p, me, scatter):
    return pltpu.make_async_copy(src_ref.at[me] if scatter else src_ref, land_ref.at[me], local_sems.at[p])


def _exchange_start(name, srcs, scatter):
    n = len(srcs)
    landings = [lax.empty((N_DEV,) + src.shape[-2:], src.dtype) for src in srcs]

    def body(*refs):
        src_refs, land_refs = refs[:n], refs[n:2 * n]
        send_sems, recv_sems, local_sems = refs[2 * n:2 * n + 3]
        token = refs[-1]
        me, others = _peers()
        for p in range(n):
            _own_copy(src_refs[p], land_refs[p], local_sems, p, me, scatter).start()
            for k, dev in enumerate(others):
                _exchange_copy(src_refs[p], land_refs[p], send_sems, recv_sems, p * (N_DEV - 1) + k, dev, me,
                               scatter, False).start()
        token[...] = jnp.zeros_like(token)

    sems = pltpu.SemaphoreType.DMA((n * (N_DEV - 1),))
    hbm = lambda a: pltpu.with_memory_space_constraint(a, pltpu.HBM)
    outs = pl.pallas_call(
        body, name=name,
        out_shape=(sems, sems, pltpu.SemaphoreType.DMA((n,)), *[pltpu.HBM(a.shape, a.dtype) for a in srcs + landings],
                   jax.ShapeDtypeStruct((8, LANES), F32)),
        in_specs=(_HBM,) * (2 * n),
        out_specs=(_SEM, _SEM, _SEM) + (_HBM,) * (2 * n) + (pl.BlockSpec(memory_space=pltpu.VMEM),),
        input_output_aliases={i: 3 + i for i in range(2 * n)},
        compiler_params=pltpu.CompilerParams(has_side_effects=_EFFECT),
    )(*[hbm(a) for a in srcs + landings])
    return (outs[0], outs[1], outs[2], list(outs[3:3 + n]), list(outs[3 + n:3 + 2 * n]), scatter), outs[-1]


def _exchange_wait(name, handle, after):
    send_sems, recv_sems, local_sems, src_thru, land_thru, scatter = handle
    n = len(src_thru)

    def body(*refs):
        src_refs, land_refs = refs[:n], refs[n:2 * n]
        send_sems, recv_sems, local_sems = refs[2 * n:2 * n + 3]
        me, others = _peers()
        for p in range(n):
            _own_copy(src_refs[p], land_refs[p], local_sems, p, me, scatter).wait()
            for k, dev in enumerate(others):
                cp = _exchange_copy(src_refs[p], land_refs[p], send_sems, recv_sems, p * (N_DEV - 1) + k, dev, me,
                                    scatter, True)
                cp.wait_send()
                cp.wait_recv()

    outs = pl.pallas_call(
        body, name=name,
        out_shape=tuple(pltpu.HBM(a.shape, a.dtype) for a in src_thru + land_thru),
        in_specs=(_HBM,) * (2 * n) + (_SEM, _SEM, _SEM, pl.BlockSpec(memory_space=pl.ANY)),
        out_specs=(_HBM,) * (2 * n),
        input_output_aliases={i: i for i in range(2 * n)},
        compiler_params=pltpu.CompilerParams(has_side_effects=_EFFECT),
    )(*src_thru, *land_thru, send_sems, recv_sems, local_sems, after)
    return list(outs[n:])


def _adamw_math(w, g, m, v):
    m = ADAM_B1 * m + (1.0 - ADAM_B1) * g
    v = ADAM_B2 * v + (1.0 - ADAM_B2) * (g * g)
    m_hat = m / (1.0 - ADAM_B1 ** ADAM_STEP)
    v_hat = v / (1.0 - ADAM_B2 ** ADAM_STEP)
    delta = -ADAM_LR * (m_hat / (jnp.sqrt(v_hat) + ADAM_EPS) + ADAM_WD * w)
    return delta, m, v


_SMALL_WIDE = (("mix_norm_g", 2), ("mlp_norm_g", 2), ("final_norm_g", 1), ("a_ln_g", 1), ("a_ln_b", 1))
_SMALL_NARROW = (("a_w_s", GROUPS * CHUNK), ("a_b_s", GROUPS), ("rel_bias", N_BUCKETS))
_SMALL = tuple(n for n, _ in _SMALL_WIDE + _SMALL_NARROW)
_BIAS_COLS = N_DIL * ATT_HEADS


def _pack_small_grads(grads, loss_term):
    D = grads["a_ln_g"].shape[-1]
    tiles = [jnp.pad(grads[n].reshape(k, D), ((0, 8 - k), (0, 0))) for n, k in _SMALL_WIDE]
    tiles.append(jnp.pad(loss_term.reshape(1, 1), ((0, 7), (0, D - 1))))
    narrow = [grads["a_w_s"].reshape(-1, LANES), grads["a_b_s"].reshape(-1, LANES),
              jnp.pad(grads["rel_bias"], ((0, 0), (0, LANES - _BIAS_COLS)))]
    return jnp.concatenate(tiles, axis=0), jnp.concatenate(narrow, axis=0)


def _adamw_small(name, g_wide, g_narrow, w, m, v):
    D = g_wide.shape[-1]
    shapes = {n: (k, D) for n, k in _SMALL_WIDE}
    shapes.update({n: (k, LANES) for n, k in _SMALL_NARROW})
    shapes["rel_bias"] = (N_BUCKETS, _BIAS_COLS)
    n_t = len(_SMALL)

    def body(gw_ref, gn_ref, *rest):
        params = rest[:3 * n_t]
        outs = rest[3 * n_t:3 * n_t + 4 * n_t]
        loss_ref, sw, sn = rest[-3:]
        sw[...] = functools.reduce(lambda a, b: a + b, [gw_ref[j] for j in range(N_DEV)])
        sn[...] = functools.reduce(lambda a, b: a + b, [gn_ref[j] for j in range(N_DEV)])
        row = 0
        for i, n in enumerate(_SMALL):
            k, cols = shapes[n]
            if i < len(_SMALL_WIDE):
                g = sw[8 * i:8 * i + k, :]
            else:
                g = sn[row:row + k, 0:cols]
                row += k
            w_ref, m_ref, v_ref = params[3 * i:3 * i + 3]
            delta, m_new, v_new = _adamw_math(w_ref[...], g, m_ref[...], v_ref[...])
            for out, val in zip(outs[4 * i:4 * i + 4], (g, delta, m_new, v_new)):
                out[...] = val
        loss_ref[...] = sw[8 * len(_SMALL_WIDE):8 * len(_SMALL_WIDE) + 8, 0:LANES]

    whole = pl.BlockSpec(memory_space=pltpu.VMEM)
    args = [t[n].reshape(shapes[n]) for n in _SMALL for t in (w, m, v)]
    res = pl.pallas_call(
        body, name=name,
        in_specs=[whole] * (2 + len(args)), out_specs=[whole] * (4 * n_t + 1),
        out_shape=[jax.ShapeDtypeStruct(shapes[n], F32) for n in _SMALL for _ in range(4)]
        + [jax.ShapeDtypeStruct((8, LANES), F32)],
        scratch_shapes=[pltpu.VMEM(g_wide.shape[1:], F32), pltpu.VMEM(g_narrow.shape[1:], F32)],
        compiler_params=_params(),
    )(g_wide, g_narrow, *args)
    small = {n: tuple(r.reshape(w[n].shape) for r in res[4 * i:4 * i + 4]) for i, n in enumerate(_SMALL)}
    return small, res[-1][0, 0]


def _adamw_shard(name, parts, w, m, v, layer, earlier=None, after=None, tr=256):
    L, K, N = w.shape
    tr = min(tr, K)
    n_prev = 0 if earlier is None else 4
    after_args, after_specs = _after_operand(after)

    def body(p_ref, w_ref, m_ref, v_ref, *rest):
        g_out, d_out, m_out, v_out = rest[n_prev + len(after_args):]
        g = p_ref[0].astype(F32)
        for j in range(1, N_DEV):
            g = g + p_ref[j].astype(F32)
        delta, m_new, v_new = _adamw_math(w_ref[...], g, m_ref[...], v_ref[...])
        g_out[...] = g
        d_out[...] = delta
        m_out[...] = m_new
        v_out[...] = v_new

    row = pl.BlockSpec((None, tr, N), lambda i: (layer, i, 0))
    shape = jax.ShapeDtypeStruct((L, K, N), F32)
    return pl.pallas_call(
        body, name=name, grid=(K // tr,),
        in_specs=[pl.BlockSpec((N_DEV, tr, N), lambda i: (0, i, 0)), row, row, row]
        + [pl.BlockSpec(memory_space=pl.ANY)] * n_prev + after_specs,
        out_specs=[row, row, row, row],
        out_shape=[shape, shape, shape, shape],
        input_output_aliases={4 + j: j for j in range(n_prev)},
        compiler_params=_params(("parallel",)),
    )(parts, w, m, v, *(earlier or ()), *after_args)


def _column_slots(full):
    K, N = full.shape
    return jnp.transpose(full.reshape(K, N_DEV, N // N_DEV), (1, 0, 2))


def _from_column_slots(slots):
    _, K, n = slots.shape
    return jnp.transpose(slots, (1, 0, 2)).reshape(K, N_DEV * n)


_STAGES = (("gate", ("a_w_in", "a_w_out"), 0),
           ("mlp0", ("w_up", "w_down"), 0),
           ("att", ("b_w_qkv", "b_w_out"), 0),
           ("mlp1", ("w_up", "w_down"), 1))


def kernel(x, mix_norm_g, mlp_norm_g, final_norm_g, a_w_in, a_ln_g, a_ln_b, a_w_s, a_b_s, a_w_out, b_w_qkv, b_w_out, rel_bias, w_up, w_down, loss_target, m_mix_norm_g, m_mlp_norm_g, m_final_norm_g, m_a_w_in, m_a_ln_g, m_a_ln_b, m_a_w_s, m_a_b_s, m_a_w_out, m_b_w_qkv, m_b_w_out, m_rel_bias, m_w_up, m_w_down, v_mix_norm_g, v_mlp_norm_g, v_final_norm_g, v_a_w_in, v_a_ln_g, v_a_ln_b, v_a_w_s, v_a_b_s, v_a_w_out, v_b_w_qkv, v_b_w_out, v_rel_bias, v_w_up, v_w_down):
    w = dict(mix_norm_g=mix_norm_g, mlp_norm_g=mlp_norm_g, final_norm_g=final_norm_g, a_w_in=a_w_in, a_ln_g=a_ln_g,
             a_ln_b=a_ln_b, a_w_s=a_w_s, a_b_s=a_b_s, a_w_out=a_w_out, b_w_qkv=b_w_qkv, b_w_out=b_w_out,
             rel_bias=rel_bias, w_up=w_up, w_down=w_down)
    m = dict(mix_norm_g=m_mix_norm_g, mlp_norm_g=m_mlp_norm_g, final_norm_g=m_final_norm_g, a_w_in=m_a_w_in,
             a_ln_g=m_a_ln_g, a_ln_b=m_a_ln_b, a_w_s=m_a_w_s, a_b_s=m_a_b_s, a_w_out=m_a_w_out, b_w_qkv=m_b_w_qkv,
             b_w_out=m_b_w_out, rel_bias=m_rel_bias, w_up=m_w_up, w_down=m_w_down)
    v = dict(mix_norm_g=v_mix_norm_g, mlp_norm_g=v_mlp_norm_g, final_norm_g=v_final_norm_g, a_w_in=v_a_w_in,
             a_ln_g=v_a_ln_g, a_ln_b=v_a_ln_b, a_w_s=v_a_w_s, a_b_s=v_a_b_s, a_w_out=v_a_w_out, b_w_qkv=v_b_w_qkv,
             b_w_out=v_b_w_out, rel_bias=v_rel_bias, w_up=v_w_up, w_down=v_w_down)

    stages = {s: (names, layer) for s, names, layer in _STAGES}
    order = [s for s, _, _ in _STAGES]

    def shards_of(stage):
        names, layer = stages[stage]
        return [_bf(w[n][layer]) for n in names]

    pending = {}
    pending[order[0]], first_token = _exchange_start("gather_" + order[0] + "_start", shards_of(order[0]), False)

    def get_weights(stage, dep):
        gathered = _exchange_wait("gather_" + stage + "_wait", pending.pop(stage), dep)
        nxt = order.index(stage) + 1
        token = None
        if nxt < len(order):
            shards, gathered = lax.optimization_barrier((shards_of(order[nxt]), gathered))
            pending[order[nxt]], token = _exchange_start("gather_" + order[nxt] + "_start", shards, False)
        return gathered, token

    sent = {}

    def put_grads(stage, slot_grads):
        sent[stage], token = _exchange_start("scatter_" + stage + "_start", slot_grads, True)
        return token

    loss_local, grad_x, small_g = _local_step(
        x[0], loss_target[0], mix_norm_g, mlp_norm_g, final_norm_g, a_ln_g, a_ln_b, a_w_s, a_b_s, rel_bias,
        get_weights, put_grads, first_token)

    small_sent, token = _exchange_start("gather_small_start", list(_pack_small_grads(small_g, loss_local)), False)

    results = {}
    prev = token
    for stage in reversed(order):
        names, layer = stages[stage]
        received = _exchange_wait("scatter_" + stage + "_wait", sent[stage], prev)
        for n, parts in zip(names, received):
            results[n] = _adamw_shard("adamw_%s_%s" % (stage, n), parts, w[n], m[n], v[n], layer, results.get(n),
                                      after=prev)
            prev = results[n][0]

    g_wide, g_narrow = _exchange_wait("gather_small_wait", small_sent, prev)
    small, loss = _adamw_small("adamw_small", g_wide, g_narrow, w, m, v)

    outs = []
    for j in range(4):
        outs.extend(small[n][j] if n in _SMALL else results[n][j] for n in w)
    return (loss, grad_x[None], *outs)


def _local_step(xs, tgt, mix_norm_g, mlp_norm_g, final_norm_g, a_ln_g, a_ln_b, a_w_s, a_b_s, rel_bias,
                get_weights, put_grads, first_token=None):
    D = xs.shape[-1]
    g_mix = [mix_norm_g[l][None, :] for l in range(2)]
    g_mlp = [mlp_norm_g[l][None, :] for l in range(2)]
    g_fin = final_norm_g[None, :]
    ln_g, ln_b = a_ln_g, a_ln_b
    causal = jnp.tril(jnp.ones((CHUNK, CHUNK), dtype=bool))
    wm = _bf(jnp.where(causal[None], a_w_s[0], 0.0))
    bs_full = jnp.repeat(a_b_s[0].T, D // GROUPS, axis=1)
    bias_tiles = _bias_tiles("att_bias", rel_bias, after=first_token)

    (win, wout), token = get_weights("gate", bias_tiles)
    wout = wout.reshape(-1, D)
    y0 = _rms_fwd("rms_mix0", xs, g_mix[0], after=token)
    uvp = _mm_nn("gate_in", y0, win, tm=512, nc=win.shape[2], shards=True)
    z = _gate_fwd("gate_mid", uvp, ln_g, ln_b, wm, bs_full)
    h1, y1 = _mm_nn("gate_out", z, wout, tm=512, nc=512, epi="res", extra=xs, norm_g=g_mlp[0])
    (wup0, wdn0), token = get_weights("mlp0", h1)
    wdn0 = wdn0.reshape(-1, D)
    a0, f0 = _mm_nn("mlp0_up", y1, wup0, tm=512, nc=wup0.shape[2], epi="relu2", shards=True, after=token)
    h2, y2 = _mm_nn("mlp0_down", f0, wdn0, tm=512, nc=512, epi="res", extra=h1, norm_g=g_mix[1])
    (wqkv, wo), token = get_weights("att", h2)
    wqkv, wo = _from_column_slots(wqkv), _from_column_slots(wo)
    qkv = _mm_nn("att_qkv", y2, wqkv, tm=512, nc=512, after=token)
    o_att, lse = _att_fwd("att_fwd", qkv, bias_tiles)
    h3, y3 = _mm_nn("att_out", o_att, wo, tm=512, nc=512, epi="res", extra=h2, norm_g=g_mlp[1])
    (wup1, wdn1), _ = get_weights("mlp1", h3)
    wdn1 = wdn1.reshape(-1, D)
    a1, f1 = _mm_nn("mlp1_up", y3, wup1, tm=512, nc=wup1.shape[2], epi="relu2", shards=True)
    dh, dg_fin, err2, dh_b = _mm_res_loss("mlp1_down_loss", f1, wdn1, h3, g_fin, tgt, tm=512, nc=512)
    loss_local = 0.5 * jnp.sum(err2) / D

    def mlp_bwd(tag, dh, dh_b, h_in, y, a, f, wup_l, wdn_l, g_row, after):
        da = _mm_nt(tag + "_dact", dh_b, wdn_l, tm=512, nc=512, epi="mask2relu", extra=a, after=after)
        g_dn = _mm_tn(tag + "_dwdown", f, dh_b, t1=1024, tn=1024, tm=DW_TOKENS)
        g_up = _mm_tn(tag + "_dwup", y, da, t1=1024, tn=1024, tm=DW_TOKENS, slot_cols=wup_l.shape[2])
        dh_in, dg, dh_in_b = _mm_nt_rms_bwd(tag + "_dy", [(da, wup_l, *_whole(wup_l))], h_in, g_row, dh, tm=512,
                                            nc=512, shards=True)
        return dh_in, dh_in_b, dg, put_grads(tag, [g_up, g_dn.reshape(N_DEV, -1, D)])

    dh3, dh3_b, dg_mlp1, token = mlp_bwd("mlp1", dh, dh_b, h3, y3, a1, f1, wup1, wdn1, g_mlp[1], None)

    d_o = _mm_nt("att_dout", dh3_b, wo, tm=512, nc=512, after=token)
    g_wo = _mm_tn("att_dwo", o_att, dh3_b, t1=512, tn=1024, tm=DW_TOKENS)
    dq, dk, dv, ds_sums = _att_bwd("att_bwd", qkv, o_att, lse, d_o, bias_tiles)
    part_w = N_DIL * ATT_WIDTH
    g_qkv = [_mm_tn("att_dwqkv%d" % p, y2, t, t1=1024, tn=part_w) for p, t in enumerate((dq, dk, dv))]
    dh2, dg_mix1, dh2_b = _mm_nt_rms_bwd(
        "att_dy", [(t, wqkv, (D, part_w), (0, p)) for p, t in enumerate((dq, dk, dv))], h2, g_mix[1], dh3, tm=512,
        nc=512)
    token = put_grads("att", [_column_slots(jnp.concatenate(g_qkv, axis=1)), _column_slots(g_wo)])

    dh1, dh1_b, dg_mlp0, token = mlp_bwd("mlp0", dh2, dh2_b, h1, y1, a0, f0, wup0, wdn0, g_mlp[0], token)

    dz = _mm_nt("gate_dz", dh1_b, wout, tm=512, nc=512, after=token)
    g_wout = _mm_tn("gate_dwout", z, dh1_b, t1=1024, tn=1024, tm=DW_TOKENS)
    duvp, d_wm, d_mixed, d_lng, d_lnb = _gate_bwd("gate_dmid", uvp, dz, ln_g, ln_b, wm, bs_full)
    g_win = _mm_tn("gate_dwin", y0, duvp, t1=1024, tn=1024, tm=DW_TOKENS, slot_cols=win.shape[2])
    token = put_grads("gate", [g_win, g_wout.reshape(N_DEV, -1, D)])
    grad_x, dg_mix0 = _mm_nt_rms_bwd("gate_dy", [(duvp, win, *_whole(win))], xs, g_mix[0], dh1, tm=512, nc=512,
                                     after=token, shards=True, emit_bf16=False)

    small_g = dict(
        mix_norm_g=jnp.concatenate([dg_mix0, dg_mix1], axis=0),
        mlp_norm_g=jnp.concatenate([dg_mlp0, dg_mlp1], axis=0),
        final_norm_g=dg_fin[0], a_ln_g=d_lng, a_ln_b=d_lnb, a_w_s=d_wm[None],
        a_b_s=jnp.sum(d_mixed.reshape(CHUNK, GROUPS, D // GROUPS), axis=2).T[None],
        rel_bias=_bias_grad("att_dbias", ds_sums))
    return loss_local, grad_x, small_g
```
